```python
import jax, jax.numpy as jnp
from jax import lax
import numpy as np

D_MODEL = 2048
BATCH = 16
SEQ = 2048
DEPTH = 2

GRID_W = 64
Q_BLOCK = 128
ROPE_THETA = 10000.0
EPS = 1e-6

GQA_HEADS = 6
GQA_KV_HEADS = 2
GQA_HEAD_DIM = 128
GQA_WIDTH = GQA_HEADS * GQA_HEAD_DIM
GQA_KV_WIDTH = GQA_KV_HEADS * GQA_HEAD_DIM

MLA_HEADS = 4
MLA_Q_LORA = 512
MLA_KV_LORA = 256
MLA_NOPE_DIM = 128
MLA_ROPE_DIM = 64
MLA_V_DIM = 128
MLA_WIDTH = MLA_HEADS * MLA_V_DIM

SSD_HEADS = 12
SSD_HEAD_DIM = 64
SSD_GROUPS = 2
SSD_STATE = 128
SSD_CONV = 5
SSD_CHUNK = 128
SSD_INNER = SSD_HEADS * SSD_HEAD_DIM
SSD_CONV_DIM = SSD_INNER + 2 * SSD_GROUPS * SSD_STATE

MIX_WIDTH = GQA_WIDTH + MLA_WIDTH + SSD_INNER
IN_SPLITS = (GQA_WIDTH, GQA_KV_WIDTH, GQA_KV_WIDTH, MLA_Q_LORA, MLA_KV_LORA, MLA_ROPE_DIM, SSD_INNER, SSD_CONV_DIM, 2 * SSD_HEADS)
IN_COLS = GQA_WIDTH + 2 * GQA_KV_WIDTH + MLA_Q_LORA + MLA_KV_LORA + MLA_ROPE_DIM + SSD_INNER + SSD_CONV_DIM + 2 * SSD_HEADS

FFN_HIDDEN = -(-8 * D_MODEL // (3 * 256)) * 256

kernel_name = 'hybrid_gqa_mla_ssd_encoder_block'


def rms_norm(x, g):
    xf = x.astype(jnp.float32)
    y = xf * lax.rsqrt(jnp.mean(xf * xf, axis=-1, keepdims=True) + EPS)
    return (y * g).astype(x.dtype)


def axial_rope_tables(seq_len, rot_dim, dtype):
    rows = seq_len // GRID_W
    row_idx = jnp.repeat(jnp.arange(rows), GRID_W).astype(jnp.float32)
    col_idx = jnp.tile(jnp.arange(GRID_W), rows).astype(jnp.float32)
    axis_dim = rot_dim // 2
    inv_freq = jnp.power(ROPE_THETA, -jnp.arange(0, axis_dim, 2, dtype=jnp.float32) / axis_dim)
    ang_r = row_idx[:, None] * inv_freq[None, :]
    ang_c = col_idx[:, None] * inv_freq[None, :]
    return (jnp.cos(ang_r).astype(dtype), jnp.sin(ang_r).astype(dtype),
            jnp.cos(ang_c).astype(dtype), jnp.sin(ang_c).astype(dtype))


def rotate(x, cos, sin):
    x1, x2 = jnp.split(x, 2, axis=-1)
    cos = cos[:, None, :]
    sin = sin[:, None, :]
    return jnp.concatenate([x1 * cos - x2 * sin, x1 * sin + x2 * cos], axis=-1)


def apply_axial_rope(x, tables):
    cos_r, sin_r, cos_c, sin_c = tables
    x_row, x_col = jnp.split(x, 2, axis=-1)
    return jnp.concatenate([rotate(x_row, cos_r, sin_r), rotate(x_col, cos_c, sin_c)], axis=-1)


def blocked_attention(q, k, v, scale):
    b, s, h, dk = q.shape
    hkv, dv = k.shape[2], v.shape[-1]
    rep = h // hkv
    nb = s // Q_BLOCK
    qb = q.reshape(b, nb, Q_BLOCK, hkv, rep, dk).transpose(1, 0, 2, 3, 4, 5)

    def attend(q_blk):
        logits = jnp.einsum('bqgrd,bsgd->bgrqs', q_blk, k).astype(jnp.float32) * scale
        probs = jax.nn.softmax(logits, axis=-1).astype(v.dtype)
        return jnp.einsum('bgrqs,bsgd->bqgrd', probs, v)

    out = lax.map(attend, qb)
    return out.transpose(1, 0, 2, 3, 4, 5).reshape(b, s, h * dv)


def segsum(a):
    t = a.shape[-1]
    a_rep = jnp.broadcast_to(a[..., :, None], a.shape + (t,))
    strict_lower = jnp.tril(jnp.ones((t, t), dtype=bool), k=-1)
    seg = jnp.cumsum(jnp.where(strict_lower, a_rep, 0.0), axis=-2)
    lower = jnp.tril(jnp.ones((t, t), dtype=bool))
    return jnp.where(lower, seg, -jnp.inf)


def ssd_chunked(x, dt, a_neg, bm, cm):
    b, s, h, p = x.shape
    g, n = bm.shape[2], bm.shape[3]
    e = h // g
    nc = s // SSD_CHUNK
    f32 = jnp.float32
    xd = (x.astype(f32) * dt[..., None]).reshape(b, nc, SSD_CHUNK, g, e, p)
    a = (dt * a_neg).reshape(b, nc, SSD_CHUNK, g, e).transpose(0, 3, 4, 1, 2)
    bc = bm.astype(f32).reshape(b, nc, SSD_CHUNK, g, n)
    cc = cm.astype(f32).reshape(b, nc, SSD_CHUNK, g, n)
    a_cs = jnp.cumsum(a, axis=-1)
    cb = jnp.einsum('bclgn,bcsgn->bgcls', cc, bc)
    w_diag = cb[:, :, None] * jnp.exp(segsum(a))
    y_diag = jnp.einsum('bgecls,bcsgep->bclgep', w_diag, xd)
    to_end = jnp.exp(a_cs[..., -1:] - a_cs).transpose(0, 3, 4, 1, 2)
    states = jnp.einsum('bclgn,bclgep->bcgepn', bc, xd * to_end[..., None])
    states = jnp.concatenate([jnp.zeros_like(states[:, :1]), states], axis=1)
    chunk_a = jnp.pad(a_cs[..., -1], ((0, 0), (0, 0), (0, 0), (1, 0)))
    chunk_decay = jnp.exp(segsum(chunk_a))
    states = jnp.einsum('bgezc,bcgepn->bzgepn', chunk_decay, states)[:, :-1]
    from_start = jnp.exp(a_cs).transpose(0, 3, 4, 1, 2)
    y_off = jnp.einsum('bclgn,bcgepn->bclgep', cc, states) * from_start[..., None]
    return (y_diag + y_off).reshape(b, s, h, p)


def depthwise_centred_conv(x, w):
    pad = SSD_CONV // 2
    return lax.conv_general_dilated(x, w[:, None, :], window_strides=(1,), padding=[(pad, pad)],
                                    dimension_numbers=('NWC', 'WIO', 'NWC'), feature_group_count=x.shape[-1])


def gqa_group(q, k, v, q_norm_g, k_norm_g, rope):
    b, s = q.shape[:2]
    q = q.reshape(b, s, GQA_HEADS, GQA_HEAD_DIM)
    k = k.reshape(b, s, GQA_KV_HEADS, GQA_HEAD_DIM)
    v = v.reshape(b, s, GQA_KV_HEADS, GQA_HEAD_DIM)
    q = apply_axial_rope(rms_norm(q, q_norm_g), rope)
    k = apply_axial_rope(rms_norm(k, k_norm_g), rope)
    return blocked_attention(q, k, v, GQA_HEAD_DIM ** -0.5)


def mla_group(c_q, c_kv, k_pe, q_norm_g, w_uq, kv_norm_g, w_ukv, rope):
    b, s = c_q.shape[:2]
    q = (rms_norm(c_q, q_norm_g) @ w_uq).reshape(b, s, MLA_HEADS, MLA_NOPE_DIM + MLA_ROPE_DIM)
    q_nope, q_pe = q[..., :MLA_NOPE_DIM], q[..., MLA_NOPE_DIM:]
    kv = (rms_norm(c_kv, kv_norm_g) @ w_ukv).reshape(b, s, MLA_HEADS, MLA_NOPE_DIM + MLA_V_DIM)
    k_nope, v = kv[..., :MLA_NOPE_DIM], kv[..., MLA_NOPE_DIM:]
    q_pe = apply_axial_rope(q_pe, rope)
    k_pe = apply_axial_rope(k_pe[:, :, None, :], rope)
    q = jnp.concatenate([q_nope, q_pe], axis=-1)
    k = jnp.concatenate([k_nope, jnp.broadcast_to(k_pe, (b, s, MLA_HEADS, MLA_ROPE_DIM))], axis=-1)
    return blocked_attention(q, k, v, (MLA_NOPE_DIM + MLA_ROPE_DIM) ** -0.5)


def ssd_group(z, xbc, dt_raw, conv_w, conv_b, dt_bias, a_log, d_skip, norm_g):
    b, s = z.shape[:2]
    xbc = jax.nn.silu(depthwise_centred_conv(xbc, conv_w) + conv_b)
    xs, bm, cm = jnp.split(xbc, [SSD_INNER, SSD_INNER + SSD_GROUPS * SSD_STATE], axis=-1)
    xs = xs.reshape(b, s, SSD_HEADS, SSD_HEAD_DIM)
    bm = bm.reshape(b, s, SSD_GROUPS, SSD_STATE)
    cm = cm.reshape(b, s, SSD_GROUPS, SSD_STATE)
    dt = jax.nn.softplus(dt_raw.reshape(b, s, 2, SSD_HEADS).astype(jnp.float32) + dt_bias)
    a_neg = -jnp.exp(a_log.astype(jnp.float32))
    y_fwd = ssd_chunked(xs, dt[:, :, 0], a_neg[0], bm, cm)
    flip = lambda t: jnp.flip(t, axis=1)
    y_bwd = flip(ssd_chunked(flip(xs), flip(dt[:, :, 1]), a_neg[1], flip(bm), flip(cm)))
    y = y_fwd + y_bwd + xs * d_skip[:, None]
    y = y.reshape(b, s, SSD_INNER) * jax.nn.silu(z)
    y = rms_norm(y.reshape(b, s, SSD_GROUPS, SSD_INNER // SSD_GROUPS), norm_g.reshape(SSD_GROUPS, -1))
    return y.reshape(b, s, SSD_INNER)


def hybrid_mixer(h, w_in, q_norm_g, k_norm_g, mla_q_norm_g, w_uq, mla_kv_norm_g, w_ukv,
                 conv_w, conv_b, dt_bias, a_log, d_skip, ssd_norm_g, w_out, rope_a, rope_b):
    proj = h @ w_in
    idx = np.cumsum(IN_SPLITS)[:-1].tolist()
    q_a, k_a, v_a, cq_b, ckv_b, kpe_b, z_c, xbc_c, dt_c = jnp.split(proj, idx, axis=-1)
    o_a = gqa_group(q_a, k_a, v_a, q_norm_g, k_norm_g, rope_a)
    o_b = mla_group(cq_b, ckv_b, kpe_b, mla_q_norm_g, w_uq, mla_kv_norm_g, w_ukv, rope_b)
    o_c = ssd_group(z_c, xbc_c, dt_c, conv_w, conv_b, dt_bias, a_log, d_skip, ssd_norm_g)
    o = jnp.concatenate([o_a.astype(h.dtype), o_b.astype(h.dtype), o_c.astype(h.dtype)], axis=-1)
    return o @ w_out


def swiglu_ffn(h, w_gate_up, w_down):
    gate, up = jnp.split(h @ w_gate_up, 2, axis=-1)
    return (jax.nn.silu(gate) * up) @ w_down


def _fwd_setup_inputs(seed: int = 0) -> dict:
    key = jax.random.key(seed)
    ks = iter(jax.random.split(key, 32))
    f32 = jnp.float32
    D, L = D_MODEL, DEPTH

    def nrm(shape, std):
        return std * jax.random.normal(next(ks), shape, f32)

    def gain(shape):
        return 1.0 + nrm(shape, 0.02)

    dt_init = jnp.exp(jax.random.uniform(next(ks), (L, 2, SSD_HEADS), f32, np.log(1e-3), np.log(1e-1)))
    dt_bias = dt_init + jnp.log(-jnp.expm1(-dt_init))
    a_log = jnp.log(jax.random.uniform(next(ks), (L, 2, SSD_HEADS), f32, 1.0, 16.0))
    return {
        'x': nrm((BATCH, SEQ, D), 1.0),
        'c': nrm((BATCH, D), 1.0),
        'w_ada': nrm((L, D, 6 * D), 0.5 * D ** -0.5),
        'b_ada': nrm((L, 6 * D), 0.01),
        'norm1_g': gain((L, D)),
        'norm2_g': gain((L, D)),
        'w_in': nrm((L, D, IN_COLS), D ** -0.5),
        'q_norm_g': gain((L, GQA_HEAD_DIM)),
        'k_norm_g': gain((L, GQA_HEAD_DIM)),
        'mla_q_norm_g': gain((L, MLA_Q_LORA)),
        'w_uq': nrm((L, MLA_Q_LORA, MLA_HEADS * (MLA_NOPE_DIM + MLA_ROPE_DIM)), MLA_Q_LORA ** -0.5),
        'mla_kv_norm_g': gain((L, MLA_KV_LORA)),
        'w_ukv': nrm((L, MLA_KV_LORA, MLA_HEADS * (MLA_NOPE_DIM + MLA_V_DIM)), MLA_KV_LORA ** -0.5),
        'conv_w': nrm((L, SSD_CONV, SSD_CONV_DIM), SSD_CONV ** -0.5),
        'conv_b': nrm((L, SSD_CONV_DIM), 0.01),
        'dt_bias': dt_bias,
        'a_log': a_log,
        'd_skip': gain((L, SSD_HEADS)),
        'ssd_norm_g': gain((L, SSD_INNER)),
        'w_out': nrm((L, MIX_WIDTH, D), MIX_WIDTH ** -0.5),
        'w_gate_up': nrm((L, D, 2 * FFN_HIDDEN), D ** -0.5),
        'w_down': nrm((L, FFN_HIDDEN, D), FFN_HIDDEN ** -0.5),
        'final_norm_g': gain((D,)),
    }


def _fwd_reference(x, c, w_ada, b_ada, norm1_g, norm2_g, w_in, q_norm_g, k_norm_g, mla_q_norm_g, w_uq,
              mla_kv_norm_g, w_ukv, conv_w, conv_b, dt_bias, a_log, d_skip, ssd_norm_g, w_out,
              w_gate_up, w_down, final_norm_g):
    seq_len = x.shape[1]
    rope_a = axial_rope_tables(seq_len, GQA_HEAD_DIM, x.dtype)
    rope_b = axial_rope_tables(seq_len, MLA_ROPE_DIM, x.dtype)
    c_act = jax.nn.silu(c)
    for l in range(DEPTH):
        mod = c_act @ w_ada[l] + b_ada[l]
        shift1, scale1, gate1, shift2, scale2, gate2 = [m[:, None, :] for m in jnp.split(mod, 6, axis=-1)]
        h = rms_norm(x, norm1_g[l]) * (1 + scale1) + shift1
        mix = hybrid_mixer(h, w_in[l], q_norm_g[l], k_norm_g[l], mla_q_norm_g[l], w_uq[l], mla_kv_norm_g[l],
                           w_ukv[l], conv_w[l], conv_b[l], dt_bias[l], a_log[l], d_skip[l], ssd_norm_g[l],
                           w_out[l], rope_a, rope_b)
        x = x + gate1 * mix
        h = rms_norm(x, norm2_g[l]) * (1 + scale2) + shift2
        x = x + gate2 * swiglu_ffn(h, w_gate_up[l], w_down[l])
    return rms_norm(x, final_norm_g)


import jax as _jax
import jax.numpy as _jnp

TWIN_FORMAT = 'train_step'
FWD_PARAMS = ['x', 'c', 'w_ada', 'b_ada', 'norm1_g', 'norm2_g', 'w_in', 'q_norm_g', 'k_norm_g', 'mla_q_norm_g', 'w_uq', 'mla_kv_norm_g', 'w_ukv', 'conv_w', 'conv_b', 'dt_bias', 'a_log', 'd_skip', 'ssd_norm_g', 'w_out', 'w_gate_up', 'w_down', 'final_norm_g']
TWIN_WEIGHTS = ['w_ada', 'b_ada', 'norm1_g', 'norm2_g', 'w_in', 'q_norm_g', 'k_norm_g', 'mla_q_norm_g', 'w_uq', 'mla_kv_norm_g', 'w_ukv', 'conv_w', 'conv_b', 'dt_bias', 'a_log', 'd_skip', 'ssd_norm_g', 'w_out', 'w_gate_up', 'w_down', 'final_norm_g']
TWIN_DIFF_INPUT = 'x'
TWIN_INPUTS = ['x', 'c', 'w_ada', 'b_ada', 'norm1_g', 'norm2_g', 'w_in', 'q_norm_g', 'k_norm_g', 'mla_q_norm_g', 'w_uq', 'mla_kv_norm_g', 'w_ukv', 'conv_w', 'conv_b', 'dt_bias', 'a_log', 'd_skip', 'ssd_norm_g', 'w_out', 'w_gate_up', 'w_down', 'final_norm_g', 'loss_target', 'm_w_ada', 'm_b_ada', 'm_norm1_g', 'm_norm2_g', 'm_w_in', 'm_q_norm_g', 'm_k_norm_g', 'm_mla_q_norm_g', 'm_w_uq', 'm_mla_kv_norm_g', 'm_w_ukv', 'm_conv_w', 'm_conv_b', 'm_dt_bias', 'm_a_log', 'm_d_skip', 'm_ssd_norm_g', 'm_w_out', 'm_w_gate_up', 'm_w_down', 'm_final_norm_g', 'v_w_ada', 'v_b_ada', 'v_norm1_g', 'v_norm2_g', 'v_w_in', 'v_q_norm_g', 'v_k_norm_g', 'v_mla_q_norm_g', 'v_w_uq', 'v_mla_kv_norm_g', 'v_w_ukv', 'v_conv_w', 'v_conv_b', 'v_dt_bias', 'v_a_log', 'v_d_skip', 'v_ssd_norm_g', 'v_w_out', 'v_w_gate_up', 'v_w_down', 'v_final_norm_g']
TWIN_OUTPUTS = ['loss', 'grad_x', 'grad_w_ada', 'grad_b_ada', 'grad_norm1_g', 'grad_norm2_g', 'grad_w_in', 'grad_q_norm_g', 'grad_k_norm_g', 'grad_mla_q_norm_g', 'grad_w_uq', 'grad_mla_kv_norm_g', 'grad_w_ukv', 'grad_conv_w', 'grad_conv_b', 'grad_dt_bias', 'grad_a_log', 'grad_d_skip', 'grad_ssd_norm_g', 'grad_w_out', 'grad_w_gate_up', 'grad_w_down', 'grad_final_norm_g', 'delta_w_ada', 'delta_b_ada', 'delta_norm1_g', 'delta_norm2_g', 'delta_w_in', 'delta_q_norm_g', 'delta_k_norm_g', 'delta_mla_q_norm_g', 'delta_w_uq', 'delta_mla_kv_norm_g', 'delta_w_ukv', 'delta_conv_w', 'delta_conv_b', 'delta_dt_bias', 'delta_a_log', 'delta_d_skip', 'delta_ssd_norm_g', 'delta_w_out', 'delta_w_gate_up', 'delta_w_down', 'delta_final_norm_g', 'new_m_w_ada', 'new_m_b_ada', 'new_m_norm1_g', 'new_m_norm2_g', 'new_m_w_in', 'new_m_q_norm_g', 'new_m_k_norm_g', 'new_m_mla_q_norm_g', 'new_m_w_uq', 'new_m_mla_kv_norm_g', 'new_m_w_ukv', 'new_m_conv_w', 'new_m_conv_b', 'new_m_dt_bias', 'new_m_a_log', 'new_m_d_skip', 'new_m_ssd_norm_g', 'new_m_w_out', 'new_m_w_gate_up', 'new_m_w_down', 'new_m_final_norm_g', 'new_v_w_ada', 'new_v_b_ada', 'new_v_norm1_g', 'new_v_norm2_g', 'new_v_w_in', 'new_v_q_norm_g', 'new_v_k_norm_g', 'new_v_mla_q_norm_g', 'new_v_w_uq', 'new_v_mla_kv_norm_g', 'new_v_w_ukv', 'new_v_conv_w', 'new_v_conv_b', 'new_v_dt_bias', 'new_v_a_log', 'new_v_d_skip', 'new_v_ssd_norm_g', 'new_v_w_out', 'new_v_w_gate_up', 'new_v_w_down', 'new_v_final_norm_g']
TWIN_LEAF_KINDS = {'loss': 'loss', 'grad_x': 'grad_x', 'grad_w_ada': 'grad_w', 'grad_b_ada': 'grad_w', 'grad_norm1_g': 'grad_w', 'grad_norm2_g': 'grad_w', 'grad_w_in': 'grad_w', 'grad_q_norm_g': 'grad_w', 'grad_k_norm_g': 'grad_w', 'grad_mla_q_norm_g': 'grad_w', 'grad_w_uq': 'grad_w', 'grad_mla_kv_norm_g': 'grad_w', 'grad_w_ukv': 'grad_w', 'grad_conv_w': 'grad_w', 'grad_conv_b': 'grad_w', 'grad_dt_bias': 'grad_w', 'grad_a_log': 'grad_w', 'grad_d_skip': 'grad_w', 'grad_ssd_norm_g': 'grad_w', 'grad_w_out': 'grad_w', 'grad_w_gate_up': 'grad_w', 'grad_w_down': 'grad_w', 'grad_final_norm_g': 'grad_w', 'delta_w_ada': 'delta_w', 'delta_b_ada': 'delta_w', 'delta_norm1_g': 'delta_w', 'delta_norm2_g': 'delta_w', 'delta_w_in': 'delta_w', 'delta_q_norm_g': 'delta_w', 'delta_k_norm_g': 'delta_w', 'delta_mla_q_norm_g': 'delta_w', 'delta_w_uq': 'delta_w', 'delta_mla_kv_norm_g': 'delta_w', 'delta_w_ukv': 'delta_w', 'delta_conv_w': 'delta_w', 'delta_conv_b': 'delta_w', 'delta_dt_bias': 'delta_w', 'delta_a_log': 'delta_w', 'delta_d_skip': 'delta_w', 'delta_ssd_norm_g': 'delta_w', 'delta_w_out': 'delta_w', 'delta_w_gate_up': 'delta_w', 'delta_w_down': 'delta_w', 'delta_final_norm_g': 'delta_w', 'new_m_w_ada': 'new_m', 'new_m_b_ada': 'new_m', 'new_m_norm1_g': 'new_m', 'new_m_norm2_g': 'new_m', 'new_m_w_in': 'new_m', 'new_m_q_norm_g': 'new_m', 'new_m_k_norm_g': 'new_m', 'new_m_mla_q_norm_g': 'new_m', 'new_m_w_uq': 'new_m', 'new_m_mla_kv_norm_g': 'new_m', 'new_m_w_ukv': 'new_m', 'new_m_conv_w': 'new_m', 'new_m_conv_b': 'new_m', 'new_m_dt_bias': 'new_m', 'new_m_a_log': 'new_m', 'new_m_d_skip': 'new_m', 'new_m_ssd_norm_g': 'new_m', 'new_m_w_out': 'new_m', 'new_m_w_gate_up': 'new_m', 'new_m_w_down': 'new_m', 'new_m_final_norm_g': 'new_m', 'new_v_w_ada': 'new_v', 'new_v_b_ada': 'new_v', 'new_v_norm1_g': 'new_v', 'new_v_norm2_g': 'new_v', 'new_v_w_in': 'new_v', 'new_v_q_norm_g': 'new_v', 'new_v_k_norm_g': 'new_v', 'new_v_mla_q_norm_g': 'new_v', 'new_v_w_uq': 'new_v', 'new_v_mla_kv_norm_g': 'new_v', 'new_v_w_ukv': 'new_v', 'new_v_conv_w': 'new_v', 'new_v_conv_b': 'new_v', 'new_v_dt_bias': 'new_v', 'new_v_a_log': 'new_v', 'new_v_d_skip': 'new_v', 'new_v_ssd_norm_g': 'new_v', 'new_v_w_out': 'new_v', 'new_v_w_gate_up': 'new_v', 'new_v_w_down': 'new_v', 'new_v_final_norm_g': 'new_v'}


def _forward(args):
    return _fwd_reference(*[args[k] for k in FWD_PARAMS])


def _output_shape():
    out = _jax.eval_shape(lambda: _forward(_fwd_setup_inputs(0)))
    return out.shape, out.dtype

N_MICROBATCH = 1
ADAM_LR = 0.001
ADAM_B1 = 0.9
ADAM_B2 = 0.999
ADAM_EPS = 1e-08
ADAM_WD = 0.01
ADAM_STEP = 10
PER_EXAMPLE_BATCH_AXIS = {'x': 0, 'c': 0, 'loss_target': 0}
SHARED_INPUTS = []
_WEIGHT_DTYPES = {'w_ada': _jnp.float32, 'b_ada': _jnp.float32, 'norm1_g': _jnp.float32, 'norm2_g': _jnp.float32, 'w_in': _jnp.float32, 'q_norm_g': _jnp.float32, 'k_norm_g': _jnp.float32, 'mla_q_norm_g': _jnp.float32, 'w_uq': _jnp.float32, 'mla_kv_norm_g': _jnp.float32, 'w_ukv': _jnp.float32, 'conv_w': _jnp.float32, 'conv_b': _jnp.float32, 'dt_bias': _jnp.float32, 'a_log': _jnp.float32, 'd_skip': _jnp.float32, 'ssd_norm_g': _jnp.float32, 'w_out': _jnp.float32, 'w_gate_up': _jnp.float32, 'w_down': _jnp.float32, 'final_norm_g': _jnp.float32}
MOMENT_SCALE = {'w_ada': 2.507053e-02, 'b_ada': 4.240943e-02, 'norm1_g': 2.255146e-02, 'norm2_g': 2.498299e-02, 'w_in': 1.767615e-02, 'q_norm_g': 6.310263e-03, 'k_norm_g': 6.234270e-03, 'mla_q_norm_g': 2.818031e-03, 'w_uq': 2.286155e-03, 'mla_kv_norm_g': 1.301367e-02, 'w_ukv': 5.866110e-03, 'conv_w': 2.128445e-02, 'conv_b': 3.104218e-02, 'dt_bias': 5.261734e-02, 'a_log': 8.964179e-02, 'd_skip': 8.148994e-02, 'ssd_norm_g': 2.745132e-02, 'w_out': 1.744663e-02, 'w_gate_up': 1.096245e-02, 'w_down': 1.787889e-02, 'final_norm_g': 1.597771e+01}


def _to_microbatches(a, axis):
    t = _jnp.moveaxis(a, axis, 0)
    t = t.reshape((N_MICROBATCH, t.shape[0] // N_MICROBATCH) + t.shape[1:])
    return _jnp.moveaxis(t, 1, axis + 1)


def setup_inputs(seed: int = 0) -> dict:
    inp = _fwd_setup_inputs(seed)
    key = _jax.random.fold_in(_jax.random.key(seed), 7919)
    shape, _ = _output_shape()
    out = dict(inp)
    out["loss_target"] = _jax.random.normal(_jax.random.fold_in(key, 0), shape, _jnp.float32)
    for i, name in enumerate(TWIN_WEIGHTS):
        w = inp[name].astype(_jnp.float32)
        if MOMENT_SCALE is None:
            s = _jnp.sqrt(_jnp.mean(_jnp.square(w)) + 1e-30)
        else:
            s = MOMENT_SCALE[name]
        km, kv = _jax.random.split(_jax.random.fold_in(key, i + 1))
        out[name] = w
        out["m_" + name] = s * _jax.random.normal(km, w.shape, _jnp.float32)
        out["v_" + name] = (s * s) * _jax.random.uniform(kv, w.shape, _jnp.float32, 0.5, 1.5)
    if N_MICROBATCH > 1:
        for name, axis in PER_EXAMPLE_BATCH_AXIS.items():
            out[name] = _to_microbatches(out[name], axis)
    return {'x': out['x'], 'c': out['c'], 'w_ada': out['w_ada'], 'b_ada': out['b_ada'], 'norm1_g': out['norm1_g'], 'norm2_g': out['norm2_g'], 'w_in': out['w_in'], 'q_norm_g': out['q_norm_g'], 'k_norm_g': out['k_norm_g'], 'mla_q_norm_g': out['mla_q_norm_g'], 'w_uq': out['w_uq'], 'mla_kv_norm_g': out['mla_kv_norm_g'], 'w_ukv': out['w_ukv'], 'conv_w': out['conv_w'], 'conv_b': out['conv_b'], 'dt_bias': out['dt_bias'], 'a_log': out['a_log'], 'd_skip': out['d_skip'], 'ssd_norm_g': out['ssd_norm_g'], 'w_out': out['w_out'], 'w_gate_up': out['w_gate_up'], 'w_down': out['w_down'], 'final_norm_g': out['final_norm_g'], 'loss_target': out['loss_target'], 'm_w_ada': out['m_w_ada'], 'm_b_ada': out['m_b_ada'], 'm_norm1_g': out['m_norm1_g'], 'm_norm2_g': out['m_norm2_g'], 'm_w_in': out['m_w_in'], 'm_q_norm_g': out['m_q_norm_g'], 'm_k_norm_g': out['m_k_norm_g'], 'm_mla_q_norm_g': out['m_mla_q_norm_g'], 'm_w_uq': out['m_w_uq'], 'm_mla_kv_norm_g': out['m_mla_kv_norm_g'], 'm_w_ukv': out['m_w_ukv'], 'm_conv_w': out['m_conv_w'], 'm_conv_b': out['m_conv_b'], 'm_dt_bias': out['m_dt_bias'], 'm_a_log': out['m_a_log'], 'm_d_skip': out['m_d_skip'], 'm_ssd_norm_g': out['m_ssd_norm_g'], 'm_w_out': out['m_w_out'], 'm_w_gate_up': out['m_w_gate_up'], 'm_w_down': out['m_w_down'], 'm_final_norm_g': out['m_final_norm_g'], 'v_w_ada': out['v_w_ada'], 'v_b_ada': out['v_b_ada'], 'v_norm1_g': out['v_norm1_g'], 'v_norm2_g': out['v_norm2_g'], 'v_w_in': out['v_w_in'], 'v_q_norm_g': out['v_q_norm_g'], 'v_k_norm_g': out['v_k_norm_g'], 'v_mla_q_norm_g': out['v_mla_q_norm_g'], 'v_w_uq': out['v_w_uq'], 'v_mla_kv_norm_g': out['v_mla_kv_norm_g'], 'v_w_ukv': out['v_w_ukv'], 'v_conv_w': out['v_conv_w'], 'v_conv_b': out['v_conv_b'], 'v_dt_bias': out['v_dt_bias'], 'v_a_log': out['v_a_log'], 'v_d_skip': out['v_d_skip'], 'v_ssd_norm_g': out['v_ssd_norm_g'], 'v_w_out': out['v_w_out'], 'v_w_gate_up': out['v_w_gate_up'], 'v_w_down': out['v_w_down'], 'v_final_norm_g': out['v_final_norm_g']}


def _loss(weights, diff, rest, loss_target):
    with _jax.named_scope("forward"):
        args = {**rest, TWIN_DIFF_INPUT: diff, **{k: w.astype(_WEIGHT_DTYPES[k]) for k, w in weights.items()}}
        y = _forward(args)
    with _jax.named_scope("loss_head"):
        err = _jnp.square(y.astype(_jnp.float32) - loss_target)
        return 0.5 * _jnp.sum(_jnp.mean(err, axis=-1)) if err.ndim else 0.5 * err


def _adamw(w, g, m, v):
    m = ADAM_B1 * m + (1.0 - ADAM_B1) * g
    v = ADAM_B2 * v + (1.0 - ADAM_B2) * _jnp.square(g)
    m_hat = m / (1.0 - ADAM_B1 ** ADAM_STEP)
    v_hat = v / (1.0 - ADAM_B2 ** ADAM_STEP)
    delta = -ADAM_LR * (m_hat / (_jnp.sqrt(v_hat) + ADAM_EPS) + ADAM_WD * w)
    return delta, m, v


def reference(x, c, w_ada, b_ada, norm1_g, norm2_g, w_in, q_norm_g, k_norm_g, mla_q_norm_g, w_uq, mla_kv_norm_g, w_ukv, conv_w, conv_b, dt_bias, a_log, d_skip, ssd_norm_g, w_out, w_gate_up, w_down, final_norm_g, loss_target, m_w_ada, m_b_ada, m_norm1_g, m_norm2_g, m_w_in, m_q_norm_g, m_k_norm_g, m_mla_q_norm_g, m_w_uq, m_mla_kv_norm_g, m_w_ukv, m_conv_w, m_conv_b, m_dt_bias, m_a_log, m_d_skip, m_ssd_norm_g, m_w_out, m_w_gate_up, m_w_down, m_final_norm_g, v_w_ada, v_b_ada, v_norm1_g, v_norm2_g, v_w_in, v_q_norm_g, v_k_norm_g, v_mla_q_norm_g, v_w_uq, v_mla_kv_norm_g, v_w_ukv, v_conv_w, v_conv_b, v_dt_bias, v_a_log, v_d_skip, v_ssd_norm_g, v_w_out, v_w_gate_up, v_w_down, v_final_norm_g):
    given = dict(x=x, c=c, w_ada=w_ada, b_ada=b_ada, norm1_g=norm1_g, norm2_g=norm2_g, w_in=w_in, q_norm_g=q_norm_g, k_norm_g=k_norm_g, mla_q_norm_g=mla_q_norm_g, w_uq=w_uq, mla_kv_norm_g=mla_kv_norm_g, w_ukv=w_ukv, conv_w=conv_w, conv_b=conv_b, dt_bias=dt_bias, a_log=a_log, d_skip=d_skip, ssd_norm_g=ssd_norm_g, w_out=w_out, w_gate_up=w_gate_up, w_down=w_down, final_norm_g=final_norm_g, loss_target=loss_target, m_w_ada=m_w_ada, m_b_ada=m_b_ada, m_norm1_g=m_norm1_g, m_norm2_g=m_norm2_g, m_w_in=m_w_in, m_q_norm_g=m_q_norm_g, m_k_norm_g=m_k_norm_g, m_mla_q_norm_g=m_mla_q_norm_g, m_w_uq=m_w_uq, m_mla_kv_norm_g=m_mla_kv_norm_g, m_w_ukv=m_w_ukv, m_conv_w=m_conv_w, m_conv_b=m_conv_b, m_dt_bias=m_dt_bias, m_a_log=m_a_log, m_d_skip=m_d_skip, m_ssd_norm_g=m_ssd_norm_g, m_w_out=m_w_out, m_w_gate_up=m_w_gate_up, m_w_down=m_w_down, m_final_norm_g=m_final_norm_g, v_w_ada=v_w_ada, v_b_ada=v_b_ada, v_norm1_g=v_norm1_g, v_norm2_g=v_norm2_g, v_w_in=v_w_in, v_q_norm_g=v_q_norm_g, v_k_norm_g=v_k_norm_g, v_mla_q_norm_g=v_mla_q_norm_g, v_w_uq=v_w_uq, v_mla_kv_norm_g=v_mla_kv_norm_g, v_w_ukv=v_w_ukv, v_conv_w=v_conv_w, v_conv_b=v_conv_b, v_dt_bias=v_dt_bias, v_a_log=v_a_log, v_d_skip=v_d_skip, v_ssd_norm_g=v_ssd_norm_g, v_w_out=v_w_out, v_w_gate_up=v_w_gate_up, v_w_down=v_w_down, v_final_norm_g=v_final_norm_g)
    weights = {n: given[n] for n in TWIN_WEIGHTS}
    shared = {n: given[n] for n in SHARED_INPUTS}
    per_example = {n: given[n] for n in ['x', 'c']}
    grad_fn = _jax.value_and_grad(_loss, argnums=(0, 1))

    def one_microbatch(ex, loss_target):
        ex = dict(ex)
        diff = ex.pop(TWIN_DIFF_INPUT)
        return grad_fn(weights, diff, {**shared, **ex}, loss_target)

    if N_MICROBATCH == 1:
        loss, (grad_w, grad_x) = one_microbatch(per_example, given["loss_target"])
    else:
        def body(carry, xs):
            loss_sum, grad_sum = carry
            l_k, (gw_k, gx_k) = one_microbatch(xs[0], xs[1])
            with _jax.named_scope("update"):
                return (loss_sum + l_k, _jax.tree.map(_jnp.add, grad_sum, gw_k)), gx_k

        init = (_jnp.zeros((), _jnp.float32), _jax.tree.map(_jnp.zeros_like, weights))
        (loss, grad_w), grad_x = _jax.lax.scan(body, init, (per_example, given["loss_target"]))
    with _jax.named_scope("update"):
        delta_w, new_m, new_v = {}, {}, {}
        for n in TWIN_WEIGHTS:
            delta_w[n], new_m[n], new_v[n] = _adamw(weights[n], grad_w[n], given["m_" + n], given["v_" + n])
    return (loss, grad_x, *[grad_w[n] for n in TWIN_WEIGHTS], *[delta_w[n] for n in TWIN_WEIGHTS],
            *[new_m[n] for n in TWIN_WEIGHTS], *[new_v[n] for n in TWIN_WEIGHTS])
```

```python
import functools

import numpy as np
import jax
import jax.numpy as jnp
from jax import lax
from jax.experimental import pallas as pl
from jax.experimental.pallas import tpu as pltpu

F32 = jnp.float32
BF16 = jnp.bfloat16
HIGHEST = lax.Precision.HIGHEST
MESH = pl.DeviceIdType.MESH

GRID_W = 64
ROPE_THETA = 10000.0
EPS = 1e-6

GQA_HEADS, GQA_KV_HEADS, GQA_HEAD_DIM = 6, 2, 128
GQA_WIDTH = GQA_HEADS * GQA_HEAD_DIM
GQA_KV_WIDTH = GQA_KV_HEADS * GQA_HEAD_DIM
MLA_HEADS, MLA_Q_LORA, MLA_KV_LORA = 4, 512, 256
MLA_NOPE_DIM, MLA_ROPE_DIM, MLA_V_DIM = 128, 64, 128
SSD_HEADS, SSD_HEAD_DIM, SSD_GROUPS, SSD_STATE, SSD_CONV, SSD_CHUNK = 12, 64, 2, 128, 5, 128
SSD_INNER = SSD_HEADS * SSD_HEAD_DIM
SSD_CONV_DIM = SSD_INNER + 2 * SSD_GROUPS * SSD_STATE
SSD_GROUP_HEADS = SSD_HEADS // SSD_GROUPS
SSD_GROUP_WIDTH = SSD_GROUP_HEADS * SSD_HEAD_DIM
IN_SPLITS = (GQA_WIDTH, GQA_KV_WIDTH, GQA_KV_WIDTH, MLA_Q_LORA, MLA_KV_LORA, MLA_ROPE_DIM, SSD_INNER, SSD_CONV_DIM,
             2 * SSD_HEADS)
IN_COLS = sum(IN_SPLITS)
LANES = 128
IN_COLS_PAD = -(-IN_COLS // LANES) * LANES

ADAM_LR, ADAM_B1, ADAM_B2, ADAM_EPS, ADAM_WD, ADAM_STEP = 0.001, 0.9, 0.999, 1e-08, 0.01, 10

N_CHIPS = 4
N_DEV = 8
TILE_BYTES = 2 * 1024 * 1024


def _pick(n, cands):
    for t in cands:
        if n % t == 0:
            return t
    return n


def _row_tile(rows, row_bytes):
    for t in (2048, 1024, 512, 256, 128, 64, 32, 16, 8):
        if rows % t == 0 and t * row_bytes <= TILE_BYTES:
            return t
    return rows


def _matmul(a, b, ta=False, tb=False, name="mm"):
    if ta:
        kdim, m = a.shape
    else:
        m, kdim = a.shape
    if tb:
        n, k2 = b.shape
    else:
        k2, n = b.shape
    assert kdim == k2, (a.shape, b.shape, ta, tb)
    tm = _pick(m, (512, 256, 128))
    tn = _pick(n, (1408, 1024, 768, 512, 384, 256, 128))
    tk = _pick(kdim, (1024, 1408, 512, 256, 128))
    nk = kdim // tk
    dn = (((0 if ta else 1,), (1 if tb else 0,)), ((), ()))

    def body(a_ref, b_ref, o_ref):
        part = lax.dot_general(a_ref[...].astype(BF16), b_ref[...].astype(BF16), dn, preferred_element_type=F32)
        if nk == 1:
            o_ref[...] = part
        else:
            k = pl.program_id(2)

            @pl.when(k == 0)
            def _():
                o_ref[...] = part

            @pl.when(k > 0)
            def _():
                o_ref[...] += part

    a_spec = pl.BlockSpec((tk, tm), lambda i, j, k: (k, i)) if ta else pl.BlockSpec((tm, tk), lambda i, j, k: (i, k))
    b_spec = pl.BlockSpec((tn, tk), lambda i, j, k: (j, k)) if tb else pl.BlockSpec((tk, tn), lambda i, j, k: (k, j))
    return pl.pallas_call(
        body, name=name, grid=(m // tm, n // tn, nk),
        in_specs=[a_spec, b_spec], out_specs=pl.BlockSpec((tm, tn), lambda i, j, k: (i, j)),
        out_shape=jax.ShapeDtypeStruct((m, n), F32),
        compiler_params=pltpu.CompilerParams(dimension_semantics=("parallel", "parallel", "arbitrary")),
    )(a, b)


@jax.custom_vjp
def linear(x, w):
    return _matmul(x, w, name="linear_fwd")


def _linear_fwd(x, w):
    return _matmul(x, w, name="linear_fwd"), (x, w)


def _linear_bwd(res, dy):
    x, w = res
    return _matmul(dy, w, tb=True, name="linear_dx"), _matmul(x, dy, ta=True, name="linear_dw")


linear.defvjp(_linear_fwd, _linear_bwd)


def _rms_fwd_call(x, g, groups):
    rows, cols = x.shape
    d = cols // groups
    tr = _row_tile(rows, cols * 4)

    def body(x_ref, g_ref, y_ref):
        for gi in range(groups):
            sl = slice(gi * d, (gi + 1) * d)
            xs = x_ref[:, sl]
            r = lax.rsqrt(jnp.mean(xs * xs, axis=-1, keepdims=True) + EPS)
            y_ref[:, sl] = xs * r * g_ref[:, sl]

    return pl.pallas_call(
        body, name="rms_fwd", grid=(rows // tr,),
        in_specs=[pl.BlockSpec((tr, cols), lambda i: (i, 0)), pl.BlockSpec((1, cols), lambda i: (0, 0))],
        out_specs=pl.BlockSpec((tr, cols), lambda i: (i, 0)),
        out_shape=jax.ShapeDtypeStruct((rows, cols), F32),
        compiler_params=pltpu.CompilerParams(dimension_semantics=("parallel",)),
    )(x, g)


def _rms_bwd_call(x, g, dy, groups):
    rows, cols = x.shape
    d = cols // groups
    tr = _row_tile(rows, cols * 4)

    def body(x_ref, g_ref, dy_ref, dx_ref, dg_ref):
        @pl.when(pl.program_id(0) == 0)
        def _():
            dg_ref[...] = jnp.zeros_like(dg_ref)

        for gi in range(groups):
            sl = slice(gi * d, (gi + 1) * d)
            xs = x_ref[:, sl]
            dys = dy_ref[:, sl]
            r = lax.rsqrt(jnp.mean(xs * xs, axis=-1, keepdims=True) + EPS)
            xhat = xs * r
            dg_ref[:, sl] += jnp.sum(dys * xhat, axis=0, keepdims=True)
            dxhat = dys * g_ref[:, sl]
            dx_ref[:, sl] = r * (dxhat - xhat * jnp.mean(dxhat * xhat, axis=-1, keepdims=True))

    return pl.pallas_call(
        body, name="rms_bwd", grid=(rows // tr,),
        in_specs=[pl.BlockSpec((tr, cols), lambda i: (i, 0)), pl.BlockSpec((1, cols), lambda i: (0, 0)),
                  pl.BlockSpec((tr, cols), lambda i: (i, 0))],
        out_specs=[pl.BlockSpec((tr, cols), lambda i: (i, 0)), pl.BlockSpec((1, cols), lambda i: (0, 0))],
        out_shape=[jax.ShapeDtypeStruct((rows, cols), F32), jax.ShapeDtypeStruct((1, cols), F32)],
        compiler_params=pltpu.CompilerParams(dimension_semantics=("arbitrary",)),
    )(x, g, dy)


@functools.partial(jax.custom_vjp, nondiff_argnums=(2,))
def rms_norm(x, g, groups):
    return _rms_fwd_call(x, g, groups)


def _rms_norm_fwd(x, g, groups):
    return _rms_fwd_call(x, g, groups), (x, g)


def _rms_norm_bwd(groups, res, dy):
    x, g = res
    dx, dg = _rms_bwd_call(x, g, dy, groups)
    return dx, dg


rms_norm.defvjp(_rms_norm_fwd, _rms_norm_bwd)


NT_DIMS = (((1,), (1,)), ((), ()))
TN_DIMS = (((0,), (0,)), ((), ()))


def _softmax_rows(q, k, scale):
    s = lax.dot_general(q, k, NT_DIMS, preferred_element_type=F32) * scale
    p = jnp.exp(s - jnp.max(s, axis=-1, keepdims=True))
    return p / jnp.sum(p, axis=-1, keepdims=True)


def _attn_fwd_call(q, k, v, scale):
    b, h, s, dk = q.shape
    hkv, dv = k.shape[1], v.shape[3]
    rep = h // hkv
    tq = _pick(s, (256, 128))

    def body(q_ref, k_ref, v_ref, o_ref):
        p = _softmax_rows(q_ref[0, 0].astype(BF16), k_ref[0, 0].astype(BF16), scale)
        o_ref[0, 0] = jnp.dot(p.astype(BF16), v_ref[0, 0].astype(BF16), preferred_element_type=F32)

    return pl.pallas_call(
        body, name="attn_fwd", grid=(b, h, s // tq),
        in_specs=[pl.BlockSpec((1, 1, tq, dk), lambda bi, hi, qi: (bi, hi, qi, 0)),
                  pl.BlockSpec((1, 1, s, dk), lambda bi, hi, qi: (bi, hi // rep, 0, 0)),
                  pl.BlockSpec((1, 1, s, dv), lambda bi, hi, qi: (bi, hi // rep, 0, 0))],
        out_specs=pl.BlockSpec((1, 1, tq, dv), lambda bi, hi, qi: (bi, hi, qi, 0)),
        out_shape=jax.ShapeDtypeStruct((b, h, s, dv), F32),
        compiler_params=pltpu.CompilerParams(dimension_semantics=("parallel", "parallel", "parallel")),
    )(q, k, v)


def _attn_bwd_call(q, k, v, do, scale):
    b, h, s, dk = q.shape
    hkv, dv = k.shape[1], v.shape[3]
    rep = h // hkv
    tq = _pick(s, (256, 128))

    def body(q_ref, k_ref, v_ref, do_ref, dq_ref, dk_ref, dv_ref):
        @pl.when((pl.program_id(2) == 0) & (pl.program_id(3) == 0))
        def _():
            dk_ref[...] = jnp.zeros_like(dk_ref)
            dv_ref[...] = jnp.zeros_like(dv_ref)

        qb = q_ref[0, 0].astype(BF16)
        kb = k_ref[0, 0].astype(BF16)
        vb = v_ref[0, 0].astype(BF16)
        dob = do_ref[0, 0].astype(BF16)
        p = _softmax_rows(qb, kb, scale)
        dp = lax.dot_general(dob, vb, NT_DIMS, preferred_element_type=F32)
        ds = (p * (dp - jnp.sum(p * dp, axis=-1, keepdims=True)) * scale).astype(BF16)
        dq_ref[0, 0] = jnp.dot(ds, kb, preferred_element_type=F32)
        dk_ref[0, 0] += lax.dot_general(ds, qb, TN_DIMS, preferred_element_type=F32)
        dv_ref[0, 0] += lax.dot_general(p.astype(BF16), dob, TN_DIMS, preferred_element_type=F32)

    return pl.pallas_call(
        body, name="attn_bwd", grid=(b, hkv, rep, s // tq),
        in_specs=[pl.BlockSpec((1, 1, tq, dk), lambda bi, gi, ri, qi: (bi, gi * rep + ri, qi, 0)),
                  pl.BlockSpec((1, 1, s, dk), lambda bi, gi, ri, qi: (bi, gi, 0, 0)),
                  pl.BlockSpec((1, 1, s, dv), lambda bi, gi, ri, qi: (bi, gi, 0, 0)),
                  pl.BlockSpec((1, 1, tq, dv), lambda bi, gi, ri, qi: (bi, gi * rep + ri, qi, 0))],
        out_specs=[pl.BlockSpec((1, 1, tq, dk), lambda bi, gi, ri, qi: (bi, gi * rep + ri, qi, 0)),
                   pl.BlockSpec((1, 1, s, dk), lambda bi, gi, ri, qi: (bi, gi, 0, 0)),
                   pl.BlockSpec((1, 1, s, dv), lambda bi, gi, ri, qi: (bi, gi, 0, 0))],
        out_shape=[jax.ShapeDtypeStruct(q.shape, F32), jax.ShapeDtypeStruct(k.shape, F32),
                   jax.ShapeDtypeStruct(v.shape, F32)],
        compiler_params=pltpu.CompilerParams(
            dimension_semantics=("parallel", "parallel", "arbitrary", "arbitrary")),
    )(q, k, v, do)


@functools.partial(jax.custom_vjp, nondiff_argnums=(3,))
def attention(q, k, v, scale):
    return _attn_fwd_call(q, k, v, scale)


def _attention_fwd(q, k, v, scale):
    return _attn_fwd_call(q, k, v, scale), (q, k, v)


def _attention_bwd(scale, res, do):
    q, k, v = res
    return tuple(_attn_bwd_call(q, k, v, do, scale))


attention.defvjp(_attention_fwd, _attention_bwd)


CONV_COL_TILE = 256
CONV_PACK_ROWS = 8


def _shifted(x, off, rows):
    if off == 0:
        return x
    s = x.shape[0]
    rolled = pltpu.roll(x, (-off) % s, 0)
    valid = (rows + off >= 0) & (rows + off < s)
    return jnp.where(valid, rolled, 0.0)


def _conv_pre(x, wb_ref, rows):
    z = jnp.zeros_like(x) + wb_ref[SSD_CONV:SSD_CONV + 1, :]
    for j in range(SSD_CONV):
        z = z + wb_ref[j:j + 1, :] * _shifted(x, j - SSD_CONV // 2, rows)
    return z


def _conv_fwd_call(x, wb):
    b, s, c = x.shape
    tc = _pick(c, (CONV_COL_TILE, LANES))

    def body(x_ref, wb_ref, y_ref):
        xv = x_ref[0]
        rows = lax.broadcasted_iota(jnp.int32, xv.shape, 0)
        z = _conv_pre(xv, wb_ref, rows)
        y_ref[0] = z * jax.nn.sigmoid(z)

    return pl.pallas_call(
        body, name="conv_fwd", grid=(b, c // tc),
        in_specs=[pl.BlockSpec((1, s, tc), lambda bi, ci: (bi, 0, ci)),
                  pl.BlockSpec((CONV_PACK_ROWS, tc), lambda bi, ci: (0, ci))],
        out_specs=pl.BlockSpec((1, s, tc), lambda bi, ci: (bi, 0, ci)),
        out_shape=jax.ShapeDtypeStruct(x.shape, F32),
        compiler_params=pltpu.CompilerParams(dimension_semantics=("parallel", "parallel")),
    )(x, wb)


def _conv_bwd_call(x, wb, dy):
    b, s, c = x.shape
    tc = _pick(c, (CONV_COL_TILE, LANES))

    def body(x_ref, wb_ref, dy_ref, dx_ref, dwb_ref):
        xv = x_ref[0]
        rows = lax.broadcasted_iota(jnp.int32, xv.shape, 0)
        z = _conv_pre(xv, wb_ref, rows)
        sg = jax.nn.sigmoid(z)
        dz = dy_ref[0] * (sg * (1.0 + z * (1.0 - sg)))
        dx = jnp.zeros_like(xv)
        for j in range(SSD_CONV):
            off = j - SSD_CONV // 2
            dx = dx + wb_ref[j:j + 1, :] * _shifted(dz, -off, rows)
            dwb_ref[0, j:j + 1, :] = jnp.sum(dz * _shifted(xv, off, rows), axis=0, keepdims=True)
        dx_ref[0] = dx
        dwb_ref[0, SSD_CONV:SSD_CONV + 1, :] = jnp.sum(dz, axis=0, keepdims=True)
        dwb_ref[0, SSD_CONV + 1:, :] = jnp.zeros((CONV_PACK_ROWS - SSD_CONV - 1, dz.shape[1]), F32)

    return pl.pallas_call(
        body, name="conv_bwd", grid=(b, c // tc),
        in_specs=[pl.BlockSpec((1, s, tc), lambda bi, ci: (bi, 0, ci)),
                  pl.BlockSpec((CONV_PACK_ROWS, tc), lambda bi, ci: (0, ci)),
                  pl.BlockSpec((1, s, tc), lambda bi, ci: (bi, 0, ci))],
        out_specs=[pl.BlockSpec((1, s, tc), lambda bi, ci: (bi, 0, ci)),
                   pl.BlockSpec((1, CONV_PACK_ROWS, tc), lambda bi, ci: (bi, 0, ci))],
        out_shape=[jax.ShapeDtypeStruct(x.shape, F32), jax.ShapeDtypeStruct((b, CONV_PACK_ROWS, c), F32)],
        compiler_params=pltpu.CompilerParams(dimension_semantics=("parallel", "parallel")),
    )(x, wb, dy)


@jax.custom_vjp
def conv_silu(x, wb):
    return _conv_fwd_call(x, wb)


def _conv_silu_fwd(x, wb):
    return _conv_fwd_call(x, wb), (x, wb)


def _conv_silu_bwd(res, dy):
    x, wb = res
    dx, dwb = _conv_bwd_call(x, wb, dy)
    return dx, jnp.sum(dwb, axis=0)


conv_silu.defvjp(_conv_silu_fwd, _conv_silu_bwd)


SSD_PAIRS = SSD_GROUP_HEADS // 2
NEG_INF = -1e30


def _ssd_common(x_ref, dtx_ref, dtt_ref, anx_ref, anc_ref, b_ref, c_ref):
    L = SSD_CHUNK
    xv = x_ref[0]
    dt = dtx_ref[0]
    ri = lax.broadcasted_iota(jnp.int32, (L, L), 0)
    ci = lax.broadcasted_iota(jnp.int32, (L, L), 1)
    causal = ri >= ci
    tri = causal.astype(F32)
    a_cs = jnp.dot(tri, dt * anx_ref[...], precision=HIGHEST, preferred_element_type=F32)
    a_row = dtt_ref[0, 0] * anc_ref[0]
    acs_row = lax.dot_general(a_row, tri, NT_DIMS, precision=HIGHEST, preferred_element_type=F32)
    xd = xv * dt
    bmat = b_ref[0].astype(BF16)
    cmat = c_ref[0].astype(BF16)
    gmat = lax.dot_general(cmat, bmat, NT_DIMS, preferred_element_type=F32)
    return xv, dt, causal, tri, a_cs, acs_row, xd, bmat, cmat, gmat


def _ssd_lambda(a_cs, acs_row, causal, h):
    col = a_cs[:, h * SSD_HEAD_DIM:h * SSD_HEAD_DIM + 1]
    row = acs_row[h:h + 1, :]
    return jnp.exp(jnp.where(causal, col - row, NEG_INF))


def _ssd_fwd_call(x, dtx, dtt, anx, anc, bm, cm):
    b, s, _ = x.shape
    L, N, GW = SSD_CHUNK, SSD_STATE, SSD_GROUP_WIDTH
    nc = s // L

    def body(x_ref, dtx_ref, dtt_ref, anx_ref, anc_ref, b_ref, c_ref, y_ref, hs_ref, state):
        @pl.when(pl.program_id(2) == 0)
        def _():
            state[...] = jnp.zeros_like(state)

        xv, dt, causal, tri, a_cs, acs_row, xd, bmat, cmat, gmat = _ssd_common(
            x_ref, dtx_ref, dtt_ref, anx_ref, anc_ref, b_ref, c_ref)
        hin = state[...]
        hs_ref[0, 0, 0] = hin
        y_off = jnp.dot(cmat, hin.astype(BF16), preferred_element_type=F32) * jnp.exp(a_cs)
        a_end = a_cs[L - 1:L, :]
        s_new = lax.dot_general(bmat, (xd * jnp.exp(a_end - a_cs)).astype(BF16), TN_DIMS, preferred_element_type=F32)
        state[...] = jnp.exp(a_end) * hin + s_new
        lane = lax.broadcasted_iota(jnp.int32, (L, LANES), 1)
        for pr in range(SSD_PAIRS):
            sl = slice(pr * LANES, (pr + 1) * LANES)
            xdp = xd[:, sl].astype(BF16)
            w0 = (gmat * _ssd_lambda(a_cs, acs_row, causal, 2 * pr)).astype(BF16)
            w1 = (gmat * _ssd_lambda(a_cs, acs_row, causal, 2 * pr + 1)).astype(BF16)
            y0 = jnp.dot(w0, xdp, preferred_element_type=F32)
            y1 = jnp.dot(w1, xdp, preferred_element_type=F32)
            y_ref[0, :, sl] = jnp.where(lane < SSD_HEAD_DIM, y0, y1) + y_off[:, sl]

    G = SSD_GROUPS
    return pl.pallas_call(
        body, name="ssd_fwd", grid=(b, G, nc),
        in_specs=[pl.BlockSpec((1, L, GW), lambda bi, gi, c: (bi, c, gi)),
                  pl.BlockSpec((1, L, GW), lambda bi, gi, c: (bi, c, gi)),
                  pl.BlockSpec((1, 1, SSD_GROUP_HEADS, L), lambda bi, gi, c: (bi, gi, 0, c)),
                  pl.BlockSpec((1, GW), lambda bi, gi, c: (0, gi)),
                  pl.BlockSpec((1, SSD_GROUP_HEADS, 1), lambda bi, gi, c: (gi, 0, 0)),
                  pl.BlockSpec((1, L, N), lambda bi, gi, c: (bi, c, gi)),
                  pl.BlockSpec((1, L, N), lambda bi, gi, c: (bi, c, gi))],
        out_specs=[pl.BlockSpec((1, L, GW), lambda bi, gi, c: (bi, c, gi)),
                   pl.BlockSpec((1, 1, 1, N, GW), lambda bi, gi, c: (bi, gi, c, 0, 0))],
        out_shape=[jax.ShapeDtypeStruct(x.shape, F32), jax.ShapeDtypeStruct((b, G, nc, N, GW), F32)],
        scratch_shapes=[pltpu.VMEM((N, GW), F32)],
        compiler_params=pltpu.CompilerParams(dimension_semantics=("parallel", "parallel", "arbitrary")),
    )(x, dtx, dtt, anx, anc, bm, cm)


def _ssd_bwd_call(x, dtx, dtt, anx, anc, bm, cm, hs, dy):
    b, s, _ = x.shape
    L, N, GW = SSD_CHUNK, SSD_STATE, SSD_GROUP_WIDTH
    nc = s // L

    def body(x_ref, dtx_ref, dtt_ref, anx_ref, anc_ref, b_ref, c_ref, hs_ref, dy_ref,
             dx_ref, ddt_ref, dan_ref, db_ref, dc_ref, dstate):
        @pl.when(pl.program_id(2) == 0)
        def _():
            dstate[...] = jnp.zeros_like(dstate)

        xv, dt, causal, tri, a_cs, acs_row, xd, bmat, cmat, gmat = _ssd_common(
            x_ref, dtx_ref, dtt_ref, anx_ref, anc_ref, b_ref, c_ref)
        hin = hs_ref[0, 0, 0]
        hinb = hin.astype(BF16)
        dyv = dy_ref[0]
        ds_out = dstate[...]
        dsb = ds_out.astype(BF16)
        eacs = jnp.exp(a_cs)
        a_end = a_cs[L - 1:L, :]
        e_end = jnp.exp(a_end)
        dec = jnp.exp(a_end - a_cs)
        dye = dyv * eacs
        dyeb = dye.astype(BF16)
        xdec = xd * dec
        ch = jnp.dot(cmat, hinb, preferred_element_type=F32)
        bds = jnp.dot(bmat, dsb, preferred_element_type=F32)
        t_state = xdec * bds
        d_aend = jnp.sum(t_state, axis=0, keepdims=True) + e_end * jnp.sum(ds_out * hin, axis=0, keepdims=True)
        dacs = dye * ch - t_state
        dxd_state = bds * dec
        dstate[...] = e_end * ds_out + lax.dot_general(cmat, dyeb, TN_DIMS, preferred_element_type=F32)

        lane = lax.broadcasted_iota(jnp.int32, (L, LANES), 1)
        ones = jnp.full((L, LANES), 1.0 / SSD_HEAD_DIM, F32)
        dg = jnp.zeros((L, L), F32)
        dxd_parts, dacs_parts = [], []
        for pr in range(SSD_PAIRS):
            sl = slice(pr * LANES, (pr + 1) * LANES)
            xdp = xd[:, sl]
            dyp = dyv[:, sl]
            dxd_p = jnp.zeros((L, LANES), F32)
            dacs_p = jnp.zeros((L, LANES), F32)
            for half in range(2):
                mine = (lane < SSD_HEAD_DIM) if half == 0 else (lane >= SSD_HEAD_DIM)
                lam = _ssd_lambda(a_cs, acs_row, causal, 2 * pr + half)
                w = gmat * lam
                xdh = jnp.where(mine, xdp, 0.0).astype(BF16)
                dyh = jnp.where(mine, dyp, 0.0).astype(BF16)
                dw = lax.dot_general(dyh, xdh, NT_DIMS, preferred_element_type=F32)
                dg = dg + dw * lam
                mm = dw * w
                rs = jnp.dot(mm, ones, precision=HIGHEST, preferred_element_type=F32)
                cs = lax.dot_general(mm, ones, TN_DIMS, precision=HIGHEST, preferred_element_type=F32)
                dacs_p = dacs_p + jnp.where(mine, rs - cs, 0.0)
                wtdy = lax.dot_general(w.astype(BF16), dyh, TN_DIMS, preferred_element_type=F32)
                dxd_p = dxd_p + wtdy
            dxd_parts.append(dxd_p)
            dacs_parts.append(dacs_p)
        dxd = jnp.concatenate(dxd_parts, axis=1) + dxd_state
        dacs = dacs + jnp.concatenate(dacs_parts, axis=1)
        last = lax.broadcasted_iota(jnp.int32, dacs.shape, 0) == L - 1
        dacs = dacs + jnp.where(last, d_aend, 0.0)
        da = lax.dot_general(tri, dacs, TN_DIMS, precision=HIGHEST, preferred_element_type=F32)
        dgb = dg.astype(BF16)
        dc_ref[0] = (jnp.dot(dgb, bmat, preferred_element_type=F32)
                     + lax.dot_general(dyeb, hinb, NT_DIMS, preferred_element_type=F32))
        db_ref[0] = (lax.dot_general(dgb, cmat, TN_DIMS, preferred_element_type=F32)
                     + lax.dot_general(xdec.astype(BF16), dsb, NT_DIMS, preferred_element_type=F32))
        dx_ref[0] = dxd * dt
        ddt_ref[0] = da * anx_ref[...] + dxd * xv
        dan_ref[0, 0, 0] = jnp.sum(da * dt, axis=0, keepdims=True)

    G = SSD_GROUPS
    rev = lambda bi, gi, c: (bi, nc - 1 - c, gi)
    return pl.pallas_call(
        body, name="ssd_bwd", grid=(b, G, nc),
        in_specs=[pl.BlockSpec((1, L, GW), rev),
                  pl.BlockSpec((1, L, GW), rev),
                  pl.BlockSpec((1, 1, SSD_GROUP_HEADS, L), lambda bi, gi, c: (bi, gi, 0, nc - 1 - c)),
                  pl.BlockSpec((1, GW), lambda bi, gi, c: (0, gi)),
                  pl.BlockSpec((1, SSD_GROUP_HEADS, 1), lambda bi, gi, c: (gi, 0, 0)),
                  pl.BlockSpec((1, L, N), rev),
                  pl.BlockSpec((1, L, N), rev),
                  pl.BlockSpec((1, 1, 1, N, GW), lambda bi, gi, c: (bi, gi, nc - 1 - c, 0, 0)),
                  pl.BlockSpec((1, L, GW), rev)],
        out_specs=[pl.BlockSpec((1, L, GW), rev),
                   pl.BlockSpec((1, L, GW), rev),
                   pl.BlockSpec((1, 1, 1, 1, GW), lambda bi, gi, c: (bi, gi, nc - 1 - c, 0, 0)),
                   pl.BlockSpec((1, L, N), rev),
                   pl.BlockSpec((1, L, N), rev)],
        out_shape=[jax.ShapeDtypeStruct(x.shape, F32), jax.ShapeDtypeStruct(x.shape, F32),
                   jax.ShapeDtypeStruct((b, G, nc, 1, GW), F32),
                   jax.ShapeDtypeStruct(bm.shape, F32), jax.ShapeDtypeStruct(cm.shape, F32)],
        scratch_shapes=[pltpu.VMEM((N, GW), F32)],
        compiler_params=pltpu.CompilerParams(dimension_semantics=("parallel", "parallel", "arbitrary")),
    )(x, dtx, dtt, anx, anc, bm, cm, hs, dy)


@jax.custom_vjp
def _ssd_scan(x, dtx, dtt, anx, anc, bm, cm):
    return _ssd_fwd_call(x, dtx, dtt, anx, anc, bm, cm)[0]


def _ssd_scan_fwd(x, dtx, dtt, anx, anc, bm, cm):
    y, hs = _ssd_fwd_call(x, dtx, dtt, anx, anc, bm, cm)
    return y, (x, dtx, dtt, anx, anc, bm, cm, hs)


def _ssd_scan_bwd(res, dy):
    x, dtx, dtt, anx, anc, bm, cm, hs = res
    dx, ddtx, dan, db, dc = _ssd_bwd_call(x, dtx, dtt, anx, anc, bm, cm, hs, dy)
    b, g, nc, _, gw = dan.shape
    danx = jnp.sum(dan, axis=(0, 2, 3)).reshape(1, g * gw)
    return dx, ddtx, jnp.zeros_like(dtt), danx, jnp.zeros_like(anc), db, dc


_ssd_scan.defvjp(_ssd_scan_fwd, _ssd_scan_bwd)


def ssd_chunked(xs, dt, a_neg, bm, cm):
    b, s, _ = xs.shape
    dtx = jnp.repeat(dt, SSD_HEAD_DIM, axis=-1)
    dtt = jnp.transpose(dt, (0, 2, 1)).reshape(b, SSD_GROUPS, SSD_GROUP_HEADS, s)
    anx = jnp.repeat(a_neg, SSD_HEAD_DIM)[None, :]
    anc = a_neg.reshape(SSD_GROUPS, SSD_GROUP_HEADS, 1)
    return _ssd_scan(xs, dtx, dtt, anx, anc, bm, cm)


def _loss_call(y, t):
    rows, cols = y.shape
    tr = _row_tile(rows, cols * 4)

    def body(y_ref, t_ref, loss_ref, diff_ref):
        @pl.when(pl.program_id(0) == 0)
        def _():
            loss_ref[...] = jnp.zeros_like(loss_ref)

        d = y_ref[...] - t_ref[...]
        diff_ref[...] = d * (1.0 / cols)
        part = jnp.sum(jnp.sum(d * d, axis=1, keepdims=True), axis=0, keepdims=True)
        loss_ref[...] += part * (0.5 / cols)

    return pl.pallas_call(
        body, name="loss_head", grid=(rows // tr,),
        in_specs=[pl.BlockSpec((tr, cols), lambda i: (i, 0)), pl.BlockSpec((tr, cols), lambda i: (i, 0))],
        out_specs=[pl.BlockSpec((1, 1), lambda i: (0, 0)), pl.BlockSpec((tr, cols), lambda i: (i, 0))],
        out_shape=[jax.ShapeDtypeStruct((1, 1), F32), jax.ShapeDtypeStruct((rows, cols), F32)],
        compiler_params=pltpu.CompilerParams(dimension_semantics=("arbitrary",)),
    )(y, t)


@jax.custom_vjp
def loss_head(y, t):
    return _loss_call(y, t)[0][0, 0]


def _loss_head_fwd(y, t):
    loss, diff = _loss_call(y, t)
    return loss[0, 0], diff


def _loss_head_bwd(diff, g):
    return g * diff, jnp.zeros_like(diff)


loss_head.defvjp(_loss_head_fwd, _loss_head_bwd)


def _axial_rope_tables(seq_len, rot_dim):
    rows = seq_len // GRID_W
    row_idx = jnp.repeat(jnp.arange(rows), GRID_W).astype(F32)
    col_idx = jnp.tile(jnp.arange(GRID_W), rows).astype(F32)
    axis_dim = rot_dim // 2
    inv_freq = jnp.power(ROPE_THETA, -jnp.arange(0, axis_dim, 2, dtype=F32) / axis_dim)
    ang_r = row_idx[:, None] * inv_freq[None, :]
    ang_c = col_idx[:, None] * inv_freq[None, :]
    return jnp.cos(ang_r), jnp.sin(ang_r), jnp.cos(ang_c), jnp.sin(ang_c)


def _rotate(x, cos, sin):
    x1, x2 = jnp.split(x, 2, axis=-1)
    cos = cos[:, None, :]
    sin = sin[:, None, :]
    return jnp.concatenate([x1 * cos - x2 * sin, x1 * sin + x2 * cos], axis=-1)


def _apply_axial_rope(x, tables):
    cos_r, sin_r, cos_c, sin_c = tables
    x_row, x_col = jnp.split(x, 2, axis=-1)
    return jnp.concatenate([_rotate(x_row, cos_r, sin_r), _rotate(x_col, cos_c, sin_c)], axis=-1)


def _heads_first(t):
    return jnp.transpose(t, (0, 2, 1, 3))


def _gqa_group(q, k, v, q_norm_g, k_norm_g, rope, b, s):
    q = rms_norm(q, jnp.tile(q_norm_g, GQA_HEADS)[None, :], GQA_HEADS).reshape(b, s, GQA_HEADS, GQA_HEAD_DIM)
    k = rms_norm(k, jnp.tile(k_norm_g, GQA_KV_HEADS)[None, :], GQA_KV_HEADS).reshape(b, s, GQA_KV_HEADS, GQA_HEAD_DIM)
    v = v.reshape(b, s, GQA_KV_HEADS, GQA_HEAD_DIM)
    q = _apply_axial_rope(q, rope)
    k = _apply_axial_rope(k, rope)
    o = attention(_heads_first(q), _heads_first(k), _heads_first(v), GQA_HEAD_DIM ** -0.5)
    return _heads_first(o).reshape(b * s, GQA_WIDTH)


def _mla_group(c_q, c_kv, k_pe, q_norm_g, w_uq, kv_norm_g, w_ukv, rope, b, s):
    q = linear(rms_norm(c_q, q_norm_g[None, :], 1), w_uq).reshape(b, s, MLA_HEADS, MLA_NOPE_DIM + MLA_ROPE_DIM)
    q_nope, q_pe = q[..., :MLA_NOPE_DIM], q[..., MLA_NOPE_DIM:]
    kv = linear(rms_norm(c_kv, kv_norm_g[None, :], 1), w_ukv).reshape(b, s, MLA_HEADS, MLA_NOPE_DIM + MLA_V_DIM)
    k_nope, v = kv[..., :MLA_NOPE_DIM], kv[..., MLA_NOPE_DIM:]
    q_pe = _apply_axial_rope(q_pe, rope)
    k_pe = _apply_axial_rope(k_pe.reshape(b, s, 1, MLA_ROPE_DIM), rope)
    q = jnp.concatenate([q_nope, q_pe], axis=-1)
    k = jnp.concatenate([k_nope, jnp.broadcast_to(k_pe, (b, s, MLA_HEADS, MLA_ROPE_DIM))], axis=-1)
    o = attention(_heads_first(q), _heads_first(k), _heads_first(v), (MLA_NOPE_DIM + MLA_ROPE_DIM) ** -0.5)
    return _heads_first(o).reshape(b * s, MLA_HEADS * MLA_V_DIM)


def _ssd_group(z, xbc, dt_raw, conv_w, conv_b, dt_bias, a_log, d_skip, norm_g, b, s):
    wb = jnp.concatenate([conv_w, conv_b[None, :], jnp.zeros((CONV_PACK_ROWS - SSD_CONV - 1, SSD_CONV_DIM), F32)], axis=0)
    xbc = conv_silu(xbc.reshape(b, s, SSD_CONV_DIM), wb)
    xs = xbc[..., :SSD_INNER]
    bm = xbc[..., SSD_INNER:SSD_INNER + SSD_GROUPS * SSD_STATE]
    cm = xbc[..., SSD_INNER + SSD_GROUPS * SSD_STATE:]
    dt = jax.nn.softplus(dt_raw.reshape(b, s, 2, SSD_HEADS) + dt_bias)
    a_neg = -jnp.exp(a_log)
    y_fwd = ssd_chunked(xs, dt[:, :, 0], a_neg[0], bm, cm)
    flip = lambda t: jnp.flip(t, axis=1)
    y_bwd = flip(ssd_chunked(flip(xs), flip(dt[:, :, 1]), a_neg[1], flip(bm), flip(cm)))
    y = y_fwd + y_bwd + xs * jnp.repeat(d_skip, SSD_HEAD_DIM)
    y = y.reshape(b * s, SSD_INNER) * jax.nn.silu(z)
    return rms_norm(y, norm_g[None, :], SSD_GROUPS)


def _modulate(n, scale, shift, b, s):
    d = n.shape[-1]
    return (n.reshape(b, s, d) * (1.0 + scale[:, None, :]) + shift[:, None, :]).reshape(b * s, d)


def _gated(t, gate, b, s):
    d = t.shape[-1]
    return (t.reshape(b, s, d) * gate[:, None, :]).reshape(b * s, d)


def _layer(x2, mod, w, rope_a, rope_b, b, s):
    shift1, scale1, gate1, shift2, scale2, gate2 = jnp.split(mod, 6, axis=-1)
    h = _modulate(rms_norm(x2, w["norm1_g"][None, :], 1), scale1, shift1, b, s)
    proj = linear(h, w["w_in"])
    idx = np.cumsum(IN_SPLITS).tolist()
    q_a, k_a, v_a, cq_b, ckv_b, kpe_b, z_c, xbc_c, dt_c = [proj[:, lo:hi] for lo, hi in zip([0] + idx[:-1], idx)]
    o_a = _gqa_group(q_a, k_a, v_a, w["q_norm_g"], w["k_norm_g"], rope_a, b, s)
    o_b = _mla_group(cq_b, ckv_b, kpe_b, w["mla_q_norm_g"], w["w_uq"], w["mla_kv_norm_g"], w["w_ukv"], rope_b, b, s)
    o_c = _ssd_group(z_c, xbc_c, dt_c, w["conv_w"], w["conv_b"], w["dt_bias"], w["a_log"], w["d_skip"],
                     w["ssd_norm_g"], b, s)
    mix = linear(jnp.concatenate([o_a, o_b, o_c], axis=-1), w["w_out"])
    x2 = x2 + _gated(mix, gate1, b, s)
    h = _modulate(rms_norm(x2, w["norm2_g"][None, :], 1), scale2, shift2, b, s)
    gu = linear(h, w["w_gate_up"])
    ffn = gu.shape[-1] // 2
    act = jax.nn.silu(gu[:, :ffn]) * gu[:, ffn:]
    return x2 + _gated(linear(act, w["w_down"]), gate2, b, s)


def _local_loss(diff_args, target):
    x, mod, layers, final_norm_g = diff_args
    b, s, d = x.shape
    rope_a = _axial_rope_tables(s, GQA_HEAD_DIM)
    rope_b = _axial_rope_tables(s, MLA_ROPE_DIM)
    x2 = x.reshape(b * s, d)
    for l, w in enumerate(layers):
        x2 = _layer(x2, mod[l], w, rope_a, rope_b, b, s)
    y = rms_norm(x2, final_norm_g[None, :], 1)
    return loss_head(y, target.reshape(b * s, d))


ANY = pl.BlockSpec(memory_space=pl.ANY)


def _flip_if(v, bit):
    return 1 - v if bit else v


def _all_gather_devices(x):
    def body(x_ref, out_ref, send_sems, recv_sems, local_sem):
        mx, my, mc = lax.axis_index("x"), lax.axis_index("y"), lax.axis_index("c")
        me = 4 * mx + 2 * my + mc
        mine = pltpu.make_async_copy(x_ref, out_ref.at[me], local_sem)
        mine.start()
        sends = []
        for k in range(1, N_DEV):
            peer = (_flip_if(mx, k & 4), _flip_if(my, k & 2), _flip_if(mc, k & 1))
            cp = pltpu.make_async_remote_copy(src_ref=x_ref, dst_ref=out_ref.at[me], send_sem=send_sems.at[k - 1],
                                              recv_sem=recv_sems.at[k - 1], device_id=peer, device_id_type=MESH)
            cp.start()
            sends.append(cp)
        for k in range(1, N_DEV):
            peer = (_flip_if(mx, k & 4), _flip_if(my, k & 2), _flip_if(mc, k & 1))
            src = 4 * peer[0] + 2 * peer[1] + peer[2]
            pltpu.make_async_remote_copy(src_ref=x_ref, dst_ref=out_ref.at[src], send_sem=send_sems.at[k - 1],
                                         recv_sem=recv_sems.at[k - 1], device_id=peer, device_id_type=MESH).wait_recv()
        for cp in sends:
            cp.wait_send()
        mine.wait()

    return pl.pallas_call(
        body, name="all_gather_devices", in_specs=[ANY], out_specs=ANY,
        out_shape=jax.ShapeDtypeStruct((N_DEV,) + x.shape, x.dtype),
        scratch_shapes=[pltpu.SemaphoreType.DMA((N_DEV - 1,)), pltpu.SemaphoreType.DMA((N_DEV - 1,)),
                        pltpu.SemaphoreType.DMA],    )(x)


def _all_gather_chips(shards):
    n = len(shards)

    def body(*refs):
        ins, outs = refs[:n], refs[n:2 * n]
        send_sems, recv_sems, local_sems = refs[2 * n:]
        mx, my, mc = lax.axis_index("x"), lax.axis_index("y"), lax.axis_index("c")
        me = 2 * mx + my
        local, sends = [], []
        for i in range(n):
            cp = pltpu.make_async_copy(ins[i], outs[i].at[me], local_sems.at[i])
            cp.start()
            local.append(cp)
        for k in range(1, N_CHIPS):
            peer = (_flip_if(mx, k & 2), _flip_if(my, k & 1), mc)
            for i in range(n):
                j = (k - 1) * n + i
                cp = pltpu.make_async_remote_copy(src_ref=ins[i], dst_ref=outs[i].at[me], send_sem=send_sems.at[j],
                                                  recv_sem=recv_sems.at[j], device_id=peer, device_id_type=MESH)
                cp.start()
                sends.append(cp)
        for k in range(1, N_CHIPS):
            peer = (_flip_if(mx, k & 2), _flip_if(my, k & 1), mc)
            src = 2 * peer[0] + peer[1]
            for i in range(n):
                j = (k - 1) * n + i
                pltpu.make_async_remote_copy(src_ref=ins[i], dst_ref=outs[i].at[src], send_sem=send_sems.at[j],
                                             recv_sem=recv_sems.at[j], device_id=peer, device_id_type=MESH).wait_recv()
        for cp in sends:
            cp.wait_send()
        for cp in local:
            cp.wait()

    return pl.pallas_call(
        body, name="all_gather_chips", in_specs=[ANY] * n, out_specs=[ANY] * n,
        out_shape=[jax.ShapeDtypeStruct((N_CHIPS,) + t.shape, t.dtype) for t in shards],
        scratch_shapes=[pltpu.SemaphoreType.DMA(((N_CHIPS - 1) * n,)), pltpu.SemaphoreType.DMA(((N_CHIPS - 1) * n,)),
                        pltpu.SemaphoreType.DMA((n,))],    )(*shards)


def _sibling_half_exchange(grads):
    n = len(grads)

    def body(*refs):
        ins, outs = refs[:n], refs[n:2 * n]
        send_sems, recv_sems = refs[2 * n:]
        mx, my, mc = lax.axis_index("x"), lax.axis_index("y"), lax.axis_index("c")
        sibling = (mx, my, 1 - mc)
        cps = []
        for i in range(n):
            half = grads[i].shape[1] // 2
            src = ins[i].at[:, pl.ds(pl.multiple_of((1 - mc) * half, 8), half), :]
            cp = pltpu.make_async_remote_copy(src_ref=src, dst_ref=outs[i], send_sem=send_sems.at[i],
                                              recv_sem=recv_sems.at[i], device_id=sibling, device_id_type=MESH)
            cp.start()
            cps.append(cp)
        for cp in cps:
            cp.wait()

    return pl.pallas_call(
        body, name="rs_sibling_exchange", in_specs=[ANY] * n, out_specs=[ANY] * n,
        out_shape=[jax.ShapeDtypeStruct((t.shape[0], t.shape[1] // 2, t.shape[2]), t.dtype) for t in grads],
        scratch_shapes=[pltpu.SemaphoreType.DMA((n,)), pltpu.SemaphoreType.DMA((n,))],    )(*grads)


def _add_own_half(g, recv, core):
    nb, r, c = g.shape
    half = r // 2
    tr = _row_tile(half, c * 4)
    per_half = half // tr

    def body(core_ref, g_ref, r_ref, o_ref):
        o_ref[...] = g_ref[...] + r_ref[...]

    return pl.pallas_call(
        body, name="rs_add_own_half",
        grid_spec=pltpu.PrefetchScalarGridSpec(
            num_scalar_prefetch=1, grid=(nb, per_half),
            in_specs=[pl.BlockSpec((1, tr, c), lambda k, i, core_ref: (k, core_ref[0] * per_half + i, 0)),
                      pl.BlockSpec((1, tr, c), lambda k, i, core_ref: (k, i, 0))],
            out_specs=pl.BlockSpec((1, tr, c), lambda k, i, core_ref: (k, i, 0))),
        out_shape=jax.ShapeDtypeStruct((nb, half, c), F32),
        compiler_params=pltpu.CompilerParams(dimension_semantics=("parallel", "parallel")),
    )(core, g, recv)


def _chip_exchange(parts):
    n = len(parts)

    def body(*refs):
        ins, outs = refs[:n], refs[n:2 * n]
        send_sems, recv_sems, local_sems = refs[2 * n:]
        mx, my, mc = lax.axis_index("x"), lax.axis_index("y"), lax.axis_index("c")
        me = 2 * mx + my
        local, sends = [], []
        for i in range(n):
            cp = pltpu.make_async_copy(ins[i].at[me], outs[i].at[me], local_sems.at[i])
            cp.start()
            local.append(cp)
        for k in range(1, N_CHIPS):
            peer = (_flip_if(mx, k & 2), _flip_if(my, k & 1), mc)
            dst_chip = 2 * peer[0] + peer[1]
            for i in range(n):
                j = (k - 1) * n + i
                cp = pltpu.make_async_remote_copy(src_ref=ins[i].at[dst_chip], dst_ref=outs[i].at[me],
                                                  send_sem=send_sems.at[j], recv_sem=recv_sems.at[j],
                                                  device_id=peer, device_id_type=MESH)
                cp.start()
                sends.append(cp)
        for k in range(1, N_CHIPS):
            peer = (_flip_if(mx, k & 2), _flip_if(my, k & 1), mc)
            src_chip = 2 * peer[0] + peer[1]
            for i in range(n):
                j = (k - 1) * n + i
                pltpu.make_async_remote_copy(src_ref=ins[i].at[src_chip], dst_ref=outs[i].at[src_chip],
                                             send_sem=send_sems.at[j], recv_sem=recv_sems.at[j],
                                             device_id=peer, device_id_type=MESH).wait_recv()
        for cp in sends:
            cp.wait_send()
        for cp in local:
            cp.wait()

    return pl.pallas_call(
        body, name="rs_chip_exchange", in_specs=[ANY] * n, out_specs=[ANY] * n,
        out_shape=[jax.ShapeDtypeStruct(t.shape, t.dtype) for t in parts],
        scratch_shapes=[pltpu.SemaphoreType.DMA(((N_CHIPS - 1) * n,)), pltpu.SemaphoreType.DMA(((N_CHIPS - 1) * n,)),
                        pltpu.SemaphoreType.DMA((n,))],    )(*parts)


def _sum_leading(t, name):
    nb, r, c = t.shape
    tr = _row_tile(r, c * 4 * nb)

    def body(t_ref, o_ref):
        acc = t_ref[0]
        for k in range(1, nb):
            acc = acc + t_ref[k]
        o_ref[...] = acc

    return pl.pallas_call(
        body, name=name, grid=(r // tr,),
        in_specs=[pl.BlockSpec((nb, tr, c), lambda i: (0, i, 0))],
        out_specs=pl.BlockSpec((tr, c), lambda i: (i, 0)),
        out_shape=jax.ShapeDtypeStruct((r, c), F32),
        compiler_params=pltpu.CompilerParams(dimension_semantics=("parallel",)),
    )(t)


def _sibling_concat(halves):
    n = len(halves)

    def body(*refs):
        ins, outs = refs[:n], refs[n:2 * n]
        send_sems, recv_sems, local_sems = refs[2 * n:]
        mx, my, mc = lax.axis_index("x"), lax.axis_index("y"), lax.axis_index("c")
        sibling = (mx, my, 1 - mc)
        cps, local = [], []
        for i in range(n):
            half = halves[i].shape[0]
            mine = outs[i].at[pl.ds(pl.multiple_of(mc * half, 8), half), :]
            lc = pltpu.make_async_copy(ins[i], mine, local_sems.at[i])
            lc.start()
            local.append(lc)
            cp = pltpu.make_async_remote_copy(src_ref=ins[i], dst_ref=mine, send_sem=send_sems.at[i],
                                              recv_sem=recv_sems.at[i], device_id=sibling, device_id_type=MESH)
            cp.start()
            cps.append(cp)
        for i in range(n):
            half = halves[i].shape[0]
            theirs = outs[i].at[pl.ds(pl.multiple_of((1 - mc) * half, 8), half), :]
            pltpu.make_async_remote_copy(src_ref=ins[i], dst_ref=theirs, send_sem=send_sems.at[i],
                                         recv_sem=recv_sems.at[i], device_id=sibling, device_id_type=MESH).wait_recv()
        for cp in cps:
            cp.wait_send()
        for lc in local:
            lc.wait()

    return pl.pallas_call(
        body, name="rs_sibling_concat", in_specs=[ANY] * n, out_specs=[ANY] * n,
        out_shape=[jax.ShapeDtypeStruct((2 * t.shape[0], t.shape[1]), t.dtype) for t in halves],
        scratch_shapes=[pltpu.SemaphoreType.DMA((n,)), pltpu.SemaphoreType.DMA((n,)), pltpu.SemaphoreType.DMA((n,))],    )(*halves)


def _reduce_scatter(grads, core):
    recv = _sibling_half_exchange(grads)
    parts = [_add_own_half(g, r, core) for g, r in zip(grads, recv)]
    gathered = _chip_exchange(parts)
    halves = [_sum_leading(t, "rs_sum_chips") for t in gathered]
    return _sibling_concat(halves)


def _adamw(w, g, m, v):
    shape = w.shape
    cols = shape[-1]
    rows = int(np.prod(shape[:-1])) if len(shape) > 1 else 1
    w2, g2, m2, v2 = [t.reshape(rows, cols) for t in (w, g, m, v)]
    tr = _row_tile(rows, cols * 4 * 4)

    def body(w_ref, g_ref, m_ref, v_ref, d_ref, mo_ref, vo_ref):
        gv = g_ref[...]
        mn = ADAM_B1 * m_ref[...] + (1.0 - ADAM_B1) * gv
        vn = ADAM_B2 * v_ref[...] + (1.0 - ADAM_B2) * (gv * gv)
        m_hat = mn / (1.0 - ADAM_B1 ** ADAM_STEP)
        v_hat = vn / (1.0 - ADAM_B2 ** ADAM_STEP)
        d_ref[...] = -ADAM_LR * (m_hat / (jnp.sqrt(v_hat) + ADAM_EPS) + ADAM_WD * w_ref[...])
        mo_ref[...] = mn
        vo_ref[...] = vn

    spec = pl.BlockSpec((tr, cols), lambda i: (i, 0))
    outs = pl.pallas_call(
        body, name="adamw", grid=(rows // tr,), in_specs=[spec] * 4, out_specs=[spec] * 3,
        out_shape=[jax.ShapeDtypeStruct((rows, cols), F32)] * 3,
        compiler_params=pltpu.CompilerParams(dimension_semantics=("parallel",)),
    )(w2, g2, m2, v2)
    return [t.reshape(shape) for t in outs]


WEIGHTS = ['w_ada', 'b_ada', 'norm1_g', 'norm2_g', 'w_in', 'q_norm_g', 'k_norm_g', 'mla_q_norm_g', 'w_uq',
           'mla_kv_norm_g', 'w_ukv', 'conv_w', 'conv_b', 'dt_bias', 'a_log', 'd_skip', 'ssd_norm_g', 'w_out',
           'w_gate_up', 'w_down', 'final_norm_g']
COL_SHARDED = ('w_in', 'w_uq', 'w_ukv', 'w_gate_up')
ROW_SHARDED = ('w_out', 'w_down')
SMALL_LAYER = ('norm1_g', 'norm2_g', 'q_norm_g', 'k_norm_g', 'mla_q_norm_g', 'mla_kv_norm_g', 'conv_w', 'conv_b',
               'dt_bias', 'a_log', 'd_skip', 'ssd_norm_g')


def _pack(parts):
    flat = jnp.concatenate([p.reshape(-1) for p in parts])
    n = flat.shape[0]
    rows = -(-n // (8 * LANES)) * 8
    return jnp.pad(flat, (0, rows * LANES - n)).reshape(rows, LANES)


def _unpack(flat, shapes):
    out, pos = [], 0
    for shp in shapes:
        size = int(np.prod(shp))
        out.append(flat[pos:pos + size].reshape(shp))
        pos += size
    return out


def _cols_full(gathered):
    k, r, c = gathered.shape
    return jnp.transpose(gathered, (1, 0, 2)).reshape(r, k * c)


def _cols_split(full):
    r, c4 = full.shape
    return jnp.transpose(full.reshape(r, N_CHIPS, c4 // N_CHIPS), (1, 0, 2))


def kernel(x, c, w_ada, b_ada, norm1_g, norm2_g, w_in, q_norm_g, k_norm_g, mla_q_norm_g, w_uq, mla_kv_norm_g, w_ukv, conv_w, conv_b, dt_bias, a_log, d_skip, ssd_norm_g, w_out, w_gate_up, w_down, final_norm_g, loss_target, m_w_ada, m_b_ada, m_norm1_g, m_norm2_g, m_w_in, m_q_norm_g, m_k_norm_g, m_mla_q_norm_g, m_w_uq, m_mla_kv_norm_g, m_w_ukv, m_conv_w, m_conv_b, m_dt_bias, m_a_log, m_d_skip, m_ssd_norm_g, m_w_out, m_w_gate_up, m_w_down, m_final_norm_g, v_w_ada, v_b_ada, v_norm1_g, v_norm2_g, v_w_in, v_q_norm_g, v_k_norm_g, v_mla_q_norm_g, v_w_uq, v_mla_kv_norm_g, v_w_ukv, v_conv_w, v_conv_b, v_dt_bias, v_a_log, v_d_skip, v_ssd_norm_g, v_w_out, v_w_gate_up, v_w_down, v_final_norm_g):
    args = dict(locals())
    weights = {n: args[n] for n in WEIGHTS}
    depth = w_in.shape[0]
    bl, s, d = x.shape
    mx, my, mc = lax.axis_index("x"), lax.axis_index("y"), lax.axis_index("c")
    chip = 2 * mx + my
    dev = 2 * chip + mc
    core = mc.astype(jnp.int32).reshape(1)
    ada_cols = w_ada.shape[-1]
    conv_cols = conv_w.shape[-1]

    first_shapes = [c.shape, conv_w.shape]
    first = _all_gather_devices(_pack([c, conv_w]))
    first = [_unpack(first[i].reshape(-1), first_shapes) for i in range(N_DEV)]
    c_act = jax.nn.silu(jnp.concatenate([f[0] for f in first], axis=0))
    conv_w_full = jnp.concatenate([first[2 * k][1] for k in range(N_CHIPS)], axis=-1)

    b_cols = lax.dynamic_slice_in_dim(b_ada, chip * ada_cols, ada_cols, axis=1)
    mod_cols = jnp.stack([_matmul(c_act, w_ada[l], name="ada_fwd") + b_cols[l][None, :] for l in range(depth)])
    mod_all = _all_gather_devices(mod_cols.reshape(depth * N_DEV * bl, ada_cols))
    mod_all = mod_all.reshape(N_DEV, depth, N_DEV, bl, ada_cols)
    mod_mine = lax.dynamic_index_in_dim(mod_all, dev, axis=2, keepdims=False)
    mod = jnp.concatenate([mod_mine[2 * k] for k in range(N_CHIPS)], axis=-1)

    big = COL_SHARDED + ROW_SHARDED
    gathered = {}
    for l in range(depth):
        outs = _all_gather_chips([weights[n][l].astype(BF16) for n in big])
        for n, g in zip(big, outs):
            gathered[(n, l)] = g
    layers = []
    for l in range(depth):
        w = {n: weights[n][l] for n in SMALL_LAYER if n != 'conv_w'}
        w['conv_w'] = conv_w_full[l]
        for n in COL_SHARDED:
            w[n] = _cols_full(gathered[(n, l)]).astype(F32)
        w['w_in'] = jnp.pad(w['w_in'], ((0, 0), (0, IN_COLS_PAD - IN_COLS)))
        for n in ROW_SHARDED:
            g = gathered[(n, l)]
            w[n] = g.reshape(g.shape[0] * g.shape[1], g.shape[2]).astype(F32)
        layers.append(w)

    loss_local, (gx, gmod, glayers, gfinal) = jax.value_and_grad(_local_loss)(
        (x, mod, layers, final_norm_g), loss_target)

    grads = {}
    for l in reversed(range(depth)):
        blocks = []
        for n in COL_SHARDED:
            g = glayers[l][n]
            if n == 'w_in':
                g = g[:, :IN_COLS]
            blocks.append(_cols_split(g))
        for n in ROW_SHARDED:
            g = glayers[l][n]
            blocks.append(g.reshape(N_CHIPS, g.shape[0] // N_CHIPS, g.shape[1]))
        for n, g in zip(big, _reduce_scatter(blocks, core)):
            grads[(n, l)] = g

    small_parts = [jnp.stack([glayers[l][n] for l in range(depth)]) for n in SMALL_LAYER]
    small_parts += [gfinal, loss_local.reshape(1), gmod]
    small_shapes = [p.shape for p in small_parts]
    last = _all_gather_devices(_pack(small_parts))
    summed = _unpack(_sum_leading(last, "sum_devices").reshape(-1), small_shapes)
    small = dict(zip(SMALL_LAYER, summed[:len(SMALL_LAYER)]))
    g_final, loss, gmod_sum = summed[len(SMALL_LAYER):]
    small['conv_w'] = lax.dynamic_slice_in_dim(small['conv_w'], chip * conv_cols, conv_cols, axis=2)
    gmod_all = jnp.stack([_unpack(last[i].reshape(-1), small_shapes)[-1] for i in range(N_DEV)], axis=1)
    gmod_all = gmod_all.reshape(depth, N_DEV * bl, gmod.shape[-1])
    gmod_cols = lax.dynamic_slice_in_dim(gmod_all, chip * ada_cols, ada_cols, axis=2)
    g_w_ada = jnp.stack([_matmul(c_act, gmod_cols[l], ta=True, name="ada_dw") for l in range(depth)])
    g_b_ada = gmod_sum[:, 0]
    for i in range(1, bl):
        g_b_ada = g_b_ada + gmod_sum[:, i]

    grad = {'w_ada': g_w_ada, 'b_ada': g_b_ada, 'final_norm_g': g_final}
    for n in SMALL_LAYER:
        grad[n] = small[n]
    for n in big:
        grad[n] = jnp.stack([grads[(n, l)] for l in range(depth)])

    delta, new_m, new_v = {}, {}, {}
    for n in WEIGHTS:
        delta[n], new_m[n], new_v[n] = _adamw(weights[n], grad[n], args["m_" + n], args["v_" + n])
    return (loss.reshape(()), gx, *[grad[n] for n in WEIGHTS], *[delta[n] for n in WEIGHTS],
            *[new_m[n] for n in WEIGHTS], *[new_v[n] for n in WEIGHTS])
```

```python
import functools

import numpy as np
import jax
import jax.numpy as jnp
from jax import lax
from jax.experimental import pallas as pl
from jax.experimental.pallas import tpu as pltpu

F32 = jnp.float32
BF16 = jnp.bfloat16
HIGHEST = lax.Precision.HIGHEST
MESH = pl.DeviceIdType.MESH

GRID_W = 64
ROPE_THETA = 10000.0
EPS = 1e-6

GQA_HEADS, GQA_KV_HEADS, GQA_HEAD_DIM = 6, 2, 128
GQA_WIDTH = GQA_HEADS * GQA_HEAD_DIM
GQA_KV_WIDTH = GQA_KV_HEADS * GQA_HEAD_DIM
MLA_HEADS, MLA_Q_LORA, MLA_KV_LORA = 4, 512, 256
MLA_NOPE_DIM, MLA_ROPE_DIM, MLA_V_DIM = 128, 64, 128
SSD_HEADS, SSD_HEAD_DIM, SSD_GROUPS, SSD_STATE, SSD_CONV, SSD_CHUNK = 12, 64, 2, 128, 5, 128
SSD_INNER = SSD_HEADS * SSD_HEAD_DIM
SSD_CONV_DIM = SSD_INNER + 2 * SSD_GROUPS * SSD_STATE
SSD_GROUP_HEADS = SSD_HEADS // SSD_GROUPS
SSD_GROUP_WIDTH = SSD_GROUP_HEADS * SSD_HEAD_DIM
IN_SPLITS = (GQA_WIDTH, GQA_KV_WIDTH, GQA_KV_WIDTH, MLA_Q_LORA, MLA_KV_LORA, MLA_ROPE_DIM, SSD_INNER, SSD_CONV_DIM,
             2 * SSD_HEADS)
IN_COLS = sum(IN_SPLITS)
LANES = 128
IN_COLS_PAD = -(-IN_COLS // LANES) * LANES

ADAM_LR, ADAM_B1, ADAM_B2, ADAM_EPS, ADAM_WD, ADAM_STEP = 0.001, 0.9, 0.999, 1e-08, 0.01, 10

N_CHIPS = 4
N_DEV = 8
TILE_BYTES = 2 * 1024 * 1024


def _pick(n, cands):
    for t in cands:
        if n % t == 0:
            return t
    return n


def _row_tile(rows, row_bytes):
    for t in (2048, 1024, 512, 256, 128, 64, 32, 16, 8):
        if rows % t == 0 and t * row_bytes <= TILE_BYTES:
            return t
    return rows


MM_VMEM_BUDGET = 36 * 1024 * 1024
MM_VMEM_LIMIT = 56 * 1024 * 1024
MM_MAX_TILE = 2048
MM_MAX_K_TILE = 4096
MXU_DIM = 256
HBM_BYTES_PER_US = 3.0e6
MXU_FLOPS_PER_US = 9.0e8
STEP_US = 0.35


def _tile_cands(d, cap):
    if d % LANES:
        return [d]
    return [t for t in range(LANES, min(d, cap) + 1, LANES) if d % t == 0] or [d]


def _mm_tiles(m, n, kdim):
    up = lambda t: -(-t // MXU_DIM) * MXU_DIM
    best = None
    for tm in _tile_cands(m, MM_MAX_TILE):
        for tn in _tile_cands(n, MM_MAX_TILE):
            for tk in _tile_cands(kdim, MM_MAX_K_TILE):
                if 2 * (tm * tk * 2 + tk * tn * 2 + tm * tn * 4) > MM_VMEM_BUDGET:
                    continue
                ni, nj, nk = m // tm, n // tn, kdim // tk
                a_reads = 1 if nk == 1 else nj
                b_reads = 1 if (nk == 1 and nj == 1) else ni
                hbm = (m * kdim * 2 * a_reads + kdim * n * 2 * b_reads + m * n * 4) / HBM_BYTES_PER_US
                mxu = ni * nj * nk * 2.0 * max(tm, 8) * up(tn) * up(tk) / MXU_FLOPS_PER_US
                cost = max(hbm, mxu) + 0.25 * min(hbm, mxu) + ni * nj * nk * STEP_US
                if best is None or cost < best[0]:
                    best = (cost, tm, tn, tk)
    return best[1:]


def _matmul(a, b, ta=False, tb=False, name="mm"):
    assert a.dtype == BF16 and b.dtype == BF16, (a.dtype, b.dtype)
    if ta:
        kdim, m = a.shape
    else:
        m, kdim = a.shape
    if tb:
        n, k2 = b.shape
    else:
        k2, n = b.shape
    assert kdim == k2, (a.shape, b.shape, ta, tb)
    tm, tn, tk = _mm_tiles(m, n, kdim)
    nk = kdim // tk
    dn = (((0 if ta else 1,), (1 if tb else 0,)), ((), ()))

    def body(a_ref, b_ref, o_ref):
        part = lax.dot_general(a_ref[...], b_ref[...], dn, preferred_element_type=F32)
        if nk == 1:
            o_ref[...] = part
        else:
            k = pl.program_id(2)

            @pl.when(k == 0)
            def _():
                o_ref[...] = part

            @pl.when(k > 0)
            def _():
                o_ref[...] += part

    a_spec = pl.BlockSpec((tk, tm), lambda i, j, k: (k, i)) if ta else pl.BlockSpec((tm, tk), lambda i, j, k: (i, k))
    b_spec = pl.BlockSpec((tn, tk), lambda i, j, k: (j, k)) if tb else pl.BlockSpec((tk, tn), lambda i, j, k: (k, j))
    return pl.pallas_call(
        body, name=name, grid=(m // tm, n // tn, nk),
        in_specs=[a_spec, b_spec], out_specs=pl.BlockSpec((tm, tn), lambda i, j, k: (i, j)),
        out_shape=jax.ShapeDtypeStruct((m, n), F32),
        compiler_params=pltpu.CompilerParams(dimension_semantics=("parallel", "arbitrary", "arbitrary"),
                                             vmem_limit_bytes=MM_VMEM_LIMIT),
    )(a, b)


@jax.custom_vjp
def linear(x, w):
    return _matmul(x.astype(BF16), w.astype(BF16), name="linear_fwd")


def _linear_fwd(x, w):
    xb, wb = x.astype(BF16), w.astype(BF16)
    return _matmul(xb, wb, name="linear_fwd"), (xb, wb)


def _linear_bwd(res, dy):
    xb, wb = res
    dyb = dy.astype(BF16)
    return _matmul(dyb, wb, tb=True, name="linear_dx"), _matmul(xb, dyb, ta=True, name="linear_dw")


linear.defvjp(_linear_fwd, _linear_bwd)


def _rms_fwd_call(x, g, groups):
    rows, cols = x.shape
    d = cols // groups
    tr = _row_tile(rows, cols * 4)

    def body(x_ref, g_ref, y_ref):
        for gi in range(groups):
            sl = slice(gi * d, (gi + 1) * d)
            xs = x_ref[:, sl]
            r = lax.rsqrt(jnp.mean(xs * xs, axis=-1, keepdims=True) + EPS)
            y_ref[:, sl] = xs * r * g_ref[:, sl]

    return pl.pallas_call(
        body, name="rms_fwd", grid=(rows // tr,),
        in_specs=[pl.BlockSpec((tr, cols), lambda i: (i, 0)), pl.BlockSpec((1, cols), lambda i: (0, 0))],
        out_specs=pl.BlockSpec((tr, cols), lambda i: (i, 0)),
        out_shape=jax.ShapeDtypeStruct((rows, cols), F32),
        compiler_params=pltpu.CompilerParams(dimension_semantics=("parallel",)),
    )(x, g)


def _rms_bwd_call(x, g, dy, groups):
    rows, cols = x.shape
    d = cols // groups
    tr = _row_tile(rows, cols * 4)

    def body(x_ref, g_ref, dy_ref, dx_ref, dg_ref):
        @pl.when(pl.program_id(0) == 0)
        def _():
            dg_ref[...] = jnp.zeros_like(dg_ref)

        for gi in range(groups):
            sl = slice(gi * d, (gi + 1) * d)
            xs = x_ref[:, sl]
            dys = dy_ref[:, sl]
            r = lax.rsqrt(jnp.mean(xs * xs, axis=-1, keepdims=True) + EPS)
            xhat = xs * r
            dg_ref[:, sl] += jnp.sum(dys * xhat, axis=0, keepdims=True)
            dxhat = dys * g_ref[:, sl]
            dx_ref[:, sl] = r * (dxhat - xhat * jnp.mean(dxhat * xhat, axis=-1, keepdims=True))

    return pl.pallas_call(
        body, name="rms_bwd", grid=(rows // tr,),
        in_specs=[pl.BlockSpec((tr, cols), lambda i: (i, 0)), pl.BlockSpec((1, cols), lambda i: (0, 0)),
                  pl.BlockSpec((tr, cols), lambda i: (i, 0))],
        out_specs=[pl.BlockSpec((tr, cols), lambda i: (i, 0)), pl.BlockSpec((1, cols), lambda i: (0, 0))],
        out_shape=[jax.ShapeDtypeStruct((rows, cols), F32), jax.ShapeDtypeStruct((1, cols), F32)],
        compiler_params=pltpu.CompilerParams(dimension_semantics=("arbitrary",)),
    )(x, g, dy)


@functools.partial(jax.custom_vjp, nondiff_argnums=(2,))
def rms_norm(x, g, groups):
    return _rms_fwd_call(x, g, groups)


def _rms_norm_fwd(x, g, groups):
    return _rms_fwd_call(x, g, groups), (x, g)


def _rms_norm_bwd(groups, res, dy):
    x, g = res
    dx, dg = _rms_bwd_call(x, g, dy, groups)
    return dx, dg


rms_norm.defvjp(_rms_norm_fwd, _rms_norm_bwd)


NT_DIMS = (((1,), (1,)), ((), ()))
TN_DIMS = (((0,), (0,)), ((), ()))


def _softmax_rows(q, k, scale):
    s = lax.dot_general(q, k, NT_DIMS, preferred_element_type=F32) * scale
    p = jnp.exp(s - jnp.max(s, axis=-1, keepdims=True))
    return p / jnp.sum(p, axis=-1, keepdims=True)


def _attn_fwd_call(q, k, v, scale):
    b, h, s, dk = q.shape
    hkv, dv = k.shape[1], v.shape[3]
    rep = h // hkv
    tq = _pick(s, (256, 128))

    def body(q_ref, k_ref, v_ref, o_ref):
        p = _softmax_rows(q_ref[0, 0], k_ref[0, 0], scale)
        o_ref[0, 0] = jnp.dot(p.astype(BF16), v_ref[0, 0], preferred_element_type=F32)

    return pl.pallas_call(
        body, name="attn_fwd", grid=(b, h, s // tq),
        in_specs=[pl.BlockSpec((1, 1, tq, dk), lambda bi, hi, qi: (bi, hi, qi, 0)),
                  pl.BlockSpec((1, 1, s, dk), lambda bi, hi, qi: (bi, hi // rep, 0, 0)),
                  pl.BlockSpec((1, 1, s, dv), lambda bi, hi, qi: (bi, hi // rep, 0, 0))],
        out_specs=pl.BlockSpec((1, 1, tq, dv), lambda bi, hi, qi: (bi, hi, qi, 0)),
        out_shape=jax.ShapeDtypeStruct((b, h, s, dv), F32),
        compiler_params=pltpu.CompilerParams(dimension_semantics=("parallel", "parallel", "parallel")),
    )(q, k, v)


def _attn_bwd_call(q, k, v, do, scale):
    b, h, s, dk = q.shape
    hkv, dv = k.shape[1], v.shape[3]
    rep = h // hkv
    tq = _pick(s, (256, 128))

    def body(q_ref, k_ref, v_ref, do_ref, dq_ref, dk_ref, dv_ref):
        @pl.when((pl.program_id(2) == 0) & (pl.program_id(3) == 0))
        def _():
            dk_ref[...] = jnp.zeros_like(dk_ref)
            dv_ref[...] = jnp.zeros_like(dv_ref)

        qb = q_ref[0, 0]
        kb = k_ref[0, 0]
        vb = v_ref[0, 0]
        dob = do_ref[0, 0]
        p = _softmax_rows(qb, kb, scale)
        dp = lax.dot_general(dob, vb, NT_DIMS, preferred_element_type=F32)
        ds = (p * (dp - jnp.sum(p * dp, axis=-1, keepdims=True)) * scale).astype(BF16)
        dq_ref[0, 0] = jnp.dot(ds, kb, preferred_element_type=F32)
        dk_ref[0, 0] += lax.dot_general(ds, qb, TN_DIMS, preferred_element_type=F32)
        dv_ref[0, 0] += lax.dot_general(p.astype(BF16), dob, TN_DIMS, preferred_element_type=F32)

    return pl.pallas_call(
        body, name="attn_bwd", grid=(b, hkv, rep, s // tq),
        in_specs=[pl.BlockSpec((1, 1, tq, dk), lambda bi, gi, ri, qi: (bi, gi * rep + ri, qi, 0)),
                  pl.BlockSpec((1, 1, s, dk), lambda bi, gi, ri, qi: (bi, gi, 0, 0)),
                  pl.BlockSpec((1, 1, s, dv), lambda bi, gi, ri, qi: (bi, gi, 0, 0)),
                  pl.BlockSpec((1, 1, tq, dv), lambda bi, gi, ri, qi: (bi, gi * rep + ri, qi, 0))],
        out_specs=[pl.BlockSpec((1, 1, tq, dk), lambda bi, gi, ri, qi: (bi, gi * rep + ri, qi, 0)),
                   pl.BlockSpec((1, 1, s, dk), lambda bi, gi, ri, qi: (bi, gi, 0, 0)),
                   pl.BlockSpec((1, 1, s, dv), lambda bi, gi, ri, qi: (bi, gi, 0, 0))],
        out_shape=[jax.ShapeDtypeStruct(q.shape, F32), jax.ShapeDtypeStruct(k.shape, F32),
                   jax.ShapeDtypeStruct(v.shape, F32)],
        compiler_params=pltpu.CompilerParams(
            dimension_semantics=("parallel", "parallel", "arbitrary", "arbitrary")),
    )(q, k, v, do)


@functools.partial(jax.custom_vjp, nondiff_argnums=(3,))
def attention(q, k, v, scale):
    return _attn_fwd_call(q.astype(BF16), k.astype(BF16), v.astype(BF16), scale)


def _attention_fwd(q, k, v, scale):
    qb, kb, vb = q.astype(BF16), k.astype(BF16), v.astype(BF16)
    return _attn_fwd_call(qb, kb, vb, scale), (qb, kb, vb)


def _attention_bwd(scale, res, do):
    qb, kb, vb = res
    return tuple(_attn_bwd_call(qb, kb, vb, do.astype(BF16), scale))


attention.defvjp(_attention_fwd, _attention_bwd)


CONV_COL_TILE = 256
CONV_PACK_ROWS = 8


def _shifted(x, off, rows):
    if off == 0:
        return x
    s = x.shape[0]
    rolled = pltpu.roll(x, (-off) % s, 0)
    valid = (rows + off >= 0) & (rows + off < s)
    return jnp.where(valid, rolled, 0.0)


def _conv_pre(x, wb_ref, rows):
    z = jnp.zeros_like(x) + wb_ref[SSD_CONV:SSD_CONV + 1, :]
    for j in range(SSD_CONV):
        z = z + wb_ref[j:j + 1, :] * _shifted(x, j - SSD_CONV // 2, rows)
    return z


def _conv_fwd_call(x, wb):
    b, s, c = x.shape
    tc = _pick(c, (CONV_COL_TILE, LANES))

    def body(x_ref, wb_ref, y_ref):
        xv = x_ref[0]
        rows = lax.broadcasted_iota(jnp.int32, xv.shape, 0)
        z = _conv_pre(xv, wb_ref, rows)
        y_ref[0] = z * jax.nn.sigmoid(z)

    return pl.pallas_call(
        body, name="conv_fwd", grid=(b, c // tc),
        in_specs=[pl.BlockSpec((1, s, tc), lambda bi, ci: (bi, 0, ci)),
                  pl.BlockSpec((CONV_PACK_ROWS, tc), lambda bi, ci: (0, ci))],
        out_specs=pl.BlockSpec((1, s, tc), lambda bi, ci: (bi, 0, ci)),
        out_shape=jax.ShapeDtypeStruct(x.shape, F32),
        compiler_params=pltpu.CompilerParams(dimension_semantics=("parallel", "parallel")),
    )(x, wb)


def _conv_bwd_call(x, wb, dy):
    b, s, c = x.shape
    tc = _pick(c, (CONV_COL_TILE, LANES))

    def body(x_ref, wb_ref, dy_ref, dx_ref, dwb_ref):
        xv = x_ref[0]
        rows = lax.broadcasted_iota(jnp.int32, xv.shape, 0)
        z = _conv_pre(xv, wb_ref, rows)
        sg = jax.nn.sigmoid(z)
        dz = dy_ref[0] * (sg * (1.0 + z * (1.0 - sg)))
        dx = jnp.zeros_like(xv)
        for j in range(SSD_CONV):
            off = j - SSD_CONV // 2
            dx = dx + wb_ref[j:j + 1, :] * _shifted(dz, -off, rows)
            dwb_ref[0, j:j + 1, :] = jnp.sum(dz * _shifted(xv, off, rows), axis=0, keepdims=True)
        dx_ref[0] = dx
        dwb_ref[0, SSD_CONV:SSD_CONV + 1, :] = jnp.sum(dz, axis=0, keepdims=True)
        dwb_ref[0, SSD_CONV + 1:, :] = jnp.zeros((CONV_PACK_ROWS - SSD_CONV - 1, dz.shape[1]), F32)

    return pl.pallas_call(
        body, name="conv_bwd", grid=(b, c // tc),
        in_specs=[pl.BlockSpec((1, s, tc), lambda bi, ci: (bi, 0, ci)),
                  pl.BlockSpec((CONV_PACK_ROWS, tc), lambda bi, ci: (0, ci)),
                  pl.BlockSpec((1, s, tc), lambda bi, ci: (bi, 0, ci))],
        out_specs=[pl.BlockSpec((1, s, tc), lambda bi, ci: (bi, 0, ci)),
                   pl.BlockSpec((1, CONV_PACK_ROWS, tc), lambda bi, ci: (bi, 0, ci))],
        out_shape=[jax.ShapeDtypeStruct(x.shape, F32), jax.ShapeDtypeStruct((b, CONV_PACK_ROWS, c), F32)],
        compiler_params=pltpu.CompilerParams(dimension_semantics=("parallel", "parallel")),
    )(x, wb, dy)


@jax.custom_vjp
def conv_silu(x, wb):
    return _conv_fwd_call(x, wb)


def _conv_silu_fwd(x, wb):
    return _conv_fwd_call(x, wb), (x, wb)


def _conv_silu_bwd(res, dy):
    x, wb = res
    dx, dwb = _conv_bwd_call(x, wb, dy)
    return dx, jnp.sum(dwb, axis=0)


conv_silu.defvjp(_conv_silu_fwd, _conv_silu_bwd)


SSD_PAIRS = SSD_GROUP_HEADS // 2
NEG_INF = -1e30


def _ssd_common(x_ref, dtx_ref, dtt_ref, anx_ref, anc_ref, b_ref, c_ref, reverse):
    L = SSD_CHUNK
    xv = x_ref[0]
    dt = dtx_ref[0]
    ri = lax.broadcasted_iota(jnp.int32, (L, L), 0)
    ci = lax.broadcasted_iota(jnp.int32, (L, L), 1)
    causal = (ri <= ci) if reverse else (ri >= ci)
    tri = causal.astype(F32)
    a_cs = jnp.dot(tri, dt * anx_ref[...], precision=HIGHEST, preferred_element_type=F32)
    a_row = dtt_ref[0, 0] * anc_ref[0]
    acs_row = lax.dot_general(a_row, tri, NT_DIMS, precision=HIGHEST, preferred_element_type=F32)
    xd = xv * dt
    bmat = b_ref[0].astype(BF16)
    cmat = c_ref[0].astype(BF16)
    gmat = lax.dot_general(cmat, bmat, NT_DIMS, preferred_element_type=F32)
    return xv, dt, causal, tri, a_cs, acs_row, xd, bmat, cmat, gmat


def _ssd_lambda(a_cs, acs_row, causal, h):
    col = a_cs[:, h * SSD_HEAD_DIM:h * SSD_HEAD_DIM + 1]
    row = acs_row[h:h + 1, :]
    return jnp.exp(jnp.where(causal, col - row, NEG_INF))


def _ssd_fwd_call(x, dtx, dtt, anx, anc, bm, cm, reverse):
    b, s, _ = x.shape
    L, N, GW = SSD_CHUNK, SSD_STATE, SSD_GROUP_WIDTH
    nc = s // L
    end = 0 if reverse else L - 1

    def body(x_ref, dtx_ref, dtt_ref, anx_ref, anc_ref, b_ref, c_ref, y_ref, hs_ref, state):
        @pl.when(pl.program_id(2) == 0)
        def _():
            state[...] = jnp.zeros_like(state)

        xv, dt, causal, tri, a_cs, acs_row, xd, bmat, cmat, gmat = _ssd_common(
            x_ref, dtx_ref, dtt_ref, anx_ref, anc_ref, b_ref, c_ref, reverse)
        hin = state[...]
        hs_ref[0, 0, 0] = hin
        y_off = jnp.dot(cmat, hin.astype(BF16), preferred_element_type=F32) * jnp.exp(a_cs)
        a_end = a_cs[end:end + 1, :]
        s_new = lax.dot_general(bmat, (xd * jnp.exp(a_end - a_cs)).astype(BF16), TN_DIMS, preferred_element_type=F32)
        state[...] = jnp.exp(a_end) * hin + s_new
        lane = lax.broadcasted_iota(jnp.int32, (L, LANES), 1)
        for pr in range(SSD_PAIRS):
            sl = slice(pr * LANES, (pr + 1) * LANES)
            xdp = xd[:, sl].astype(BF16)
            w0 = (gmat * _ssd_lambda(a_cs, acs_row, causal, 2 * pr)).astype(BF16)
            w1 = (gmat * _ssd_lambda(a_cs, acs_row, causal, 2 * pr + 1)).astype(BF16)
            y0 = jnp.dot(w0, xdp, preferred_element_type=F32)
            y1 = jnp.dot(w1, xdp, preferred_element_type=F32)
            y_ref[0, :, sl] = jnp.where(lane < SSD_HEAD_DIM, y0, y1) + y_off[:, sl]

    G = SSD_GROUPS
    chunk = (lambda c: nc - 1 - c) if reverse else (lambda c: c)
    seq = lambda bi, gi, c: (bi, chunk(c), gi)
    return pl.pallas_call(
        body, name="ssd_fwd", grid=(b, G, nc),
        in_specs=[pl.BlockSpec((1, L, GW), seq),
                  pl.BlockSpec((1, L, GW), seq),
                  pl.BlockSpec((1, 1, SSD_GROUP_HEADS, L), lambda bi, gi, c: (bi, gi, 0, chunk(c))),
                  pl.BlockSpec((1, GW), lambda bi, gi, c: (0, gi)),
                  pl.BlockSpec((1, SSD_GROUP_HEADS, 1), lambda bi, gi, c: (gi, 0, 0)),
                  pl.BlockSpec((1, L, N), seq),
                  pl.BlockSpec((1, L, N), seq)],
        out_specs=[pl.BlockSpec((1, L, GW), seq),
                   pl.BlockSpec((1, 1, 1, N, GW), lambda bi, gi, c: (bi, gi, chunk(c), 0, 0))],
        out_shape=[jax.ShapeDtypeStruct(x.shape, F32), jax.ShapeDtypeStruct((b, G, nc, N, GW), F32)],
        scratch_shapes=[pltpu.VMEM((N, GW), F32)],
        compiler_params=pltpu.CompilerParams(dimension_semantics=("parallel", "parallel", "arbitrary")),
    )(x, dtx, dtt, anx, anc, bm, cm)


def _ssd_bwd_call(x, dtx, dtt, anx, anc, bm, cm, hs, dy, reverse):
    b, s, _ = x.shape
    L, N, GW = SSD_CHUNK, SSD_STATE, SSD_GROUP_WIDTH
    nc = s // L
    end = 0 if reverse else L - 1

    def body(x_ref, dtx_ref, dtt_ref, anx_ref, anc_ref, b_ref, c_ref, hs_ref, dy_ref,
             dx_ref, ddt_ref, dan_ref, db_ref, dc_ref, dstate):
        @pl.when(pl.program_id(2) == 0)
        def _():
            dstate[...] = jnp.zeros_like(dstate)

        xv, dt, causal, tri, a_cs, acs_row, xd, bmat, cmat, gmat = _ssd_common(
            x_ref, dtx_ref, dtt_ref, anx_ref, anc_ref, b_ref, c_ref, reverse)
        hin = hs_ref[0, 0, 0]
        hinb = hin.astype(BF16)
        dyv = dy_ref[0]
        ds_out = dstate[...]
        dsb = ds_out.astype(BF16)
        eacs = jnp.exp(a_cs)
        a_end = a_cs[end:end + 1, :]
        e_end = jnp.exp(a_end)
        dec = jnp.exp(a_end - a_cs)
        dye = dyv * eacs
        dyeb = dye.astype(BF16)
        xdec = xd * dec
        ch = jnp.dot(cmat, hinb, preferred_element_type=F32)
        bds = jnp.dot(bmat, dsb, preferred_element_type=F32)
        t_state = xdec * bds
        d_aend = jnp.sum(t_state, axis=0, keepdims=True) + e_end * jnp.sum(ds_out * hin, axis=0, keepdims=True)
        dacs = dye * ch - t_state
        dxd_state = bds * dec
        dstate[...] = e_end * ds_out + lax.dot_general(cmat, dyeb, TN_DIMS, preferred_element_type=F32)

        lane = lax.broadcasted_iota(jnp.int32, (L, LANES), 1)
        ones = jnp.full((L, LANES), 1.0 / SSD_HEAD_DIM, F32)
        dg = jnp.zeros((L, L), F32)
        dxd_parts, dacs_parts = [], []
        for pr in range(SSD_PAIRS):
            sl = slice(pr * LANES, (pr + 1) * LANES)
            xdp = xd[:, sl]
            dyp = dyv[:, sl]
            dxd_p = jnp.zeros((L, LANES), F32)
            dacs_p = jnp.zeros((L, LANES), F32)
            for half in range(2):
                mine = (lane < SSD_HEAD_DIM) if half == 0 else (lane >= SSD_HEAD_DIM)
                lam = _ssd_lambda(a_cs, acs_row, causal, 2 * pr + half)
                w = gmat * lam
                xdh = jnp.where(mine, xdp, 0.0).astype(BF16)
                dyh = jnp.where(mine, dyp, 0.0).astype(BF16)
                dw = lax.dot_general(dyh, xdh, NT_DIMS, preferred_element_type=F32)
                dg = dg + dw * lam
                mm = dw * w
                rs = jnp.dot(mm, ones, precision=HIGHEST, preferred_element_type=F32)
                cs = lax.dot_general(mm, ones, TN_DIMS, precision=HIGHEST, preferred_element_type=F32)
                dacs_p = dacs_p + jnp.where(mine, rs - cs, 0.0)
                wtdy = lax.dot_general(w.astype(BF16), dyh, TN_DIMS, preferred_element_type=F32)
                dxd_p = dxd_p + wtdy
            dxd_parts.append(dxd_p)
            dacs_parts.append(dacs_p)
        dxd = jnp.concatenate(dxd_parts, axis=1) + dxd_state
        dacs = dacs + jnp.concatenate(dacs_parts, axis=1)
        last = lax.broadcasted_iota(jnp.int32, dacs.shape, 0) == end
        dacs = dacs + jnp.where(last, d_aend, 0.0)
        da = lax.dot_general(tri, dacs, TN_DIMS, precision=HIGHEST, preferred_element_type=F32)
        dgb = dg.astype(BF16)
        dc_ref[0] = (jnp.dot(dgb, bmat, preferred_element_type=F32)
                     + lax.dot_general(dyeb, hinb, NT_DIMS, preferred_element_type=F32))
        db_ref[0] = (lax.dot_general(dgb, cmat, TN_DIMS, preferred_element_type=F32)
                     + lax.dot_general(xdec.astype(BF16), dsb, NT_DIMS, preferred_element_type=F32))
        dx_ref[0] = dxd * dt
        ddt_ref[0] = da * anx_ref[...] + dxd * xv
        dan_ref[0, 0, 0] = jnp.sum(da * dt, axis=0, keepdims=True)

    G = SSD_GROUPS
    chunk = (lambda c: c) if reverse else (lambda c: nc - 1 - c)
    rev = lambda bi, gi, c: (bi, chunk(c), gi)
    return pl.pallas_call(
        body, name="ssd_bwd", grid=(b, G, nc),
        in_specs=[pl.BlockSpec((1, L, GW), rev),
                  pl.BlockSpec((1, L, GW), rev),
                  pl.BlockSpec((1, 1, SSD_GROUP_HEADS, L), lambda bi, gi, c: (bi, gi, 0, chunk(c))),
                  pl.BlockSpec((1, GW), lambda bi, gi, c: (0, gi)),
                  pl.BlockSpec((1, SSD_GROUP_HEADS, 1), lambda bi, gi, c: (gi, 0, 0)),
                  pl.BlockSpec((1, L, N), rev),
                  pl.BlockSpec((1, L, N), rev),
                  pl.BlockSpec((1, 1, 1, N, GW), lambda bi, gi, c: (bi, gi, chunk(c), 0, 0)),
                  pl.BlockSpec((1, L, GW), rev)],
        out_specs=[pl.BlockSpec((1, L, GW), rev),
                   pl.BlockSpec((1, L, GW), rev),
                   pl.BlockSpec((1, 1, 1, 1, GW), lambda bi, gi, c: (bi, gi, chunk(c), 0, 0)),
                   pl.BlockSpec((1, L, N), rev),
                   pl.BlockSpec((1, L, N), rev)],
        out_shape=[jax.ShapeDtypeStruct(x.shape, F32), jax.ShapeDtypeStruct(x.shape, F32),
                   jax.ShapeDtypeStruct((b, G, nc, 1, GW), F32),
                   jax.ShapeDtypeStruct(bm.shape, F32), jax.ShapeDtypeStruct(cm.shape, F32)],
        scratch_shapes=[pltpu.VMEM((N, GW), F32)],
        compiler_params=pltpu.CompilerParams(dimension_semantics=("parallel", "parallel", "arbitrary")),
    )(x, dtx, dtt, anx, anc, bm, cm, hs, dy)


@functools.partial(jax.custom_vjp, nondiff_argnums=(7,))
def _ssd_scan(x, dtx, dtt, anx, anc, bm, cm, reverse):
    return _ssd_fwd_call(x, dtx, dtt, anx, anc, bm, cm, reverse)[0]


def _ssd_scan_fwd(x, dtx, dtt, anx, anc, bm, cm, reverse):
    y, hs = _ssd_fwd_call(x, dtx, dtt, anx, anc, bm, cm, reverse)
    return y, (x, dtx, dtt, anx, anc, bm, cm, hs)


def _ssd_scan_bwd(reverse, res, dy):
    x, dtx, dtt, anx, anc, bm, cm, hs = res
    dx, ddtx, dan, db, dc = _ssd_bwd_call(x, dtx, dtt, anx, anc, bm, cm, hs, dy, reverse)
    b, g, nc, _, gw = dan.shape
    danx = jnp.sum(dan, axis=(0, 2, 3)).reshape(1, g * gw)
    return dx, ddtx, jnp.zeros_like(dtt), danx, jnp.zeros_like(anc), db, dc


_ssd_scan.defvjp(_ssd_scan_fwd, _ssd_scan_bwd)


def ssd_chunked(xs, dt, a_neg, bm, cm, reverse):
    b, s, _ = xs.shape
    dtx = jnp.repeat(dt, SSD_HEAD_DIM, axis=-1)
    dtt = jnp.transpose(dt, (0, 2, 1)).reshape(b, SSD_GROUPS, SSD_GROUP_HEADS, s)
    anx = jnp.repeat(a_neg, SSD_HEAD_DIM)[None, :]
    anc = a_neg.reshape(SSD_GROUPS, SSD_GROUP_HEADS, 1)
    return _ssd_scan(xs, dtx, dtt, anx, anc, bm, cm, reverse)


def _loss_call(y, t):
    rows, cols = y.shape
    tr = _row_tile(rows, cols * 4)

    def body(y_ref, t_ref, loss_ref, diff_ref):
        @pl.when(pl.program_id(0) == 0)
        def _():
            loss_ref[...] = jnp.zeros_like(loss_ref)

        d = y_ref[...] - t_ref[...]
        diff_ref[...] = d * (1.0 / cols)
        part = jnp.sum(jnp.sum(d * d, axis=1, keepdims=True), axis=0, keepdims=True)
        loss_ref[...] += part * (0.5 / cols)

    return pl.pallas_call(
        body, name="loss_head", grid=(rows // tr,),
        in_specs=[pl.BlockSpec((tr, cols), lambda i: (i, 0)), pl.BlockSpec((tr, cols), lambda i: (i, 0))],
        out_specs=[pl.BlockSpec((1, 1), lambda i: (0, 0)), pl.BlockSpec((tr, cols), lambda i: (i, 0))],
        out_shape=[jax.ShapeDtypeStruct((1, 1), F32), jax.ShapeDtypeStruct((rows, cols), F32)],
        compiler_params=pltpu.CompilerParams(dimension_semantics=("arbitrary",)),
    )(y, t)


@jax.custom_vjp
def loss_head(y, t):
    return _loss_call(y, t)[0][0, 0]


def _loss_head_fwd(y, t):
    loss, diff = _loss_call(y, t)
    return loss[0, 0], diff


def _loss_head_bwd(diff, g):
    return g * diff, jnp.zeros_like(diff)


loss_head.defvjp(_loss_head_fwd, _loss_head_bwd)


def _axial_rope_tables(seq_len, rot_dim):
    rows = seq_len // GRID_W
    row_idx = jnp.repeat(jnp.arange(rows), GRID_W).astype(F32)
    col_idx = jnp.tile(jnp.arange(GRID_W), rows).astype(F32)
    axis_dim = rot_dim // 2
    inv_freq = jnp.power(ROPE_THETA, -jnp.arange(0, axis_dim, 2, dtype=F32) / axis_dim)
    ang_r = row_idx[:, None] * inv_freq[None, :]
    ang_c = col_idx[:, None] * inv_freq[None, :]
    return jnp.cos(ang_r), jnp.sin(ang_r), jnp.cos(ang_c), jnp.sin(ang_c)


def _rotate(x, cos, sin):
    x1, x2 = jnp.split(x, 2, axis=-1)
    cos = cos[:, None, :]
    sin = sin[:, None, :]
    return jnp.concatenate([x1 * cos - x2 * sin, x1 * sin + x2 * cos], axis=-1)


def _apply_axial_rope(x, tables):
    cos_r, sin_r, cos_c, sin_c = tables
    x_row, x_col = jnp.split(x, 2, axis=-1)
    return jnp.concatenate([_rotate(x_row, cos_r, sin_r), _rotate(x_col, cos_c, sin_c)], axis=-1)


def _heads_first(t):
    return jnp.transpose(t, (0, 2, 1, 3))


def _gqa_group(q, k, v, q_norm_g, k_norm_g, rope, b, s):
    q = rms_norm(q, jnp.tile(q_norm_g, GQA_HEADS)[None, :], GQA_HEADS).reshape(b, s, GQA_HEADS, GQA_HEAD_DIM)
    k = rms_norm(k, jnp.tile(k_norm_g, GQA_KV_HEADS)[None, :], GQA_KV_HEADS).reshape(b, s, GQA_KV_HEADS, GQA_HEAD_DIM)
    v = v.reshape(b, s, GQA_KV_HEADS, GQA_HEAD_DIM)
    q = _apply_axial_rope(q, rope)
    k = _apply_axial_rope(k, rope)
    o = attention(_heads_first(q), _heads_first(k), _heads_first(v), GQA_HEAD_DIM ** -0.5)
    return _heads_first(o).reshape(b * s, GQA_WIDTH)


def _mla_group(c_q, c_kv, k_pe, q_norm_g, w_uq, kv_norm_g, w_ukv, rope, b, s):
    q = linear(rms_norm(c_q, q_norm_g[None, :], 1), w_uq).reshape(b, s, MLA_HEADS, MLA_NOPE_DIM + MLA_ROPE_DIM)
    q_nope, q_pe = q[..., :MLA_NOPE_DIM], q[..., MLA_NOPE_DIM:]
    kv = linear(rms_norm(c_kv, kv_norm_g[None, :], 1), w_ukv).reshape(b, s, MLA_HEADS, MLA_NOPE_DIM + MLA_V_DIM)
    k_nope, v = kv[..., :MLA_NOPE_DIM], kv[..., MLA_NOPE_DIM:]
    q_pe = _apply_axial_rope(q_pe, rope)
    k_pe = _apply_axial_rope(k_pe.reshape(b, s, 1, MLA_ROPE_DIM), rope)
    q = jnp.concatenate([q_nope, q_pe], axis=-1)
    k = jnp.concatenate([k_nope, jnp.broadcast_to(k_pe, (b, s, MLA_HEADS, MLA_ROPE_DIM))], axis=-1)
    o = attention(_heads_first(q), _heads_first(k), _heads_first(v), (MLA_NOPE_DIM + MLA_ROPE_DIM) ** -0.5)
    return _heads_first(o).reshape(b * s, MLA_HEADS * MLA_V_DIM)


def _ssd_group(z, xbc, dt_raw, conv_w, conv_b, dt_bias, a_log, d_skip, norm_g, b, s):
    wb = jnp.concatenate([conv_w, conv_b[None, :], jnp.zeros((CONV_PACK_ROWS - SSD_CONV - 1, SSD_CONV_DIM), F32)], axis=0)
    xbc = conv_silu(xbc.reshape(b, s, SSD_CONV_DIM), wb)
    xs = xbc[..., :SSD_INNER]
    bm = xbc[..., SSD_INNER:SSD_INNER + SSD_GROUPS * SSD_STATE]
    cm = xbc[..., SSD_INNER + SSD_GROUPS * SSD_STATE:]
    dt = jax.nn.softplus(dt_raw.reshape(b, s, 2, SSD_HEADS) + dt_bias)
    a_neg = -jnp.exp(a_log)
    y_fwd = ssd_chunked(xs, dt[:, :, 0], a_neg[0], bm, cm, False)
    y_bwd = ssd_chunked(xs, dt[:, :, 1], a_neg[1], bm, cm, True)
    y = y_fwd + y_bwd + xs * jnp.repeat(d_skip, SSD_HEAD_DIM)
    y = y.reshape(b * s, SSD_INNER) * jax.nn.silu(z)
    return rms_norm(y, norm_g[None, :], SSD_GROUPS)


def _modulate(n, scale, shift, b, s):
    d = n.shape[-1]
    return (n.reshape(b, s, d) * (1.0 + scale[:, None, :]) + shift[:, None, :]).reshape(b * s, d)


def _gated(t, gate, b, s):
    d = t.shape[-1]
    return (t.reshape(b, s, d) * gate[:, None, :]).reshape(b * s, d)


def _layer(x2, mod, w, rope_a, rope_b, b, s):
    shift1, scale1, gate1, shift2, scale2, gate2 = jnp.split(mod, 6, axis=-1)
    h = _modulate(rms_norm(x2, w["norm1_g"][None, :], 1), scale1, shift1, b, s)
    proj = linear(h, w["w_in"])
    idx = np.cumsum(IN_SPLITS).tolist()
    q_a, k_a, v_a, cq_b, ckv_b, kpe_b, z_c, xbc_c, dt_c = [proj[:, lo:hi] for lo, hi in zip([0] + idx[:-1], idx)]
    o_a = _gqa_group(q_a, k_a, v_a, w["q_norm_g"], w["k_norm_g"], rope_a, b, s)
    o_b = _mla_group(cq_b, ckv_b, kpe_b, w["mla_q_norm_g"], w["w_uq"], w["mla_kv_norm_g"], w["w_ukv"], rope_b, b, s)
    o_c = _ssd_group(z_c, xbc_c, dt_c, w["conv_w"], w["conv_b"], w["dt_bias"], w["a_log"], w["d_skip"],
                     w["ssd_norm_g"], b, s)
    mix = linear(jnp.concatenate([o_a, o_b, o_c], axis=-1), w["w_out"])
    x2 = x2 + _gated(mix, gate1, b, s)
    h = _modulate(rms_norm(x2, w["norm2_g"][None, :], 1), scale2, shift2, b, s)
    gu = linear(h, w["w_gate_up"])
    ffn = gu.shape[-1] // 2
    act = jax.nn.silu(gu[:, :ffn]) * gu[:, ffn:]
    return x2 + _gated(linear(act, w["w_down"]), gate2, b, s)


def _local_loss(diff_args, target):
    x, mod, layers, final_norm_g = diff_args
    b, s, d = x.shape
    rope_a = _axial_rope_tables(s, GQA_HEAD_DIM)
    rope_b = _axial_rope_tables(s, MLA_ROPE_DIM)
    x2 = x.reshape(b * s, d)
    for l, w in enumerate(layers):
        x2 = _layer(x2, mod[l], w, rope_a, rope_b, b, s)
    y = rms_norm(x2, final_norm_g[None, :], 1)
    return loss_head(y, target.reshape(b * s, d))


ANY = pl.BlockSpec(memory_space=pl.ANY)


def _flip_if(v, bit):
    return 1 - v if bit else v


def _all_gather_devices(x):
    def body(x_ref, out_ref, send_sems, recv_sems):
        mx, my, mc = lax.axis_index("x"), lax.axis_index("y"), lax.axis_index("c")
        me = 4 * mx + 2 * my + mc
        sends = []
        for k in range(1, N_DEV):
            peer = (_flip_if(mx, k & 4), _flip_if(my, k & 2), _flip_if(mc, k & 1))
            cp = pltpu.make_async_remote_copy(src_ref=x_ref, dst_ref=out_ref.at[me], send_sem=send_sems.at[k - 1],
                                              recv_sem=recv_sems.at[k - 1], device_id=peer, device_id_type=MESH)
            cp.start()
            sends.append(cp)
        for k in range(1, N_DEV):
            peer = (_flip_if(mx, k & 4), _flip_if(my, k & 2), _flip_if(mc, k & 1))
            src = 4 * peer[0] + 2 * peer[1] + peer[2]
            pltpu.make_async_remote_copy(src_ref=x_ref, dst_ref=out_ref.at[src], send_sem=send_sems.at[k - 1],
                                         recv_sem=recv_sems.at[k - 1], device_id=peer, device_id_type=MESH).wait_recv()
        for cp in sends:
            cp.wait_send()

    out = pl.pallas_call(
        body, name="all_gather_devices", in_specs=[ANY], out_specs=ANY,
        out_shape=jax.ShapeDtypeStruct((N_DEV,) + x.shape, x.dtype),
        scratch_shapes=[pltpu.SemaphoreType.DMA((N_DEV - 1,)), pltpu.SemaphoreType.DMA((N_DEV - 1,))],
    )(x)
    me = 4 * lax.axis_index("x") + 2 * lax.axis_index("y") + lax.axis_index("c")
    return lax.dynamic_update_index_in_dim(out, x, me, 0)


def _all_gather_chips(shards):
    n = len(shards)
    halves = [t.reshape(2, t.shape[0] // 2, t.shape[1]) for t in shards]

    def body(*refs):
        ins, outs = refs[:n], refs[n:2 * n]
        ici_send, ici_recv, d2d_send, d2d_recv = refs[2 * n:]
        mx, my, mc = lax.axis_index("x"), lax.axis_index("y"), lax.axis_index("c")
        me = 2 * mx + my
        sibling = (mx, my, 1 - mc)
        sends = []
        for k in range(1, N_CHIPS):
            peer = (_flip_if(mx, k & 2), _flip_if(my, k & 1), mc)
            for i in range(n):
                j = (k - 1) * n + i
                cp = pltpu.make_async_remote_copy(src_ref=ins[i].at[mc], dst_ref=outs[i].at[me, mc],
                                                  send_sem=ici_send.at[j], recv_sem=ici_recv.at[j],
                                                  device_id=peer, device_id_type=MESH)
                cp.start()
                sends.append(cp)
        for k in range(1, N_CHIPS):
            peer = (_flip_if(mx, k & 2), _flip_if(my, k & 1), mc)
            src = 2 * peer[0] + peer[1]
            for i in range(n):
                j = (k - 1) * n + i
                landed = outs[i].at[src, mc]
                pltpu.make_async_remote_copy(src_ref=ins[i].at[mc], dst_ref=landed, send_sem=ici_send.at[j],
                                             recv_sem=ici_recv.at[j], device_id=peer, device_id_type=MESH).wait_recv()
                cp = pltpu.make_async_remote_copy(src_ref=landed, dst_ref=landed, send_sem=d2d_send.at[j],
                                                  recv_sem=d2d_recv.at[j], device_id=sibling, device_id_type=MESH)
                cp.start()
                sends.append(cp)
        for k in range(1, N_CHIPS):
            src = 2 * _flip_if(mx, k & 2) + _flip_if(my, k & 1)
            for i in range(n):
                j = (k - 1) * n + i
                theirs = outs[i].at[src, 1 - mc]
                pltpu.make_async_remote_copy(src_ref=theirs, dst_ref=theirs, send_sem=d2d_send.at[j],
                                             recv_sem=d2d_recv.at[j], device_id=sibling, device_id_type=MESH).wait_recv()
        for cp in sends:
            cp.wait_send()

    count = (N_CHIPS - 1) * n
    outs = pl.pallas_call(
        body, name="all_gather_chips", in_specs=[ANY] * n, out_specs=[ANY] * n,
        out_shape=[jax.ShapeDtypeStruct((N_CHIPS,) + t.shape, t.dtype) for t in halves],
        scratch_shapes=[pltpu.SemaphoreType.DMA((count,))] * 4,
    )(*halves)
    chip = 2 * lax.axis_index("x") + lax.axis_index("y")
    return [lax.dynamic_update_index_in_dim(o.reshape((N_CHIPS,) + t.shape), t, chip, 0) for o, t in zip(outs, shards)]


def _sibling_exchange(blocks, name):
    n = len(blocks)

    def body(*refs):
        ins, outs = refs[:n], refs[n:2 * n]
        send_sems, recv_sems = refs[2 * n:]
        mx, my, mc = lax.axis_index("x"), lax.axis_index("y"), lax.axis_index("c")
        cps = []
        for i in range(n):
            cp = pltpu.make_async_remote_copy(src_ref=ins[i], dst_ref=outs[i], send_sem=send_sems.at[i],
                                              recv_sem=recv_sems.at[i], device_id=(mx, my, 1 - mc),
                                              device_id_type=MESH)
            cp.start()
            cps.append(cp)
        for cp in cps:
            cp.wait()

    return pl.pallas_call(
        body, name=name, in_specs=[ANY] * n, out_specs=[ANY] * n,
        out_shape=[jax.ShapeDtypeStruct(t.shape, t.dtype) for t in blocks],
        scratch_shapes=[pltpu.SemaphoreType.DMA((n,)), pltpu.SemaphoreType.DMA((n,))],
    )(*blocks)


def _add_halves(own, recv):
    nb, r, c = own.shape
    tr = _row_tile(r, c * 4)

    def body(g_ref, r_ref, o_ref):
        o_ref[...] = (g_ref[...] + r_ref[...].astype(F32)).astype(BF16)

    spec = pl.BlockSpec((1, tr, c), lambda k, i: (k, i, 0))
    return pl.pallas_call(
        body, name="rs_add_halves", grid=(nb, r // tr), in_specs=[spec, spec], out_specs=spec,
        out_shape=jax.ShapeDtypeStruct((nb, r, c), BF16),
        compiler_params=pltpu.CompilerParams(dimension_semantics=("parallel", "parallel")),
    )(own, recv)


def _chip_exchange(parts):
    n = len(parts)

    def body(*refs):
        ins, outs = refs[:n], refs[n:2 * n]
        send_sems, recv_sems = refs[2 * n:]
        mx, my, mc = lax.axis_index("x"), lax.axis_index("y"), lax.axis_index("c")
        sends = []
        for k in range(1, N_CHIPS):
            peer = (_flip_if(mx, k & 2), _flip_if(my, k & 1), mc)
            dst_chip = 2 * peer[0] + peer[1]
            for i in range(n):
                j = (k - 1) * n + i
                cp = pltpu.make_async_remote_copy(src_ref=ins[i].at[dst_chip], dst_ref=outs[i].at[k - 1],
                                                  send_sem=send_sems.at[j], recv_sem=recv_sems.at[j],
                                                  device_id=peer, device_id_type=MESH)
                cp.start()
                sends.append(cp)
        for cp in sends:
            cp.wait()

    count = (N_CHIPS - 1) * n
    return pl.pallas_call(
        body, name="rs_chip_exchange", in_specs=[ANY] * n, out_specs=[ANY] * n,
        out_shape=[jax.ShapeDtypeStruct((N_CHIPS - 1,) + t.shape[1:], t.dtype) for t in parts],
        scratch_shapes=[pltpu.SemaphoreType.DMA((count,)), pltpu.SemaphoreType.DMA((count,))],
    )(*parts)


def _sum_chips(parts, recv, chip):
    _, r, c = parts.shape
    tr = _row_tile(r, c * 4)

    def body(chip_ref, p_ref, r_ref, o_ref):
        acc = p_ref[0].astype(F32)
        for k in range(N_CHIPS - 1):
            acc = acc + r_ref[k].astype(F32)
        o_ref[...] = acc

    return pl.pallas_call(
        body, name="rs_sum_chips",
        grid_spec=pltpu.PrefetchScalarGridSpec(
            num_scalar_prefetch=1, grid=(r // tr,),
            in_specs=[pl.BlockSpec((1, tr, c), lambda i, chip_ref: (chip_ref[0], i, 0)),
                      pl.BlockSpec((N_CHIPS - 1, tr, c), lambda i, chip_ref: (0, i, 0))],
            out_specs=pl.BlockSpec((tr, c), lambda i, chip_ref: (i, 0))),
        out_shape=jax.ShapeDtypeStruct((r, c), F32),
        compiler_params=pltpu.CompilerParams(dimension_semantics=("parallel",)),
    )(chip, parts, recv)


def _sum_leading(t, name):
    nb, r, c = t.shape
    tr = _row_tile(r, c * 4 * nb)

    def body(t_ref, o_ref):
        acc = t_ref[0]
        for k in range(1, nb):
            acc = acc + t_ref[k]
        o_ref[...] = acc

    return pl.pallas_call(
        body, name=name, grid=(r // tr,),
        in_specs=[pl.BlockSpec((nb, tr, c), lambda i: (0, i, 0))],
        out_specs=pl.BlockSpec((tr, c), lambda i: (i, 0)),
        out_shape=jax.ShapeDtypeStruct((r, c), F32),
        compiler_params=pltpu.CompilerParams(dimension_semantics=("parallel",)),
    )(t)


def _reduce_scatter(grads):
    mc = lax.axis_index("c")
    chip = (2 * lax.axis_index("x") + lax.axis_index("y")).astype(jnp.int32).reshape(1)
    split = [g.reshape(g.shape[0], 2, g.shape[1] // 2, g.shape[2]) for g in grads]
    own = [lax.dynamic_index_in_dim(g, mc, axis=1, keepdims=False) for g in split]
    away = [lax.dynamic_index_in_dim(g, 1 - mc, axis=1, keepdims=False).astype(BF16) for g in split]
    recv = _sibling_exchange(away, "rs_sibling_exchange")
    parts = [_add_halves(o, r) for o, r in zip(own, recv)]
    gathered = _chip_exchange(parts)
    mine = [_sum_chips(p, t, chip) for p, t in zip(parts, gathered)]
    theirs = _sibling_exchange(mine, "rs_sibling_swap")
    low = [jnp.where(mc == 0, a, b) for a, b in zip(mine, theirs)]
    high = [jnp.where(mc == 0, b, a) for a, b in zip(mine, theirs)]
    return [jnp.concatenate([lo, hi], axis=0) for lo, hi in zip(low, high)]


def _adamw(w, g, m, v):
    shape = w.shape
    cols = shape[-1]
    rows = int(np.prod(shape[:-1])) if len(shape) > 1 else 1
    w2, g2, m2, v2 = [t.reshape(rows, cols) for t in (w, g, m, v)]
    tr = _row_tile(rows, cols * 4 * 4)

    def body(w_ref, g_ref, m_ref, v_ref, d_ref, mo_ref, vo_ref):
        gv = g_ref[...]
        mn = ADAM_B1 * m_ref[...] + (1.0 - ADAM_B1) * gv
        vn = ADAM_B2 * v_ref[...] + (1.0 - ADAM_B2) * (gv * gv)
        m_hat = mn / (1.0 - ADAM_B1 ** ADAM_STEP)
        v_hat = vn / (1.0 - ADAM_B2 ** ADAM_STEP)
        d_ref[...] = -ADAM_LR * (m_hat / (jnp.sqrt(v_hat) + ADAM_EPS) + ADAM_WD * w_ref[...])
        mo_ref[...] = mn
        vo_ref[...] = vn

    spec = pl.BlockSpec((tr, cols), lambda i: (i, 0))
    outs = pl.pallas_call(
        body, name="adamw", grid=(rows // tr,), in_specs=[spec] * 4, out_specs=[spec] * 3,
        out_shape=[jax.ShapeDtypeStruct((rows, cols), F32)] * 3,
        compiler_params=pltpu.CompilerParams(dimension_semantics=("parallel",)),
    )(w2, g2, m2, v2)
    return [t.reshape(shape) for t in outs]


WEIGHTS = ['w_ada', 'b_ada', 'norm1_g', 'norm2_g', 'w_in', 'q_norm_g', 'k_norm_g', 'mla_q_norm_g', 'w_uq',
           'mla_kv_norm_g', 'w_ukv', 'conv_w', 'conv_b', 'dt_bias', 'a_log', 'd_skip', 'ssd_norm_g', 'w_out',
           'w_gate_up', 'w_down', 'final_norm_g']
COL_SHARDED = ('w_in', 'w_uq', 'w_ukv', 'w_gate_up')
ROW_SHARDED = ('w_out', 'w_down')
SMALL_LAYER = ('norm1_g', 'norm2_g', 'q_norm_g', 'k_norm_g', 'mla_q_norm_g', 'mla_kv_norm_g', 'conv_w', 'conv_b',
               'dt_bias', 'a_log', 'd_skip', 'ssd_norm_g')


def _pack(parts):
    flat = jnp.concatenate([p.reshape(-1) for p in parts])
    n = flat.shape[0]
    rows = -(-n // (8 * LANES)) * 8
    return jnp.pad(flat, (0, rows * LANES - n)).reshape(rows, LANES)


def _unpack(flat, shapes):
    out, pos = [], 0
    for shp in shapes:
        size = int(np.prod(shp))
        out.append(flat[pos:pos + size].reshape(shp))
        pos += size
    return out


def _cols_full(gathered):
    k, r, c = gathered.shape
    return jnp.transpose(gathered, (1, 0, 2)).reshape(r, k * c)


def _cols_split(full):
    r, c4 = full.shape
    return jnp.transpose(full.reshape(r, N_CHIPS, c4 // N_CHIPS), (1, 0, 2))


def kernel(x, c, w_ada, b_ada, norm1_g, norm2_g, w_in, q_norm_g, k_norm_g, mla_q_norm_g, w_uq, mla_kv_norm_g, w_ukv, conv_w, conv_b, dt_bias, a_log, d_skip, ssd_norm_g, w_out, w_gate_up, w_down, final_norm_g, loss_target, m_w_ada, m_b_ada, m_norm1_g, m_norm2_g, m_w_in, m_q_norm_g, m_k_norm_g, m_mla_q_norm_g, m_w_uq, m_mla_kv_norm_g, m_w_ukv, m_conv_w, m_conv_b, m_dt_bias, m_a_log, m_d_skip, m_ssd_norm_g, m_w_out, m_w_gate_up, m_w_down, m_final_norm_g, v_w_ada, v_b_ada, v_norm1_g, v_norm2_g, v_w_in, v_q_norm_g, v_k_norm_g, v_mla_q_norm_g, v_w_uq, v_mla_kv_norm_g, v_w_ukv, v_conv_w, v_conv_b, v_dt_bias, v_a_log, v_d_skip, v_ssd_norm_g, v_w_out, v_w_gate_up, v_w_down, v_final_norm_g):
    args = dict(locals())
    weights = {n: args[n] for n in WEIGHTS}
    depth = w_in.shape[0]
    bl, s, d = x.shape
    mx, my, mc = lax.axis_index("x"), lax.axis_index("y"), lax.axis_index("c")
    chip = 2 * mx + my
    dev = 2 * chip + mc
    ada_cols = w_ada.shape[-1]
    conv_cols = conv_w.shape[-1]

    first_shapes = [c.shape, conv_w.shape]
    first = _all_gather_devices(_pack([c, conv_w]))
    first = [_unpack(first[i].reshape(-1), first_shapes) for i in range(N_DEV)]
    c_act = jax.nn.silu(jnp.concatenate([f[0] for f in first], axis=0))
    conv_w_full = jnp.concatenate([first[2 * k][1] for k in range(N_CHIPS)], axis=-1)

    b_cols = lax.dynamic_slice_in_dim(b_ada, chip * ada_cols, ada_cols, axis=1)
    c_act_b = c_act.astype(BF16)
    mod_cols = jnp.stack([_matmul(c_act_b, w_ada[l].astype(BF16), name="ada_fwd") + b_cols[l][None, :]
                          for l in range(depth)])
    mod_all = _all_gather_devices(mod_cols.reshape(depth * N_DEV * bl, ada_cols))
    mod_all = mod_all.reshape(N_DEV, depth, N_DEV, bl, ada_cols)
    mod_mine = lax.dynamic_index_in_dim(mod_all, dev, axis=2, keepdims=False)
    mod = jnp.concatenate([mod_mine[2 * k] for k in range(N_CHIPS)], axis=-1)

    big = COL_SHARDED + ROW_SHARDED
    gathered = {}
    for l in range(depth):
        outs = _all_gather_chips([weights[n][l].astype(BF16) for n in big])
        for n, g in zip(big, outs):
            gathered[(n, l)] = g
    layers = []
    for l in range(depth):
        w = {n: weights[n][l] for n in SMALL_LAYER if n != 'conv_w'}
        w['conv_w'] = conv_w_full[l]
        for n in COL_SHARDED:
            full = _cols_full(gathered[(n, l)])
            if n == 'w_in':
                full = jnp.pad(full, ((0, 0), (0, IN_COLS_PAD - IN_COLS)))
            w[n] = full.astype(F32)
        for n in ROW_SHARDED:
            g = gathered[(n, l)]
            w[n] = g.reshape(g.shape[0] * g.shape[1], g.shape[2]).astype(F32)
        layers.append(w)

    loss_local, (gx, gmod, glayers, gfinal) = jax.value_and_grad(_local_loss)(
        (x, mod, layers, final_norm_g), loss_target)

    grads = {}
    for l in reversed(range(depth)):
        blocks = []
        for n in COL_SHARDED:
            g = glayers[l][n]
            if n == 'w_in':
                g = g[:, :IN_COLS]
            blocks.append(_cols_split(g))
        for n in ROW_SHARDED:
            g = glayers[l][n]
            blocks.append(g.reshape(N_CHIPS, g.shape[0] // N_CHIPS, g.shape[1]))
        for n, g in zip(big, _reduce_scatter(blocks)):
            grads[(n, l)] = g

    small_parts = [jnp.stack([glayers[l][n] for l in range(depth)]) for n in SMALL_LAYER]
    small_parts += [gfinal, loss_local.reshape(1), gmod]
    small_shapes = [p.shape for p in small_parts]
    last = _all_gather_devices(_pack(small_parts))
    summed = _unpack(_sum_leading(last, "sum_devices").reshape(-1), small_shapes)
    small = dict(zip(SMALL_LAYER, summed[:len(SMALL_LAYER)]))
    g_final, loss, gmod_sum = summed[len(SMALL_LAYER):]
    small['conv_w'] = lax.dynamic_slice_in_dim(small['conv_w'], chip * conv_cols, conv_cols, axis=2)
    gmod_all = jnp.stack([_unpack(last[i].reshape(-1), small_shapes)[-1] for i in range(N_DEV)], axis=1)
    gmod_all = gmod_all.reshape(depth, N_DEV * bl, gmod.shape[-1])
    gmod_cols = lax.dynamic_slice_in_dim(gmod_all, chip * ada_cols, ada_cols, axis=2)
    g_w_ada = jnp.stack([_matmul(c_act_b, gmod_cols[l].astype(BF16), ta=True, name="ada_dw") for l in range(depth)])
    g_b_ada = gmod_sum[:, 0]
    for i in range(1, bl):
        g_b_ada = g_b_ada + gmod_sum[:, i]

    grad = {'w_ada': g_w_ada, 'b_ada': g_b_ada, 'final_norm_g': g_final}
    for n in SMALL_LAYER:
        grad[n] = small[n]
    for n in big:
        grad[n] = jnp.stack([grads[(n, l)] for l in range(depth)])

    delta, new_m, new_v = {}, {}, {}
    for n in WEIGHTS:
        delta[n], new_m[n], new_v[n] = _adamw(weights[n], grad[n], args["m_" + n], args["v_" + n])
    return (loss.reshape(()), gx, *[grad[n] for n in WEIGHTS], *[delta[n] for n in WEIGHTS],
            *[new_m[n] for n in WEIGHTS], *[new_v[n] for n in WEIGHTS])
```

```python
import functools

import numpy as np
import jax
import jax.numpy as jnp
from jax import lax
from jax.experimental import pallas as pl
from jax.experimental.pallas import tpu as pltpu

F32 = jnp.float32
BF16 = jnp.bfloat16
HIGHEST = lax.Precision.HIGHEST
MESH = pl.DeviceIdType.MESH

GRID_W = 64
ROPE_THETA = 10000.0
EPS = 1e-6

GQA_HEADS, GQA_KV_HEADS, GQA_HEAD_DIM = 6, 2, 128
GQA_WIDTH = GQA_HEADS * GQA_HEAD_DIM
GQA_KV_WIDTH = GQA_KV_HEADS * GQA_HEAD_DIM
MLA_HEADS, MLA_Q_LORA, MLA_KV_LORA = 4, 512, 256
MLA_NOPE_DIM, MLA_ROPE_DIM, MLA_V_DIM = 128, 64, 128
SSD_HEADS, SSD_HEAD_DIM, SSD_GROUPS, SSD_STATE, SSD_CONV, SSD_CHUNK = 12, 64, 2, 128, 5, 128
SSD_INNER = SSD_HEADS * SSD_HEAD_DIM
SSD_CONV_DIM = SSD_INNER + 2 * SSD_GROUPS * SSD_STATE
SSD_GROUP_HEADS = SSD_HEADS // SSD_GROUPS
SSD_GROUP_WIDTH = SSD_GROUP_HEADS * SSD_HEAD_DIM
IN_SPLITS = (GQA_WIDTH, GQA_KV_WIDTH, GQA_KV_WIDTH, MLA_Q_LORA, MLA_KV_LORA, MLA_ROPE_DIM, SSD_INNER, SSD_CONV_DIM,
             2 * SSD_HEADS)
IN_COLS = sum(IN_SPLITS)
LANES = 128
IN_COLS_PAD = -(-IN_COLS // LANES) * LANES

ADAM_LR, ADAM_B1, ADAM_B2, ADAM_EPS, ADAM_WD, ADAM_STEP = 0.001, 0.9, 0.999, 1e-08, 0.01, 10

N_CHIPS = 4
N_DEV = 8
TILE_BYTES = 2 * 1024 * 1024


def _pick(n, cands):
    for t in cands:
        if n % t == 0:
            return t
    return n


def _row_tile(rows, row_bytes):
    for t in (2048, 1024, 512, 256, 128, 64, 32, 16, 8):
        if rows % t == 0 and t * row_bytes <= TILE_BYTES:
            return t
    return rows


MM_VMEM_BUDGET = 36 * 1024 * 1024
MM_VMEM_LIMIT = 56 * 1024 * 1024
MM_MAX_TILE = 2048
MM_MAX_K_TILE = 4096
MXU_DIM = 256
HBM_BYTES_PER_US = 3.0e6
MXU_FLOPS_PER_US = 9.0e8
STEP_US = 0.35


def _tile_cands(d, cap):
    if d % LANES:
        return [d]
    return [t for t in range(LANES, min(d, cap) + 1, LANES) if d % t == 0] or [d]


def _mm_tiles(m, n, kdim):
    up = lambda t: -(-t // MXU_DIM) * MXU_DIM
    best = None
    for tm in _tile_cands(m, MM_MAX_TILE):
        for tn in _tile_cands(n, MM_MAX_TILE):
            for tk in _tile_cands(kdim, MM_MAX_K_TILE):
                if 2 * (tm * tk * 2 + tk * tn * 2 + tm * tn * 4) > MM_VMEM_BUDGET:
                    continue
                ni, nj, nk = m // tm, n // tn, kdim // tk
                a_reads = 1 if nk == 1 else nj
                b_reads = 1 if (nk == 1 and nj == 1) else ni
                hbm = (m * kdim * 2 * a_reads + kdim * n * 2 * b_reads + m * n * 4) / HBM_BYTES_PER_US
                mxu = ni * nj * nk * 2.0 * max(tm, 8) * up(tn) * up(tk) / MXU_FLOPS_PER_US
                cost = max(hbm, mxu) + 0.25 * min(hbm, mxu) + ni * nj * nk * STEP_US
                if best is None or cost < best[0]:
                    best = (cost, tm, tn, tk)
    return best[1:]


def _matmul(a, b, ta=False, tb=False, name="mm"):
    assert a.dtype == BF16 and b.dtype == BF16, (a.dtype, b.dtype)
    if ta:
        kdim, m = a.shape
    else:
        m, kdim = a.shape
    if tb:
        n, k2 = b.shape
    else:
        k2, n = b.shape
    assert kdim == k2, (a.shape, b.shape, ta, tb)
    tm, tn, tk = _mm_tiles(m, n, kdim)
    nk = kdim // tk
    dn = (((0 if ta else 1,), (1 if tb else 0,)), ((), ()))

    def body(a_ref, b_ref, o_ref):
        part = lax.dot_general(a_ref[...], b_ref[...], dn, preferred_element_type=F32)
        if nk == 1:
            o_ref[...] = part
        else:
            k = pl.program_id(2)

            @pl.when(k == 0)
            def _():
                o_ref[...] = part

            @pl.when(k > 0)
            def _():
                o_ref[...] += part

    a_spec = pl.BlockSpec((tk, tm), lambda i, j, k: (k, i)) if ta else pl.BlockSpec((tm, tk), lambda i, j, k: (i, k))
    b_spec = pl.BlockSpec((tn, tk), lambda i, j, k: (j, k)) if tb else pl.BlockSpec((tk, tn), lambda i, j, k: (k, j))
    return pl.pallas_call(
        body, name=name, grid=(m // tm, n // tn, nk),
        in_specs=[a_spec, b_spec], out_specs=pl.BlockSpec((tm, tn), lambda i, j, k: (i, j)),
        out_shape=jax.ShapeDtypeStruct((m, n), F32),
        compiler_params=pltpu.CompilerParams(dimension_semantics=("parallel", "arbitrary", "arbitrary"),
                                             vmem_limit_bytes=MM_VMEM_LIMIT),
    )(a, b)


@jax.custom_vjp
def linear(x, w):
    return _matmul(x.astype(BF16), w.astype(BF16), name="linear_fwd")


def _linear_fwd(x, w):
    xb, wb = x.astype(BF16), w.astype(BF16)
    return _matmul(xb, wb, name="linear_fwd"), (xb, wb)


def _linear_bwd(res, dy):
    xb, wb = res
    dyb = dy.astype(BF16)
    return _matmul(dyb, wb, tb=True, name="linear_dx"), _matmul(xb, dyb, ta=True, name="linear_dw")


linear.defvjp(_linear_fwd, _linear_bwd)


def _rms_fwd_call(x, g, groups):
    rows, cols = x.shape
    d = cols // groups
    tr = _row_tile(rows, cols * 4)

    def body(x_ref, g_ref, y_ref):
        for gi in range(groups):
            sl = slice(gi * d, (gi + 1) * d)
            xs = x_ref[:, sl]
            r = lax.rsqrt(jnp.mean(xs * xs, axis=-1, keepdims=True) + EPS)
            y_ref[:, sl] = xs * r * g_ref[:, sl]

    return pl.pallas_call(
        body, name="rms_fwd", grid=(rows // tr,),
        in_specs=[pl.BlockSpec((tr, cols), lambda i: (i, 0)), pl.BlockSpec((1, cols), lambda i: (0, 0))],
        out_specs=pl.BlockSpec((tr, cols), lambda i: (i, 0)),
        out_shape=jax.ShapeDtypeStruct((rows, cols), F32),
        compiler_params=pltpu.CompilerParams(dimension_semantics=("parallel",)),
    )(x, g)


def _rms_bwd_call(x, g, dy, groups):
    rows, cols = x.shape
    d = cols // groups
    tr = _row_tile(rows, cols * 4)

    def body(x_ref, g_ref, dy_ref, dx_ref, dg_ref):
        @pl.when(pl.program_id(0) == 0)
        def _():
            dg_ref[...] = jnp.zeros_like(dg_ref)

        for gi in range(groups):
            sl = slice(gi * d, (gi + 1) * d)
            xs = x_ref[:, sl]
            dys = dy_ref[:, sl]
            r = lax.rsqrt(jnp.mean(xs * xs, axis=-1, keepdims=True) + EPS)
            xhat = xs * r
            dg_ref[:, sl] += jnp.sum(dys * xhat, axis=0, keepdims=True)
            dxhat = dys * g_ref[:, sl]
            dx_ref[:, sl] = r * (dxhat - xhat * jnp.mean(dxhat * xhat, axis=-1, keepdims=True))

    return pl.pallas_call(
        body, name="rms_bwd", grid=(rows // tr,),
        in_specs=[pl.BlockSpec((tr, cols), lambda i: (i, 0)), pl.BlockSpec((1, cols), lambda i: (0, 0)),
                  pl.BlockSpec((tr, cols), lambda i: (i, 0))],
        out_specs=[pl.BlockSpec((tr, cols), lambda i: (i, 0)), pl.BlockSpec((1, cols), lambda i: (0, 0))],
        out_shape=[jax.ShapeDtypeStruct((rows, cols), F32), jax.ShapeDtypeStruct((1, cols), F32)],
        compiler_params=pltpu.CompilerParams(dimension_semantics=("arbitrary",)),
    )(x, g, dy)


@functools.partial(jax.custom_vjp, nondiff_argnums=(2,))
def rms_norm(x, g, groups):
    return _rms_fwd_call(x, g, groups)


def _rms_norm_fwd(x, g, groups):
    return _rms_fwd_call(x, g, groups), (x, g)


def _rms_norm_bwd(groups, res, dy):
    x, g = res
    dx, dg = _rms_bwd_call(x, g, dy, groups)
    return dx, dg


rms_norm.defvjp(_rms_norm_fwd, _rms_norm_bwd)


NT_DIMS = (((1,), (1,)), ((), ()))
TN_DIMS = (((0,), (0,)), ((), ()))


def _softmax_rows(q, k, scale):
    s = lax.dot_general(q, k, NT_DIMS, preferred_element_type=F32) * scale
    p = jnp.exp(s - jnp.max(s, axis=-1, keepdims=True))
    return p / jnp.sum(p, axis=-1, keepdims=True)


def _attn_fwd_call(q, k, v, scale):
    b, h, s, dk = q.shape
    hkv, dv = k.shape[1], v.shape[3]
    rep = h // hkv
    tq = _pick(s, (256, 128))

    def body(q_ref, k_ref, v_ref, o_ref):
        p = _softmax_rows(q_ref[0, 0], k_ref[0, 0], scale)
        o_ref[0, 0] = jnp.dot(p.astype(BF16), v_ref[0, 0], preferred_element_type=F32)

    return pl.pallas_call(
        body, name="attn_fwd", grid=(b, h, s // tq),
        in_specs=[pl.BlockSpec((1, 1, tq, dk), lambda bi, hi, qi: (bi, hi, qi, 0)),
                  pl.BlockSpec((1, 1, s, dk), lambda bi, hi, qi: (bi, hi // rep, 0, 0)),
                  pl.BlockSpec((1, 1, s, dv), lambda bi, hi, qi: (bi, hi // rep, 0, 0))],
        out_specs=pl.BlockSpec((1, 1, tq, dv), lambda bi, hi, qi: (bi, hi, qi, 0)),
        out_shape=jax.ShapeDtypeStruct((b, h, s, dv), F32),
        compiler_params=pltpu.CompilerParams(dimension_semantics=("parallel", "parallel", "parallel")),
    )(q, k, v)


def _attn_bwd_call(q, k, v, do, scale):
    b, h, s, dk = q.shape
    hkv, dv = k.shape[1], v.shape[3]
    rep = h // hkv
    tq = _pick(s, (256, 128))

    def body(q_ref, k_ref, v_ref, do_ref, dq_ref, dk_ref, dv_ref):
        @pl.when((pl.program_id(2) == 0) & (pl.program_id(3) == 0))
        def _():
            dk_ref[...] = jnp.zeros_like(dk_ref)
            dv_ref[...] = jnp.zeros_like(dv_ref)

        qb = q_ref[0, 0]
        kb = k_ref[0, 0]
        vb = v_ref[0, 0]
        dob = do_ref[0, 0]
        p = _softmax_rows(qb, kb, scale)
        dp = lax.dot_general(dob, vb, NT_DIMS, preferred_element_type=F32)
        ds = (p * (dp - jnp.sum(p * dp, axis=-1, keepdims=True)) * scale).astype(BF16)
        dq_ref[0, 0] = jnp.dot(ds, kb, preferred_element_type=F32)
        dk_ref[0, 0] += lax.dot_general(ds, qb, TN_DIMS, preferred_element_type=F32)
        dv_ref[0, 0] += lax.dot_general(p.astype(BF16), dob, TN_DIMS, preferred_element_type=F32)

    return pl.pallas_call(
        body, name="attn_bwd", grid=(b, hkv, rep, s // tq),
        in_specs=[pl.BlockSpec((1, 1, tq, dk), lambda bi, gi, ri, qi: (bi, gi * rep + ri, qi, 0)),
                  pl.BlockSpec((1, 1, s, dk), lambda bi, gi, ri, qi: (bi, gi, 0, 0)),
                  pl.BlockSpec((1, 1, s, dv), lambda bi, gi, ri, qi: (bi, gi, 0, 0)),
                  pl.BlockSpec((1, 1, tq, dv), lambda bi, gi, ri, qi: (bi, gi * rep + ri, qi, 0))],
        out_specs=[pl.BlockSpec((1, 1, tq, dk), lambda bi, gi, ri, qi: (bi, gi * rep + ri, qi, 0)),
                   pl.BlockSpec((1, 1, s, dk), lambda bi, gi, ri, qi: (bi, gi, 0, 0)),
                   pl.BlockSpec((1, 1, s, dv), lambda bi, gi, ri, qi: (bi, gi, 0, 0))],
        out_shape=[jax.ShapeDtypeStruct(q.shape, F32), jax.ShapeDtypeStruct(k.shape, F32),
                   jax.ShapeDtypeStruct(v.shape, F32)],
        compiler_params=pltpu.CompilerParams(
            dimension_semantics=("parallel", "parallel", "arbitrary", "arbitrary")),
    )(q, k, v, do)


@functools.partial(jax.custom_vjp, nondiff_argnums=(3,))
def attention(q, k, v, scale):
    return _attn_fwd_call(q.astype(BF16), k.astype(BF16), v.astype(BF16), scale)


def _attention_fwd(q, k, v, scale):
    qb, kb, vb = q.astype(BF16), k.astype(BF16), v.astype(BF16)
    return _attn_fwd_call(qb, kb, vb, scale), (qb, kb, vb)


def _attention_bwd(scale, res, do):
    qb, kb, vb = res
    return tuple(_attn_bwd_call(qb, kb, vb, do.astype(BF16), scale))


attention.defvjp(_attention_fwd, _attention_bwd)


CONV_COL_TILE = 256
CONV_PACK_ROWS = 8


def _shifted(x, off, rows):
    if off == 0:
        return x
    s = x.shape[0]
    rolled = pltpu.roll(x, (-off) % s, 0)
    valid = (rows + off >= 0) & (rows + off < s)
    return jnp.where(valid, rolled, 0.0)


def _conv_pre(x, wb_ref, rows):
    z = jnp.zeros_like(x) + wb_ref[SSD_CONV:SSD_CONV + 1, :]
    for j in range(SSD_CONV):
        z = z + wb_ref[j:j + 1, :] * _shifted(x, j - SSD_CONV // 2, rows)
    return z


def _conv_fwd_call(x, wb):
    b, s, c = x.shape
    tc = _pick(c, (CONV_COL_TILE, LANES))

    def body(x_ref, wb_ref, y_ref):
        xv = x_ref[0]
        rows = lax.broadcasted_iota(jnp.int32, xv.shape, 0)
        z = _conv_pre(xv, wb_ref, rows)
        y_ref[0] = z * jax.nn.sigmoid(z)

    return pl.pallas_call(
        body, name="conv_fwd", grid=(b, c // tc),
        in_specs=[pl.BlockSpec((1, s, tc), lambda bi, ci: (bi, 0, ci)),
                  pl.BlockSpec((CONV_PACK_ROWS, tc), lambda bi, ci: (0, ci))],
        out_specs=pl.BlockSpec((1, s, tc), lambda bi, ci: (bi, 0, ci)),
        out_shape=jax.ShapeDtypeStruct(x.shape, F32),
        compiler_params=pltpu.CompilerParams(dimension_semantics=("parallel", "parallel")),
    )(x, wb)


def _conv_bwd_call(x, wb, dy):
    b, s, c = x.shape
    tc = _pick(c, (CONV_COL_TILE, LANES))

    def body(x_ref, wb_ref, dy_ref, dx_ref, dwb_ref):
        xv = x_ref[0]
        rows = lax.broadcasted_iota(jnp.int32, xv.shape, 0)
        z = _conv_pre(xv, wb_ref, rows)
        sg = jax.nn.sigmoid(z)
        dz = dy_ref[0] * (sg * (1.0 + z * (1.0 - sg)))
        dx = jnp.zeros_like(xv)
        for j in range(SSD_CONV):
            off = j - SSD_CONV // 2
            dx = dx + wb_ref[j:j + 1, :] * _shifted(dz, -off, rows)
            dwb_ref[0, j:j + 1, :] = jnp.sum(dz * _shifted(xv, off, rows), axis=0, keepdims=True)
        dx_ref[0] = dx
        dwb_ref[0, SSD_CONV:SSD_CONV + 1, :] = jnp.sum(dz, axis=0, keepdims=True)
        dwb_ref[0, SSD_CONV + 1:, :] = jnp.zeros((CONV_PACK_ROWS - SSD_CONV - 1, dz.shape[1]), F32)

    return pl.pallas_call(
        body, name="conv_bwd", grid=(b, c // tc),
        in_specs=[pl.BlockSpec((1, s, tc), lambda bi, ci: (bi, 0, ci)),
                  pl.BlockSpec((CONV_PACK_ROWS, tc), lambda bi, ci: (0, ci)),
                  pl.BlockSpec((1, s, tc), lambda bi, ci: (bi, 0, ci))],
        out_specs=[pl.BlockSpec((1, s, tc), lambda bi, ci: (bi, 0, ci)),
                   pl.BlockSpec((1, CONV_PACK_ROWS, tc), lambda bi, ci: (bi, 0, ci))],
        out_shape=[jax.ShapeDtypeStruct(x.shape, F32), jax.ShapeDtypeStruct((b, CONV_PACK_ROWS, c), F32)],
        compiler_params=pltpu.CompilerParams(dimension_semantics=("parallel", "parallel")),
    )(x, wb, dy)


@jax.custom_vjp
def conv_silu(x, wb):
    return _conv_fwd_call(x, wb)


def _conv_silu_fwd(x, wb):
    return _conv_fwd_call(x, wb), (x, wb)


def _conv_silu_bwd(res, dy):
    x, wb = res
    dx, dwb = _conv_bwd_call(x, wb, dy)
    return dx, jnp.sum(dwb, axis=0)


conv_silu.defvjp(_conv_silu_fwd, _conv_silu_bwd)


SSD_PAIRS = SSD_GROUP_HEADS // 2
NEG_INF = -1e30


def _ssd_common(x_ref, dtx_ref, dtt_ref, anx_ref, anc_ref, b_ref, c_ref, reverse):
    L = SSD_CHUNK
    xv = x_ref[0]
    dt = dtx_ref[0]
    ri = lax.broadcasted_iota(jnp.int32, (L, L), 0)
    ci = lax.broadcasted_iota(jnp.int32, (L, L), 1)
    causal = (ri <= ci) if reverse else (ri >= ci)
    tri = causal.astype(F32)
    a_cs = jnp.dot(tri, dt * anx_ref[...], precision=HIGHEST, preferred_element_type=F32)
    a_row = dtt_ref[0, 0] * anc_ref[0]
    acs_row = lax.dot_general(a_row, tri, NT_DIMS, precision=HIGHEST, preferred_element_type=F32)
    xd = xv * dt
    bmat = b_ref[0].astype(BF16)
    cmat = c_ref[0].astype(BF16)
    gmat = lax.dot_general(cmat, bmat, NT_DIMS, preferred_element_type=F32)
    return xv, dt, causal, tri, a_cs, acs_row, xd, bmat, cmat, gmat


def _ssd_lambda(a_cs, acs_row, causal, h):
    col = a_cs[:, h * SSD_HEAD_DIM:h * SSD_HEAD_DIM + 1]
    row = acs_row[h:h + 1, :]
    return jnp.exp(jnp.where(causal, col - row, NEG_INF))


def _ssd_fwd_call(x, dtx, dtt, anx, anc, bm, cm, reverse):
    b, s, _ = x.shape
    L, N, GW = SSD_CHUNK, SSD_STATE, SSD_GROUP_WIDTH
    nc = s // L
    end = 0 if reverse else L - 1

    def body(x_ref, dtx_ref, dtt_ref, anx_ref, anc_ref, b_ref, c_ref, y_ref, hs_ref, state):
        @pl.when(pl.program_id(2) == 0)
        def _():
            state[...] = jnp.zeros_like(state)

        xv, dt, causal, tri, a_cs, acs_row, xd, bmat, cmat, gmat = _ssd_common(
            x_ref, dtx_ref, dtt_ref, anx_ref, anc_ref, b_ref, c_ref, reverse)
        hin = state[...]
        hs_ref[0, 0, 0] = hin
        y_off = jnp.dot(cmat, hin.astype(BF16), preferred_element_type=F32) * jnp.exp(a_cs)
        a_end = a_cs[end:end + 1, :]
        s_new = lax.dot_general(bmat, (xd * jnp.exp(a_end - a_cs)).astype(BF16), TN_DIMS, preferred_element_type=F32)
        state[...] = jnp.exp(a_end) * hin + s_new
        lane = lax.broadcasted_iota(jnp.int32, (L, LANES), 1)
        for pr in range(SSD_PAIRS):
            sl = slice(pr * LANES, (pr + 1) * LANES)
            xdp = xd[:, sl].astype(BF16)
            w0 = (gmat * _ssd_lambda(a_cs, acs_row, causal, 2 * pr)).astype(BF16)
            w1 = (gmat * _ssd_lambda(a_cs, acs_row, causal, 2 * pr + 1)).astype(BF16)
            y0 = jnp.dot(w0, xdp, preferred_element_type=F32)
            y1 = jnp.dot(w1, xdp, preferred_element_type=F32)
            y_ref[0, :, sl] = jnp.where(lane < SSD_HEAD_DIM, y0, y1) + y_off[:, sl]

    G = SSD_GROUPS
    chunk = (lambda c: nc - 1 - c) if reverse else (lambda c: c)
    seq = lambda bi, gi, c: (bi, chunk(c), gi)
    return pl.pallas_call(
        body, name="ssd_fwd", grid=(b, G, nc),
        in_specs=[pl.BlockSpec((1, L, GW), seq),
                  pl.BlockSpec((1, L, GW), seq),
                  pl.BlockSpec((1, 1, SSD_GROUP_HEADS, L), lambda bi, gi, c: (bi, gi, 0, chunk(c))),
                  pl.BlockSpec((1, GW), lambda bi, gi, c: (0, gi)),
                  pl.BlockSpec((1, SSD_GROUP_HEADS, 1), lambda bi, gi, c: (gi, 0, 0)),
                  pl.BlockSpec((1, L, N), seq),
                  pl.BlockSpec((1, L, N), seq)],
        out_specs=[pl.BlockSpec((1, L, GW), seq),
                   pl.BlockSpec((1, 1, 1, N, GW), lambda bi, gi, c: (bi, gi, chunk(c), 0, 0))],
        out_shape=[jax.ShapeDtypeStruct(x.shape, F32), jax.ShapeDtypeStruct((b, G, nc, N, GW), F32)],
        scratch_shapes=[pltpu.VMEM((N, GW), F32)],
        compiler_params=pltpu.CompilerParams(dimension_semantics=("parallel", "parallel", "arbitrary")),
    )(x, dtx, dtt, anx, anc, bm, cm)


def _ssd_bwd_call(x, dtx, dtt, anx, anc, bm, cm, hs, dy, reverse):
    b, s, _ = x.shape
    L, N, GW = SSD_CHUNK, SSD_STATE, SSD_GROUP_WIDTH
    nc = s // L
    end = 0 if reverse else L - 1

    def body(x_ref, dtx_ref, dtt_ref, anx_ref, anc_ref, b_ref, c_ref, hs_ref, dy_ref,
             dx_ref, ddt_ref, dan_ref, db_ref, dc_ref, dstate):
        @pl.when(pl.program_id(2) == 0)
        def _():
            dstate[...] = jnp.zeros_like(dstate)

        xv, dt, causal, tri, a_cs, acs_row, xd, bmat, cmat, gmat = _ssd_common(
            x_ref, dtx_ref, dtt_ref, anx_ref, anc_ref, b_ref, c_ref, reverse)
        hin = hs_ref[0, 0, 0]
        hinb = hin.astype(BF16)
        dyv = dy_ref[0]
        ds_out = dstate[...]
        dsb = ds_out.astype(BF16)
        eacs = jnp.exp(a_cs)
        a_end = a_cs[end:end + 1, :]
        e_end = jnp.exp(a_end)
        dec = jnp.exp(a_end - a_cs)
        dye = dyv * eacs
        dyeb = dye.astype(BF16)
        xdec = xd * dec
        ch = jnp.dot(cmat, hinb, preferred_element_type=F32)
        bds = jnp.dot(bmat, dsb, preferred_element_type=F32)
        t_state = xdec * bds
        d_aend = jnp.sum(t_state, axis=0, keepdims=True) + e_end * jnp.sum(ds_out * hin, axis=0, keepdims=True)
        dacs = dye * ch - t_state
        dxd_state = bds * dec
        dstate[...] = e_end * ds_out + lax.dot_general(cmat, dyeb, TN_DIMS, preferred_element_type=F32)

        lane = lax.broadcasted_iota(jnp.int32, (L, LANES), 1)
        ones = jnp.full((L, LANES), 1.0 / SSD_HEAD_DIM, F32)
        dg = jnp.zeros((L, L), F32)
        dxd_parts, dacs_parts = [], []
        for pr in range(SSD_PAIRS):
            sl = slice(pr * LANES, (pr + 1) * LANES)
            xdp = xd[:, sl]
            dyp = dyv[:, sl]
            dxd_p = jnp.zeros((L, LANES), F32)
            dacs_p = jnp.zeros((L, LANES), F32)
            for half in range(2):
                mine = (lane < SSD_HEAD_DIM) if half == 0 else (lane >= SSD_HEAD_DIM)
                lam = _ssd_lambda(a_cs, acs_row, causal, 2 * pr + half)
                w = gmat * lam
                xdh = jnp.where(mine, xdp, 0.0).astype(BF16)
                dyh = jnp.where(mine, dyp, 0.0).astype(BF16)
                dw = lax.dot_general(dyh, xdh, NT_DIMS, preferred_element_type=F32)
                dg = dg + dw * lam
                mm = dw * w
                rs = jnp.dot(mm, ones, precision=HIGHEST, preferred_element_type=F32)
                cs = lax.dot_general(mm, ones, TN_DIMS, precision=HIGHEST, preferred_element_type=F32)
                dacs_p = dacs_p + jnp.where(mine, rs - cs, 0.0)
                wtdy = lax.dot_general(w.astype(BF16), dyh, TN_DIMS, preferred_element_type=F32)
                dxd_p = dxd_p + wtdy
            dxd_parts.append(dxd_p)
            dacs_parts.append(dacs_p)
        dxd = jnp.concatenate(dxd_parts, axis=1) + dxd_state
        dacs = dacs + jnp.concatenate(dacs_parts, axis=1)
        last = lax.broadcasted_iota(jnp.int32, dacs.shape, 0) == end
        dacs = dacs + jnp.where(last, d_aend, 0.0)
        da = lax.dot_general(tri, dacs, TN_DIMS, precision=HIGHEST, preferred_element_type=F32)
        dgb = dg.astype(BF16)
        dc_ref[0] = (jnp.dot(dgb, bmat, preferred_element_type=F32)
                     + lax.dot_general(dyeb, hinb, NT_DIMS, preferred_element_type=F32))
        db_ref[0] = (lax.dot_general(dgb, cmat, TN_DIMS, preferred_element_type=F32)
                     + lax.dot_general(xdec.astype(BF16), dsb, NT_DIMS, preferred_element_type=F32))
        dx_ref[0] = dxd * dt
        ddt_ref[0] = da * anx_ref[...] + dxd * xv
        dan_ref[0, 0, 0] = jnp.sum(da * dt, axis=0, keepdims=True)

    G = SSD_GROUPS
    chunk = (lambda c: c) if reverse else (lambda c: nc - 1 - c)
    rev = lambda bi, gi, c: (bi, chunk(c), gi)
    return pl.pallas_call(
        body, name="ssd_bwd", grid=(b, G, nc),
        in_specs=[pl.BlockSpec((1, L, GW), rev),
                  pl.BlockSpec((1, L, GW), rev),
                  pl.BlockSpec((1, 1, SSD_GROUP_HEADS, L), lambda bi, gi, c: (bi, gi, 0, chunk(c))),
                  pl.BlockSpec((1, GW), lambda bi, gi, c: (0, gi)),
                  pl.BlockSpec((1, SSD_GROUP_HEADS, 1), lambda bi, gi, c: (gi, 0, 0)),
                  pl.BlockSpec((1, L, N), rev),
                  pl.BlockSpec((1, L, N), rev),
                  pl.BlockSpec((1, 1, 1, N, GW), lambda bi, gi, c: (bi, gi, chunk(c), 0, 0)),
                  pl.BlockSpec((1, L, GW), rev)],
        out_specs=[pl.BlockSpec((1, L, GW), rev),
                   pl.BlockSpec((1, L, GW), rev),
                   pl.BlockSpec((1, 1, 1, 1, GW), lambda bi, gi, c: (bi, gi, chunk(c), 0, 0)),
                   pl.BlockSpec((1, L, N), rev),
                   pl.BlockSpec((1, L, N), rev)],
        out_shape=[jax.ShapeDtypeStruct(x.shape, F32), jax.ShapeDtypeStruct(x.shape, F32),
                   jax.ShapeDtypeStruct((b, G, nc, 1, GW), F32),
                   jax.ShapeDtypeStruct(bm.shape, F32), jax.ShapeDtypeStruct(cm.shape, F32)],
        scratch_shapes=[pltpu.VMEM((N, GW), F32)],
        compiler_params=pltpu.CompilerParams(dimension_semantics=("parallel", "parallel", "arbitrary")),
    )(x, dtx, dtt, anx, anc, bm, cm, hs, dy)


@functools.partial(jax.custom_vjp, nondiff_argnums=(7,))
def _ssd_scan(x, dtx, dtt, anx, anc, bm, cm, reverse):
    return _ssd_fwd_call(x, dtx, dtt, anx, anc, bm, cm, reverse)[0]


def _ssd_scan_fwd(x, dtx, dtt, anx, anc, bm, cm, reverse):
    y, hs = _ssd_fwd_call(x, dtx, dtt, anx, anc, bm, cm, reverse)
    return y, (x, dtx, dtt, anx, anc, bm, cm, hs)


def _ssd_scan_bwd(reverse, res, dy):
    x, dtx, dtt, anx, anc, bm, cm, hs = res
    dx, ddtx, dan, db, dc = _ssd_bwd_call(x, dtx, dtt, anx, anc, bm, cm, hs, dy, reverse)
    b, g, nc, _, gw = dan.shape
    danx = jnp.sum(dan, axis=(0, 2, 3)).reshape(1, g * gw)
    return dx, ddtx, jnp.zeros_like(dtt), danx, jnp.zeros_like(anc), db, dc


_ssd_scan.defvjp(_ssd_scan_fwd, _ssd_scan_bwd)


def ssd_chunked(xs, dt, a_neg, bm, cm, reverse):
    b, s, _ = xs.shape
    dtx = jnp.repeat(dt, SSD_HEAD_DIM, axis=-1)
    dtt = jnp.transpose(dt, (0, 2, 1)).reshape(b, SSD_GROUPS, SSD_GROUP_HEADS, s)
    anx = jnp.repeat(a_neg, SSD_HEAD_DIM)[None, :]
    anc = a_neg.reshape(SSD_GROUPS, SSD_GROUP_HEADS, 1)
    return _ssd_scan(xs, dtx, dtt, anx, anc, bm, cm, reverse)


def _loss_call(y, t):
    rows, cols = y.shape
    tr = _row_tile(rows, cols * 4)

    def body(y_ref, t_ref, loss_ref, diff_ref):
        @pl.when(pl.program_id(0) == 0)
        def _():
            loss_ref[...] = jnp.zeros_like(loss_ref)

        d = y_ref[...] - t_ref[...]
        diff_ref[...] = d * (1.0 / cols)
        part = jnp.sum(jnp.sum(d * d, axis=1, keepdims=True), axis=0, keepdims=True)
        loss_ref[...] += part * (0.5 / cols)

    return pl.pallas_call(
        body, name="loss_head", grid=(rows // tr,),
        in_specs=[pl.BlockSpec((tr, cols), lambda i: (i, 0)), pl.BlockSpec((tr, cols), lambda i: (i, 0))],
        out_specs=[pl.BlockSpec((1, 1), lambda i: (0, 0)), pl.BlockSpec((tr, cols), lambda i: (i, 0))],
        out_shape=[jax.ShapeDtypeStruct((1, 1), F32), jax.ShapeDtypeStruct((rows, cols), F32)],
        compiler_params=pltpu.CompilerParams(dimension_semantics=("arbitrary",)),
    )(y, t)


@jax.custom_vjp
def loss_head(y, t):
    return _loss_call(y, t)[0][0, 0]


def _loss_head_fwd(y, t):
    loss, diff = _loss_call(y, t)
    return loss[0, 0], diff


def _loss_head_bwd(diff, g):
    return g * diff, jnp.zeros_like(diff)


loss_head.defvjp(_loss_head_fwd, _loss_head_bwd)


def _axial_rope_tables(seq_len, rot_dim):
    rows = seq_len // GRID_W
    row_idx = jnp.repeat(jnp.arange(rows), GRID_W).astype(F32)
    col_idx = jnp.tile(jnp.arange(GRID_W), rows).astype(F32)
    axis_dim = rot_dim // 2
    inv_freq = jnp.power(ROPE_THETA, -jnp.arange(0, axis_dim, 2, dtype=F32) / axis_dim)
    ang_r = row_idx[:, None] * inv_freq[None, :]
    ang_c = col_idx[:, None] * inv_freq[None, :]
    return jnp.cos(ang_r), jnp.sin(ang_r), jnp.cos(ang_c), jnp.sin(ang_c)


def _rotate(x, cos, sin):
    x1, x2 = jnp.split(x, 2, axis=-1)
    cos = cos[:, None, :]
    sin = sin[:, None, :]
    return jnp.concatenate([x1 * cos - x2 * sin, x1 * sin + x2 * cos], axis=-1)


def _apply_axial_rope(x, tables):
    cos_r, sin_r, cos_c, sin_c = tables
    x_row, x_col = jnp.split(x, 2, axis=-1)
    return jnp.concatenate([_rotate(x_row, cos_r, sin_r), _rotate(x_col, cos_c, sin_c)], axis=-1)


def _heads_first(t):
    return jnp.transpose(t, (0, 2, 1, 3))


def _gqa_group(q, k, v, q_norm_g, k_norm_g, rope, b, s):
    q = rms_norm(q, jnp.tile(q_norm_g, GQA_HEADS)[None, :], GQA_HEADS).reshape(b, s, GQA_HEADS, GQA_HEAD_DIM)
    k = rms_norm(k, jnp.tile(k_norm_g, GQA_KV_HEADS)[None, :], GQA_KV_HEADS).reshape(b, s, GQA_KV_HEADS, GQA_HEAD_DIM)
    v = v.reshape(b, s, GQA_KV_HEADS, GQA_HEAD_DIM)
    q = _apply_axial_rope(q, rope)
    k = _apply_axial_rope(k, rope)
    o = attention(_heads_first(q), _heads_first(k), _heads_first(v), GQA_HEAD_DIM ** -0.5)
    return _heads_first(o).reshape(b * s, GQA_WIDTH)


def _mla_group(c_q, c_kv, k_pe, q_norm_g, w_uq, kv_norm_g, w_ukv, rope, b, s):
    q = linear(rms_norm(c_q, q_norm_g[None, :], 1), w_uq).reshape(b, s, MLA_HEADS, MLA_NOPE_DIM + MLA_ROPE_DIM)
    q_nope, q_pe = q[..., :MLA_NOPE_DIM], q[..., MLA_NOPE_DIM:]
    kv = linear(rms_norm(c_kv, kv_norm_g[None, :], 1), w_ukv).reshape(b, s, MLA_HEADS, MLA_NOPE_DIM + MLA_V_DIM)
    k_nope, v = kv[..., :MLA_NOPE_DIM], kv[..., MLA_NOPE_DIM:]
    q_pe = _apply_axial_rope(q_pe, rope)
    k_pe = _apply_axial_rope(k_pe.reshape(b, s, 1, MLA_ROPE_DIM), rope)
    q = jnp.concatenate([q_nope, q_pe], axis=-1)
    k = jnp.concatenate([k_nope, jnp.broadcast_to(k_pe, (b, s, MLA_HEADS, MLA_ROPE_DIM))], axis=-1)
    o = attention(_heads_first(q), _heads_first(k), _heads_first(v), (MLA_NOPE_DIM + MLA_ROPE_DIM) ** -0.5)
    return _heads_first(o).reshape(b * s, MLA_HEADS * MLA_V_DIM)


def _ssd_group(z, xbc, dt_raw, conv_w, conv_b, dt_bias, a_log, d_skip, norm_g, b, s):
    wb = jnp.concatenate([conv_w, conv_b[None, :], jnp.zeros((CONV_PACK_ROWS - SSD_CONV - 1, SSD_CONV_DIM), F32)], axis=0)
    xbc = conv_silu(xbc.reshape(b, s, SSD_CONV_DIM), wb)
    xs = xbc[..., :SSD_INNER]
    bm = xbc[..., SSD_INNER:SSD_INNER + SSD_GROUPS * SSD_STATE]
    cm = xbc[..., SSD_INNER + SSD_GROUPS * SSD_STATE:]
    dt = jax.nn.softplus(dt_raw.reshape(b, s, 2, SSD_HEADS) + dt_bias)
    a_neg = -jnp.exp(a_log)
    y_fwd = ssd_chunked(xs, dt[:, :, 0], a_neg[0], bm, cm, False)
    y_bwd = ssd_chunked(xs, dt[:, :, 1], a_neg[1], bm, cm, True)
    y = y_fwd + y_bwd + xs * jnp.repeat(d_skip, SSD_HEAD_DIM)
    y = y.reshape(b * s, SSD_INNER) * jax.nn.silu(z)
    return rms_norm(y, norm_g[None, :], SSD_GROUPS)


MIXER_WEIGHTS = ('q_norm_g', 'k_norm_g', 'mla_q_norm_g', 'w_uq', 'mla_kv_norm_g', 'w_ukv', 'conv_w', 'conv_b',
                 'dt_bias', 'a_log', 'd_skip', 'ssd_norm_g')


def _mixer(proj, w, rope_a, rope_b, b, s):
    idx = np.cumsum(IN_SPLITS).tolist()
    q_a, k_a, v_a, cq_b, ckv_b, kpe_b, z_c, xbc_c, dt_c = [proj[:, lo:hi] for lo, hi in zip([0] + idx[:-1], idx)]
    o_a = _gqa_group(q_a, k_a, v_a, w["q_norm_g"], w["k_norm_g"], rope_a, b, s)
    o_b = _mla_group(cq_b, ckv_b, kpe_b, w["mla_q_norm_g"], w["w_uq"], w["mla_kv_norm_g"], w["w_ukv"], rope_b, b, s)
    o_c = _ssd_group(z_c, xbc_c, dt_c, w["conv_w"], w["conv_b"], w["dt_bias"], w["a_log"], w["d_skip"],
                     w["ssd_norm_g"], b, s)
    return jnp.concatenate([o_a, o_b, o_c], axis=-1)


def _seq_tile(s, row_bytes):
    return _row_tile(s, row_bytes)


def _normmod_fwd(x, g, scale, shift):
    b, s, d = x.shape
    tr = _seq_tile(s, d * 4)

    def body(x_ref, g_ref, sc_ref, sh_ref, h_ref):
        xv = x_ref[0]
        r = lax.rsqrt(jnp.mean(xv * xv, axis=-1, keepdims=True) + EPS)
        h_ref[0] = (xv * r * g_ref[...] * (1.0 + sc_ref[0]) + sh_ref[0]).astype(BF16)

    act = pl.BlockSpec((1, tr, d), lambda bi, i: (bi, i, 0))
    vec = pl.BlockSpec((1, 1, d), lambda bi, i: (bi, 0, 0))
    return pl.pallas_call(
        body, name="normmod_fwd", grid=(b, s // tr),
        in_specs=[act, pl.BlockSpec((1, d), lambda bi, i: (0, 0)), vec, vec], out_specs=act,
        out_shape=jax.ShapeDtypeStruct((b, s, d), BF16),
        compiler_params=pltpu.CompilerParams(dimension_semantics=("parallel", "parallel")),
    )(x, g, scale, shift)


def _normmod_bwd(x, g, scale, dh, resid):
    b, s, d = x.shape
    tr = _seq_tile(s, d * 4)

    def body(x_ref, g_ref, sc_ref, dh_ref, res_ref, dx_ref, dg_ref, dsc_ref, dsh_ref):
        bi, i = pl.program_id(0), pl.program_id(1)

        @pl.when((bi == 0) & (i == 0))
        def _():
            dg_ref[...] = jnp.zeros_like(dg_ref)

        @pl.when(i == 0)
        def _():
            dsc_ref[...] = jnp.zeros_like(dsc_ref)
            dsh_ref[...] = jnp.zeros_like(dsh_ref)

        xv = x_ref[0]
        dhv = dh_ref[0]
        gv = g_ref[...]
        r = lax.rsqrt(jnp.mean(xv * xv, axis=-1, keepdims=True) + EPS)
        xhat = xv * r
        dsh_ref[0] += jnp.sum(dhv, axis=0, keepdims=True)
        dsc_ref[0] += jnp.sum(dhv * (xhat * gv), axis=0, keepdims=True)
        dn = dhv * (1.0 + sc_ref[0])
        dg_ref[...] += jnp.sum(dn * xhat, axis=0, keepdims=True)
        dxhat = dn * gv
        dx_ref[0] = r * (dxhat - xhat * jnp.mean(dxhat * xhat, axis=-1, keepdims=True)) + res_ref[0]

    act = pl.BlockSpec((1, tr, d), lambda bi, i: (bi, i, 0))
    vec = pl.BlockSpec((1, 1, d), lambda bi, i: (bi, 0, 0))
    gain = pl.BlockSpec((1, d), lambda bi, i: (0, 0))
    return pl.pallas_call(
        body, name="normmod_bwd", grid=(b, s // tr),
        in_specs=[act, gain, vec, act, act], out_specs=[act, gain, vec, vec],
        out_shape=[jax.ShapeDtypeStruct((b, s, d), F32), jax.ShapeDtypeStruct((1, d), F32),
                   jax.ShapeDtypeStruct((b, 1, d), F32), jax.ShapeDtypeStruct((b, 1, d), F32)],
        compiler_params=pltpu.CompilerParams(dimension_semantics=("arbitrary", "arbitrary")),
    )(x, g, scale, dh, resid)


def _gated_add(x, gate, t):
    b, s, d = x.shape
    tr = _seq_tile(s, d * 4)

    def body(x_ref, g_ref, t_ref, o_ref):
        o_ref[0] = x_ref[0] + g_ref[0] * t_ref[0]

    act = pl.BlockSpec((1, tr, d), lambda bi, i: (bi, i, 0))
    vec = pl.BlockSpec((1, 1, d), lambda bi, i: (bi, 0, 0))
    return pl.pallas_call(
        body, name="gated_add", grid=(b, s // tr), in_specs=[act, vec, act], out_specs=act,
        out_shape=jax.ShapeDtypeStruct((b, s, d), F32),
        compiler_params=pltpu.CompilerParams(dimension_semantics=("parallel", "parallel")),
    )(x, gate, t)


def _gated_bwd(dy, gate, t):
    b, s, d = dy.shape
    tr = _seq_tile(s, d * 4)

    def body(dy_ref, g_ref, t_ref, dt_ref, dgate_ref):
        @pl.when(pl.program_id(1) == 0)
        def _():
            dgate_ref[...] = jnp.zeros_like(dgate_ref)

        dyv = dy_ref[0]
        dt_ref[0] = (g_ref[0] * dyv).astype(BF16)
        dgate_ref[0] += jnp.sum(dyv * t_ref[0], axis=0, keepdims=True)

    act = pl.BlockSpec((1, tr, d), lambda bi, i: (bi, i, 0))
    vec = pl.BlockSpec((1, 1, d), lambda bi, i: (bi, 0, 0))
    return pl.pallas_call(
        body, name="gated_bwd", grid=(b, s // tr), in_specs=[act, vec, act], out_specs=[act, vec],
        out_shape=[jax.ShapeDtypeStruct((b, s, d), BF16), jax.ShapeDtypeStruct((b, 1, d), F32)],
        compiler_params=pltpu.CompilerParams(dimension_semantics=("parallel", "arbitrary")),
    )(dy, gate, t)


def _swiglu_fwd(gu):
    rows, f2 = gu.shape
    f = f2 // 2
    tr = _row_tile(rows, f2 * 4)

    def body(gu_ref, a_ref):
        gt = gu_ref[:, :f]
        a_ref[...] = (gt * jax.nn.sigmoid(gt) * gu_ref[:, f:]).astype(BF16)

    return pl.pallas_call(
        body, name="swiglu_fwd", grid=(rows // tr,),
        in_specs=[pl.BlockSpec((tr, f2), lambda i: (i, 0))], out_specs=pl.BlockSpec((tr, f), lambda i: (i, 0)),
        out_shape=jax.ShapeDtypeStruct((rows, f), BF16),
        compiler_params=pltpu.CompilerParams(dimension_semantics=("parallel",)),
    )(gu)


def _swiglu_bwd(gu, dact):
    rows, f2 = gu.shape
    f = f2 // 2
    tr = _row_tile(rows, f2 * 4)

    def body(gu_ref, da_ref, dgu_ref):
        gt = gu_ref[:, :f]
        up = gu_ref[:, f:]
        da = da_ref[...]
        sg = jax.nn.sigmoid(gt)
        dgu_ref[:, :f] = (da * up * (sg * (1.0 + gt * (1.0 - sg)))).astype(BF16)
        dgu_ref[:, f:] = (da * gt * sg).astype(BF16)

    return pl.pallas_call(
        body, name="swiglu_bwd", grid=(rows // tr,),
        in_specs=[pl.BlockSpec((tr, f2), lambda i: (i, 0)), pl.BlockSpec((tr, f), lambda i: (i, 0))],
        out_specs=pl.BlockSpec((tr, f2), lambda i: (i, 0)),
        out_shape=jax.ShapeDtypeStruct((rows, f2), BF16),
        compiler_params=pltpu.CompilerParams(dimension_semantics=("parallel",)),
    )(gu, dact)


def _layer_fwd(x, mod, w, rope_a, rope_b):
    b, s, d = x.shape
    m = b * s
    shift1, scale1, gate1, shift2, scale2, gate2 = [t[:, None, :] for t in jnp.split(mod, 6, axis=-1)]
    g1, g2 = w["norm1_g"][None, :], w["norm2_g"][None, :]
    h1 = _normmod_fwd(x, g1, scale1, shift1).reshape(m, d)
    proj = _matmul(h1, w["w_in"], name="w_in_fwd")
    mixer_w = {n: w[n] for n in MIXER_WEIGHTS}
    o, mixer_vjp = jax.vjp(lambda p, mw: _mixer(p, mw, rope_a, rope_b, b, s), proj, mixer_w)
    o = o.astype(BF16)
    mix = _matmul(o, w["w_out"], name="w_out_fwd").reshape(b, s, d)
    x_mid = _gated_add(x, gate1, mix)
    h2 = _normmod_fwd(x_mid, g2, scale2, shift2).reshape(m, d)
    gu = _matmul(h2, w["w_gate_up"], name="w_gate_up_fwd")
    act = _swiglu_fwd(gu)
    ffn = _matmul(act, w["w_down"], name="w_down_fwd").reshape(b, s, d)
    x_out = _gated_add(x_mid, gate2, ffn)
    res = (x, x_mid, h1, h2, o, mix, gu, act, ffn, mixer_vjp, scale1, gate1, scale2, gate2, g1, g2)
    return x_out, res


def _layer_bwd(res, w, dx_out):
    x, x_mid, h1, h2, o, mix, gu, act, ffn, mixer_vjp, scale1, gate1, scale2, gate2, g1, g2 = res
    b, s, d = x.shape
    m = b * s
    grads = {}
    dffn, dgate2 = _gated_bwd(dx_out, gate2, ffn)
    dffn = dffn.reshape(m, d)
    dact = _matmul(dffn, w["w_down"], tb=True, name="w_down_dx")
    grads["w_down"] = _matmul(act, dffn, ta=True, name="w_down_dw")
    dgu = _swiglu_bwd(gu, dact)
    dh2 = _matmul(dgu, w["w_gate_up"], tb=True, name="w_gate_up_dx").reshape(b, s, d)
    grads["w_gate_up"] = _matmul(h2, dgu, ta=True, name="w_gate_up_dw")
    dx_mid, dg2, dscale2, dshift2 = _normmod_bwd(x_mid, g2, scale2, dh2, dx_out)
    dmix, dgate1 = _gated_bwd(dx_mid, gate1, mix)
    dmix = dmix.reshape(m, d)
    do = _matmul(dmix, w["w_out"], tb=True, name="w_out_dx")
    grads["w_out"] = _matmul(o, dmix, ta=True, name="w_out_dw")
    dproj, dmixer_w = mixer_vjp(do)
    dproj = dproj.astype(BF16)
    dh1 = _matmul(dproj, w["w_in"], tb=True, name="w_in_dx").reshape(b, s, d)
    grads["w_in"] = _matmul(h1, dproj, ta=True, name="w_in_dw")
    dx, dg1, dscale1, dshift1 = _normmod_bwd(x, g1, scale1, dh1, dx_mid)
    grads.update(dmixer_w)
    grads["norm1_g"], grads["norm2_g"] = dg1[0], dg2[0]
    dmod = jnp.concatenate([dshift1, dscale1, dgate1, dshift2, dscale2, dgate2], axis=-1)[:, 0, :]
    return dx, dmod, grads


def _tail_loss(x2, final_norm_g, target2):
    return loss_head(rms_norm(x2, final_norm_g[None, :], 1), target2)


def _forward_backward(x, mod, layers, final_norm_g, target):
    b, s, d = x.shape
    rope_a = _axial_rope_tables(s, GQA_HEAD_DIM)
    rope_b = _axial_rope_tables(s, MLA_ROPE_DIM)
    saved = []
    for l, w in enumerate(layers):
        x, res = _layer_fwd(x, mod[l], w, rope_a, rope_b)
        saved.append(res)
    loss, (dx2, dfinal) = jax.value_and_grad(_tail_loss, argnums=(0, 1))(
        x.reshape(b * s, d), final_norm_g, target.reshape(b * s, d))
    dx = dx2.reshape(b, s, d)
    dmods, glayers = [None] * len(layers), [None] * len(layers)
    for l in reversed(range(len(layers))):
        dx, dmods[l], glayers[l] = _layer_bwd(saved[l], layers[l], dx)
    return loss, dx, jnp.stack(dmods), glayers, dfinal


ANY = pl.BlockSpec(memory_space=pl.ANY)


def _flip_if(v, bit):
    return 1 - v if bit else v


def _all_gather_devices(x):
    def body(x_ref, out_ref, send_sems, recv_sems):
        mx, my, mc = lax.axis_index("x"), lax.axis_index("y"), lax.axis_index("c")
        me = 4 * mx + 2 * my + mc
        sends = []
        for k in range(1, N_DEV):
            peer = (_flip_if(mx, k & 4), _flip_if(my, k & 2), _flip_if(mc, k & 1))
            cp = pltpu.make_async_remote_copy(src_ref=x_ref, dst_ref=out_ref.at[me], send_sem=send_sems.at[k - 1],
                                              recv_sem=recv_sems.at[k - 1], device_id=peer, device_id_type=MESH)
            cp.start()
            sends.append(cp)
        for k in range(1, N_DEV):
            peer = (_flip_if(mx, k & 4), _flip_if(my, k & 2), _flip_if(mc, k & 1))
            src = 4 * peer[0] + 2 * peer[1] + peer[2]
            pltpu.make_async_remote_copy(src_ref=x_ref, dst_ref=out_ref.at[src], send_sem=send_sems.at[k - 1],
                                         recv_sem=recv_sems.at[k - 1], device_id=peer, device_id_type=MESH).wait_recv()
        for cp in sends:
            cp.wait_send()

    out = pl.pallas_call(
        body, name="all_gather_devices", in_specs=[ANY], out_specs=ANY,
        out_shape=jax.ShapeDtypeStruct((N_DEV,) + x.shape, x.dtype),
        scratch_shapes=[pltpu.SemaphoreType.DMA((N_DEV - 1,)), pltpu.SemaphoreType.DMA((N_DEV - 1,))],
    )(x)
    me = 4 * lax.axis_index("x") + 2 * lax.axis_index("y") + lax.axis_index("c")
    return lax.dynamic_update_index_in_dim(out, x, me, 0)


def _all_gather_chips(shards):
    n = len(shards)
    halves = [t.reshape(2, t.shape[0] // 2, t.shape[1]) for t in shards]

    def body(*refs):
        ins, outs = refs[:n], refs[n:2 * n]
        ici_send, ici_recv, d2d_send, d2d_recv, own_send, own_recv = refs[2 * n:]
        mx, my, mc = lax.axis_index("x"), lax.axis_index("y"), lax.axis_index("c")
        me = 2 * mx + my
        sibling = (mx, my, 1 - mc)
        sends = []
        for i in range(n):
            cp = pltpu.make_async_remote_copy(src_ref=ins[i], dst_ref=outs[i].at[me], send_sem=own_send.at[i],
                                              recv_sem=own_recv.at[i], device_id=sibling, device_id_type=MESH)
            cp.start()
            sends.append(cp)
        for k in range(1, N_CHIPS):
            peer = (_flip_if(mx, k & 2), _flip_if(my, k & 1), mc)
            for i in range(n):
                j = (k - 1) * n + i
                cp = pltpu.make_async_remote_copy(src_ref=ins[i].at[mc], dst_ref=outs[i].at[me, mc],
                                                  send_sem=ici_send.at[j], recv_sem=ici_recv.at[j],
                                                  device_id=peer, device_id_type=MESH)
                cp.start()
                sends.append(cp)
        for k in range(1, N_CHIPS):
            peer = (_flip_if(mx, k & 2), _flip_if(my, k & 1), mc)
            src = 2 * peer[0] + peer[1]
            for i in range(n):
                j = (k - 1) * n + i
                landed = outs[i].at[src, mc]
                pltpu.make_async_remote_copy(src_ref=ins[i].at[mc], dst_ref=landed, send_sem=ici_send.at[j],
                                             recv_sem=ici_recv.at[j], device_id=peer, device_id_type=MESH).wait_recv()
                cp = pltpu.make_async_remote_copy(src_ref=landed, dst_ref=landed, send_sem=d2d_send.at[j],
                                                  recv_sem=d2d_recv.at[j], device_id=sibling, device_id_type=MESH)
                cp.start()
                sends.append(cp)
        for k in range(1, N_CHIPS):
            src = 2 * _flip_if(mx, k & 2) + _flip_if(my, k & 1)
            for i in range(n):
                j = (k - 1) * n + i
                theirs = outs[i].at[src, 1 - mc]
                pltpu.make_async_remote_copy(src_ref=theirs, dst_ref=theirs, send_sem=d2d_send.at[j],
                                             recv_sem=d2d_recv.at[j], device_id=sibling, device_id_type=MESH).wait_recv()
        for i in range(n):
            pltpu.make_async_remote_copy(src_ref=ins[i], dst_ref=outs[i].at[me], send_sem=own_send.at[i],
                                         recv_sem=own_recv.at[i], device_id=sibling, device_id_type=MESH).wait_recv()
        for cp in sends:
            cp.wait_send()

    count = (N_CHIPS - 1) * n
    outs = pl.pallas_call(
        body, name="all_gather_chips", in_specs=[ANY] * n, out_specs=[ANY] * n,
        out_shape=[jax.ShapeDtypeStruct((N_CHIPS,) + t.shape, t.dtype) for t in halves],
        scratch_shapes=[pltpu.SemaphoreType.DMA((count,))] * 4 + [pltpu.SemaphoreType.DMA((n,))] * 2,
    )(*halves)
    return [o.reshape((N_CHIPS,) + t.shape) for o, t in zip(outs, shards)]


def _sibling_exchange(blocks, name):
    n = len(blocks)

    def body(*refs):
        ins, outs = refs[:n], refs[n:2 * n]
        send_sems, recv_sems = refs[2 * n:]
        mx, my, mc = lax.axis_index("x"), lax.axis_index("y"), lax.axis_index("c")
        cps = []
        for i in range(n):
            cp = pltpu.make_async_remote_copy(src_ref=ins[i], dst_ref=outs[i], send_sem=send_sems.at[i],
                                              recv_sem=recv_sems.at[i], device_id=(mx, my, 1 - mc),
                                              device_id_type=MESH)
            cp.start()
            cps.append(cp)
        for cp in cps:
            cp.wait()

    return pl.pallas_call(
        body, name=name, in_specs=[ANY] * n, out_specs=[ANY] * n,
        out_shape=[jax.ShapeDtypeStruct(t.shape, t.dtype) for t in blocks],
        scratch_shapes=[pltpu.SemaphoreType.DMA((n,)), pltpu.SemaphoreType.DMA((n,))],
    )(*blocks)


def _add_halves(own, recv):
    nb, r, c = own.shape
    tr = _row_tile(r, c * 4)

    def body(g_ref, r_ref, o_ref):
        o_ref[...] = (g_ref[...] + r_ref[...].astype(F32)).astype(BF16)

    spec = pl.BlockSpec((1, tr, c), lambda k, i: (k, i, 0))
    return pl.pallas_call(
        body, name="rs_add_halves", grid=(nb, r // tr), in_specs=[spec, spec], out_specs=spec,
        out_shape=jax.ShapeDtypeStruct((nb, r, c), BF16),
        compiler_params=pltpu.CompilerParams(dimension_semantics=("parallel", "parallel")),
    )(own, recv)


def _chip_exchange(parts):
    n = len(parts)

    def body(*refs):
        ins, outs = refs[:n], refs[n:2 * n]
        send_sems, recv_sems = refs[2 * n:]
        mx, my, mc = lax.axis_index("x"), lax.axis_index("y"), lax.axis_index("c")
        sends = []
        for k in range(1, N_CHIPS):
            peer = (_flip_if(mx, k & 2), _flip_if(my, k & 1), mc)
            dst_chip = 2 * peer[0] + peer[1]
            for i in range(n):
                j = (k - 1) * n + i
                cp = pltpu.make_async_remote_copy(src_ref=ins[i].at[dst_chip], dst_ref=outs[i].at[k - 1],
                                                  send_sem=send_sems.at[j], recv_sem=recv_sems.at[j],
                                                  device_id=peer, device_id_type=MESH)
                cp.start()
                sends.append(cp)
        for cp in sends:
            cp.wait()

    count = (N_CHIPS - 1) * n
    return pl.pallas_call(
        body, name="rs_chip_exchange", in_specs=[ANY] * n, out_specs=[ANY] * n,
        out_shape=[jax.ShapeDtypeStruct((N_CHIPS - 1,) + t.shape[1:], t.dtype) for t in parts],
        scratch_shapes=[pltpu.SemaphoreType.DMA((count,)), pltpu.SemaphoreType.DMA((count,))],
    )(*parts)


def _sum_chips(parts, recv, chip):
    _, r, c = parts.shape
    tr = _row_tile(r, c * 4)

    def body(chip_ref, p_ref, r_ref, o_ref):
        acc = p_ref[0].astype(F32)
        for k in range(N_CHIPS - 1):
            acc = acc + r_ref[k].astype(F32)
        o_ref[...] = acc

    return pl.pallas_call(
        body, name="rs_sum_chips",
        grid_spec=pltpu.PrefetchScalarGridSpec(
            num_scalar_prefetch=1, grid=(r // tr,),
            in_specs=[pl.BlockSpec((1, tr, c), lambda i, chip_ref: (chip_ref[0], i, 0)),
                      pl.BlockSpec((N_CHIPS - 1, tr, c), lambda i, chip_ref: (0, i, 0))],
            out_specs=pl.BlockSpec((tr, c), lambda i, chip_ref: (i, 0))),
        out_shape=jax.ShapeDtypeStruct((r, c), F32),
        compiler_params=pltpu.CompilerParams(dimension_semantics=("parallel",)),
    )(chip, parts, recv)


def _sum_leading(t, name):
    nb, r, c = t.shape
    tr = _row_tile(r, c * 4 * nb)

    def body(t_ref, o_ref):
        acc = t_ref[0]
        for k in range(1, nb):
            acc = acc + t_ref[k]
        o_ref[...] = acc

    return pl.pallas_call(
        body, name=name, grid=(r // tr,),
        in_specs=[pl.BlockSpec((nb, tr, c), lambda i: (0, i, 0))],
        out_specs=pl.BlockSpec((tr, c), lambda i: (i, 0)),
        out_shape=jax.ShapeDtypeStruct((r, c), F32),
        compiler_params=pltpu.CompilerParams(dimension_semantics=("parallel",)),
    )(t)


def _reduce_scatter(grads):
    mc = lax.axis_index("c")
    chip = (2 * lax.axis_index("x") + lax.axis_index("y")).astype(jnp.int32).reshape(1)
    split = [g.reshape(g.shape[0], 2, g.shape[1] // 2, g.shape[2]) for g in grads]
    own = [lax.dynamic_index_in_dim(g, mc, axis=1, keepdims=False) for g in split]
    away = [lax.dynamic_index_in_dim(g, 1 - mc, axis=1, keepdims=False).astype(BF16) for g in split]
    recv = _sibling_exchange(away, "rs_sibling_exchange")
    parts = [_add_halves(o, r) for o, r in zip(own, recv)]
    gathered = _chip_exchange(parts)
    mine = [_sum_chips(p, t, chip) for p, t in zip(parts, gathered)]
    theirs = _sibling_exchange(mine, "rs_sibling_swap")
    low = [jnp.where(mc == 0, a, b) for a, b in zip(mine, theirs)]
    high = [jnp.where(mc == 0, b, a) for a, b in zip(mine, theirs)]
    return [jnp.concatenate([lo, hi], axis=0) for lo, hi in zip(low, high)]


def _adamw(w, g, m, v):
    shape = w.shape
    cols = shape[-1]
    rows = int(np.prod(shape[:-1])) if len(shape) > 1 else 1
    w2, g2, m2, v2 = [t.reshape(rows, cols) for t in (w, g, m, v)]
    tr = _row_tile(rows, cols * 4 * 4)

    def body(w_ref, g_ref, m_ref, v_ref, d_ref, mo_ref, vo_ref):
        gv = g_ref[...]
        mn = ADAM_B1 * m_ref[...] + (1.0 - ADAM_B1) * gv
        vn = ADAM_B2 * v_ref[...] + (1.0 - ADAM_B2) * (gv * gv)
        m_hat = mn / (1.0 - ADAM_B1 ** ADAM_STEP)
        v_hat = vn / (1.0 - ADAM_B2 ** ADAM_STEP)
        d_ref[...] = -ADAM_LR * (m_hat / (jnp.sqrt(v_hat) + ADAM_EPS) + ADAM_WD * w_ref[...])
        mo_ref[...] = mn
        vo_ref[...] = vn

    spec = pl.BlockSpec((tr, cols), lambda i: (i, 0))
    outs = pl.pallas_call(
        body, name="adamw", grid=(rows // tr,), in_specs=[spec] * 4, out_specs=[spec] * 3,
        out_shape=[jax.ShapeDtypeStruct((rows, cols), F32)] * 3,
        compiler_params=pltpu.CompilerParams(dimension_semantics=("parallel",)),
    )(w2, g2, m2, v2)
    return [t.reshape(shape) for t in outs]


WEIGHTS = ['w_ada', 'b_ada', 'norm1_g', 'norm2_g', 'w_in', 'q_norm_g', 'k_norm_g', 'mla_q_norm_g', 'w_uq',
           'mla_kv_norm_g', 'w_ukv', 'conv_w', 'conv_b', 'dt_bias', 'a_log', 'd_skip', 'ssd_norm_g', 'w_out',
           'w_gate_up', 'w_down', 'final_norm_g']
COL_SHARDED = ('w_in', 'w_uq', 'w_ukv', 'w_gate_up')
ROW_SHARDED = ('w_out', 'w_down')
SMALL_LAYER = ('norm1_g', 'norm2_g', 'q_norm_g', 'k_norm_g', 'mla_q_norm_g', 'mla_kv_norm_g', 'conv_w', 'conv_b',
               'dt_bias', 'a_log', 'd_skip', 'ssd_norm_g')


def _pack(parts):
    flat = jnp.concatenate([p.reshape(-1) for p in parts])
    n = flat.shape[0]
    rows = -(-n // (8 * LANES)) * 8
    return jnp.pad(flat, (0, rows * LANES - n)).reshape(rows, LANES)


def _unpack(flat, shapes):
    out, pos = [], 0
    for shp in shapes:
        size = int(np.prod(shp))
        out.append(flat[pos:pos + size].reshape(shp))
        pos += size
    return out


def _cols_full(gathered):
    k, r, c = gathered.shape
    return jnp.transpose(gathered, (1, 0, 2)).reshape(r, k * c)


def _cols_split(full):
    r, c4 = full.shape
    return jnp.transpose(full.reshape(r, N_CHIPS, c4 // N_CHIPS), (1, 0, 2))


def kernel(x, c, w_ada, b_ada, norm1_g, norm2_g, w_in, q_norm_g, k_norm_g, mla_q_norm_g, w_uq, mla_kv_norm_g, w_ukv, conv_w, conv_b, dt_bias, a_log, d_skip, ssd_norm_g, w_out, w_gate_up, w_down, final_norm_g, loss_target, m_w_ada, m_b_ada, m_norm1_g, m_norm2_g, m_w_in, m_q_norm_g, m_k_norm_g, m_mla_q_norm_g, m_w_uq, m_mla_kv_norm_g, m_w_ukv, m_conv_w, m_conv_b, m_dt_bias, m_a_log, m_d_skip, m_ssd_norm_g, m_w_out, m_w_gate_up, m_w_down, m_final_norm_g, v_w_ada, v_b_ada, v_norm1_g, v_norm2_g, v_w_in, v_q_norm_g, v_k_norm_g, v_mla_q_norm_g, v_w_uq, v_mla_kv_norm_g, v_w_ukv, v_conv_w, v_conv_b, v_dt_bias, v_a_log, v_d_skip, v_ssd_norm_g, v_w_out, v_w_gate_up, v_w_down, v_final_norm_g):
    args = dict(locals())
    weights = {n: args[n] for n in WEIGHTS}
    depth = w_in.shape[0]
    bl, s, d = x.shape
    mx, my, mc = lax.axis_index("x"), lax.axis_index("y"), lax.axis_index("c")
    chip = 2 * mx + my
    dev = 2 * chip + mc
    ada_cols = w_ada.shape[-1]
    conv_cols = conv_w.shape[-1]

    first_shapes = [c.shape, conv_w.shape]
    first = _all_gather_devices(_pack([c, conv_w]))
    first = [_unpack(first[i].reshape(-1), first_shapes) for i in range(N_DEV)]
    c_act = jax.nn.silu(jnp.concatenate([f[0] for f in first], axis=0))
    conv_w_full = jnp.concatenate([first[2 * k][1] for k in range(N_CHIPS)], axis=-1)

    b_cols = lax.dynamic_slice_in_dim(b_ada, chip * ada_cols, ada_cols, axis=1)
    c_act_b = c_act.astype(BF16)
    mod_cols = jnp.stack([_matmul(c_act_b, w_ada[l].astype(BF16), name="ada_fwd") + b_cols[l][None, :]
                          for l in range(depth)])
    mod_all = _all_gather_devices(mod_cols.reshape(depth * N_DEV * bl, ada_cols))
    mod_all = mod_all.reshape(N_DEV, depth, N_DEV, bl, ada_cols)
    mod_mine = lax.dynamic_index_in_dim(mod_all, dev, axis=2, keepdims=False)
    mod = jnp.concatenate([mod_mine[2 * k] for k in range(N_CHIPS)], axis=-1)

    big = COL_SHARDED + ROW_SHARDED
    gathered = {}
    for l in range(depth):
        outs = _all_gather_chips([weights[n][l].astype(BF16) for n in big])
        for n, g in zip(big, outs):
            gathered[(n, l)] = g
    layers = []
    for l in range(depth):
        w = {n: weights[n][l] for n in SMALL_LAYER if n != 'conv_w'}
        w['conv_w'] = conv_w_full[l]
        for n in COL_SHARDED:
            w[n] = _cols_full(gathered[(n, l)])
        w['w_in'] = jnp.pad(w['w_in'], ((0, 0), (0, IN_COLS_PAD - IN_COLS)))
        for n in ('w_uq', 'w_ukv'):
            w[n] = w[n].astype(F32)
        for n in ROW_SHARDED:
            g = gathered[(n, l)]
            w[n] = g.reshape(g.shape[0] * g.shape[1], g.shape[2])
        layers.append(w)

    loss_local, gx, gmod, glayers, gfinal = _forward_backward(x, mod, layers, final_norm_g, loss_target)

    grads = {}
    for l in reversed(range(depth)):
        blocks = []
        for n in COL_SHARDED:
            g = glayers[l][n]
            if n == 'w_in':
                g = g[:, :IN_COLS]
            blocks.append(_cols_split(g))
        for n in ROW_SHARDED:
            g = glayers[l][n]
            blocks.append(g.reshape(N_CHIPS, g.shape[0] // N_CHIPS, g.shape[1]))
        for n, g in zip(big, _reduce_scatter(blocks)):
            grads[(n, l)] = g

    small_parts = [jnp.stack([glayers[l][n] for l in range(depth)]) for n in SMALL_LAYER]
    small_parts += [gfinal, loss_local.reshape(1), gmod]
    small_shapes = [p.shape for p in small_parts]
    last = _all_gather_devices(_pack(small_parts))
    summed = _unpack(_sum_leading(last, "sum_devices").reshape(-1), small_shapes)
    small = dict(zip(SMALL_LAYER, summed[:len(SMALL_LAYER)]))
    g_final, loss, gmod_sum = summed[len(SMALL_LAYER):]
    small['conv_w'] = lax.dynamic_slice_in_dim(small['conv_w'], chip * conv_cols, conv_cols, axis=2)
    gmod_all = jnp.stack([_unpack(last[i].reshape(-1), small_shapes)[-1] for i in range(N_DEV)], axis=1)
    gmod_all = gmod_all.reshape(depth, N_DEV * bl, gmod.shape[-1])
    gmod_cols = lax.dynamic_slice_in_dim(gmod_all, chip * ada_cols, ada_cols, axis=2)
    g_w_ada = jnp.stack([_matmul(c_act_b, gmod_cols[l].astype(BF16), ta=True, name="ada_dw") for l in range(depth)])
    g_b_ada = gmod_sum[:, 0]
    for i in range(1, bl):
        g_b_ada = g_b_ada + gmod_sum[:, i]

    grad = {'w_ada': g_w_ada, 'b_ada': g_b_ada, 'final_norm_g': g_final}
    for n in SMALL_LAYER:
        grad[n] = small[n]
    for n in big:
        grad[n] = jnp.stack([grads[(n, l)] for l in range(depth)])

    delta, new_m, new_v = {}, {}, {}
    for n in WEIGHTS:
        delta[n], new_m[n], new_v[n] = _adamw(weights[n], grad[n], args["m_" + n], args["v_" + n])
    return (loss.reshape(()), gx, *[grad[n] for n in WEIGHTS], *[delta[n] for n in WEIGHTS],
            *[new_m[n] for n in WEIGHTS], *[new_v[n] for n in WEIGHTS])
```

```python
import functools

import numpy as np
import jax
import jax.numpy as jnp
from jax import lax
from jax.experimental import pallas as pl
from jax.experimental.pallas import tpu as pltpu

F32 = jnp.float32
BF16 = jnp.bfloat16
HIGHEST = lax.Precision.HIGHEST
MESH = pl.DeviceIdType.MESH

GRID_W = 64
ROPE_THETA = 10000.0
EPS = 1e-6

GQA_HEADS, GQA_KV_HEADS, GQA_HEAD_DIM = 6, 2, 128
GQA_WIDTH = GQA_HEADS * GQA_HEAD_DIM
GQA_KV_WIDTH = GQA_KV_HEADS * GQA_HEAD_DIM
MLA_HEADS, MLA_Q_LORA, MLA_KV_LORA = 4, 512, 256
MLA_NOPE_DIM, MLA_ROPE_DIM, MLA_V_DIM = 128, 64, 128
SSD_HEADS, SSD_HEAD_DIM, SSD_GROUPS, SSD_STATE, SSD_CONV, SSD_CHUNK = 12, 64, 2, 128, 5, 128
SSD_INNER = SSD_HEADS * SSD_HEAD_DIM
SSD_CONV_DIM = SSD_INNER + 2 * SSD_GROUPS * SSD_STATE
SSD_GROUP_HEADS = SSD_HEADS // SSD_GROUPS
SSD_GROUP_WIDTH = SSD_GROUP_HEADS * SSD_HEAD_DIM
IN_SPLITS = (GQA_WIDTH, GQA_KV_WIDTH, GQA_KV_WIDTH, MLA_Q_LORA, MLA_KV_LORA, MLA_ROPE_DIM, SSD_INNER, SSD_CONV_DIM,
             2 * SSD_HEADS)
IN_COLS = sum(IN_SPLITS)
LANES = 128
IN_COLS_PAD = -(-IN_COLS // LANES) * LANES

ADAM_LR, ADAM_B1, ADAM_B2, ADAM_EPS, ADAM_WD, ADAM_STEP = 0.001, 0.9, 0.999, 1e-08, 0.01, 10

N_CHIPS = 4
N_DEV = 8
TILE_BYTES = 2 * 1024 * 1024


def _pick(n, cands):
    for t in cands:
        if n % t == 0:
            return t
    return n


def _row_tile(rows, row_bytes):
    for t in (2048, 1024, 512, 256, 128, 64, 32, 16, 8):
        if rows % t == 0 and t * row_bytes <= TILE_BYTES:
            return t
    return rows


MM_VMEM_BUDGET = 36 * 1024 * 1024
MM_VMEM_LIMIT = 56 * 1024 * 1024
MM_MAX_TILE = 2048
MM_MAX_K_TILE = 4096
MXU_DIM = 256
HBM_BYTES_PER_US = 3.0e6
MXU_FLOPS_PER_US = 9.0e8
STEP_US = 0.35


def _tile_cands(d, cap):
    if d % LANES:
        return [d]
    return [t for t in range(LANES, min(d, cap) + 1, LANES) if d % t == 0] or [d]


def _mm_tiles(m, n, kdim):
    up = lambda t: -(-t // MXU_DIM) * MXU_DIM
    best = None
    for tm in _tile_cands(m, MM_MAX_TILE):
        for tn in _tile_cands(n, MM_MAX_TILE):
            for tk in _tile_cands(kdim, MM_MAX_K_TILE):
                if 2 * (tm * tk * 2 + tk * tn * 2 + tm * tn * 4) > MM_VMEM_BUDGET:
                    continue
                ni, nj, nk = m // tm, n // tn, kdim // tk
                a_reads = 1 if nk == 1 else nj
                b_reads = 1 if (nk == 1 and nj == 1) else ni
                hbm = (m * kdim * 2 * a_reads + kdim * n * 2 * b_reads + m * n * 4) / HBM_BYTES_PER_US
                mxu = ni * nj * nk * 2.0 * max(tm, 8) * up(tn) * up(tk) / MXU_FLOPS_PER_US
                cost = max(hbm, mxu) + 0.25 * min(hbm, mxu) + ni * nj * nk * STEP_US
                if best is None or cost < best[0]:
                    best = (cost, tm, tn, tk)
    return best[1:]


def _ride(plan, refs, first, last, compute):
    if plan is None:
        compute()
        return
    ins, outs, sems = plan.split(refs)

    @pl.when(first)
    def _():
        plan.start(ins, outs, sems)

    compute()

    @pl.when(last)
    def _():
        plan.finish(ins, outs, sems)


def _plan_specs(plan):
    if plan is None:
        return [], [], [], [], []
    in_specs, out_specs, scratch = plan.specs()
    return in_specs, out_specs, plan.out_shapes, scratch, plan.inputs


def _matmul(a, b, ta=False, tb=False, name="mm", plan=None):
    assert a.dtype == BF16 and b.dtype == BF16, (a.dtype, b.dtype)
    if ta:
        kdim, m = a.shape
    else:
        m, kdim = a.shape
    if tb:
        n, k2 = b.shape
    else:
        k2, n = b.shape
    assert kdim == k2, (a.shape, b.shape, ta, tb)
    tm, tn, tk = _mm_tiles(m, n, kdim)
    ni, nj, nk = m // tm, n // tn, kdim // tk
    dn = (((0 if ta else 1,), (1 if tb else 0,)), ((), ()))
    p_in, p_out, p_shapes, p_scratch, p_args = _plan_specs(plan)

    def body(a_ref, b_ref, *rest):
        o_ref = rest[len(p_in)]
        i, j, k = pl.program_id(0), pl.program_id(1), pl.program_id(2)

        def compute():
            part = lax.dot_general(a_ref[...], b_ref[...], dn, preferred_element_type=F32)
            if nk == 1:
                o_ref[...] = part
            else:
                @pl.when(k == 0)
                def _():
                    o_ref[...] = part

                @pl.when(k > 0)
                def _():
                    o_ref[...] += part

        _ride(plan, rest[:len(p_in)] + rest[len(p_in) + 1:], (i == 0) & (j == 0) & (k == 0),
              (i == ni - 1) & (j == nj - 1) & (k == nk - 1), compute)

    a_spec = pl.BlockSpec((tk, tm), lambda i, j, k: (k, i)) if ta else pl.BlockSpec((tm, tk), lambda i, j, k: (i, k))
    b_spec = pl.BlockSpec((tn, tk), lambda i, j, k: (j, k)) if tb else pl.BlockSpec((tk, tn), lambda i, j, k: (k, j))
    outs = pl.pallas_call(
        body, name=name, grid=(ni, nj, nk),
        in_specs=[a_spec, b_spec] + p_in, out_specs=[pl.BlockSpec((tm, tn), lambda i, j, k: (i, j))] + p_out,
        out_shape=[jax.ShapeDtypeStruct((m, n), F32)] + p_shapes, scratch_shapes=p_scratch,
        compiler_params=pltpu.CompilerParams(
            dimension_semantics=("arbitrary" if plan is not None else "parallel", "arbitrary", "arbitrary"),
            vmem_limit_bytes=MM_VMEM_LIMIT),
    )(a, b, *p_args)
    return outs[0] if plan is None else (outs[0], list(outs[1:]))


@jax.custom_vjp
def linear(x, w):
    return _matmul(x.astype(BF16), w.astype(BF16), name="linear_fwd")


def _linear_fwd(x, w):
    xb, wb = x.astype(BF16), w.astype(BF16)
    return _matmul(xb, wb, name="linear_fwd"), (xb, wb)


def _linear_bwd(res, dy):
    xb, wb = res
    dyb = dy.astype(BF16)
    return _matmul(dyb, wb, tb=True, name="linear_dx"), _matmul(xb, dyb, ta=True, name="linear_dw")


linear.defvjp(_linear_fwd, _linear_bwd)


def _rms_fwd_call(x, g, groups):
    rows, cols = x.shape
    d = cols // groups
    tr = _row_tile(rows, cols * 4)

    def body(x_ref, g_ref, y_ref):
        for gi in range(groups):
            sl = slice(gi * d, (gi + 1) * d)
            xs = x_ref[:, sl]
            r = lax.rsqrt(jnp.mean(xs * xs, axis=-1, keepdims=True) + EPS)
            y_ref[:, sl] = xs * r * g_ref[:, sl]

    return pl.pallas_call(
        body, name="rms_fwd", grid=(rows // tr,),
        in_specs=[pl.BlockSpec((tr, cols), lambda i: (i, 0)), pl.BlockSpec((1, cols), lambda i: (0, 0))],
        out_specs=pl.BlockSpec((tr, cols), lambda i: (i, 0)),
        out_shape=jax.ShapeDtypeStruct((rows, cols), F32),
        compiler_params=pltpu.CompilerParams(dimension_semantics=("parallel",)),
    )(x, g)


def _rms_bwd_call(x, g, dy, groups):
    rows, cols = x.shape
    d = cols // groups
    tr = _row_tile(rows, cols * 4)

    def body(x_ref, g_ref, dy_ref, dx_ref, dg_ref):
        @pl.when(pl.program_id(0) == 0)
        def _():
            dg_ref[...] = jnp.zeros_like(dg_ref)

        for gi in range(groups):
            sl = slice(gi * d, (gi + 1) * d)
            xs = x_ref[:, sl]
            dys = dy_ref[:, sl]
            r = lax.rsqrt(jnp.mean(xs * xs, axis=-1, keepdims=True) + EPS)
            xhat = xs * r
            dg_ref[:, sl] += jnp.sum(dys * xhat, axis=0, keepdims=True)
            dxhat = dys * g_ref[:, sl]
            dx_ref[:, sl] = r * (dxhat - xhat * jnp.mean(dxhat * xhat, axis=-1, keepdims=True))

    return pl.pallas_call(
        body, name="rms_bwd", grid=(rows // tr,),
        in_specs=[pl.BlockSpec((tr, cols), lambda i: (i, 0)), pl.BlockSpec((1, cols), lambda i: (0, 0)),
                  pl.BlockSpec((tr, cols), lambda i: (i, 0))],
        out_specs=[pl.BlockSpec((tr, cols), lambda i: (i, 0)), pl.BlockSpec((1, cols), lambda i: (0, 0))],
        out_shape=[jax.ShapeDtypeStruct((rows, cols), F32), jax.ShapeDtypeStruct((1, cols), F32)],
        compiler_params=pltpu.CompilerParams(dimension_semantics=("arbitrary",)),
    )(x, g, dy)


@functools.partial(jax.custom_vjp, nondiff_argnums=(2,))
def rms_norm(x, g, groups):
    return _rms_fwd_call(x, g, groups)


def _rms_norm_fwd(x, g, groups):
    return _rms_fwd_call(x, g, groups), (x, g)


def _rms_norm_bwd(groups, res, dy):
    x, g = res
    dx, dg = _rms_bwd_call(x, g, dy, groups)
    return dx, dg


rms_norm.defvjp(_rms_norm_fwd, _rms_norm_bwd)


NT_DIMS = (((1,), (1,)), ((), ()))
TN_DIMS = (((0,), (0,)), ((), ()))


def _softmax_rows(q, k, scale):
    s = lax.dot_general(q, k, NT_DIMS, preferred_element_type=F32) * scale
    p = jnp.exp(s - jnp.max(s, axis=-1, keepdims=True))
    return p / jnp.sum(p, axis=-1, keepdims=True)


def _attn_fwd_call(q, k, v, scale):
    b, h, s, dk = q.shape
    hkv, dv = k.shape[1], v.shape[3]
    rep = h // hkv
    tq = _pick(s, (256, 128))

    def body(q_ref, k_ref, v_ref, o_ref):
        p = _softmax_rows(q_ref[0, 0], k_ref[0, 0], scale)
        o_ref[0, 0] = jnp.dot(p.astype(BF16), v_ref[0, 0], preferred_element_type=F32)

    return pl.pallas_call(
        body, name="attn_fwd", grid=(b, h, s // tq),
        in_specs=[pl.BlockSpec((1, 1, tq, dk), lambda bi, hi, qi: (bi, hi, qi, 0)),
                  pl.BlockSpec((1, 1, s, dk), lambda bi, hi, qi: (bi, hi // rep, 0, 0)),
                  pl.BlockSpec((1, 1, s, dv), lambda bi, hi, qi: (bi, hi // rep, 0, 0))],
        out_specs=pl.BlockSpec((1, 1, tq, dv), lambda bi, hi, qi: (bi, hi, qi, 0)),
        out_shape=jax.ShapeDtypeStruct((b, h, s, dv), F32),
        compiler_params=pltpu.CompilerParams(dimension_semantics=("parallel", "parallel", "parallel")),
    )(q, k, v)


def _attn_bwd_call(q, k, v, do, scale):
    b, h, s, dk = q.shape
    hkv, dv = k.shape[1], v.shape[3]
    rep = h // hkv
    tq = _pick(s, (256, 128))

    def body(q_ref, k_ref, v_ref, do_ref, dq_ref, dk_ref, dv_ref):
        @pl.when((pl.program_id(2) == 0) & (pl.program_id(3) == 0))
        def _():
            dk_ref[...] = jnp.zeros_like(dk_ref)
            dv_ref[...] = jnp.zeros_like(dv_ref)

        qb = q_ref[0, 0]
        kb = k_ref[0, 0]
        vb = v_ref[0, 0]
        dob = do_ref[0, 0]
        p = _softmax_rows(qb, kb, scale)
        dp = lax.dot_general(dob, vb, NT_DIMS, preferred_element_type=F32)
        ds = (p * (dp - jnp.sum(p * dp, axis=-1, keepdims=True)) * scale).astype(BF16)
        dq_ref[0, 0] = jnp.dot(ds, kb, preferred_element_type=F32)
        dk_ref[0, 0] += lax.dot_general(ds, qb, TN_DIMS, preferred_element_type=F32)
        dv_ref[0, 0] += lax.dot_general(p.astype(BF16), dob, TN_DIMS, preferred_element_type=F32)

    return pl.pallas_call(
        body, name="attn_bwd", grid=(b, hkv, rep, s // tq),
        in_specs=[pl.BlockSpec((1, 1, tq, dk), lambda bi, gi, ri, qi: (bi, gi * rep + ri, qi, 0)),
                  pl.BlockSpec((1, 1, s, dk), lambda bi, gi, ri, qi: (bi, gi, 0, 0)),
                  pl.BlockSpec((1, 1, s, dv), lambda bi, gi, ri, qi: (bi, gi, 0, 0)),
                  pl.BlockSpec((1, 1, tq, dv), lambda bi, gi, ri, qi: (bi, gi * rep + ri, qi, 0))],
        out_specs=[pl.BlockSpec((1, 1, tq, dk), lambda bi, gi, ri, qi: (bi, gi * rep + ri, qi, 0)),
                   pl.BlockSpec((1, 1, s, dk), lambda bi, gi, ri, qi: (bi, gi, 0, 0)),
                   pl.BlockSpec((1, 1, s, dv), lambda bi, gi, ri, qi: (bi, gi, 0, 0))],
        out_shape=[jax.ShapeDtypeStruct(q.shape, F32), jax.ShapeDtypeStruct(k.shape, F32),
                   jax.ShapeDtypeStruct(v.shape, F32)],
        compiler_params=pltpu.CompilerParams(
            dimension_semantics=("parallel", "parallel", "arbitrary", "arbitrary")),
    )(q, k, v, do)


@functools.partial(jax.custom_vjp, nondiff_argnums=(3,))
def attention(q, k, v, scale):
    return _attn_fwd_call(q.astype(BF16), k.astype(BF16), v.astype(BF16), scale)


def _attention_fwd(q, k, v, scale):
    qb, kb, vb = q.astype(BF16), k.astype(BF16), v.astype(BF16)
    return _attn_fwd_call(qb, kb, vb, scale), (qb, kb, vb)


def _attention_bwd(scale, res, do):
    qb, kb, vb = res
    return tuple(_attn_bwd_call(qb, kb, vb, do.astype(BF16), scale))


attention.defvjp(_attention_fwd, _attention_bwd)


CONV_COL_TILE = 256
CONV_PACK_ROWS = 8


def _shifted(x, off, rows):
    if off == 0:
        return x
    s = x.shape[0]
    rolled = pltpu.roll(x, (-off) % s, 0)
    valid = (rows + off >= 0) & (rows + off < s)
    return jnp.where(valid, rolled, 0.0)


def _conv_pre(x, wb_ref, rows):
    z = jnp.zeros_like(x) + wb_ref[SSD_CONV:SSD_CONV + 1, :]
    for j in range(SSD_CONV):
        z = z + wb_ref[j:j + 1, :] * _shifted(x, j - SSD_CONV // 2, rows)
    return z


def _conv_fwd_call(x, wb):
    b, s, c = x.shape
    tc = _pick(c, (CONV_COL_TILE, LANES))

    def body(x_ref, wb_ref, y_ref):
        xv = x_ref[0]
        rows = lax.broadcasted_iota(jnp.int32, xv.shape, 0)
        z = _conv_pre(xv, wb_ref, rows)
        y_ref[0] = z * jax.nn.sigmoid(z)

    return pl.pallas_call(
        body, name="conv_fwd", grid=(b, c // tc),
        in_specs=[pl.BlockSpec((1, s, tc), lambda bi, ci: (bi, 0, ci)),
                  pl.BlockSpec((CONV_PACK_ROWS, tc), lambda bi, ci: (0, ci))],
        out_specs=pl.BlockSpec((1, s, tc), lambda bi, ci: (bi, 0, ci)),
        out_shape=jax.ShapeDtypeStruct(x.shape, F32),
        compiler_params=pltpu.CompilerParams(dimension_semantics=("parallel", "parallel")),
    )(x, wb)


def _conv_bwd_call(x, wb, dy):
    b, s, c = x.shape
    tc = _pick(c, (CONV_COL_TILE, LANES))

    def body(x_ref, wb_ref, dy_ref, dx_ref, dwb_ref):
        xv = x_ref[0]
        rows = lax.broadcasted_iota(jnp.int32, xv.shape, 0)
        z = _conv_pre(xv, wb_ref, rows)
        sg = jax.nn.sigmoid(z)
        dz = dy_ref[0] * (sg * (1.0 + z * (1.0 - sg)))
        dx = jnp.zeros_like(xv)
        for j in range(SSD_CONV):
            off = j - SSD_CONV // 2
            dx = dx + wb_ref[j:j + 1, :] * _shifted(dz, -off, rows)
            dwb_ref[0, j:j + 1, :] = jnp.sum(dz * _shifted(xv, off, rows), axis=0, keepdims=True)
        dx_ref[0] = dx
        dwb_ref[0, SSD_CONV:SSD_CONV + 1, :] = jnp.sum(dz, axis=0, keepdims=True)
        dwb_ref[0, SSD_CONV + 1:, :] = jnp.zeros((CONV_PACK_ROWS - SSD_CONV - 1, dz.shape[1]), F32)

    return pl.pallas_call(
        body, name="conv_bwd", grid=(b, c // tc),
        in_specs=[pl.BlockSpec((1, s, tc), lambda bi, ci: (bi, 0, ci)),
                  pl.BlockSpec((CONV_PACK_ROWS, tc), lambda bi, ci: (0, ci)),
                  pl.BlockSpec((1, s, tc), lambda bi, ci: (bi, 0, ci))],
        out_specs=[pl.BlockSpec((1, s, tc), lambda bi, ci: (bi, 0, ci)),
                   pl.BlockSpec((1, CONV_PACK_ROWS, tc), lambda bi, ci: (bi, 0, ci))],
        out_shape=[jax.ShapeDtypeStruct(x.shape, F32), jax.ShapeDtypeStruct((b, CONV_PACK_ROWS, c), F32)],
        compiler_params=pltpu.CompilerParams(dimension_semantics=("parallel", "parallel")),
    )(x, wb, dy)


@jax.custom_vjp
def conv_silu(x, wb):
    return _conv_fwd_call(x, wb)


def _conv_silu_fwd(x, wb):
    return _conv_fwd_call(x, wb), (x, wb)


def _conv_silu_bwd(res, dy):
    x, wb = res
    dx, dwb = _conv_bwd_call(x, wb, dy)
    return dx, jnp.sum(dwb, axis=0)


conv_silu.defvjp(_conv_silu_fwd, _conv_silu_bwd)


SSD_PAIRS = SSD_GROUP_HEADS // 2
NEG_INF = -1e30


def _ssd_common(x_ref, dtx_ref, dtt_ref, anx_ref, anc_ref, b_ref, c_ref, reverse):
    L = SSD_CHUNK
    xv = x_ref[0]
    dt = dtx_ref[0]
    ri = lax.broadcasted_iota(jnp.int32, (L, L), 0)
    ci = lax.broadcasted_iota(jnp.int32, (L, L), 1)
    causal = (ri <= ci) if reverse else (ri >= ci)
    tri = causal.astype(F32)
    a_cs = jnp.dot(tri, dt * anx_ref[...], precision=HIGHEST, preferred_element_type=F32)
    a_row = dtt_ref[0, 0] * anc_ref[0]
    acs_row = lax.dot_general(a_row, tri, NT_DIMS, precision=HIGHEST, preferred_element_type=F32)
    xd = xv * dt
    bmat = b_ref[0].astype(BF16)
    cmat = c_ref[0].astype(BF16)
    gmat = lax.dot_general(cmat, bmat, NT_DIMS, preferred_element_type=F32)
    return xv, dt, causal, tri, a_cs, acs_row, xd, bmat, cmat, gmat


def _ssd_lambda(a_cs, acs_row, causal, h):
    col = a_cs[:, h * SSD_HEAD_DIM:h * SSD_HEAD_DIM + 1]
    row = acs_row[h:h + 1, :]
    return jnp.exp(jnp.where(causal, col - row, NEG_INF))


def _ssd_fwd_call(x, dtx, dtt, anx, anc, bm, cm, reverse):
    b, s, _ = x.shape
    L, N, GW = SSD_CHUNK, SSD_STATE, SSD_GROUP_WIDTH
    nc = s // L
    end = 0 if reverse else L - 1

    def body(x_ref, dtx_ref, dtt_ref, anx_ref, anc_ref, b_ref, c_ref, y_ref, hs_ref, state):
        @pl.when(pl.program_id(2) == 0)
        def _():
            state[...] = jnp.zeros_like(state)

        xv, dt, causal, tri, a_cs, acs_row, xd, bmat, cmat, gmat = _ssd_common(
            x_ref, dtx_ref, dtt_ref, anx_ref, anc_ref, b_ref, c_ref, reverse)
        hin = state[...]
        hs_ref[0, 0, 0] = hin
        y_off = jnp.dot(cmat, hin.astype(BF16), preferred_element_type=F32) * jnp.exp(a_cs)
        a_end = a_cs[end:end + 1, :]
        s_new = lax.dot_general(bmat, (xd * jnp.exp(a_end - a_cs)).astype(BF16), TN_DIMS, preferred_element_type=F32)
        state[...] = jnp.exp(a_end) * hin + s_new
        lane = lax.broadcasted_iota(jnp.int32, (L, LANES), 1)
        for pr in range(SSD_PAIRS):
            sl = slice(pr * LANES, (pr + 1) * LANES)
            xdp = xd[:, sl].astype(BF16)
            w0 = (gmat * _ssd_lambda(a_cs, acs_row, causal, 2 * pr)).astype(BF16)
            w1 = (gmat * _ssd_lambda(a_cs, acs_row, causal, 2 * pr + 1)).astype(BF16)
            y0 = jnp.dot(w0, xdp, preferred_element_type=F32)
            y1 = jnp.dot(w1, xdp, preferred_element_type=F32)
            y_ref[0, :, sl] = jnp.where(lane < SSD_HEAD_DIM, y0, y1) + y_off[:, sl]

    G = SSD_GROUPS
    chunk = (lambda c: nc - 1 - c) if reverse else (lambda c: c)
    seq = lambda bi, gi, c: (bi, chunk(c), gi)
    return pl.pallas_call(
        body, name="ssd_fwd", grid=(b, G, nc),
        in_specs=[pl.BlockSpec((1, L, GW), seq),
                  pl.BlockSpec((1, L, GW), seq),
                  pl.BlockSpec((1, 1, SSD_GROUP_HEADS, L), lambda bi, gi, c: (bi, gi, 0, chunk(c))),
                  pl.BlockSpec((1, GW), lambda bi, gi, c: (0, gi)),
                  pl.BlockSpec((1, SSD_GROUP_HEADS, 1), lambda bi, gi, c: (gi, 0, 0)),
                  pl.BlockSpec((1, L, N), seq),
                  pl.BlockSpec((1, L, N), seq)],
        out_specs=[pl.BlockSpec((1, L, GW), seq),
                   pl.BlockSpec((1, 1, 1, N, GW), lambda bi, gi, c: (bi, gi, chunk(c), 0, 0))],
        out_shape=[jax.ShapeDtypeStruct(x.shape, F32), jax.ShapeDtypeStruct((b, G, nc, N, GW), F32)],
        scratch_shapes=[pltpu.VMEM((N, GW), F32)],
        compiler_params=pltpu.CompilerParams(dimension_semantics=("parallel", "parallel", "arbitrary")),
    )(x, dtx, dtt, anx, anc, bm, cm)


def _ssd_bwd_call(x, dtx, dtt, anx, anc, bm, cm, hs, dy, reverse):
    b, s, _ = x.shape
    L, N, GW = SSD_CHUNK, SSD_STATE, SSD_GROUP_WIDTH
    nc = s // L
    end = 0 if reverse else L - 1

    def body(x_ref, dtx_ref, dtt_ref, anx_ref, anc_ref, b_ref, c_ref, hs_ref, dy_ref,
             dx_ref, ddt_ref, dan_ref, db_ref, dc_ref, dstate):
        @pl.when(pl.program_id(2) == 0)
        def _():
            dstate[...] = jnp.zeros_like(dstate)

        xv, dt, causal, tri, a_cs, acs_row, xd, bmat, cmat, gmat = _ssd_common(
            x_ref, dtx_ref, dtt_ref, anx_ref, anc_ref, b_ref, c_ref, reverse)
        hin = hs_ref[0, 0, 0]
        hinb = hin.astype(BF16)
        dyv = dy_ref[0]
        ds_out = dstate[...]
        dsb = ds_out.astype(BF16)
        eacs = jnp.exp(a_cs)
        a_end = a_cs[end:end + 1, :]
        e_end = jnp.exp(a_end)
        dec = jnp.exp(a_end - a_cs)
        dye = dyv * eacs
        dyeb = dye.astype(BF16)
        xdec = xd * dec
        ch = jnp.dot(cmat, hinb, preferred_element_type=F32)
        bds = jnp.dot(bmat, dsb, preferred_element_type=F32)
        t_state = xdec * bds
        d_aend = jnp.sum(t_state, axis=0, keepdims=True) + e_end * jnp.sum(ds_out * hin, axis=0, keepdims=True)
        dacs = dye * ch - t_state
        dxd_state = bds * dec
        dstate[...] = e_end * ds_out + lax.dot_general(cmat, dyeb, TN_DIMS, preferred_element_type=F32)

        lane = lax.broadcasted_iota(jnp.int32, (L, LANES), 1)
        ones = jnp.full((L, LANES), 1.0 / SSD_HEAD_DIM, F32)
        dg = jnp.zeros((L, L), F32)
        dxd_parts, dacs_parts = [], []
        for pr in range(SSD_PAIRS):
            sl = slice(pr * LANES, (pr + 1) * LANES)
            xdp = xd[:, sl]
            dyp = dyv[:, sl]
            dxd_p = jnp.zeros((L, LANES), F32)
            dacs_p = jnp.zeros((L, LANES), F32)
            for half in range(2):
                mine = (lane < SSD_HEAD_DIM) if half == 0 else (lane >= SSD_HEAD_DIM)
                lam = _ssd_lambda(a_cs, acs_row, causal, 2 * pr + half)
                w = gmat * lam
                xdh = jnp.where(mine, xdp, 0.0).astype(BF16)
                dyh = jnp.where(mine, dyp, 0.0).astype(BF16)
                dw = lax.dot_general(dyh, xdh, NT_DIMS, preferred_element_type=F32)
                dg = dg + dw * lam
                mm = dw * w
                rs = jnp.dot(mm, ones, precision=HIGHEST, preferred_element_type=F32)
                cs = lax.dot_general(mm, ones, TN_DIMS, precision=HIGHEST, preferred_element_type=F32)
                dacs_p = dacs_p + jnp.where(mine, rs - cs, 0.0)
                wtdy = lax.dot_general(w.astype(BF16), dyh, TN_DIMS, preferred_element_type=F32)
                dxd_p = dxd_p + wtdy
            dxd_parts.append(dxd_p)
            dacs_parts.append(dacs_p)
        dxd = jnp.concatenate(dxd_parts, axis=1) + dxd_state
        dacs = dacs + jnp.concatenate(dacs_parts, axis=1)
        last = lax.broadcasted_iota(jnp.int32, dacs.shape, 0) == end
        dacs = dacs + jnp.where(last, d_aend, 0.0)
        da = lax.dot_general(tri, dacs, TN_DIMS, precision=HIGHEST, preferred_element_type=F32)
        dgb = dg.astype(BF16)
        dc_ref[0] = (jnp.dot(dgb, bmat, preferred_element_type=F32)
                     + lax.dot_general(dyeb, hinb, NT_DIMS, preferred_element_type=F32))
        db_ref[0] = (lax.dot_general(dgb, cmat, TN_DIMS, preferred_element_type=F32)
                     + lax.dot_general(xdec.astype(BF16), dsb, NT_DIMS, preferred_element_type=F32))
        dx_ref[0] = dxd * dt
        ddt_ref[0] = da * anx_ref[...] + dxd * xv
        dan_ref[0, 0, 0] = jnp.sum(da * dt, axis=0, keepdims=True)

    G = SSD_GROUPS
    chunk = (lambda c: c) if reverse else (lambda c: nc - 1 - c)
    rev = lambda bi, gi, c: (bi, chunk(c), gi)
    return pl.pallas_call(
        body, name="ssd_bwd", grid=(b, G, nc),
        in_specs=[pl.BlockSpec((1, L, GW), rev),
                  pl.BlockSpec((1, L, GW), rev),
                  pl.BlockSpec((1, 1, SSD_GROUP_HEADS, L), lambda bi, gi, c: (bi, gi, 0, chunk(c))),
                  pl.BlockSpec((1, GW), lambda bi, gi, c: (0, gi)),
                  pl.BlockSpec((1, SSD_GROUP_HEADS, 1), lambda bi, gi, c: (gi, 0, 0)),
                  pl.BlockSpec((1, L, N), rev),
                  pl.BlockSpec((1, L, N), rev),
                  pl.BlockSpec((1, 1, 1, N, GW), lambda bi, gi, c: (bi, gi, chunk(c), 0, 0)),
                  pl.BlockSpec((1, L, GW), rev)],
        out_specs=[pl.BlockSpec((1, L, GW), rev),
                   pl.BlockSpec((1, L, GW), rev),
                   pl.BlockSpec((1, 1, 1, 1, GW), lambda bi, gi, c: (bi, gi, chunk(c), 0, 0)),
                   pl.BlockSpec((1, L, N), rev),
                   pl.BlockSpec((1, L, N), rev)],
        out_shape=[jax.ShapeDtypeStruct(x.shape, F32), jax.ShapeDtypeStruct(x.shape, F32),
                   jax.ShapeDtypeStruct((b, G, nc, 1, GW), F32),
                   jax.ShapeDtypeStruct(bm.shape, F32), jax.ShapeDtypeStruct(cm.shape, F32)],
        scratch_shapes=[pltpu.VMEM((N, GW), F32)],
        compiler_params=pltpu.CompilerParams(dimension_semantics=("parallel", "parallel", "arbitrary")),
    )(x, dtx, dtt, anx, anc, bm, cm, hs, dy)


@functools.partial(jax.custom_vjp, nondiff_argnums=(7,))
def _ssd_scan(x, dtx, dtt, anx, anc, bm, cm, reverse):
    return _ssd_fwd_call(x, dtx, dtt, anx, anc, bm, cm, reverse)[0]


def _ssd_scan_fwd(x, dtx, dtt, anx, anc, bm, cm, reverse):
    y, hs = _ssd_fwd_call(x, dtx, dtt, anx, anc, bm, cm, reverse)
    return y, (x, dtx, dtt, anx, anc, bm, cm, hs)


def _ssd_scan_bwd(reverse, res, dy):
    x, dtx, dtt, anx, anc, bm, cm, hs = res
    dx, ddtx, dan, db, dc = _ssd_bwd_call(x, dtx, dtt, anx, anc, bm, cm, hs, dy, reverse)
    b, g, nc, _, gw = dan.shape
    danx = jnp.sum(dan, axis=(0, 2, 3)).reshape(1, g * gw)
    return dx, ddtx, jnp.zeros_like(dtt), danx, jnp.zeros_like(anc), db, dc


_ssd_scan.defvjp(_ssd_scan_fwd, _ssd_scan_bwd)


def ssd_chunked(xs, dt, a_neg, bm, cm, reverse):
    b, s, _ = xs.shape
    dtx = jnp.repeat(dt, SSD_HEAD_DIM, axis=-1)
    dtt = jnp.transpose(dt, (0, 2, 1)).reshape(b, SSD_GROUPS, SSD_GROUP_HEADS, s)
    anx = jnp.repeat(a_neg, SSD_HEAD_DIM)[None, :]
    anc = a_neg.reshape(SSD_GROUPS, SSD_GROUP_HEADS, 1)
    return _ssd_scan(xs, dtx, dtt, anx, anc, bm, cm, reverse)


def _loss_call(y, t):
    rows, cols = y.shape
    tr = _row_tile(rows, cols * 4)

    def body(y_ref, t_ref, loss_ref, diff_ref):
        @pl.when(pl.program_id(0) == 0)
        def _():
            loss_ref[...] = jnp.zeros_like(loss_ref)

        d = y_ref[...] - t_ref[...]
        diff_ref[...] = d * (1.0 / cols)
        part = jnp.sum(jnp.sum(d * d, axis=1, keepdims=True), axis=0, keepdims=True)
        loss_ref[...] += part * (0.5 / cols)

    return pl.pallas_call(
        body, name="loss_head", grid=(rows // tr,),
        in_specs=[pl.BlockSpec((tr, cols), lambda i: (i, 0)), pl.BlockSpec((tr, cols), lambda i: (i, 0))],
        out_specs=[pl.BlockSpec((1, 1), lambda i: (0, 0)), pl.BlockSpec((tr, cols), lambda i: (i, 0))],
        out_shape=[jax.ShapeDtypeStruct((1, 1), F32), jax.ShapeDtypeStruct((rows, cols), F32)],
        compiler_params=pltpu.CompilerParams(dimension_semantics=("arbitrary",)),
    )(y, t)


@jax.custom_vjp
def loss_head(y, t):
    return _loss_call(y, t)[0][0, 0]


def _loss_head_fwd(y, t):
    loss, diff = _loss_call(y, t)
    return loss[0, 0], diff


def _loss_head_bwd(diff, g):
    return g * diff, jnp.zeros_like(diff)


loss_head.defvjp(_loss_head_fwd, _loss_head_bwd)


def _axial_rope_tables(seq_len, rot_dim):
    rows = seq_len // GRID_W
    row_idx = jnp.repeat(jnp.arange(rows), GRID_W).astype(F32)
    col_idx = jnp.tile(jnp.arange(GRID_W), rows).astype(F32)
    axis_dim = rot_dim // 2
    inv_freq = jnp.power(ROPE_THETA, -jnp.arange(0, axis_dim, 2, dtype=F32) / axis_dim)
    ang_r = row_idx[:, None] * inv_freq[None, :]
    ang_c = col_idx[:, None] * inv_freq[None, :]
    return jnp.cos(ang_r), jnp.sin(ang_r), jnp.cos(ang_c), jnp.sin(ang_c)


def _rotate(x, cos, sin):
    x1, x2 = jnp.split(x, 2, axis=-1)
    cos = cos[:, None, :]
    sin = sin[:, None, :]
    return jnp.concatenate([x1 * cos - x2 * sin, x1 * sin + x2 * cos], axis=-1)


def _apply_axial_rope(x, tables):
    cos_r, sin_r, cos_c, sin_c = tables
    x_row, x_col = jnp.split(x, 2, axis=-1)
    return jnp.concatenate([_rotate(x_row, cos_r, sin_r), _rotate(x_col, cos_c, sin_c)], axis=-1)


def _heads_first(t):
    return jnp.transpose(t, (0, 2, 1, 3))


def _gqa_group(q, k, v, q_norm_g, k_norm_g, rope, b, s):
    q = rms_norm(q, jnp.tile(q_norm_g, GQA_HEADS)[None, :], GQA_HEADS).reshape(b, s, GQA_HEADS, GQA_HEAD_DIM)
    k = rms_norm(k, jnp.tile(k_norm_g, GQA_KV_HEADS)[None, :], GQA_KV_HEADS).reshape(b, s, GQA_KV_HEADS, GQA_HEAD_DIM)
    v = v.reshape(b, s, GQA_KV_HEADS, GQA_HEAD_DIM)
    q = _apply_axial_rope(q, rope)
    k = _apply_axial_rope(k, rope)
    o = attention(_heads_first(q), _heads_first(k), _heads_first(v), GQA_HEAD_DIM ** -0.5)
    return _heads_first(o).reshape(b * s, GQA_WIDTH)


def _mla_group(c_q, c_kv, k_pe, q_norm_g, w_uq, kv_norm_g, w_ukv, rope, b, s):
    q = linear(rms_norm(c_q, q_norm_g[None, :], 1), w_uq).reshape(b, s, MLA_HEADS, MLA_NOPE_DIM + MLA_ROPE_DIM)
    q_nope, q_pe = q[..., :MLA_NOPE_DIM], q[..., MLA_NOPE_DIM:]
    kv = linear(rms_norm(c_kv, kv_norm_g[None, :], 1), w_ukv).reshape(b, s, MLA_HEADS, MLA_NOPE_DIM + MLA_V_DIM)
    k_nope, v = kv[..., :MLA_NOPE_DIM], kv[..., MLA_NOPE_DIM:]
    q_pe = _apply_axial_rope(q_pe, rope)
    k_pe = _apply_axial_rope(k_pe.reshape(b, s, 1, MLA_ROPE_DIM), rope)
    q = jnp.concatenate([q_nope, q_pe], axis=-1)
    k = jnp.concatenate([k_nope, jnp.broadcast_to(k_pe, (b, s, MLA_HEADS, MLA_ROPE_DIM))], axis=-1)
    o = attention(_heads_first(q), _heads_first(k), _heads_first(v), (MLA_NOPE_DIM + MLA_ROPE_DIM) ** -0.5)
    return _heads_first(o).reshape(b * s, MLA_HEADS * MLA_V_DIM)


def _ssd_group(z, xbc, dt_raw, conv_w, conv_b, dt_bias, a_log, d_skip, norm_g, b, s):
    wb = jnp.concatenate([conv_w, conv_b[None, :], jnp.zeros((CONV_PACK_ROWS - SSD_CONV - 1, SSD_CONV_DIM), F32)], axis=0)
    xbc = conv_silu(xbc.reshape(b, s, SSD_CONV_DIM), wb)
    xs = xbc[..., :SSD_INNER]
    bm = xbc[..., SSD_INNER:SSD_INNER + SSD_GROUPS * SSD_STATE]
    cm = xbc[..., SSD_INNER + SSD_GROUPS * SSD_STATE:]
    dt = jax.nn.softplus(dt_raw.reshape(b, s, 2, SSD_HEADS) + dt_bias)
    a_neg = -jnp.exp(a_log)
    y_fwd = ssd_chunked(xs, dt[:, :, 0], a_neg[0], bm, cm, False)
    y_bwd = ssd_chunked(xs, dt[:, :, 1], a_neg[1], bm, cm, True)
    y = y_fwd + y_bwd + xs * jnp.repeat(d_skip, SSD_HEAD_DIM)
    y = y.reshape(b * s, SSD_INNER) * jax.nn.silu(z)
    return rms_norm(y, norm_g[None, :], SSD_GROUPS)


MIXER_WEIGHTS = ('q_norm_g', 'k_norm_g', 'mla_q_norm_g', 'w_uq', 'mla_kv_norm_g', 'w_ukv', 'conv_w', 'conv_b',
                 'dt_bias', 'a_log', 'd_skip', 'ssd_norm_g')


def _mixer(proj, w, rope_a, rope_b, b, s):
    idx = np.cumsum(IN_SPLITS).tolist()
    q_a, k_a, v_a, cq_b, ckv_b, kpe_b, z_c, xbc_c, dt_c = [proj[:, lo:hi] for lo, hi in zip([0] + idx[:-1], idx)]
    o_a = _gqa_group(q_a, k_a, v_a, w["q_norm_g"], w["k_norm_g"], rope_a, b, s)
    o_b = _mla_group(cq_b, ckv_b, kpe_b, w["mla_q_norm_g"], w["w_uq"], w["mla_kv_norm_g"], w["w_ukv"], rope_b, b, s)
    o_c = _ssd_group(z_c, xbc_c, dt_c, w["conv_w"], w["conv_b"], w["dt_bias"], w["a_log"], w["d_skip"],
                     w["ssd_norm_g"], b, s)
    return jnp.concatenate([o_a, o_b, o_c], axis=-1)


def _seq_tile(s, row_bytes):
    return _row_tile(s, row_bytes)


def _normmod_fwd(x, g, scale, shift):
    b, s, d = x.shape
    tr = _seq_tile(s, d * 4)

    def body(x_ref, g_ref, sc_ref, sh_ref, h_ref):
        xv = x_ref[0]
        r = lax.rsqrt(jnp.mean(xv * xv, axis=-1, keepdims=True) + EPS)
        h_ref[0] = (xv * r * g_ref[...] * (1.0 + sc_ref[0]) + sh_ref[0]).astype(BF16)

    act = pl.BlockSpec((1, tr, d), lambda bi, i: (bi, i, 0))
    vec = pl.BlockSpec((1, 1, d), lambda bi, i: (bi, 0, 0))
    return pl.pallas_call(
        body, name="normmod_fwd", grid=(b, s // tr),
        in_specs=[act, pl.BlockSpec((1, d), lambda bi, i: (0, 0)), vec, vec], out_specs=act,
        out_shape=jax.ShapeDtypeStruct((b, s, d), BF16),
        compiler_params=pltpu.CompilerParams(dimension_semantics=("parallel", "parallel")),
    )(x, g, scale, shift)


def _normmod_bwd(x, g, scale, dh, resid):
    b, s, d = x.shape
    tr = _seq_tile(s, d * 4)

    def body(x_ref, g_ref, sc_ref, dh_ref, res_ref, dx_ref, dg_ref, dsc_ref, dsh_ref):
        bi, i = pl.program_id(0), pl.program_id(1)

        @pl.when((bi == 0) & (i == 0))
        def _():
            dg_ref[...] = jnp.zeros_like(dg_ref)

        @pl.when(i == 0)
        def _():
            dsc_ref[...] = jnp.zeros_like(dsc_ref)
            dsh_ref[...] = jnp.zeros_like(dsh_ref)

        xv = x_ref[0]
        dhv = dh_ref[0]
        gv = g_ref[...]
        r = lax.rsqrt(jnp.mean(xv * xv, axis=-1, keepdims=True) + EPS)
        xhat = xv * r
        dsh_ref[0] += jnp.sum(dhv, axis=0, keepdims=True)
        dsc_ref[0] += jnp.sum(dhv * (xhat * gv), axis=0, keepdims=True)
        dn = dhv * (1.0 + sc_ref[0])
        dg_ref[...] += jnp.sum(dn * xhat, axis=0, keepdims=True)
        dxhat = dn * gv
        dx_ref[0] = r * (dxhat - xhat * jnp.mean(dxhat * xhat, axis=-1, keepdims=True)) + res_ref[0]

    act = pl.BlockSpec((1, tr, d), lambda bi, i: (bi, i, 0))
    vec = pl.BlockSpec((1, 1, d), lambda bi, i: (bi, 0, 0))
    gain = pl.BlockSpec((1, d), lambda bi, i: (0, 0))
    return pl.pallas_call(
        body, name="normmod_bwd", grid=(b, s // tr),
        in_specs=[act, gain, vec, act, act], out_specs=[act, gain, vec, vec],
        out_shape=[jax.ShapeDtypeStruct((b, s, d), F32), jax.ShapeDtypeStruct((1, d), F32),
                   jax.ShapeDtypeStruct((b, 1, d), F32), jax.ShapeDtypeStruct((b, 1, d), F32)],
        compiler_params=pltpu.CompilerParams(dimension_semantics=("arbitrary", "arbitrary")),
    )(x, g, scale, dh, resid)


def _gated_add(x, gate, t):
    b, s, d = x.shape
    tr = _seq_tile(s, d * 4)

    def body(x_ref, g_ref, t_ref, o_ref):
        o_ref[0] = x_ref[0] + g_ref[0] * t_ref[0]

    act = pl.BlockSpec((1, tr, d), lambda bi, i: (bi, i, 0))
    vec = pl.BlockSpec((1, 1, d), lambda bi, i: (bi, 0, 0))
    return pl.pallas_call(
        body, name="gated_add", grid=(b, s // tr), in_specs=[act, vec, act], out_specs=act,
        out_shape=jax.ShapeDtypeStruct((b, s, d), F32),
        compiler_params=pltpu.CompilerParams(dimension_semantics=("parallel", "parallel")),
    )(x, gate, t)


def _gated_bwd(dy, gate, t):
    b, s, d = dy.shape
    tr = _seq_tile(s, d * 4)

    def body(dy_ref, g_ref, t_ref, dt_ref, dgate_ref):
        @pl.when(pl.program_id(1) == 0)
        def _():
            dgate_ref[...] = jnp.zeros_like(dgate_ref)

        dyv = dy_ref[0]
        dt_ref[0] = (g_ref[0] * dyv).astype(BF16)
        dgate_ref[0] += jnp.sum(dyv * t_ref[0], axis=0, keepdims=True)

    act = pl.BlockSpec((1, tr, d), lambda bi, i: (bi, i, 0))
    vec = pl.BlockSpec((1, 1, d), lambda bi, i: (bi, 0, 0))
    return pl.pallas_call(
        body, name="gated_bwd", grid=(b, s // tr), in_specs=[act, vec, act], out_specs=[act, vec],
        out_shape=[jax.ShapeDtypeStruct((b, s, d), BF16), jax.ShapeDtypeStruct((b, 1, d), F32)],
        compiler_params=pltpu.CompilerParams(dimension_semantics=("parallel", "arbitrary")),
    )(dy, gate, t)


def _swiglu_fwd(gu, plan=None):
    rows, f2 = gu.shape
    f = f2 // 2
    tr = _row_tile(rows, f2 * 4)
    steps = rows // tr
    p_in, p_out, p_shapes, p_scratch, p_args = _plan_specs(plan)

    def body(gu_ref, *rest):
        a_ref = rest[len(p_in)]
        i = pl.program_id(0)

        def compute():
            gt = gu_ref[:, :f]
            a_ref[...] = (gt * jax.nn.sigmoid(gt) * gu_ref[:, f:]).astype(BF16)

        _ride(plan, rest[:len(p_in)] + rest[len(p_in) + 1:], i == 0, i == steps - 1, compute)

    outs = pl.pallas_call(
        body, name="swiglu_fwd", grid=(steps,),
        in_specs=[pl.BlockSpec((tr, f2), lambda i: (i, 0))] + p_in,
        out_specs=[pl.BlockSpec((tr, f), lambda i: (i, 0))] + p_out,
        out_shape=[jax.ShapeDtypeStruct((rows, f), BF16)] + p_shapes, scratch_shapes=p_scratch,
        compiler_params=pltpu.CompilerParams(dimension_semantics=("arbitrary" if plan is not None else "parallel",)),
    )(gu, *p_args)
    return outs[0] if plan is None else (outs[0], list(outs[1:]))


def _swiglu_bwd(gu, dact):
    rows, f2 = gu.shape
    f = f2 // 2
    tr = _row_tile(rows, f2 * 4)

    def body(gu_ref, da_ref, dgu_ref):
        gt = gu_ref[:, :f]
        up = gu_ref[:, f:]
        da = da_ref[...]
        sg = jax.nn.sigmoid(gt)
        dgu_ref[:, :f] = (da * up * (sg * (1.0 + gt * (1.0 - sg)))).astype(BF16)
        dgu_ref[:, f:] = (da * gt * sg).astype(BF16)

    return pl.pallas_call(
        body, name="swiglu_bwd", grid=(rows // tr,),
        in_specs=[pl.BlockSpec((tr, f2), lambda i: (i, 0)), pl.BlockSpec((tr, f), lambda i: (i, 0))],
        out_specs=pl.BlockSpec((tr, f2), lambda i: (i, 0)),
        out_shape=jax.ShapeDtypeStruct((rows, f2), BF16),
        compiler_params=pltpu.CompilerParams(dimension_semantics=("parallel",)),
    )(gu, dact)


class _Gathered:
    def __init__(self, shards):
        self.shards, self.full = shards, {}

    def plan(self, keys):
        return _gather_plan([self.shards[k] for k in keys])

    def store(self, keys, outs):
        for key, out in zip(keys, outs):
            name = key[0]
            g = out.reshape((N_CHIPS,) + self.shards[key].shape)
            if name in COL_SHARDED:
                full = _cols_full(g)
                if name == 'w_in':
                    full = jnp.pad(full, ((0, 0), (0, IN_COLS_PAD - IN_COLS)))
                if name in ('w_uq', 'w_ukv'):
                    full = full.astype(F32)
            else:
                full = g.reshape(g.shape[0] * g.shape[1], g.shape[2])
            self.full[key] = full

    def carry(self, keys, fn):
        if not keys:
            return fn(None)
        res, outs = fn(self.plan(keys))
        self.store(keys, outs)
        return res


def _gather_schedule(depth):
    every = [(n, l) for l in range(depth) for n in ('w_uq', 'w_ukv')]
    sched = {'first': [('w_in', 0), ('w_gate_up', 0)] + every}
    for l in range(depth):
        sched[('w_in_fwd', l)] = [('w_out', l)] + ([('w_down', 0)] if l == 0 else [])
        if l + 1 < depth:
            sched[('w_gate_up_fwd', l)] = [('w_gate_up', l + 1)]
            sched[('swiglu_fwd', l)] = [('w_in', l + 1)]
            sched[('w_down_fwd', l)] = [('w_down', l + 1)]
    return sched


GATE_UP_PIECES = 2


class _Reducer:
    def __init__(self):
        self.parts, self.recv, self.result = {}, {}, {}

    def add(self, key, grad):
        name = key[0]
        if name in COL_SHARDED:
            blocks = _cols_split(grad[:, :IN_COLS] if name == 'w_in' else grad)
        else:
            blocks = grad.reshape(N_CHIPS, grad.shape[0] // N_CHIPS, grad.shape[1])
        self.parts[key] = _rs_parts(blocks)
        self.recv[key] = []

    def pieces(self, key):
        rows = self.parts[key].shape[1]
        n = GATE_UP_PIECES if key[0] == 'w_gate_up' else 1
        return [(key, i * (rows // n), rows // n) for i in range(n)]

    def plan(self, jobs):
        return _chip_exchange_plan([(self.parts[key], row0, rows) for key, row0, rows in jobs])

    def store(self, jobs, outs):
        for (key, row0, rows), out in zip(jobs, outs):
            self.recv[key].append((row0, out))
            if len(self.recv[key]) == len(self.pieces(key)):
                self.result[key] = _rs_result(self.parts[key], sorted(self.recv[key], key=lambda t: t[0]))

    def carry(self, keys, fn, piece=None):
        jobs = [j for key in keys for j in self.pieces(key)]
        if piece is not None:
            jobs = [j for key in keys for j in self.pieces(key)[piece:piece + 1]]
        if not jobs:
            return fn(None)
        res, outs = fn(self.plan(jobs))
        self.store(jobs, outs)
        return res

    def flush(self):
        jobs = [j for key in self.parts for j in self.pieces(key)
                if key not in self.result and j[1] not in [r for r, _ in self.recv[key]]]
        if jobs:
            self.store(jobs, _run_plan(self.plan(jobs), "rs_chip_exchange"))


def _layer_fwd(x, mod, w, gathered, l, sched, rope_a, rope_b):
    b, s, d = x.shape
    m = b * s
    shift1, scale1, gate1, shift2, scale2, gate2 = [t[:, None, :] for t in jnp.split(mod, 6, axis=-1)]
    g1, g2 = w["norm1_g"][None, :], w["norm2_g"][None, :]
    full = lambda n: gathered.full[(n, l)]
    h1 = _normmod_fwd(x, g1, scale1, shift1).reshape(m, d)
    proj = gathered.carry(sched.get(('w_in_fwd', l)), lambda p: _matmul(h1, full('w_in'), name="w_in_fwd", plan=p))
    mixer_w = {n: (full(n) if n in COL_SHARDED else w[n]) for n in MIXER_WEIGHTS}
    o, mixer_vjp = jax.vjp(lambda p, mw: _mixer(p, mw, rope_a, rope_b, b, s), proj, mixer_w)
    o = o.astype(BF16)
    mix = _matmul(o, full('w_out'), name="w_out_fwd").reshape(b, s, d)
    x_mid = _gated_add(x, gate1, mix)
    h2 = _normmod_fwd(x_mid, g2, scale2, shift2).reshape(m, d)
    gu = gathered.carry(sched.get(('w_gate_up_fwd', l)),
                        lambda p: _matmul(h2, full('w_gate_up'), name="w_gate_up_fwd", plan=p))
    act = gathered.carry(sched.get(('swiglu_fwd', l)), lambda p: _swiglu_fwd(gu, plan=p))
    ffn = gathered.carry(sched.get(('w_down_fwd', l)), lambda p: _matmul(act, full('w_down'), name="w_down_fwd", plan=p))
    ffn = ffn.reshape(b, s, d)
    x_out = _gated_add(x_mid, gate2, ffn)
    res = (x, x_mid, h1, h2, o, mix, gu, act, ffn, mixer_vjp, scale1, gate1, scale2, gate2, g1, g2)
    return x_out, res


def _layer_bwd(res, gathered, reducer, l, depth, dx_out):
    x, x_mid, h1, h2, o, mix, gu, act, ffn, mixer_vjp, scale1, gate1, scale2, gate2, g1, g2 = res
    b, s, d = x.shape
    m = b * s
    full = lambda n: gathered.full[(n, l)]
    above = l + 1 < depth
    dffn, dgate2 = _gated_bwd(dx_out, gate2, ffn)
    dffn = dffn.reshape(m, d)
    dact = reducer.carry([('w_out', l + 1), ('w_uq', l + 1), ('w_ukv', l + 1)] if above else [],
                         lambda p: _matmul(dffn, full('w_down'), tb=True, name="w_down_dx", plan=p))
    dw = reducer.carry([('w_in', l + 1)] if above else [],
                       lambda p: _matmul(act, dffn, ta=True, name="w_down_dw", plan=p))
    reducer.add(('w_down', l), dw)
    dgu = _swiglu_bwd(gu, dact)
    dh2 = reducer.carry([('w_down', l)], lambda p: _matmul(dgu, full('w_gate_up'), tb=True, name="w_gate_up_dx", plan=p))
    dh2 = dh2.reshape(b, s, d)
    reducer.add(('w_gate_up', l), _matmul(h2, dgu, ta=True, name="w_gate_up_dw"))
    dx_mid, dg2, dscale2, dshift2 = _normmod_bwd(x_mid, g2, scale2, dh2, dx_out)
    dmix, dgate1 = _gated_bwd(dx_mid, gate1, mix)
    dmix = dmix.reshape(m, d)
    do = _matmul(dmix, full('w_out'), tb=True, name="w_out_dx")
    reducer.add(('w_out', l), _matmul(o, dmix, ta=True, name="w_out_dw"))
    dproj, grads = mixer_vjp(do)
    grads = dict(grads)
    reducer.add(('w_uq', l), grads.pop('w_uq'))
    reducer.add(('w_ukv', l), grads.pop('w_ukv'))
    dproj = dproj.astype(BF16)
    dh1 = reducer.carry([('w_gate_up', l)], lambda p: _matmul(dproj, full('w_in'), tb=True, name="w_in_dx", plan=p),
                        piece=0)
    dh1 = dh1.reshape(b, s, d)
    dw = reducer.carry([('w_gate_up', l)], lambda p: _matmul(h1, dproj, ta=True, name="w_in_dw", plan=p), piece=1)
    reducer.add(('w_in', l), dw)
    dx, dg1, dscale1, dshift1 = _normmod_bwd(x, g1, scale1, dh1, dx_mid)
    grads["norm1_g"], grads["norm2_g"] = dg1[0], dg2[0]
    dmod = jnp.concatenate([dshift1, dscale1, dgate1, dshift2, dscale2, dgate2], axis=-1)[:, 0, :]
    return dx, dmod, grads


def _tail_loss(x2, final_norm_g, target2):
    return loss_head(rms_norm(x2, final_norm_g[None, :], 1), target2)


def _forward_backward(x, mod, small, gathered, reducer, final_norm_g, target):
    b, s, d = x.shape
    depth = len(small)
    rope_a = _axial_rope_tables(s, GQA_HEAD_DIM)
    rope_b = _axial_rope_tables(s, MLA_ROPE_DIM)
    sched = _gather_schedule(depth)
    first = sched['first']
    gathered.store(first, _run_plan(gathered.plan(first), "all_gather_chips"))
    saved = []
    for l in range(depth):
        x, res = _layer_fwd(x, mod[l], small[l], gathered, l, sched, rope_a, rope_b)
        saved.append(res)
    loss, (dx2, dfinal) = jax.value_and_grad(_tail_loss, argnums=(0, 1))(
        x.reshape(b * s, d), final_norm_g, target.reshape(b * s, d))
    dx = dx2.reshape(b, s, d)
    dmods, gsmall = [None] * depth, [None] * depth
    for l in reversed(range(depth)):
        dx, dmods[l], gsmall[l] = _layer_bwd(saved[l], gathered, reducer, l, depth, dx)
    return loss, dx, jnp.stack(dmods), gsmall, dfinal


ANY = pl.BlockSpec(memory_space=pl.ANY)


def _flip_if(v, bit):
    return 1 - v if bit else v


def _all_gather_devices(x):
    def body(x_ref, out_ref, send_sems, recv_sems):
        mx, my, mc = lax.axis_index("x"), lax.axis_index("y"), lax.axis_index("c")
        me = 4 * mx + 2 * my + mc
        sends = []
        for k in range(1, N_DEV):
            peer = (_flip_if(mx, k & 4), _flip_if(my, k & 2), _flip_if(mc, k & 1))
            cp = pltpu.make_async_remote_copy(src_ref=x_ref, dst_ref=out_ref.at[me], send_sem=send_sems.at[k - 1],
                                              recv_sem=recv_sems.at[k - 1], device_id=peer, device_id_type=MESH)
            cp.start()
            sends.append(cp)
        for k in range(1, N_DEV):
            peer = (_flip_if(mx, k & 4), _flip_if(my, k & 2), _flip_if(mc, k & 1))
            src = 4 * peer[0] + 2 * peer[1] + peer[2]
            pltpu.make_async_remote_copy(src_ref=x_ref, dst_ref=out_ref.at[src], send_sem=send_sems.at[k - 1],
                                         recv_sem=recv_sems.at[k - 1], device_id=peer, device_id_type=MESH).wait_recv()
        for cp in sends:
            cp.wait_send()

    out = pl.pallas_call(
        body, name="all_gather_devices", in_specs=[ANY], out_specs=ANY,
        out_shape=jax.ShapeDtypeStruct((N_DEV,) + x.shape, x.dtype),
        scratch_shapes=[pltpu.SemaphoreType.DMA((N_DEV - 1,)), pltpu.SemaphoreType.DMA((N_DEV - 1,))],
    )(x)
    me = 4 * lax.axis_index("x") + 2 * lax.axis_index("y") + lax.axis_index("c")
    return lax.dynamic_update_index_in_dim(out, x, me, 0)


class _Plan:
    def __init__(self, inputs, out_shapes, sem_counts, start, finish):
        self.inputs, self.out_shapes, self.sem_counts = list(inputs), list(out_shapes), list(sem_counts)
        self.start, self.finish = start, finish

    def specs(self):
        return ([ANY] * len(self.inputs), [ANY] * len(self.out_shapes),
                [pltpu.SemaphoreType.DMA((c,)) for c in self.sem_counts])

    def split(self, refs):
        a, b = len(self.inputs), len(self.inputs) + len(self.out_shapes)
        return refs[:a], refs[a:b], refs[b:]


def _run_plan(plan, name):
    def body(*refs):
        ins, outs, sems = plan.split(refs)
        plan.start(ins, outs, sems)
        plan.finish(ins, outs, sems)

    in_specs, out_specs, scratch = plan.specs()
    return pl.pallas_call(body, name=name, in_specs=in_specs, out_specs=out_specs, out_shape=plan.out_shapes,
                          scratch_shapes=scratch)(*plan.inputs)


def _gather_plan(shards):
    n = len(shards)
    halves = [t.reshape(2, t.shape[0] // 2, t.shape[1]) for t in shards]
    count = (N_CHIPS - 1) * n

    def copies(kind, ins, outs, sems):
        ici_send, ici_recv, d2d_send, d2d_recv, own_send, own_recv = sems
        mx, my, mc = lax.axis_index("x"), lax.axis_index("y"), lax.axis_index("c")
        me = 2 * mx + my
        sibling = (mx, my, 1 - mc)
        if kind == 'own':
            return [pltpu.make_async_remote_copy(src_ref=ins[i], dst_ref=outs[i].at[me], send_sem=own_send.at[i],
                                                 recv_sem=own_recv.at[i], device_id=sibling, device_id_type=MESH)
                    for i in range(n)]
        cps = []
        for k in range(1, N_CHIPS):
            peer = (_flip_if(mx, k & 2), _flip_if(my, k & 1), mc)
            src = 2 * peer[0] + peer[1]
            for i in range(n):
                j = (k - 1) * n + i
                if kind in ('ici', 'landed'):
                    dst = outs[i].at[me, mc] if kind == 'ici' else outs[i].at[src, mc]
                    cps.append(pltpu.make_async_remote_copy(
                        src_ref=ins[i].at[mc], dst_ref=dst, send_sem=ici_send.at[j], recv_sem=ici_recv.at[j],
                        device_id=peer, device_id_type=MESH))
                else:
                    half = outs[i].at[src, mc] if kind == 'fwd' else outs[i].at[src, 1 - mc]
                    cps.append(pltpu.make_async_remote_copy(
                        src_ref=half, dst_ref=half, send_sem=d2d_send.at[j], recv_sem=d2d_recv.at[j],
                        device_id=sibling, device_id_type=MESH))
        return cps

    def start(ins, outs, sems):
        for cp in copies('own', ins, outs, sems) + copies('ici', ins, outs, sems):
            cp.start()

    def finish(ins, outs, sems):
        fwd = copies('fwd', ins, outs, sems)
        for arrived, onward in zip(copies('landed', ins, outs, sems), fwd):
            arrived.wait_recv()
            onward.start()
        own = copies('own', ins, outs, sems)
        for cp in copies('fwd_in', ins, outs, sems) + own:
            cp.wait_recv()
        for cp in own + copies('ici', ins, outs, sems) + fwd:
            cp.wait_send()

    out_shapes = [jax.ShapeDtypeStruct((N_CHIPS,) + t.shape, t.dtype) for t in halves]
    return _Plan(halves, out_shapes, [count] * 4 + [n] * 2, start, finish)


def _sibling_exchange(blocks, name):
    n = len(blocks)

    def body(*refs):
        ins, outs = refs[:n], refs[n:2 * n]
        send_sems, recv_sems = refs[2 * n:]
        mx, my, mc = lax.axis_index("x"), lax.axis_index("y"), lax.axis_index("c")
        cps = []
        for i in range(n):
            cp = pltpu.make_async_remote_copy(src_ref=ins[i], dst_ref=outs[i], send_sem=send_sems.at[i],
                                              recv_sem=recv_sems.at[i], device_id=(mx, my, 1 - mc),
                                              device_id_type=MESH)
            cp.start()
            cps.append(cp)
        for cp in cps:
            cp.wait()

    return pl.pallas_call(
        body, name=name, in_specs=[ANY] * n, out_specs=[ANY] * n,
        out_shape=[jax.ShapeDtypeStruct(t.shape, t.dtype) for t in blocks],
        scratch_shapes=[pltpu.SemaphoreType.DMA((n,)), pltpu.SemaphoreType.DMA((n,))],
    )(*blocks)


def _add_halves(own, recv):
    nb, r, c = own.shape
    tr = _row_tile(r, c * 4)

    def body(g_ref, r_ref, o_ref):
        o_ref[...] = (g_ref[...] + r_ref[...].astype(F32)).astype(BF16)

    spec = pl.BlockSpec((1, tr, c), lambda k, i: (k, i, 0))
    return pl.pallas_call(
        body, name="rs_add_halves", grid=(nb, r // tr), in_specs=[spec, spec], out_specs=spec,
        out_shape=jax.ShapeDtypeStruct((nb, r, c), BF16),
        compiler_params=pltpu.CompilerParams(dimension_semantics=("parallel", "parallel")),
    )(own, recv)


def _chip_exchange_plan(jobs):
    n = len(jobs)
    count = (N_CHIPS - 1) * n

    def copies(ins, outs, sems):
        send_sems, recv_sems = sems
        mx, my, mc = lax.axis_index("x"), lax.axis_index("y"), lax.axis_index("c")
        cps = []
        for k in range(1, N_CHIPS):
            peer = (_flip_if(mx, k & 2), _flip_if(my, k & 1), mc)
            dst_chip = 2 * peer[0] + peer[1]
            for i, (_, row0, rows) in enumerate(jobs):
                j = (k - 1) * n + i
                cps.append(pltpu.make_async_remote_copy(
                    src_ref=ins[i].at[dst_chip, pl.ds(row0, rows)], dst_ref=outs[i].at[k - 1],
                    send_sem=send_sems.at[j], recv_sem=recv_sems.at[j], device_id=peer, device_id_type=MESH))
        return cps

    def start(ins, outs, sems):
        for cp in copies(ins, outs, sems):
            cp.start()

    def finish(ins, outs, sems):
        for cp in copies(ins, outs, sems):
            cp.wait()

    out_shapes = [jax.ShapeDtypeStruct((N_CHIPS - 1, rows, p.shape[2]), p.dtype) for p, _, rows in jobs]
    return _Plan([p for p, _, _ in jobs], out_shapes, [count, count], start, finish)


def _sum_chips(parts, recv, chip, row0):
    _, rows, c = recv.shape
    tr = _row_tile(rows, c * 4)
    assert row0 % tr == 0

    def body(chip_ref, p_ref, r_ref, o_ref):
        acc = p_ref[0].astype(F32)
        for k in range(N_CHIPS - 1):
            acc = acc + r_ref[k].astype(F32)
        o_ref[...] = acc

    return pl.pallas_call(
        body, name="rs_sum_chips",
        grid_spec=pltpu.PrefetchScalarGridSpec(
            num_scalar_prefetch=1, grid=(rows // tr,),
            in_specs=[pl.BlockSpec((1, tr, c), lambda i, chip_ref: (chip_ref[0], i + row0 // tr, 0)),
                      pl.BlockSpec((N_CHIPS - 1, tr, c), lambda i, chip_ref: (0, i, 0))],
            out_specs=pl.BlockSpec((tr, c), lambda i, chip_ref: (i, 0))),
        out_shape=jax.ShapeDtypeStruct((rows, c), F32),
        compiler_params=pltpu.CompilerParams(dimension_semantics=("parallel",)),
    )(chip, parts, recv)


def _sum_leading(t, name):
    nb, r, c = t.shape
    tr = _row_tile(r, c * 4 * nb)

    def body(t_ref, o_ref):
        acc = t_ref[0]
        for k in range(1, nb):
            acc = acc + t_ref[k]
        o_ref[...] = acc

    return pl.pallas_call(
        body, name=name, grid=(r // tr,),
        in_specs=[pl.BlockSpec((nb, tr, c), lambda i: (0, i, 0))],
        out_specs=pl.BlockSpec((tr, c), lambda i: (i, 0)),
        out_shape=jax.ShapeDtypeStruct((r, c), F32),
        compiler_params=pltpu.CompilerParams(dimension_semantics=("parallel",)),
    )(t)


def _rs_parts(grad):
    mc = lax.axis_index("c")
    split = grad.reshape(grad.shape[0], 2, grad.shape[1] // 2, grad.shape[2])
    own = lax.dynamic_index_in_dim(split, mc, axis=1, keepdims=False)
    away = lax.dynamic_index_in_dim(split, 1 - mc, axis=1, keepdims=False).astype(BF16)
    return _add_halves(own, _sibling_exchange([away], "rs_sibling_exchange")[0])


def _rs_result(parts, pieces):
    mc = lax.axis_index("c")
    chip = (2 * lax.axis_index("x") + lax.axis_index("y")).astype(jnp.int32).reshape(1)
    mine = [_sum_chips(parts, recv, chip, row0) for row0, recv in pieces]
    mine = mine[0] if len(mine) == 1 else jnp.concatenate(mine, axis=0)
    theirs = _sibling_exchange([mine], "rs_sibling_swap")[0]
    return jnp.concatenate([jnp.where(mc == 0, mine, theirs), jnp.where(mc == 0, theirs, mine)], axis=0)


def _adamw(w, g, m, v, plan=None):
    shape = w.shape
    cols = shape[-1]
    rows = int(np.prod(shape[:-1])) if len(shape) > 1 else 1
    w2, g2, m2, v2 = [t.reshape(rows, cols) for t in (w, g, m, v)]
    tr = _row_tile(rows, cols * 4 * 4)
    steps = rows // tr
    p_in, p_out, p_shapes, p_scratch, p_args = _plan_specs(plan)

    def body(w_ref, g_ref, m_ref, v_ref, *rest):
        d_ref, mo_ref, vo_ref = rest[len(p_in):len(p_in) + 3]
        i = pl.program_id(0)

        def compute():
            gv = g_ref[...]
            mn = ADAM_B1 * m_ref[...] + (1.0 - ADAM_B1) * gv
            vn = ADAM_B2 * v_ref[...] + (1.0 - ADAM_B2) * (gv * gv)
            m_hat = mn / (1.0 - ADAM_B1 ** ADAM_STEP)
            v_hat = vn / (1.0 - ADAM_B2 ** ADAM_STEP)
            d_ref[...] = -ADAM_LR * (m_hat / (jnp.sqrt(v_hat) + ADAM_EPS) + ADAM_WD * w_ref[...])
            mo_ref[...] = mn
            vo_ref[...] = vn

        _ride(plan, rest[:len(p_in)] + rest[len(p_in) + 3:], i == 0, i == steps - 1, compute)

    spec = pl.BlockSpec((tr, cols), lambda i: (i, 0))
    outs = pl.pallas_call(
        body, name="adamw", grid=(steps,), in_specs=[spec] * 4 + p_in, out_specs=[spec] * 3 + p_out,
        out_shape=[jax.ShapeDtypeStruct((rows, cols), F32)] * 3 + p_shapes, scratch_shapes=p_scratch,
        compiler_params=pltpu.CompilerParams(dimension_semantics=("arbitrary" if plan is not None else "parallel",)),
    )(w2, g2, m2, v2, *p_args)
    res = [t.reshape(shape) for t in outs[:3]]
    return res if plan is None else (res, list(outs[3:]))


WEIGHTS = ['w_ada', 'b_ada', 'norm1_g', 'norm2_g', 'w_in', 'q_norm_g', 'k_norm_g', 'mla_q_norm_g', 'w_uq',
           'mla_kv_norm_g', 'w_ukv', 'conv_w', 'conv_b', 'dt_bias', 'a_log', 'd_skip', 'ssd_norm_g', 'w_out',
           'w_gate_up', 'w_down', 'final_norm_g']
COL_SHARDED = ('w_in', 'w_uq', 'w_ukv', 'w_gate_up')
ROW_SHARDED = ('w_out', 'w_down')
SMALL_LAYER = ('norm1_g', 'norm2_g', 'q_norm_g', 'k_norm_g', 'mla_q_norm_g', 'mla_kv_norm_g', 'conv_w', 'conv_b',
               'dt_bias', 'a_log', 'd_skip', 'ssd_norm_g')


def _pack(parts):
    flat = jnp.concatenate([p.reshape(-1) for p in parts])
    n = flat.shape[0]
    rows = -(-n // (8 * LANES)) * 8
    return jnp.pad(flat, (0, rows * LANES - n)).reshape(rows, LANES)


def _unpack(flat, shapes):
    out, pos = [], 0
    for shp in shapes:
        size = int(np.prod(shp))
        out.append(flat[pos:pos + size].reshape(shp))
        pos += size
    return out


def _cols_full(gathered):
    k, r, c = gathered.shape
    return jnp.transpose(gathered, (1, 0, 2)).reshape(r, k * c)


def _cols_split(full):
    r, c4 = full.shape
    return jnp.transpose(full.reshape(r, N_CHIPS, c4 // N_CHIPS), (1, 0, 2))


def kernel(x, c, w_ada, b_ada, norm1_g, norm2_g, w_in, q_norm_g, k_norm_g, mla_q_norm_g, w_uq, mla_kv_norm_g, w_ukv, conv_w, conv_b, dt_bias, a_log, d_skip, ssd_norm_g, w_out, w_gate_up, w_down, final_norm_g, loss_target, m_w_ada, m_b_ada, m_norm1_g, m_norm2_g, m_w_in, m_q_norm_g, m_k_norm_g, m_mla_q_norm_g, m_w_uq, m_mla_kv_norm_g, m_w_ukv, m_conv_w, m_conv_b, m_dt_bias, m_a_log, m_d_skip, m_ssd_norm_g, m_w_out, m_w_gate_up, m_w_down, m_final_norm_g, v_w_ada, v_b_ada, v_norm1_g, v_norm2_g, v_w_in, v_q_norm_g, v_k_norm_g, v_mla_q_norm_g, v_w_uq, v_mla_kv_norm_g, v_w_ukv, v_conv_w, v_conv_b, v_dt_bias, v_a_log, v_d_skip, v_ssd_norm_g, v_w_out, v_w_gate_up, v_w_down, v_final_norm_g):
    args = dict(locals())
    weights = {n: args[n] for n in WEIGHTS}
    depth = w_in.shape[0]
    bl, s, d = x.shape
    mx, my, mc = lax.axis_index("x"), lax.axis_index("y"), lax.axis_index("c")
    chip = 2 * mx + my
    dev = 2 * chip + mc
    ada_cols = w_ada.shape[-1]
    conv_cols = conv_w.shape[-1]

    first_shapes = [c.shape, conv_w.shape]
    first = _all_gather_devices(_pack([c, conv_w]))
    first = [_unpack(first[i].reshape(-1), first_shapes) for i in range(N_DEV)]
    c_act = jax.nn.silu(jnp.concatenate([f[0] for f in first], axis=0))
    conv_w_full = jnp.concatenate([first[2 * k][1] for k in range(N_CHIPS)], axis=-1)

    b_cols = lax.dynamic_slice_in_dim(b_ada, chip * ada_cols, ada_cols, axis=1)
    c_act_b = c_act.astype(BF16)
    mod_cols = jnp.stack([_matmul(c_act_b, w_ada[l].astype(BF16), name="ada_fwd") + b_cols[l][None, :]
                          for l in range(depth)])
    mod_all = _all_gather_devices(mod_cols.reshape(depth * N_DEV * bl, ada_cols))
    mod_all = mod_all.reshape(N_DEV, depth, N_DEV, bl, ada_cols)
    mod_mine = lax.dynamic_index_in_dim(mod_all, dev, axis=2, keepdims=False)
    mod = jnp.concatenate([mod_mine[2 * k] for k in range(N_CHIPS)], axis=-1)

    big = COL_SHARDED + ROW_SHARDED
    gathered = _Gathered({(n, l): weights[n][l].astype(BF16) for n in big for l in range(depth)})
    reducer = _Reducer()
    small_w = []
    for l in range(depth):
        w = {n: weights[n][l] for n in SMALL_LAYER if n != 'conv_w'}
        w['conv_w'] = conv_w_full[l]
        small_w.append(w)
    loss_local, gx, gmod, glayers, gfinal = _forward_backward(x, mod, small_w, gathered, reducer, final_norm_g,
                                                              loss_target)

    small_parts = [jnp.stack([glayers[l][n] for l in range(depth)]) for n in SMALL_LAYER]
    small_parts += [gfinal, loss_local.reshape(1), gmod]
    small_shapes = [p.shape for p in small_parts]
    last = _all_gather_devices(_pack(small_parts))
    summed = _unpack(_sum_leading(last, "sum_devices").reshape(-1), small_shapes)
    small = dict(zip(SMALL_LAYER, summed[:len(SMALL_LAYER)]))
    g_final, loss, gmod_sum = summed[len(SMALL_LAYER):]
    small['conv_w'] = lax.dynamic_slice_in_dim(small['conv_w'], chip * conv_cols, conv_cols, axis=2)
    gmod_all = jnp.stack([_unpack(last[i].reshape(-1), small_shapes)[-1] for i in range(N_DEV)], axis=1)
    gmod_all = gmod_all.reshape(depth, N_DEV * bl, gmod.shape[-1])
    gmod_cols = lax.dynamic_slice_in_dim(gmod_all, chip * ada_cols, ada_cols, axis=2)
    g_w_ada = jnp.stack([_matmul(c_act_b, gmod_cols[l].astype(BF16), ta=True, name="ada_dw") for l in range(depth)])
    g_b_ada = gmod_sum[:, 0]
    for i in range(1, bl):
        g_b_ada = g_b_ada + gmod_sum[:, i]

    grad = {'w_ada': g_w_ada, 'b_ada': g_b_ada, 'final_norm_g': g_final}
    for n in SMALL_LAYER:
        grad[n] = small[n]

    delta, new_m, new_v = {}, {}, {}
    left = [key for key in reducer.parts if key not in reducer.result and not reducer.recv[key]]
    delta['w_ada'], new_m['w_ada'], new_v['w_ada'] = reducer.carry(
        left, lambda p: _adamw(w_ada, g_w_ada, m_w_ada, v_w_ada, plan=p))
    reducer.flush()
    for n in big:
        grad[n] = jnp.stack([reducer.result[(n, l)] for l in range(depth)])
    for n in WEIGHTS:
        if n != 'w_ada':
            delta[n], new_m[n], new_v[n] = _adamw(weights[n], grad[n], args["m_" + n], args["v_" + n])
    return (loss.reshape(()), gx, *[grad[n] for n in WEIGHTS], *[delta[n] for n in WEIGHTS],
            *[new_m[n] for n in WEIGHTS], *[new_v[n] for n in WEIGHTS])
```

```python
import functools

import numpy as np
import jax
import jax.numpy as jnp
from jax import lax
from jax.experimental import pallas as pl
from jax.experimental.pallas import tpu as pltpu

F32 = jnp.float32
BF16 = jnp.bfloat16
HIGHEST = lax.Precision.HIGHEST
MESH = pl.DeviceIdType.MESH

GRID_W = 64
ROPE_THETA = 10000.0
EPS = 1e-6

GQA_HEADS, GQA_KV_HEADS, GQA_HEAD_DIM = 6, 2, 128
GQA_WIDTH = GQA_HEADS * GQA_HEAD_DIM
GQA_KV_WIDTH = GQA_KV_HEADS * GQA_HEAD_DIM
MLA_HEADS, MLA_Q_LORA, MLA_KV_LORA = 4, 512, 256
MLA_NOPE_DIM, MLA_ROPE_DIM, MLA_V_DIM = 128, 64, 128
SSD_HEADS, SSD_HEAD_DIM, SSD_GROUPS, SSD_STATE, SSD_CONV, SSD_CHUNK = 12, 64, 2, 128, 5, 128
SSD_INNER = SSD_HEADS * SSD_HEAD_DIM
SSD_CONV_DIM = SSD_INNER + 2 * SSD_GROUPS * SSD_STATE
SSD_GROUP_HEADS = SSD_HEADS // SSD_GROUPS
SSD_GROUP_WIDTH = SSD_GROUP_HEADS * SSD_HEAD_DIM
IN_SPLITS = (GQA_WIDTH, GQA_KV_WIDTH, GQA_KV_WIDTH, MLA_Q_LORA, MLA_KV_LORA, MLA_ROPE_DIM, SSD_INNER, SSD_CONV_DIM,
             2 * SSD_HEADS)
IN_COLS = sum(IN_SPLITS)
LANES = 128
N_CHIPS = 4
IN_SHARD = IN_COLS // N_CHIPS
IN_SHARD_PAD = -(-IN_SHARD // LANES) * LANES


def _in_cols(proj, lo, hi):
    parts = []
    for chip in range(lo // IN_SHARD, (hi - 1) // IN_SHARD + 1):
        a, z = max(lo, chip * IN_SHARD), min(hi, (chip + 1) * IN_SHARD)
        base = chip * IN_SHARD_PAD - chip * IN_SHARD
        parts.append(proj[:, base + a:base + z])
    return parts[0] if len(parts) == 1 else jnp.concatenate(parts, axis=-1)

ADAM_LR, ADAM_B1, ADAM_B2, ADAM_EPS, ADAM_WD, ADAM_STEP = 0.001, 0.9, 0.999, 1e-08, 0.01, 10

N_DEV = 8
TILE_BYTES = 2 * 1024 * 1024


def _pick(n, cands):
    for t in cands:
        if n % t == 0:
            return t
    return n


def _row_tile(rows, row_bytes):
    for t in (2048, 1024, 512, 256, 128, 64, 32, 16, 8):
        if rows % t == 0 and t * row_bytes <= TILE_BYTES:
            return t
    return rows


MM_VMEM_BUDGET = 36 * 1024 * 1024
MM_VMEM_LIMIT = 56 * 1024 * 1024
MM_MAX_TILE = 2048
MM_MAX_K_TILE = 4096
MXU_DIM = 256
HBM_BYTES_PER_US = 3.0e6
MXU_FLOPS_PER_US = 9.0e8
STEP_US = 0.35


def _tile_cands(d, cap):
    if d % LANES:
        return [d]
    return [t for t in range(LANES, min(d, cap) + 1, LANES) if d % t == 0] or [d]


def _mm_tiles(m, n, kdim, n_unit=None, k_unit=None):
    up = lambda t: -(-t // MXU_DIM) * MXU_DIM
    best = None
    for tm in _tile_cands(m, MM_MAX_TILE):
        for tn in _tile_cands(n_unit or n, MM_MAX_TILE):
            for tk in _tile_cands(k_unit or kdim, MM_MAX_K_TILE):
                if 2 * (tm * tk * 2 + tk * tn * 2 + tm * tn * 4) > MM_VMEM_BUDGET:
                    continue
                ni, nj, nk = m // tm, n // tn, kdim // tk
                a_reads = 1 if nk == 1 else nj
                b_reads = 1 if (nk == 1 and nj == 1) else ni
                hbm = (m * kdim * 2 * a_reads + kdim * n * 2 * b_reads + m * n * 4) / HBM_BYTES_PER_US
                mxu = ni * nj * nk * 2.0 * max(tm, 8) * up(tn) * up(tk) / MXU_FLOPS_PER_US
                cost = max(hbm, mxu) + 0.25 * min(hbm, mxu) + ni * nj * nk * STEP_US
                if best is None or cost < best[0]:
                    best = (cost, tm, tn, tk)
    return best[1:]


def _ride(plan, refs, first, last, compute):
    if plan is None:
        compute()
        return
    ins, outs, sems = plan.split(refs)

    @pl.when(first)
    def _():
        plan.start(ins, outs, sems)

    compute()

    @pl.when(last)
    def _():
        plan.finish(ins, outs, sems)


def _plan_specs(plan):
    if plan is None:
        return [], [], [], [], []
    in_specs, out_specs, scratch = plan.specs()
    return in_specs, out_specs, plan.out_shapes, scratch, plan.inputs


def _matmul(a, b, ta=False, tb=False, name="mm", plan=None, chips=None):
    assert a.dtype == BF16 and b.dtype == BF16, (a.dtype, b.dtype)
    if ta:
        kdim, m = a.shape
    else:
        m, kdim = a.shape
    n_unit = k_unit = None
    if chips == 'b':
        nb, rows, unit = b.shape
        if tb:
            n, k2, k_unit = rows, nb * unit, unit
        else:
            k2, n, n_unit = rows, nb * unit, unit
    else:
        if tb:
            n, k2 = b.shape
        else:
            k2, n = b.shape
        if chips == 'out':
            n_unit = n // N_CHIPS
    assert kdim == k2, (a.shape, b.shape, ta, tb)
    tm, tn, tk = _mm_tiles(m, n, kdim, n_unit, k_unit)
    ni, nj, nk = m // tm, n // tn, kdim // tk
    dn = (((0 if ta else 1,), (1 if tb else 0,)), ((), ()))
    p_in, p_out, p_shapes, p_scratch, p_args = _plan_specs(plan)

    def body(a_ref, b_ref, *rest):
        o_ref = rest[len(p_in)]
        i, j, k = pl.program_id(0), pl.program_id(1), pl.program_id(2)

        def compute():
            bv = b_ref[0] if chips == 'b' else b_ref[...]
            part = lax.dot_general(a_ref[...], bv, dn, preferred_element_type=F32)
            if chips == 'out':
                part = part[None]
            if nk == 1:
                o_ref[...] = part
            else:
                @pl.when(k == 0)
                def _():
                    o_ref[...] = part

                @pl.when(k > 0)
                def _():
                    o_ref[...] += part

        _ride(plan, rest[:len(p_in)] + rest[len(p_in) + 1:], (i == 0) & (j == 0) & (k == 0),
              (i == ni - 1) & (j == nj - 1) & (k == nk - 1), compute)

    a_spec = pl.BlockSpec((tk, tm), lambda i, j, k: (k, i)) if ta else pl.BlockSpec((tm, tk), lambda i, j, k: (i, k))
    if chips == 'b' and tb:
        per = k_unit // tk
        b_spec = pl.BlockSpec((1, tn, tk), lambda i, j, k: (k // per, j, k % per))
    elif chips == 'b':
        per = n_unit // tn
        b_spec = pl.BlockSpec((1, tk, tn), lambda i, j, k: (j // per, k, j % per))
    else:
        b_spec = pl.BlockSpec((tn, tk), lambda i, j, k: (j, k)) if tb else pl.BlockSpec((tk, tn), lambda i, j, k: (k, j))
    if chips == 'out':
        per = n_unit // tn
        o_spec = pl.BlockSpec((1, tm, tn), lambda i, j, k: (j // per, i, j % per))
        o_shape = jax.ShapeDtypeStruct((N_CHIPS, m, n_unit), F32)
    else:
        o_spec = pl.BlockSpec((tm, tn), lambda i, j, k: (i, j))
        o_shape = jax.ShapeDtypeStruct((m, n), F32)
    outs = pl.pallas_call(
        body, name=name, grid=(ni, nj, nk),
        in_specs=[a_spec, b_spec] + p_in, out_specs=[o_spec] + p_out,
        out_shape=[o_shape] + p_shapes, scratch_shapes=p_scratch,
        compiler_params=pltpu.CompilerParams(
            dimension_semantics=("arbitrary" if plan is not None else "parallel", "arbitrary", "arbitrary"),
            vmem_limit_bytes=MM_VMEM_LIMIT),
    )(a, b, *p_args)
    return outs[0] if plan is None else (outs[0], list(outs[1:]))


@jax.custom_vjp
def linear(x, w):
    return _matmul(x.astype(BF16), w.astype(BF16), name="linear_fwd")


def _linear_fwd(x, w):
    xb, wb = x.astype(BF16), w.astype(BF16)
    return _matmul(xb, wb, name="linear_fwd"), (xb, wb)


def _linear_bwd(res, dy):
    xb, wb = res
    dyb = dy.astype(BF16)
    return _matmul(dyb, wb, tb=True, name="linear_dx"), _matmul(xb, dyb, ta=True, name="linear_dw")


linear.defvjp(_linear_fwd, _linear_bwd)


def _rms_fwd_call(x, g, groups):
    rows, cols = x.shape
    d = cols // groups
    tr = _row_tile(rows, cols * 4)

    def body(x_ref, g_ref, y_ref):
        for gi in range(groups):
            sl = slice(gi * d, (gi + 1) * d)
            xs = x_ref[:, sl]
            r = lax.rsqrt(jnp.mean(xs * xs, axis=-1, keepdims=True) + EPS)
            y_ref[:, sl] = xs * r * g_ref[:, sl]

    return pl.pallas_call(
        body, name="rms_fwd", grid=(rows // tr,),
        in_specs=[pl.BlockSpec((tr, cols), lambda i: (i, 0)), pl.BlockSpec((1, cols), lambda i: (0, 0))],
        out_specs=pl.BlockSpec((tr, cols), lambda i: (i, 0)),
        out_shape=jax.ShapeDtypeStruct((rows, cols), F32),
        compiler_params=pltpu.CompilerParams(dimension_semantics=("parallel",)),
    )(x, g)


def _rms_bwd_call(x, g, dy, groups):
    rows, cols = x.shape
    d = cols // groups
    tr = _row_tile(rows, cols * 4)

    def body(x_ref, g_ref, dy_ref, dx_ref, dg_ref):
        @pl.when(pl.program_id(0) == 0)
        def _():
            dg_ref[...] = jnp.zeros_like(dg_ref)

        for gi in range(groups):
            sl = slice(gi * d, (gi + 1) * d)
            xs = x_ref[:, sl]
            dys = dy_ref[:, sl]
            r = lax.rsqrt(jnp.mean(xs * xs, axis=-1, keepdims=True) + EPS)
            xhat = xs * r
            dg_ref[:, sl] += jnp.sum(dys * xhat, axis=0, keepdims=True)
            dxhat = dys * g_ref[:, sl]
            dx_ref[:, sl] = r * (dxhat - xhat * jnp.mean(dxhat * xhat, axis=-1, keepdims=True))

    return pl.pallas_call(
        body, name="rms_bwd", grid=(rows // tr,),
        in_specs=[pl.BlockSpec((tr, cols), lambda i: (i, 0)), pl.BlockSpec((1, cols), lambda i: (0, 0)),
                  pl.BlockSpec((tr, cols), lambda i: (i, 0))],
        out_specs=[pl.BlockSpec((tr, cols), lambda i: (i, 0)), pl.BlockSpec((1, cols), lambda i: (0, 0))],
        out_shape=[jax.ShapeDtypeStruct((rows, cols), F32), jax.ShapeDtypeStruct((1, cols), F32)],
        compiler_params=pltpu.CompilerParams(dimension_semantics=("arbitrary",)),
    )(x, g, dy)


@functools.partial(jax.custom_vjp, nondiff_argnums=(2,))
def rms_norm(x, g, groups):
    return _rms_fwd_call(x, g, groups)


def _rms_norm_fwd(x, g, groups):
    return _rms_fwd_call(x, g, groups), (x, g)


def _rms_norm_bwd(groups, res, dy):
    x, g = res
    dx, dg = _rms_bwd_call(x, g, dy, groups)
    return dx, dg


rms_norm.defvjp(_rms_norm_fwd, _rms_norm_bwd)


NT_DIMS = (((1,), (1,)), ((), ()))
TN_DIMS = (((0,), (0,)), ((), ()))


LOG2E = 1.4426950408889634
ATTN_VMEM_LIMIT = 48 * 1024 * 1024


def _exp_rows(q, k, scale):
    s2 = lax.dot_general(q, k, NT_DIMS, preferred_element_type=F32) * (scale * LOG2E)
    e = jnp.exp2(s2 - jnp.max(s2, axis=-1, keepdims=True))
    return e, 1.0 / jnp.sum(e, axis=-1, keepdims=True)


def _attn_fwd_call(q, k, v, scale):
    b, h, s, dk = q.shape
    hkv, dv = k.shape[1], v.shape[3]
    rep = h // hkv
    tq = _pick(s, (512, 256, 128))

    def body(q_ref, k_ref, v_ref, o_ref):
        e, inv = _exp_rows(q_ref[0, 0], k_ref[0, 0], scale)
        o_ref[0, 0] = jnp.dot(e.astype(BF16), v_ref[0, 0], preferred_element_type=F32) * inv

    return pl.pallas_call(
        body, name="attn_fwd", grid=(b, h, s // tq),
        in_specs=[pl.BlockSpec((1, 1, tq, dk), lambda bi, hi, qi: (bi, hi, qi, 0)),
                  pl.BlockSpec((1, 1, s, dk), lambda bi, hi, qi: (bi, hi // rep, 0, 0)),
                  pl.BlockSpec((1, 1, s, dv), lambda bi, hi, qi: (bi, hi // rep, 0, 0))],
        out_specs=pl.BlockSpec((1, 1, tq, dv), lambda bi, hi, qi: (bi, hi, qi, 0)),
        out_shape=jax.ShapeDtypeStruct((b, h, s, dv), F32),
        compiler_params=pltpu.CompilerParams(dimension_semantics=("parallel", "parallel", "parallel"),
                                             vmem_limit_bytes=ATTN_VMEM_LIMIT),
    )(q, k, v)


def _attn_bwd_call(q, k, v, do, scale):
    b, h, s, dk = q.shape
    hkv, dv = k.shape[1], v.shape[3]
    rep = h // hkv
    tq = _pick(s, (256, 128))

    def body(q_ref, k_ref, v_ref, do_ref, dq_ref, dk_ref, dv_ref):
        @pl.when((pl.program_id(2) == 0) & (pl.program_id(3) == 0))
        def _():
            dk_ref[...] = jnp.zeros_like(dk_ref)
            dv_ref[...] = jnp.zeros_like(dv_ref)

        qb = q_ref[0, 0]
        kb = k_ref[0, 0]
        vb = v_ref[0, 0]
        dob = do_ref[0, 0]
        e, inv = _exp_rows(qb, kb, scale)
        dp = lax.dot_general(dob, vb, NT_DIMS, preferred_element_type=F32)
        delta = jnp.sum(e * dp, axis=-1, keepdims=True) * inv
        ds = (e * ((dp - delta) * (inv * scale))).astype(BF16)
        dq_ref[0, 0] = jnp.dot(ds, kb, preferred_element_type=F32)
        dk_ref[0, 0] += lax.dot_general(ds, qb, TN_DIMS, preferred_element_type=F32)
        dv_ref[0, 0] += lax.dot_general(e.astype(BF16), (dob.astype(F32) * inv).astype(BF16), TN_DIMS,
                                        preferred_element_type=F32)

    return pl.pallas_call(
        body, name="attn_bwd", grid=(b, hkv, rep, s // tq),
        in_specs=[pl.BlockSpec((1, 1, tq, dk), lambda bi, gi, ri, qi: (bi, gi * rep + ri, qi, 0)),
                  pl.BlockSpec((1, 1, s, dk), lambda bi, gi, ri, qi: (bi, gi, 0, 0)),
                  pl.BlockSpec((1, 1, s, dv), lambda bi, gi, ri, qi: (bi, gi, 0, 0)),
                  pl.BlockSpec((1, 1, tq, dv), lambda bi, gi, ri, qi: (bi, gi * rep + ri, qi, 0))],
        out_specs=[pl.BlockSpec((1, 1, tq, dk), lambda bi, gi, ri, qi: (bi, gi * rep + ri, qi, 0)),
                   pl.BlockSpec((1, 1, s, dk), lambda bi, gi, ri, qi: (bi, gi, 0, 0)),
                   pl.BlockSpec((1, 1, s, dv), lambda bi, gi, ri, qi: (bi, gi, 0, 0))],
        out_shape=[jax.ShapeDtypeStruct(q.shape, F32), jax.ShapeDtypeStruct(k.shape, F32),
                   jax.ShapeDtypeStruct(v.shape, F32)],
        compiler_params=pltpu.CompilerParams(
            dimension_semantics=("parallel", "parallel", "arbitrary", "arbitrary"), vmem_limit_bytes=ATTN_VMEM_LIMIT),
    )(q, k, v, do)


@functools.partial(jax.custom_vjp, nondiff_argnums=(3,))
def attention(q, k, v, scale):
    return _attn_fwd_call(q.astype(BF16), k.astype(BF16), v.astype(BF16), scale)


def _attention_fwd(q, k, v, scale):
    qb, kb, vb = q.astype(BF16), k.astype(BF16), v.astype(BF16)
    return _attn_fwd_call(qb, kb, vb, scale), (qb, kb, vb)


def _attention_bwd(scale, res, do):
    qb, kb, vb = res
    return tuple(_attn_bwd_call(qb, kb, vb, do.astype(BF16), scale))


attention.defvjp(_attention_fwd, _attention_bwd)


CONV_COL_TILE = 256
CONV_PACK_ROWS = 8


def _shifted(x, off, rows):
    if off == 0:
        return x
    s = x.shape[0]
    rolled = pltpu.roll(x, (-off) % s, 0)
    valid = (rows + off >= 0) & (rows + off < s)
    return jnp.where(valid, rolled, 0.0)


def _conv_pre(x, wb_ref, rows):
    z = jnp.zeros_like(x) + wb_ref[SSD_CONV:SSD_CONV + 1, :]
    for j in range(SSD_CONV):
        z = z + wb_ref[j:j + 1, :] * _shifted(x, j - SSD_CONV // 2, rows)
    return z


def _conv_fwd_call(x, wb):
    b, s, c = x.shape
    tc = _pick(c, (CONV_COL_TILE, LANES))

    def body(x_ref, wb_ref, y_ref):
        xv = x_ref[0]
        rows = lax.broadcasted_iota(jnp.int32, xv.shape, 0)
        z = _conv_pre(xv, wb_ref, rows)
        y_ref[0] = z * jax.nn.sigmoid(z)

    return pl.pallas_call(
        body, name="conv_fwd", grid=(b, c // tc),
        in_specs=[pl.BlockSpec((1, s, tc), lambda bi, ci: (bi, 0, ci)),
                  pl.BlockSpec((CONV_PACK_ROWS, tc), lambda bi, ci: (0, ci))],
        out_specs=pl.BlockSpec((1, s, tc), lambda bi, ci: (bi, 0, ci)),
        out_shape=jax.ShapeDtypeStruct(x.shape, F32),
        compiler_params=pltpu.CompilerParams(dimension_semantics=("parallel", "parallel")),
    )(x, wb)


def _conv_bwd_call(x, wb, dy):
    b, s, c = x.shape
    tc = _pick(c, (CONV_COL_TILE, LANES))

    def body(x_ref, wb_ref, dy_ref, dx_ref, dwb_ref):
        xv = x_ref[0]
        rows = lax.broadcasted_iota(jnp.int32, xv.shape, 0)
        z = _conv_pre(xv, wb_ref, rows)
        sg = jax.nn.sigmoid(z)
        dz = dy_ref[0] * (sg * (1.0 + z * (1.0 - sg)))
        dx = jnp.zeros_like(xv)
        for j in range(SSD_CONV):
            off = j - SSD_CONV // 2
            dx = dx + wb_ref[j:j + 1, :] * _shifted(dz, -off, rows)
            dwb_ref[0, j:j + 1, :] = jnp.sum(dz * _shifted(xv, off, rows), axis=0, keepdims=True)
        dx_ref[0] = dx
        dwb_ref[0, SSD_CONV:SSD_CONV + 1, :] = jnp.sum(dz, axis=0, keepdims=True)
        dwb_ref[0, SSD_CONV + 1:, :] = jnp.zeros((CONV_PACK_ROWS - SSD_CONV - 1, dz.shape[1]), F32)

    return pl.pallas_call(
        body, name="conv_bwd", grid=(b, c // tc),
        in_specs=[pl.BlockSpec((1, s, tc), lambda bi, ci: (bi, 0, ci)),
                  pl.BlockSpec((CONV_PACK_ROWS, tc), lambda bi, ci: (0, ci)),
                  pl.BlockSpec((1, s, tc), lambda bi, ci: (bi, 0, ci))],
        out_specs=[pl.BlockSpec((1, s, tc), lambda bi, ci: (bi, 0, ci)),
                   pl.BlockSpec((1, CONV_PACK_ROWS, tc), lambda bi, ci: (bi, 0, ci))],
        out_shape=[jax.ShapeDtypeStruct(x.shape, F32), jax.ShapeDtypeStruct((b, CONV_PACK_ROWS, c), F32)],
        compiler_params=pltpu.CompilerParams(dimension_semantics=("parallel", "parallel")),
    )(x, wb, dy)


@jax.custom_vjp
def conv_silu(x, wb):
    return _conv_fwd_call(x, wb)


def _conv_silu_fwd(x, wb):
    return _conv_fwd_call(x, wb), (x, wb)


def _conv_silu_bwd(res, dy):
    x, wb = res
    dx, dwb = _conv_bwd_call(x, wb, dy)
    return dx, jnp.sum(dwb, axis=0)


conv_silu.defvjp(_conv_silu_fwd, _conv_silu_bwd)


SSD_PAIRS = SSD_GROUP_HEADS // 2
NEG_INF = -1e30


def _ssd_common(x_ref, dtx_ref, dtt_ref, anx_ref, anc_ref, b_ref, c_ref, reverse):
    L = SSD_CHUNK
    xv = x_ref[0]
    dt = dtx_ref[0]
    ri = lax.broadcasted_iota(jnp.int32, (L, L), 0)
    ci = lax.broadcasted_iota(jnp.int32, (L, L), 1)
    causal = (ri <= ci) if reverse else (ri >= ci)
    tri = causal.astype(F32)
    a_cs = jnp.dot(tri, dt * anx_ref[...], precision=HIGHEST, preferred_element_type=F32)
    a_row = dtt_ref[0, 0] * anc_ref[0]
    acs_row = lax.dot_general(a_row, tri, NT_DIMS, precision=HIGHEST, preferred_element_type=F32)
    xd = xv * dt
    bmat = b_ref[0].astype(BF16)
    cmat = c_ref[0].astype(BF16)
    gmat = lax.dot_general(cmat, bmat, NT_DIMS, preferred_element_type=F32)
    return xv, dt, causal, tri, a_cs, acs_row, xd, bmat, cmat, gmat


def _ssd_lambda(a_cs, acs_row, causal, h):
    col = a_cs[:, h * SSD_HEAD_DIM:h * SSD_HEAD_DIM + 1]
    row = acs_row[h:h + 1, :]
    return jnp.exp(jnp.where(causal, col - row, NEG_INF))


def _ssd_fwd_call(x, dtx, dtt, anx, anc, bm, cm, reverse):
    b, s, _ = x.shape
    L, N, GW = SSD_CHUNK, SSD_STATE, SSD_GROUP_WIDTH
    nc = s // L
    end = 0 if reverse else L - 1

    def body(x_ref, dtx_ref, dtt_ref, anx_ref, anc_ref, b_ref, c_ref, y_ref, hs_ref, state):
        @pl.when(pl.program_id(2) == 0)
        def _():
            state[...] = jnp.zeros_like(state)

        xv, dt, causal, tri, a_cs, acs_row, xd, bmat, cmat, gmat = _ssd_common(
            x_ref, dtx_ref, dtt_ref, anx_ref, anc_ref, b_ref, c_ref, reverse)
        hin = state[...]
        hs_ref[0, 0, 0] = hin
        y_off = jnp.dot(cmat, hin.astype(BF16), preferred_element_type=F32) * jnp.exp(a_cs)
        a_end = a_cs[end:end + 1, :]
        s_new = lax.dot_general(bmat, (xd * jnp.exp(a_end - a_cs)).astype(BF16), TN_DIMS, preferred_element_type=F32)
        state[...] = jnp.exp(a_end) * hin + s_new
        lane = lax.broadcasted_iota(jnp.int32, (L, LANES), 1)
        for pr in range(SSD_PAIRS):
            sl = slice(pr * LANES, (pr + 1) * LANES)
            xdp = xd[:, sl].astype(BF16)
            w0 = (gmat * _ssd_lambda(a_cs, acs_row, causal, 2 * pr)).astype(BF16)
            w1 = (gmat * _ssd_lambda(a_cs, acs_row, causal, 2 * pr + 1)).astype(BF16)
            y0 = jnp.dot(w0, xdp, preferred_element_type=F32)
            y1 = jnp.dot(w1, xdp, preferred_element_type=F32)
            y_ref[0, :, sl] = jnp.where(lane < SSD_HEAD_DIM, y0, y1) + y_off[:, sl]

    G = SSD_GROUPS
    chunk = (lambda c: nc - 1 - c) if reverse else (lambda c: c)
    seq = lambda bi, gi, c: (bi, chunk(c), gi)
    return pl.pallas_call(
        body, name="ssd_fwd", grid=(b, G, nc),
        in_specs=[pl.BlockSpec((1, L, GW), seq),
                  pl.BlockSpec((1, L, GW), seq),
                  pl.BlockSpec((1, 1, SSD_GROUP_HEADS, L), lambda bi, gi, c: (bi, gi, 0, chunk(c))),
                  pl.BlockSpec((1, GW), lambda bi, gi, c: (0, gi)),
                  pl.BlockSpec((1, SSD_GROUP_HEADS, 1), lambda bi, gi, c: (gi, 0, 0)),
                  pl.BlockSpec((1, L, N), seq),
                  pl.BlockSpec((1, L, N), seq)],
        out_specs=[pl.BlockSpec((1, L, GW), seq),
                   pl.BlockSpec((1, 1, 1, N, GW), lambda bi, gi, c: (bi, gi, chunk(c), 0, 0))],
        out_shape=[jax.ShapeDtypeStruct(x.shape, F32), jax.ShapeDtypeStruct((b, G, nc, N, GW), F32)],
        scratch_shapes=[pltpu.VMEM((N, GW), F32)],
        compiler_params=pltpu.CompilerParams(dimension_semantics=("parallel", "parallel", "arbitrary")),
    )(x, dtx, dtt, anx, anc, bm, cm)


def _ssd_bwd_call(x, dtx, dtt, anx, anc, bm, cm, hs, dy, reverse):
    b, s, _ = x.shape
    L, N, GW = SSD_CHUNK, SSD_STATE, SSD_GROUP_WIDTH
    nc = s // L
    end = 0 if reverse else L - 1

    def body(x_ref, dtx_ref, dtt_ref, anx_ref, anc_ref, b_ref, c_ref, hs_ref, dy_ref,
             dx_ref, ddt_ref, dan_ref, db_ref, dc_ref, dstate):
        @pl.when(pl.program_id(2) == 0)
        def _():
            dstate[...] = jnp.zeros_like(dstate)

        xv, dt, causal, tri, a_cs, acs_row, xd, bmat, cmat, gmat = _ssd_common(
            x_ref, dtx_ref, dtt_ref, anx_ref, anc_ref, b_ref, c_ref, reverse)
        hin = hs_ref[0, 0, 0]
        hinb = hin.astype(BF16)
        dyv = dy_ref[0]
        ds_out = dstate[...]
        dsb = ds_out.astype(BF16)
        eacs = jnp.exp(a_cs)
        a_end = a_cs[end:end + 1, :]
        e_end = jnp.exp(a_end)
        dec = jnp.exp(a_end - a_cs)
        dye = dyv * eacs
        dyeb = dye.astype(BF16)
        xdec = xd * dec
        ch = jnp.dot(cmat, hinb, preferred_element_type=F32)
        bds = jnp.dot(bmat, dsb, preferred_element_type=F32)
        t_state = xdec * bds
        d_aend = jnp.sum(t_state, axis=0, keepdims=True) + e_end * jnp.sum(ds_out * hin, axis=0, keepdims=True)
        dacs = dye * ch - t_state
        dxd_state = bds * dec
        dstate[...] = e_end * ds_out + lax.dot_general(cmat, dyeb, TN_DIMS, preferred_element_type=F32)

        lane = lax.broadcasted_iota(jnp.int32, (L, LANES), 1)
        ones = jnp.full((L, LANES), 1.0 / SSD_HEAD_DIM, F32)
        dg = jnp.zeros((L, L), F32)
        dxd_parts, dacs_parts = [], []
        for pr in range(SSD_PAIRS):
            sl = slice(pr * LANES, (pr + 1) * LANES)
            xdp = xd[:, sl]
            dyp = dyv[:, sl]
            dxd_p = jnp.zeros((L, LANES), F32)
            dacs_p = jnp.zeros((L, LANES), F32)
            for half in range(2):
                mine = (lane < SSD_HEAD_DIM) if half == 0 else (lane >= SSD_HEAD_DIM)
                lam = _ssd_lambda(a_cs, acs_row, causal, 2 * pr + half)
                w = gmat * lam
                xdh = jnp.where(mine, xdp, 0.0).astype(BF16)
                dyh = jnp.where(mine, dyp, 0.0).astype(BF16)
                dw = lax.dot_general(dyh, xdh, NT_DIMS, preferred_element_type=F32)
                dg = dg + dw * lam
                mm = dw * w
                rs = jnp.dot(mm, ones, precision=HIGHEST, preferred_element_type=F32)
                cs = lax.dot_general(mm, ones, TN_DIMS, precision=HIGHEST, preferred_element_type=F32)
                dacs_p = dacs_p + jnp.where(mine, rs - cs, 0.0)
                wtdy = lax.dot_general(w.astype(BF16), dyh, TN_DIMS, preferred_element_type=F32)
                dxd_p = dxd_p + wtdy
            dxd_parts.append(dxd_p)
            dacs_parts.append(dacs_p)
        dxd = jnp.concatenate(dxd_parts, axis=1) + dxd_state
        dacs = dacs + jnp.concatenate(dacs_parts, axis=1)
        last = lax.broadcasted_iota(jnp.int32, dacs.shape, 0) == end
        dacs = dacs + jnp.where(last, d_aend, 0.0)
        da = lax.dot_general(tri, dacs, TN_DIMS, precision=HIGHEST, preferred_element_type=F32)
        dgb = dg.astype(BF16)
        dc_ref[0] = (jnp.dot(dgb, bmat, preferred_element_type=F32)
                     + lax.dot_general(dyeb, hinb, NT_DIMS, preferred_element_type=F32))
        db_ref[0] = (lax.dot_general(dgb, cmat, TN_DIMS, preferred_element_type=F32)
                     + lax.dot_general(xdec.astype(BF16), dsb, NT_DIMS, preferred_element_type=F32))
        dx_ref[0] = dxd * dt
        ddt_ref[0] = da * anx_ref[...] + dxd * xv
        dan_ref[0, 0, 0] = jnp.sum(da * dt, axis=0, keepdims=True)

    G = SSD_GROUPS
    chunk = (lambda c: c) if reverse else (lambda c: nc - 1 - c)
    rev = lambda bi, gi, c: (bi, chunk(c), gi)
    return pl.pallas_call(
        body, name="ssd_bwd", grid=(b, G, nc),
        in_specs=[pl.BlockSpec((1, L, GW), rev),
                  pl.BlockSpec((1, L, GW), rev),
                  pl.BlockSpec((1, 1, SSD_GROUP_HEADS, L), lambda bi, gi, c: (bi, gi, 0, chunk(c))),
                  pl.BlockSpec((1, GW), lambda bi, gi, c: (0, gi)),
                  pl.BlockSpec((1, SSD_GROUP_HEADS, 1), lambda bi, gi, c: (gi, 0, 0)),
                  pl.BlockSpec((1, L, N), rev),
                  pl.BlockSpec((1, L, N), rev),
                  pl.BlockSpec((1, 1, 1, N, GW), lambda bi, gi, c: (bi, gi, chunk(c), 0, 0)),
                  pl.BlockSpec((1, L, GW), rev)],
        out_specs=[pl.BlockSpec((1, L, GW), rev),
                   pl.BlockSpec((1, L, GW), rev),
                   pl.BlockSpec((1, 1, 1, 1, GW), lambda bi, gi, c: (bi, gi, chunk(c), 0, 0)),
                   pl.BlockSpec((1, L, N), rev),
                   pl.BlockSpec((1, L, N), rev)],
        out_shape=[jax.ShapeDtypeStruct(x.shape, F32), jax.ShapeDtypeStruct(x.shape, F32),
                   jax.ShapeDtypeStruct((b, G, nc, 1, GW), F32),
                   jax.ShapeDtypeStruct(bm.shape, F32), jax.ShapeDtypeStruct(cm.shape, F32)],
        scratch_shapes=[pltpu.VMEM((N, GW), F32)],
        compiler_params=pltpu.CompilerParams(dimension_semantics=("parallel", "parallel", "arbitrary")),
    )(x, dtx, dtt, anx, anc, bm, cm, hs, dy)


@functools.partial(jax.custom_vjp, nondiff_argnums=(7,))
def _ssd_scan(x, dtx, dtt, anx, anc, bm, cm, reverse):
    return _ssd_fwd_call(x, dtx, dtt, anx, anc, bm, cm, reverse)[0]


def _ssd_scan_fwd(x, dtx, dtt, anx, anc, bm, cm, reverse):
    y, hs = _ssd_fwd_call(x, dtx, dtt, anx, anc, bm, cm, reverse)
    return y, (x, dtx, dtt, anx, anc, bm, cm, hs)


def _ssd_scan_bwd(reverse, res, dy):
    x, dtx, dtt, anx, anc, bm, cm, hs = res
    dx, ddtx, dan, db, dc = _ssd_bwd_call(x, dtx, dtt, anx, anc, bm, cm, hs, dy, reverse)
    b, g, nc, _, gw = dan.shape
    danx = jnp.sum(dan, axis=(0, 2, 3)).reshape(1, g * gw)
    return dx, ddtx, jnp.zeros_like(dtt), danx, jnp.zeros_like(anc), db, dc


_ssd_scan.defvjp(_ssd_scan_fwd, _ssd_scan_bwd)


def ssd_chunked(xs, dt, a_neg, bm, cm, reverse):
    b, s, _ = xs.shape
    dtx = jnp.repeat(dt, SSD_HEAD_DIM, axis=-1)
    dtt = jnp.transpose(dt, (0, 2, 1)).reshape(b, SSD_GROUPS, SSD_GROUP_HEADS, s)
    anx = jnp.repeat(a_neg, SSD_HEAD_DIM)[None, :]
    anc = a_neg.reshape(SSD_GROUPS, SSD_GROUP_HEADS, 1)
    return _ssd_scan(xs, dtx, dtt, anx, anc, bm, cm, reverse)


def _loss_call(y, t):
    rows, cols = y.shape
    tr = _row_tile(rows, cols * 4)

    def body(y_ref, t_ref, loss_ref, diff_ref):
        @pl.when(pl.program_id(0) == 0)
        def _():
            loss_ref[...] = jnp.zeros_like(loss_ref)

        d = y_ref[...] - t_ref[...]
        diff_ref[...] = d * (1.0 / cols)
        part = jnp.sum(jnp.sum(d * d, axis=1, keepdims=True), axis=0, keepdims=True)
        loss_ref[...] += part * (0.5 / cols)

    return pl.pallas_call(
        body, name="loss_head", grid=(rows // tr,),
        in_specs=[pl.BlockSpec((tr, cols), lambda i: (i, 0)), pl.BlockSpec((tr, cols), lambda i: (i, 0))],
        out_specs=[pl.BlockSpec((1, 1), lambda i: (0, 0)), pl.BlockSpec((tr, cols), lambda i: (i, 0))],
        out_shape=[jax.ShapeDtypeStruct((1, 1), F32), jax.ShapeDtypeStruct((rows, cols), F32)],
        compiler_params=pltpu.CompilerParams(dimension_semantics=("arbitrary",)),
    )(y, t)


@jax.custom_vjp
def loss_head(y, t):
    return _loss_call(y, t)[0][0, 0]


def _loss_head_fwd(y, t):
    loss, diff = _loss_call(y, t)
    return loss[0, 0], diff


def _loss_head_bwd(diff, g):
    return g * diff, jnp.zeros_like(diff)


loss_head.defvjp(_loss_head_fwd, _loss_head_bwd)


def _axial_rope_tables(seq_len, rot_dim):
    rows = seq_len // GRID_W
    row_idx = jnp.repeat(jnp.arange(rows), GRID_W).astype(F32)
    col_idx = jnp.tile(jnp.arange(GRID_W), rows).astype(F32)
    axis_dim = rot_dim // 2
    inv_freq = jnp.power(ROPE_THETA, -jnp.arange(0, axis_dim, 2, dtype=F32) / axis_dim)
    ang_r = row_idx[:, None] * inv_freq[None, :]
    ang_c = col_idx[:, None] * inv_freq[None, :]
    return jnp.cos(ang_r), jnp.sin(ang_r), jnp.cos(ang_c), jnp.sin(ang_c)


def _rotate(x, cos, sin):
    x1, x2 = jnp.split(x, 2, axis=-1)
    cos = cos[:, None, :]
    sin = sin[:, None, :]
    return jnp.concatenate([x1 * cos - x2 * sin, x1 * sin + x2 * cos], axis=-1)


def _apply_axial_rope(x, tables):
    cos_r, sin_r, cos_c, sin_c = tables
    x_row, x_col = jnp.split(x, 2, axis=-1)
    return jnp.concatenate([_rotate(x_row, cos_r, sin_r), _rotate(x_col, cos_c, sin_c)], axis=-1)


def _heads_first(t):
    return jnp.transpose(t, (0, 2, 1, 3))


def _gqa_group(q, k, v, q_norm_g, k_norm_g, rope, b, s):
    q = rms_norm(q, jnp.tile(q_norm_g, GQA_HEADS)[None, :], GQA_HEADS).reshape(b, s, GQA_HEADS, GQA_HEAD_DIM)
    k = rms_norm(k, jnp.tile(k_norm_g, GQA_KV_HEADS)[None, :], GQA_KV_HEADS).reshape(b, s, GQA_KV_HEADS, GQA_HEAD_DIM)
    v = v.reshape(b, s, GQA_KV_HEADS, GQA_HEAD_DIM)
    q = _apply_axial_rope(q, rope)
    k = _apply_axial_rope(k, rope)
    o = attention(_heads_first(q), _heads_first(k), _heads_first(v), GQA_HEAD_DIM ** -0.5)
    return _heads_first(o).reshape(b * s, GQA_WIDTH)


def _mla_group(c_q, c_kv, k_pe, q_norm_g, w_uq, kv_norm_g, w_ukv, rope, b, s):
    q = linear(rms_norm(c_q, q_norm_g[None, :], 1), w_uq).reshape(b, s, MLA_HEADS, MLA_NOPE_DIM + MLA_ROPE_DIM)
    q_nope, q_pe = q[..., :MLA_NOPE_DIM], q[..., MLA_NOPE_DIM:]
    kv = linear(rms_norm(c_kv, kv_norm_g[None, :], 1), w_ukv).reshape(b, s, MLA_HEADS, MLA_NOPE_DIM + MLA_V_DIM)
    k_nope, v = kv[..., :MLA_NOPE_DIM], kv[..., MLA_NOPE_DIM:]
    q_pe = _apply_axial_rope(q_pe, rope)
    k_pe = _apply_axial_rope(k_pe.reshape(b, s, 1, MLA_ROPE_DIM), rope)
    q = jnp.concatenate([q_nope, q_pe], axis=-1)
    k = jnp.concatenate([k_nope, jnp.broadcast_to(k_pe, (b, s, MLA_HEADS, MLA_ROPE_DIM))], axis=-1)
    o = attention(_heads_first(q), _heads_first(k), _heads_first(v), (MLA_NOPE_DIM + MLA_ROPE_DIM) ** -0.5)
    return _heads_first(o).reshape(b * s, MLA_HEADS * MLA_V_DIM)


def _ssd_group(z, xbc, dt_raw, conv_w, conv_b, dt_bias, a_log, d_skip, norm_g, b, s):
    wb = jnp.concatenate([conv_w, conv_b[None, :], jnp.zeros((CONV_PACK_ROWS - SSD_CONV - 1, SSD_CONV_DIM), F32)], axis=0)
    xbc = conv_silu(xbc.reshape(b, s, SSD_CONV_DIM), wb)
    xs = xbc[..., :SSD_INNER]
    bm = xbc[..., SSD_INNER:SSD_INNER + SSD_GROUPS * SSD_STATE]
    cm = xbc[..., SSD_INNER + SSD_GROUPS * SSD_STATE:]
    dt = jax.nn.softplus(dt_raw.reshape(b, s, 2, SSD_HEADS) + dt_bias)
    a_neg = -jnp.exp(a_log)
    y_fwd = ssd_chunked(xs, dt[:, :, 0], a_neg[0], bm, cm, False)
    y_bwd = ssd_chunked(xs, dt[:, :, 1], a_neg[1], bm, cm, True)
    y = y_fwd + y_bwd + xs * jnp.repeat(d_skip, SSD_HEAD_DIM)
    y = y.reshape(b * s, SSD_INNER) * jax.nn.silu(z)
    return rms_norm(y, norm_g[None, :], SSD_GROUPS)


MIXER_WEIGHTS = ('q_norm_g', 'k_norm_g', 'mla_q_norm_g', 'w_uq', 'mla_kv_norm_g', 'w_ukv', 'conv_w', 'conv_b',
                 'dt_bias', 'a_log', 'd_skip', 'ssd_norm_g')


def _mixer(proj, w, rope_a, rope_b, b, s):
    idx = np.cumsum(IN_SPLITS).tolist()
    q_a, k_a, v_a, cq_b, ckv_b, kpe_b, z_c, xbc_c, dt_c = [_in_cols(proj, lo, hi)
                                                           for lo, hi in zip([0] + idx[:-1], idx)]
    o_a = _gqa_group(q_a, k_a, v_a, w["q_norm_g"], w["k_norm_g"], rope_a, b, s)
    o_b = _mla_group(cq_b, ckv_b, kpe_b, w["mla_q_norm_g"], w["w_uq"], w["mla_kv_norm_g"], w["w_ukv"], rope_b, b, s)
    o_c = _ssd_group(z_c, xbc_c, dt_c, w["conv_w"], w["conv_b"], w["dt_bias"], w["a_log"], w["d_skip"],
                     w["ssd_norm_g"], b, s)
    return jnp.concatenate([o_a, o_b, o_c], axis=-1)


def _seq_tile(s, row_bytes):
    return _row_tile(s, row_bytes)


def _normmod_fwd(x, g, scale, shift):
    b, s, d = x.shape
    tr = _seq_tile(s, d * 4)

    def body(x_ref, g_ref, sc_ref, sh_ref, h_ref):
        xv = x_ref[0]
        r = lax.rsqrt(jnp.mean(xv * xv, axis=-1, keepdims=True) + EPS)
        h_ref[0] = (xv * r * g_ref[...] * (1.0 + sc_ref[0]) + sh_ref[0]).astype(BF16)

    act = pl.BlockSpec((1, tr, d), lambda bi, i: (bi, i, 0))
    vec = pl.BlockSpec((1, 1, d), lambda bi, i: (bi, 0, 0))
    return pl.pallas_call(
        body, name="normmod_fwd", grid=(b, s // tr),
        in_specs=[act, pl.BlockSpec((1, d), lambda bi, i: (0, 0)), vec, vec], out_specs=act,
        out_shape=jax.ShapeDtypeStruct((b, s, d), BF16),
        compiler_params=pltpu.CompilerParams(dimension_semantics=("parallel", "parallel")),
    )(x, g, scale, shift)


def _normmod_bwd(x, g, scale, dh, resid):
    b, s, d = x.shape
    tr = _seq_tile(s, d * 4)

    def body(x_ref, g_ref, sc_ref, dh_ref, res_ref, dx_ref, dg_ref, dsc_ref, dsh_ref):
        bi, i = pl.program_id(0), pl.program_id(1)

        @pl.when((bi == 0) & (i == 0))
        def _():
            dg_ref[...] = jnp.zeros_like(dg_ref)

        @pl.when(i == 0)
        def _():
            dsc_ref[...] = jnp.zeros_like(dsc_ref)
            dsh_ref[...] = jnp.zeros_like(dsh_ref)

        xv = x_ref[0]
        dhv = dh_ref[0]
        gv = g_ref[...]
        r = lax.rsqrt(jnp.mean(xv * xv, axis=-1, keepdims=True) + EPS)
        xhat = xv * r
        dsh_ref[0] += jnp.sum(dhv, axis=0, keepdims=True)
        dsc_ref[0] += jnp.sum(dhv * (xhat * gv), axis=0, keepdims=True)
        dn = dhv * (1.0 + sc_ref[0])
        dg_ref[...] += jnp.sum(dn * xhat, axis=0, keepdims=True)
        dxhat = dn * gv
        dx_ref[0] = r * (dxhat - xhat * jnp.mean(dxhat * xhat, axis=-1, keepdims=True)) + res_ref[0]

    act = pl.BlockSpec((1, tr, d), lambda bi, i: (bi, i, 0))
    vec = pl.BlockSpec((1, 1, d), lambda bi, i: (bi, 0, 0))
    gain = pl.BlockSpec((1, d), lambda bi, i: (0, 0))
    return pl.pallas_call(
        body, name="normmod_bwd", grid=(b, s // tr),
        in_specs=[act, gain, vec, act, act], out_specs=[act, gain, vec, vec],
        out_shape=[jax.ShapeDtypeStruct((b, s, d), F32), jax.ShapeDtypeStruct((1, d), F32),
                   jax.ShapeDtypeStruct((b, 1, d), F32), jax.ShapeDtypeStruct((b, 1, d), F32)],
        compiler_params=pltpu.CompilerParams(dimension_semantics=("arbitrary", "arbitrary")),
    )(x, g, scale, dh, resid)


def _gated_add(x, gate, t):
    b, s, d = x.shape
    tr = _seq_tile(s, d * 4)

    def body(x_ref, g_ref, t_ref, o_ref):
        o_ref[0] = x_ref[0] + g_ref[0] * t_ref[0]

    act = pl.BlockSpec((1, tr, d), lambda bi, i: (bi, i, 0))
    vec = pl.BlockSpec((1, 1, d), lambda bi, i: (bi, 0, 0))
    return pl.pallas_call(
        body, name="gated_add", grid=(b, s // tr), in_specs=[act, vec, act], out_specs=act,
        out_shape=jax.ShapeDtypeStruct((b, s, d), F32),
        compiler_params=pltpu.CompilerParams(dimension_semantics=("parallel", "parallel")),
    )(x, gate, t)


def _gated_bwd(dy, gate, t):
    b, s, d = dy.shape
    tr = _seq_tile(s, d * 4)

    def body(dy_ref, g_ref, t_ref, dt_ref, dgate_ref):
        @pl.when(pl.program_id(1) == 0)
        def _():
            dgate_ref[...] = jnp.zeros_like(dgate_ref)

        dyv = dy_ref[0]
        dt_ref[0] = (g_ref[0] * dyv).astype(BF16)
        dgate_ref[0] += jnp.sum(dyv * t_ref[0], axis=0, keepdims=True)

    act = pl.BlockSpec((1, tr, d), lambda bi, i: (bi, i, 0))
    vec = pl.BlockSpec((1, 1, d), lambda bi, i: (bi, 0, 0))
    return pl.pallas_call(
        body, name="gated_bwd", grid=(b, s // tr), in_specs=[act, vec, act], out_specs=[act, vec],
        out_shape=[jax.ShapeDtypeStruct((b, s, d), BF16), jax.ShapeDtypeStruct((b, 1, d), F32)],
        compiler_params=pltpu.CompilerParams(dimension_semantics=("parallel", "arbitrary")),
    )(dy, gate, t)


def _swiglu_fwd(gu, plan=None):
    rows, f2 = gu.shape
    f = f2 // 2
    tr = _row_tile(rows, f2 * 4)
    steps = rows // tr
    p_in, p_out, p_shapes, p_scratch, p_args = _plan_specs(plan)

    def body(gu_ref, *rest):
        a_ref = rest[len(p_in)]
        i = pl.program_id(0)

        def compute():
            gt = gu_ref[:, :f]
            a_ref[...] = (gt * jax.nn.sigmoid(gt) * gu_ref[:, f:]).astype(BF16)

        _ride(plan, rest[:len(p_in)] + rest[len(p_in) + 1:], i == 0, i == steps - 1, compute)

    outs = pl.pallas_call(
        body, name="swiglu_fwd", grid=(steps,),
        in_specs=[pl.BlockSpec((tr, f2), lambda i: (i, 0))] + p_in,
        out_specs=[pl.BlockSpec((tr, f), lambda i: (i, 0))] + p_out,
        out_shape=[jax.ShapeDtypeStruct((rows, f), BF16)] + p_shapes, scratch_shapes=p_scratch,
        compiler_params=pltpu.CompilerParams(dimension_semantics=("arbitrary" if plan is not None else "parallel",)),
    )(gu, *p_args)
    return outs[0] if plan is None else (outs[0], list(outs[1:]))


def _swiglu_bwd(gu, dact):
    rows, f2 = gu.shape
    f = f2 // 2
    tr = _row_tile(rows, f2 * 4)

    def body(gu_ref, da_ref, dgu_ref):
        gt = gu_ref[:, :f]
        up = gu_ref[:, f:]
        da = da_ref[...]
        sg = jax.nn.sigmoid(gt)
        dgu_ref[:, :f] = (da * up * (sg * (1.0 + gt * (1.0 - sg)))).astype(BF16)
        dgu_ref[:, f:] = (da * gt * sg).astype(BF16)

    return pl.pallas_call(
        body, name="swiglu_bwd", grid=(rows // tr,),
        in_specs=[pl.BlockSpec((tr, f2), lambda i: (i, 0)), pl.BlockSpec((tr, f), lambda i: (i, 0))],
        out_specs=pl.BlockSpec((tr, f2), lambda i: (i, 0)),
        out_shape=jax.ShapeDtypeStruct((rows, f2), BF16),
        compiler_params=pltpu.CompilerParams(dimension_semantics=("parallel",)),
    )(gu, dact)


class _Gathered:
    def __init__(self, shards):
        self.shards, self.full = shards, {}

    def plan(self, keys):
        return _gather_plan([self.shards[k] for k in keys])

    def store(self, keys, outs):
        for key, out in zip(keys, outs):
            name = key[0]
            g = out.reshape((N_CHIPS,) + self.shards[key].shape)
            if name in CHIP_BLOCKED:
                full = g
            elif name in COL_SHARDED:
                full = _cols_full(g).astype(F32)
            else:
                full = g.reshape(g.shape[0] * g.shape[1], g.shape[2])
            self.full[key] = full

    def carry(self, keys, fn):
        if not keys:
            return fn(None)
        res, outs = fn(self.plan(keys))
        self.store(keys, outs)
        return res


def _gather_schedule(depth):
    every = [(n, l) for l in range(depth) for n in ('w_uq', 'w_ukv')]
    sched = {'first': [('w_in', 0), ('w_gate_up', 0)] + every}
    for l in range(depth):
        sched[('w_in_fwd', l)] = [('w_out', l)] + ([('w_down', 0)] if l == 0 else [])
        if l + 1 < depth:
            sched[('w_gate_up_fwd', l)] = [('w_gate_up', l + 1)]
            sched[('swiglu_fwd', l)] = [('w_in', l + 1)]
            sched[('w_down_fwd', l)] = [('w_down', l + 1)]
    return sched


GATE_UP_PIECES = 2


class _Reducer:
    def __init__(self):
        self.parts, self.recv, self.result = {}, {}, {}

    def add(self, key, grad):
        name = key[0]
        if name in CHIP_BLOCKED:
            blocks = grad
        elif name in COL_SHARDED:
            blocks = _cols_split(grad)
        else:
            blocks = grad.reshape(N_CHIPS, grad.shape[0] // N_CHIPS, grad.shape[1])
        self.parts[key] = _rs_parts(blocks)
        self.recv[key] = []

    def pieces(self, key):
        rows = self.parts[key].shape[1]
        n = GATE_UP_PIECES if key[0] == 'w_gate_up' else 1
        return [(key, i * (rows // n), rows // n) for i in range(n)]

    def plan(self, jobs):
        return _chip_exchange_plan([(self.parts[key], row0, rows) for key, row0, rows in jobs])

    def store(self, jobs, outs):
        for (key, row0, rows), out in zip(jobs, outs):
            self.recv[key].append((row0, out))
            if len(self.recv[key]) == len(self.pieces(key)):
                self.result[key] = _rs_result(self.parts[key], sorted(self.recv[key], key=lambda t: t[0]))

    def carry(self, keys, fn, piece=None):
        jobs = [j for key in keys for j in self.pieces(key)]
        if piece is not None:
            jobs = [j for key in keys for j in self.pieces(key)[piece:piece + 1]]
        if not jobs:
            return fn(None)
        res, outs = fn(self.plan(jobs))
        self.store(jobs, outs)
        return res

    def flush(self):
        jobs = [j for key in self.parts for j in self.pieces(key)
                if key not in self.result and j[1] not in [r for r, _ in self.recv[key]]]
        if jobs:
            self.store(jobs, _run_plan(self.plan(jobs), "rs_chip_exchange"))


def _layer_fwd(x, mod, w, gathered, l, sched, rope_a, rope_b):
    b, s, d = x.shape
    m = b * s
    shift1, scale1, gate1, shift2, scale2, gate2 = [t[:, None, :] for t in jnp.split(mod, 6, axis=-1)]
    g1, g2 = w["norm1_g"][None, :], w["norm2_g"][None, :]
    full = lambda n: gathered.full[(n, l)]
    h1 = _normmod_fwd(x, g1, scale1, shift1).reshape(m, d)
    proj = gathered.carry(sched.get(('w_in_fwd', l)), lambda p: _matmul(h1, full('w_in'), name="w_in_fwd", plan=p, chips='b'))
    mixer_w = {n: (full(n) if n in COL_SHARDED else w[n]) for n in MIXER_WEIGHTS}
    o, mixer_vjp = jax.vjp(lambda p, mw: _mixer(p, mw, rope_a, rope_b, b, s), proj, mixer_w)
    o = o.astype(BF16)
    mix = _matmul(o, full('w_out'), name="w_out_fwd").reshape(b, s, d)
    x_mid = _gated_add(x, gate1, mix)
    h2 = _normmod_fwd(x_mid, g2, scale2, shift2).reshape(m, d)
    gu = gathered.carry(sched.get(('w_gate_up_fwd', l)),
                        lambda p: _matmul(h2, full('w_gate_up'), name="w_gate_up_fwd", plan=p, chips='b'))
    act = gathered.carry(sched.get(('swiglu_fwd', l)), lambda p: _swiglu_fwd(gu, plan=p))
    ffn = gathered.carry(sched.get(('w_down_fwd', l)), lambda p: _matmul(act, full('w_down'), name="w_down_fwd", plan=p))
    ffn = ffn.reshape(b, s, d)
    x_out = _gated_add(x_mid, gate2, ffn)
    res = (x, x_mid, h1, h2, o, mix, gu, act, ffn, mixer_vjp, scale1, gate1, scale2, gate2, g1, g2)
    return x_out, res


def _layer_bwd(res, gathered, reducer, l, depth, dx_out):
    x, x_mid, h1, h2, o, mix, gu, act, ffn, mixer_vjp, scale1, gate1, scale2, gate2, g1, g2 = res
    b, s, d = x.shape
    m = b * s
    full = lambda n: gathered.full[(n, l)]
    above = l + 1 < depth
    dffn, dgate2 = _gated_bwd(dx_out, gate2, ffn)
    dffn = dffn.reshape(m, d)
    dact = reducer.carry([('w_out', l + 1), ('w_uq', l + 1), ('w_ukv', l + 1)] if above else [],
                         lambda p: _matmul(dffn, full('w_down'), tb=True, name="w_down_dx", plan=p))
    dw = reducer.carry([('w_in', l + 1)] if above else [],
                       lambda p: _matmul(act, dffn, ta=True, name="w_down_dw", plan=p))
    reducer.add(('w_down', l), dw)
    dgu = _swiglu_bwd(gu, dact)
    dh2 = reducer.carry([('w_down', l)], lambda p: _matmul(dgu, full('w_gate_up'), tb=True, name="w_gate_up_dx", plan=p,
                                                           chips='b'))
    dh2 = dh2.reshape(b, s, d)
    reducer.add(('w_gate_up', l), _matmul(h2, dgu, ta=True, name="w_gate_up_dw", chips='out'))
    dx_mid, dg2, dscale2, dshift2 = _normmod_bwd(x_mid, g2, scale2, dh2, dx_out)
    dmix, dgate1 = _gated_bwd(dx_mid, gate1, mix)
    dmix = dmix.reshape(m, d)
    do = _matmul(dmix, full('w_out'), tb=True, name="w_out_dx")
    reducer.add(('w_out', l), _matmul(o, dmix, ta=True, name="w_out_dw"))
    dproj, grads = mixer_vjp(do)
    grads = dict(grads)
    reducer.add(('w_uq', l), grads.pop('w_uq'))
    reducer.add(('w_ukv', l), grads.pop('w_ukv'))
    dproj = dproj.astype(BF16)
    dh1 = reducer.carry([('w_gate_up', l)], lambda p: _matmul(dproj, full('w_in'), tb=True, name="w_in_dx", plan=p, chips='b'),
                        piece=0)
    dh1 = dh1.reshape(b, s, d)
    dw = reducer.carry([('w_gate_up', l)], lambda p: _matmul(h1, dproj, ta=True, name="w_in_dw", plan=p, chips='out'),
                       piece=1)
    reducer.add(('w_in', l), dw)
    dx, dg1, dscale1, dshift1 = _normmod_bwd(x, g1, scale1, dh1, dx_mid)
    grads["norm1_g"], grads["norm2_g"] = dg1[0], dg2[0]
    dmod = jnp.concatenate([dshift1, dscale1, dgate1, dshift2, dscale2, dgate2], axis=-1)[:, 0, :]
    return dx, dmod, grads


def _tail_loss(x2, final_norm_g, target2):
    return loss_head(rms_norm(x2, final_norm_g[None, :], 1), target2)


def _forward_backward(x, mod, small, gathered, reducer, final_norm_g, target):
    b, s, d = x.shape
    depth = len(small)
    rope_a = _axial_rope_tables(s, GQA_HEAD_DIM)
    rope_b = _axial_rope_tables(s, MLA_ROPE_DIM)
    sched = _gather_schedule(depth)
    first = sched['first']
    gathered.store(first, _run_plan(gathered.plan(first), "all_gather_chips"))
    saved = []
    for l in range(depth):
        x, res = _layer_fwd(x, mod[l], small[l], gathered, l, sched, rope_a, rope_b)
        saved.append(res)
    loss, (dx2, dfinal) = jax.value_and_grad(_tail_loss, argnums=(0, 1))(
        x.reshape(b * s, d), final_norm_g, target.reshape(b * s, d))
    dx = dx2.reshape(b, s, d)
    dmods, gsmall = [None] * depth, [None] * depth
    for l in reversed(range(depth)):
        dx, dmods[l], gsmall[l] = _layer_bwd(saved[l], gathered, reducer, l, depth, dx)
    return loss, dx, jnp.stack(dmods), gsmall, dfinal


ANY = pl.BlockSpec(memory_space=pl.ANY)


def _flip_if(v, bit):
    return 1 - v if bit else v


def _all_gather_devices(x):
    def body(x_ref, out_ref, send_sems, recv_sems):
        mx, my, mc = lax.axis_index("x"), lax.axis_index("y"), lax.axis_index("c")
        me = 4 * mx + 2 * my + mc
        sends = []
        for k in range(1, N_DEV):
            peer = (_flip_if(mx, k & 4), _flip_if(my, k & 2), _flip_if(mc, k & 1))
            cp = pltpu.make_async_remote_copy(src_ref=x_ref, dst_ref=out_ref.at[me], send_sem=send_sems.at[k - 1],
                                              recv_sem=recv_sems.at[k - 1], device_id=peer, device_id_type=MESH)
            cp.start()
            sends.append(cp)
        for k in range(1, N_DEV):
            peer = (_flip_if(mx, k & 4), _flip_if(my, k & 2), _flip_if(mc, k & 1))
            src = 4 * peer[0] + 2 * peer[1] + peer[2]
            pltpu.make_async_remote_copy(src_ref=x_ref, dst_ref=out_ref.at[src], send_sem=send_sems.at[k - 1],
                                         recv_sem=recv_sems.at[k - 1], device_id=peer, device_id_type=MESH).wait_recv()
        for cp in sends:
            cp.wait_send()

    out = pl.pallas_call(
        body, name="all_gather_devices", in_specs=[ANY], out_specs=ANY,
        out_shape=jax.ShapeDtypeStruct((N_DEV,) + x.shape, x.dtype),
        scratch_shapes=[pltpu.SemaphoreType.DMA((N_DEV - 1,)), pltpu.SemaphoreType.DMA((N_DEV - 1,))],
    )(x)
    me = 4 * lax.axis_index("x") + 2 * lax.axis_index("y") + lax.axis_index("c")
    return lax.dynamic_update_index_in_dim(out, x, me, 0)


class _Plan:
    def __init__(self, inputs, out_shapes, sem_counts, start, finish):
        self.inputs, self.out_shapes, self.sem_counts = list(inputs), list(out_shapes), list(sem_counts)
        self.start, self.finish = start, finish

    def specs(self):
        return ([ANY] * len(self.inputs), [ANY] * len(self.out_shapes),
                [pltpu.SemaphoreType.DMA((c,)) for c in self.sem_counts])

    def split(self, refs):
        a, b = len(self.inputs), len(self.inputs) + len(self.out_shapes)
        return refs[:a], refs[a:b], refs[b:]


def _run_plan(plan, name):
    def body(*refs):
        ins, outs, sems = plan.split(refs)
        plan.start(ins, outs, sems)
        plan.finish(ins, outs, sems)

    in_specs, out_specs, scratch = plan.specs()
    return pl.pallas_call(body, name=name, in_specs=in_specs, out_specs=out_specs, out_shape=plan.out_shapes,
                          scratch_shapes=scratch)(*plan.inputs)


def _gather_plan(shards):
    n = len(shards)
    halves = [t.reshape(2, t.shape[0] // 2, t.shape[1]) for t in shards]
    count = (N_CHIPS - 1) * n

    def copies(kind, ins, outs, sems):
        ici_send, ici_recv, d2d_send, d2d_recv, own_send, own_recv = sems
        mx, my, mc = lax.axis_index("x"), lax.axis_index("y"), lax.axis_index("c")
        me = 2 * mx + my
        sibling = (mx, my, 1 - mc)
        if kind == 'own':
            return [pltpu.make_async_remote_copy(src_ref=ins[i], dst_ref=outs[i].at[me], send_sem=own_send.at[i],
                                                 recv_sem=own_recv.at[i], device_id=sibling, device_id_type=MESH)
                    for i in range(n)]
        cps = []
        for k in range(1, N_CHIPS):
            peer = (_flip_if(mx, k & 2), _flip_if(my, k & 1), mc)
            src = 2 * peer[0] + peer[1]
            for i in range(n):
                j = (k - 1) * n + i
                if kind in ('ici', 'landed'):
                    dst = outs[i].at[me, mc] if kind == 'ici' else outs[i].at[src, mc]
                    cps.append(pltpu.make_async_remote_copy(
                        src_ref=ins[i].at[mc], dst_ref=dst, send_sem=ici_send.at[j], recv_sem=ici_recv.at[j],
                        device_id=peer, device_id_type=MESH))
                else:
                    half = outs[i].at[src, mc] if kind == 'fwd' else outs[i].at[src, 1 - mc]
                    cps.append(pltpu.make_async_remote_copy(
                        src_ref=half, dst_ref=half, send_sem=d2d_send.at[j], recv_sem=d2d_recv.at[j],
                        device_id=sibling, device_id_type=MESH))
        return cps

    def start(ins, outs, sems):
        for cp in copies('own', ins, outs, sems) + copies('ici', ins, outs, sems):
            cp.start()

    def finish(ins, outs, sems):
        fwd = copies('fwd', ins, outs, sems)
        for arrived, onward in zip(copies('landed', ins, outs, sems), fwd):
            arrived.wait_recv()
            onward.start()
        own = copies('own', ins, outs, sems)
        for cp in copies('fwd_in', ins, outs, sems) + own:
            cp.wait_recv()
        for cp in own + copies('ici', ins, outs, sems) + fwd:
            cp.wait_send()

    out_shapes = [jax.ShapeDtypeStruct((N_CHIPS,) + t.shape, t.dtype) for t in halves]
    return _Plan(halves, out_shapes, [count] * 4 + [n] * 2, start, finish)


def _sibling_exchange(blocks, name):
    n = len(blocks)

    def body(*refs):
        ins, outs = refs[:n], refs[n:2 * n]
        send_sems, recv_sems = refs[2 * n:]
        mx, my, mc = lax.axis_index("x"), lax.axis_index("y"), lax.axis_index("c")
        cps = []
        for i in range(n):
            cp = pltpu.make_async_remote_copy(src_ref=ins[i], dst_ref=outs[i], send_sem=send_sems.at[i],
                                              recv_sem=recv_sems.at[i], device_id=(mx, my, 1 - mc),
                                              device_id_type=MESH)
            cp.start()
            cps.append(cp)
        for cp in cps:
            cp.wait()

    return pl.pallas_call(
        body, name=name, in_specs=[ANY] * n, out_specs=[ANY] * n,
        out_shape=[jax.ShapeDtypeStruct(t.shape, t.dtype) for t in blocks],
        scratch_shapes=[pltpu.SemaphoreType.DMA((n,)), pltpu.SemaphoreType.DMA((n,))],
    )(*blocks)


def _add_halves(own, recv):
    nb, r, c = own.shape
    tr = _row_tile(r, c * 4)

    def body(g_ref, r_ref, o_ref):
        o_ref[...] = (g_ref[...] + r_ref[...].astype(F32)).astype(BF16)

    spec = pl.BlockSpec((1, tr, c), lambda k, i: (k, i, 0))
    return pl.pallas_call(
        body, name="rs_add_halves", grid=(nb, r // tr), in_specs=[spec, spec], out_specs=spec,
        out_shape=jax.ShapeDtypeStruct((nb, r, c), BF16),
        compiler_params=pltpu.CompilerParams(dimension_semantics=("parallel", "parallel")),
    )(own, recv)


def _chip_exchange_plan(jobs):
    n = len(jobs)
    count = (N_CHIPS - 1) * n

    def copies(ins, outs, sems):
        send_sems, recv_sems = sems
        mx, my, mc = lax.axis_index("x"), lax.axis_index("y"), lax.axis_index("c")
        cps = []
        for k in range(1, N_CHIPS):
            peer = (_flip_if(mx, k & 2), _flip_if(my, k & 1), mc)
            dst_chip = 2 * peer[0] + peer[1]
            for i, (_, row0, rows) in enumerate(jobs):
                j = (k - 1) * n + i
                cps.append(pltpu.make_async_remote_copy(
                    src_ref=ins[i].at[dst_chip, pl.ds(row0, rows)], dst_ref=outs[i].at[k - 1],
                    send_sem=send_sems.at[j], recv_sem=recv_sems.at[j], device_id=peer, device_id_type=MESH))
        return cps

    def start(ins, outs, sems):
        for cp in copies(ins, outs, sems):
            cp.start()

    def finish(ins, outs, sems):
        for cp in copies(ins, outs, sems):
            cp.wait()

    out_shapes = [jax.ShapeDtypeStruct((N_CHIPS - 1, rows, p.shape[2]), p.dtype) for p, _, rows in jobs]
    return _Plan([p for p, _, _ in jobs], out_shapes, [count, count], start, finish)


def _sum_chips(parts, recv, chip, row0):
    _, rows, c = recv.shape
    tr = _row_tile(rows, c * 4)
    assert row0 % tr == 0

    def body(chip_ref, p_ref, r_ref, o_ref):
        acc = p_ref[0].astype(F32)
        for k in range(N_CHIPS - 1):
            acc = acc + r_ref[k].astype(F32)
        o_ref[...] = acc

    return pl.pallas_call(
        body, name="rs_sum_chips",
        grid_spec=pltpu.PrefetchScalarGridSpec(
            num_scalar_prefetch=1, grid=(rows // tr,),
            in_specs=[pl.BlockSpec((1, tr, c), lambda i, chip_ref: (chip_ref[0], i + row0 // tr, 0)),
                      pl.BlockSpec((N_CHIPS - 1, tr, c), lambda i, chip_ref: (0, i, 0))],
            out_specs=pl.BlockSpec((tr, c), lambda i, chip_ref: (i, 0))),
        out_shape=jax.ShapeDtypeStruct((rows, c), F32),
        compiler_params=pltpu.CompilerParams(dimension_semantics=("parallel",)),
    )(chip, parts, recv)


def _sum_leading(t, name):
    nb, r, c = t.shape
    tr = _row_tile(r, c * 4 * nb)

    def body(t_ref, o_ref):
        acc = t_ref[0]
        for k in range(1, nb):
            acc = acc + t_ref[k]
        o_ref[...] = acc

    return pl.pallas_call(
        body, name=name, grid=(r // tr,),
        in_specs=[pl.BlockSpec((nb, tr, c), lambda i: (0, i, 0))],
        out_specs=pl.BlockSpec((tr, c), lambda i: (i, 0)),
        out_shape=jax.ShapeDtypeStruct((r, c), F32),
        compiler_params=pltpu.CompilerParams(dimension_semantics=("parallel",)),
    )(t)


def _rs_parts(grad):
    mc = lax.axis_index("c")
    split = grad.reshape(grad.shape[0], 2, grad.shape[1] // 2, grad.shape[2])
    own = lax.dynamic_index_in_dim(split, mc, axis=1, keepdims=False)
    away = lax.dynamic_index_in_dim(split, 1 - mc, axis=1, keepdims=False).astype(BF16)
    return _add_halves(own, _sibling_exchange([away], "rs_sibling_exchange")[0])


def _rs_result(parts, pieces):
    mc = lax.axis_index("c")
    chip = (2 * lax.axis_index("x") + lax.axis_index("y")).astype(jnp.int32).reshape(1)
    mine = [_sum_chips(parts, recv, chip, row0) for row0, recv in pieces]
    mine = mine[0] if len(mine) == 1 else jnp.concatenate(mine, axis=0)
    theirs = _sibling_exchange([mine], "rs_sibling_swap")[0]
    return jnp.concatenate([jnp.where(mc == 0, mine, theirs), jnp.where(mc == 0, theirs, mine)], axis=0)


def _adamw(w, g, m, v, plan=None):
    shape = w.shape
    cols = shape[-1]
    rows = int(np.prod(shape[:-1])) if len(shape) > 1 else 1
    w2, g2, m2, v2 = [t.reshape(rows, cols) for t in (w, g, m, v)]
    tr = _row_tile(rows, cols * 4 * 4)
    steps = rows // tr
    p_in, p_out, p_shapes, p_scratch, p_args = _plan_specs(plan)

    def body(w_ref, g_ref, m_ref, v_ref, *rest):
        d_ref, mo_ref, vo_ref = rest[len(p_in):len(p_in) + 3]
        i = pl.program_id(0)

        def compute():
            gv = g_ref[...]
            mn = ADAM_B1 * m_ref[...] + (1.0 - ADAM_B1) * gv
            vn = ADAM_B2 * v_ref[...] + (1.0 - ADAM_B2) * (gv * gv)
            m_hat = mn / (1.0 - ADAM_B1 ** ADAM_STEP)
            v_hat = vn / (1.0 - ADAM_B2 ** ADAM_STEP)
            d_ref[...] = -ADAM_LR * (m_hat / (jnp.sqrt(v_hat) + ADAM_EPS) + ADAM_WD * w_ref[...])
            mo_ref[...] = mn
            vo_ref[...] = vn

        _ride(plan, rest[:len(p_in)] + rest[len(p_in) + 3:], i == 0, i == steps - 1, compute)

    spec = pl.BlockSpec((tr, cols), lambda i: (i, 0))
    outs = pl.pallas_call(
        body, name="adamw", grid=(steps,), in_specs=[spec] * 4 + p_in, out_specs=[spec] * 3 + p_out,
        out_shape=[jax.ShapeDtypeStruct((rows, cols), F32)] * 3 + p_shapes, scratch_shapes=p_scratch,
        compiler_params=pltpu.CompilerParams(dimension_semantics=("arbitrary" if plan is not None else "parallel",)),
    )(w2, g2, m2, v2, *p_args)
    res = [t.reshape(shape) for t in outs[:3]]
    return res if plan is None else (res, list(outs[3:]))


WEIGHTS = ['w_ada', 'b_ada', 'norm1_g', 'norm2_g', 'w_in', 'q_norm_g', 'k_norm_g', 'mla_q_norm_g', 'w_uq',
           'mla_kv_norm_g', 'w_ukv', 'conv_w', 'conv_b', 'dt_bias', 'a_log', 'd_skip', 'ssd_norm_g', 'w_out',
           'w_gate_up', 'w_down', 'final_norm_g']
COL_SHARDED = ('w_in', 'w_uq', 'w_ukv', 'w_gate_up')
ROW_SHARDED = ('w_out', 'w_down')
CHIP_BLOCKED = ('w_in', 'w_gate_up')
SMALL_LAYER = ('norm1_g', 'norm2_g', 'q_norm_g', 'k_norm_g', 'mla_q_norm_g', 'mla_kv_norm_g', 'conv_w', 'conv_b',
               'dt_bias', 'a_log', 'd_skip', 'ssd_norm_g')


def _pack(parts):
    flat = jnp.concatenate([p.reshape(-1) for p in parts])
    n = flat.shape[0]
    rows = -(-n // (8 * LANES)) * 8
    return jnp.pad(flat, (0, rows * LANES - n)).reshape(rows, LANES)


def _unpack(flat, shapes):
    out, pos = [], 0
    for shp in shapes:
        size = int(np.prod(shp))
        out.append(flat[pos:pos + size].reshape(shp))
        pos += size
    return out


def _cols_full(gathered):
    k, r, c = gathered.shape
    return jnp.transpose(gathered, (1, 0, 2)).reshape(r, k * c)


def _cols_split(full):
    r, c4 = full.shape
    return jnp.transpose(full.reshape(r, N_CHIPS, c4 // N_CHIPS), (1, 0, 2))


def kernel(x, c, w_ada, b_ada, norm1_g, norm2_g, w_in, q_norm_g, k_norm_g, mla_q_norm_g, w_uq, mla_kv_norm_g, w_ukv, conv_w, conv_b, dt_bias, a_log, d_skip, ssd_norm_g, w_out, w_gate_up, w_down, final_norm_g, loss_target, m_w_ada, m_b_ada, m_norm1_g, m_norm2_g, m_w_in, m_q_norm_g, m_k_norm_g, m_mla_q_norm_g, m_w_uq, m_mla_kv_norm_g, m_w_ukv, m_conv_w, m_conv_b, m_dt_bias, m_a_log, m_d_skip, m_ssd_norm_g, m_w_out, m_w_gate_up, m_w_down, m_final_norm_g, v_w_ada, v_b_ada, v_norm1_g, v_norm2_g, v_w_in, v_q_norm_g, v_k_norm_g, v_mla_q_norm_g, v_w_uq, v_mla_kv_norm_g, v_w_ukv, v_conv_w, v_conv_b, v_dt_bias, v_a_log, v_d_skip, v_ssd_norm_g, v_w_out, v_w_gate_up, v_w_down, v_final_norm_g):
    args = dict(locals())
    weights = {n: args[n] for n in WEIGHTS}
    depth = w_in.shape[0]
    bl, s, d = x.shape
    mx, my, mc = lax.axis_index("x"), lax.axis_index("y"), lax.axis_index("c")
    chip = 2 * mx + my
    dev = 2 * chip + mc
    ada_cols = w_ada.shape[-1]
    conv_cols = conv_w.shape[-1]

    first_shapes = [c.shape, conv_w.shape]
    first = _all_gather_devices(_pack([c, conv_w]))
    first = [_unpack(first[i].reshape(-1), first_shapes) for i in range(N_DEV)]
    c_act = jax.nn.silu(jnp.concatenate([f[0] for f in first], axis=0))
    conv_w_full = jnp.concatenate([first[2 * k][1] for k in range(N_CHIPS)], axis=-1)

    b_cols = lax.dynamic_slice_in_dim(b_ada, chip * ada_cols, ada_cols, axis=1)
    c_act_b = c_act.astype(BF16)
    mod_cols = jnp.stack([_matmul(c_act_b, w_ada[l].astype(BF16), name="ada_fwd") + b_cols[l][None, :]
                          for l in range(depth)])
    mod_all = _all_gather_devices(mod_cols.reshape(depth * N_DEV * bl, ada_cols))
    mod_all = mod_all.reshape(N_DEV, depth, N_DEV, bl, ada_cols)
    mod_mine = lax.dynamic_index_in_dim(mod_all, dev, axis=2, keepdims=False)
    mod = jnp.concatenate([mod_mine[2 * k] for k in range(N_CHIPS)], axis=-1)

    big = COL_SHARDED + ROW_SHARDED
    shards = {(n, l): weights[n][l].astype(BF16) for n in big for l in range(depth)}
    for l in range(depth):
        shards[('w_in', l)] = jnp.pad(shards[('w_in', l)], ((0, 0), (0, IN_SHARD_PAD - IN_SHARD)))
    gathered = _Gathered(shards)
    reducer = _Reducer()
    small_w = []
    for l in range(depth):
        w = {n: weights[n][l] for n in SMALL_LAYER if n != 'conv_w'}
        w['conv_w'] = conv_w_full[l]
        small_w.append(w)
    loss_local, gx, gmod, glayers, gfinal = _forward_backward(x, mod, small_w, gathered, reducer, final_norm_g,
                                                              loss_target)

    small_parts = [jnp.stack([glayers[l][n] for l in range(depth)]) for n in SMALL_LAYER]
    small_parts += [gfinal, loss_local.reshape(1), gmod]
    small_shapes = [p.shape for p in small_parts]
    last = _all_gather_devices(_pack(small_parts))
    summed = _unpack(_sum_leading(last, "sum_devices").reshape(-1), small_shapes)
    small = dict(zip(SMALL_LAYER, summed[:len(SMALL_LAYER)]))
    g_final, loss, gmod_sum = summed[len(SMALL_LAYER):]
    small['conv_w'] = lax.dynamic_slice_in_dim(small['conv_w'], chip * conv_cols, conv_cols, axis=2)
    gmod_all = jnp.stack([_unpack(last[i].reshape(-1), small_shapes)[-1] for i in range(N_DEV)], axis=1)
    gmod_all = gmod_all.reshape(depth, N_DEV * bl, gmod.shape[-1])
    gmod_cols = lax.dynamic_slice_in_dim(gmod_all, chip * ada_cols, ada_cols, axis=2)
    g_w_ada = jnp.stack([_matmul(c_act_b, gmod_cols[l].astype(BF16), ta=True, name="ada_dw") for l in range(depth)])
    g_b_ada = gmod_sum[:, 0]
    for i in range(1, bl):
        g_b_ada = g_b_ada + gmod_sum[:, i]

    grad = {'w_ada': g_w_ada, 'b_ada': g_b_ada, 'final_norm_g': g_final}
    for n in SMALL_LAYER:
        grad[n] = small[n]

    delta, new_m, new_v = {}, {}, {}
    left = [key for key in reducer.parts if key not in reducer.result and not reducer.recv[key]]
    delta['w_ada'], new_m['w_ada'], new_v['w_ada'] = reducer.carry(
        left, lambda p: _adamw(w_ada, g_w_ada, m_w_ada, v_w_ada, plan=p))
    reducer.flush()
    for n in big:
        grad[n] = jnp.stack([reducer.result[(n, l)] for l in range(depth)])[..., :weights[n].shape[-1]]
    for n in WEIGHTS:
        if n != 'w_ada':
            delta[n], new_m[n], new_v[n] = _adamw(weights[n], grad[n], args["m_" + n], args["v_" + n])
    return (loss.reshape(()), gx, *[grad[n] for n in WEIGHTS], *[delta[n] for n in WEIGHTS],
            *[new_m[n] for n in WEIGHTS], *[new_v[n] for n in WEIGHTS])
```

```python
import functools

import numpy as np
import jax
import jax.numpy as jnp
from jax import lax
from jax.experimental import pallas as pl
from jax.experimental.pallas import tpu as pltpu

F32 = jnp.float32
BF16 = jnp.bfloat16
HIGHEST = lax.Precision.HIGHEST
MESH = pl.DeviceIdType.MESH

GRID_W = 64
ROPE_THETA = 10000.0
EPS = 1e-6

GQA_HEADS, GQA_KV_HEADS, GQA_HEAD_DIM = 6, 2, 128
GQA_WIDTH = GQA_HEADS * GQA_HEAD_DIM
GQA_KV_WIDTH = GQA_KV_HEADS * GQA_HEAD_DIM
MLA_HEADS, MLA_Q_LORA, MLA_KV_LORA = 4, 512, 256
MLA_NOPE_DIM, MLA_ROPE_DIM, MLA_V_DIM = 128, 64, 128
SSD_HEADS, SSD_HEAD_DIM, SSD_GROUPS, SSD_STATE, SSD_CONV, SSD_CHUNK = 12, 64, 2, 128, 5, 128
SSD_INNER = SSD_HEADS * SSD_HEAD_DIM
SSD_CONV_DIM = SSD_INNER + 2 * SSD_GROUPS * SSD_STATE
SSD_GROUP_HEADS = SSD_HEADS // SSD_GROUPS
SSD_GROUP_WIDTH = SSD_GROUP_HEADS * SSD_HEAD_DIM
IN_SPLITS = (GQA_WIDTH, GQA_KV_WIDTH, GQA_KV_WIDTH, MLA_Q_LORA, MLA_KV_LORA, MLA_ROPE_DIM, SSD_INNER, SSD_CONV_DIM,
             2 * SSD_HEADS)
IN_COLS = sum(IN_SPLITS)
LANES = 128
N_CHIPS = 4
IN_SHARD = IN_COLS // N_CHIPS
IN_SHARD_PAD = -(-IN_SHARD // LANES) * LANES


def _in_cols(proj, lo, hi):
    parts = []
    for chip in range(lo // IN_SHARD, (hi - 1) // IN_SHARD + 1):
        a, z = max(lo, chip * IN_SHARD), min(hi, (chip + 1) * IN_SHARD)
        base = chip * IN_SHARD_PAD - chip * IN_SHARD
        parts.append(proj[:, base + a:base + z])
    return parts[0] if len(parts) == 1 else jnp.concatenate(parts, axis=-1)

ADAM_LR, ADAM_B1, ADAM_B2, ADAM_EPS, ADAM_WD, ADAM_STEP = 0.001, 0.9, 0.999, 1e-08, 0.01, 10

N_DEV = 8
TILE_BYTES = 2 * 1024 * 1024


def _pick(n, cands):
    for t in cands:
        if n % t == 0:
            return t
    return n


ADAM_TILE_BYTES = 3 * 512 * 1024
ADAM_VMEM_LIMIT = 40 * 1024 * 1024
SWIGLU_TILE_BYTES = 4 * 1024 * 1024


def _row_tile(rows, row_bytes, limit=TILE_BYTES):
    for t in (2048, 1024, 512, 256, 128, 64, 32, 16, 8):
        if rows % t == 0 and t * row_bytes <= limit:
            return t
    return rows


MM_VMEM_BUDGET = 36 * 1024 * 1024
MM_VMEM_LIMIT = 56 * 1024 * 1024
MM_MAX_TILE = 2048
MM_MAX_K_TILE = 4096
MXU_DIM = 256
HBM_BYTES_PER_US = 3.0e6
MXU_FLOPS_PER_US = 9.0e8
STEP_US = 0.35


def _tile_cands(d, cap):
    if d % LANES:
        return [d]
    return [t for t in range(LANES, min(d, cap) + 1, LANES) if d % t == 0] or [d]


def _mm_tiles(m, n, kdim, n_unit=None, k_unit=None):
    up = lambda t: -(-t // MXU_DIM) * MXU_DIM
    best = None
    for tm in _tile_cands(m, MM_MAX_TILE):
        for tn in _tile_cands(n_unit or n, MM_MAX_TILE):
            for tk in _tile_cands(k_unit or kdim, MM_MAX_K_TILE):
                if 2 * (tm * tk * 2 + tk * tn * 2 + tm * tn * 4) > MM_VMEM_BUDGET:
                    continue
                ni, nj, nk = m // tm, n // tn, kdim // tk
                a_reads = 1 if nk == 1 else nj
                b_reads = 1 if (nk == 1 and nj == 1) else ni
                hbm = (m * kdim * 2 * a_reads + kdim * n * 2 * b_reads + m * n * 4) / HBM_BYTES_PER_US
                mxu = ni * nj * nk * 2.0 * max(tm, 8) * up(tn) * up(tk) / MXU_FLOPS_PER_US
                cost = max(hbm, mxu) + 0.25 * min(hbm, mxu) + ni * nj * nk * STEP_US
                if best is None or cost < best[0]:
                    best = (cost, tm, tn, tk)
    return best[1:]


def _ride(plan, refs, first, last, compute):
    if plan is None:
        compute()
        return
    ins, outs, sems = plan.split(refs)

    @pl.when(first)
    def _():
        plan.start(ins, outs, sems)

    compute()

    @pl.when(last)
    def _():
        plan.finish(ins, outs, sems)


def _plan_specs(plan):
    if plan is None:
        return [], [], [], [], []
    in_specs, out_specs, scratch = plan.specs()
    return in_specs, out_specs, plan.out_shapes, scratch, plan.inputs


def _matmul(a, b, ta=False, tb=False, name="mm", plan=None, chips=None):
    assert a.dtype == BF16 and b.dtype == BF16, (a.dtype, b.dtype)
    if ta:
        kdim, m = a.shape
    else:
        m, kdim = a.shape
    n_unit = k_unit = None
    if chips == 'b':
        nb, rows, unit = b.shape
        if tb:
            n, k2, k_unit = rows, nb * unit, unit
        else:
            k2, n, n_unit = rows, nb * unit, unit
    else:
        if tb:
            n, k2 = b.shape
        else:
            k2, n = b.shape
        if chips == 'out':
            n_unit = n // N_CHIPS
    assert kdim == k2, (a.shape, b.shape, ta, tb)
    tm, tn, tk = _mm_tiles(m, n, kdim, n_unit, k_unit)
    ni, nj, nk = m // tm, n // tn, kdim // tk
    dn = (((0 if ta else 1,), (1 if tb else 0,)), ((), ()))
    p_in, p_out, p_shapes, p_scratch, p_args = _plan_specs(plan)

    def body(a_ref, b_ref, *rest):
        o_ref = rest[len(p_in)]
        i, j, k = pl.program_id(0), pl.program_id(1), pl.program_id(2)

        def compute():
            bv = b_ref[0] if chips == 'b' else b_ref[...]
            part = lax.dot_general(a_ref[...], bv, dn, preferred_element_type=F32)
            if chips == 'out':
                part = part[None]
            if nk == 1:
                o_ref[...] = part
            else:
                @pl.when(k == 0)
                def _():
                    o_ref[...] = part

                @pl.when(k > 0)
                def _():
                    o_ref[...] += part

        _ride(plan, rest[:len(p_in)] + rest[len(p_in) + 1:], (i == 0) & (j == 0) & (k == 0),
              (i == ni - 1) & (j == nj - 1) & (k == nk - 1), compute)

    a_spec = pl.BlockSpec((tk, tm), lambda i, j, k: (k, i)) if ta else pl.BlockSpec((tm, tk), lambda i, j, k: (i, k))
    if chips == 'b' and tb:
        per = k_unit // tk
        b_spec = pl.BlockSpec((1, tn, tk), lambda i, j, k: (k // per, j, k % per))
    elif chips == 'b':
        per = n_unit // tn
        b_spec = pl.BlockSpec((1, tk, tn), lambda i, j, k: (j // per, k, j % per))
    else:
        b_spec = pl.BlockSpec((tn, tk), lambda i, j, k: (j, k)) if tb else pl.BlockSpec((tk, tn), lambda i, j, k: (k, j))
    if chips == 'out':
        per = n_unit // tn
        o_spec = pl.BlockSpec((1, tm, tn), lambda i, j, k: (j // per, i, j % per))
        o_shape = jax.ShapeDtypeStruct((N_CHIPS, m, n_unit), F32)
    else:
        o_spec = pl.BlockSpec((tm, tn), lambda i, j, k: (i, j))
        o_shape = jax.ShapeDtypeStruct((m, n), F32)
    outs = pl.pallas_call(
        body, name=name, grid=(ni, nj, nk),
        in_specs=[a_spec, b_spec] + p_in, out_specs=[o_spec] + p_out,
        out_shape=[o_shape] + p_shapes, scratch_shapes=p_scratch,
        compiler_params=pltpu.CompilerParams(
            dimension_semantics=("arbitrary" if plan is not None else "parallel", "arbitrary", "arbitrary"),
            vmem_limit_bytes=MM_VMEM_LIMIT),
    )(a, b, *p_args)
    return outs[0] if plan is None else (outs[0], list(outs[1:]))


@jax.custom_vjp
def linear(x, w):
    return _matmul(x.astype(BF16), w.astype(BF16), name="linear_fwd")


def _linear_fwd(x, w):
    xb, wb = x.astype(BF16), w.astype(BF16)
    return _matmul(xb, wb, name="linear_fwd"), (xb, wb)


def _linear_bwd(res, dy):
    xb, wb = res
    dyb = dy.astype(BF16)
    return _matmul(dyb, wb, tb=True, name="linear_dx"), _matmul(xb, dyb, ta=True, name="linear_dw")


linear.defvjp(_linear_fwd, _linear_bwd)


def _rms_fwd_call(x, g, groups):
    rows, cols = x.shape
    d = cols // groups
    tr = _row_tile(rows, cols * 4)

    def body(x_ref, g_ref, y_ref):
        for gi in range(groups):
            sl = slice(gi * d, (gi + 1) * d)
            xs = x_ref[:, sl]
            r = lax.rsqrt(jnp.mean(xs * xs, axis=-1, keepdims=True) + EPS)
            y_ref[:, sl] = xs * r * g_ref[:, sl]

    return pl.pallas_call(
        body, name="rms_fwd", grid=(rows // tr,),
        in_specs=[pl.BlockSpec((tr, cols), lambda i: (i, 0)), pl.BlockSpec((1, cols), lambda i: (0, 0))],
        out_specs=pl.BlockSpec((tr, cols), lambda i: (i, 0)),
        out_shape=jax.ShapeDtypeStruct((rows, cols), F32),
        compiler_params=pltpu.CompilerParams(dimension_semantics=("parallel",)),
    )(x, g)


def _rms_bwd_call(x, g, dy, groups):
    rows, cols = x.shape
    d = cols // groups
    tr = _row_tile(rows, cols * 4)

    def body(x_ref, g_ref, dy_ref, dx_ref, dg_ref):
        @pl.when(pl.program_id(0) == 0)
        def _():
            dg_ref[...] = jnp.zeros_like(dg_ref)

        for gi in range(groups):
            sl = slice(gi * d, (gi + 1) * d)
            xs = x_ref[:, sl]
            dys = dy_ref[:, sl]
            r = lax.rsqrt(jnp.mean(xs * xs, axis=-1, keepdims=True) + EPS)
            xhat = xs * r
            dg_ref[:, sl] += jnp.sum(dys * xhat, axis=0, keepdims=True)
            dxhat = dys * g_ref[:, sl]
            dx_ref[:, sl] = r * (dxhat - xhat * jnp.mean(dxhat * xhat, axis=-1, keepdims=True))

    return pl.pallas_call(
        body, name="rms_bwd", grid=(rows // tr,),
        in_specs=[pl.BlockSpec((tr, cols), lambda i: (i, 0)), pl.BlockSpec((1, cols), lambda i: (0, 0)),
                  pl.BlockSpec((tr, cols), lambda i: (i, 0))],
        out_specs=[pl.BlockSpec((tr, cols), lambda i: (i, 0)), pl.BlockSpec((1, cols), lambda i: (0, 0))],
        out_shape=[jax.ShapeDtypeStruct((rows, cols), F32), jax.ShapeDtypeStruct((1, cols), F32)],
        compiler_params=pltpu.CompilerParams(dimension_semantics=("arbitrary",)),
    )(x, g, dy)


@functools.partial(jax.custom_vjp, nondiff_argnums=(2,))
def rms_norm(x, g, groups):
    return _rms_fwd_call(x, g, groups)


def _rms_norm_fwd(x, g, groups):
    return _rms_fwd_call(x, g, groups), (x, g)


def _rms_norm_bwd(groups, res, dy):
    x, g = res
    dx, dg = _rms_bwd_call(x, g, dy, groups)
    return dx, dg


rms_norm.defvjp(_rms_norm_fwd, _rms_norm_bwd)


NT_DIMS = (((1,), (1,)), ((), ()))
TN_DIMS = (((0,), (0,)), ((), ()))


LOG2E = 1.4426950408889634
ATTN_VMEM_LIMIT = 48 * 1024 * 1024


def _exp_rows(q, k, scale):
    s2 = lax.dot_general(q, k, NT_DIMS, preferred_element_type=F32) * (scale * LOG2E)
    e = jnp.exp2(s2 - jnp.max(s2, axis=-1, keepdims=True))
    return e, 1.0 / jnp.sum(e, axis=-1, keepdims=True)


def _attn_fwd_call(q, k, v, scale):
    b, h, s, dk = q.shape
    hkv, dv = k.shape[1], v.shape[3]
    rep = h // hkv
    tq = _pick(s, (512, 256, 128))

    def body(q_ref, k_ref, v_ref, o_ref):
        e, inv = _exp_rows(q_ref[0, 0], k_ref[0, 0], scale)
        o_ref[0, 0] = jnp.dot(e.astype(BF16), v_ref[0, 0], preferred_element_type=F32) * inv

    return pl.pallas_call(
        body, name="attn_fwd", grid=(b, h, s // tq),
        in_specs=[pl.BlockSpec((1, 1, tq, dk), lambda bi, hi, qi: (bi, hi, qi, 0)),
                  pl.BlockSpec((1, 1, s, dk), lambda bi, hi, qi: (bi, hi // rep, 0, 0)),
                  pl.BlockSpec((1, 1, s, dv), lambda bi, hi, qi: (bi, hi // rep, 0, 0))],
        out_specs=pl.BlockSpec((1, 1, tq, dv), lambda bi, hi, qi: (bi, hi, qi, 0)),
        out_shape=jax.ShapeDtypeStruct((b, h, s, dv), F32),
        compiler_params=pltpu.CompilerParams(dimension_semantics=("parallel", "parallel", "parallel"),
                                             vmem_limit_bytes=ATTN_VMEM_LIMIT),
    )(q, k, v)


def _attn_bwd_call(q, k, v, do, scale):
    b, h, s, dk = q.shape
    hkv, dv = k.shape[1], v.shape[3]
    rep = h // hkv
    tq = _pick(s, (256, 128))

    def body(q_ref, k_ref, v_ref, do_ref, dq_ref, dk_ref, dv_ref):
        @pl.when((pl.program_id(2) == 0) & (pl.program_id(3) == 0))
        def _():
            dk_ref[...] = jnp.zeros_like(dk_ref)
            dv_ref[...] = jnp.zeros_like(dv_ref)

        qb = q_ref[0, 0]
        kb = k_ref[0, 0]
        vb = v_ref[0, 0]
        dob = do_ref[0, 0]
        e, inv = _exp_rows(qb, kb, scale)
        dp = lax.dot_general(dob, vb, NT_DIMS, preferred_element_type=F32)
        delta = jnp.sum(e * dp, axis=-1, keepdims=True) * inv
        ds = (e * ((dp - delta) * (inv * scale))).astype(BF16)
        dq_ref[0, 0] = jnp.dot(ds, kb, preferred_element_type=F32)
        dk_ref[0, 0] += lax.dot_general(ds, qb, TN_DIMS, preferred_element_type=F32)
        dv_ref[0, 0] += lax.dot_general(e.astype(BF16), (dob.astype(F32) * inv).astype(BF16), TN_DIMS,
                                        preferred_element_type=F32)

    return pl.pallas_call(
        body, name="attn_bwd", grid=(b, hkv, rep, s // tq),
        in_specs=[pl.BlockSpec((1, 1, tq, dk), lambda bi, gi, ri, qi: (bi, gi * rep + ri, qi, 0)),
                  pl.BlockSpec((1, 1, s, dk), lambda bi, gi, ri, qi: (bi, gi, 0, 0)),
                  pl.BlockSpec((1, 1, s, dv), lambda bi, gi, ri, qi: (bi, gi, 0, 0)),
                  pl.BlockSpec((1, 1, tq, dv), lambda bi, gi, ri, qi: (bi, gi * rep + ri, qi, 0))],
        out_specs=[pl.BlockSpec((1, 1, tq, dk), lambda bi, gi, ri, qi: (bi, gi * rep + ri, qi, 0)),
                   pl.BlockSpec((1, 1, s, dk), lambda bi, gi, ri, qi: (bi, gi, 0, 0)),
                   pl.BlockSpec((1, 1, s, dv), lambda bi, gi, ri, qi: (bi, gi, 0, 0))],
        out_shape=[jax.ShapeDtypeStruct(q.shape, F32), jax.ShapeDtypeStruct(k.shape, F32),
                   jax.ShapeDtypeStruct(v.shape, F32)],
        compiler_params=pltpu.CompilerParams(
            dimension_semantics=("parallel", "parallel", "arbitrary", "arbitrary"), vmem_limit_bytes=ATTN_VMEM_LIMIT),
    )(q, k, v, do)


@functools.partial(jax.custom_vjp, nondiff_argnums=(3,))
def attention(q, k, v, scale):
    return _attn_fwd_call(q.astype(BF16), k.astype(BF16), v.astype(BF16), scale)


def _attention_fwd(q, k, v, scale):
    qb, kb, vb = q.astype(BF16), k.astype(BF16), v.astype(BF16)
    return _attn_fwd_call(qb, kb, vb, scale), (qb, kb, vb)


def _attention_bwd(scale, res, do):
    qb, kb, vb = res
    return tuple(_attn_bwd_call(qb, kb, vb, do.astype(BF16), scale))


attention.defvjp(_attention_fwd, _attention_bwd)


CONV_COL_TILE = 256
CONV_PACK_ROWS = 8


def _shifted(x, off, rows):
    if off == 0:
        return x
    s = x.shape[0]
    rolled = pltpu.roll(x, (-off) % s, 0)
    valid = (rows + off >= 0) & (rows + off < s)
    return jnp.where(valid, rolled, 0.0)


def _conv_pre(x, wb_ref, rows):
    z = jnp.zeros_like(x) + wb_ref[SSD_CONV:SSD_CONV + 1, :]
    for j in range(SSD_CONV):
        z = z + wb_ref[j:j + 1, :] * _shifted(x, j - SSD_CONV // 2, rows)
    return z


def _conv_fwd_call(x, wb):
    b, s, c = x.shape
    tc = _pick(c, (CONV_COL_TILE, LANES))

    def body(x_ref, wb_ref, y_ref):
        xv = x_ref[0]
        rows = lax.broadcasted_iota(jnp.int32, xv.shape, 0)
        z = _conv_pre(xv, wb_ref, rows)
        y_ref[0] = z * jax.nn.sigmoid(z)

    return pl.pallas_call(
        body, name="conv_fwd", grid=(b, c // tc),
        in_specs=[pl.BlockSpec((1, s, tc), lambda bi, ci: (bi, 0, ci)),
                  pl.BlockSpec((CONV_PACK_ROWS, tc), lambda bi, ci: (0, ci))],
        out_specs=pl.BlockSpec((1, s, tc), lambda bi, ci: (bi, 0, ci)),
        out_shape=jax.ShapeDtypeStruct(x.shape, F32),
        compiler_params=pltpu.CompilerParams(dimension_semantics=("parallel", "parallel")),
    )(x, wb)


def _conv_bwd_call(x, wb, dy):
    b, s, c = x.shape
    tc = _pick(c, (CONV_COL_TILE, LANES))

    def body(x_ref, wb_ref, dy_ref, dx_ref, dwb_ref):
        xv = x_ref[0]
        rows = lax.broadcasted_iota(jnp.int32, xv.shape, 0)
        z = _conv_pre(xv, wb_ref, rows)
        sg = jax.nn.sigmoid(z)
        dz = dy_ref[0] * (sg * (1.0 + z * (1.0 - sg)))
        dx = jnp.zeros_like(xv)
        for j in range(SSD_CONV):
            off = j - SSD_CONV // 2
            dx = dx + wb_ref[j:j + 1, :] * _shifted(dz, -off, rows)
            dwb_ref[0, j:j + 1, :] = jnp.sum(dz * _shifted(xv, off, rows), axis=0, keepdims=True)
        dx_ref[0] = dx
        dwb_ref[0, SSD_CONV:SSD_CONV + 1, :] = jnp.sum(dz, axis=0, keepdims=True)
        dwb_ref[0, SSD_CONV + 1:, :] = jnp.zeros((CONV_PACK_ROWS - SSD_CONV - 1, dz.shape[1]), F32)

    return pl.pallas_call(
        body, name="conv_bwd", grid=(b, c // tc),
        in_specs=[pl.BlockSpec((1, s, tc), lambda bi, ci: (bi, 0, ci)),
                  pl.BlockSpec((CONV_PACK_ROWS, tc), lambda bi, ci: (0, ci)),
                  pl.BlockSpec((1, s, tc), lambda bi, ci: (bi, 0, ci))],
        out_specs=[pl.BlockSpec((1, s, tc), lambda bi, ci: (bi, 0, ci)),
                   pl.BlockSpec((1, CONV_PACK_ROWS, tc), lambda bi, ci: (bi, 0, ci))],
        out_shape=[jax.ShapeDtypeStruct(x.shape, F32), jax.ShapeDtypeStruct((b, CONV_PACK_ROWS, c), F32)],
        compiler_params=pltpu.CompilerParams(dimension_semantics=("parallel", "parallel")),
    )(x, wb, dy)


@jax.custom_vjp
def conv_silu(x, wb):
    return _conv_fwd_call(x, wb)


def _conv_silu_fwd(x, wb):
    return _conv_fwd_call(x, wb), (x, wb)


def _conv_silu_bwd(res, dy):
    x, wb = res
    dx, dwb = _conv_bwd_call(x, wb, dy)
    return dx, jnp.sum(dwb, axis=0)


conv_silu.defvjp(_conv_silu_fwd, _conv_silu_bwd)


SSD_PAIRS = SSD_GROUP_HEADS // 2
NEG_INF = -1e30


def _ssd_common(x_ref, dtx_ref, dtt_ref, anx_ref, anc_ref, b_ref, c_ref, reverse):
    L = SSD_CHUNK
    xv = x_ref[0]
    dt = dtx_ref[0]
    ri = lax.broadcasted_iota(jnp.int32, (L, L), 0)
    ci = lax.broadcasted_iota(jnp.int32, (L, L), 1)
    causal = (ri <= ci) if reverse else (ri >= ci)
    tri = causal.astype(F32)
    a_cs = jnp.dot(tri, dt * anx_ref[...], precision=HIGHEST, preferred_element_type=F32)
    a_row = dtt_ref[0, 0] * anc_ref[0]
    acs_row = lax.dot_general(a_row, tri, NT_DIMS, precision=HIGHEST, preferred_element_type=F32)
    xd = xv * dt
    bmat = b_ref[0].astype(BF16)
    cmat = c_ref[0].astype(BF16)
    gmat = lax.dot_general(cmat, bmat, NT_DIMS, preferred_element_type=F32)
    return xv, dt, causal, tri, a_cs, acs_row, xd, bmat, cmat, gmat


def _ssd_lambda(a_cs, acs_row, causal, h):
    col = a_cs[:, h * SSD_HEAD_DIM:h * SSD_HEAD_DIM + 1]
    row = acs_row[h:h + 1, :]
    return jnp.exp(jnp.where(causal, col - row, NEG_INF))


def _ssd_fwd_call(x, dtx, dtt, anx, anc, bm, cm, reverse):
    b, s, _ = x.shape
    L, N, GW = SSD_CHUNK, SSD_STATE, SSD_GROUP_WIDTH
    nc = s // L
    end = 0 if reverse else L - 1

    def body(x_ref, dtx_ref, dtt_ref, anx_ref, anc_ref, b_ref, c_ref, y_ref, hs_ref, state):
        @pl.when(pl.program_id(2) == 0)
        def _():
            state[...] = jnp.zeros_like(state)

        xv, dt, causal, tri, a_cs, acs_row, xd, bmat, cmat, gmat = _ssd_common(
            x_ref, dtx_ref, dtt_ref, anx_ref, anc_ref, b_ref, c_ref, reverse)
        hin = state[...]
        hs_ref[0, 0, 0] = hin
        y_off = jnp.dot(cmat, hin.astype(BF16), preferred_element_type=F32) * jnp.exp(a_cs)
        a_end = a_cs[end:end + 1, :]
        s_new = lax.dot_general(bmat, (xd * jnp.exp(a_end - a_cs)).astype(BF16), TN_DIMS, preferred_element_type=F32)
        state[...] = jnp.exp(a_end) * hin + s_new
        lane = lax.broadcasted_iota(jnp.int32, (L, LANES), 1)
        for pr in range(SSD_PAIRS):
            sl = slice(pr * LANES, (pr + 1) * LANES)
            xdp = xd[:, sl].astype(BF16)
            w0 = (gmat * _ssd_lambda(a_cs, acs_row, causal, 2 * pr)).astype(BF16)
            w1 = (gmat * _ssd_lambda(a_cs, acs_row, causal, 2 * pr + 1)).astype(BF16)
            y0 = jnp.dot(w0, xdp, preferred_element_type=F32)
            y1 = jnp.dot(w1, xdp, preferred_element_type=F32)
            y_ref[0, :, sl] = jnp.where(lane < SSD_HEAD_DIM, y0, y1) + y_off[:, sl]

    G = SSD_GROUPS
    chunk = (lambda c: nc - 1 - c) if reverse else (lambda c: c)
    seq = lambda bi, gi, c: (bi, chunk(c), gi)
    return pl.pallas_call(
        body, name="ssd_fwd", grid=(b, G, nc),
        in_specs=[pl.BlockSpec((1, L, GW), seq),
                  pl.BlockSpec((1, L, GW), seq),
                  pl.BlockSpec((1, 1, SSD_GROUP_HEADS, L), lambda bi, gi, c: (bi, gi, 0, chunk(c))),
                  pl.BlockSpec((1, GW), lambda bi, gi, c: (0, gi)),
                  pl.BlockSpec((1, SSD_GROUP_HEADS, 1), lambda bi, gi, c: (gi, 0, 0)),
                  pl.BlockSpec((1, L, N), seq),
                  pl.BlockSpec((1, L, N), seq)],
        out_specs=[pl.BlockSpec((1, L, GW), seq),
                   pl.BlockSpec((1, 1, 1, N, GW), lambda bi, gi, c: (bi, gi, chunk(c), 0, 0))],
        out_shape=[jax.ShapeDtypeStruct(x.shape, F32), jax.ShapeDtypeStruct((b, G, nc, N, GW), F32)],
        scratch_shapes=[pltpu.VMEM((N, GW), F32)],
        compiler_params=pltpu.CompilerParams(dimension_semantics=("parallel", "parallel", "arbitrary")),
    )(x, dtx, dtt, anx, anc, bm, cm)


def _ssd_bwd_call(x, dtx, dtt, anx, anc, bm, cm, hs, dy, reverse):
    b, s, _ = x.shape
    L, N, GW = SSD_CHUNK, SSD_STATE, SSD_GROUP_WIDTH
    nc = s // L
    end = 0 if reverse else L - 1

    def body(x_ref, dtx_ref, dtt_ref, anx_ref, anc_ref, b_ref, c_ref, hs_ref, dy_ref,
             dx_ref, ddt_ref, dan_ref, db_ref, dc_ref, dstate):
        @pl.when(pl.program_id(2) == 0)
        def _():
            dstate[...] = jnp.zeros_like(dstate)

        xv, dt, causal, tri, a_cs, acs_row, xd, bmat, cmat, gmat = _ssd_common(
            x_ref, dtx_ref, dtt_ref, anx_ref, anc_ref, b_ref, c_ref, reverse)
        hin = hs_ref[0, 0, 0]
        hinb = hin.astype(BF16)
        dyv = dy_ref[0]
        ds_out = dstate[...]
        dsb = ds_out.astype(BF16)
        eacs = jnp.exp(a_cs)
        a_end = a_cs[end:end + 1, :]
        e_end = jnp.exp(a_end)
        dec = jnp.exp(a_end - a_cs)
        dye = dyv * eacs
        dyeb = dye.astype(BF16)
        xdec = xd * dec
        ch = jnp.dot(cmat, hinb, preferred_element_type=F32)
        bds = jnp.dot(bmat, dsb, preferred_element_type=F32)
        t_state = xdec * bds
        d_aend = jnp.sum(t_state, axis=0, keepdims=True) + e_end * jnp.sum(ds_out * hin, axis=0, keepdims=True)
        dacs = dye * ch - t_state
        dxd_state = bds * dec
        dstate[...] = e_end * ds_out + lax.dot_general(cmat, dyeb, TN_DIMS, preferred_element_type=F32)

        lane = lax.broadcasted_iota(jnp.int32, (L, LANES), 1)
        dg = jnp.zeros((L, L), F32)
        dxd_parts, dacs_parts = [], []
        for pr in range(SSD_PAIRS):
            sl = slice(pr * LANES, (pr + 1) * LANES)
            xdp = xd[:, sl]
            dyp = dyv[:, sl]
            dxd_p = jnp.zeros((L, LANES), F32)
            dacs_p = jnp.zeros((L, LANES), F32)
            for half in range(2):
                mine = (lane < SSD_HEAD_DIM) if half == 0 else (lane >= SSD_HEAD_DIM)
                lam = _ssd_lambda(a_cs, acs_row, causal, 2 * pr + half)
                w = gmat * lam
                xdh = jnp.where(mine, xdp, 0.0).astype(BF16)
                dyh = jnp.where(mine, dyp, 0.0).astype(BF16)
                dw = lax.dot_general(dyh, xdh, NT_DIMS, preferred_element_type=F32)
                dg = dg + dw * lam
                mm = dw * w
                rs = jnp.sum(mm, axis=1, keepdims=True)
                cs = jnp.sum(mm.T, axis=1, keepdims=True)
                dacs_p = dacs_p + jnp.where(mine, (rs - cs) * (1.0 / SSD_HEAD_DIM), 0.0)
                wtdy = lax.dot_general(w.astype(BF16), dyh, TN_DIMS, preferred_element_type=F32)
                dxd_p = dxd_p + wtdy
            dxd_parts.append(dxd_p)
            dacs_parts.append(dacs_p)
        dxd = jnp.concatenate(dxd_parts, axis=1) + dxd_state
        dacs = dacs + jnp.concatenate(dacs_parts, axis=1)
        last = lax.broadcasted_iota(jnp.int32, dacs.shape, 0) == end
        dacs = dacs + jnp.where(last, d_aend, 0.0)
        da = lax.dot_general(tri, dacs, TN_DIMS, precision=HIGHEST, preferred_element_type=F32)
        dgb = dg.astype(BF16)
        dc_ref[0] = (jnp.dot(dgb, bmat, preferred_element_type=F32)
                     + lax.dot_general(dyeb, hinb, NT_DIMS, preferred_element_type=F32))
        db_ref[0] = (lax.dot_general(dgb, cmat, TN_DIMS, preferred_element_type=F32)
                     + lax.dot_general(xdec.astype(BF16), dsb, NT_DIMS, preferred_element_type=F32))
        dx_ref[0] = dxd * dt
        ddt_ref[0] = da * anx_ref[...] + dxd * xv
        dan_ref[0, 0, 0] = jnp.sum(da * dt, axis=0, keepdims=True)

    G = SSD_GROUPS
    chunk = (lambda c: c) if reverse else (lambda c: nc - 1 - c)
    rev = lambda bi, gi, c: (bi, chunk(c), gi)
    return pl.pallas_call(
        body, name="ssd_bwd", grid=(b, G, nc),
        in_specs=[pl.BlockSpec((1, L, GW), rev),
                  pl.BlockSpec((1, L, GW), rev),
                  pl.BlockSpec((1, 1, SSD_GROUP_HEADS, L), lambda bi, gi, c: (bi, gi, 0, chunk(c))),
                  pl.BlockSpec((1, GW), lambda bi, gi, c: (0, gi)),
                  pl.BlockSpec((1, SSD_GROUP_HEADS, 1), lambda bi, gi, c: (gi, 0, 0)),
                  pl.BlockSpec((1, L, N), rev),
                  pl.BlockSpec((1, L, N), rev),
                  pl.BlockSpec((1, 1, 1, N, GW), lambda bi, gi, c: (bi, gi, chunk(c), 0, 0)),
                  pl.BlockSpec((1, L, GW), rev)],
        out_specs=[pl.BlockSpec((1, L, GW), rev),
                   pl.BlockSpec((1, L, GW), rev),
                   pl.BlockSpec((1, 1, 1, 1, GW), lambda bi, gi, c: (bi, gi, chunk(c), 0, 0)),
                   pl.BlockSpec((1, L, N), rev),
                   pl.BlockSpec((1, L, N), rev)],
        out_shape=[jax.ShapeDtypeStruct(x.shape, F32), jax.ShapeDtypeStruct(x.shape, F32),
                   jax.ShapeDtypeStruct((b, G, nc, 1, GW), F32),
                   jax.ShapeDtypeStruct(bm.shape, F32), jax.ShapeDtypeStruct(cm.shape, F32)],
        scratch_shapes=[pltpu.VMEM((N, GW), F32)],
        compiler_params=pltpu.CompilerParams(dimension_semantics=("parallel", "parallel", "arbitrary")),
    )(x, dtx, dtt, anx, anc, bm, cm, hs, dy)


@functools.partial(jax.custom_vjp, nondiff_argnums=(7,))
def _ssd_scan(x, dtx, dtt, anx, anc, bm, cm, reverse):
    return _ssd_fwd_call(x, dtx, dtt, anx, anc, bm, cm, reverse)[0]


def _ssd_scan_fwd(x, dtx, dtt, anx, anc, bm, cm, reverse):
    y, hs = _ssd_fwd_call(x, dtx, dtt, anx, anc, bm, cm, reverse)
    return y, (x, dtx, dtt, anx, anc, bm, cm, hs)


def _ssd_scan_bwd(reverse, res, dy):
    x, dtx, dtt, anx, anc, bm, cm, hs = res
    dx, ddtx, dan, db, dc = _ssd_bwd_call(x, dtx, dtt, anx, anc, bm, cm, hs, dy, reverse)
    b, g, nc, _, gw = dan.shape
    danx = jnp.sum(dan, axis=(0, 2, 3)).reshape(1, g * gw)
    return dx, ddtx, jnp.zeros_like(dtt), danx, jnp.zeros_like(anc), db, dc


_ssd_scan.defvjp(_ssd_scan_fwd, _ssd_scan_bwd)


def ssd_chunked(xs, dt, a_neg, bm, cm, reverse):
    b, s, _ = xs.shape
    dtx = jnp.repeat(dt, SSD_HEAD_DIM, axis=-1)
    dtt = jnp.transpose(dt, (0, 2, 1)).reshape(b, SSD_GROUPS, SSD_GROUP_HEADS, s)
    anx = jnp.repeat(a_neg, SSD_HEAD_DIM)[None, :]
    anc = a_neg.reshape(SSD_GROUPS, SSD_GROUP_HEADS, 1)
    return _ssd_scan(xs, dtx, dtt, anx, anc, bm, cm, reverse)


def _loss_call(y, t):
    rows, cols = y.shape
    tr = _row_tile(rows, cols * 4)

    def body(y_ref, t_ref, loss_ref, diff_ref):
        @pl.when(pl.program_id(0) == 0)
        def _():
            loss_ref[...] = jnp.zeros_like(loss_ref)

        d = y_ref[...] - t_ref[...]
        diff_ref[...] = d * (1.0 / cols)
        part = jnp.sum(jnp.sum(d * d, axis=1, keepdims=True), axis=0, keepdims=True)
        loss_ref[...] += part * (0.5 / cols)

    return pl.pallas_call(
        body, name="loss_head", grid=(rows // tr,),
        in_specs=[pl.BlockSpec((tr, cols), lambda i: (i, 0)), pl.BlockSpec((tr, cols), lambda i: (i, 0))],
        out_specs=[pl.BlockSpec((1, 1), lambda i: (0, 0)), pl.BlockSpec((tr, cols), lambda i: (i, 0))],
        out_shape=[jax.ShapeDtypeStruct((1, 1), F32), jax.ShapeDtypeStruct((rows, cols), F32)],
        compiler_params=pltpu.CompilerParams(dimension_semantics=("arbitrary",)),
    )(y, t)


@jax.custom_vjp
def loss_head(y, t):
    return _loss_call(y, t)[0][0, 0]


def _loss_head_fwd(y, t):
    loss, diff = _loss_call(y, t)
    return loss[0, 0], diff


def _loss_head_bwd(diff, g):
    return g * diff, jnp.zeros_like(diff)


loss_head.defvjp(_loss_head_fwd, _loss_head_bwd)


def _axial_rope_tables(seq_len, rot_dim):
    rows = seq_len // GRID_W
    row_idx = jnp.repeat(jnp.arange(rows), GRID_W).astype(F32)
    col_idx = jnp.tile(jnp.arange(GRID_W), rows).astype(F32)
    axis_dim = rot_dim // 2
    inv_freq = jnp.power(ROPE_THETA, -jnp.arange(0, axis_dim, 2, dtype=F32) / axis_dim)
    ang_r = row_idx[:, None] * inv_freq[None, :]
    ang_c = col_idx[:, None] * inv_freq[None, :]
    return jnp.cos(ang_r), jnp.sin(ang_r), jnp.cos(ang_c), jnp.sin(ang_c)


def _rotate(x, cos, sin):
    x1, x2 = jnp.split(x, 2, axis=-1)
    cos = cos[:, None, :]
    sin = sin[:, None, :]
    return jnp.concatenate([x1 * cos - x2 * sin, x1 * sin + x2 * cos], axis=-1)


def _apply_axial_rope(x, tables):
    cos_r, sin_r, cos_c, sin_c = tables
    x_row, x_col = jnp.split(x, 2, axis=-1)
    return jnp.concatenate([_rotate(x_row, cos_r, sin_r), _rotate(x_col, cos_c, sin_c)], axis=-1)


def _heads_first(t):
    return jnp.transpose(t, (0, 2, 1, 3))


def _gqa_group(q, k, v, q_norm_g, k_norm_g, rope, b, s):
    q = rms_norm(q, jnp.tile(q_norm_g, GQA_HEADS)[None, :], GQA_HEADS).reshape(b, s, GQA_HEADS, GQA_HEAD_DIM)
    k = rms_norm(k, jnp.tile(k_norm_g, GQA_KV_HEADS)[None, :], GQA_KV_HEADS).reshape(b, s, GQA_KV_HEADS, GQA_HEAD_DIM)
    v = v.reshape(b, s, GQA_KV_HEADS, GQA_HEAD_DIM)
    q = _apply_axial_rope(q, rope)
    k = _apply_axial_rope(k, rope)
    o = attention(_heads_first(q), _heads_first(k), _heads_first(v), GQA_HEAD_DIM ** -0.5)
    return _heads_first(o).reshape(b * s, GQA_WIDTH)


def _mla_group(c_q, c_kv, k_pe, q_norm_g, w_uq, kv_norm_g, w_ukv, rope, b, s):
    q = linear(rms_norm(c_q, q_norm_g[None, :], 1), w_uq).reshape(b, s, MLA_HEADS, MLA_NOPE_DIM + MLA_ROPE_DIM)
    q_nope, q_pe = q[..., :MLA_NOPE_DIM], q[..., MLA_NOPE_DIM:]
    kv = linear(rms_norm(c_kv, kv_norm_g[None, :], 1), w_ukv).reshape(b, s, MLA_HEADS, MLA_NOPE_DIM + MLA_V_DIM)
    k_nope, v = kv[..., :MLA_NOPE_DIM], kv[..., MLA_NOPE_DIM:]
    q_pe = _apply_axial_rope(q_pe, rope)
    k_pe = _apply_axial_rope(k_pe.reshape(b, s, 1, MLA_ROPE_DIM), rope)
    q = jnp.concatenate([q_nope, q_pe], axis=-1)
    k = jnp.concatenate([k_nope, jnp.broadcast_to(k_pe, (b, s, MLA_HEADS, MLA_ROPE_DIM))], axis=-1)
    o = attention(_heads_first(q), _heads_first(k), _heads_first(v), (MLA_NOPE_DIM + MLA_ROPE_DIM) ** -0.5)
    return _heads_first(o).reshape(b * s, MLA_HEADS * MLA_V_DIM)


def _ssd_group(z, xbc, dt_raw, conv_w, conv_b, dt_bias, a_log, d_skip, norm_g, b, s):
    wb = jnp.concatenate([conv_w, conv_b[None, :], jnp.zeros((CONV_PACK_ROWS - SSD_CONV - 1, SSD_CONV_DIM), F32)], axis=0)
    xbc = conv_silu(xbc.reshape(b, s, SSD_CONV_DIM), wb)
    xs = xbc[..., :SSD_INNER]
    bm = xbc[..., SSD_INNER:SSD_INNER + SSD_GROUPS * SSD_STATE]
    cm = xbc[..., SSD_INNER + SSD_GROUPS * SSD_STATE:]
    dt = jax.nn.softplus(dt_raw.reshape(b, s, 2, SSD_HEADS) + dt_bias)
    a_neg = -jnp.exp(a_log)
    y_fwd = ssd_chunked(xs, dt[:, :, 0], a_neg[0], bm, cm, False)
    y_bwd = ssd_chunked(xs, dt[:, :, 1], a_neg[1], bm, cm, True)
    y = y_fwd + y_bwd + xs * jnp.repeat(d_skip, SSD_HEAD_DIM)
    y = y.reshape(b * s, SSD_INNER) * jax.nn.silu(z)
    return rms_norm(y, norm_g[None, :], SSD_GROUPS)


MIXER_WEIGHTS = ('q_norm_g', 'k_norm_g', 'mla_q_norm_g', 'w_uq', 'mla_kv_norm_g', 'w_ukv', 'conv_w', 'conv_b',
                 'dt_bias', 'a_log', 'd_skip', 'ssd_norm_g')


def _mixer(proj, w, rope_a, rope_b, b, s):
    idx = np.cumsum(IN_SPLITS).tolist()
    q_a, k_a, v_a, cq_b, ckv_b, kpe_b, z_c, xbc_c, dt_c = [_in_cols(proj, lo, hi)
                                                           for lo, hi in zip([0] + idx[:-1], idx)]
    o_a = _gqa_group(q_a, k_a, v_a, w["q_norm_g"], w["k_norm_g"], rope_a, b, s)
    o_b = _mla_group(cq_b, ckv_b, kpe_b, w["mla_q_norm_g"], w["w_uq"], w["mla_kv_norm_g"], w["w_ukv"], rope_b, b, s)
    o_c = _ssd_group(z_c, xbc_c, dt_c, w["conv_w"], w["conv_b"], w["dt_bias"], w["a_log"], w["d_skip"],
                     w["ssd_norm_g"], b, s)
    return jnp.concatenate([o_a, o_b, o_c], axis=-1)


def _seq_tile(s, row_bytes):
    return _row_tile(s, row_bytes)


def _normmod_fwd(x, g, scale, shift):
    b, s, d = x.shape
    tr = _seq_tile(s, d * 4)

    def body(x_ref, g_ref, sc_ref, sh_ref, h_ref):
        xv = x_ref[0]
        r = lax.rsqrt(jnp.mean(xv * xv, axis=-1, keepdims=True) + EPS)
        h_ref[0] = (xv * r * g_ref[...] * (1.0 + sc_ref[0]) + sh_ref[0]).astype(BF16)

    act = pl.BlockSpec((1, tr, d), lambda bi, i: (bi, i, 0))
    vec = pl.BlockSpec((1, 1, d), lambda bi, i: (bi, 0, 0))
    return pl.pallas_call(
        body, name="normmod_fwd", grid=(b, s // tr),
        in_specs=[act, pl.BlockSpec((1, d), lambda bi, i: (0, 0)), vec, vec], out_specs=act,
        out_shape=jax.ShapeDtypeStruct((b, s, d), BF16),
        compiler_params=pltpu.CompilerParams(dimension_semantics=("parallel", "parallel")),
    )(x, g, scale, shift)


def _normmod_bwd(x, g, scale, dh, resid):
    b, s, d = x.shape
    tr = _seq_tile(s, d * 4)

    def body(x_ref, g_ref, sc_ref, dh_ref, res_ref, dx_ref, dg_ref, dsc_ref, dsh_ref):
        bi, i = pl.program_id(0), pl.program_id(1)

        @pl.when((bi == 0) & (i == 0))
        def _():
            dg_ref[...] = jnp.zeros_like(dg_ref)

        @pl.when(i == 0)
        def _():
            dsc_ref[...] = jnp.zeros_like(dsc_ref)
            dsh_ref[...] = jnp.zeros_like(dsh_ref)

        xv = x_ref[0]
        dhv = dh_ref[0]
        gv = g_ref[...]
        r = lax.rsqrt(jnp.mean(xv * xv, axis=-1, keepdims=True) + EPS)
        xhat = xv * r
        dsh_ref[0] += jnp.sum(dhv, axis=0, keepdims=True)
        dsc_ref[0] += jnp.sum(dhv * (xhat * gv), axis=0, keepdims=True)
        dn = dhv * (1.0 + sc_ref[0])
        dg_ref[...] += jnp.sum(dn * xhat, axis=0, keepdims=True)
        dxhat = dn * gv
        dx_ref[0] = r * (dxhat - xhat * jnp.mean(dxhat * xhat, axis=-1, keepdims=True)) + res_ref[0]

    act = pl.BlockSpec((1, tr, d), lambda bi, i: (bi, i, 0))
    vec = pl.BlockSpec((1, 1, d), lambda bi, i: (bi, 0, 0))
    gain = pl.BlockSpec((1, d), lambda bi, i: (0, 0))
    return pl.pallas_call(
        body, name="normmod_bwd", grid=(b, s // tr),
        in_specs=[act, gain, vec, act, act], out_specs=[act, gain, vec, vec],
        out_shape=[jax.ShapeDtypeStruct((b, s, d), F32), jax.ShapeDtypeStruct((1, d), F32),
                   jax.ShapeDtypeStruct((b, 1, d), F32), jax.ShapeDtypeStruct((b, 1, d), F32)],
        compiler_params=pltpu.CompilerParams(dimension_semantics=("arbitrary", "arbitrary")),
    )(x, g, scale, dh, resid)


def _gated_add(x, gate, t):
    b, s, d = x.shape
    tr = _seq_tile(s, d * 4)

    def body(x_ref, g_ref, t_ref, o_ref):
        o_ref[0] = x_ref[0] + g_ref[0] * t_ref[0]

    act = pl.BlockSpec((1, tr, d), lambda bi, i: (bi, i, 0))
    vec = pl.BlockSpec((1, 1, d), lambda bi, i: (bi, 0, 0))
    return pl.pallas_call(
        body, name="gated_add", grid=(b, s // tr), in_specs=[act, vec, act], out_specs=act,
        out_shape=jax.ShapeDtypeStruct((b, s, d), F32),
        compiler_params=pltpu.CompilerParams(dimension_semantics=("parallel", "parallel")),
    )(x, gate, t)


def _gated_bwd(dy, gate, t):
    b, s, d = dy.shape
    tr = _seq_tile(s, d * 4)

    def body(dy_ref, g_ref, t_ref, dt_ref, dgate_ref):
        @pl.when(pl.program_id(1) == 0)
        def _():
            dgate_ref[...] = jnp.zeros_like(dgate_ref)

        dyv = dy_ref[0]
        dt_ref[0] = (g_ref[0] * dyv).astype(BF16)
        dgate_ref[0] += jnp.sum(dyv * t_ref[0], axis=0, keepdims=True)

    act = pl.BlockSpec((1, tr, d), lambda bi, i: (bi, i, 0))
    vec = pl.BlockSpec((1, 1, d), lambda bi, i: (bi, 0, 0))
    return pl.pallas_call(
        body, name="gated_bwd", grid=(b, s // tr), in_specs=[act, vec, act], out_specs=[act, vec],
        out_shape=[jax.ShapeDtypeStruct((b, s, d), BF16), jax.ShapeDtypeStruct((b, 1, d), F32)],
        compiler_params=pltpu.CompilerParams(dimension_semantics=("parallel", "arbitrary")),
    )(dy, gate, t)


def _swiglu_fwd(gu, plan=None):
    rows, f2 = gu.shape
    f = f2 // 2
    tr = _row_tile(rows, f2 * 4, SWIGLU_TILE_BYTES)
    steps = rows // tr
    p_in, p_out, p_shapes, p_scratch, p_args = _plan_specs(plan)

    def body(gu_ref, *rest):
        a_ref = rest[len(p_in)]
        i = pl.program_id(0)

        def compute():
            gt = gu_ref[:, :f]
            a_ref[...] = (gt * jax.nn.sigmoid(gt) * gu_ref[:, f:]).astype(BF16)

        _ride(plan, rest[:len(p_in)] + rest[len(p_in) + 1:], i == 0, i == steps - 1, compute)

    outs = pl.pallas_call(
        body, name="swiglu_fwd", grid=(steps,),
        in_specs=[pl.BlockSpec((tr, f2), lambda i: (i, 0))] + p_in,
        out_specs=[pl.BlockSpec((tr, f), lambda i: (i, 0))] + p_out,
        out_shape=[jax.ShapeDtypeStruct((rows, f), BF16)] + p_shapes, scratch_shapes=p_scratch,
        compiler_params=pltpu.CompilerParams(dimension_semantics=("arbitrary" if plan is not None else "parallel",)),
    )(gu, *p_args)
    return outs[0] if plan is None else (outs[0], list(outs[1:]))


def _swiglu_bwd(gu, dact):
    rows, f2 = gu.shape
    f = f2 // 2
    tr = _row_tile(rows, f2 * 4, SWIGLU_TILE_BYTES)

    def body(gu_ref, da_ref, dgu_ref):
        gt = gu_ref[:, :f]
        up = gu_ref[:, f:]
        da = da_ref[...]
        sg = jax.nn.sigmoid(gt)
        dgu_ref[:, :f] = (da * up * (sg * (1.0 + gt * (1.0 - sg)))).astype(BF16)
        dgu_ref[:, f:] = (da * gt * sg).astype(BF16)

    return pl.pallas_call(
        body, name="swiglu_bwd", grid=(rows // tr,),
        in_specs=[pl.BlockSpec((tr, f2), lambda i: (i, 0)), pl.BlockSpec((tr, f), lambda i: (i, 0))],
        out_specs=pl.BlockSpec((tr, f2), lambda i: (i, 0)),
        out_shape=jax.ShapeDtypeStruct((rows, f2), BF16),
        compiler_params=pltpu.CompilerParams(dimension_semantics=("parallel",)),
    )(gu, dact)


class _Gathered:
    def __init__(self, shards):
        self.shards, self.full = shards, {}

    def plan(self, keys):
        return _gather_plan([self.shards[k] for k in keys])

    def store(self, keys, outs):
        for key, out in zip(keys, outs):
            name = key[0]
            g = out.reshape((N_CHIPS,) + self.shards[key].shape)
            if name in CHIP_BLOCKED:
                full = g
            elif name in COL_SHARDED:
                full = _cols_full(g).astype(F32)
            else:
                full = g.reshape(g.shape[0] * g.shape[1], g.shape[2])
            self.full[key] = full

    def carry(self, keys, fn):
        if not keys:
            return fn(None)
        res, outs = fn(self.plan(keys))
        self.store(keys, outs)
        return res


def _gather_schedule(depth):
    every = [(n, l) for l in range(depth) for n in ('w_uq', 'w_ukv')]
    sched = {'first': [('w_in', 0), ('w_gate_up', 0)] + every}
    for l in range(depth):
        sched[('w_in_fwd', l)] = [('w_out', l)] + ([('w_down', 0)] if l == 0 else [])
        if l + 1 < depth:
            sched[('w_gate_up_fwd', l)] = [('w_gate_up', l + 1)]
            sched[('swiglu_fwd', l)] = [('w_in', l + 1)]
            sched[('w_down_fwd', l)] = [('w_down', l + 1)]
    return sched


GATE_UP_PIECES = 2


class _Reducer:
    def __init__(self):
        self.parts, self.recv, self.result = {}, {}, {}

    def add(self, key, grad):
        name = key[0]
        if name in CHIP_BLOCKED:
            blocks = grad
        elif name in COL_SHARDED:
            blocks = _cols_split(grad)
        else:
            blocks = grad.reshape(N_CHIPS, grad.shape[0] // N_CHIPS, grad.shape[1])
        self.parts[key] = _rs_parts(blocks)
        self.recv[key] = []

    def pieces(self, key):
        rows = self.parts[key].shape[1]
        n = GATE_UP_PIECES if key[0] == 'w_gate_up' else 1
        return [(key, i * (rows // n), rows // n) for i in range(n)]

    def plan(self, jobs):
        return _chip_exchange_plan([(self.parts[key], row0, rows) for key, row0, rows in jobs])

    def store(self, jobs, outs):
        for (key, row0, rows), out in zip(jobs, outs):
            self.recv[key].append((row0, out))
            if len(self.recv[key]) == len(self.pieces(key)):
                self.result[key] = _rs_result(self.parts[key], sorted(self.recv[key], key=lambda t: t[0]))

    def carry(self, keys, fn, piece=None):
        jobs = [j for key in keys for j in self.pieces(key)]
        if piece is not None:
            jobs = [j for key in keys for j in self.pieces(key)[piece:piece + 1]]
        if not jobs:
            return fn(None)
        res, outs = fn(self.plan(jobs))
        self.store(jobs, outs)
        return res

    def flush(self):
        jobs = [j for key in self.parts for j in self.pieces(key)
                if key not in self.result and j[1] not in [r for r, _ in self.recv[key]]]
        if jobs:
            self.store(jobs, _run_plan(self.plan(jobs), "rs_chip_exchange"))


def _layer_fwd(x, mod, w, gathered, l, sched, rope_a, rope_b):
    b, s, d = x.shape
    m = b * s
    shift1, scale1, gate1, shift2, scale2, gate2 = [t[:, None, :] for t in jnp.split(mod, 6, axis=-1)]
    g1, g2 = w["norm1_g"][None, :], w["norm2_g"][None, :]
    full = lambda n: gathered.full[(n, l)]
    h1 = _normmod_fwd(x, g1, scale1, shift1).reshape(m, d)
    proj = gathered.carry(sched.get(('w_in_fwd', l)), lambda p: _matmul(h1, full('w_in'), name="w_in_fwd", plan=p, chips='b'))
    mixer_w = {n: (full(n) if n in COL_SHARDED else w[n]) for n in MIXER_WEIGHTS}
    o, mixer_vjp = jax.vjp(lambda p, mw: _mixer(p, mw, rope_a, rope_b, b, s), proj, mixer_w)
    o = o.astype(BF16)
    mix = _matmul(o, full('w_out'), name="w_out_fwd").reshape(b, s, d)
    x_mid = _gated_add(x, gate1, mix)
    h2 = _normmod_fwd(x_mid, g2, scale2, shift2).reshape(m, d)
    gu = gathered.carry(sched.get(('w_gate_up_fwd', l)),
                        lambda p: _matmul(h2, full('w_gate_up'), name="w_gate_up_fwd", plan=p, chips='b'))
    act = gathered.carry(sched.get(('swiglu_fwd', l)), lambda p: _swiglu_fwd(gu, plan=p))
    ffn = gathered.carry(sched.get(('w_down_fwd', l)), lambda p: _matmul(act, full('w_down'), name="w_down_fwd", plan=p))
    ffn = ffn.reshape(b, s, d)
    x_out = _gated_add(x_mid, gate2, ffn)
    res = (x, x_mid, h1, h2, o, mix, gu, act, ffn, mixer_vjp, scale1, gate1, scale2, gate2, g1, g2)
    return x_out, res


def _layer_bwd(res, gathered, reducer, l, depth, dx_out):
    x, x_mid, h1, h2, o, mix, gu, act, ffn, mixer_vjp, scale1, gate1, scale2, gate2, g1, g2 = res
    b, s, d = x.shape
    m = b * s
    full = lambda n: gathered.full[(n, l)]
    above = l + 1 < depth
    dffn, dgate2 = _gated_bwd(dx_out, gate2, ffn)
    dffn = dffn.reshape(m, d)
    dact = reducer.carry([('w_out', l + 1), ('w_uq', l + 1), ('w_ukv', l + 1)] if above else [],
                         lambda p: _matmul(dffn, full('w_down'), tb=True, name="w_down_dx", plan=p))
    dw = reducer.carry([('w_in', l + 1)] if above else [],
                       lambda p: _matmul(act, dffn, ta=True, name="w_down_dw", plan=p))
    reducer.add(('w_down', l), dw)
    dgu = _swiglu_bwd(gu, dact)
    dh2 = reducer.carry([('w_down', l)], lambda p: _matmul(dgu, full('w_gate_up'), tb=True, name="w_gate_up_dx", plan=p,
                                                           chips='b'))
    dh2 = dh2.reshape(b, s, d)
    reducer.add(('w_gate_up', l), _matmul(h2, dgu, ta=True, name="w_gate_up_dw", chips='out'))
    dx_mid, dg2, dscale2, dshift2 = _normmod_bwd(x_mid, g2, scale2, dh2, dx_out)
    dmix, dgate1 = _gated_bwd(dx_mid, gate1, mix)
    dmix = dmix.reshape(m, d)
    do = _matmul(dmix, full('w_out'), tb=True, name="w_out_dx")
    reducer.add(('w_out', l), _matmul(o, dmix, ta=True, name="w_out_dw"))
    dproj, grads = mixer_vjp(do)
    grads = dict(grads)
    reducer.add(('w_uq', l), grads.pop('w_uq'))
    reducer.add(('w_ukv', l), grads.pop('w_ukv'))
    dproj = dproj.astype(BF16)
    dh1 = reducer.carry([('w_gate_up', l)], lambda p: _matmul(dproj, full('w_in'), tb=True, name="w_in_dx", plan=p, chips='b'),
                        piece=0)
    dh1 = dh1.reshape(b, s, d)
    dw = reducer.carry([('w_gate_up', l)], lambda p: _matmul(h1, dproj, ta=True, name="w_in_dw", plan=p, chips='out'),
                       piece=1)
    reducer.add(('w_in', l), dw)
    dx, dg1, dscale1, dshift1 = _normmod_bwd(x, g1, scale1, dh1, dx_mid)
    grads["norm1_g"], grads["norm2_g"] = dg1[0], dg2[0]
    dmod = jnp.concatenate([dshift1, dscale1, dgate1, dshift2, dscale2, dgate2], axis=-1)[:, 0, :]
    return dx, dmod, grads


def _tail_loss(x2, final_norm_g, target2):
    return loss_head(rms_norm(x2, final_norm_g[None, :], 1), target2)


def _forward_backward(x, mod, small, gathered, reducer, final_norm_g, target):
    b, s, d = x.shape
    depth = len(small)
    rope_a = _axial_rope_tables(s, GQA_HEAD_DIM)
    rope_b = _axial_rope_tables(s, MLA_ROPE_DIM)
    sched = _gather_schedule(depth)
    first = sched['first']
    gathered.store(first, _run_plan(gathered.plan(first), "all_gather_chips"))
    saved = []
    for l in range(depth):
        x, res = _layer_fwd(x, mod[l], small[l], gathered, l, sched, rope_a, rope_b)
        saved.append(res)
    loss, (dx2, dfinal) = jax.value_and_grad(_tail_loss, argnums=(0, 1))(
        x.reshape(b * s, d), final_norm_g, target.reshape(b * s, d))
    dx = dx2.reshape(b, s, d)
    dmods, gsmall = [None] * depth, [None] * depth
    for l in reversed(range(depth)):
        dx, dmods[l], gsmall[l] = _layer_bwd(saved[l], gathered, reducer, l, depth, dx)
    return loss, dx, jnp.stack(dmods), gsmall, dfinal


ANY = pl.BlockSpec(memory_space=pl.ANY)


def _flip_if(v, bit):
    return 1 - v if bit else v


def _all_gather_devices(x):
    def body(x_ref, out_ref, send_sems, recv_sems):
        mx, my, mc = lax.axis_index("x"), lax.axis_index("y"), lax.axis_index("c")
        me = 4 * mx + 2 * my + mc
        sends = []
        for k in range(1, N_DEV):
            peer = (_flip_if(mx, k & 4), _flip_if(my, k & 2), _flip_if(mc, k & 1))
            cp = pltpu.make_async_remote_copy(src_ref=x_ref, dst_ref=out_ref.at[me], send_sem=send_sems.at[k - 1],
                                              recv_sem=recv_sems.at[k - 1], device_id=peer, device_id_type=MESH)
            cp.start()
            sends.append(cp)
        for k in range(1, N_DEV):
            peer = (_flip_if(mx, k & 4), _flip_if(my, k & 2), _flip_if(mc, k & 1))
            src = 4 * peer[0] + 2 * peer[1] + peer[2]
            pltpu.make_async_remote_copy(src_ref=x_ref, dst_ref=out_ref.at[src], send_sem=send_sems.at[k - 1],
                                         recv_sem=recv_sems.at[k - 1], device_id=peer, device_id_type=MESH).wait_recv()
        for cp in sends:
            cp.wait_send()

    out = pl.pallas_call(
        body, name="all_gather_devices", in_specs=[ANY], out_specs=ANY,
        out_shape=jax.ShapeDtypeStruct((N_DEV,) + x.shape, x.dtype),
        scratch_shapes=[pltpu.SemaphoreType.DMA((N_DEV - 1,)), pltpu.SemaphoreType.DMA((N_DEV - 1,))],
    )(x)
    me = 4 * lax.axis_index("x") + 2 * lax.axis_index("y") + lax.axis_index("c")
    return lax.dynamic_update_index_in_dim(out, x, me, 0)


class _Plan:
    def __init__(self, inputs, out_shapes, sem_counts, start, finish):
        self.inputs, self.out_shapes, self.sem_counts = list(inputs), list(out_shapes), list(sem_counts)
        self.start, self.finish = start, finish

    def specs(self):
        return ([ANY] * len(self.inputs), [ANY] * len(self.out_shapes),
                [pltpu.SemaphoreType.DMA((c,)) for c in self.sem_counts])

    def split(self, refs):
        a, b = len(self.inputs), len(self.inputs) + len(self.out_shapes)
        return refs[:a], refs[a:b], refs[b:]


def _run_plan(plan, name):
    def body(*refs):
        ins, outs, sems = plan.split(refs)
        plan.start(ins, outs, sems)
        plan.finish(ins, outs, sems)

    in_specs, out_specs, scratch = plan.specs()
    return pl.pallas_call(body, name=name, in_specs=in_specs, out_specs=out_specs, out_shape=plan.out_shapes,
                          scratch_shapes=scratch)(*plan.inputs)


def _gather_plan(shards):
    n = len(shards)
    halves = [t.reshape(2, t.shape[0] // 2, t.shape[1]) for t in shards]
    count = (N_CHIPS - 1) * n

    def copies(kind, ins, outs, sems):
        ici_send, ici_recv, d2d_send, d2d_recv, own_send, own_recv = sems
        mx, my, mc = lax.axis_index("x"), lax.axis_index("y"), lax.axis_index("c")
        me = 2 * mx + my
        sibling = (mx, my, 1 - mc)
        if kind == 'own':
            return [pltpu.make_async_remote_copy(src_ref=ins[i], dst_ref=outs[i].at[me], send_sem=own_send.at[i],
                                                 recv_sem=own_recv.at[i], device_id=sibling, device_id_type=MESH)
                    for i in range(n)]
        cps = []
        for k in range(1, N_CHIPS):
            peer = (_flip_if(mx, k & 2), _flip_if(my, k & 1), mc)
            src = 2 * peer[0] + peer[1]
            for i in range(n):
                j = (k - 1) * n + i
                if kind in ('ici', 'landed'):
                    dst = outs[i].at[me, mc] if kind == 'ici' else outs[i].at[src, mc]
                    cps.append(pltpu.make_async_remote_copy(
                        src_ref=ins[i].at[mc], dst_ref=dst, send_sem=ici_send.at[j], recv_sem=ici_recv.at[j],
                        device_id=peer, device_id_type=MESH))
                else:
                    half = outs[i].at[src, mc] if kind == 'fwd' else outs[i].at[src, 1 - mc]
                    cps.append(pltpu.make_async_remote_copy(
                        src_ref=half, dst_ref=half, send_sem=d2d_send.at[j], recv_sem=d2d_recv.at[j],
                        device_id=sibling, device_id_type=MESH))
        return cps

    def start(ins, outs, sems):
        for cp in copies('own', ins, outs, sems) + copies('ici', ins, outs, sems):
            cp.start()

    def finish(ins, outs, sems):
        fwd = copies('fwd', ins, outs, sems)
        for arrived, onward in zip(copies('landed', ins, outs, sems), fwd):
            arrived.wait_recv()
            onward.start()
        own = copies('own', ins, outs, sems)
        for cp in copies('fwd_in', ins, outs, sems) + own:
            cp.wait_recv()
        for cp in own + copies('ici', ins, outs, sems) + fwd:
            cp.wait_send()

    out_shapes = [jax.ShapeDtypeStruct((N_CHIPS,) + t.shape, t.dtype) for t in halves]
    return _Plan(halves, out_shapes, [count] * 4 + [n] * 2, start, finish)


def _sibling_exchange(blocks, name):
    n = len(blocks)

    def body(*refs):
        ins, outs = refs[:n], refs[n:2 * n]
        send_sems, recv_sems = refs[2 * n:]
        mx, my, mc = lax.axis_index("x"), lax.axis_index("y"), lax.axis_index("c")
        cps = []
        for i in range(n):
            cp = pltpu.make_async_remote_copy(src_ref=ins[i], dst_ref=outs[i], send_sem=send_sems.at[i],
                                              recv_sem=recv_sems.at[i], device_id=(mx, my, 1 - mc),
                                              device_id_type=MESH)
            cp.start()
            cps.append(cp)
        for cp in cps:
            cp.wait()

    return pl.pallas_call(
        body, name=name, in_specs=[ANY] * n, out_specs=[ANY] * n,
        out_shape=[jax.ShapeDtypeStruct(t.shape, t.dtype) for t in blocks],
        scratch_shapes=[pltpu.SemaphoreType.DMA((n,)), pltpu.SemaphoreType.DMA((n,))],
    )(*blocks)


def _add_halves(own, recv):
    nb, r, c = own.shape
    tr = _row_tile(r, c * 4)

    def body(g_ref, r_ref, o_ref):
        o_ref[...] = (g_ref[...] + r_ref[...].astype(F32)).astype(BF16)

    spec = pl.BlockSpec((1, tr, c), lambda k, i: (k, i, 0))
    return pl.pallas_call(
        body, name="rs_add_halves", grid=(nb, r // tr), in_specs=[spec, spec], out_specs=spec,
        out_shape=jax.ShapeDtypeStruct((nb, r, c), BF16),
        compiler_params=pltpu.CompilerParams(dimension_semantics=("parallel", "parallel")),
    )(own, recv)


def _chip_exchange_plan(jobs):
    n = len(jobs)
    count = (N_CHIPS - 1) * n

    def copies(ins, outs, sems):
        send_sems, recv_sems = sems
        mx, my, mc = lax.axis_index("x"), lax.axis_index("y"), lax.axis_index("c")
        cps = []
        for k in range(1, N_CHIPS):
            peer = (_flip_if(mx, k & 2), _flip_if(my, k & 1), mc)
            dst_chip = 2 * peer[0] + peer[1]
            for i, (_, row0, rows) in enumerate(jobs):
                j = (k - 1) * n + i
                cps.append(pltpu.make_async_remote_copy(
                    src_ref=ins[i].at[dst_chip, pl.ds(row0, rows)], dst_ref=outs[i].at[k - 1],
                    send_sem=send_sems.at[j], recv_sem=recv_sems.at[j], device_id=peer, device_id_type=MESH))
        return cps

    def start(ins, outs, sems):
        for cp in copies(ins, outs, sems):
            cp.start()

    def finish(ins, outs, sems):
        for cp in copies(ins, outs, sems):
            cp.wait()

    out_shapes = [jax.ShapeDtypeStruct((N_CHIPS - 1, rows, p.shape[2]), p.dtype) for p, _, rows in jobs]
    return _Plan([p for p, _, _ in jobs], out_shapes, [count, count], start, finish)


def _sum_chips(parts, recv, chip, row0):
    _, rows, c = recv.shape
    tr = _row_tile(rows, c * 4)
    assert row0 % tr == 0

    def body(chip_ref, p_ref, r_ref, o_ref):
        acc = p_ref[0].astype(F32)
        for k in range(N_CHIPS - 1):
            acc = acc + r_ref[k].astype(F32)
        o_ref[...] = acc

    return pl.pallas_call(
        body, name="rs_sum_chips",
        grid_spec=pltpu.PrefetchScalarGridSpec(
            num_scalar_prefetch=1, grid=(rows // tr,),
            in_specs=[pl.BlockSpec((1, tr, c), lambda i, chip_ref: (chip_ref[0], i + row0 // tr, 0)),
                      pl.BlockSpec((N_CHIPS - 1, tr, c), lambda i, chip_ref: (0, i, 0))],
            out_specs=pl.BlockSpec((tr, c), lambda i, chip_ref: (i, 0))),
        out_shape=jax.ShapeDtypeStruct((rows, c), F32),
        compiler_params=pltpu.CompilerParams(dimension_semantics=("parallel",)),
    )(chip, parts, recv)


def _sum_leading(t, name):
    nb, r, c = t.shape
    tr = _row_tile(r, c * 4 * nb)

    def body(t_ref, o_ref):
        acc = t_ref[0]
        for k in range(1, nb):
            acc = acc + t_ref[k]
        o_ref[...] = acc

    return pl.pallas_call(
        body, name=name, grid=(r // tr,),
        in_specs=[pl.BlockSpec((nb, tr, c), lambda i: (0, i, 0))],
        out_specs=pl.BlockSpec((tr, c), lambda i: (i, 0)),
        out_shape=jax.ShapeDtypeStruct((r, c), F32),
        compiler_params=pltpu.CompilerParams(dimension_semantics=("parallel",)),
    )(t)


def _rs_parts(grad):
    mc = lax.axis_index("c")
    split = grad.reshape(grad.shape[0], 2, grad.shape[1] // 2, grad.shape[2])
    own = lax.dynamic_index_in_dim(split, mc, axis=1, keepdims=False)
    away = lax.dynamic_index_in_dim(split, 1 - mc, axis=1, keepdims=False).astype(BF16)
    return _add_halves(own, _sibling_exchange([away], "rs_sibling_exchange")[0])


def _rs_result(parts, pieces):
    mc = lax.axis_index("c")
    chip = (2 * lax.axis_index("x") + lax.axis_index("y")).astype(jnp.int32).reshape(1)
    mine = [_sum_chips(parts, recv, chip, row0) for row0, recv in pieces]
    mine = mine[0] if len(mine) == 1 else jnp.concatenate(mine, axis=0)
    theirs = _sibling_exchange([mine], "rs_sibling_swap")[0]
    return jnp.concatenate([jnp.where(mc == 0, mine, theirs), jnp.where(mc == 0, theirs, mine)], axis=0)


def _adamw(w, g, m, v, plan=None):
    shape = w.shape
    cols = shape[-1]
    if len(shape) == 3:
        lead, rows = shape[0], shape[1]
    else:
        lead, rows = 1, (int(np.prod(shape[:-1])) if len(shape) > 1 else 1)
    w2, g2, m2, v2 = [t.reshape(lead, rows, cols) for t in (w, g, m, v)]
    tr = _row_tile(rows, cols * 4, ADAM_TILE_BYTES)
    per = rows // tr
    steps = lead * per
    p_in, p_out, p_shapes, p_scratch, p_args = _plan_specs(plan)

    def body(w_ref, g_ref, m_ref, v_ref, *rest):
        d_ref, mo_ref, vo_ref = rest[len(p_in):len(p_in) + 3]
        i = pl.program_id(0) * per + pl.program_id(1)

        def compute():
            gv = g_ref[...]
            mn = ADAM_B1 * m_ref[...] + (1.0 - ADAM_B1) * gv
            vn = ADAM_B2 * v_ref[...] + (1.0 - ADAM_B2) * (gv * gv)
            m_hat = mn / (1.0 - ADAM_B1 ** ADAM_STEP)
            v_hat = vn / (1.0 - ADAM_B2 ** ADAM_STEP)
            d_ref[...] = -ADAM_LR * (m_hat / (jnp.sqrt(v_hat) + ADAM_EPS) + ADAM_WD * w_ref[...])
            mo_ref[...] = mn
            vo_ref[...] = vn

        _ride(plan, rest[:len(p_in)] + rest[len(p_in) + 3:], i == 0, i == steps - 1, compute)

    spec = pl.BlockSpec((1, tr, cols), lambda a, i: (a, i, 0))
    sem = "arbitrary" if plan is not None else "parallel"
    outs = pl.pallas_call(
        body, name="adamw", grid=(lead, per), in_specs=[spec] * 4 + p_in, out_specs=[spec] * 3 + p_out,
        out_shape=[jax.ShapeDtypeStruct((lead, rows, cols), F32)] * 3 + p_shapes, scratch_shapes=p_scratch,
        compiler_params=pltpu.CompilerParams(dimension_semantics=(sem, sem), vmem_limit_bytes=ADAM_VMEM_LIMIT),
    )(w2, g2, m2, v2, *p_args)
    res = [t.reshape(shape) for t in outs[:3]]
    return res if plan is None else (res, list(outs[3:]))


WEIGHTS = ['w_ada', 'b_ada', 'norm1_g', 'norm2_g', 'w_in', 'q_norm_g', 'k_norm_g', 'mla_q_norm_g', 'w_uq',
           'mla_kv_norm_g', 'w_ukv', 'conv_w', 'conv_b', 'dt_bias', 'a_log', 'd_skip', 'ssd_norm_g', 'w_out',
           'w_gate_up', 'w_down', 'final_norm_g']
COL_SHARDED = ('w_in', 'w_uq', 'w_ukv', 'w_gate_up')
ROW_SHARDED = ('w_out', 'w_down')
CHIP_BLOCKED = ('w_in', 'w_gate_up')
SMALL_LAYER = ('norm1_g', 'norm2_g', 'q_norm_g', 'k_norm_g', 'mla_q_norm_g', 'mla_kv_norm_g', 'conv_w', 'conv_b',
               'dt_bias', 'a_log', 'd_skip', 'ssd_norm_g')


def _pack(parts):
    flat = jnp.concatenate([p.reshape(-1) for p in parts])
    n = flat.shape[0]
    rows = -(-n // (8 * LANES)) * 8
    return jnp.pad(flat, (0, rows * LANES - n)).reshape(rows, LANES)


def _unpack(flat, shapes):
    out, pos = [], 0
    for shp in shapes:
        size = int(np.prod(shp))
        out.append(flat[pos:pos + size].reshape(shp))
        pos += size
    return out


def _cols_full(gathered):
    k, r, c = gathered.shape
    return jnp.transpose(gathered, (1, 0, 2)).reshape(r, k * c)


def _cols_split(full):
    r, c4 = full.shape
    return jnp.transpose(full.reshape(r, N_CHIPS, c4 // N_CHIPS), (1, 0, 2))


def kernel(x, c, w_ada, b_ada, norm1_g, norm2_g, w_in, q_norm_g, k_norm_g, mla_q_norm_g, w_uq, mla_kv_norm_g, w_ukv, conv_w, conv_b, dt_bias, a_log, d_skip, ssd_norm_g, w_out, w_gate_up, w_down, final_norm_g, loss_target, m_w_ada, m_b_ada, m_norm1_g, m_norm2_g, m_w_in, m_q_norm_g, m_k_norm_g, m_mla_q_norm_g, m_w_uq, m_mla_kv_norm_g, m_w_ukv, m_conv_w, m_conv_b, m_dt_bias, m_a_log, m_d_skip, m_ssd_norm_g, m_w_out, m_w_gate_up, m_w_down, m_final_norm_g, v_w_ada, v_b_ada, v_norm1_g, v_norm2_g, v_w_in, v_q_norm_g, v_k_norm_g, v_mla_q_norm_g, v_w_uq, v_mla_kv_norm_g, v_w_ukv, v_conv_w, v_conv_b, v_dt_bias, v_a_log, v_d_skip, v_ssd_norm_g, v_w_out, v_w_gate_up, v_w_down, v_final_norm_g):
    args = dict(locals())
    weights = {n: args[n] for n in WEIGHTS}
    depth = w_in.shape[0]
    bl, s, d = x.shape
    mx, my, mc = lax.axis_index("x"), lax.axis_index("y"), lax.axis_index("c")
    chip = 2 * mx + my
    dev = 2 * chip + mc
    ada_cols = w_ada.shape[-1]
    conv_cols = conv_w.shape[-1]

    first_shapes = [c.shape, conv_w.shape]
    first = _all_gather_devices(_pack([c, conv_w]))
    first = [_unpack(first[i].reshape(-1), first_shapes) for i in range(N_DEV)]
    c_act = jax.nn.silu(jnp.concatenate([f[0] for f in first], axis=0))
    conv_w_full = jnp.concatenate([first[2 * k][1] for k in range(N_CHIPS)], axis=-1)

    b_cols = lax.dynamic_slice_in_dim(b_ada, chip * ada_cols, ada_cols, axis=1)
    c_act_b = c_act.astype(BF16)
    mod_cols = jnp.stack([_matmul(c_act_b, w_ada[l].astype(BF16), name="ada_fwd") + b_cols[l][None, :]
                          for l in range(depth)])
    mod_all = _all_gather_devices(mod_cols.reshape(depth * N_DEV * bl, ada_cols))
    mod_all = mod_all.reshape(N_DEV, depth, N_DEV, bl, ada_cols)
    mod_mine = lax.dynamic_index_in_dim(mod_all, dev, axis=2, keepdims=False)
    mod = jnp.concatenate([mod_mine[2 * k] for k in range(N_CHIPS)], axis=-1)

    big = COL_SHARDED + ROW_SHARDED
    shards = {(n, l): weights[n][l].astype(BF16) for n in big for l in range(depth)}
    for l in range(depth):
        shards[('w_in', l)] = jnp.pad(shards[('w_in', l)], ((0, 0), (0, IN_SHARD_PAD - IN_SHARD)))
    gathered = _Gathered(shards)
    reducer = _Reducer()
    small_w = []
    for l in range(depth):
        w = {n: weights[n][l] for n in SMALL_LAYER if n != 'conv_w'}
        w['conv_w'] = conv_w_full[l]
        small_w.append(w)
    loss_local, gx, gmod, glayers, gfinal = _forward_backward(x, mod, small_w, gathered, reducer, final_norm_g,
                                                              loss_target)

    small_parts = [jnp.stack([glayers[l][n] for l in range(depth)]) for n in SMALL_LAYER]
    small_parts += [gfinal, loss_local.reshape(1), gmod]
    small_shapes = [p.shape for p in small_parts]
    last = _all_gather_devices(_pack(small_parts))
    summed = _unpack(_sum_leading(last, "sum_devices").reshape(-1), small_shapes)
    small = dict(zip(SMALL_LAYER, summed[:len(SMALL_LAYER)]))
    g_final, loss, gmod_sum = summed[len(SMALL_LAYER):]
    small['conv_w'] = lax.dynamic_slice_in_dim(small['conv_w'], chip * conv_cols, conv_cols, axis=2)
    gmod_all = jnp.stack([_unpack(last[i].reshape(-1), small_shapes)[-1] for i in range(N_DEV)], axis=1)
    gmod_all = gmod_all.reshape(depth, N_DEV * bl, gmod.shape[-1])
    gmod_cols = lax.dynamic_slice_in_dim(gmod_all, chip * ada_cols, ada_cols, axis=2)
    g_w_ada = jnp.stack([_matmul(c_act_b, gmod_cols[l].astype(BF16), ta=True, name="ada_dw") for l in range(depth)])
    g_b_ada = gmod_sum[:, 0]
    for i in range(1, bl):
        g_b_ada = g_b_ada + gmod_sum[:, i]

    grad = {'w_ada': g_w_ada, 'b_ada': g_b_ada, 'final_norm_g': g_final}
    for n in SMALL_LAYER:
        grad[n] = small[n]

    delta, new_m, new_v = {}, {}, {}
    left = [key for key in reducer.parts if key not in reducer.result and not reducer.recv[key]]
    delta['w_ada'], new_m['w_ada'], new_v['w_ada'] = reducer.carry(
        left, lambda p: _adamw(w_ada, g_w_ada, m_w_ada, v_w_ada, plan=p))
    reducer.flush()
    for n in big:
        grad[n] = jnp.stack([reducer.result[(n, l)] for l in range(depth)])[..., :weights[n].shape[-1]]
    for n in WEIGHTS:
        if n != 'w_ada':
            delta[n], new_m[n], new_v[n] = _adamw(weights[n], grad[n], args["m_" + n], args["v_" + n])
    return (loss.reshape(()), gx, *[grad[n] for n in WEIGHTS], *[delta[n] for n in WEIGHTS],
            *[new_m[n] for n in WEIGHTS], *[new_v[n] for n in WEIGHTS])
```

```python
import functools

import numpy as np
import jax
import jax.numpy as jnp
from jax import lax
from jax.experimental import pallas as pl
from jax.experimental.pallas import tpu as pltpu

F32 = jnp.float32
BF16 = jnp.bfloat16
HIGHEST = lax.Precision.HIGHEST
MESH = pl.DeviceIdType.MESH

GRID_W = 64
ROPE_THETA = 10000.0
EPS = 1e-6

GQA_HEADS, GQA_KV_HEADS, GQA_HEAD_DIM = 6, 2, 128
GQA_WIDTH = GQA_HEADS * GQA_HEAD_DIM
GQA_KV_WIDTH = GQA_KV_HEADS * GQA_HEAD_DIM
MLA_HEADS, MLA_Q_LORA, MLA_KV_LORA = 4, 512, 256
MLA_NOPE_DIM, MLA_ROPE_DIM, MLA_V_DIM = 128, 64, 128
SSD_HEADS, SSD_HEAD_DIM, SSD_GROUPS, SSD_STATE, SSD_CONV, SSD_CHUNK = 12, 64, 2, 128, 5, 128
SSD_INNER = SSD_HEADS * SSD_HEAD_DIM
SSD_CONV_DIM = SSD_INNER + 2 * SSD_GROUPS * SSD_STATE
SSD_GROUP_HEADS = SSD_HEADS // SSD_GROUPS
SSD_GROUP_WIDTH = SSD_GROUP_HEADS * SSD_HEAD_DIM
IN_SPLITS = (GQA_WIDTH, GQA_KV_WIDTH, GQA_KV_WIDTH, MLA_Q_LORA, MLA_KV_LORA, MLA_ROPE_DIM, SSD_INNER, SSD_CONV_DIM,
             2 * SSD_HEADS)
IN_COLS = sum(IN_SPLITS)
LANES = 128
N_CHIPS = 4
IN_SHARD = IN_COLS // N_CHIPS
IN_SHARD_PAD = -(-IN_SHARD // LANES) * LANES


def _in_cols(proj, lo, hi):
    parts = []
    for chip in range(lo // IN_SHARD, (hi - 1) // IN_SHARD + 1):
        a, z = max(lo, chip * IN_SHARD), min(hi, (chip + 1) * IN_SHARD)
        base = chip * IN_SHARD_PAD - chip * IN_SHARD
        parts.append(proj[:, base + a:base + z])
    return parts[0] if len(parts) == 1 else jnp.concatenate(parts, axis=-1)

ADAM_LR, ADAM_B1, ADAM_B2, ADAM_EPS, ADAM_WD, ADAM_STEP = 0.001, 0.9, 0.999, 1e-08, 0.01, 10

N_DEV = 8
TILE_BYTES = 2 * 1024 * 1024


def _pick(n, cands):
    for t in cands:
        if n % t == 0:
            return t
    return n


ADAM_TILE_BYTES = 3 * 512 * 1024
ADAM_VMEM_LIMIT = 40 * 1024 * 1024
SWIGLU_TILE_BYTES = 4 * 1024 * 1024


def _row_tile(rows, row_bytes, limit=TILE_BYTES):
    for t in (2048, 1024, 512, 256, 128, 64, 32, 16, 8):
        if rows % t == 0 and t * row_bytes <= limit:
            return t
    return rows


MM_VMEM_BUDGET = 36 * 1024 * 1024
MM_VMEM_LIMIT = 56 * 1024 * 1024
MM_MAX_TILE = 2048
MM_MAX_K_TILE = 4096
MXU_DIM = 256
HBM_BYTES_PER_US = 3.0e6
MXU_FLOPS_PER_US = 9.0e8
STEP_US = 0.35


def _tile_cands(d, cap):
    if d % LANES:
        return [d]
    return [t for t in range(LANES, min(d, cap) + 1, LANES) if d % t == 0] or [d]


def _mm_tiles(m, n, kdim, n_unit=None, k_unit=None):
    up = lambda t: -(-t // MXU_DIM) * MXU_DIM
    best = None
    for tm in _tile_cands(m, MM_MAX_TILE):
        for tn in _tile_cands(n_unit or n, MM_MAX_TILE):
            for tk in _tile_cands(k_unit or kdim, MM_MAX_K_TILE):
                if 2 * (tm * tk * 2 + tk * tn * 2 + tm * tn * 4) > MM_VMEM_BUDGET:
                    continue
                ni, nj, nk = m // tm, n // tn, kdim // tk
                a_reads = 1 if nk == 1 else nj
                b_reads = 1 if (nk == 1 and nj == 1) else ni
                hbm = (m * kdim * 2 * a_reads + kdim * n * 2 * b_reads + m * n * 4) / HBM_BYTES_PER_US
                mxu = ni * nj * nk * 2.0 * max(tm, 8) * up(tn) * up(tk) / MXU_FLOPS_PER_US
                cost = max(hbm, mxu) + 0.25 * min(hbm, mxu) + ni * nj * nk * STEP_US
                if best is None or cost < best[0]:
                    best = (cost, tm, tn, tk)
    return best[1:]


def _ride(plan, refs, first, last, compute):
    if plan is None:
        compute()
        return
    ins, outs, sems = plan.split(refs)

    @pl.when(first)
    def _():
        plan.start(ins, outs, sems)

    compute()

    @pl.when(last)
    def _():
        plan.finish(ins, outs, sems)


def _plan_specs(plan):
    if plan is None:
        return [], [], [], [], []
    in_specs, out_specs, scratch = plan.specs()
    return in_specs, out_specs, plan.out_shapes, scratch, plan.inputs


def _matmul(a, b, ta=False, tb=False, name="mm", plan=None, chips=None):
    assert a.dtype == BF16 and b.dtype in (BF16, F32), (a.dtype, b.dtype)
    if ta:
        kdim, m = a.shape
    else:
        m, kdim = a.shape
    n_unit = k_unit = None
    if chips == 'b':
        nb, rows, unit = b.shape
        if tb:
            n, k2, k_unit = rows, nb * unit, unit
        else:
            k2, n, n_unit = rows, nb * unit, unit
    else:
        if tb:
            n, k2 = b.shape
        else:
            k2, n = b.shape
        if chips == 'out':
            n_unit = n // N_CHIPS
    assert kdim == k2, (a.shape, b.shape, ta, tb)
    tm, tn, tk = _mm_tiles(m, n, kdim, n_unit, k_unit)
    ni, nj, nk = m // tm, n // tn, kdim // tk
    dn = (((0 if ta else 1,), (1 if tb else 0,)), ((), ()))
    p_in, p_out, p_shapes, p_scratch, p_args = _plan_specs(plan)

    def body(a_ref, b_ref, *rest):
        o_ref = rest[len(p_in)]
        i, j, k = pl.program_id(0), pl.program_id(1), pl.program_id(2)

        def compute():
            bv = (b_ref[0] if chips == 'b' else b_ref[...]).astype(BF16)
            part = lax.dot_general(a_ref[...], bv, dn, preferred_element_type=F32)
            if chips == 'out':
                part = part[None]
            if nk == 1:
                o_ref[...] = part
            else:
                @pl.when(k == 0)
                def _():
                    o_ref[...] = part

                @pl.when(k > 0)
                def _():
                    o_ref[...] += part

        _ride(plan, rest[:len(p_in)] + rest[len(p_in) + 1:], (i == 0) & (j == 0) & (k == 0),
              (i == ni - 1) & (j == nj - 1) & (k == nk - 1), compute)

    a_spec = pl.BlockSpec((tk, tm), lambda i, j, k: (k, i)) if ta else pl.BlockSpec((tm, tk), lambda i, j, k: (i, k))
    if chips == 'b' and tb:
        per = k_unit // tk
        b_spec = pl.BlockSpec((1, tn, tk), lambda i, j, k: (k // per, j, k % per))
    elif chips == 'b':
        per = n_unit // tn
        b_spec = pl.BlockSpec((1, tk, tn), lambda i, j, k: (j // per, k, j % per))
    else:
        b_spec = pl.BlockSpec((tn, tk), lambda i, j, k: (j, k)) if tb else pl.BlockSpec((tk, tn), lambda i, j, k: (k, j))
    if chips == 'out':
        per = n_unit // tn
        o_spec = pl.BlockSpec((1, tm, tn), lambda i, j, k: (j // per, i, j % per))
        o_shape = jax.ShapeDtypeStruct((N_CHIPS, m, n_unit), F32)
    else:
        o_spec = pl.BlockSpec((tm, tn), lambda i, j, k: (i, j))
        o_shape = jax.ShapeDtypeStruct((m, n), F32)
    outs = pl.pallas_call(
        body, name=name, grid=(ni, nj, nk),
        in_specs=[a_spec, b_spec] + p_in, out_specs=[o_spec] + p_out,
        out_shape=[o_shape] + p_shapes, scratch_shapes=p_scratch,
        compiler_params=pltpu.CompilerParams(
            dimension_semantics=("arbitrary" if plan is not None else "parallel", "arbitrary", "arbitrary"),
            vmem_limit_bytes=MM_VMEM_LIMIT),
    )(a, b, *p_args)
    return outs[0] if plan is None else (outs[0], list(outs[1:]))


@jax.custom_vjp
def linear(x, w):
    return _matmul(x.astype(BF16), w.astype(BF16), name="linear_fwd")


def _linear_fwd(x, w):
    xb, wb = x.astype(BF16), w.astype(BF16)
    return _matmul(xb, wb, name="linear_fwd"), (xb, wb)


def _linear_bwd(res, dy):
    xb, wb = res
    dyb = dy.astype(BF16)
    return _matmul(dyb, wb, tb=True, name="linear_dx"), _matmul(xb, dyb, ta=True, name="linear_dw")


linear.defvjp(_linear_fwd, _linear_bwd)


def _rms_fwd_call(x, g, groups):
    rows, cols = x.shape
    d = cols // groups
    tr = _row_tile(rows, cols * 4)

    def body(x_ref, g_ref, y_ref):
        for gi in range(groups):
            sl = slice(gi * d, (gi + 1) * d)
            xs = x_ref[:, sl]
            r = lax.rsqrt(jnp.mean(xs * xs, axis=-1, keepdims=True) + EPS)
            y_ref[:, sl] = xs * r * g_ref[:, sl]

    return pl.pallas_call(
        body, name="rms_fwd", grid=(rows // tr,),
        in_specs=[pl.BlockSpec((tr, cols), lambda i: (i, 0)), pl.BlockSpec((1, cols), lambda i: (0, 0))],
        out_specs=pl.BlockSpec((tr, cols), lambda i: (i, 0)),
        out_shape=jax.ShapeDtypeStruct((rows, cols), F32),
        compiler_params=pltpu.CompilerParams(dimension_semantics=("parallel",)),
    )(x, g)


def _rms_bwd_call(x, g, dy, groups):
    rows, cols = x.shape
    d = cols // groups
    tr = _row_tile(rows, cols * 4)

    def body(x_ref, g_ref, dy_ref, dx_ref, dg_ref):
        @pl.when(pl.program_id(0) == 0)
        def _():
            dg_ref[...] = jnp.zeros_like(dg_ref)

        for gi in range(groups):
            sl = slice(gi * d, (gi + 1) * d)
            xs = x_ref[:, sl]
            dys = dy_ref[:, sl]
            r = lax.rsqrt(jnp.mean(xs * xs, axis=-1, keepdims=True) + EPS)
            xhat = xs * r
            dg_ref[:, sl] += jnp.sum(dys * xhat, axis=0, keepdims=True)
            dxhat = dys * g_ref[:, sl]
            dx_ref[:, sl] = r * (dxhat - xhat * jnp.mean(dxhat * xhat, axis=-1, keepdims=True))

    return pl.pallas_call(
        body, name="rms_bwd", grid=(rows // tr,),
        in_specs=[pl.BlockSpec((tr, cols), lambda i: (i, 0)), pl.BlockSpec((1, cols), lambda i: (0, 0)),
                  pl.BlockSpec((tr, cols), lambda i: (i, 0))],
        out_specs=[pl.BlockSpec((tr, cols), lambda i: (i, 0)), pl.BlockSpec((1, cols), lambda i: (0, 0))],
        out_shape=[jax.ShapeDtypeStruct((rows, cols), F32), jax.ShapeDtypeStruct((1, cols), F32)],
        compiler_params=pltpu.CompilerParams(dimension_semantics=("arbitrary",)),
    )(x, g, dy)


@functools.partial(jax.custom_vjp, nondiff_argnums=(2,))
def rms_norm(x, g, groups):
    return _rms_fwd_call(x, g, groups)


def _rms_norm_fwd(x, g, groups):
    return _rms_fwd_call(x, g, groups), (x, g)


def _rms_norm_bwd(groups, res, dy):
    x, g = res
    dx, dg = _rms_bwd_call(x, g, dy, groups)
    return dx, dg


rms_norm.defvjp(_rms_norm_fwd, _rms_norm_bwd)


NT_DIMS = (((1,), (1,)), ((), ()))
TN_DIMS = (((0,), (0,)), ((), ()))


LOG2E = 1.4426950408889634
ATTN_VMEM_LIMIT = 48 * 1024 * 1024

def _exp_rows(q, k, scale):
    s2 = lax.dot_general(q, k, NT_DIMS, preferred_element_type=F32) * (scale * LOG2E)
    e = jnp.exp2(s2 - jnp.max(s2, axis=-1, keepdims=True))
    return e, 1.0 / jnp.sum(e, axis=-1, keepdims=True)


def _attn_fwd_call(q, k, v, scale):
    b, h, s, dk = q.shape
    hkv, dv = k.shape[1], v.shape[3]
    rep = h // hkv
    tq = _pick(s, (512, 256, 128))

    def body(q_ref, k_ref, v_ref, o_ref):
        e, inv = _exp_rows(q_ref[0, 0], k_ref[0, 0], scale)
        o_ref[0, 0] = jnp.dot(e.astype(BF16), v_ref[0, 0], preferred_element_type=F32) * inv

    return pl.pallas_call(
        body, name="attn_fwd", grid=(b, h, s // tq),
        in_specs=[pl.BlockSpec((1, 1, tq, dk), lambda bi, hi, qi: (bi, hi, qi, 0)),
                  pl.BlockSpec((1, 1, s, dk), lambda bi, hi, qi: (bi, hi // rep, 0, 0)),
                  pl.BlockSpec((1, 1, s, dv), lambda bi, hi, qi: (bi, hi // rep, 0, 0))],
        out_specs=pl.BlockSpec((1, 1, tq, dv), lambda bi, hi, qi: (bi, hi, qi, 0)),
        out_shape=jax.ShapeDtypeStruct((b, h, s, dv), F32),
        compiler_params=pltpu.CompilerParams(dimension_semantics=("parallel", "parallel", "parallel"),
                                             vmem_limit_bytes=ATTN_VMEM_LIMIT),
    )(q, k, v)


def _attn_bwd_call(q, k, v, do, scale):
    b, h, s, dk = q.shape
    hkv, dv = k.shape[1], v.shape[3]
    rep = h // hkv
    tq = _pick(s, (512, 256, 128))

    def body(q_ref, k_ref, v_ref, do_ref, dq_ref, dk_ref, dv_ref):
        @pl.when((pl.program_id(2) == 0) & (pl.program_id(3) == 0))
        def _():
            dk_ref[...] = jnp.zeros_like(dk_ref)
            dv_ref[...] = jnp.zeros_like(dv_ref)

        qb = q_ref[0, 0]
        kb = k_ref[0, 0]
        vb = v_ref[0, 0]
        dob = do_ref[0, 0]
        e, inv = _exp_rows(qb, kb, scale)
        dp = lax.dot_general(dob, vb, NT_DIMS, preferred_element_type=F32)
        delta = jnp.sum(e * dp, axis=-1, keepdims=True) * inv
        ds = (e * ((dp - delta) * (inv * scale))).astype(BF16)
        dq_ref[0, 0] = jnp.dot(ds, kb, preferred_element_type=F32)
        dk_ref[0, 0] += lax.dot_general(ds, qb, TN_DIMS, preferred_element_type=F32)
        dv_ref[0, 0] += lax.dot_general(e.astype(BF16), (dob.astype(F32) * inv).astype(BF16), TN_DIMS,
                                        preferred_element_type=F32)

    return pl.pallas_call(
        body, name="attn_bwd", grid=(b, hkv, rep, s // tq),
        in_specs=[pl.BlockSpec((1, 1, tq, dk), lambda bi, gi, ri, qi: (bi, gi * rep + ri, qi, 0)),
                  pl.BlockSpec((1, 1, s, dk), lambda bi, gi, ri, qi: (bi, gi, 0, 0)),
                  pl.BlockSpec((1, 1, s, dv), lambda bi, gi, ri, qi: (bi, gi, 0, 0)),
                  pl.BlockSpec((1, 1, tq, dv), lambda bi, gi, ri, qi: (bi, gi * rep + ri, qi, 0))],
        out_specs=[pl.BlockSpec((1, 1, tq, dk), lambda bi, gi, ri, qi: (bi, gi * rep + ri, qi, 0)),
                   pl.BlockSpec((1, 1, s, dk), lambda bi, gi, ri, qi: (bi, gi, 0, 0)),
                   pl.BlockSpec((1, 1, s, dv), lambda bi, gi, ri, qi: (bi, gi, 0, 0))],
        out_shape=[jax.ShapeDtypeStruct(q.shape, F32), jax.ShapeDtypeStruct(k.shape, F32),
                   jax.ShapeDtypeStruct(v.shape, F32)],
        compiler_params=pltpu.CompilerParams(
            dimension_semantics=("parallel", "parallel", "arbitrary", "arbitrary"), vmem_limit_bytes=ATTN_VMEM_LIMIT),
    )(q, k, v, do)


@functools.partial(jax.custom_vjp, nondiff_argnums=(3,))
def attention(q, k, v, scale):
    return _attn_fwd_call(q.astype(BF16), k.astype(BF16), v.astype(BF16), scale)


def _attention_fwd(q, k, v, scale):
    qb, kb, vb = q.astype(BF16), k.astype(BF16), v.astype(BF16)
    return _attn_fwd_call(qb, kb, vb, scale), (qb, kb, vb)


def _attention_bwd(scale, res, do):
    qb, kb, vb = res
    return tuple(_attn_bwd_call(qb, kb, vb, do.astype(BF16), scale))


attention.defvjp(_attention_fwd, _attention_bwd)


CONV_COL_TILE = 256
CONV_PACK_ROWS = 8


def _shifted(x, off, rows):
    if off == 0:
        return x
    s = x.shape[0]
    rolled = pltpu.roll(x, (-off) % s, 0)
    valid = (rows + off >= 0) & (rows + off < s)
    return jnp.where(valid, rolled, 0.0)


def _conv_pre(x, wb_ref, rows):
    z = jnp.zeros_like(x) + wb_ref[SSD_CONV:SSD_CONV + 1, :]
    for j in range(SSD_CONV):
        z = z + wb_ref[j:j + 1, :] * _shifted(x, j - SSD_CONV // 2, rows)
    return z


def _conv_fwd_call(x, wb):
    b, s, c = x.shape
    tc = _pick(c, (CONV_COL_TILE, LANES))

    def body(x_ref, wb_ref, y_ref):
        xv = x_ref[0]
        rows = lax.broadcasted_iota(jnp.int32, xv.shape, 0)
        z = _conv_pre(xv, wb_ref, rows)
        y_ref[0] = z * jax.nn.sigmoid(z)

    return pl.pallas_call(
        body, name="conv_fwd", grid=(b, c // tc),
        in_specs=[pl.BlockSpec((1, s, tc), lambda bi, ci: (bi, 0, ci)),
                  pl.BlockSpec((CONV_PACK_ROWS, tc), lambda bi, ci: (0, ci))],
        out_specs=pl.BlockSpec((1, s, tc), lambda bi, ci: (bi, 0, ci)),
        out_shape=jax.ShapeDtypeStruct(x.shape, F32),
        compiler_params=pltpu.CompilerParams(dimension_semantics=("parallel", "parallel")),
    )(x, wb)


def _conv_bwd_call(x, wb, dy):
    b, s, c = x.shape
    tc = _pick(c, (CONV_COL_TILE, LANES))

    def body(x_ref, wb_ref, dy_ref, dx_ref, dwb_ref):
        xv = x_ref[0]
        rows = lax.broadcasted_iota(jnp.int32, xv.shape, 0)
        z = _conv_pre(xv, wb_ref, rows)
        sg = jax.nn.sigmoid(z)
        dz = dy_ref[0] * (sg * (1.0 + z * (1.0 - sg)))
        dx = jnp.zeros_like(xv)
        for j in range(SSD_CONV):
            off = j - SSD_CONV // 2
            dx = dx + wb_ref[j:j + 1, :] * _shifted(dz, -off, rows)
            dwb_ref[0, j:j + 1, :] = jnp.sum(dz * _shifted(xv, off, rows), axis=0, keepdims=True)
        dx_ref[0] = dx
        dwb_ref[0, SSD_CONV:SSD_CONV + 1, :] = jnp.sum(dz, axis=0, keepdims=True)
        dwb_ref[0, SSD_CONV + 1:, :] = jnp.zeros((CONV_PACK_ROWS - SSD_CONV - 1, dz.shape[1]), F32)

    return pl.pallas_call(
        body, name="conv_bwd", grid=(b, c // tc),
        in_specs=[pl.BlockSpec((1, s, tc), lambda bi, ci: (bi, 0, ci)),
                  pl.BlockSpec((CONV_PACK_ROWS, tc), lambda bi, ci: (0, ci)),
                  pl.BlockSpec((1, s, tc), lambda bi, ci: (bi, 0, ci))],
        out_specs=[pl.BlockSpec((1, s, tc), lambda bi, ci: (bi, 0, ci)),
                   pl.BlockSpec((1, CONV_PACK_ROWS, tc), lambda bi, ci: (bi, 0, ci))],
        out_shape=[jax.ShapeDtypeStruct(x.shape, F32), jax.ShapeDtypeStruct((b, CONV_PACK_ROWS, c), F32)],
        compiler_params=pltpu.CompilerParams(dimension_semantics=("parallel", "parallel")),
    )(x, wb, dy)


@jax.custom_vjp
def conv_silu(x, wb):
    return _conv_fwd_call(x, wb)


def _conv_silu_fwd(x, wb):
    return _conv_fwd_call(x, wb), (x, wb)


def _conv_silu_bwd(res, dy):
    x, wb = res
    dx, dwb = _conv_bwd_call(x, wb, dy)
    return dx, jnp.sum(dwb, axis=0)


conv_silu.defvjp(_conv_silu_fwd, _conv_silu_bwd)


SSD_PAIRS = SSD_GROUP_HEADS // 2
NEG_INF = -1e30


def _ssd_common(x_ref, dtx_ref, dtt_ref, anx_ref, anc_ref, b_ref, c_ref, reverse):
    L = SSD_CHUNK
    xv = x_ref[0]
    dt = dtx_ref[0]
    ri = lax.broadcasted_iota(jnp.int32, (L, L), 0)
    ci = lax.broadcasted_iota(jnp.int32, (L, L), 1)
    causal = (ri <= ci) if reverse else (ri >= ci)
    tri = causal.astype(F32)
    a_cs = jnp.dot(tri, dt * anx_ref[...], precision=HIGHEST, preferred_element_type=F32)
    a_row = dtt_ref[0, 0] * anc_ref[0]
    acs_row = lax.dot_general(a_row, tri, NT_DIMS, precision=HIGHEST, preferred_element_type=F32)
    xd = xv * dt
    bmat = b_ref[0].astype(BF16)
    cmat = c_ref[0].astype(BF16)
    gmat = lax.dot_general(cmat, bmat, NT_DIMS, preferred_element_type=F32)
    return xv, dt, causal, tri, a_cs, acs_row, xd, bmat, cmat, gmat


def _ssd_lambda(a_cs, acs_row, causal, h):
    col = a_cs[:, h * SSD_HEAD_DIM:h * SSD_HEAD_DIM + 1]
    row = acs_row[h:h + 1, :]
    return jnp.exp(jnp.where(causal, col - row, NEG_INF))


def _ssd_fwd_call(x, dtx, dtt, anx, anc, bm, cm, reverse):
    b, s, _ = x.shape
    L, N, GW = SSD_CHUNK, SSD_STATE, SSD_GROUP_WIDTH
    nc = s // L
    end = 0 if reverse else L - 1

    def body(x_ref, dtx_ref, dtt_ref, anx_ref, anc_ref, b_ref, c_ref, y_ref, hs_ref, state):
        @pl.when(pl.program_id(2) == 0)
        def _():
            state[...] = jnp.zeros_like(state)

        xv, dt, causal, tri, a_cs, acs_row, xd, bmat, cmat, gmat = _ssd_common(
            x_ref, dtx_ref, dtt_ref, anx_ref, anc_ref, b_ref, c_ref, reverse)
        hin = state[...]
        hs_ref[0, 0, 0] = hin
        y_off = jnp.dot(cmat, hin.astype(BF16), preferred_element_type=F32) * jnp.exp(a_cs)
        a_end = a_cs[end:end + 1, :]
        s_new = lax.dot_general(bmat, (xd * jnp.exp(a_end - a_cs)).astype(BF16), TN_DIMS, preferred_element_type=F32)
        state[...] = jnp.exp(a_end) * hin + s_new
        lane = lax.broadcasted_iota(jnp.int32, (L, LANES), 1)
        for pr in range(SSD_PAIRS):
            sl = slice(pr * LANES, (pr + 1) * LANES)
            xdp = xd[:, sl].astype(BF16)
            w0 = (gmat * _ssd_lambda(a_cs, acs_row, causal, 2 * pr)).astype(BF16)
            w1 = (gmat * _ssd_lambda(a_cs, acs_row, causal, 2 * pr + 1)).astype(BF16)
            y0 = jnp.dot(w0, xdp, preferred_element_type=F32)
            y1 = jnp.dot(w1, xdp, preferred_element_type=F32)
            y_ref[0, :, sl] = jnp.where(lane < SSD_HEAD_DIM, y0, y1) + y_off[:, sl]

    G = SSD_GROUPS
    chunk = (lambda c: nc - 1 - c) if reverse else (lambda c: c)
    seq = lambda bi, gi, c: (bi, chunk(c), gi)
    return pl.pallas_call(
        body, name="ssd_fwd", grid=(b, G, nc),
        in_specs=[pl.BlockSpec((1, L, GW), seq),
                  pl.BlockSpec((1, L, GW), seq),
                  pl.BlockSpec((1, 1, SSD_GROUP_HEADS, L), lambda bi, gi, c: (bi, gi, 0, chunk(c))),
                  pl.BlockSpec((1, GW), lambda bi, gi, c: (0, gi)),
                  pl.BlockSpec((1, SSD_GROUP_HEADS, 1), lambda bi, gi, c: (gi, 0, 0)),
                  pl.BlockSpec((1, L, N), seq),
                  pl.BlockSpec((1, L, N), seq)],
        out_specs=[pl.BlockSpec((1, L, GW), seq),
                   pl.BlockSpec((1, 1, 1, N, GW), lambda bi, gi, c: (bi, gi, chunk(c), 0, 0))],
        out_shape=[jax.ShapeDtypeStruct(x.shape, F32), jax.ShapeDtypeStruct((b, G, nc, N, GW), F32)],
        scratch_shapes=[pltpu.VMEM((N, GW), F32)],
        compiler_params=pltpu.CompilerParams(dimension_semantics=("parallel", "parallel", "arbitrary")),
    )(x, dtx, dtt, anx, anc, bm, cm)


def _ssd_bwd_call(x, dtx, dtt, anx, anc, bm, cm, hs, dy, reverse):
    b, s, _ = x.shape
    L, N, GW = SSD_CHUNK, SSD_STATE, SSD_GROUP_WIDTH
    nc = s // L
    end = 0 if reverse else L - 1

    def body(x_ref, dtx_ref, dtt_ref, anx_ref, anc_ref, b_ref, c_ref, hs_ref, dy_ref,
             dx_ref, ddt_ref, dan_ref, db_ref, dc_ref, dstate):
        @pl.when(pl.program_id(2) == 0)
        def _():
            dstate[...] = jnp.zeros_like(dstate)

        xv, dt, causal, tri, a_cs, acs_row, xd, bmat, cmat, gmat = _ssd_common(
            x_ref, dtx_ref, dtt_ref, anx_ref, anc_ref, b_ref, c_ref, reverse)
        hin = hs_ref[0, 0, 0]
        hinb = hin.astype(BF16)
        dyv = dy_ref[0]
        ds_out = dstate[...]
        dsb = ds_out.astype(BF16)
        eacs = jnp.exp(a_cs)
        a_end = a_cs[end:end + 1, :]
        e_end = jnp.exp(a_end)
        dec = jnp.exp(a_end - a_cs)
        dye = dyv * eacs
        dyeb = dye.astype(BF16)
        xdec = xd * dec
        ch = jnp.dot(cmat, hinb, preferred_element_type=F32)
        bds = jnp.dot(bmat, dsb, preferred_element_type=F32)
        t_state = xdec * bds
        d_aend = jnp.sum(t_state, axis=0, keepdims=True) + e_end * jnp.sum(ds_out * hin, axis=0, keepdims=True)
        dacs = dye * ch - t_state
        dxd_state = bds * dec
        dstate[...] = e_end * ds_out + lax.dot_general(cmat, dyeb, TN_DIMS, preferred_element_type=F32)

        lane = lax.broadcasted_iota(jnp.int32, (L, LANES), 1)
        dg = jnp.zeros((L, L), F32)
        dxd_parts, dacs_parts = [], []
        for pr in range(SSD_PAIRS):
            sl = slice(pr * LANES, (pr + 1) * LANES)
            xdp = xd[:, sl]
            dyp = dyv[:, sl]
            dxd_p = jnp.zeros((L, LANES), F32)
            dacs_p = jnp.zeros((L, LANES), F32)
            for half in range(2):
                mine = (lane < SSD_HEAD_DIM) if half == 0 else (lane >= SSD_HEAD_DIM)
                lam = _ssd_lambda(a_cs, acs_row, causal, 2 * pr + half)
                w = gmat * lam
                xdh = jnp.where(mine, xdp, 0.0).astype(BF16)
                dyh = jnp.where(mine, dyp, 0.0).astype(BF16)
                dw = lax.dot_general(dyh, xdh, NT_DIMS, preferred_element_type=F32)
                dg = dg + dw * lam
                mm = dw * w
                rs = jnp.sum(mm, axis=1, keepdims=True)
                cs = jnp.sum(mm.T, axis=1, keepdims=True)
                dacs_p = dacs_p + jnp.where(mine, (rs - cs) * (1.0 / SSD_HEAD_DIM), 0.0)
                wtdy = lax.dot_general(w.astype(BF16), dyh, TN_DIMS, preferred_element_type=F32)
                dxd_p = dxd_p + wtdy
            dxd_parts.append(dxd_p)
            dacs_parts.append(dacs_p)
        dxd = jnp.concatenate(dxd_parts, axis=1) + dxd_state
        dacs = dacs + jnp.concatenate(dacs_parts, axis=1)
        last = lax.broadcasted_iota(jnp.int32, dacs.shape, 0) == end
        dacs = dacs + jnp.where(last, d_aend, 0.0)
        da = lax.dot_general(tri, dacs, TN_DIMS, precision=HIGHEST, preferred_element_type=F32)
        dgb = dg.astype(BF16)
        dc_ref[0] = (jnp.dot(dgb, bmat, preferred_element_type=F32)
                     + lax.dot_general(dyeb, hinb, NT_DIMS, preferred_element_type=F32))
        db_ref[0] = (lax.dot_general(dgb, cmat, TN_DIMS, preferred_element_type=F32)
                     + lax.dot_general(xdec.astype(BF16), dsb, NT_DIMS, preferred_element_type=F32))
        dx_ref[0] = dxd * dt
        ddt_ref[0] = da * anx_ref[...] + dxd * xv
        dan_ref[0, 0, 0] = jnp.sum(da * dt, axis=0, keepdims=True)

    G = SSD_GROUPS
    chunk = (lambda c: c) if reverse else (lambda c: nc - 1 - c)
    rev = lambda bi, gi, c: (bi, chunk(c), gi)
    return pl.pallas_call(
        body, name="ssd_bwd", grid=(b, G, nc),
        in_specs=[pl.BlockSpec((1, L, GW), rev),
                  pl.BlockSpec((1, L, GW), rev),
                  pl.BlockSpec((1, 1, SSD_GROUP_HEADS, L), lambda bi, gi, c: (bi, gi, 0, chunk(c))),
                  pl.BlockSpec((1, GW), lambda bi, gi, c: (0, gi)),
                  pl.BlockSpec((1, SSD_GROUP_HEADS, 1), lambda bi, gi, c: (gi, 0, 0)),
                  pl.BlockSpec((1, L, N), rev),
                  pl.BlockSpec((1, L, N), rev),
                  pl.BlockSpec((1, 1, 1, N, GW), lambda bi, gi, c: (bi, gi, chunk(c), 0, 0)),
                  pl.BlockSpec((1, L, GW), rev)],
        out_specs=[pl.BlockSpec((1, L, GW), rev),
                   pl.BlockSpec((1, L, GW), rev),
                   pl.BlockSpec((1, 1, 1, 1, GW), lambda bi, gi, c: (bi, gi, chunk(c), 0, 0)),
                   pl.BlockSpec((1, L, N), rev),
                   pl.BlockSpec((1, L, N), rev)],
        out_shape=[jax.ShapeDtypeStruct(x.shape, F32), jax.ShapeDtypeStruct(x.shape, F32),
                   jax.ShapeDtypeStruct((b, G, nc, 1, GW), F32),
                   jax.ShapeDtypeStruct(bm.shape, F32), jax.ShapeDtypeStruct(cm.shape, F32)],
        scratch_shapes=[pltpu.VMEM((N, GW), F32)],
        compiler_params=pltpu.CompilerParams(dimension_semantics=("parallel", "parallel", "arbitrary")),
    )(x, dtx, dtt, anx, anc, bm, cm, hs, dy)


@functools.partial(jax.custom_vjp, nondiff_argnums=(7,))
def _ssd_scan(x, dtx, dtt, anx, anc, bm, cm, reverse):
    return _ssd_fwd_call(x, dtx, dtt, anx, anc, bm, cm, reverse)[0]


def _ssd_scan_fwd(x, dtx, dtt, anx, anc, bm, cm, reverse):
    y, hs = _ssd_fwd_call(x, dtx, dtt, anx, anc, bm, cm, reverse)
    return y, (x, dtx, dtt, anx, anc, bm, cm, hs)


def _ssd_scan_bwd(reverse, res, dy):
    x, dtx, dtt, anx, anc, bm, cm, hs = res
    dx, ddtx, dan, db, dc = _ssd_bwd_call(x, dtx, dtt, anx, anc, bm, cm, hs, dy, reverse)
    b, g, nc, _, gw = dan.shape
    danx = jnp.sum(dan, axis=(0, 2, 3)).reshape(1, g * gw)
    return dx, ddtx, jnp.zeros_like(dtt), danx, jnp.zeros_like(anc), db, dc


_ssd_scan.defvjp(_ssd_scan_fwd, _ssd_scan_bwd)


def ssd_chunked(xs, dt, a_neg, bm, cm, reverse):
    b, s, _ = xs.shape
    dtx = jnp.repeat(dt, SSD_HEAD_DIM, axis=-1)
    dtt = jnp.transpose(dt, (0, 2, 1)).reshape(b, SSD_GROUPS, SSD_GROUP_HEADS, s)
    anx = jnp.repeat(a_neg, SSD_HEAD_DIM)[None, :]
    anc = a_neg.reshape(SSD_GROUPS, SSD_GROUP_HEADS, 1)
    return _ssd_scan(xs, dtx, dtt, anx, anc, bm, cm, reverse)


def _loss_call(y, t):
    rows, cols = y.shape
    tr = _row_tile(rows, cols * 4)

    def body(y_ref, t_ref, loss_ref, diff_ref):
        @pl.when(pl.program_id(0) == 0)
        def _():
            loss_ref[...] = jnp.zeros_like(loss_ref)

        d = y_ref[...] - t_ref[...]
        diff_ref[...] = d * (1.0 / cols)
        part = jnp.sum(jnp.sum(d * d, axis=1, keepdims=True), axis=0, keepdims=True)
        loss_ref[...] += part * (0.5 / cols)

    return pl.pallas_call(
        body, name="loss_head", grid=(rows // tr,),
        in_specs=[pl.BlockSpec((tr, cols), lambda i: (i, 0)), pl.BlockSpec((tr, cols), lambda i: (i, 0))],
        out_specs=[pl.BlockSpec((1, 1), lambda i: (0, 0)), pl.BlockSpec((tr, cols), lambda i: (i, 0))],
        out_shape=[jax.ShapeDtypeStruct((1, 1), F32), jax.ShapeDtypeStruct((rows, cols), F32)],
        compiler_params=pltpu.CompilerParams(dimension_semantics=("arbitrary",)),
    )(y, t)


@jax.custom_vjp
def loss_head(y, t):
    return _loss_call(y, t)[0][0, 0]


def _loss_head_fwd(y, t):
    loss, diff = _loss_call(y, t)
    return loss[0, 0], diff


def _loss_head_bwd(diff, g):
    return g * diff, jnp.zeros_like(diff)


loss_head.defvjp(_loss_head_fwd, _loss_head_bwd)


def _axial_rope_tables(seq_len, rot_dim):
    rows = seq_len // GRID_W
    row_idx = jnp.repeat(jnp.arange(rows), GRID_W).astype(F32)
    col_idx = jnp.tile(jnp.arange(GRID_W), rows).astype(F32)
    axis_dim = rot_dim // 2
    inv_freq = jnp.power(ROPE_THETA, -jnp.arange(0, axis_dim, 2, dtype=F32) / axis_dim)
    ang_r = row_idx[:, None] * inv_freq[None, :]
    ang_c = col_idx[:, None] * inv_freq[None, :]
    return jnp.cos(ang_r), jnp.sin(ang_r), jnp.cos(ang_c), jnp.sin(ang_c)


def _rotate(x, cos, sin):
    x1, x2 = jnp.split(x, 2, axis=-1)
    cos = cos[:, None, :]
    sin = sin[:, None, :]
    return jnp.concatenate([x1 * cos - x2 * sin, x1 * sin + x2 * cos], axis=-1)


def _apply_axial_rope(x, tables):
    cos_r, sin_r, cos_c, sin_c = tables
    x_row, x_col = jnp.split(x, 2, axis=-1)
    return jnp.concatenate([_rotate(x_row, cos_r, sin_r), _rotate(x_col, cos_c, sin_c)], axis=-1)


def _heads_first(t):
    return jnp.transpose(t, (0, 2, 1, 3))


def _gqa_group(q, k, v, q_norm_g, k_norm_g, rope, b, s):
    q = rms_norm(q, jnp.tile(q_norm_g, GQA_HEADS)[None, :], GQA_HEADS).reshape(b, s, GQA_HEADS, GQA_HEAD_DIM)
    k = rms_norm(k, jnp.tile(k_norm_g, GQA_KV_HEADS)[None, :], GQA_KV_HEADS).reshape(b, s, GQA_KV_HEADS, GQA_HEAD_DIM)
    v = v.reshape(b, s, GQA_KV_HEADS, GQA_HEAD_DIM)
    q = _apply_axial_rope(q, rope)
    k = _apply_axial_rope(k, rope)
    o = attention(_heads_first(q), _heads_first(k), _heads_first(v), GQA_HEAD_DIM ** -0.5)
    return _heads_first(o).reshape(b * s, GQA_WIDTH)


def _mla_group(c_q, c_kv, k_pe, q_norm_g, w_uq, kv_norm_g, w_ukv, rope, b, s):
    q = linear(rms_norm(c_q, q_norm_g[None, :], 1), w_uq).reshape(b, s, MLA_HEADS, MLA_NOPE_DIM + MLA_ROPE_DIM)
    q_nope, q_pe = q[..., :MLA_NOPE_DIM], q[..., MLA_NOPE_DIM:]
    kv = linear(rms_norm(c_kv, kv_norm_g[None, :], 1), w_ukv).reshape(b, s, MLA_HEADS, MLA_NOPE_DIM + MLA_V_DIM)
    k_nope, v = kv[..., :MLA_NOPE_DIM], kv[..., MLA_NOPE_DIM:]
    q_pe = _apply_axial_rope(q_pe, rope)
    k_pe = _apply_axial_rope(k_pe.reshape(b, s, 1, MLA_ROPE_DIM), rope)
    q = jnp.concatenate([q_nope, q_pe], axis=-1)
    k = jnp.concatenate([k_nope, jnp.broadcast_to(k_pe, (b, s, MLA_HEADS, MLA_ROPE_DIM))], axis=-1)
    o = attention(_heads_first(q), _heads_first(k), _heads_first(v), (MLA_NOPE_DIM + MLA_ROPE_DIM) ** -0.5)
    return _heads_first(o).reshape(b * s, MLA_HEADS * MLA_V_DIM)


def _ssd_group(z, xbc, dt_raw, conv_w, conv_b, dt_bias, a_log, d_skip, norm_g, b, s):
    wb = jnp.concatenate([conv_w, conv_b[None, :], jnp.zeros((CONV_PACK_ROWS - SSD_CONV - 1, SSD_CONV_DIM), F32)], axis=0)
    xbc = conv_silu(xbc.reshape(b, s, SSD_CONV_DIM), wb)
    xs = xbc[..., :SSD_INNER]
    bm = xbc[..., SSD_INNER:SSD_INNER + SSD_GROUPS * SSD_STATE]
    cm = xbc[..., SSD_INNER + SSD_GROUPS * SSD_STATE:]
    dt = jax.nn.softplus(dt_raw.reshape(b, s, 2, SSD_HEADS) + dt_bias)
    a_neg = -jnp.exp(a_log)
    y_fwd = ssd_chunked(xs, dt[:, :, 0], a_neg[0], bm, cm, False)
    y_bwd = ssd_chunked(xs, dt[:, :, 1], a_neg[1], bm, cm, True)
    y = y_fwd + y_bwd + xs * jnp.repeat(d_skip, SSD_HEAD_DIM)
    y = y.reshape(b * s, SSD_INNER) * jax.nn.silu(z)
    return rms_norm(y, norm_g[None, :], SSD_GROUPS)


MIXER_WEIGHTS = ('q_norm_g', 'k_norm_g', 'mla_q_norm_g', 'w_uq', 'mla_kv_norm_g', 'w_ukv', 'conv_w', 'conv_b',
                 'dt_bias', 'a_log', 'd_skip', 'ssd_norm_g')


def _mixer(proj, w, rope_a, rope_b, b, s):
    idx = np.cumsum(IN_SPLITS).tolist()
    q_a, k_a, v_a, cq_b, ckv_b, kpe_b, z_c, xbc_c, dt_c = [_in_cols(proj, lo, hi)
                                                           for lo, hi in zip([0] + idx[:-1], idx)]
    o_a = _gqa_group(q_a, k_a, v_a, w["q_norm_g"], w["k_norm_g"], rope_a, b, s)
    o_b = _mla_group(cq_b, ckv_b, kpe_b, w["mla_q_norm_g"], w["w_uq"], w["mla_kv_norm_g"], w["w_ukv"], rope_b, b, s)
    o_c = _ssd_group(z_c, xbc_c, dt_c, w["conv_w"], w["conv_b"], w["dt_bias"], w["a_log"], w["d_skip"],
                     w["ssd_norm_g"], b, s)
    return jnp.concatenate([o_a, o_b, o_c], axis=-1)


def _seq_tile(s, row_bytes):
    return _row_tile(s, row_bytes)


def _normmod_fwd(x, g, scale, shift):
    b, s, d = x.shape
    tr = _seq_tile(s, d * 4)

    def body(x_ref, g_ref, sc_ref, sh_ref, h_ref):
        xv = x_ref[0]
        r = lax.rsqrt(jnp.mean(xv * xv, axis=-1, keepdims=True) + EPS)
        h_ref[0] = (xv * r * g_ref[...] * (1.0 + sc_ref[0]) + sh_ref[0]).astype(BF16)

    act = pl.BlockSpec((1, tr, d), lambda bi, i: (bi, i, 0))
    vec = pl.BlockSpec((1, 1, d), lambda bi, i: (bi, 0, 0))
    return pl.pallas_call(
        body, name="normmod_fwd", grid=(b, s // tr),
        in_specs=[act, pl.BlockSpec((1, d), lambda bi, i: (0, 0)), vec, vec], out_specs=act,
        out_shape=jax.ShapeDtypeStruct((b, s, d), BF16),
        compiler_params=pltpu.CompilerParams(dimension_semantics=("parallel", "parallel")),
    )(x, g, scale, shift)


def _normmod_bwd(x, g, scale, dh, resid):
    b, s, d = x.shape
    tr = _seq_tile(s, d * 4)

    def body(x_ref, g_ref, sc_ref, dh_ref, res_ref, dx_ref, dg_ref, dsc_ref, dsh_ref):
        bi, i = pl.program_id(0), pl.program_id(1)

        @pl.when((bi == 0) & (i == 0))
        def _():
            dg_ref[...] = jnp.zeros_like(dg_ref)

        @pl.when(i == 0)
        def _():
            dsc_ref[...] = jnp.zeros_like(dsc_ref)
            dsh_ref[...] = jnp.zeros_like(dsh_ref)

        xv = x_ref[0]
        dhv = dh_ref[0]
        gv = g_ref[...]
        r = lax.rsqrt(jnp.mean(xv * xv, axis=-1, keepdims=True) + EPS)
        xhat = xv * r
        dsh_ref[0] += jnp.sum(dhv, axis=0, keepdims=True)
        dsc_ref[0] += jnp.sum(dhv * (xhat * gv), axis=0, keepdims=True)
        dn = dhv * (1.0 + sc_ref[0])
        dg_ref[...] += jnp.sum(dn * xhat, axis=0, keepdims=True)
        dxhat = dn * gv
        dx_ref[0] = r * (dxhat - xhat * jnp.mean(dxhat * xhat, axis=-1, keepdims=True)) + res_ref[0]

    act = pl.BlockSpec((1, tr, d), lambda bi, i: (bi, i, 0))
    vec = pl.BlockSpec((1, 1, d), lambda bi, i: (bi, 0, 0))
    gain = pl.BlockSpec((1, d), lambda bi, i: (0, 0))
    return pl.pallas_call(
        body, name="normmod_bwd", grid=(b, s // tr),
        in_specs=[act, gain, vec, act, act], out_specs=[act, gain, vec, vec],
        out_shape=[jax.ShapeDtypeStruct((b, s, d), F32), jax.ShapeDtypeStruct((1, d), F32),
                   jax.ShapeDtypeStruct((b, 1, d), F32), jax.ShapeDtypeStruct((b, 1, d), F32)],
        compiler_params=pltpu.CompilerParams(dimension_semantics=("arbitrary", "arbitrary")),
    )(x, g, scale, dh, resid)


def _gated_add(x, gate, t):
    b, s, d = x.shape
    tr = _seq_tile(s, d * 4)

    def body(x_ref, g_ref, t_ref, o_ref):
        o_ref[0] = x_ref[0] + g_ref[0] * t_ref[0]

    act = pl.BlockSpec((1, tr, d), lambda bi, i: (bi, i, 0))
    vec = pl.BlockSpec((1, 1, d), lambda bi, i: (bi, 0, 0))
    return pl.pallas_call(
        body, name="gated_add", grid=(b, s // tr), in_specs=[act, vec, act], out_specs=act,
        out_shape=jax.ShapeDtypeStruct((b, s, d), F32),
        compiler_params=pltpu.CompilerParams(dimension_semantics=("parallel", "parallel")),
    )(x, gate, t)


def _gated_bwd(dy, gate, t):
    b, s, d = dy.shape
    tr = _seq_tile(s, d * 4)

    def body(dy_ref, g_ref, t_ref, dt_ref, dgate_ref):
        @pl.when(pl.program_id(1) == 0)
        def _():
            dgate_ref[...] = jnp.zeros_like(dgate_ref)

        dyv = dy_ref[0]
        dt_ref[0] = (g_ref[0] * dyv).astype(BF16)
        dgate_ref[0] += jnp.sum(dyv * t_ref[0], axis=0, keepdims=True)

    act = pl.BlockSpec((1, tr, d), lambda bi, i: (bi, i, 0))
    vec = pl.BlockSpec((1, 1, d), lambda bi, i: (bi, 0, 0))
    return pl.pallas_call(
        body, name="gated_bwd", grid=(b, s // tr), in_specs=[act, vec, act], out_specs=[act, vec],
        out_shape=[jax.ShapeDtypeStruct((b, s, d), BF16), jax.ShapeDtypeStruct((b, 1, d), F32)],
        compiler_params=pltpu.CompilerParams(dimension_semantics=("parallel", "arbitrary")),
    )(dy, gate, t)


def _swiglu_fwd(gu, plan=None):
    rows, f2 = gu.shape
    f = f2 // 2
    tr = _row_tile(rows, f2 * 4, SWIGLU_TILE_BYTES)
    steps = rows // tr
    p_in, p_out, p_shapes, p_scratch, p_args = _plan_specs(plan)

    def body(gu_ref, *rest):
        a_ref = rest[len(p_in)]
        i = pl.program_id(0)

        def compute():
            gt = gu_ref[:, :f]
            a_ref[...] = (gt * jax.nn.sigmoid(gt) * gu_ref[:, f:]).astype(BF16)

        _ride(plan, rest[:len(p_in)] + rest[len(p_in) + 1:], i == 0, i == steps - 1, compute)

    outs = pl.pallas_call(
        body, name="swiglu_fwd", grid=(steps,),
        in_specs=[pl.BlockSpec((tr, f2), lambda i: (i, 0))] + p_in,
        out_specs=[pl.BlockSpec((tr, f), lambda i: (i, 0))] + p_out,
        out_shape=[jax.ShapeDtypeStruct((rows, f), BF16)] + p_shapes, scratch_shapes=p_scratch,
        compiler_params=pltpu.CompilerParams(dimension_semantics=("arbitrary" if plan is not None else "parallel",)),
    )(gu, *p_args)
    return outs[0] if plan is None else (outs[0], list(outs[1:]))


def _swiglu_bwd(gu, dact):
    rows, f2 = gu.shape
    f = f2 // 2
    tr = _row_tile(rows, f2 * 4, SWIGLU_TILE_BYTES)

    def body(gu_ref, da_ref, dgu_ref):
        gt = gu_ref[:, :f]
        up = gu_ref[:, f:]
        da = da_ref[...]
        sg = jax.nn.sigmoid(gt)
        dgu_ref[:, :f] = (da * up * (sg * (1.0 + gt * (1.0 - sg)))).astype(BF16)
        dgu_ref[:, f:] = (da * gt * sg).astype(BF16)

    return pl.pallas_call(
        body, name="swiglu_bwd", grid=(rows // tr,),
        in_specs=[pl.BlockSpec((tr, f2), lambda i: (i, 0)), pl.BlockSpec((tr, f), lambda i: (i, 0))],
        out_specs=pl.BlockSpec((tr, f2), lambda i: (i, 0)),
        out_shape=jax.ShapeDtypeStruct((rows, f2), BF16),
        compiler_params=pltpu.CompilerParams(dimension_semantics=("parallel",)),
    )(gu, dact)


class _Gathered:
    def __init__(self, shards):
        self.shards, self.full = shards, {}

    def plan(self, keys):
        return _gather_plan([self.shards[k] for k in keys])

    def store(self, keys, outs):
        for key, out in zip(keys, outs):
            name = key[0]
            g = out.reshape((N_CHIPS,) + self.shards[key].shape)
            if name in CHIP_BLOCKED:
                full = g
            elif name in COL_SHARDED:
                full = _cols_full(g).astype(F32)
            else:
                full = g.reshape(g.shape[0] * g.shape[1], g.shape[2])
            self.full[key] = full

    def carry(self, keys, fn):
        if not keys:
            return fn(None)
        res, outs = fn(self.plan(keys))
        self.store(keys, outs)
        return res


def _gather_schedule(depth):
    every = [(n, l) for l in range(depth) for n in ('w_uq', 'w_ukv')]
    sched = {'first': [('w_in', 0), ('w_gate_up', 0)] + every}
    for l in range(depth):
        sched[('w_in_fwd', l)] = [('w_out', l)] + ([('w_in', l + 1)] if l + 1 < depth else [])
        if l == 0:
            sched[('swiglu_fwd', l)] = [('w_down', l)]
        if l + 1 < depth:
            sched[('w_gate_up_fwd', l)] = [('w_gate_up', l + 1)]
            sched[('w_down_fwd', l)] = [('w_down', l + 1)]
    return sched


GATE_UP_PIECES = 4


class _Reducer:
    def __init__(self):
        self.parts, self.recv, self.result = {}, {}, {}

    def add(self, items):
        blocks = []
        for (name, _), grad in items:
            if name in CHIP_BLOCKED:
                blocks.append(grad)
            elif name in COL_SHARDED:
                blocks.append(_cols_split(grad))
            else:
                blocks.append(grad.reshape(N_CHIPS, grad.shape[0] // N_CHIPS, grad.shape[1]))
        for (key, _), parts in zip(items, _rs_parts(blocks)):
            self.parts[key] = parts
            self.recv[key] = []

    def pieces(self, key):
        rows = self.parts[key].shape[1]
        n = GATE_UP_PIECES if key[0] == 'w_gate_up' else 1
        return [(key, i * (rows // n), rows // n) for i in range(n)]

    def plan(self, jobs):
        return _chip_exchange_plan([(self.parts[key], row0, rows) for key, row0, rows in jobs])

    def store(self, jobs, outs):
        complete = []
        for (key, row0, rows), out in zip(jobs, outs):
            self.recv[key].append((row0, out))
            if len(self.recv[key]) == len(self.pieces(key)):
                complete.append(key)
        if complete:
            items = [(self.parts[key], sorted(self.recv[key], key=lambda t: t[0])) for key in complete]
            self.result.update(zip(complete, _rs_result(items)))

    def carry(self, keys, fn, piece=None):
        jobs = [j for key in keys for j in self.pieces(key)]
        if piece is not None:
            jobs = [j for key in keys for j in self.pieces(key)[piece:piece + 1]]
        if not jobs:
            return fn(None)
        res, outs = fn(self.plan(jobs))
        self.store(jobs, outs)
        return res

    def flush(self):
        jobs = [j for key in self.parts for j in self.pieces(key)
                if key not in self.result and j[1] not in [r for r, _ in self.recv[key]]]
        if jobs:
            self.store(jobs, _run_plan(self.plan(jobs), "rs_chip_exchange"))


def _layer_fwd(x, mod, w, gathered, l, sched, rope_a, rope_b):
    b, s, d = x.shape
    m = b * s
    shift1, scale1, gate1, shift2, scale2, gate2 = [t[:, None, :] for t in jnp.split(mod, 6, axis=-1)]
    g1, g2 = w["norm1_g"][None, :], w["norm2_g"][None, :]
    full = lambda n: gathered.full[(n, l)]
    h1 = _normmod_fwd(x, g1, scale1, shift1).reshape(m, d)
    proj = gathered.carry(sched.get(('w_in_fwd', l)), lambda p: _matmul(h1, full('w_in'), name="w_in_fwd", plan=p, chips='b'))
    mixer_w = {n: (full(n) if n in COL_SHARDED else w[n]) for n in MIXER_WEIGHTS}
    o, mixer_vjp = jax.vjp(lambda p, mw: _mixer(p, mw, rope_a, rope_b, b, s), proj, mixer_w)
    o = o.astype(BF16)
    mix = _matmul(o, full('w_out'), name="w_out_fwd").reshape(b, s, d)
    x_mid = _gated_add(x, gate1, mix)
    h2 = _normmod_fwd(x_mid, g2, scale2, shift2).reshape(m, d)
    gu = gathered.carry(sched.get(('w_gate_up_fwd', l)),
                        lambda p: _matmul(h2, full('w_gate_up'), name="w_gate_up_fwd", plan=p, chips='b'))
    act = gathered.carry(sched.get(('swiglu_fwd', l)), lambda p: _swiglu_fwd(gu, plan=p))
    ffn = gathered.carry(sched.get(('w_down_fwd', l)), lambda p: _matmul(act, full('w_down'), name="w_down_fwd", plan=p))
    ffn = ffn.reshape(b, s, d)
    x_out = _gated_add(x_mid, gate2, ffn)
    res = (x, x_mid, h1, h2, o, mix, gu, act, ffn, mixer_vjp, scale1, gate1, scale2, gate2, g1, g2)
    return x_out, res


def _layer_bwd(res, gathered, reducer, l, depth, dx_out):
    x, x_mid, h1, h2, o, mix, gu, act, ffn, mixer_vjp, scale1, gate1, scale2, gate2, g1, g2 = res
    b, s, d = x.shape
    m = b * s
    full = lambda n: gathered.full[(n, l)]
    above = l + 1 < depth
    dffn, dgate2 = _gated_bwd(dx_out, gate2, ffn)
    dffn = dffn.reshape(m, d)
    dact = reducer.carry([('w_out', l + 1), ('w_uq', l + 1), ('w_ukv', l + 1)] if above else [],
                         lambda p: _matmul(dffn, full('w_down'), tb=True, name="w_down_dx", plan=p))
    dw = reducer.carry([('w_in', l + 1)] if above else [],
                       lambda p: _matmul(act, dffn, ta=True, name="w_down_dw", plan=p))
    reducer.add([(('w_down', l), dw)])
    dgu = _swiglu_bwd(gu, dact)
    dh2 = reducer.carry([('w_down', l)], lambda p: _matmul(dgu, full('w_gate_up'), tb=True, name="w_gate_up_dx", plan=p,
                                                           chips='b'))
    dh2 = dh2.reshape(b, s, d)
    reducer.add([(('w_gate_up', l), _matmul(h2, dgu, ta=True, name="w_gate_up_dw", chips='out'))])
    dx_mid, dg2, dscale2, dshift2 = _normmod_bwd(x_mid, g2, scale2, dh2, dx_out)
    dmix, dgate1 = _gated_bwd(dx_mid, gate1, mix)
    dmix = dmix.reshape(m, d)
    gate_up = [('w_gate_up', l)]
    do = reducer.carry(gate_up, lambda p: _matmul(dmix, full('w_out'), tb=True, name="w_out_dx", plan=p), piece=0)
    dw_out = reducer.carry(gate_up, lambda p: _matmul(o, dmix, ta=True, name="w_out_dw", plan=p), piece=1)
    dproj, grads = mixer_vjp(do)
    grads = dict(grads)
    reducer.add([(('w_out', l), dw_out), (('w_uq', l), grads.pop('w_uq')), (('w_ukv', l), grads.pop('w_ukv'))])
    dproj = dproj.astype(BF16)
    dh1 = reducer.carry(gate_up, lambda p: _matmul(dproj, full('w_in'), tb=True, name="w_in_dx", plan=p, chips='b'),
                        piece=2)
    dh1 = dh1.reshape(b, s, d)
    dw = reducer.carry(gate_up, lambda p: _matmul(h1, dproj, ta=True, name="w_in_dw", plan=p, chips='out'), piece=3)
    reducer.add([(('w_in', l), dw)])
    dx, dg1, dscale1, dshift1 = _normmod_bwd(x, g1, scale1, dh1, dx_mid)
    grads["norm1_g"], grads["norm2_g"] = dg1[0], dg2[0]
    dmod = jnp.concatenate([dshift1, dscale1, dgate1, dshift2, dscale2, dgate2], axis=-1)[:, 0, :]
    return dx, dmod, grads


def _tail_loss(x2, final_norm_g, target2):
    return loss_head(rms_norm(x2, final_norm_g[None, :], 1), target2)


def _forward_backward(x, mod, small, gathered, reducer, final_norm_g, target):
    b, s, d = x.shape
    depth = len(small)
    rope_a = _axial_rope_tables(s, GQA_HEAD_DIM)
    rope_b = _axial_rope_tables(s, MLA_ROPE_DIM)
    sched = _gather_schedule(depth)
    first = sched['first']
    gathered.store(first, _run_plan(gathered.plan(first), "all_gather_chips"))
    saved = []
    for l in range(depth):
        x, res = _layer_fwd(x, mod[l], small[l], gathered, l, sched, rope_a, rope_b)
        saved.append(res)
    loss, (dx2, dfinal) = jax.value_and_grad(_tail_loss, argnums=(0, 1))(
        x.reshape(b * s, d), final_norm_g, target.reshape(b * s, d))
    dx = dx2.reshape(b, s, d)
    dmods, gsmall = [None] * depth, [None] * depth
    for l in reversed(range(depth)):
        dx, dmods[l], gsmall[l] = _layer_bwd(saved[l], gathered, reducer, l, depth, dx)
    return loss, dx, jnp.stack(dmods), gsmall, dfinal


ANY = pl.BlockSpec(memory_space=pl.ANY)


def _flip_if(v, bit):
    return 1 - v if bit else v


def _all_gather_devices(x):
    def body(x_ref, out_ref, send_sems, recv_sems):
        mx, my, mc = lax.axis_index("x"), lax.axis_index("y"), lax.axis_index("c")
        me = 4 * mx + 2 * my + mc
        sends = []
        for k in range(1, N_DEV):
            peer = (_flip_if(mx, k & 4), _flip_if(my, k & 2), _flip_if(mc, k & 1))
            cp = pltpu.make_async_remote_copy(src_ref=x_ref, dst_ref=out_ref.at[me], send_sem=send_sems.at[k - 1],
                                              recv_sem=recv_sems.at[k - 1], device_id=peer, device_id_type=MESH)
            cp.start()
            sends.append(cp)
        for k in range(1, N_DEV):
            peer = (_flip_if(mx, k & 4), _flip_if(my, k & 2), _flip_if(mc, k & 1))
            src = 4 * peer[0] + 2 * peer[1] + peer[2]
            pltpu.make_async_remote_copy(src_ref=x_ref, dst_ref=out_ref.at[src], send_sem=send_sems.at[k - 1],
                                         recv_sem=recv_sems.at[k - 1], device_id=peer, device_id_type=MESH).wait_recv()
        for cp in sends:
            cp.wait_send()

    out = pl.pallas_call(
        body, name="all_gather_devices", in_specs=[ANY], out_specs=ANY,
        out_shape=jax.ShapeDtypeStruct((N_DEV,) + x.shape, x.dtype),
        scratch_shapes=[pltpu.SemaphoreType.DMA((N_DEV - 1,)), pltpu.SemaphoreType.DMA((N_DEV - 1,))],
    )(x)
    me = 4 * lax.axis_index("x") + 2 * lax.axis_index("y") + lax.axis_index("c")
    return lax.dynamic_update_index_in_dim(out, x, me, 0)


class _Plan:
    def __init__(self, inputs, out_shapes, sem_counts, start, finish):
        self.inputs, self.out_shapes, self.sem_counts = list(inputs), list(out_shapes), list(sem_counts)
        self.start, self.finish = start, finish

    def specs(self):
        return ([ANY] * len(self.inputs), [ANY] * len(self.out_shapes),
                [pltpu.SemaphoreType.DMA((c,)) for c in self.sem_counts])

    def split(self, refs):
        a, b = len(self.inputs), len(self.inputs) + len(self.out_shapes)
        return refs[:a], refs[a:b], refs[b:]


def _run_plan(plan, name):
    def body(*refs):
        ins, outs, sems = plan.split(refs)
        plan.start(ins, outs, sems)
        plan.finish(ins, outs, sems)

    in_specs, out_specs, scratch = plan.specs()
    return pl.pallas_call(body, name=name, in_specs=in_specs, out_specs=out_specs, out_shape=plan.out_shapes,
                          scratch_shapes=scratch)(*plan.inputs)


def _gather_plan(shards):
    n = len(shards)
    halves = [t.reshape(2, t.shape[0] // 2, t.shape[1]) for t in shards]
    count = (N_CHIPS - 1) * n

    def copies(kind, ins, outs, sems):
        ici_send, ici_recv, d2d_send, d2d_recv, own_send, own_recv = sems
        mx, my, mc = lax.axis_index("x"), lax.axis_index("y"), lax.axis_index("c")
        me = 2 * mx + my
        sibling = (mx, my, 1 - mc)
        if kind == 'own':
            return [pltpu.make_async_remote_copy(src_ref=ins[i], dst_ref=outs[i].at[me], send_sem=own_send.at[i],
                                                 recv_sem=own_recv.at[i], device_id=sibling, device_id_type=MESH)
                    for i in range(n)]
        cps = []
        for k in range(1, N_CHIPS):
            peer = (_flip_if(mx, k & 2), _flip_if(my, k & 1), mc)
            src = 2 * peer[0] + peer[1]
            for i in range(n):
                j = (k - 1) * n + i
                if kind in ('ici', 'landed'):
                    dst = outs[i].at[me, mc] if kind == 'ici' else outs[i].at[src, mc]
                    cps.append(pltpu.make_async_remote_copy(
                        src_ref=ins[i].at[mc], dst_ref=dst, send_sem=ici_send.at[j], recv_sem=ici_recv.at[j],
                        device_id=peer, device_id_type=MESH))
                else:
                    half = outs[i].at[src, mc] if kind == 'fwd' else outs[i].at[src, 1 - mc]
                    cps.append(pltpu.make_async_remote_copy(
                        src_ref=half, dst_ref=half, send_sem=d2d_send.at[j], recv_sem=d2d_recv.at[j],
                        device_id=sibling, device_id_type=MESH))
        return cps

    def start(ins, outs, sems):
        for cp in copies('own', ins, outs, sems) + copies('ici', ins, outs, sems):
            cp.start()

    def finish(ins, outs, sems):
        fwd = copies('fwd', ins, outs, sems)
        for arrived, onward in zip(copies('landed', ins, outs, sems), fwd):
            arrived.wait_recv()
            onward.start()
        own = copies('own', ins, outs, sems)
        for cp in copies('fwd_in', ins, outs, sems) + own:
            cp.wait_recv()
        for cp in own + copies('ici', ins, outs, sems) + fwd:
            cp.wait_send()

    out_shapes = [jax.ShapeDtypeStruct((N_CHIPS,) + t.shape, t.dtype) for t in halves]
    return _Plan(halves, out_shapes, [count] * 4 + [n] * 2, start, finish)


def _sibling_exchange(blocks, name):
    n = len(blocks)

    def body(*refs):
        ins, outs = refs[:n], refs[n:2 * n]
        send_sems, recv_sems = refs[2 * n:]
        mx, my, mc = lax.axis_index("x"), lax.axis_index("y"), lax.axis_index("c")
        cps = []
        for i in range(n):
            cp = pltpu.make_async_remote_copy(src_ref=ins[i], dst_ref=outs[i], send_sem=send_sems.at[i],
                                              recv_sem=recv_sems.at[i], device_id=(mx, my, 1 - mc),
                                              device_id_type=MESH)
            cp.start()
            cps.append(cp)
        for cp in cps:
            cp.wait()

    return pl.pallas_call(
        body, name=name, in_specs=[ANY] * n, out_specs=[ANY] * n,
        out_shape=[jax.ShapeDtypeStruct(t.shape, t.dtype) for t in blocks],
        scratch_shapes=[pltpu.SemaphoreType.DMA((n,)), pltpu.SemaphoreType.DMA((n,))],
    )(*blocks)


def _add_halves(own, recv):
    nb, r, c = own.shape
    tr = _row_tile(r, c * 4)

    def body(g_ref, r_ref, o_ref):
        o_ref[...] = (g_ref[...] + r_ref[...].astype(F32)).astype(BF16)

    spec = pl.BlockSpec((1, tr, c), lambda k, i: (k, i, 0))
    return pl.pallas_call(
        body, name="rs_add_halves", grid=(nb, r // tr), in_specs=[spec, spec], out_specs=spec,
        out_shape=jax.ShapeDtypeStruct((nb, r, c), BF16),
        compiler_params=pltpu.CompilerParams(dimension_semantics=("parallel", "parallel")),
    )(own, recv)


def _chip_exchange_plan(jobs):
    n = len(jobs)
    count = (N_CHIPS - 1) * n

    def copies(ins, outs, sems):
        send_sems, recv_sems = sems
        mx, my, mc = lax.axis_index("x"), lax.axis_index("y"), lax.axis_index("c")
        cps = []
        for k in range(1, N_CHIPS):
            peer = (_flip_if(mx, k & 2), _flip_if(my, k & 1), mc)
            dst_chip = 2 * peer[0] + peer[1]
            for i, (_, row0, rows) in enumerate(jobs):
                j = (k - 1) * n + i
                cps.append(pltpu.make_async_remote_copy(
                    src_ref=ins[i].at[dst_chip, pl.ds(row0, rows)], dst_ref=outs[i].at[k - 1],
                    send_sem=send_sems.at[j], recv_sem=recv_sems.at[j], device_id=peer, device_id_type=MESH))
        return cps

    def start(ins, outs, sems):
        for cp in copies(ins, outs, sems):
            cp.start()

    def finish(ins, outs, sems):
        for cp in copies(ins, outs, sems):
            cp.wait()

    out_shapes = [jax.ShapeDtypeStruct((N_CHIPS - 1, rows, p.shape[2]), p.dtype) for p, _, rows in jobs]
    return _Plan([p for p, _, _ in jobs], out_shapes, [count, count], start, finish)


def _sum_chips(parts, recv, chip, row0):
    _, rows, c = recv.shape
    tr = _row_tile(rows, c * 4)
    assert row0 % tr == 0

    def body(chip_ref, p_ref, r_ref, o_ref):
        acc = p_ref[0].astype(F32)
        for k in range(N_CHIPS - 1):
            acc = acc + r_ref[k].astype(F32)
        o_ref[...] = acc

    return pl.pallas_call(
        body, name="rs_sum_chips",
        grid_spec=pltpu.PrefetchScalarGridSpec(
            num_scalar_prefetch=1, grid=(rows // tr,),
            in_specs=[pl.BlockSpec((1, tr, c), lambda i, chip_ref: (chip_ref[0], i + row0 // tr, 0)),
                      pl.BlockSpec((N_CHIPS - 1, tr, c), lambda i, chip_ref: (0, i, 0))],
            out_specs=pl.BlockSpec((tr, c), lambda i, chip_ref: (i, 0))),
        out_shape=jax.ShapeDtypeStruct((rows, c), F32),
        compiler_params=pltpu.CompilerParams(dimension_semantics=("parallel",)),
    )(chip, parts, recv)


def _sum_leading(t, name):
    nb, r, c = t.shape
    tr = _row_tile(r, c * 4 * nb)

    def body(t_ref, o_ref):
        acc = t_ref[0]
        for k in range(1, nb):
            acc = acc + t_ref[k]
        o_ref[...] = acc

    return pl.pallas_call(
        body, name=name, grid=(r // tr,),
        in_specs=[pl.BlockSpec((nb, tr, c), lambda i: (0, i, 0))],
        out_specs=pl.BlockSpec((tr, c), lambda i: (i, 0)),
        out_shape=jax.ShapeDtypeStruct((r, c), F32),
        compiler_params=pltpu.CompilerParams(dimension_semantics=("parallel",)),
    )(t)


def _rs_parts(grads):
    mc = lax.axis_index("c")
    split = [g.reshape(g.shape[0], 2, g.shape[1] // 2, g.shape[2]) for g in grads]
    own = [lax.dynamic_index_in_dim(g, mc, axis=1, keepdims=False) for g in split]
    away = [lax.dynamic_index_in_dim(g, 1 - mc, axis=1, keepdims=False).astype(BF16) for g in split]
    return [_add_halves(o, r) for o, r in zip(own, _sibling_exchange(away, "rs_sibling_exchange"))]


def _rs_result(items):
    mc = lax.axis_index("c")
    chip = (2 * lax.axis_index("x") + lax.axis_index("y")).astype(jnp.int32).reshape(1)
    mine = []
    for parts, pieces in items:
        done = [_sum_chips(parts, recv, chip, row0) for row0, recv in pieces]
        mine.append(done[0] if len(done) == 1 else jnp.concatenate(done, axis=0))
    theirs = _sibling_exchange(mine, "rs_sibling_swap")
    return [jnp.concatenate([jnp.where(mc == 0, a, b), jnp.where(mc == 0, b, a)], axis=0)
            for a, b in zip(mine, theirs)]


def _adamw(w, g, m, v, plan=None):
    shape = w.shape
    cols = shape[-1]
    if len(shape) == 3:
        lead, rows = shape[0], shape[1]
    else:
        lead, rows = 1, (int(np.prod(shape[:-1])) if len(shape) > 1 else 1)
    w2, g2, m2, v2 = [t.reshape(lead, rows, cols) for t in (w, g, m, v)]
    tr = _row_tile(rows, cols * 4, ADAM_TILE_BYTES)
    per = rows // tr
    steps = lead * per
    p_in, p_out, p_shapes, p_scratch, p_args = _plan_specs(plan)

    def body(w_ref, g_ref, m_ref, v_ref, *rest):
        d_ref, mo_ref, vo_ref = rest[len(p_in):len(p_in) + 3]
        i = pl.program_id(0) * per + pl.program_id(1)

        def compute():
            gv = g_ref[...]
            mn = ADAM_B1 * m_ref[...] + (1.0 - ADAM_B1) * gv
            vn = ADAM_B2 * v_ref[...] + (1.0 - ADAM_B2) * (gv * gv)
            m_hat = mn / (1.0 - ADAM_B1 ** ADAM_STEP)
            v_hat = vn / (1.0 - ADAM_B2 ** ADAM_STEP)
            d_ref[...] = -ADAM_LR * (m_hat / (jnp.sqrt(v_hat) + ADAM_EPS) + ADAM_WD * w_ref[...])
            mo_ref[...] = mn
            vo_ref[...] = vn

        _ride(plan, rest[:len(p_in)] + rest[len(p_in) + 3:], i == 0, i == steps - 1, compute)

    spec = pl.BlockSpec((1, tr, cols), lambda a, i: (a, i, 0))
    sem = "arbitrary" if plan is not None else "parallel"
    outs = pl.pallas_call(
        body, name="adamw", grid=(lead, per), in_specs=[spec] * 4 + p_in, out_specs=[spec] * 3 + p_out,
        out_shape=[jax.ShapeDtypeStruct((lead, rows, cols), F32)] * 3 + p_shapes, scratch_shapes=p_scratch,
        compiler_params=pltpu.CompilerParams(dimension_semantics=(sem, sem), vmem_limit_bytes=ADAM_VMEM_LIMIT),
    )(w2, g2, m2, v2, *p_args)
    res = [t.reshape(shape) for t in outs[:3]]
    return res if plan is None else (res, list(outs[3:]))


WEIGHTS = ['w_ada', 'b_ada', 'norm1_g', 'norm2_g', 'w_in', 'q_norm_g', 'k_norm_g', 'mla_q_norm_g', 'w_uq',
           'mla_kv_norm_g', 'w_ukv', 'conv_w', 'conv_b', 'dt_bias', 'a_log', 'd_skip', 'ssd_norm_g', 'w_out',
           'w_gate_up', 'w_down', 'final_norm_g']
COL_SHARDED = ('w_in', 'w_uq', 'w_ukv', 'w_gate_up')
ROW_SHARDED = ('w_out', 'w_down')
CHIP_BLOCKED = ('w_in', 'w_gate_up')
SMALL_LAYER = ('norm1_g', 'norm2_g', 'q_norm_g', 'k_norm_g', 'mla_q_norm_g', 'mla_kv_norm_g', 'conv_w', 'conv_b',
               'dt_bias', 'a_log', 'd_skip', 'ssd_norm_g')


def _pack(parts):
    flat = jnp.concatenate([p.reshape(-1) for p in parts])
    n = flat.shape[0]
    rows = -(-n // (8 * LANES)) * 8
    return jnp.pad(flat, (0, rows * LANES - n)).reshape(rows, LANES)


def _unpack(flat, shapes):
    out, pos = [], 0
    for shp in shapes:
        size = int(np.prod(shp))
        out.append(flat[pos:pos + size].reshape(shp))
        pos += size
    return out


def _cols_full(gathered):
    k, r, c = gathered.shape
    return jnp.transpose(gathered, (1, 0, 2)).reshape(r, k * c)


def _cols_split(full):
    r, c4 = full.shape
    return jnp.transpose(full.reshape(r, N_CHIPS, c4 // N_CHIPS), (1, 0, 2))


def kernel(x, c, w_ada, b_ada, norm1_g, norm2_g, w_in, q_norm_g, k_norm_g, mla_q_norm_g, w_uq, mla_kv_norm_g, w_ukv, conv_w, conv_b, dt_bias, a_log, d_skip, ssd_norm_g, w_out, w_gate_up, w_down, final_norm_g, loss_target, m_w_ada, m_b_ada, m_norm1_g, m_norm2_g, m_w_in, m_q_norm_g, m_k_norm_g, m_mla_q_norm_g, m_w_uq, m_mla_kv_norm_g, m_w_ukv, m_conv_w, m_conv_b, m_dt_bias, m_a_log, m_d_skip, m_ssd_norm_g, m_w_out, m_w_gate_up, m_w_down, m_final_norm_g, v_w_ada, v_b_ada, v_norm1_g, v_norm2_g, v_w_in, v_q_norm_g, v_k_norm_g, v_mla_q_norm_g, v_w_uq, v_mla_kv_norm_g, v_w_ukv, v_conv_w, v_conv_b, v_dt_bias, v_a_log, v_d_skip, v_ssd_norm_g, v_w_out, v_w_gate_up, v_w_down, v_final_norm_g):
    args = dict(locals())
    weights = {n: args[n] for n in WEIGHTS}
    depth = w_in.shape[0]
    bl, s, d = x.shape
    mx, my, mc = lax.axis_index("x"), lax.axis_index("y"), lax.axis_index("c")
    chip = 2 * mx + my
    dev = 2 * chip + mc
    ada_cols = w_ada.shape[-1]
    conv_cols = conv_w.shape[-1]

    first_shapes = [c.shape, conv_w.shape]
    first = _all_gather_devices(_pack([c, conv_w]))
    first = [_unpack(first[i].reshape(-1), first_shapes) for i in range(N_DEV)]
    c_act = jax.nn.silu(jnp.concatenate([f[0] for f in first], axis=0))
    conv_w_full = jnp.concatenate([first[2 * k][1] for k in range(N_CHIPS)], axis=-1)

    b_cols = lax.dynamic_slice_in_dim(b_ada, chip * ada_cols, ada_cols, axis=1)
    c_act_b = c_act.astype(BF16)
    mod_cols = jnp.stack([_matmul(c_act_b, w_ada[l], name="ada_fwd") + b_cols[l][None, :]
                          for l in range(depth)])
    mod_all = _all_gather_devices(mod_cols.reshape(depth * N_DEV * bl, ada_cols))
    mod_all = mod_all.reshape(N_DEV, depth, N_DEV, bl, ada_cols)
    mod_mine = lax.dynamic_index_in_dim(mod_all, dev, axis=2, keepdims=False)
    mod = jnp.concatenate([mod_mine[2 * k] for k in range(N_CHIPS)], axis=-1)

    big = COL_SHARDED + ROW_SHARDED
    shards = {(n, l): weights[n][l].astype(BF16) for n in big for l in range(depth)}
    for l in range(depth):
        shards[('w_in', l)] = jnp.pad(shards[('w_in', l)], ((0, 0), (0, IN_SHARD_PAD - IN_SHARD)))
    gathered = _Gathered(shards)
    reducer = _Reducer()
    small_w = []
    for l in range(depth):
        w = {n: weights[n][l] for n in SMALL_LAYER if n != 'conv_w'}
        w['conv_w'] = conv_w_full[l]
        small_w.append(w)
    loss_local, gx, gmod, glayers, gfinal = _forward_backward(x, mod, small_w, gathered, reducer, final_norm_g,
                                                              loss_target)

    small_parts = [jnp.stack([glayers[l][n] for l in range(depth)]) for n in SMALL_LAYER]
    small_parts += [gfinal, loss_local.reshape(1), gmod]
    small_shapes = [p.shape for p in small_parts]
    last = _all_gather_devices(_pack(small_parts))
    summed = _unpack(_sum_leading(last, "sum_devices").reshape(-1), small_shapes)
    small = dict(zip(SMALL_LAYER, summed[:len(SMALL_LAYER)]))
    g_final, loss, gmod_sum = summed[len(SMALL_LAYER):]
    small['conv_w'] = lax.dynamic_slice_in_dim(small['conv_w'], chip * conv_cols, conv_cols, axis=2)
    gmod_all = jnp.stack([_unpack(last[i].reshape(-1), small_shapes)[-1] for i in range(N_DEV)], axis=1)
    gmod_all = gmod_all.reshape(depth, N_DEV * bl, gmod.shape[-1])
    gmod_cols = lax.dynamic_slice_in_dim(gmod_all, chip * ada_cols, ada_cols, axis=2)
    g_w_ada = jnp.stack([_matmul(c_act_b, gmod_cols[l].astype(BF16), ta=True, name="ada_dw") for l in range(depth)])
    g_b_ada = gmod_sum[:, 0]
    for i in range(1, bl):
        g_b_ada = g_b_ada + gmod_sum[:, i]

    grad = {'w_ada': g_w_ada, 'b_ada': g_b_ada, 'final_norm_g': g_final}
    for n in SMALL_LAYER:
        grad[n] = small[n]

    delta, new_m, new_v = {}, {}, {}
    left = [key for key in reducer.parts if key not in reducer.result and not reducer.recv[key]]
    delta['w_ada'], new_m['w_ada'], new_v['w_ada'] = reducer.carry(
        left, lambda p: _adamw(w_ada, g_w_ada, m_w_ada, v_w_ada, plan=p))
    reducer.flush()
    for n in big:
        grad[n] = jnp.stack([reducer.result[(n, l)] for l in range(depth)])[..., :weights[n].shape[-1]]
    for n in WEIGHTS:
        if n != 'w_ada':
            delta[n], new_m[n], new_v[n] = _adamw(weights[n], grad[n], args["m_" + n], args["v_" + n])
    return (loss.reshape(()), gx, *[grad[n] for n in WEIGHTS], *[delta[n] for n in WEIGHTS],
            *[new_m[n] for n in WEIGHTS], *[new_v[n] for n in WEIGHTS])
```

```python
import functools

import numpy as np
import jax
import jax.numpy as jnp
from jax import lax
from jax.experimental import pallas as pl
from jax.experimental.pallas import tpu as pltpu

F32 = jnp.float32
BF16 = jnp.bfloat16
HIGHEST = lax.Precision.HIGHEST
MESH = pl.DeviceIdType.MESH

GRID_W = 64
ROPE_THETA = 10000.0
EPS = 1e-6

GQA_HEADS, GQA_KV_HEADS, GQA_HEAD_DIM = 6, 2, 128
GQA_WIDTH = GQA_HEADS * GQA_HEAD_DIM
GQA_KV_WIDTH = GQA_KV_HEADS * GQA_HEAD_DIM
MLA_HEADS, MLA_Q_LORA, MLA_KV_LORA = 4, 512, 256
MLA_NOPE_DIM, MLA_ROPE_DIM, MLA_V_DIM = 128, 64, 128
SSD_HEADS, SSD_HEAD_DIM, SSD_GROUPS, SSD_STATE, SSD_CONV, SSD_CHUNK = 12, 64, 2, 128, 5, 128
SSD_INNER = SSD_HEADS * SSD_HEAD_DIM
SSD_CONV_DIM = SSD_INNER + 2 * SSD_GROUPS * SSD_STATE
SSD_GROUP_HEADS = SSD_HEADS // SSD_GROUPS
SSD_GROUP_WIDTH = SSD_GROUP_HEADS * SSD_HEAD_DIM
IN_SPLITS = (GQA_WIDTH, GQA_KV_WIDTH, GQA_KV_WIDTH, MLA_Q_LORA, MLA_KV_LORA, MLA_ROPE_DIM, SSD_INNER, SSD_CONV_DIM,
             2 * SSD_HEADS)
IN_COLS = sum(IN_SPLITS)
LANES = 128
N_CHIPS = 4
IN_SHARD = IN_COLS // N_CHIPS
IN_SHARD_PAD = -(-IN_SHARD // LANES) * LANES


def _in_cols(proj, lo, hi):
    parts = []
    for chip in range(lo // IN_SHARD, (hi - 1) // IN_SHARD + 1):
        a, z = max(lo, chip * IN_SHARD), min(hi, (chip + 1) * IN_SHARD)
        base = chip * IN_SHARD_PAD - chip * IN_SHARD
        parts.append(proj[:, base + a:base + z])
    return parts[0] if len(parts) == 1 else jnp.concatenate(parts, axis=-1)

ADAM_LR, ADAM_B1, ADAM_B2, ADAM_EPS, ADAM_WD, ADAM_STEP = 0.001, 0.9, 0.999, 1e-08, 0.01, 10

N_DEV = 8
TILE_BYTES = 2 * 1024 * 1024


def _pick(n, cands):
    for t in cands:
        if n % t == 0:
            return t
    return n


ADAM_TILE_BYTES = 3 * 512 * 1024
ADAM_VMEM_LIMIT = 40 * 1024 * 1024
SWIGLU_TILE_BYTES = 4 * 1024 * 1024


def _row_tile(rows, row_bytes, limit=TILE_BYTES):
    for t in (2048, 1024, 512, 256, 128, 64, 32, 16, 8):
        if rows % t == 0 and t * row_bytes <= limit:
            return t
    return rows


MM_VMEM_BUDGET = 36 * 1024 * 1024
MM_VMEM_LIMIT = 56 * 1024 * 1024
MM_MAX_TILE = 2048
MM_MAX_K_TILE = 4096
MXU_DIM = 256
HBM_BYTES_PER_US = 3.0e6
MXU_FLOPS_PER_US = 9.0e8
STEP_US = 0.35


def _tile_cands(d, cap):
    if d % LANES:
        return [d]
    return [t for t in range(LANES, min(d, cap) + 1, LANES) if d % t == 0] or [d]


def _mm_tiles(m, n, kdim, n_unit=None, k_unit=None):
    up = lambda t: -(-t // MXU_DIM) * MXU_DIM
    best = None
    for tm in _tile_cands(m, MM_MAX_TILE):
        for tn in _tile_cands(n_unit or n, MM_MAX_TILE):
            for tk in _tile_cands(k_unit or kdim, MM_MAX_K_TILE):
                if 2 * (tm * tk * 2 + tk * tn * 2 + tm * tn * 4) > MM_VMEM_BUDGET:
                    continue
                ni, nj, nk = m // tm, n // tn, kdim // tk
                a_reads = 1 if nk == 1 else nj
                b_reads = 1 if (nk == 1 and nj == 1) else ni
                hbm = (m * kdim * 2 * a_reads + kdim * n * 2 * b_reads + m * n * 4) / HBM_BYTES_PER_US
                mxu = ni * nj * nk * 2.0 * max(tm, 8) * up(tn) * up(tk) / MXU_FLOPS_PER_US
                cost = max(hbm, mxu) + 0.25 * min(hbm, mxu) + ni * nj * nk * STEP_US
                if best is None or cost < best[0]:
                    best = (cost, tm, tn, tk)
    return best[1:]


def _ride(plan, refs, first, last, compute):
    if plan is None:
        compute()
        return
    ins, outs, sems = plan.split(refs)

    @pl.when(first)
    def _():
        plan.start(ins, outs, sems)

    compute()

    @pl.when(last)
    def _():
        plan.finish(ins, outs, sems)


def _plan_specs(plan):
    if plan is None:
        return [], [], [], [], []
    in_specs, out_specs, scratch = plan.specs()
    return in_specs, out_specs, plan.out_shapes, scratch, plan.inputs


def _matmul(a, b, ta=False, tb=False, name="mm", plan=None, chips=None):
    assert a.dtype == BF16 and b.dtype in (BF16, F32), (a.dtype, b.dtype)
    if ta:
        kdim, m = a.shape
    else:
        m, kdim = a.shape
    n_unit = k_unit = None
    if chips == 'b':
        nb, rows, unit = b.shape
        if tb:
            n, k2, k_unit = rows, nb * unit, unit
        else:
            k2, n, n_unit = rows, nb * unit, unit
    else:
        if tb:
            n, k2 = b.shape
        else:
            k2, n = b.shape
        if chips == 'out':
            n_unit = n // N_CHIPS
    assert kdim == k2, (a.shape, b.shape, ta, tb)
    tm, tn, tk = _mm_tiles(m, n, kdim, n_unit, k_unit)
    ni, nj, nk = m // tm, n // tn, kdim // tk
    dn = (((0 if ta else 1,), (1 if tb else 0,)), ((), ()))
    p_in, p_out, p_shapes, p_scratch, p_args = _plan_specs(plan)

    def body(a_ref, b_ref, *rest):
        o_ref = rest[len(p_in)]
        i, j, k = pl.program_id(0), pl.program_id(1), pl.program_id(2)

        def compute():
            bv = (b_ref[0] if chips == 'b' else b_ref[...]).astype(BF16)
            part = lax.dot_general(a_ref[...], bv, dn, preferred_element_type=F32)
            if chips == 'out':
                part = part[None]
            if nk == 1:
                o_ref[...] = part
            else:
                @pl.when(k == 0)
                def _():
                    o_ref[...] = part

                @pl.when(k > 0)
                def _():
                    o_ref[...] += part

        _ride(plan, rest[:len(p_in)] + rest[len(p_in) + 1:], (i == 0) & (j == 0) & (k == 0),
              (i == ni - 1) & (j == nj - 1) & (k == nk - 1), compute)

    a_spec = pl.BlockSpec((tk, tm), lambda i, j, k: (k, i)) if ta else pl.BlockSpec((tm, tk), lambda i, j, k: (i, k))
    if chips == 'b' and tb:
        per = k_unit // tk
        b_spec = pl.BlockSpec((1, tn, tk), lambda i, j, k: (k // per, j, k % per))
    elif chips == 'b':
        per = n_unit // tn
        b_spec = pl.BlockSpec((1, tk, tn), lambda i, j, k: (j // per, k, j % per))
    else:
        b_spec = pl.BlockSpec((tn, tk), lambda i, j, k: (j, k)) if tb else pl.BlockSpec((tk, tn), lambda i, j, k: (k, j))
    if chips == 'out':
        per = n_unit // tn
        o_spec = pl.BlockSpec((1, tm, tn), lambda i, j, k: (j // per, i, j % per))
        o_shape = jax.ShapeDtypeStruct((N_CHIPS, m, n_unit), F32)
    else:
        o_spec = pl.BlockSpec((tm, tn), lambda i, j, k: (i, j))
        o_shape = jax.ShapeDtypeStruct((m, n), F32)
    outs = pl.pallas_call(
        body, name=name, grid=(ni, nj, nk),
        in_specs=[a_spec, b_spec] + p_in, out_specs=[o_spec] + p_out,
        out_shape=[o_shape] + p_shapes, scratch_shapes=p_scratch,
        compiler_params=pltpu.CompilerParams(
            dimension_semantics=("arbitrary" if plan is not None else "parallel", "arbitrary", "arbitrary"),
            vmem_limit_bytes=MM_VMEM_LIMIT),
    )(a, b, *p_args)
    return outs[0] if plan is None else (outs[0], list(outs[1:]))


@jax.custom_vjp
def linear(x, w):
    return _matmul(x.astype(BF16), w.astype(BF16), name="linear_fwd")


def _linear_fwd(x, w):
    xb, wb = x.astype(BF16), w.astype(BF16)
    return _matmul(xb, wb, name="linear_fwd"), (xb, wb)


def _linear_bwd(res, dy):
    xb, wb = res
    dyb = dy.astype(BF16)
    return _matmul(dyb, wb, tb=True, name="linear_dx"), _matmul(xb, dyb, ta=True, name="linear_dw")


linear.defvjp(_linear_fwd, _linear_bwd)


def _rms_fwd_call(x, g, groups):
    rows, cols = x.shape
    d = cols // groups
    tr = _row_tile(rows, cols * 4)

    def body(x_ref, g_ref, y_ref):
        for gi in range(groups):
            sl = slice(gi * d, (gi + 1) * d)
            xs = x_ref[:, sl]
            r = lax.rsqrt(jnp.mean(xs * xs, axis=-1, keepdims=True) + EPS)
            y_ref[:, sl] = xs * r * g_ref[:, sl]

    return pl.pallas_call(
        body, name="rms_fwd", grid=(rows // tr,),
        in_specs=[pl.BlockSpec((tr, cols), lambda i: (i, 0)), pl.BlockSpec((1, cols), lambda i: (0, 0))],
        out_specs=pl.BlockSpec((tr, cols), lambda i: (i, 0)),
        out_shape=jax.ShapeDtypeStruct((rows, cols), F32),
        compiler_params=pltpu.CompilerParams(dimension_semantics=("parallel",)),
    )(x, g)


def _rms_bwd_call(x, g, dy, groups):
    rows, cols = x.shape
    d = cols // groups
    tr = _row_tile(rows, cols * 4)

    def body(x_ref, g_ref, dy_ref, dx_ref, dg_ref):
        @pl.when(pl.program_id(0) == 0)
        def _():
            dg_ref[...] = jnp.zeros_like(dg_ref)

        for gi in range(groups):
            sl = slice(gi * d, (gi + 1) * d)
            xs = x_ref[:, sl]
            dys = dy_ref[:, sl]
            r = lax.rsqrt(jnp.mean(xs * xs, axis=-1, keepdims=True) + EPS)
            xhat = xs * r
            dg_ref[:, sl] += jnp.sum(dys * xhat, axis=0, keepdims=True)
            dxhat = dys * g_ref[:, sl]
            dx_ref[:, sl] = r * (dxhat - xhat * jnp.mean(dxhat * xhat, axis=-1, keepdims=True))

    return pl.pallas_call(
        body, name="rms_bwd", grid=(rows // tr,),
        in_specs=[pl.BlockSpec((tr, cols), lambda i: (i, 0)), pl.BlockSpec((1, cols), lambda i: (0, 0)),
                  pl.BlockSpec((tr, cols), lambda i: (i, 0))],
        out_specs=[pl.BlockSpec((tr, cols), lambda i: (i, 0)), pl.BlockSpec((1, cols), lambda i: (0, 0))],
        out_shape=[jax.ShapeDtypeStruct((rows, cols), F32), jax.ShapeDtypeStruct((1, cols), F32)],
        compiler_params=pltpu.CompilerParams(dimension_semantics=("arbitrary",)),
    )(x, g, dy)


@functools.partial(jax.custom_vjp, nondiff_argnums=(2,))
def rms_norm(x, g, groups):
    return _rms_fwd_call(x, g, groups)


def _rms_norm_fwd(x, g, groups):
    return _rms_fwd_call(x, g, groups), (x, g)


def _rms_norm_bwd(groups, res, dy):
    x, g = res
    dx, dg = _rms_bwd_call(x, g, dy, groups)
    return dx, dg


rms_norm.defvjp(_rms_norm_fwd, _rms_norm_bwd)


NT_DIMS = (((1,), (1,)), ((), ()))
TN_DIMS = (((0,), (0,)), ((), ()))


LOG2E = 1.4426950408889634
ATTN_VMEM_LIMIT = 48 * 1024 * 1024

def _exp_rows(q, k, scale):
    s2 = lax.dot_general(q, k, NT_DIMS, preferred_element_type=F32) * (scale * LOG2E)
    e = jnp.exp2(s2 - jnp.max(s2, axis=-1, keepdims=True))
    return e, 1.0 / jnp.sum(e, axis=-1, keepdims=True)


def _attn_fwd_call(q, k, v, scale):
    b, h, s, dk = q.shape
    hkv, dv = k.shape[1], v.shape[3]
    rep = h // hkv
    tq = _pick(s, (512, 256, 128))

    def body(q_ref, k_ref, v_ref, o_ref):
        e, inv = _exp_rows(q_ref[0, 0], k_ref[0, 0], scale)
        o_ref[0, 0] = jnp.dot(e.astype(BF16), v_ref[0, 0], preferred_element_type=F32) * inv

    return pl.pallas_call(
        body, name="attn_fwd", grid=(b, h, s // tq),
        in_specs=[pl.BlockSpec((1, 1, tq, dk), lambda bi, hi, qi: (bi, hi, qi, 0)),
                  pl.BlockSpec((1, 1, s, dk), lambda bi, hi, qi: (bi, hi // rep, 0, 0)),
                  pl.BlockSpec((1, 1, s, dv), lambda bi, hi, qi: (bi, hi // rep, 0, 0))],
        out_specs=pl.BlockSpec((1, 1, tq, dv), lambda bi, hi, qi: (bi, hi, qi, 0)),
        out_shape=jax.ShapeDtypeStruct((b, h, s, dv), F32),
        compiler_params=pltpu.CompilerParams(dimension_semantics=("parallel", "parallel", "parallel"),
                                             vmem_limit_bytes=ATTN_VMEM_LIMIT),
    )(q, k, v)


def _attn_bwd_call(q, k, v, do, scale):
    b, h, s, dk = q.shape
    hkv, dv = k.shape[1], v.shape[3]
    rep = h // hkv
    tq = _pick(s, (512, 256, 128))

    def body(q_ref, k_ref, v_ref, do_ref, dq_ref, dk_ref, dv_ref):
        @pl.when((pl.program_id(2) == 0) & (pl.program_id(3) == 0))
        def _():
            dk_ref[...] = jnp.zeros_like(dk_ref)
            dv_ref[...] = jnp.zeros_like(dv_ref)

        qb = q_ref[0, 0]
        kb = k_ref[0, 0]
        vb = v_ref[0, 0]
        dob = do_ref[0, 0]
        e, inv = _exp_rows(qb, kb, scale)
        dp = lax.dot_general(dob, vb, NT_DIMS, preferred_element_type=F32)
        delta = jnp.sum(e * dp, axis=-1, keepdims=True) * inv
        ds = (e * ((dp - delta) * (inv * scale))).astype(BF16)
        dq_ref[0, 0] = jnp.dot(ds, kb, preferred_element_type=F32)
        dk_ref[0, 0] += lax.dot_general(ds, qb, TN_DIMS, preferred_element_type=F32)
        dv_ref[0, 0] += lax.dot_general(e.astype(BF16), (dob.astype(F32) * inv).astype(BF16), TN_DIMS,
                                        preferred_element_type=F32)

    return pl.pallas_call(
        body, name="attn_bwd", grid=(b, hkv, rep, s // tq),
        in_specs=[pl.BlockSpec((1, 1, tq, dk), lambda bi, gi, ri, qi: (bi, gi * rep + ri, qi, 0)),
                  pl.BlockSpec((1, 1, s, dk), lambda bi, gi, ri, qi: (bi, gi, 0, 0)),
                  pl.BlockSpec((1, 1, s, dv), lambda bi, gi, ri, qi: (bi, gi, 0, 0)),
                  pl.BlockSpec((1, 1, tq, dv), lambda bi, gi, ri, qi: (bi, gi * rep + ri, qi, 0))],
        out_specs=[pl.BlockSpec((1, 1, tq, dk), lambda bi, gi, ri, qi: (bi, gi * rep + ri, qi, 0)),
                   pl.BlockSpec((1, 1, s, dk), lambda bi, gi, ri, qi: (bi, gi, 0, 0)),
                   pl.BlockSpec((1, 1, s, dv), lambda bi, gi, ri, qi: (bi, gi, 0, 0))],
        out_shape=[jax.ShapeDtypeStruct(q.shape, F32), jax.ShapeDtypeStruct(k.shape, F32),
                   jax.ShapeDtypeStruct(v.shape, F32)],
        compiler_params=pltpu.CompilerParams(
            dimension_semantics=("parallel", "parallel", "arbitrary", "arbitrary"), vmem_limit_bytes=ATTN_VMEM_LIMIT),
    )(q, k, v, do)


@functools.partial(jax.custom_vjp, nondiff_argnums=(3,))
def attention(q, k, v, scale):
    return _attn_fwd_call(q.astype(BF16), k.astype(BF16), v.astype(BF16), scale)


def _attention_fwd(q, k, v, scale):
    qb, kb, vb = q.astype(BF16), k.astype(BF16), v.astype(BF16)
    return _attn_fwd_call(qb, kb, vb, scale), (qb, kb, vb)


def _attention_bwd(scale, res, do):
    qb, kb, vb = res
    return tuple(_attn_bwd_call(qb, kb, vb, do.astype(BF16), scale))


attention.defvjp(_attention_fwd, _attention_bwd)


CONV_COL_TILE = 256
CONV_PACK_ROWS = 8


def _shifted(x, off, rows):
    if off == 0:
        return x
    s = x.shape[0]
    rolled = pltpu.roll(x, (-off) % s, 0)
    valid = (rows + off >= 0) & (rows + off < s)
    return jnp.where(valid, rolled, 0.0)


def _conv_pre(x, wb_ref, rows):
    z = jnp.zeros_like(x) + wb_ref[SSD_CONV:SSD_CONV + 1, :]
    for j in range(SSD_CONV):
        z = z + wb_ref[j:j + 1, :] * _shifted(x, j - SSD_CONV // 2, rows)
    return z


def _conv_fwd_call(x, wb):
    b, s, c = x.shape
    tc = _pick(c, (CONV_COL_TILE, LANES))

    def body(x_ref, wb_ref, y_ref):
        xv = x_ref[0]
        rows = lax.broadcasted_iota(jnp.int32, xv.shape, 0)
        z = _conv_pre(xv, wb_ref, rows)
        y_ref[0] = z * jax.nn.sigmoid(z)

    return pl.pallas_call(
        body, name="conv_fwd", grid=(b, c // tc),
        in_specs=[pl.BlockSpec((1, s, tc), lambda bi, ci: (bi, 0, ci)),
                  pl.BlockSpec((CONV_PACK_ROWS, tc), lambda bi, ci: (0, ci))],
        out_specs=pl.BlockSpec((1, s, tc), lambda bi, ci: (bi, 0, ci)),
        out_shape=jax.ShapeDtypeStruct(x.shape, F32),
        compiler_params=pltpu.CompilerParams(dimension_semantics=("parallel", "parallel")),
    )(x, wb)


def _conv_bwd_call(x, wb, dy):
    b, s, c = x.shape
    tc = _pick(c, (CONV_COL_TILE, LANES))

    def body(x_ref, wb_ref, dy_ref, dx_ref, dwb_ref):
        xv = x_ref[0]
        rows = lax.broadcasted_iota(jnp.int32, xv.shape, 0)
        z = _conv_pre(xv, wb_ref, rows)
        sg = jax.nn.sigmoid(z)
        dz = dy_ref[0] * (sg * (1.0 + z * (1.0 - sg)))
        dx = jnp.zeros_like(xv)
        for j in range(SSD_CONV):
            off = j - SSD_CONV // 2
            dx = dx + wb_ref[j:j + 1, :] * _shifted(dz, -off, rows)
            dwb_ref[0, j:j + 1, :] = jnp.sum(dz * _shifted(xv, off, rows), axis=0, keepdims=True)
        dx_ref[0] = dx
        dwb_ref[0, SSD_CONV:SSD_CONV + 1, :] = jnp.sum(dz, axis=0, keepdims=True)
        dwb_ref[0, SSD_CONV + 1:, :] = jnp.zeros((CONV_PACK_ROWS - SSD_CONV - 1, dz.shape[1]), F32)

    return pl.pallas_call(
        body, name="conv_bwd", grid=(b, c // tc),
        in_specs=[pl.BlockSpec((1, s, tc), lambda bi, ci: (bi, 0, ci)),
                  pl.BlockSpec((CONV_PACK_ROWS, tc), lambda bi, ci: (0, ci)),
                  pl.BlockSpec((1, s, tc), lambda bi, ci: (bi, 0, ci))],
        out_specs=[pl.BlockSpec((1, s, tc), lambda bi, ci: (bi, 0, ci)),
                   pl.BlockSpec((1, CONV_PACK_ROWS, tc), lambda bi, ci: (bi, 0, ci))],
        out_shape=[jax.ShapeDtypeStruct(x.shape, F32), jax.ShapeDtypeStruct((b, CONV_PACK_ROWS, c), F32)],
        compiler_params=pltpu.CompilerParams(dimension_semantics=("parallel", "parallel")),
    )(x, wb, dy)


@jax.custom_vjp
def conv_silu(x, wb):
    return _conv_fwd_call(x, wb)


def _conv_silu_fwd(x, wb):
    return _conv_fwd_call(x, wb), (x, wb)


def _conv_silu_bwd(res, dy):
    x, wb = res
    dx, dwb = _conv_bwd_call(x, wb, dy)
    return dx, jnp.sum(dwb, axis=0)


conv_silu.defvjp(_conv_silu_fwd, _conv_silu_bwd)


SSD_PAIRS = SSD_GROUP_HEADS // 2
NEG_INF = -1e30


def _ssd_common(x_ref, dtx_ref, dtt_ref, anx_ref, anc_ref, b_ref, c_ref, reverse):
    L = SSD_CHUNK
    xv = x_ref[0]
    dt = dtx_ref[0]
    ri = lax.broadcasted_iota(jnp.int32, (L, L), 0)
    ci = lax.broadcasted_iota(jnp.int32, (L, L), 1)
    causal = (ri <= ci) if reverse else (ri >= ci)
    tri = causal.astype(F32)
    a_cs = jnp.dot(tri, dt * anx_ref[...], precision=HIGHEST, preferred_element_type=F32)
    a_row = dtt_ref[0, 0] * anc_ref[0]
    acs_row = lax.dot_general(a_row, tri, NT_DIMS, precision=HIGHEST, preferred_element_type=F32)
    xd = xv * dt
    bmat = b_ref[0].astype(BF16)
    cmat = c_ref[0].astype(BF16)
    gmat = lax.dot_general(cmat, bmat, NT_DIMS, preferred_element_type=F32)
    return xv, dt, causal, tri, a_cs, acs_row, xd, bmat, cmat, gmat


def _ssd_lambda(a_cs, acs_row, causal, h):
    col = a_cs[:, h * SSD_HEAD_DIM:h * SSD_HEAD_DIM + 1]
    row = acs_row[h:h + 1, :]
    return jnp.exp(jnp.where(causal, col - row, NEG_INF))


def _ssd_fwd_call(x, dtx, dtt, anx, anc, bm, cm, reverse):
    b, s, _ = x.shape
    L, N, GW = SSD_CHUNK, SSD_STATE, SSD_GROUP_WIDTH
    nc = s // L
    end = 0 if reverse else L - 1

    def body(x_ref, dtx_ref, dtt_ref, anx_ref, anc_ref, b_ref, c_ref, y_ref, hs_ref, state):
        @pl.when(pl.program_id(2) == 0)
        def _():
            state[...] = jnp.zeros_like(state)

        xv, dt, causal, tri, a_cs, acs_row, xd, bmat, cmat, gmat = _ssd_common(
            x_ref, dtx_ref, dtt_ref, anx_ref, anc_ref, b_ref, c_ref, reverse)
        hin = state[...]
        hs_ref[0, 0, 0] = hin
        y_off = jnp.dot(cmat, hin.astype(BF16), preferred_element_type=F32) * jnp.exp(a_cs)
        a_end = a_cs[end:end + 1, :]
        s_new = lax.dot_general(bmat, (xd * jnp.exp(a_end - a_cs)).astype(BF16), TN_DIMS, preferred_element_type=F32)
        state[...] = jnp.exp(a_end) * hin + s_new
        lane = lax.broadcasted_iota(jnp.int32, (L, LANES), 1)
        for pr in range(SSD_PAIRS):
            sl = slice(pr * LANES, (pr + 1) * LANES)
            xdp = xd[:, sl].astype(BF16)
            w0 = (gmat * _ssd_lambda(a_cs, acs_row, causal, 2 * pr)).astype(BF16)
            w1 = (gmat * _ssd_lambda(a_cs, acs_row, causal, 2 * pr + 1)).astype(BF16)
            y0 = jnp.dot(w0, xdp, preferred_element_type=F32)
            y1 = jnp.dot(w1, xdp, preferred_element_type=F32)
            y_ref[0, :, sl] = jnp.where(lane < SSD_HEAD_DIM, y0, y1) + y_off[:, sl]

    G = SSD_GROUPS
    chunk = (lambda c: nc - 1 - c) if reverse else (lambda c: c)
    seq = lambda bi, gi, c: (bi, chunk(c), gi)
    return pl.pallas_call(
        body, name="ssd_fwd", grid=(b, G, nc),
        in_specs=[pl.BlockSpec((1, L, GW), seq),
                  pl.BlockSpec((1, L, GW), seq),
                  pl.BlockSpec((1, 1, SSD_GROUP_HEADS, L), lambda bi, gi, c: (bi, gi, 0, chunk(c))),
                  pl.BlockSpec((1, GW), lambda bi, gi, c: (0, gi)),
                  pl.BlockSpec((1, SSD_GROUP_HEADS, 1), lambda bi, gi, c: (gi, 0, 0)),
                  pl.BlockSpec((1, L, N), seq),
                  pl.BlockSpec((1, L, N), seq)],
        out_specs=[pl.BlockSpec((1, L, GW), seq),
                   pl.BlockSpec((1, 1, 1, N, GW), lambda bi, gi, c: (bi, gi, chunk(c), 0, 0))],
        out_shape=[jax.ShapeDtypeStruct(x.shape, F32), jax.ShapeDtypeStruct((b, G, nc, N, GW), F32)],
        scratch_shapes=[pltpu.VMEM((N, GW), F32)],
        compiler_params=pltpu.CompilerParams(dimension_semantics=("parallel", "parallel", "arbitrary")),
    )(x, dtx, dtt, anx, anc, bm, cm)


def _ssd_bwd_call(x, dtx, dtt, anx, anc, bm, cm, hs, dy, reverse):
    b, s, _ = x.shape
    L, N, GW = SSD_CHUNK, SSD_STATE, SSD_GROUP_WIDTH
    nc = s // L
    end = 0 if reverse else L - 1

    def body(x_ref, dtx_ref, dtt_ref, anx_ref, anc_ref, b_ref, c_ref, hs_ref, dy_ref,
             dx_ref, ddt_ref, dan_ref, db_ref, dc_ref, dstate):
        @pl.when(pl.program_id(2) == 0)
        def _():
            dstate[...] = jnp.zeros_like(dstate)

        xv, dt, causal, tri, a_cs, acs_row, xd, bmat, cmat, gmat = _ssd_common(
            x_ref, dtx_ref, dtt_ref, anx_ref, anc_ref, b_ref, c_ref, reverse)
        hin = hs_ref[0, 0, 0]
        hinb = hin.astype(BF16)
        dyv = dy_ref[0]
        ds_out = dstate[...]
        dsb = ds_out.astype(BF16)
        eacs = jnp.exp(a_cs)
        a_end = a_cs[end:end + 1, :]
        e_end = jnp.exp(a_end)
        dec = jnp.exp(a_end - a_cs)
        dye = dyv * eacs
        dyeb = dye.astype(BF16)
        xdec = xd * dec
        ch = jnp.dot(cmat, hinb, preferred_element_type=F32)
        bds = jnp.dot(bmat, dsb, preferred_element_type=F32)
        t_state = xdec * bds
        d_aend = jnp.sum(t_state, axis=0, keepdims=True) + e_end * jnp.sum(ds_out * hin, axis=0, keepdims=True)
        dacs = dye * ch - t_state
        dxd_state = bds * dec
        dstate[...] = e_end * ds_out + lax.dot_general(cmat, dyeb, TN_DIMS, preferred_element_type=F32)

        lane = lax.broadcasted_iota(jnp.int32, (L, LANES), 1)
        dg = jnp.zeros((L, L), F32)
        dxd_parts, dacs_parts = [], []
        for pr in range(SSD_PAIRS):
            sl = slice(pr * LANES, (pr + 1) * LANES)
            xdp = xd[:, sl]
            dyp = dyv[:, sl]
            dxd_p = jnp.zeros((L, LANES), F32)
            dacs_p = jnp.zeros((L, LANES), F32)
            for half in range(2):
                mine = (lane < SSD_HEAD_DIM) if half == 0 else (lane >= SSD_HEAD_DIM)
                lam = _ssd_lambda(a_cs, acs_row, causal, 2 * pr + half)
                w = gmat * lam
                xdh = jnp.where(mine, xdp, 0.0).astype(BF16)
                dyh = jnp.where(mine, dyp, 0.0).astype(BF16)
                dw = lax.dot_general(dyh, xdh, NT_DIMS, preferred_element_type=F32)
                dg = dg + dw * lam
                mm = dw * w
                rs = jnp.sum(mm, axis=1, keepdims=True)
                cs = jnp.sum(mm.T, axis=1, keepdims=True)
                dacs_p = dacs_p + jnp.where(mine, (rs - cs) * (1.0 / SSD_HEAD_DIM), 0.0)
                wtdy = lax.dot_general(w.astype(BF16), dyh, TN_DIMS, preferred_element_type=F32)
                dxd_p = dxd_p + wtdy
            dxd_parts.append(dxd_p)
            dacs_parts.append(dacs_p)
        dxd = jnp.concatenate(dxd_parts, axis=1) + dxd_state
        dacs = dacs + jnp.concatenate(dacs_parts, axis=1)
        last = lax.broadcasted_iota(jnp.int32, dacs.shape, 0) == end
        dacs = dacs + jnp.where(last, d_aend, 0.0)
        da = lax.dot_general(tri, dacs, TN_DIMS, precision=HIGHEST, preferred_element_type=F32)
        dgb = dg.astype(BF16)
        dc_ref[0] = (jnp.dot(dgb, bmat, preferred_element_type=F32)
                     + lax.dot_general(dyeb, hinb, NT_DIMS, preferred_element_type=F32))
        db_ref[0] = (lax.dot_general(dgb, cmat, TN_DIMS, preferred_element_type=F32)
                     + lax.dot_general(xdec.astype(BF16), dsb, NT_DIMS, preferred_element_type=F32))
        dx_ref[0] = dxd * dt
        ddt_ref[0] = da * anx_ref[...] + dxd * xv
        dan_ref[0, 0, 0] = jnp.sum(da * dt, axis=0, keepdims=True)

    G = SSD_GROUPS
    chunk = (lambda c: c) if reverse else (lambda c: nc - 1 - c)
    rev = lambda bi, gi, c: (bi, chunk(c), gi)
    return pl.pallas_call(
        body, name="ssd_bwd", grid=(b, G, nc),
        in_specs=[pl.BlockSpec((1, L, GW), rev),
                  pl.BlockSpec((1, L, GW), rev),
                  pl.BlockSpec((1, 1, SSD_GROUP_HEADS, L), lambda bi, gi, c: (bi, gi, 0, chunk(c))),
                  pl.BlockSpec((1, GW), lambda bi, gi, c: (0, gi)),
                  pl.BlockSpec((1, SSD_GROUP_HEADS, 1), lambda bi, gi, c: (gi, 0, 0)),
                  pl.BlockSpec((1, L, N), rev),
                  pl.BlockSpec((1, L, N), rev),
                  pl.BlockSpec((1, 1, 1, N, GW), lambda bi, gi, c: (bi, gi, chunk(c), 0, 0)),
                  pl.BlockSpec((1, L, GW), rev)],
        out_specs=[pl.BlockSpec((1, L, GW), rev),
                   pl.BlockSpec((1, L, GW), rev),
                   pl.BlockSpec((1, 1, 1, 1, GW), lambda bi, gi, c: (bi, gi, chunk(c), 0, 0)),
                   pl.BlockSpec((1, L, N), rev),
                   pl.BlockSpec((1, L, N), rev)],
        out_shape=[jax.ShapeDtypeStruct(x.shape, F32), jax.ShapeDtypeStruct(x.shape, F32),
                   jax.ShapeDtypeStruct((b, G, nc, 1, GW), F32),
                   jax.ShapeDtypeStruct(bm.shape, F32), jax.ShapeDtypeStruct(cm.shape, F32)],
        scratch_shapes=[pltpu.VMEM((N, GW), F32)],
        compiler_params=pltpu.CompilerParams(dimension_semantics=("parallel", "parallel", "arbitrary")),
    )(x, dtx, dtt, anx, anc, bm, cm, hs, dy)


@functools.partial(jax.custom_vjp, nondiff_argnums=(7,))
def _ssd_scan(x, dtx, dtt, anx, anc, bm, cm, reverse):
    return _ssd_fwd_call(x, dtx, dtt, anx, anc, bm, cm, reverse)[0]


def _ssd_scan_fwd(x, dtx, dtt, anx, anc, bm, cm, reverse):
    y, hs = _ssd_fwd_call(x, dtx, dtt, anx, anc, bm, cm, reverse)
    return y, (x, dtx, dtt, anx, anc, bm, cm, hs)


def _ssd_scan_bwd(reverse, res, dy):
    x, dtx, dtt, anx, anc, bm, cm, hs = res
    dx, ddtx, dan, db, dc = _ssd_bwd_call(x, dtx, dtt, anx, anc, bm, cm, hs, dy, reverse)
    b, g, nc, _, gw = dan.shape
    danx = jnp.sum(dan, axis=(0, 2, 3)).reshape(1, g * gw)
    return dx, ddtx, jnp.zeros_like(dtt), danx, jnp.zeros_like(anc), db, dc


_ssd_scan.defvjp(_ssd_scan_fwd, _ssd_scan_bwd)


def ssd_chunked(xs, dt, a_neg, bm, cm, reverse):
    b, s, _ = xs.shape
    dtx = jnp.repeat(dt, SSD_HEAD_DIM, axis=-1)
    dtt = jnp.transpose(dt, (0, 2, 1)).reshape(b, SSD_GROUPS, SSD_GROUP_HEADS, s)
    anx = jnp.repeat(a_neg, SSD_HEAD_DIM)[None, :]
    anc = a_neg.reshape(SSD_GROUPS, SSD_GROUP_HEADS, 1)
    return _ssd_scan(xs, dtx, dtt, anx, anc, bm, cm, reverse)


def _loss_call(y, t):
    rows, cols = y.shape
    tr = _row_tile(rows, cols * 4)

    def body(y_ref, t_ref, loss_ref, diff_ref):
        @pl.when(pl.program_id(0) == 0)
        def _():
            loss_ref[...] = jnp.zeros_like(loss_ref)

        d = y_ref[...] - t_ref[...]
        diff_ref[...] = d * (1.0 / cols)
        part = jnp.sum(jnp.sum(d * d, axis=1, keepdims=True), axis=0, keepdims=True)
        loss_ref[...] += part * (0.5 / cols)

    return pl.pallas_call(
        body, name="loss_head", grid=(rows // tr,),
        in_specs=[pl.BlockSpec((tr, cols), lambda i: (i, 0)), pl.BlockSpec((tr, cols), lambda i: (i, 0))],
        out_specs=[pl.BlockSpec((1, 1), lambda i: (0, 0)), pl.BlockSpec((tr, cols), lambda i: (i, 0))],
        out_shape=[jax.ShapeDtypeStruct((1, 1), F32), jax.ShapeDtypeStruct((rows, cols), F32)],
        compiler_params=pltpu.CompilerParams(dimension_semantics=("arbitrary",)),
    )(y, t)


@jax.custom_vjp
def loss_head(y, t):
    return _loss_call(y, t)[0][0, 0]


def _loss_head_fwd(y, t):
    loss, diff = _loss_call(y, t)
    return loss[0, 0], diff


def _loss_head_bwd(diff, g):
    return g * diff, jnp.zeros_like(diff)


loss_head.defvjp(_loss_head_fwd, _loss_head_bwd)


def _axial_rope_tables(seq_len, rot_dim):
    rows = seq_len // GRID_W
    row_idx = jnp.repeat(jnp.arange(rows), GRID_W).astype(F32)
    col_idx = jnp.tile(jnp.arange(GRID_W), rows).astype(F32)
    axis_dim = rot_dim // 2
    inv_freq = jnp.power(ROPE_THETA, -jnp.arange(0, axis_dim, 2, dtype=F32) / axis_dim)
    ang_r = row_idx[:, None] * inv_freq[None, :]
    ang_c = col_idx[:, None] * inv_freq[None, :]
    return jnp.cos(ang_r), jnp.sin(ang_r), jnp.cos(ang_c), jnp.sin(ang_c)


def _rotate(x, cos, sin):
    x1, x2 = jnp.split(x, 2, axis=-1)
    cos = cos[:, None, :]
    sin = sin[:, None, :]
    return jnp.concatenate([x1 * cos - x2 * sin, x1 * sin + x2 * cos], axis=-1)


def _apply_axial_rope(x, tables):
    cos_r, sin_r, cos_c, sin_c = tables
    x_row, x_col = jnp.split(x, 2, axis=-1)
    return jnp.concatenate([_rotate(x_row, cos_r, sin_r), _rotate(x_col, cos_c, sin_c)], axis=-1)


def _heads_first(t):
    return jnp.transpose(t, (0, 2, 1, 3))


def _gqa_group(q, k, v, q_norm_g, k_norm_g, rope, b, s):
    q = rms_norm(q, jnp.tile(q_norm_g, GQA_HEADS)[None, :], GQA_HEADS).reshape(b, s, GQA_HEADS, GQA_HEAD_DIM)
    k = rms_norm(k, jnp.tile(k_norm_g, GQA_KV_HEADS)[None, :], GQA_KV_HEADS).reshape(b, s, GQA_KV_HEADS, GQA_HEAD_DIM)
    v = v.reshape(b, s, GQA_KV_HEADS, GQA_HEAD_DIM)
    q = _apply_axial_rope(q, rope)
    k = _apply_axial_rope(k, rope)
    o = attention(_heads_first(q), _heads_first(k), _heads_first(v), GQA_HEAD_DIM ** -0.5)
    return _heads_first(o).reshape(b * s, GQA_WIDTH)


def _mla_group(c_q, c_kv, k_pe, q_norm_g, w_uq, kv_norm_g, w_ukv, rope, b, s):
    q = linear(rms_norm(c_q, q_norm_g[None, :], 1), w_uq).reshape(b, s, MLA_HEADS, MLA_NOPE_DIM + MLA_ROPE_DIM)
    q_nope, q_pe = q[..., :MLA_NOPE_DIM], q[..., MLA_NOPE_DIM:]
    kv = linear(rms_norm(c_kv, kv_norm_g[None, :], 1), w_ukv).reshape(b, s, MLA_HEADS, MLA_NOPE_DIM + MLA_V_DIM)
    k_nope, v = kv[..., :MLA_NOPE_DIM], kv[..., MLA_NOPE_DIM:]
    q_pe = _apply_axial_rope(q_pe, rope)
    k_pe = _apply_axial_rope(k_pe.reshape(b, s, 1, MLA_ROPE_DIM), rope)
    q = jnp.concatenate([q_nope, q_pe], axis=-1)
    k = jnp.concatenate([k_nope, jnp.broadcast_to(k_pe, (b, s, MLA_HEADS, MLA_ROPE_DIM))], axis=-1)
    o = attention(_heads_first(q), _heads_first(k), _heads_first(v), (MLA_NOPE_DIM + MLA_ROPE_DIM) ** -0.5)
    return _heads_first(o).reshape(b * s, MLA_HEADS * MLA_V_DIM)


def _ssd_group(z, xbc, dt_raw, conv_w, conv_b, dt_bias, a_log, d_skip, norm_g, b, s):
    wb = jnp.concatenate([conv_w, conv_b[None, :], jnp.zeros((CONV_PACK_ROWS - SSD_CONV - 1, SSD_CONV_DIM), F32)], axis=0)
    xbc = conv_silu(xbc.reshape(b, s, SSD_CONV_DIM), wb)
    xs = xbc[..., :SSD_INNER]
    bm = xbc[..., SSD_INNER:SSD_INNER + SSD_GROUPS * SSD_STATE]
    cm = xbc[..., SSD_INNER + SSD_GROUPS * SSD_STATE:]
    dt = jax.nn.softplus(dt_raw.reshape(b, s, 2, SSD_HEADS) + dt_bias)
    a_neg = -jnp.exp(a_log)
    y_fwd = ssd_chunked(xs, dt[:, :, 0], a_neg[0], bm, cm, False)
    y_bwd = ssd_chunked(xs, dt[:, :, 1], a_neg[1], bm, cm, True)
    y = y_fwd + y_bwd + xs * jnp.repeat(d_skip, SSD_HEAD_DIM)
    y = y.reshape(b * s, SSD_INNER) * jax.nn.silu(z)
    return rms_norm(y, norm_g[None, :], SSD_GROUPS)


MIXER_WEIGHTS = ('q_norm_g', 'k_norm_g', 'mla_q_norm_g', 'w_uq', 'mla_kv_norm_g', 'w_ukv', 'conv_w', 'conv_b',
                 'dt_bias', 'a_log', 'd_skip', 'ssd_norm_g')


def _mixer(proj, w, rope_a, rope_b, b, s):
    idx = np.cumsum(IN_SPLITS).tolist()
    q_a, k_a, v_a, cq_b, ckv_b, kpe_b, z_c, xbc_c, dt_c = [_in_cols(proj, lo, hi)
                                                           for lo, hi in zip([0] + idx[:-1], idx)]
    o_a = _gqa_group(q_a, k_a, v_a, w["q_norm_g"], w["k_norm_g"], rope_a, b, s)
    o_b = _mla_group(cq_b, ckv_b, kpe_b, w["mla_q_norm_g"], w["w_uq"], w["mla_kv_norm_g"], w["w_ukv"], rope_b, b, s)
    o_c = _ssd_group(z_c, xbc_c, dt_c, w["conv_w"], w["conv_b"], w["dt_bias"], w["a_log"], w["d_skip"],
                     w["ssd_norm_g"], b, s)
    return jnp.concatenate([o_a, o_b, o_c], axis=-1)


def _seq_tile(s, row_bytes):
    return _row_tile(s, row_bytes)


def _normmod_fwd(x, g, scale, shift):
    b, s, d = x.shape
    tr = _seq_tile(s, d * 4)

    def body(x_ref, g_ref, sc_ref, sh_ref, h_ref):
        xv = x_ref[0]
        r = lax.rsqrt(jnp.mean(xv * xv, axis=-1, keepdims=True) + EPS)
        h_ref[0] = (xv * r * g_ref[...] * (1.0 + sc_ref[0]) + sh_ref[0]).astype(BF16)

    act = pl.BlockSpec((1, tr, d), lambda bi, i: (bi, i, 0))
    vec = pl.BlockSpec((1, 1, d), lambda bi, i: (bi, 0, 0))
    return pl.pallas_call(
        body, name="normmod_fwd", grid=(b, s // tr),
        in_specs=[act, pl.BlockSpec((1, d), lambda bi, i: (0, 0)), vec, vec], out_specs=act,
        out_shape=jax.ShapeDtypeStruct((b, s, d), BF16),
        compiler_params=pltpu.CompilerParams(dimension_semantics=("parallel", "parallel")),
    )(x, g, scale, shift)


def _normmod_bwd(x, g, scale, dh, resid):
    b, s, d = x.shape
    tr = _seq_tile(s, d * 4)

    def body(x_ref, g_ref, sc_ref, dh_ref, res_ref, dx_ref, dg_ref, dsc_ref, dsh_ref):
        bi, i = pl.program_id(0), pl.program_id(1)

        @pl.when((bi == 0) & (i == 0))
        def _():
            dg_ref[...] = jnp.zeros_like(dg_ref)

        @pl.when(i == 0)
        def _():
            dsc_ref[...] = jnp.zeros_like(dsc_ref)
            dsh_ref[...] = jnp.zeros_like(dsh_ref)

        xv = x_ref[0]
        dhv = dh_ref[0]
        gv = g_ref[...]
        r = lax.rsqrt(jnp.mean(xv * xv, axis=-1, keepdims=True) + EPS)
        xhat = xv * r
        dsh_ref[0] += jnp.sum(dhv, axis=0, keepdims=True)
        dsc_ref[0] += jnp.sum(dhv * (xhat * gv), axis=0, keepdims=True)
        dn = dhv * (1.0 + sc_ref[0])
        dg_ref[...] += jnp.sum(dn * xhat, axis=0, keepdims=True)
        dxhat = dn * gv
        dx_ref[0] = r * (dxhat - xhat * jnp.mean(dxhat * xhat, axis=-1, keepdims=True)) + res_ref[0]

    act = pl.BlockSpec((1, tr, d), lambda bi, i: (bi, i, 0))
    vec = pl.BlockSpec((1, 1, d), lambda bi, i: (bi, 0, 0))
    gain = pl.BlockSpec((1, d), lambda bi, i: (0, 0))
    return pl.pallas_call(
        body, name="normmod_bwd", grid=(b, s // tr),
        in_specs=[act, gain, vec, act, act], out_specs=[act, gain, vec, vec],
        out_shape=[jax.ShapeDtypeStruct((b, s, d), F32), jax.ShapeDtypeStruct((1, d), F32),
                   jax.ShapeDtypeStruct((b, 1, d), F32), jax.ShapeDtypeStruct((b, 1, d), F32)],
        compiler_params=pltpu.CompilerParams(dimension_semantics=("arbitrary", "arbitrary")),
    )(x, g, scale, dh, resid)


def _gated_add(x, gate, t):
    b, s, d = x.shape
    tr = _seq_tile(s, d * 4)

    def body(x_ref, g_ref, t_ref, o_ref):
        o_ref[0] = x_ref[0] + g_ref[0] * t_ref[0]

    act = pl.BlockSpec((1, tr, d), lambda bi, i: (bi, i, 0))
    vec = pl.BlockSpec((1, 1, d), lambda bi, i: (bi, 0, 0))
    return pl.pallas_call(
        body, name="gated_add", grid=(b, s // tr), in_specs=[act, vec, act], out_specs=act,
        out_shape=jax.ShapeDtypeStruct((b, s, d), F32),
        compiler_params=pltpu.CompilerParams(dimension_semantics=("parallel", "parallel")),
    )(x, gate, t)


def _gated_bwd(dy, gate, t):
    b, s, d = dy.shape
    tr = _seq_tile(s, d * 4)

    def body(dy_ref, g_ref, t_ref, dt_ref, dgate_ref):
        @pl.when(pl.program_id(1) == 0)
        def _():
            dgate_ref[...] = jnp.zeros_like(dgate_ref)

        dyv = dy_ref[0]
        dt_ref[0] = (g_ref[0] * dyv).astype(BF16)
        dgate_ref[0] += jnp.sum(dyv * t_ref[0], axis=0, keepdims=True)

    act = pl.BlockSpec((1, tr, d), lambda bi, i: (bi, i, 0))
    vec = pl.BlockSpec((1, 1, d), lambda bi, i: (bi, 0, 0))
    return pl.pallas_call(
        body, name="gated_bwd", grid=(b, s // tr), in_specs=[act, vec, act], out_specs=[act, vec],
        out_shape=[jax.ShapeDtypeStruct((b, s, d), BF16), jax.ShapeDtypeStruct((b, 1, d), F32)],
        compiler_params=pltpu.CompilerParams(dimension_semantics=("parallel", "arbitrary")),
    )(dy, gate, t)


def _swiglu_fwd(gu, plan=None):
    rows, f2 = gu.shape
    f = f2 // 2
    tr = _row_tile(rows, f2 * 4, SWIGLU_TILE_BYTES)
    steps = rows // tr
    p_in, p_out, p_shapes, p_scratch, p_args = _plan_specs(plan)

    def body(gu_ref, *rest):
        a_ref = rest[len(p_in)]
        i = pl.program_id(0)

        def compute():
            gt = gu_ref[:, :f]
            a_ref[...] = (gt * jax.nn.sigmoid(gt) * gu_ref[:, f:]).astype(BF16)

        _ride(plan, rest[:len(p_in)] + rest[len(p_in) + 1:], i == 0, i == steps - 1, compute)

    outs = pl.pallas_call(
        body, name="swiglu_fwd", grid=(steps,),
        in_specs=[pl.BlockSpec((tr, f2), lambda i: (i, 0))] + p_in,
        out_specs=[pl.BlockSpec((tr, f), lambda i: (i, 0))] + p_out,
        out_shape=[jax.ShapeDtypeStruct((rows, f), BF16)] + p_shapes, scratch_shapes=p_scratch,
        compiler_params=pltpu.CompilerParams(dimension_semantics=("arbitrary" if plan is not None else "parallel",)),
    )(gu, *p_args)
    return outs[0] if plan is None else (outs[0], list(outs[1:]))


def _swiglu_bwd(gu, dact):
    rows, f2 = gu.shape
    f = f2 // 2
    tr = _row_tile(rows, f2 * 4, SWIGLU_TILE_BYTES)

    def body(gu_ref, da_ref, dgu_ref):
        gt = gu_ref[:, :f]
        up = gu_ref[:, f:]
        da = da_ref[...]
        sg = jax.nn.sigmoid(gt)
        dgu_ref[:, :f] = (da * up * (sg * (1.0 + gt * (1.0 - sg)))).astype(BF16)
        dgu_ref[:, f:] = (da * gt * sg).astype(BF16)

    return pl.pallas_call(
        body, name="swiglu_bwd", grid=(rows // tr,),
        in_specs=[pl.BlockSpec((tr, f2), lambda i: (i, 0)), pl.BlockSpec((tr, f), lambda i: (i, 0))],
        out_specs=pl.BlockSpec((tr, f2), lambda i: (i, 0)),
        out_shape=jax.ShapeDtypeStruct((rows, f2), BF16),
        compiler_params=pltpu.CompilerParams(dimension_semantics=("parallel",)),
    )(gu, dact)


class _Gathered:
    def __init__(self, shards):
        self.shards, self.full = shards, {}

    def plan(self, keys):
        return _gather_plan([self.shards[k] for k in keys])

    def store(self, keys, outs):
        for key, out in zip(keys, outs):
            name = key[0]
            g = out.reshape((N_CHIPS,) + self.shards[key].shape)
            if name in CHIP_BLOCKED:
                full = g
            elif name in COL_SHARDED and name not in TRANSPOSED:
                full = _cols_full(g).astype(F32)
            else:
                full = g.reshape(g.shape[0] * g.shape[1], g.shape[2])
            self.full[key] = full

    def carry(self, keys, fn):
        if not keys:
            return fn(None)
        res, outs = fn(self.plan(keys))
        self.store(keys, outs)
        return res


def _gather_schedule(depth):
    every = [(n, l) for l in range(depth) for n in ('w_uq', 'w_ukv')]
    sched = {'first': [('w_in', 0), ('w_gate_up', 0)] + every}
    for l in range(depth):
        sched[('w_in_fwd', l)] = [('w_out', l)] + ([('w_in', l + 1)] if l + 1 < depth else [])
        if l == 0:
            sched[('swiglu_fwd', l)] = [('w_down', l)]
        if l + 1 < depth:
            sched[('w_gate_up_fwd', l)] = [('w_gate_up', l + 1)]
            sched[('w_down_fwd', l)] = [('w_down', l + 1)]
    return sched


GATE_UP_PIECES = 4


class _Reducer:
    def __init__(self):
        self.parts, self.recv, self.result = {}, {}, {}

    def add(self, items):
        blocks = []
        for (name, _), grad in items:
            if name in CHIP_BLOCKED:
                blocks.append(grad)
            elif name in COL_SHARDED and name not in TRANSPOSED:
                blocks.append(_cols_split(grad))
            else:
                blocks.append(grad.reshape(N_CHIPS, grad.shape[0] // N_CHIPS, grad.shape[1]))
        for (key, _), parts in zip(items, _rs_parts(blocks)):
            self.parts[key] = parts
            self.recv[key] = []

    def pieces(self, key):
        rows = self.parts[key].shape[1]
        n = GATE_UP_PIECES if key[0] == 'w_gate_up' else 1
        return [(key, i * (rows // n), rows // n) for i in range(n)]

    def plan(self, jobs):
        return _chip_exchange_plan([(self.parts[key], row0, rows) for key, row0, rows in jobs])

    def store(self, jobs, outs):
        complete = []
        for (key, row0, rows), out in zip(jobs, outs):
            self.recv[key].append((row0, out))
            if len(self.recv[key]) == len(self.pieces(key)):
                complete.append(key)
        if complete:
            items = [(self.parts[key], sorted(self.recv[key], key=lambda t: t[0])) for key in complete]
            self.result.update(zip(complete, _rs_result(items)))

    def carry(self, keys, fn, piece=None):
        jobs = [j for key in keys for j in self.pieces(key)]
        if piece is not None:
            jobs = [j for key in keys for j in self.pieces(key)[piece:piece + 1]]
        if not jobs:
            return fn(None)
        res, outs = fn(self.plan(jobs))
        self.store(jobs, outs)
        return res

    def flush(self):
        jobs = [j for key in self.parts for j in self.pieces(key)
                if key not in self.result and j[1] not in [r for r, _ in self.recv[key]]]
        if jobs:
            self.store(jobs, _run_plan(self.plan(jobs), "rs_chip_exchange"))


def _layer_fwd(x, mod, w, gathered, l, sched, rope_a, rope_b):
    b, s, d = x.shape
    m = b * s
    shift1, scale1, gate1, shift2, scale2, gate2 = [t[:, None, :] for t in jnp.split(mod, 6, axis=-1)]
    g1, g2 = w["norm1_g"][None, :], w["norm2_g"][None, :]
    full = lambda n: gathered.full[(n, l)]
    h1 = _normmod_fwd(x, g1, scale1, shift1).reshape(m, d)
    proj = gathered.carry(sched.get(('w_in_fwd', l)), lambda p: _matmul(h1, full('w_in'), tb=True, name="w_in_fwd", plan=p))
    mixer_w = {n: (full(n) if n in COL_SHARDED else w[n]) for n in MIXER_WEIGHTS}
    o, mixer_vjp = jax.vjp(lambda p, mw: _mixer(p, mw, rope_a, rope_b, b, s), proj, mixer_w)
    o = o.astype(BF16)
    mix = _matmul(o, full('w_out'), name="w_out_fwd").reshape(b, s, d)
    x_mid = _gated_add(x, gate1, mix)
    h2 = _normmod_fwd(x_mid, g2, scale2, shift2).reshape(m, d)
    gu = gathered.carry(sched.get(('w_gate_up_fwd', l)),
                        lambda p: _matmul(h2, full('w_gate_up'), name="w_gate_up_fwd", plan=p, chips='b'))
    act = gathered.carry(sched.get(('swiglu_fwd', l)), lambda p: _swiglu_fwd(gu, plan=p))
    ffn = gathered.carry(sched.get(('w_down_fwd', l)), lambda p: _matmul(act, full('w_down'), name="w_down_fwd", plan=p))
    ffn = ffn.reshape(b, s, d)
    x_out = _gated_add(x_mid, gate2, ffn)
    res = (x, x_mid, h1, h2, o, mix, gu, act, ffn, mixer_vjp, scale1, gate1, scale2, gate2, g1, g2)
    return x_out, res


def _layer_bwd(res, gathered, reducer, l, depth, dx_out):
    x, x_mid, h1, h2, o, mix, gu, act, ffn, mixer_vjp, scale1, gate1, scale2, gate2, g1, g2 = res
    b, s, d = x.shape
    m = b * s
    full = lambda n: gathered.full[(n, l)]
    above = l + 1 < depth
    dffn, dgate2 = _gated_bwd(dx_out, gate2, ffn)
    dffn = dffn.reshape(m, d)
    dact = reducer.carry([('w_out', l + 1), ('w_uq', l + 1), ('w_ukv', l + 1)] if above else [],
                         lambda p: _matmul(dffn, full('w_down'), tb=True, name="w_down_dx", plan=p))
    dw = reducer.carry([('w_in', l + 1)] if above else [],
                       lambda p: _matmul(act, dffn, ta=True, name="w_down_dw", plan=p))
    reducer.add([(('w_down', l), dw)])
    dgu = _swiglu_bwd(gu, dact)
    dh2 = reducer.carry([('w_down', l)], lambda p: _matmul(dgu, full('w_gate_up'), tb=True, name="w_gate_up_dx", plan=p,
                                                           chips='b'))
    dh2 = dh2.reshape(b, s, d)
    reducer.add([(('w_gate_up', l), _matmul(h2, dgu, ta=True, name="w_gate_up_dw", chips='out'))])
    dx_mid, dg2, dscale2, dshift2 = _normmod_bwd(x_mid, g2, scale2, dh2, dx_out)
    dmix, dgate1 = _gated_bwd(dx_mid, gate1, mix)
    dmix = dmix.reshape(m, d)
    gate_up = [('w_gate_up', l)]
    do = reducer.carry(gate_up, lambda p: _matmul(dmix, full('w_out'), tb=True, name="w_out_dx", plan=p), piece=0)
    dw_out = reducer.carry(gate_up, lambda p: _matmul(o, dmix, ta=True, name="w_out_dw", plan=p), piece=1)
    dproj, grads = mixer_vjp(do)
    grads = dict(grads)
    reducer.add([(('w_out', l), dw_out), (('w_uq', l), grads.pop('w_uq')), (('w_ukv', l), grads.pop('w_ukv'))])
    dproj = dproj.astype(BF16)
    dh1 = reducer.carry(gate_up, lambda p: _matmul(dproj, full('w_in'), name="w_in_dx", plan=p), piece=2)
    dh1 = dh1.reshape(b, s, d)
    dw = reducer.carry(gate_up, lambda p: _matmul(dproj, h1, ta=True, name="w_in_dw", plan=p), piece=3)
    reducer.add([(('w_in', l), dw)])
    dx, dg1, dscale1, dshift1 = _normmod_bwd(x, g1, scale1, dh1, dx_mid)
    grads["norm1_g"], grads["norm2_g"] = dg1[0], dg2[0]
    dmod = jnp.concatenate([dshift1, dscale1, dgate1, dshift2, dscale2, dgate2], axis=-1)[:, 0, :]
    return dx, dmod, grads


def _tail_loss(x2, final_norm_g, target2):
    return loss_head(rms_norm(x2, final_norm_g[None, :], 1), target2)


def _forward_backward(x, mod, small, gathered, reducer, final_norm_g, target):
    b, s, d = x.shape
    depth = len(small)
    rope_a = _axial_rope_tables(s, GQA_HEAD_DIM)
    rope_b = _axial_rope_tables(s, MLA_ROPE_DIM)
    sched = _gather_schedule(depth)
    first = sched['first']
    gathered.store(first, _run_plan(gathered.plan(first), "all_gather_chips"))
    saved = []
    for l in range(depth):
        x, res = _layer_fwd(x, mod[l], small[l], gathered, l, sched, rope_a, rope_b)
        saved.append(res)
    loss, (dx2, dfinal) = jax.value_and_grad(_tail_loss, argnums=(0, 1))(
        x.reshape(b * s, d), final_norm_g, target.reshape(b * s, d))
    dx = dx2.reshape(b, s, d)
    dmods, gsmall = [None] * depth, [None] * depth
    for l in reversed(range(depth)):
        dx, dmods[l], gsmall[l] = _layer_bwd(saved[l], gathered, reducer, l, depth, dx)
    return loss, dx, jnp.stack(dmods), gsmall, dfinal


ANY = pl.BlockSpec(memory_space=pl.ANY)


def _flip_if(v, bit):
    return 1 - v if bit else v


def _all_gather_devices(x):
    def body(x_ref, out_ref, send_sems, recv_sems):
        mx, my, mc = lax.axis_index("x"), lax.axis_index("y"), lax.axis_index("c")
        me = 4 * mx + 2 * my + mc
        sends = []
        for k in range(1, N_DEV):
            peer = (_flip_if(mx, k & 4), _flip_if(my, k & 2), _flip_if(mc, k & 1))
            cp = pltpu.make_async_remote_copy(src_ref=x_ref, dst_ref=out_ref.at[me], send_sem=send_sems.at[k - 1],
                                              recv_sem=recv_sems.at[k - 1], device_id=peer, device_id_type=MESH)
            cp.start()
            sends.append(cp)
        for k in range(1, N_DEV):
            peer = (_flip_if(mx, k & 4), _flip_if(my, k & 2), _flip_if(mc, k & 1))
            src = 4 * peer[0] + 2 * peer[1] + peer[2]
            pltpu.make_async_remote_copy(src_ref=x_ref, dst_ref=out_ref.at[src], send_sem=send_sems.at[k - 1],
                                         recv_sem=recv_sems.at[k - 1], device_id=peer, device_id_type=MESH).wait_recv()
        for cp in sends:
            cp.wait_send()

    out = pl.pallas_call(
        body, name="all_gather_devices", in_specs=[ANY], out_specs=ANY,
        out_shape=jax.ShapeDtypeStruct((N_DEV,) + x.shape, x.dtype),
        scratch_shapes=[pltpu.SemaphoreType.DMA((N_DEV - 1,)), pltpu.SemaphoreType.DMA((N_DEV - 1,))],
    )(x)
    me = 4 * lax.axis_index("x") + 2 * lax.axis_index("y") + lax.axis_index("c")
    return lax.dynamic_update_index_in_dim(out, x, me, 0)


class _Plan:
    def __init__(self, inputs, out_shapes, sem_counts, start, finish):
        self.inputs, self.out_shapes, self.sem_counts = list(inputs), list(out_shapes), list(sem_counts)
        self.start, self.finish = start, finish

    def specs(self):
        return ([ANY] * len(self.inputs), [ANY] * len(self.out_shapes),
                [pltpu.SemaphoreType.DMA((c,)) for c in self.sem_counts])

    def split(self, refs):
        a, b = len(self.inputs), len(self.inputs) + len(self.out_shapes)
        return refs[:a], refs[a:b], refs[b:]


def _run_plan(plan, name):
    def body(*refs):
        ins, outs, sems = plan.split(refs)
        plan.start(ins, outs, sems)
        plan.finish(ins, outs, sems)

    in_specs, out_specs, scratch = plan.specs()
    return pl.pallas_call(body, name=name, in_specs=in_specs, out_specs=out_specs, out_shape=plan.out_shapes,
                          scratch_shapes=scratch)(*plan.inputs)


def _gather_plan(shards):
    n = len(shards)
    halves = [t.reshape(2, t.shape[0] // 2, t.shape[1]) for t in shards]
    count = (N_CHIPS - 1) * n

    def copies(kind, ins, outs, sems):
        ici_send, ici_recv, d2d_send, d2d_recv, own_send, own_recv = sems
        mx, my, mc = lax.axis_index("x"), lax.axis_index("y"), lax.axis_index("c")
        me = 2 * mx + my
        sibling = (mx, my, 1 - mc)
        if kind == 'own':
            return [pltpu.make_async_remote_copy(src_ref=ins[i], dst_ref=outs[i].at[me], send_sem=own_send.at[i],
                                                 recv_sem=own_recv.at[i], device_id=sibling, device_id_type=MESH)
                    for i in range(n)]
        cps = []
        for k in range(1, N_CHIPS):
            peer = (_flip_if(mx, k & 2), _flip_if(my, k & 1), mc)
            src = 2 * peer[0] + peer[1]
            for i in range(n):
                j = (k - 1) * n + i
                if kind in ('ici', 'landed'):
                    dst = outs[i].at[me, mc] if kind == 'ici' else outs[i].at[src, mc]
                    cps.append(pltpu.make_async_remote_copy(
                        src_ref=ins[i].at[mc], dst_ref=dst, send_sem=ici_send.at[j], recv_sem=ici_recv.at[j],
                        device_id=peer, device_id_type=MESH))
                else:
                    half = outs[i].at[src, mc] if kind == 'fwd' else outs[i].at[src, 1 - mc]
                    cps.append(pltpu.make_async_remote_copy(
                        src_ref=half, dst_ref=half, send_sem=d2d_send.at[j], recv_sem=d2d_recv.at[j],
                        device_id=sibling, device_id_type=MESH))
        return cps

    def start(ins, outs, sems):
        for cp in copies('own', ins, outs, sems) + copies('ici', ins, outs, sems):
            cp.start()

    def finish(ins, outs, sems):
        fwd = copies('fwd', ins, outs, sems)
        for arrived, onward in zip(copies('landed', ins, outs, sems), fwd):
            arrived.wait_recv()
            onward.start()
        own = copies('own', ins, outs, sems)
        for cp in copies('fwd_in', ins, outs, sems) + own:
            cp.wait_recv()
        for cp in own + copies('ici', ins, outs, sems) + fwd:
            cp.wait_send()

    out_shapes = [jax.ShapeDtypeStruct((N_CHIPS,) + t.shape, t.dtype) for t in halves]
    return _Plan(halves, out_shapes, [count] * 4 + [n] * 2, start, finish)


def _sibling_exchange(blocks, name):
    n = len(blocks)

    def body(*refs):
        ins, outs = refs[:n], refs[n:2 * n]
        send_sems, recv_sems = refs[2 * n:]
        mx, my, mc = lax.axis_index("x"), lax.axis_index("y"), lax.axis_index("c")
        cps = []
        for i in range(n):
            cp = pltpu.make_async_remote_copy(src_ref=ins[i], dst_ref=outs[i], send_sem=send_sems.at[i],
                                              recv_sem=recv_sems.at[i], device_id=(mx, my, 1 - mc),
                                              device_id_type=MESH)
            cp.start()
            cps.append(cp)
        for cp in cps:
            cp.wait()

    return pl.pallas_call(
        body, name=name, in_specs=[ANY] * n, out_specs=[ANY] * n,
        out_shape=[jax.ShapeDtypeStruct(t.shape, t.dtype) for t in blocks],
        scratch_shapes=[pltpu.SemaphoreType.DMA((n,)), pltpu.SemaphoreType.DMA((n,))],
    )(*blocks)


def _add_halves(own, recv):
    nb, r, c = own.shape
    tr = _row_tile(r, c * 4)

    def body(g_ref, r_ref, o_ref):
        o_ref[...] = (g_ref[...] + r_ref[...].astype(F32)).astype(BF16)

    spec = pl.BlockSpec((1, tr, c), lambda k, i: (k, i, 0))
    return pl.pallas_call(
        body, name="rs_add_halves", grid=(nb, r // tr), in_specs=[spec, spec], out_specs=spec,
        out_shape=jax.ShapeDtypeStruct((nb, r, c), BF16),
        compiler_params=pltpu.CompilerParams(dimension_semantics=("parallel", "parallel")),
    )(own, recv)


def _chip_exchange_plan(jobs):
    n = len(jobs)
    count = (N_CHIPS - 1) * n

    def copies(ins, outs, sems):
        send_sems, recv_sems = sems
        mx, my, mc = lax.axis_index("x"), lax.axis_index("y"), lax.axis_index("c")
        cps = []
        for k in range(1, N_CHIPS):
            peer = (_flip_if(mx, k & 2), _flip_if(my, k & 1), mc)
            dst_chip = 2 * peer[0] + peer[1]
            for i, (_, row0, rows) in enumerate(jobs):
                j = (k - 1) * n + i
                cps.append(pltpu.make_async_remote_copy(
                    src_ref=ins[i].at[dst_chip, pl.ds(row0, rows)], dst_ref=outs[i].at[k - 1],
                    send_sem=send_sems.at[j], recv_sem=recv_sems.at[j], device_id=peer, device_id_type=MESH))
        return cps

    def start(ins, outs, sems):
        for cp in copies(ins, outs, sems):
            cp.start()

    def finish(ins, outs, sems):
        for cp in copies(ins, outs, sems):
            cp.wait()

    out_shapes = [jax.ShapeDtypeStruct((N_CHIPS - 1, rows, p.shape[2]), p.dtype) for p, _, rows in jobs]
    return _Plan([p for p, _, _ in jobs], out_shapes, [count, count], start, finish)


def _sum_chips(parts, recv, chip, row0):
    _, rows, c = recv.shape
    tr = _row_tile(rows, c * 4)
    assert row0 % tr == 0

    def body(chip_ref, p_ref, r_ref, o_ref):
        acc = p_ref[0].astype(F32)
        for k in range(N_CHIPS - 1):
            acc = acc + r_ref[k].astype(F32)
        o_ref[...] = acc

    return pl.pallas_call(
        body, name="rs_sum_chips",
        grid_spec=pltpu.PrefetchScalarGridSpec(
            num_scalar_prefetch=1, grid=(rows // tr,),
            in_specs=[pl.BlockSpec((1, tr, c), lambda i, chip_ref: (chip_ref[0], i + row0 // tr, 0)),
                      pl.BlockSpec((N_CHIPS - 1, tr, c), lambda i, chip_ref: (0, i, 0))],
            out_specs=pl.BlockSpec((tr, c), lambda i, chip_ref: (i, 0))),
        out_shape=jax.ShapeDtypeStruct((rows, c), F32),
        compiler_params=pltpu.CompilerParams(dimension_semantics=("parallel",)),
    )(chip, parts, recv)


def _sum_leading(t, name):
    nb, r, c = t.shape
    tr = _row_tile(r, c * 4 * nb)

    def body(t_ref, o_ref):
        acc = t_ref[0]
        for k in range(1, nb):
            acc = acc + t_ref[k]
        o_ref[...] = acc

    return pl.pallas_call(
        body, name=name, grid=(r // tr,),
        in_specs=[pl.BlockSpec((nb, tr, c), lambda i: (0, i, 0))],
        out_specs=pl.BlockSpec((tr, c), lambda i: (i, 0)),
        out_shape=jax.ShapeDtypeStruct((r, c), F32),
        compiler_params=pltpu.CompilerParams(dimension_semantics=("parallel",)),
    )(t)


def _rs_parts(grads):
    mc = lax.axis_index("c")
    split = [g.reshape(g.shape[0], 2, g.shape[1] // 2, g.shape[2]) for g in grads]
    own = [lax.dynamic_index_in_dim(g, mc, axis=1, keepdims=False) for g in split]
    away = [lax.dynamic_index_in_dim(g, 1 - mc, axis=1, keepdims=False).astype(BF16) for g in split]
    return [_add_halves(o, r) for o, r in zip(own, _sibling_exchange(away, "rs_sibling_exchange"))]


def _rs_result(items):
    mc = lax.axis_index("c")
    chip = (2 * lax.axis_index("x") + lax.axis_index("y")).astype(jnp.int32).reshape(1)
    mine = []
    for parts, pieces in items:
        done = [_sum_chips(parts, recv, chip, row0) for row0, recv in pieces]
        mine.append(done[0] if len(done) == 1 else jnp.concatenate(done, axis=0))
    theirs = _sibling_exchange(mine, "rs_sibling_swap")
    return [jnp.concatenate([jnp.where(mc == 0, a, b), jnp.where(mc == 0, b, a)], axis=0)
            for a, b in zip(mine, theirs)]


def _adamw(w, g, m, v, plan=None):
    shape = w.shape
    cols = shape[-1]
    if len(shape) == 3:
        lead, rows = shape[0], shape[1]
    else:
        lead, rows = 1, (int(np.prod(shape[:-1])) if len(shape) > 1 else 1)
    w2, g2, m2, v2 = [t.reshape(lead, rows, cols) for t in (w, g, m, v)]
    tr = _row_tile(rows, cols * 4, ADAM_TILE_BYTES)
    t0, tc = 1, cols
    if tr == rows and rows * cols * 4 > ADAM_TILE_BYTES and cols % LANES == 0:
        tc = max(t for t in _tile_cands(cols, cols) if t == LANES or rows * t * 4 <= ADAM_TILE_BYTES)
    elif rows < 8 and cols % LANES == 0:
        fits = [(a * t, t, a) for a in range(1, lead + 1) if lead % a == 0 for t in _tile_cands(cols, cols)
                if a * rows * t * 4 <= ADAM_TILE_BYTES]
        _, tc, t0 = max(fits) if fits else (0, LANES, 1)
    per = (rows // tr) * (cols // tc)
    steps = (lead // t0) * per
    p_in, p_out, p_shapes, p_scratch, p_args = _plan_specs(plan)

    def body(w_ref, g_ref, m_ref, v_ref, *rest):
        d_ref, mo_ref, vo_ref = rest[len(p_in):len(p_in) + 3]
        i = pl.program_id(0) * per + pl.program_id(1)

        def compute():
            gv = g_ref[...]
            mn = ADAM_B1 * m_ref[...] + (1.0 - ADAM_B1) * gv
            vn = ADAM_B2 * v_ref[...] + (1.0 - ADAM_B2) * (gv * gv)
            m_hat = mn / (1.0 - ADAM_B1 ** ADAM_STEP)
            v_hat = vn / (1.0 - ADAM_B2 ** ADAM_STEP)
            d_ref[...] = -ADAM_LR * (m_hat / (jnp.sqrt(v_hat) + ADAM_EPS) + ADAM_WD * w_ref[...])
            mo_ref[...] = mn
            vo_ref[...] = vn

        _ride(plan, rest[:len(p_in)] + rest[len(p_in) + 3:], i == 0, i == steps - 1, compute)

    col_blocks = cols // tc
    spec = pl.BlockSpec((t0, tr, tc), lambda a, i: (a, i // col_blocks, i % col_blocks))
    sem = "arbitrary" if plan is not None else "parallel"
    outs = pl.pallas_call(
        body, name="adamw", grid=(lead // t0, per), in_specs=[spec] * 4 + p_in, out_specs=[spec] * 3 + p_out,
        out_shape=[jax.ShapeDtypeStruct((lead, rows, cols), F32)] * 3 + p_shapes, scratch_shapes=p_scratch,
        compiler_params=pltpu.CompilerParams(dimension_semantics=(sem, sem), vmem_limit_bytes=ADAM_VMEM_LIMIT),
    )(w2, g2, m2, v2, *p_args)
    res = [t.reshape(shape) for t in outs[:3]]
    return res if plan is None else (res, list(outs[3:]))


WEIGHTS = ['w_ada', 'b_ada', 'norm1_g', 'norm2_g', 'w_in', 'q_norm_g', 'k_norm_g', 'mla_q_norm_g', 'w_uq',
           'mla_kv_norm_g', 'w_ukv', 'conv_w', 'conv_b', 'dt_bias', 'a_log', 'd_skip', 'ssd_norm_g', 'w_out',
           'w_gate_up', 'w_down', 'final_norm_g']
COL_SHARDED = ('w_in', 'w_uq', 'w_ukv', 'w_gate_up')
ROW_SHARDED = ('w_out', 'w_down')
CHIP_BLOCKED = ('w_gate_up',)
TRANSPOSED = ('w_in',)
SMALL_LAYER = ('norm1_g', 'norm2_g', 'q_norm_g', 'k_norm_g', 'mla_q_norm_g', 'mla_kv_norm_g', 'conv_w', 'conv_b',
               'dt_bias', 'a_log', 'd_skip', 'ssd_norm_g')


def _pack(parts):
    flat = jnp.concatenate([p.reshape(-1) for p in parts])
    n = flat.shape[0]
    rows = -(-n // (8 * LANES)) * 8
    return jnp.pad(flat, (0, rows * LANES - n)).reshape(rows, LANES)


def _unpack(flat, shapes):
    out, pos = [], 0
    for shp in shapes:
        size = int(np.prod(shp))
        out.append(flat[pos:pos + size].reshape(shp))
        pos += size
    return out


def _cols_full(gathered):
    k, r, c = gathered.shape
    return jnp.transpose(gathered, (1, 0, 2)).reshape(r, k * c)


def _cols_split(full):
    r, c4 = full.shape
    return jnp.transpose(full.reshape(r, N_CHIPS, c4 // N_CHIPS), (1, 0, 2))


def kernel(x, c, w_ada, b_ada, norm1_g, norm2_g, w_in, q_norm_g, k_norm_g, mla_q_norm_g, w_uq, mla_kv_norm_g, w_ukv, conv_w, conv_b, dt_bias, a_log, d_skip, ssd_norm_g, w_out, w_gate_up, w_down, final_norm_g, loss_target, m_w_ada, m_b_ada, m_norm1_g, m_norm2_g, m_w_in, m_q_norm_g, m_k_norm_g, m_mla_q_norm_g, m_w_uq, m_mla_kv_norm_g, m_w_ukv, m_conv_w, m_conv_b, m_dt_bias, m_a_log, m_d_skip, m_ssd_norm_g, m_w_out, m_w_gate_up, m_w_down, m_final_norm_g, v_w_ada, v_b_ada, v_norm1_g, v_norm2_g, v_w_in, v_q_norm_g, v_k_norm_g, v_mla_q_norm_g, v_w_uq, v_mla_kv_norm_g, v_w_ukv, v_conv_w, v_conv_b, v_dt_bias, v_a_log, v_d_skip, v_ssd_norm_g, v_w_out, v_w_gate_up, v_w_down, v_final_norm_g):
    args = dict(locals())
    weights = {n: args[n] for n in WEIGHTS}
    depth = w_in.shape[0]
    bl, s, d = x.shape
    mx, my, mc = lax.axis_index("x"), lax.axis_index("y"), lax.axis_index("c")
    chip = 2 * mx + my
    dev = 2 * chip + mc
    ada_cols = w_ada.shape[-1]
    conv_cols = conv_w.shape[-1]

    first_shapes = [c.shape, conv_w.shape]
    first = _all_gather_devices(_pack([c, conv_w]))
    first = [_unpack(first[i].reshape(-1), first_shapes) for i in range(N_DEV)]
    c_act = jax.nn.silu(jnp.concatenate([f[0] for f in first], axis=0))
    conv_w_full = jnp.concatenate([first[2 * k][1] for k in range(N_CHIPS)], axis=-1)

    b_cols = lax.dynamic_slice_in_dim(b_ada, chip * ada_cols, ada_cols, axis=1)
    c_act_b = c_act.astype(BF16)
    mod_cols = jnp.stack([_matmul(c_act_b, w_ada[l], name="ada_fwd") + b_cols[l][None, :]
                          for l in range(depth)])
    mod_all = _all_gather_devices(mod_cols.reshape(depth * N_DEV * bl, ada_cols))
    mod_all = mod_all.reshape(N_DEV, depth, N_DEV, bl, ada_cols)
    mod_mine = lax.dynamic_index_in_dim(mod_all, dev, axis=2, keepdims=False)
    mod = jnp.concatenate([mod_mine[2 * k] for k in range(N_CHIPS)], axis=-1)

    big = COL_SHARDED + ROW_SHARDED
    flip = lambda t: jnp.transpose(t, (2, 0, 1))
    unflip = lambda t: jnp.transpose(t, (1, 2, 0))
    shards = {(n, l): weights[n][l].astype(BF16) for n in big for l in range(depth)}
    for l in range(depth):
        shards[('w_in', l)] = jnp.pad(flip(w_in)[:, l, :].astype(BF16), ((0, IN_SHARD_PAD - IN_SHARD), (0, 0)))
    gathered = _Gathered(shards)
    reducer = _Reducer()
    small_w = []
    for l in range(depth):
        w = {n: weights[n][l] for n in SMALL_LAYER if n != 'conv_w'}
        w['conv_w'] = conv_w_full[l]
        small_w.append(w)
    loss_local, gx, gmod, glayers, gfinal = _forward_backward(x, mod, small_w, gathered, reducer, final_norm_g,
                                                              loss_target)

    small_parts = [jnp.stack([glayers[l][n] for l in range(depth)]) for n in SMALL_LAYER]
    small_parts += [gfinal, loss_local.reshape(1), gmod]
    small_shapes = [p.shape for p in small_parts]
    last = _all_gather_devices(_pack(small_parts))
    summed = _unpack(_sum_leading(last, "sum_devices").reshape(-1), small_shapes)
    small = dict(zip(SMALL_LAYER, summed[:len(SMALL_LAYER)]))
    g_final, loss, gmod_sum = summed[len(SMALL_LAYER):]
    small['conv_w'] = lax.dynamic_slice_in_dim(small['conv_w'], chip * conv_cols, conv_cols, axis=2)
    gmod_all = jnp.stack([_unpack(last[i].reshape(-1), small_shapes)[-1] for i in range(N_DEV)], axis=1)
    gmod_all = gmod_all.reshape(depth, N_DEV * bl, gmod.shape[-1])
    gmod_cols = lax.dynamic_slice_in_dim(gmod_all, chip * ada_cols, ada_cols, axis=2)
    g_w_ada = jnp.stack([_matmul(c_act_b, gmod_cols[l].astype(BF16), ta=True, name="ada_dw") for l in range(depth)])
    g_b_ada = gmod_sum[:, 0]
    for i in range(1, bl):
        g_b_ada = g_b_ada + gmod_sum[:, i]

    grad = {'w_ada': g_w_ada, 'b_ada': g_b_ada, 'final_norm_g': g_final}
    for n in SMALL_LAYER:
        grad[n] = small[n]

    delta, new_m, new_v = {}, {}, {}
    left = [key for key in reducer.parts if key not in reducer.result and not reducer.recv[key]]
    delta['w_ada'], new_m['w_ada'], new_v['w_ada'] = reducer.carry(
        left, lambda p: _adamw(w_ada, g_w_ada, m_w_ada, v_w_ada, plan=p))
    reducer.flush()
    for n in big:
        if n in TRANSPOSED:
            grad[n] = jnp.stack([reducer.result[(n, l)][:weights[n].shape[2]] for l in range(depth)], axis=1)
        else:
            grad[n] = jnp.stack([reducer.result[(n, l)] for l in range(depth)])
    for n in WEIGHTS:
        if n in TRANSPOSED:
            outs = _adamw(flip(weights[n]), grad[n], flip(args["m_" + n]), flip(args["v_" + n]))
            grad[n], delta[n], new_m[n], new_v[n] = [unflip(t) for t in [grad[n]] + outs]
        elif n != 'w_ada':
            delta[n], new_m[n], new_v[n] = _adamw(weights[n], grad[n], args["m_" + n], args["v_" + n])
    return (loss.reshape(()), gx, *[grad[n] for n in WEIGHTS], *[delta[n] for n in WEIGHTS],
            *[new_m[n] for n in WEIGHTS], *[new_v[n] for n in WEIGHTS])
```

```python
import functools

import numpy as np
import jax
import jax.numpy as jnp
from jax import lax
from jax.experimental import pallas as pl
from jax.experimental.pallas import tpu as pltpu

F32 = jnp.float32
BF16 = jnp.bfloat16
HIGHEST = lax.Precision.HIGHEST
MESH = pl.DeviceIdType.MESH

GRID_W = 64
ROPE_THETA = 10000.0
EPS = 1e-6

GQA_HEADS, GQA_KV_HEADS, GQA_HEAD_DIM = 6, 2, 128
GQA_WIDTH = GQA_HEADS * GQA_HEAD_DIM
GQA_KV_WIDTH = GQA_KV_HEADS * GQA_HEAD_DIM
MLA_HEADS, MLA_Q_LORA, MLA_KV_LORA = 4, 512, 256
MLA_NOPE_DIM, MLA_ROPE_DIM, MLA_V_DIM = 128, 64, 128
SSD_HEADS, SSD_HEAD_DIM, SSD_GROUPS, SSD_STATE, SSD_CONV, SSD_CHUNK = 12, 64, 2, 128, 5, 128
SSD_INNER = SSD_HEADS * SSD_HEAD_DIM
SSD_CONV_DIM = SSD_INNER + 2 * SSD_GROUPS * SSD_STATE
SSD_GROUP_HEADS = SSD_HEADS // SSD_GROUPS
SSD_GROUP_WIDTH = SSD_GROUP_HEADS * SSD_HEAD_DIM
IN_SPLITS = (GQA_WIDTH, GQA_KV_WIDTH, GQA_KV_WIDTH, MLA_Q_LORA, MLA_KV_LORA, MLA_ROPE_DIM, SSD_INNER, SSD_CONV_DIM,
             2 * SSD_HEADS)
IN_COLS = sum(IN_SPLITS)
LANES = 128
N_CHIPS = 4
IN_SHARD = IN_COLS // N_CHIPS
IN_SHARD_PAD = -(-IN_SHARD // LANES) * LANES


def _in_cols(proj, lo, hi):
    parts = []
    for chip in range(lo // IN_SHARD, (hi - 1) // IN_SHARD + 1):
        a, z = max(lo, chip * IN_SHARD), min(hi, (chip + 1) * IN_SHARD)
        base = chip * IN_SHARD_PAD - chip * IN_SHARD
        parts.append(proj[:, base + a:base + z])
    return parts[0] if len(parts) == 1 else jnp.concatenate(parts, axis=-1)

ADAM_LR, ADAM_B1, ADAM_B2, ADAM_EPS, ADAM_WD, ADAM_STEP = 0.001, 0.9, 0.999, 1e-08, 0.01, 10

N_DEV = 8
TILE_BYTES = 2 * 1024 * 1024


def _pick(n, cands):
    for t in cands:
        if n % t == 0:
            return t
    return n


ADAM_TILE_BYTES = 2 * 1024 * 1024
ADAM_VMEM_LIMIT = 48 * 1024 * 1024
SWIGLU_TILE_BYTES = 4 * 1024 * 1024


def _row_tile(rows, row_bytes, limit=TILE_BYTES):
    for t in (2048, 1024, 512, 256, 128, 64, 32, 16, 8):
        if rows % t == 0 and t * row_bytes <= limit:
            return t
    return rows


MM_VMEM_BUDGET = 36 * 1024 * 1024
MM_VMEM_LIMIT = 56 * 1024 * 1024
MM_MAX_TILE = 2048
MM_MAX_K_TILE = 4096
MXU_DIM = 256
HBM_BYTES_PER_US = 3.0e6
MXU_FLOPS_PER_US = 9.0e8
STEP_US = 0.35


def _tile_cands(d, cap):
    if d % LANES:
        return [d]
    return [t for t in range(LANES, min(d, cap) + 1, LANES) if d % t == 0] or [d]


def _mm_tiles(m, n, kdim, n_unit=None, k_unit=None):
    up = lambda t: -(-t // MXU_DIM) * MXU_DIM
    best = None
    for tm in _tile_cands(m, MM_MAX_TILE):
        for tn in _tile_cands(n_unit or n, MM_MAX_TILE):
            for tk in _tile_cands(k_unit or kdim, MM_MAX_K_TILE):
                if 2 * (tm * tk * 2 + tk * tn * 2 + tm * tn * 4) > MM_VMEM_BUDGET:
                    continue
                ni, nj, nk = m // tm, n // tn, kdim // tk
                a_reads = 1 if nk == 1 else nj
                b_reads = 1 if (nk == 1 and nj == 1) else ni
                hbm = (m * kdim * 2 * a_reads + kdim * n * 2 * b_reads + m * n * 4) / HBM_BYTES_PER_US
                mxu = ni * nj * nk * 2.0 * max(tm, 8) * up(tn) * up(tk) / MXU_FLOPS_PER_US
                cost = max(hbm, mxu) + 0.25 * min(hbm, mxu) + ni * nj * nk * STEP_US
                if best is None or cost < best[0]:
                    best = (cost, tm, tn, tk)
    return best[1:]


def _ride(plan, refs, first, last, compute):
    if plan is None:
        compute()
        return
    ins, outs, sems = plan.split(refs)

    @pl.when(first)
    def _():
        plan.start(ins, outs, sems)

    compute()

    @pl.when(last)
    def _():
        plan.finish(ins, outs, sems)


def _plan_specs(plan):
    if plan is None:
        return [], [], [], [], []
    in_specs, out_specs, scratch = plan.specs()
    return in_specs, out_specs, plan.out_shapes, scratch, plan.inputs


def _matmul(a, b, ta=False, tb=False, name="mm", plan=None, chips=None):
    assert a.dtype == BF16 and b.dtype in (BF16, F32), (a.dtype, b.dtype)
    if ta:
        kdim, m = a.shape
    else:
        m, kdim = a.shape
    n_unit = k_unit = None
    if chips == 'b':
        nb, rows, unit = b.shape
        if tb:
            n, k2, k_unit = rows, nb * unit, unit
        else:
            k2, n, n_unit = rows, nb * unit, unit
    else:
        if tb:
            n, k2 = b.shape
        else:
            k2, n = b.shape
        if chips == 'out':
            n_unit = n // N_CHIPS
    assert kdim == k2, (a.shape, b.shape, ta, tb)
    tm, tn, tk = _mm_tiles(m, n, kdim, n_unit, k_unit)
    ni, nj, nk = m // tm, n // tn, kdim // tk
    dn = (((0 if ta else 1,), (1 if tb else 0,)), ((), ()))
    p_in, p_out, p_shapes, p_scratch, p_args = _plan_specs(plan)

    def body(a_ref, b_ref, *rest):
        o_ref = rest[len(p_in)]
        i, j, k = pl.program_id(0), pl.program_id(1), pl.program_id(2)

        def compute():
            bv = (b_ref[0] if chips == 'b' else b_ref[...]).astype(BF16)
            part = lax.dot_general(a_ref[...], bv, dn, preferred_element_type=F32)
            if chips == 'out':
                part = part[None]
            if nk == 1:
                o_ref[...] = part
            else:
                @pl.when(k == 0)
                def _():
                    o_ref[...] = part

                @pl.when(k > 0)
                def _():
                    o_ref[...] += part

        _ride(plan, rest[:len(p_in)] + rest[len(p_in) + 1:], (i == 0) & (j == 0) & (k == 0),
              (i == ni - 1) & (j == nj - 1) & (k == nk - 1), compute)

    a_spec = pl.BlockSpec((tk, tm), lambda i, j, k: (k, i)) if ta else pl.BlockSpec((tm, tk), lambda i, j, k: (i, k))
    if chips == 'b' and tb:
        per = k_unit // tk
        b_spec = pl.BlockSpec((1, tn, tk), lambda i, j, k: (k // per, j, k % per))
    elif chips == 'b':
        per = n_unit // tn
        b_spec = pl.BlockSpec((1, tk, tn), lambda i, j, k: (j // per, k, j % per))
    else:
        b_spec = pl.BlockSpec((tn, tk), lambda i, j, k: (j, k)) if tb else pl.BlockSpec((tk, tn), lambda i, j, k: (k, j))
    if chips == 'out':
        per = n_unit // tn
        o_spec = pl.BlockSpec((1, tm, tn), lambda i, j, k: (j // per, i, j % per))
        o_shape = jax.ShapeDtypeStruct((N_CHIPS, m, n_unit), F32)
    else:
        o_spec = pl.BlockSpec((tm, tn), lambda i, j, k: (i, j))
        o_shape = jax.ShapeDtypeStruct((m, n), F32)
    outs = pl.pallas_call(
        body, name=name, grid=(ni, nj, nk),
        in_specs=[a_spec, b_spec] + p_in, out_specs=[o_spec] + p_out,
        out_shape=[o_shape] + p_shapes, scratch_shapes=p_scratch,
        compiler_params=pltpu.CompilerParams(
            dimension_semantics=("arbitrary" if plan is not None else "parallel", "arbitrary", "arbitrary"),
            vmem_limit_bytes=MM_VMEM_LIMIT),
    )(a, b, *p_args)
    return outs[0] if plan is None else (outs[0], list(outs[1:]))


@jax.custom_vjp
def linear(x, w):
    return _matmul(x.astype(BF16), w.astype(BF16), name="linear_fwd")


def _linear_fwd(x, w):
    xb, wb = x.astype(BF16), w.astype(BF16)
    return _matmul(xb, wb, name="linear_fwd"), (xb, wb)


def _linear_bwd(res, dy):
    xb, wb = res
    dyb = dy.astype(BF16)
    return _matmul(dyb, wb, tb=True, name="linear_dx"), _matmul(xb, dyb, ta=True, name="linear_dw")


linear.defvjp(_linear_fwd, _linear_bwd)


def _rms_fwd_call(x, g, groups):
    rows, cols = x.shape
    d = cols // groups
    tr = _row_tile(rows, cols * 4)

    def body(x_ref, g_ref, y_ref):
        for gi in range(groups):
            sl = slice(gi * d, (gi + 1) * d)
            xs = x_ref[:, sl]
            r = lax.rsqrt(jnp.mean(xs * xs, axis=-1, keepdims=True) + EPS)
            y_ref[:, sl] = xs * r * g_ref[:, sl]

    return pl.pallas_call(
        body, name="rms_fwd", grid=(rows // tr,),
        in_specs=[pl.BlockSpec((tr, cols), lambda i: (i, 0)), pl.BlockSpec((1, cols), lambda i: (0, 0))],
        out_specs=pl.BlockSpec((tr, cols), lambda i: (i, 0)),
        out_shape=jax.ShapeDtypeStruct((rows, cols), F32),
        compiler_params=pltpu.CompilerParams(dimension_semantics=("parallel",)),
    )(x, g)


def _rms_bwd_call(x, g, dy, groups):
    rows, cols = x.shape
    d = cols // groups
    tr = _row_tile(rows, cols * 4)

    def body(x_ref, g_ref, dy_ref, dx_ref, dg_ref):
        @pl.when(pl.program_id(0) == 0)
        def _():
            dg_ref[...] = jnp.zeros_like(dg_ref)

        for gi in range(groups):
            sl = slice(gi * d, (gi + 1) * d)
            xs = x_ref[:, sl]
            dys = dy_ref[:, sl]
            r = lax.rsqrt(jnp.mean(xs * xs, axis=-1, keepdims=True) + EPS)
            xhat = xs * r
            dg_ref[:, sl] += jnp.sum(dys * xhat, axis=0, keepdims=True)
            dxhat = dys * g_ref[:, sl]
            dx_ref[:, sl] = r * (dxhat - xhat * jnp.mean(dxhat * xhat, axis=-1, keepdims=True))

    return pl.pallas_call(
        body, name="rms_bwd", grid=(rows // tr,),
        in_specs=[pl.BlockSpec((tr, cols), lambda i: (i, 0)), pl.BlockSpec((1, cols), lambda i: (0, 0)),
                  pl.BlockSpec((tr, cols), lambda i: (i, 0))],
        out_specs=[pl.BlockSpec((tr, cols), lambda i: (i, 0)), pl.BlockSpec((1, cols), lambda i: (0, 0))],
        out_shape=[jax.ShapeDtypeStruct((rows, cols), F32), jax.ShapeDtypeStruct((1, cols), F32)],
        compiler_params=pltpu.CompilerParams(dimension_semantics=("arbitrary",)),
    )(x, g, dy)


@functools.partial(jax.custom_vjp, nondiff_argnums=(2,))
def rms_norm(x, g, groups):
    return _rms_fwd_call(x, g, groups)


def _rms_norm_fwd(x, g, groups):
    return _rms_fwd_call(x, g, groups), (x, g)


def _rms_norm_bwd(groups, res, dy):
    x, g = res
    dx, dg = _rms_bwd_call(x, g, dy, groups)
    return dx, dg


rms_norm.defvjp(_rms_norm_fwd, _rms_norm_bwd)


NT_DIMS = (((1,), (1,)), ((), ()))
TN_DIMS = (((0,), (0,)), ((), ()))


LOG2E = 1.4426950408889634
ATTN_VMEM_LIMIT = 60 * 1024 * 1024

def _exp_rows(q, k, scale):
    s2 = lax.dot_general(q, k, NT_DIMS, preferred_element_type=F32) * (scale * LOG2E)
    e = jnp.exp2(s2 - jnp.max(s2, axis=-1, keepdims=True))
    return e, 1.0 / jnp.sum(e, axis=-1, keepdims=True)


def _attn_fwd_call(q, k, v, scale):
    b, h, s, dk = q.shape
    hkv, dv = k.shape[1], v.shape[3]
    rep = h // hkv
    tq = _pick(s, (1024, 512, 256, 128))

    def body(q_ref, k_ref, v_ref, o_ref):
        e, inv = _exp_rows(q_ref[0, 0], k_ref[0, 0], scale)
        o_ref[0, 0] = jnp.dot(e.astype(BF16), v_ref[0, 0], preferred_element_type=F32) * inv

    return pl.pallas_call(
        body, name="attn_fwd", grid=(b, h, s // tq),
        in_specs=[pl.BlockSpec((1, 1, tq, dk), lambda bi, hi, qi: (bi, hi, qi, 0)),
                  pl.BlockSpec((1, 1, s, dk), lambda bi, hi, qi: (bi, hi // rep, 0, 0)),
                  pl.BlockSpec((1, 1, s, dv), lambda bi, hi, qi: (bi, hi // rep, 0, 0))],
        out_specs=pl.BlockSpec((1, 1, tq, dv), lambda bi, hi, qi: (bi, hi, qi, 0)),
        out_shape=jax.ShapeDtypeStruct((b, h, s, dv), F32),
        compiler_params=pltpu.CompilerParams(dimension_semantics=("parallel", "parallel", "parallel"),
                                             vmem_limit_bytes=ATTN_VMEM_LIMIT),
    )(q, k, v)


def _attn_bwd_call(q, k, v, do, scale):
    b, h, s, dk = q.shape
    hkv, dv = k.shape[1], v.shape[3]
    rep = h // hkv
    tq = _pick(s, (1024, 512, 256, 128))

    def body(q_ref, k_ref, v_ref, do_ref, dq_ref, dk_ref, dv_ref):
        @pl.when((pl.program_id(2) == 0) & (pl.program_id(3) == 0))
        def _():
            dk_ref[...] = jnp.zeros_like(dk_ref)
            dv_ref[...] = jnp.zeros_like(dv_ref)

        qb = q_ref[0, 0]
        kb = k_ref[0, 0]
        vb = v_ref[0, 0]
        dob = do_ref[0, 0]
        e, inv = _exp_rows(qb, kb, scale)
        dp = lax.dot_general(dob, vb, NT_DIMS, preferred_element_type=F32)
        delta = jnp.sum(e * dp, axis=-1, keepdims=True) * inv
        ds = (e * ((dp - delta) * (inv * scale))).astype(BF16)
        dq_ref[0, 0] = jnp.dot(ds, kb, preferred_element_type=F32)
        dk_ref[0, 0] += lax.dot_general(ds, qb, TN_DIMS, preferred_element_type=F32)
        dv_ref[0, 0] += lax.dot_general(e.astype(BF16), (dob.astype(F32) * inv).astype(BF16), TN_DIMS,
                                        preferred_element_type=F32)

    return pl.pallas_call(
        body, name="attn_bwd", grid=(b, hkv, rep, s // tq),
        in_specs=[pl.BlockSpec((1, 1, tq, dk), lambda bi, gi, ri, qi: (bi, gi * rep + ri, qi, 0)),
                  pl.BlockSpec((1, 1, s, dk), lambda bi, gi, ri, qi: (bi, gi, 0, 0)),
                  pl.BlockSpec((1, 1, s, dv), lambda bi, gi, ri, qi: (bi, gi, 0, 0)),
                  pl.BlockSpec((1, 1, tq, dv), lambda bi, gi, ri, qi: (bi, gi * rep + ri, qi, 0))],
        out_specs=[pl.BlockSpec((1, 1, tq, dk), lambda bi, gi, ri, qi: (bi, gi * rep + ri, qi, 0)),
                   pl.BlockSpec((1, 1, s, dk), lambda bi, gi, ri, qi: (bi, gi, 0, 0)),
                   pl.BlockSpec((1, 1, s, dv), lambda bi, gi, ri, qi: (bi, gi, 0, 0))],
        out_shape=[jax.ShapeDtypeStruct(q.shape, F32), jax.ShapeDtypeStruct(k.shape, F32),
                   jax.ShapeDtypeStruct(v.shape, F32)],
        compiler_params=pltpu.CompilerParams(
            dimension_semantics=("parallel", "parallel", "arbitrary", "arbitrary"), vmem_limit_bytes=ATTN_VMEM_LIMIT),
    )(q, k, v, do)


@functools.partial(jax.custom_vjp, nondiff_argnums=(3,))
def attention(q, k, v, scale):
    return _attn_fwd_call(q.astype(BF16), k.astype(BF16), v.astype(BF16), scale)


def _attention_fwd(q, k, v, scale):
    qb, kb, vb = q.astype(BF16), k.astype(BF16), v.astype(BF16)
    return _attn_fwd_call(qb, kb, vb, scale), (qb, kb, vb)


def _attention_bwd(scale, res, do):
    qb, kb, vb = res
    return tuple(_attn_bwd_call(qb, kb, vb, do.astype(BF16), scale))


attention.defvjp(_attention_fwd, _attention_bwd)


CONV_COL_TILE = 256
CONV_PACK_ROWS = 8


def _shifted(x, off, rows):
    if off == 0:
        return x
    s = x.shape[0]
    rolled = pltpu.roll(x, (-off) % s, 0)
    valid = (rows + off >= 0) & (rows + off < s)
    return jnp.where(valid, rolled, 0.0)


def _conv_pre(x, wb_ref, rows):
    z = jnp.zeros_like(x) + wb_ref[SSD_CONV:SSD_CONV + 1, :]
    for j in range(SSD_CONV):
        z = z + wb_ref[j:j + 1, :] * _shifted(x, j - SSD_CONV // 2, rows)
    return z


def _conv_fwd_call(x, wb):
    b, s, c = x.shape
    tc = _pick(c, (CONV_COL_TILE, LANES))

    def body(x_ref, wb_ref, y_ref):
        xv = x_ref[0]
        rows = lax.broadcasted_iota(jnp.int32, xv.shape, 0)
        z = _conv_pre(xv, wb_ref, rows)
        y_ref[0] = z * jax.nn.sigmoid(z)

    return pl.pallas_call(
        body, name="conv_fwd", grid=(b, c // tc),
        in_specs=[pl.BlockSpec((1, s, tc), lambda bi, ci: (bi, 0, ci)),
                  pl.BlockSpec((CONV_PACK_ROWS, tc), lambda bi, ci: (0, ci))],
        out_specs=pl.BlockSpec((1, s, tc), lambda bi, ci: (bi, 0, ci)),
        out_shape=jax.ShapeDtypeStruct(x.shape, F32),
        compiler_params=pltpu.CompilerParams(dimension_semantics=("parallel", "parallel")),
    )(x, wb)


def _conv_bwd_call(x, wb, dy):
    b, s, c = x.shape
    tc = _pick(c, (CONV_COL_TILE, LANES))

    def body(x_ref, wb_ref, dy_ref, dx_ref, dwb_ref):
        xv = x_ref[0]
        rows = lax.broadcasted_iota(jnp.int32, xv.shape, 0)
        z = _conv_pre(xv, wb_ref, rows)
        sg = jax.nn.sigmoid(z)
        dz = dy_ref[0] * (sg * (1.0 + z * (1.0 - sg)))
        dx = jnp.zeros_like(xv)
        for j in range(SSD_CONV):
            off = j - SSD_CONV // 2
            dx = dx + wb_ref[j:j + 1, :] * _shifted(dz, -off, rows)
            dwb_ref[0, j:j + 1, :] = jnp.sum(dz * _shifted(xv, off, rows), axis=0, keepdims=True)
        dx_ref[0] = dx
        dwb_ref[0, SSD_CONV:SSD_CONV + 1, :] = jnp.sum(dz, axis=0, keepdims=True)
        dwb_ref[0, SSD_CONV + 1:, :] = jnp.zeros((CONV_PACK_ROWS - SSD_CONV - 1, dz.shape[1]), F32)

    return pl.pallas_call(
        body, name="conv_bwd", grid=(b, c // tc),
        in_specs=[pl.BlockSpec((1, s, tc), lambda bi, ci: (bi, 0, ci)),
                  pl.BlockSpec((CONV_PACK_ROWS, tc), lambda bi, ci: (0, ci)),
                  pl.BlockSpec((1, s, tc), lambda bi, ci: (bi, 0, ci))],
        out_specs=[pl.BlockSpec((1, s, tc), lambda bi, ci: (bi, 0, ci)),
                   pl.BlockSpec((1, CONV_PACK_ROWS, tc), lambda bi, ci: (bi, 0, ci))],
        out_shape=[jax.ShapeDtypeStruct(x.shape, F32), jax.ShapeDtypeStruct((b, CONV_PACK_ROWS, c), F32)],
        compiler_params=pltpu.CompilerParams(dimension_semantics=("parallel", "parallel")),
    )(x, wb, dy)


@jax.custom_vjp
def conv_silu(x, wb):
    return _conv_fwd_call(x, wb)


def _conv_silu_fwd(x, wb):
    return _conv_fwd_call(x, wb), (x, wb)


def _conv_silu_bwd(res, dy):
    x, wb = res
    dx, dwb = _conv_bwd_call(x, wb, dy)
    return dx, jnp.sum(dwb, axis=0)


conv_silu.defvjp(_conv_silu_fwd, _conv_silu_bwd)


SSD_PAIRS = SSD_GROUP_HEADS // 2
NEG_INF = -1e30


def _ssd_common(x_ref, dtx_ref, dtt_ref, anx_ref, anc_ref, b_ref, c_ref, reverse):
    L = SSD_CHUNK
    xv = x_ref[0]
    dt = dtx_ref[0]
    ri = lax.broadcasted_iota(jnp.int32, (L, L), 0)
    ci = lax.broadcasted_iota(jnp.int32, (L, L), 1)
    causal = (ri <= ci) if reverse else (ri >= ci)
    tri = causal.astype(F32)
    a_cs = jnp.dot(tri, dt * anx_ref[...], precision=HIGHEST, preferred_element_type=F32)
    a_row = dtt_ref[0, 0] * anc_ref[0]
    acs_row = lax.dot_general(a_row, tri, NT_DIMS, precision=HIGHEST, preferred_element_type=F32)
    xd = xv * dt
    bmat = b_ref[0].astype(BF16)
    cmat = c_ref[0].astype(BF16)
    gmat = lax.dot_general(cmat, bmat, NT_DIMS, preferred_element_type=F32)
    return xv, dt, causal, tri, a_cs, acs_row, xd, bmat, cmat, gmat


def _ssd_lambda(a_cs, acs_row, causal, h):
    col = a_cs[:, h * SSD_HEAD_DIM:h * SSD_HEAD_DIM + 1]
    row = acs_row[h:h + 1, :]
    return jnp.exp(jnp.where(causal, col - row, NEG_INF))


def _ssd_fwd_call(x, dtx, dtt, anx, anc, bm, cm, reverse):
    b, s, _ = x.shape
    L, N, GW = SSD_CHUNK, SSD_STATE, SSD_GROUP_WIDTH
    nc = s // L
    end = 0 if reverse else L - 1

    def body(x_ref, dtx_ref, dtt_ref, anx_ref, anc_ref, b_ref, c_ref, y_ref, hs_ref, state):
        @pl.when(pl.program_id(2) == 0)
        def _():
            state[...] = jnp.zeros_like(state)

        xv, dt, causal, tri, a_cs, acs_row, xd, bmat, cmat, gmat = _ssd_common(
            x_ref, dtx_ref, dtt_ref, anx_ref, anc_ref, b_ref, c_ref, reverse)
        hin = state[...]
        hs_ref[0, 0, 0] = hin
        y_off = jnp.dot(cmat, hin.astype(BF16), preferred_element_type=F32) * jnp.exp(a_cs)
        a_end = a_cs[end:end + 1, :]
        s_new = lax.dot_general(bmat, (xd * jnp.exp(a_end - a_cs)).astype(BF16), TN_DIMS, preferred_element_type=F32)
        state[...] = jnp.exp(a_end) * hin + s_new
        lane = lax.broadcasted_iota(jnp.int32, (L, LANES), 1)
        for pr in range(SSD_PAIRS):
            sl = slice(pr * LANES, (pr + 1) * LANES)
            xdp = xd[:, sl].astype(BF16)
            w0 = (gmat * _ssd_lambda(a_cs, acs_row, causal, 2 * pr)).astype(BF16)
            w1 = (gmat * _ssd_lambda(a_cs, acs_row, causal, 2 * pr + 1)).astype(BF16)
            y0 = jnp.dot(w0, xdp, preferred_element_type=F32)
            y1 = jnp.dot(w1, xdp, preferred_element_type=F32)
            y_ref[0, :, sl] = jnp.where(lane < SSD_HEAD_DIM, y0, y1) + y_off[:, sl]

    G = SSD_GROUPS
    chunk = (lambda c: nc - 1 - c) if reverse else (lambda c: c)
    seq = lambda bi, gi, c: (bi, chunk(c), gi)
    return pl.pallas_call(
        body, name="ssd_fwd", grid=(b, G, nc),
        in_specs=[pl.BlockSpec((1, L, GW), seq),
                  pl.BlockSpec((1, L, GW), seq),
                  pl.BlockSpec((1, 1, SSD_GROUP_HEADS, L), lambda bi, gi, c: (bi, gi, 0, chunk(c))),
                  pl.BlockSpec((1, GW), lambda bi, gi, c: (0, gi)),
                  pl.BlockSpec((1, SSD_GROUP_HEADS, 1), lambda bi, gi, c: (gi, 0, 0)),
                  pl.BlockSpec((1, L, N), seq),
                  pl.BlockSpec((1, L, N), seq)],
        out_specs=[pl.BlockSpec((1, L, GW), seq),
                   pl.BlockSpec((1, 1, 1, N, GW), lambda bi, gi, c: (bi, gi, chunk(c), 0, 0))],
        out_shape=[jax.ShapeDtypeStruct(x.shape, F32), jax.ShapeDtypeStruct((b, G, nc, N, GW), F32)],
        scratch_shapes=[pltpu.VMEM((N, GW), F32)],
        compiler_params=pltpu.CompilerParams(dimension_semantics=("parallel", "parallel", "arbitrary")),
    )(x, dtx, dtt, anx, anc, bm, cm)


def _ssd_bwd_call(x, dtx, dtt, anx, anc, bm, cm, hs, dy, reverse):
    b, s, _ = x.shape
    L, N, GW = SSD_CHUNK, SSD_STATE, SSD_GROUP_WIDTH
    nc = s // L
    end = 0 if reverse else L - 1

    def body(x_ref, dtx_ref, dtt_ref, anx_ref, anc_ref, b_ref, c_ref, hs_ref, dy_ref,
             dx_ref, ddt_ref, dan_ref, db_ref, dc_ref, dstate):
        @pl.when(pl.program_id(2) == 0)
        def _():
            dstate[...] = jnp.zeros_like(dstate)

        xv, dt, causal, tri, a_cs, acs_row, xd, bmat, cmat, gmat = _ssd_common(
            x_ref, dtx_ref, dtt_ref, anx_ref, anc_ref, b_ref, c_ref, reverse)
        hin = hs_ref[0, 0, 0]
        hinb = hin.astype(BF16)
        dyv = dy_ref[0]
        ds_out = dstate[...]
        dsb = ds_out.astype(BF16)
        eacs = jnp.exp(a_cs)
        a_end = a_cs[end:end + 1, :]
        e_end = jnp.exp(a_end)
        dec = jnp.exp(a_end - a_cs)
        dye = dyv * eacs
        dyeb = dye.astype(BF16)
        xdec = xd * dec
        ch = jnp.dot(cmat, hinb, preferred_element_type=F32)
        bds = jnp.dot(bmat, dsb, preferred_element_type=F32)
        t_state = xdec * bds
        d_aend = jnp.sum(t_state, axis=0, keepdims=True) + e_end * jnp.sum(ds_out * hin, axis=0, keepdims=True)
        dacs = dye * ch - t_state
        dxd_state = bds * dec
        dstate[...] = e_end * ds_out + lax.dot_general(cmat, dyeb, TN_DIMS, preferred_element_type=F32)

        lane = lax.broadcasted_iota(jnp.int32, (L, LANES), 1)
        dg = jnp.zeros((L, L), F32)
        dxd_parts, dacs_parts = [], []
        for pr in range(SSD_PAIRS):
            sl = slice(pr * LANES, (pr + 1) * LANES)
            xdp = xd[:, sl]
            dyp = dyv[:, sl]
            dxd_p = jnp.zeros((L, LANES), F32)
            dacs_p = jnp.zeros((L, LANES), F32)
            for half in range(2):
                mine = (lane < SSD_HEAD_DIM) if half == 0 else (lane >= SSD_HEAD_DIM)
                lam = _ssd_lambda(a_cs, acs_row, causal, 2 * pr + half)
                w = gmat * lam
                xdh = jnp.where(mine, xdp, 0.0).astype(BF16)
                dyh = jnp.where(mine, dyp, 0.0).astype(BF16)
                dw = lax.dot_general(dyh, xdh, NT_DIMS, preferred_element_type=F32)
                dg = dg + dw * lam
                mm = dw * w
                rs = jnp.sum(mm, axis=1, keepdims=True)
                cs = jnp.sum(mm.T, axis=1, keepdims=True)
                dacs_p = dacs_p + jnp.where(mine, (rs - cs) * (1.0 / SSD_HEAD_DIM), 0.0)
                wtdy = lax.dot_general(w.astype(BF16), dyh, TN_DIMS, preferred_element_type=F32)
                dxd_p = dxd_p + wtdy
            dxd_parts.append(dxd_p)
            dacs_parts.append(dacs_p)
        dxd = jnp.concatenate(dxd_parts, axis=1) + dxd_state
        dacs = dacs + jnp.concatenate(dacs_parts, axis=1)
        last = lax.broadcasted_iota(jnp.int32, dacs.shape, 0) == end
        dacs = dacs + jnp.where(last, d_aend, 0.0)
        da = lax.dot_general(tri, dacs, TN_DIMS, precision=HIGHEST, preferred_element_type=F32)
        dgb = dg.astype(BF16)
        dc_ref[0] = (jnp.dot(dgb, bmat, preferred_element_type=F32)
                     + lax.dot_general(dyeb, hinb, NT_DIMS, preferred_element_type=F32))
        db_ref[0] = (lax.dot_general(dgb, cmat, TN_DIMS, preferred_element_type=F32)
                     + lax.dot_general(xdec.astype(BF16), dsb, NT_DIMS, preferred_element_type=F32))
        dx_ref[0] = dxd * dt
        ddt_ref[0] = da * anx_ref[...] + dxd * xv
        dan_ref[0, 0, 0] = jnp.sum(da * dt, axis=0, keepdims=True)

    G = SSD_GROUPS
    chunk = (lambda c: c) if reverse else (lambda c: nc - 1 - c)
    rev = lambda bi, gi, c: (bi, chunk(c), gi)
    return pl.pallas_call(
        body, name="ssd_bwd", grid=(b, G, nc),
        in_specs=[pl.BlockSpec((1, L, GW), rev),
                  pl.BlockSpec((1, L, GW), rev),
                  pl.BlockSpec((1, 1, SSD_GROUP_HEADS, L), lambda bi, gi, c: (bi, gi, 0, chunk(c))),
                  pl.BlockSpec((1, GW), lambda bi, gi, c: (0, gi)),
                  pl.BlockSpec((1, SSD_GROUP_HEADS, 1), lambda bi, gi, c: (gi, 0, 0)),
                  pl.BlockSpec((1, L, N), rev),
                  pl.BlockSpec((1, L, N), rev),
                  pl.BlockSpec((1, 1, 1, N, GW), lambda bi, gi, c: (bi, gi, chunk(c), 0, 0)),
                  pl.BlockSpec((1, L, GW), rev)],
        out_specs=[pl.BlockSpec((1, L, GW), rev),
                   pl.BlockSpec((1, L, GW), rev),
                   pl.BlockSpec((1, 1, 1, 1, GW), lambda bi, gi, c: (bi, gi, chunk(c), 0, 0)),
                   pl.BlockSpec((1, L, N), rev),
                   pl.BlockSpec((1, L, N), rev)],
        out_shape=[jax.ShapeDtypeStruct(x.shape, F32), jax.ShapeDtypeStruct(x.shape, F32),
                   jax.ShapeDtypeStruct((b, G, nc, 1, GW), F32),
                   jax.ShapeDtypeStruct(bm.shape, F32), jax.ShapeDtypeStruct(cm.shape, F32)],
        scratch_shapes=[pltpu.VMEM((N, GW), F32)],
        compiler_params=pltpu.CompilerParams(dimension_semantics=("parallel", "parallel", "arbitrary")),
    )(x, dtx, dtt, anx, anc, bm, cm, hs, dy)


@functools.partial(jax.custom_vjp, nondiff_argnums=(7,))
def _ssd_scan(x, dtx, dtt, anx, anc, bm, cm, reverse):
    return _ssd_fwd_call(x, dtx, dtt, anx, anc, bm, cm, reverse)[0]


def _ssd_scan_fwd(x, dtx, dtt, anx, anc, bm, cm, reverse):
    y, hs = _ssd_fwd_call(x, dtx, dtt, anx, anc, bm, cm, reverse)
    return y, (x, dtx, dtt, anx, anc, bm, cm, hs)


def _ssd_scan_bwd(reverse, res, dy):
    x, dtx, dtt, anx, anc, bm, cm, hs = res
    dx, ddtx, dan, db, dc = _ssd_bwd_call(x, dtx, dtt, anx, anc, bm, cm, hs, dy, reverse)
    b, g, nc, _, gw = dan.shape
    danx = jnp.sum(dan, axis=(0, 2, 3)).reshape(1, g * gw)
    return dx, ddtx, jnp.zeros_like(dtt), danx, jnp.zeros_like(anc), db, dc


_ssd_scan.defvjp(_ssd_scan_fwd, _ssd_scan_bwd)


def ssd_chunked(xs, dt, a_neg, bm, cm, reverse):
    b, s, _ = xs.shape
    dtx = jnp.repeat(dt, SSD_HEAD_DIM, axis=-1)
    dtt = jnp.transpose(dt, (0, 2, 1)).reshape(b, SSD_GROUPS, SSD_GROUP_HEADS, s)
    anx = jnp.repeat(a_neg, SSD_HEAD_DIM)[None, :]
    anc = a_neg.reshape(SSD_GROUPS, SSD_GROUP_HEADS, 1)
    return _ssd_scan(xs, dtx, dtt, anx, anc, bm, cm, reverse)


def _loss_call(y, t):
    rows, cols = y.shape
    tr = _row_tile(rows, cols * 4)

    def body(y_ref, t_ref, loss_ref, diff_ref):
        @pl.when(pl.program_id(0) == 0)
        def _():
            loss_ref[...] = jnp.zeros_like(loss_ref)

        d = y_ref[...] - t_ref[...]
        diff_ref[...] = d * (1.0 / cols)
        part = jnp.sum(jnp.sum(d * d, axis=1, keepdims=True), axis=0, keepdims=True)
        loss_ref[...] += part * (0.5 / cols)

    return pl.pallas_call(
        body, name="loss_head", grid=(rows // tr,),
        in_specs=[pl.BlockSpec((tr, cols), lambda i: (i, 0)), pl.BlockSpec((tr, cols), lambda i: (i, 0))],
        out_specs=[pl.BlockSpec((1, 1), lambda i: (0, 0)), pl.BlockSpec((tr, cols), lambda i: (i, 0))],
        out_shape=[jax.ShapeDtypeStruct((1, 1), F32), jax.ShapeDtypeStruct((rows, cols), F32)],
        compiler_params=pltpu.CompilerParams(dimension_semantics=("arbitrary",)),
    )(y, t)


@jax.custom_vjp
def loss_head(y, t):
    return _loss_call(y, t)[0][0, 0]


def _loss_head_fwd(y, t):
    loss, diff = _loss_call(y, t)
    return loss[0, 0], diff


def _loss_head_bwd(diff, g):
    return g * diff, jnp.zeros_like(diff)


loss_head.defvjp(_loss_head_fwd, _loss_head_bwd)


def _axial_rope_tables(seq_len, rot_dim):
    rows = seq_len // GRID_W
    row_idx = jnp.repeat(jnp.arange(rows), GRID_W).astype(F32)
    col_idx = jnp.tile(jnp.arange(GRID_W), rows).astype(F32)
    axis_dim = rot_dim // 2
    inv_freq = jnp.power(ROPE_THETA, -jnp.arange(0, axis_dim, 2, dtype=F32) / axis_dim)
    ang_r = row_idx[:, None] * inv_freq[None, :]
    ang_c = col_idx[:, None] * inv_freq[None, :]
    return jnp.cos(ang_r), jnp.sin(ang_r), jnp.cos(ang_c), jnp.sin(ang_c)


def _rotate(x, cos, sin):
    x1, x2 = jnp.split(x, 2, axis=-1)
    cos = cos[:, None, :]
    sin = sin[:, None, :]
    return jnp.concatenate([x1 * cos - x2 * sin, x1 * sin + x2 * cos], axis=-1)


def _apply_axial_rope(x, tables):
    cos_r, sin_r, cos_c, sin_c = tables
    x_row, x_col = jnp.split(x, 2, axis=-1)
    return jnp.concatenate([_rotate(x_row, cos_r, sin_r), _rotate(x_col, cos_c, sin_c)], axis=-1)


def _heads_first(t):
    return jnp.transpose(t, (0, 2, 1, 3))


def _gqa_group(q, k, v, q_norm_g, k_norm_g, rope, b, s):
    q = rms_norm(q, jnp.tile(q_norm_g, GQA_HEADS)[None, :], GQA_HEADS).reshape(b, s, GQA_HEADS, GQA_HEAD_DIM)
    k = rms_norm(k, jnp.tile(k_norm_g, GQA_KV_HEADS)[None, :], GQA_KV_HEADS).reshape(b, s, GQA_KV_HEADS, GQA_HEAD_DIM)
    v = v.reshape(b, s, GQA_KV_HEADS, GQA_HEAD_DIM)
    q = _apply_axial_rope(q, rope)
    k = _apply_axial_rope(k, rope)
    o = attention(_heads_first(q), _heads_first(k), _heads_first(v), GQA_HEAD_DIM ** -0.5)
    return _heads_first(o).reshape(b * s, GQA_WIDTH)


def _mla_group(c_q, c_kv, k_pe, q_norm_g, w_uq, kv_norm_g, w_ukv, rope, b, s):
    q = linear(rms_norm(c_q, q_norm_g[None, :], 1), w_uq).reshape(b, s, MLA_HEADS, MLA_NOPE_DIM + MLA_ROPE_DIM)
    q_nope, q_pe = q[..., :MLA_NOPE_DIM], q[..., MLA_NOPE_DIM:]
    kv = linear(rms_norm(c_kv, kv_norm_g[None, :], 1), w_ukv).reshape(b, s, MLA_HEADS, MLA_NOPE_DIM + MLA_V_DIM)
    k_nope, v = kv[..., :MLA_NOPE_DIM], kv[..., MLA_NOPE_DIM:]
    q_pe = _apply_axial_rope(q_pe, rope)
    k_pe = _apply_axial_rope(k_pe.reshape(b, s, 1, MLA_ROPE_DIM), rope)
    q = jnp.concatenate([q_nope, q_pe], axis=-1)
    k = jnp.concatenate([k_nope, jnp.broadcast_to(k_pe, (b, s, MLA_HEADS, MLA_ROPE_DIM))], axis=-1)
    o = attention(_heads_first(q), _heads_first(k), _heads_first(v), (MLA_NOPE_DIM + MLA_ROPE_DIM) ** -0.5)
    return _heads_first(o).reshape(b * s, MLA_HEADS * MLA_V_DIM)


def _ssd_group(z, xbc, dt_raw, conv_w, conv_b, dt_bias, a_log, d_skip, norm_g, b, s):
    wb = jnp.concatenate([conv_w, conv_b[None, :], jnp.zeros((CONV_PACK_ROWS - SSD_CONV - 1, SSD_CONV_DIM), F32)], axis=0)
    xbc = conv_silu(xbc.reshape(b, s, SSD_CONV_DIM), wb)
    xs = xbc[..., :SSD_INNER]
    bm = xbc[..., SSD_INNER:SSD_INNER + SSD_GROUPS * SSD_STATE]
    cm = xbc[..., SSD_INNER + SSD_GROUPS * SSD_STATE:]
    dt = jax.nn.softplus(dt_raw.reshape(b, s, 2, SSD_HEADS) + dt_bias)
    a_neg = -jnp.exp(a_log)
    y_fwd = ssd_chunked(xs, dt[:, :, 0], a_neg[0], bm, cm, False)
    y_bwd = ssd_chunked(xs, dt[:, :, 1], a_neg[1], bm, cm, True)
    y = y_fwd + y_bwd + xs * jnp.repeat(d_skip, SSD_HEAD_DIM)
    y = y.reshape(b * s, SSD_INNER) * jax.nn.silu(z)
    return rms_norm(y, norm_g[None, :], SSD_GROUPS)


MIXER_WEIGHTS = ('q_norm_g', 'k_norm_g', 'mla_q_norm_g', 'w_uq', 'mla_kv_norm_g', 'w_ukv', 'conv_w', 'conv_b',
                 'dt_bias', 'a_log', 'd_skip', 'ssd_norm_g')


def _mixer(proj, w, rope_a, rope_b, b, s):
    idx = np.cumsum(IN_SPLITS).tolist()
    q_a, k_a, v_a, cq_b, ckv_b, kpe_b, z_c, xbc_c, dt_c = [_in_cols(proj, lo, hi)
                                                           for lo, hi in zip([0] + idx[:-1], idx)]
    o_a = _gqa_group(q_a, k_a, v_a, w["q_norm_g"], w["k_norm_g"], rope_a, b, s)
    o_b = _mla_group(cq_b, ckv_b, kpe_b, w["mla_q_norm_g"], w["w_uq"], w["mla_kv_norm_g"], w["w_ukv"], rope_b, b, s)
    o_c = _ssd_group(z_c, xbc_c, dt_c, w["conv_w"], w["conv_b"], w["dt_bias"], w["a_log"], w["d_skip"],
                     w["ssd_norm_g"], b, s)
    return jnp.concatenate([o_a, o_b, o_c], axis=-1)


def _seq_tile(s, row_bytes):
    return _row_tile(s, row_bytes)


def _normmod_fwd(x, g, scale, shift):
    b, s, d = x.shape
    tr = _seq_tile(s, d * 4)

    def body(x_ref, g_ref, sc_ref, sh_ref, h_ref):
        xv = x_ref[0]
        r = lax.rsqrt(jnp.mean(xv * xv, axis=-1, keepdims=True) + EPS)
        h_ref[0] = (xv * r * g_ref[...] * (1.0 + sc_ref[0]) + sh_ref[0]).astype(BF16)

    act = pl.BlockSpec((1, tr, d), lambda bi, i: (bi, i, 0))
    vec = pl.BlockSpec((1, 1, d), lambda bi, i: (bi, 0, 0))
    return pl.pallas_call(
        body, name="normmod_fwd", grid=(b, s // tr),
        in_specs=[act, pl.BlockSpec((1, d), lambda bi, i: (0, 0)), vec, vec], out_specs=act,
        out_shape=jax.ShapeDtypeStruct((b, s, d), BF16),
        compiler_params=pltpu.CompilerParams(dimension_semantics=("parallel", "parallel")),
    )(x, g, scale, shift)


def _normmod_bwd(x, g, scale, dh, resid):
    b, s, d = x.shape
    tr = _seq_tile(s, d * 4)

    def body(x_ref, g_ref, sc_ref, dh_ref, res_ref, dx_ref, dg_ref, dsc_ref, dsh_ref):
        bi, i = pl.program_id(0), pl.program_id(1)

        @pl.when((bi == 0) & (i == 0))
        def _():
            dg_ref[...] = jnp.zeros_like(dg_ref)

        @pl.when(i == 0)
        def _():
            dsc_ref[...] = jnp.zeros_like(dsc_ref)
            dsh_ref[...] = jnp.zeros_like(dsh_ref)

        xv = x_ref[0]
        dhv = dh_ref[0]
        gv = g_ref[...]
        r = lax.rsqrt(jnp.mean(xv * xv, axis=-1, keepdims=True) + EPS)
        xhat = xv * r
        dsh_ref[0] += jnp.sum(dhv, axis=0, keepdims=True)
        dsc_ref[0] += jnp.sum(dhv * (xhat * gv), axis=0, keepdims=True)
        dn = dhv * (1.0 + sc_ref[0])
        dg_ref[...] += jnp.sum(dn * xhat, axis=0, keepdims=True)
        dxhat = dn * gv
        dx_ref[0] = r * (dxhat - xhat * jnp.mean(dxhat * xhat, axis=-1, keepdims=True)) + res_ref[0]

    act = pl.BlockSpec((1, tr, d), lambda bi, i: (bi, i, 0))
    vec = pl.BlockSpec((1, 1, d), lambda bi, i: (bi, 0, 0))
    gain = pl.BlockSpec((1, d), lambda bi, i: (0, 0))
    return pl.pallas_call(
        body, name="normmod_bwd", grid=(b, s // tr),
        in_specs=[act, gain, vec, act, act], out_specs=[act, gain, vec, vec],
        out_shape=[jax.ShapeDtypeStruct((b, s, d), F32), jax.ShapeDtypeStruct((1, d), F32),
                   jax.ShapeDtypeStruct((b, 1, d), F32), jax.ShapeDtypeStruct((b, 1, d), F32)],
        compiler_params=pltpu.CompilerParams(dimension_semantics=("arbitrary", "arbitrary")),
    )(x, g, scale, dh, resid)


def _gated_add(x, gate, t):
    b, s, d = x.shape
    tr = _seq_tile(s, d * 4)

    def body(x_ref, g_ref, t_ref, o_ref):
        o_ref[0] = x_ref[0] + g_ref[0] * t_ref[0]

    act = pl.BlockSpec((1, tr, d), lambda bi, i: (bi, i, 0))
    vec = pl.BlockSpec((1, 1, d), lambda bi, i: (bi, 0, 0))
    return pl.pallas_call(
        body, name="gated_add", grid=(b, s // tr), in_specs=[act, vec, act], out_specs=act,
        out_shape=jax.ShapeDtypeStruct((b, s, d), F32),
        compiler_params=pltpu.CompilerParams(dimension_semantics=("parallel", "parallel")),
    )(x, gate, t)


def _gated_bwd(dy, gate, t):
    b, s, d = dy.shape
    tr = _seq_tile(s, d * 4)

    def body(dy_ref, g_ref, t_ref, dt_ref, dgate_ref):
        @pl.when(pl.program_id(1) == 0)
        def _():
            dgate_ref[...] = jnp.zeros_like(dgate_ref)

        dyv = dy_ref[0]
        dt_ref[0] = (g_ref[0] * dyv).astype(BF16)
        dgate_ref[0] += jnp.sum(dyv * t_ref[0], axis=0, keepdims=True)

    act = pl.BlockSpec((1, tr, d), lambda bi, i: (bi, i, 0))
    vec = pl.BlockSpec((1, 1, d), lambda bi, i: (bi, 0, 0))
    return pl.pallas_call(
        body, name="gated_bwd", grid=(b, s // tr), in_specs=[act, vec, act], out_specs=[act, vec],
        out_shape=[jax.ShapeDtypeStruct((b, s, d), BF16), jax.ShapeDtypeStruct((b, 1, d), F32)],
        compiler_params=pltpu.CompilerParams(dimension_semantics=("parallel", "arbitrary")),
    )(dy, gate, t)


def _swiglu_fwd(gu, plan=None):
    rows, f2 = gu.shape
    f = f2 // 2
    tr = _row_tile(rows, f2 * 4, SWIGLU_TILE_BYTES)
    steps = rows // tr
    p_in, p_out, p_shapes, p_scratch, p_args = _plan_specs(plan)

    def body(gu_ref, *rest):
        a_ref = rest[len(p_in)]
        i = pl.program_id(0)

        def compute():
            gt = gu_ref[:, :f]
            a_ref[...] = (gt * jax.nn.sigmoid(gt) * gu_ref[:, f:]).astype(BF16)

        _ride(plan, rest[:len(p_in)] + rest[len(p_in) + 1:], i == 0, i == steps - 1, compute)

    outs = pl.pallas_call(
        body, name="swiglu_fwd", grid=(steps,),
        in_specs=[pl.BlockSpec((tr, f2), lambda i: (i, 0))] + p_in,
        out_specs=[pl.BlockSpec((tr, f), lambda i: (i, 0))] + p_out,
        out_shape=[jax.ShapeDtypeStruct((rows, f), BF16)] + p_shapes, scratch_shapes=p_scratch,
        compiler_params=pltpu.CompilerParams(dimension_semantics=("arbitrary" if plan is not None else "parallel",)),
    )(gu, *p_args)
    return outs[0] if plan is None else (outs[0], list(outs[1:]))


def _swiglu_bwd(gu, dact):
    rows, f2 = gu.shape
    f = f2 // 2
    tr = _row_tile(rows, f2 * 4, SWIGLU_TILE_BYTES)

    def body(gu_ref, da_ref, dgu_ref):
        gt = gu_ref[:, :f]
        up = gu_ref[:, f:]
        da = da_ref[...]
        sg = jax.nn.sigmoid(gt)
        dgu_ref[:, :f] = (da * up * (sg * (1.0 + gt * (1.0 - sg)))).astype(BF16)
        dgu_ref[:, f:] = (da * gt * sg).astype(BF16)

    return pl.pallas_call(
        body, name="swiglu_bwd", grid=(rows // tr,),
        in_specs=[pl.BlockSpec((tr, f2), lambda i: (i, 0)), pl.BlockSpec((tr, f), lambda i: (i, 0))],
        out_specs=pl.BlockSpec((tr, f2), lambda i: (i, 0)),
        out_shape=jax.ShapeDtypeStruct((rows, f2), BF16),
        compiler_params=pltpu.CompilerParams(dimension_semantics=("parallel",)),
    )(gu, dact)


class _Gathered:
    def __init__(self, shards):
        self.shards, self.full = shards, {}

    def plan(self, keys):
        return _gather_plan([self.shards[k] for k in keys])

    def store(self, keys, outs):
        for key, out in zip(keys, outs):
            name = key[0]
            g = out.reshape((N_CHIPS,) + self.shards[key].shape)
            if name in CHIP_BLOCKED:
                full = g
            elif name in COL_SHARDED and name not in TRANSPOSED:
                full = _cols_full(g).astype(F32)
            else:
                full = g.reshape(g.shape[0] * g.shape[1], g.shape[2])
            self.full[key] = full

    def carry(self, keys, fn):
        if not keys:
            return fn(None)
        res, outs = fn(self.plan(keys))
        self.store(keys, outs)
        return res


def _gather_schedule(depth):
    every = [(n, l) for l in range(depth) for n in ('w_uq', 'w_ukv')]
    sched = {'first': [('w_in', 0), ('w_gate_up', 0)] + every}
    for l in range(depth):
        sched[('w_in_fwd', l)] = [('w_out', l)] + ([('w_in', l + 1)] if l + 1 < depth else [])
        if l == 0:
            sched[('swiglu_fwd', l)] = [('w_down', l)]
        if l + 1 < depth:
            sched[('w_gate_up_fwd', l)] = [('w_gate_up', l + 1)]
            sched[('w_down_fwd', l)] = [('w_down', l + 1)]
    return sched


GATE_UP_PIECES = 4


class _Reducer:
    def __init__(self):
        self.parts, self.recv, self.result = {}, {}, {}

    def add(self, items):
        blocks = []
        for (name, _), grad in items:
            if name in CHIP_BLOCKED:
                blocks.append(grad)
            elif name in COL_SHARDED and name not in TRANSPOSED:
                blocks.append(_cols_split(grad))
            else:
                blocks.append(grad.reshape(N_CHIPS, grad.shape[0] // N_CHIPS, grad.shape[1]))
        for (key, _), parts in zip(items, _rs_parts(blocks)):
            self.parts[key] = parts
            self.recv[key] = []

    def pieces(self, key):
        rows = self.parts[key].shape[1]
        n = GATE_UP_PIECES if key[0] == 'w_gate_up' else 1
        return [(key, i * (rows // n), rows // n) for i in range(n)]

    def plan(self, jobs):
        return _chip_exchange_plan([(self.parts[key], row0, rows) for key, row0, rows in jobs])

    def store(self, jobs, outs):
        complete = []
        for (key, row0, rows), out in zip(jobs, outs):
            self.recv[key].append((row0, out))
            if len(self.recv[key]) == len(self.pieces(key)):
                complete.append(key)
        if complete:
            items = [(self.parts[key], sorted(self.recv[key], key=lambda t: t[0])) for key in complete]
            self.result.update(zip(complete, _rs_result(items)))

    def carry(self, keys, fn, piece=None, also=()):
        jobs = [j for key in keys for j in self.pieces(key)]
        if piece is not None:
            jobs = [j for key in keys for j in self.pieces(key)[piece:piece + 1]]
        jobs += [j for key in also for j in self.pieces(key)]
        if not jobs:
            return fn(None)
        res, outs = fn(self.plan(jobs))
        self.store(jobs, outs)
        return res

    def flush(self):
        jobs = [j for key in self.parts for j in self.pieces(key)
                if key not in self.result and j[1] not in [r for r, _ in self.recv[key]]]
        if jobs:
            self.store(jobs, _run_plan(self.plan(jobs), "rs_chip_exchange"))


def _layer_fwd(x, mod, w, gathered, l, sched, rope_a, rope_b):
    b, s, d = x.shape
    m = b * s
    shift1, scale1, gate1, shift2, scale2, gate2 = [t[:, None, :] for t in jnp.split(mod, 6, axis=-1)]
    g1, g2 = w["norm1_g"][None, :], w["norm2_g"][None, :]
    full = lambda n: gathered.full[(n, l)]
    h1 = _normmod_fwd(x, g1, scale1, shift1).reshape(m, d)
    proj = gathered.carry(sched.get(('w_in_fwd', l)), lambda p: _matmul(h1, full('w_in'), tb=True, name="w_in_fwd", plan=p))
    mixer_w = {n: (full(n) if n in COL_SHARDED else w[n]) for n in MIXER_WEIGHTS}
    o, mixer_vjp = jax.vjp(lambda p, mw: _mixer(p, mw, rope_a, rope_b, b, s), proj, mixer_w)
    o = o.astype(BF16)
    mix = _matmul(o, full('w_out'), name="w_out_fwd").reshape(b, s, d)
    x_mid = _gated_add(x, gate1, mix)
    h2 = _normmod_fwd(x_mid, g2, scale2, shift2).reshape(m, d)
    gu = gathered.carry(sched.get(('w_gate_up_fwd', l)),
                        lambda p: _matmul(h2, full('w_gate_up'), name="w_gate_up_fwd", plan=p, chips='b'))
    act = gathered.carry(sched.get(('swiglu_fwd', l)), lambda p: _swiglu_fwd(gu, plan=p))
    ffn = gathered.carry(sched.get(('w_down_fwd', l)), lambda p: _matmul(act, full('w_down'), name="w_down_fwd", plan=p))
    ffn = ffn.reshape(b, s, d)
    x_out = _gated_add(x_mid, gate2, ffn)
    res = (x, x_mid, h1, h2, o, mix, gu, act, ffn, mixer_vjp, scale1, gate1, scale2, gate2, g1, g2)
    return x_out, res


def _layer_bwd(res, gathered, reducer, l, depth, dx_out):
    x, x_mid, h1, h2, o, mix, gu, act, ffn, mixer_vjp, scale1, gate1, scale2, gate2, g1, g2 = res
    b, s, d = x.shape
    m = b * s
    full = lambda n: gathered.full[(n, l)]
    above = l + 1 < depth
    dffn, dgate2 = _gated_bwd(dx_out, gate2, ffn)
    dffn = dffn.reshape(m, d)
    dact = reducer.carry([('w_out', l + 1), ('w_uq', l + 1), ('w_ukv', l + 1)] if above else [],
                         lambda p: _matmul(dffn, full('w_down'), tb=True, name="w_down_dx", plan=p))
    dw = reducer.carry([('w_in', l + 1)] if above else [],
                       lambda p: _matmul(act, dffn, ta=True, name="w_down_dw", plan=p))
    reducer.add([(('w_down', l), dw)])
    dgu = _swiglu_bwd(gu, dact)
    dh2 = reducer.carry([('w_down', l)], lambda p: _matmul(dgu, full('w_gate_up'), tb=True, name="w_gate_up_dx", plan=p,
                                                           chips='b'))
    dh2 = dh2.reshape(b, s, d)
    reducer.add([(('w_gate_up', l), _matmul(h2, dgu, ta=True, name="w_gate_up_dw", chips='out'))])
    dx_mid, dg2, dscale2, dshift2 = _normmod_bwd(x_mid, g2, scale2, dh2, dx_out)
    dmix, dgate1 = _gated_bwd(dx_mid, gate1, mix)
    dmix = dmix.reshape(m, d)
    gate_up = [('w_gate_up', l)]
    do = reducer.carry(gate_up, lambda p: _matmul(dmix, full('w_out'), tb=True, name="w_out_dx", plan=p), piece=0)
    dw_out = reducer.carry(gate_up, lambda p: _matmul(o, dmix, ta=True, name="w_out_dw", plan=p), piece=1)
    dproj, grads = mixer_vjp(do)
    grads = dict(grads)
    reducer.add([(('w_out', l), dw_out), (('w_uq', l), grads.pop('w_uq')), (('w_ukv', l), grads.pop('w_ukv'))])
    dproj = dproj.astype(BF16)
    bottom = [('w_out', l), ('w_uq', l), ('w_ukv', l)] if l == 0 else []
    dh1 = reducer.carry(gate_up, lambda p: _matmul(dproj, full('w_in'), name="w_in_dx", plan=p), piece=2, also=bottom)
    dh1 = dh1.reshape(b, s, d)
    dw = reducer.carry(gate_up, lambda p: _matmul(dproj, h1, ta=True, name="w_in_dw", plan=p), piece=3)
    reducer.add([(('w_in', l), dw)])
    dx, dg1, dscale1, dshift1 = _normmod_bwd(x, g1, scale1, dh1, dx_mid)
    grads["norm1_g"], grads["norm2_g"] = dg1[0], dg2[0]
    dmod = jnp.concatenate([dshift1, dscale1, dgate1, dshift2, dscale2, dgate2], axis=-1)[:, 0, :]
    return dx, dmod, grads


def _tail_loss(x2, final_norm_g, target2):
    return loss_head(rms_norm(x2, final_norm_g[None, :], 1), target2)


def _forward_backward(x, mod, small, gathered, reducer, final_norm_g, target):
    b, s, d = x.shape
    depth = len(small)
    rope_a = _axial_rope_tables(s, GQA_HEAD_DIM)
    rope_b = _axial_rope_tables(s, MLA_ROPE_DIM)
    sched = _gather_schedule(depth)
    first = sched['first']
    gathered.store(first, _run_plan(gathered.plan(first), "all_gather_chips"))
    saved = []
    for l in range(depth):
        x, res = _layer_fwd(x, mod[l], small[l], gathered, l, sched, rope_a, rope_b)
        saved.append(res)
    loss, (dx2, dfinal) = jax.value_and_grad(_tail_loss, argnums=(0, 1))(
        x.reshape(b * s, d), final_norm_g, target.reshape(b * s, d))
    dx = dx2.reshape(b, s, d)
    dmods, gsmall = [None] * depth, [None] * depth
    for l in reversed(range(depth)):
        dx, dmods[l], gsmall[l] = _layer_bwd(saved[l], gathered, reducer, l, depth, dx)
    return loss, dx, jnp.stack(dmods), gsmall, dfinal


ANY = pl.BlockSpec(memory_space=pl.ANY)


def _flip_if(v, bit):
    return 1 - v if bit else v


def _all_gather_devices(x):
    def body(x_ref, out_ref, send_sems, recv_sems):
        mx, my, mc = lax.axis_index("x"), lax.axis_index("y"), lax.axis_index("c")
        me = 4 * mx + 2 * my + mc
        sends = []
        for k in range(1, N_DEV):
            peer = (_flip_if(mx, k & 4), _flip_if(my, k & 2), _flip_if(mc, k & 1))
            cp = pltpu.make_async_remote_copy(src_ref=x_ref, dst_ref=out_ref.at[me], send_sem=send_sems.at[k - 1],
                                              recv_sem=recv_sems.at[k - 1], device_id=peer, device_id_type=MESH)
            cp.start()
            sends.append(cp)
        for k in range(1, N_DEV):
            peer = (_flip_if(mx, k & 4), _flip_if(my, k & 2), _flip_if(mc, k & 1))
            src = 4 * peer[0] + 2 * peer[1] + peer[2]
            pltpu.make_async_remote_copy(src_ref=x_ref, dst_ref=out_ref.at[src], send_sem=send_sems.at[k - 1],
                                         recv_sem=recv_sems.at[k - 1], device_id=peer, device_id_type=MESH).wait_recv()
        for cp in sends:
            cp.wait_send()

    out = pl.pallas_call(
        body, name="all_gather_devices", in_specs=[ANY], out_specs=ANY,
        out_shape=jax.ShapeDtypeStruct((N_DEV,) + x.shape, x.dtype),
        scratch_shapes=[pltpu.SemaphoreType.DMA((N_DEV - 1,)), pltpu.SemaphoreType.DMA((N_DEV - 1,))],
    )(x)
    me = 4 * lax.axis_index("x") + 2 * lax.axis_index("y") + lax.axis_index("c")
    return lax.dynamic_update_index_in_dim(out, x, me, 0)


class _Plan:
    def __init__(self, inputs, out_shapes, sem_counts, start, finish):
        self.inputs, self.out_shapes, self.sem_counts = list(inputs), list(out_shapes), list(sem_counts)
        self.start, self.finish = start, finish

    def specs(self):
        return ([ANY] * len(self.inputs), [ANY] * len(self.out_shapes),
                [pltpu.SemaphoreType.DMA((c,)) for c in self.sem_counts])

    def split(self, refs):
        a, b = len(self.inputs), len(self.inputs) + len(self.out_shapes)
        return refs[:a], refs[a:b], refs[b:]


def _run_plan(plan, name):
    def body(*refs):
        ins, outs, sems = plan.split(refs)
        plan.start(ins, outs, sems)
        plan.finish(ins, outs, sems)

    in_specs, out_specs, scratch = plan.specs()
    return pl.pallas_call(body, name=name, in_specs=in_specs, out_specs=out_specs, out_shape=plan.out_shapes,
                          scratch_shapes=scratch)(*plan.inputs)


def _gather_plan(shards):
    n = len(shards)
    halves = [t.reshape(2, t.shape[0] // 2, t.shape[1]) for t in shards]
    count = (N_CHIPS - 1) * n

    def copies(kind, ins, outs, sems):
        ici_send, ici_recv, d2d_send, d2d_recv, own_send, own_recv = sems
        mx, my, mc = lax.axis_index("x"), lax.axis_index("y"), lax.axis_index("c")
        me = 2 * mx + my
        sibling = (mx, my, 1 - mc)
        if kind == 'own':
            return [pltpu.make_async_remote_copy(src_ref=ins[i], dst_ref=outs[i].at[me], send_sem=own_send.at[i],
                                                 recv_sem=own_recv.at[i], device_id=sibling, device_id_type=MESH)
                    for i in range(n)]
        cps = []
        for k in range(1, N_CHIPS):
            peer = (_flip_if(mx, k & 2), _flip_if(my, k & 1), mc)
            src = 2 * peer[0] + peer[1]
            for i in range(n):
                j = (k - 1) * n + i
                if kind in ('ici', 'landed'):
                    dst = outs[i].at[me, mc] if kind == 'ici' else outs[i].at[src, mc]
                    cps.append(pltpu.make_async_remote_copy(
                        src_ref=ins[i].at[mc], dst_ref=dst, send_sem=ici_send.at[j], recv_sem=ici_recv.at[j],
                        device_id=peer, device_id_type=MESH))
                else:
                    half = outs[i].at[src, mc] if kind == 'fwd' else outs[i].at[src, 1 - mc]
                    cps.append(pltpu.make_async_remote_copy(
                        src_ref=half, dst_ref=half, send_sem=d2d_send.at[j], recv_sem=d2d_recv.at[j],
                        device_id=sibling, device_id_type=MESH))
        return cps

    def start(ins, outs, sems):
        for cp in copies('own', ins, outs, sems) + copies('ici', ins, outs, sems):
            cp.start()

    def finish(ins, outs, sems):
        fwd = copies('fwd', ins, outs, sems)
        for arrived, onward in zip(copies('landed', ins, outs, sems), fwd):
            arrived.wait_recv()
            onward.start()
        own = copies('own', ins, outs, sems)
        for cp in copies('fwd_in', ins, outs, sems) + own:
            cp.wait_recv()
        for cp in own + copies('ici', ins, outs, sems) + fwd:
            cp.wait_send()

    out_shapes = [jax.ShapeDtypeStruct((N_CHIPS,) + t.shape, t.dtype) for t in halves]
    return _Plan(halves, out_shapes, [count] * 4 + [n] * 2, start, finish)


def _sibling_exchange(blocks, name):
    n = len(blocks)

    def body(*refs):
        ins, outs = refs[:n], refs[n:2 * n]
        send_sems, recv_sems = refs[2 * n:]
        mx, my, mc = lax.axis_index("x"), lax.axis_index("y"), lax.axis_index("c")
        cps = []
        for i in range(n):
            cp = pltpu.make_async_remote_copy(src_ref=ins[i], dst_ref=outs[i], send_sem=send_sems.at[i],
                                              recv_sem=recv_sems.at[i], device_id=(mx, my, 1 - mc),
                                              device_id_type=MESH)
            cp.start()
            cps.append(cp)
        for cp in cps:
            cp.wait()

    return pl.pallas_call(
        body, name=name, in_specs=[ANY] * n, out_specs=[ANY] * n,
        out_shape=[jax.ShapeDtypeStruct(t.shape, t.dtype) for t in blocks],
        scratch_shapes=[pltpu.SemaphoreType.DMA((n,)), pltpu.SemaphoreType.DMA((n,))],
    )(*blocks)


def _add_halves(own, recv):
    nb, r, c = own.shape
    tr = _row_tile(r, c * 4)

    def body(g_ref, r_ref, o_ref):
        o_ref[...] = (g_ref[...] + r_ref[...].astype(F32)).astype(BF16)

    spec = pl.BlockSpec((1, tr, c), lambda k, i: (k, i, 0))
    return pl.pallas_call(
        body, name="rs_add_halves", grid=(nb, r // tr), in_specs=[spec, spec], out_specs=spec,
        out_shape=jax.ShapeDtypeStruct((nb, r, c), BF16),
        compiler_params=pltpu.CompilerParams(dimension_semantics=("parallel", "parallel")),
    )(own, recv)


def _chip_exchange_plan(jobs):
    n = len(jobs)
    count = (N_CHIPS - 1) * n

    def copies(ins, outs, sems):
        send_sems, recv_sems = sems
        mx, my, mc = lax.axis_index("x"), lax.axis_index("y"), lax.axis_index("c")
        cps = []
        for k in range(1, N_CHIPS):
            peer = (_flip_if(mx, k & 2), _flip_if(my, k & 1), mc)
            dst_chip = 2 * peer[0] + peer[1]
            for i, (_, row0, rows) in enumerate(jobs):
                j = (k - 1) * n + i
                cps.append(pltpu.make_async_remote_copy(
                    src_ref=ins[i].at[dst_chip, pl.ds(row0, rows)], dst_ref=outs[i].at[k - 1],
                    send_sem=send_sems.at[j], recv_sem=recv_sems.at[j], device_id=peer, device_id_type=MESH))
        return cps

    def start(ins, outs, sems):
        for cp in copies(ins, outs, sems):
            cp.start()

    def finish(ins, outs, sems):
        for cp in copies(ins, outs, sems):
            cp.wait()

    out_shapes = [jax.ShapeDtypeStruct((N_CHIPS - 1, rows, p.shape[2]), p.dtype) for p, _, rows in jobs]
    return _Plan([p for p, _, _ in jobs], out_shapes, [count, count], start, finish)


def _sum_chips(parts, recv, chip, row0):
    _, rows, c = recv.shape
    tr = _row_tile(rows, c * 4)
    assert row0 % tr == 0

    def body(chip_ref, p_ref, r_ref, o_ref):
        acc = p_ref[0].astype(F32)
        for k in range(N_CHIPS - 1):
            acc = acc + r_ref[k].astype(F32)
        o_ref[...] = acc

    return pl.pallas_call(
        body, name="rs_sum_chips",
        grid_spec=pltpu.PrefetchScalarGridSpec(
            num_scalar_prefetch=1, grid=(rows // tr,),
            in_specs=[pl.BlockSpec((1, tr, c), lambda i, chip_ref: (chip_ref[0], i + row0 // tr, 0)),
                      pl.BlockSpec((N_CHIPS - 1, tr, c), lambda i, chip_ref: (0, i, 0))],
            out_specs=pl.BlockSpec((tr, c), lambda i, chip_ref: (i, 0))),
        out_shape=jax.ShapeDtypeStruct((rows, c), F32),
        compiler_params=pltpu.CompilerParams(dimension_semantics=("parallel",)),
    )(chip, parts, recv)


def _sum_leading(t, name):
    nb, r, c = t.shape
    tr = _row_tile(r, c * 4 * nb)

    def body(t_ref, o_ref):
        acc = t_ref[0]
        for k in range(1, nb):
            acc = acc + t_ref[k]
        o_ref[...] = acc

    return pl.pallas_call(
        body, name=name, grid=(r // tr,),
        in_specs=[pl.BlockSpec((nb, tr, c), lambda i: (0, i, 0))],
        out_specs=pl.BlockSpec((tr, c), lambda i: (i, 0)),
        out_shape=jax.ShapeDtypeStruct((r, c), F32),
        compiler_params=pltpu.CompilerParams(dimension_semantics=("parallel",)),
    )(t)


def _rs_parts(grads):
    mc = lax.axis_index("c")
    split = [g.reshape(g.shape[0], 2, g.shape[1] // 2, g.shape[2]) for g in grads]
    own = [lax.dynamic_index_in_dim(g, mc, axis=1, keepdims=False) for g in split]
    away = [lax.dynamic_index_in_dim(g, 1 - mc, axis=1, keepdims=False).astype(BF16) for g in split]
    return [_add_halves(o, r) for o, r in zip(own, _sibling_exchange(away, "rs_sibling_exchange"))]


def _rs_result(items):
    mc = lax.axis_index("c")
    chip = (2 * lax.axis_index("x") + lax.axis_index("y")).astype(jnp.int32).reshape(1)
    mine = []
    for parts, pieces in items:
        done = [_sum_chips(parts, recv, chip, row0) for row0, recv in pieces]
        mine.append(done[0] if len(done) == 1 else jnp.concatenate(done, axis=0))
    theirs = _sibling_exchange(mine, "rs_sibling_swap")
    return [jnp.concatenate([jnp.where(mc == 0, a, b), jnp.where(mc == 0, b, a)], axis=0)
            for a, b in zip(mine, theirs)]


def _adamw(w, g, m, v, plan=None):
    shape = w.shape
    cols = shape[-1]
    if len(shape) == 3:
        lead, rows = shape[0], shape[1]
    else:
        lead, rows = 1, (int(np.prod(shape[:-1])) if len(shape) > 1 else 1)
    w2, g2, m2, v2 = [t.reshape(lead, rows, cols) for t in (w, g, m, v)]
    tr = _row_tile(rows, cols * 4, ADAM_TILE_BYTES)
    t0, tc = 1, cols
    if tr == rows and rows * cols * 4 > ADAM_TILE_BYTES and cols % LANES == 0:
        tc = max(t for t in _tile_cands(cols, cols) if t == LANES or rows * t * 4 <= ADAM_TILE_BYTES)
    elif rows < 8 and cols % LANES == 0:
        fits = [(a * t, t, a) for a in range(1, lead + 1) if lead % a == 0 for t in _tile_cands(cols, cols)
                if a * rows * t * 4 <= ADAM_TILE_BYTES]
        _, tc, t0 = max(fits) if fits else (0, LANES, 1)
    per = (rows // tr) * (cols // tc)
    steps = (lead // t0) * per
    p_in, p_out, p_shapes, p_scratch, p_args = _plan_specs(plan)

    def body(w_ref, g_ref, m_ref, v_ref, *rest):
        d_ref, mo_ref, vo_ref = rest[len(p_in):len(p_in) + 3]
        i = pl.program_id(0) * per + pl.program_id(1)

        def compute():
            gv = g_ref[...]
            mn = ADAM_B1 * m_ref[...] + (1.0 - ADAM_B1) * gv
            vn = ADAM_B2 * v_ref[...] + (1.0 - ADAM_B2) * (gv * gv)
            m_hat = mn / (1.0 - ADAM_B1 ** ADAM_STEP)
            v_hat = vn / (1.0 - ADAM_B2 ** ADAM_STEP)
            d_ref[...] = -ADAM_LR * (m_hat / (jnp.sqrt(v_hat) + ADAM_EPS) + ADAM_WD * w_ref[...])
            mo_ref[...] = mn
            vo_ref[...] = vn

        _ride(plan, rest[:len(p_in)] + rest[len(p_in) + 3:], i == 0, i == steps - 1, compute)

    col_blocks = cols // tc
    spec = pl.BlockSpec((t0, tr, tc), lambda a, i: (a, i // col_blocks, i % col_blocks))
    sem = "arbitrary" if plan is not None else "parallel"
    outs = pl.pallas_call(
        body, name="adamw", grid=(lead // t0, per), in_specs=[spec] * 4 + p_in, out_specs=[spec] * 3 + p_out,
        out_shape=[jax.ShapeDtypeStruct((lead, rows, cols), F32)] * 3 + p_shapes, scratch_shapes=p_scratch,
        compiler_params=pltpu.CompilerParams(dimension_semantics=(sem, sem), vmem_limit_bytes=ADAM_VMEM_LIMIT),
    )(w2, g2, m2, v2, *p_args)
    res = [t.reshape(shape) for t in outs[:3]]
    return res if plan is None else (res, list(outs[3:]))


WEIGHTS = ['w_ada', 'b_ada', 'norm1_g', 'norm2_g', 'w_in', 'q_norm_g', 'k_norm_g', 'mla_q_norm_g', 'w_uq',
           'mla_kv_norm_g', 'w_ukv', 'conv_w', 'conv_b', 'dt_bias', 'a_log', 'd_skip', 'ssd_norm_g', 'w_out',
           'w_gate_up', 'w_down', 'final_norm_g']
COL_SHARDED = ('w_in', 'w_uq', 'w_ukv', 'w_gate_up')
ROW_SHARDED = ('w_out', 'w_down')
CHIP_BLOCKED = ('w_gate_up',)
TRANSPOSED = ('w_in',)
SMALL_LAYER = ('norm1_g', 'norm2_g', 'q_norm_g', 'k_norm_g', 'mla_q_norm_g', 'mla_kv_norm_g', 'conv_w', 'conv_b',
               'dt_bias', 'a_log', 'd_skip', 'ssd_norm_g')


def _pack(parts):
    flat = jnp.concatenate([p.reshape(-1) for p in parts])
    n = flat.shape[0]
    rows = -(-n // (8 * LANES)) * 8
    return jnp.pad(flat, (0, rows * LANES - n)).reshape(rows, LANES)


def _unpack(flat, shapes):
    out, pos = [], 0
    for shp in shapes:
        size = int(np.prod(shp))
        out.append(flat[pos:pos + size].reshape(shp))
        pos += size
    return out


def _cols_full(gathered):
    k, r, c = gathered.shape
    return jnp.transpose(gathered, (1, 0, 2)).reshape(r, k * c)


def _cols_split(full):
    r, c4 = full.shape
    return jnp.transpose(full.reshape(r, N_CHIPS, c4 // N_CHIPS), (1, 0, 2))


def kernel(x, c, w_ada, b_ada, norm1_g, norm2_g, w_in, q_norm_g, k_norm_g, mla_q_norm_g, w_uq, mla_kv_norm_g, w_ukv, conv_w, conv_b, dt_bias, a_log, d_skip, ssd_norm_g, w_out, w_gate_up, w_down, final_norm_g, loss_target, m_w_ada, m_b_ada, m_norm1_g, m_norm2_g, m_w_in, m_q_norm_g, m_k_norm_g, m_mla_q_norm_g, m_w_uq, m_mla_kv_norm_g, m_w_ukv, m_conv_w, m_conv_b, m_dt_bias, m_a_log, m_d_skip, m_ssd_norm_g, m_w_out, m_w_gate_up, m_w_down, m_final_norm_g, v_w_ada, v_b_ada, v_norm1_g, v_norm2_g, v_w_in, v_q_norm_g, v_k_norm_g, v_mla_q_norm_g, v_w_uq, v_mla_kv_norm_g, v_w_ukv, v_conv_w, v_conv_b, v_dt_bias, v_a_log, v_d_skip, v_ssd_norm_g, v_w_out, v_w_gate_up, v_w_down, v_final_norm_g):
    args = dict(locals())
    weights = {n: args[n] for n in WEIGHTS}
    depth = w_in.shape[0]
    bl, s, d = x.shape
    mx, my, mc = lax.axis_index("x"), lax.axis_index("y"), lax.axis_index("c")
    chip = 2 * mx + my
    dev = 2 * chip + mc
    ada_cols = w_ada.shape[-1]
    conv_cols = conv_w.shape[-1]

    first_shapes = [c.shape, conv_w.shape]
    first = _all_gather_devices(_pack([c, conv_w]))
    first = [_unpack(first[i].reshape(-1), first_shapes) for i in range(N_DEV)]
    c_act = jax.nn.silu(jnp.concatenate([f[0] for f in first], axis=0))
    conv_w_full = jnp.concatenate([first[2 * k][1] for k in range(N_CHIPS)], axis=-1)

    b_cols = lax.dynamic_slice_in_dim(b_ada, chip * ada_cols, ada_cols, axis=1)
    c_act_b = c_act.astype(BF16)
    mod_cols = jnp.stack([_matmul(c_act_b, w_ada[l], name="ada_fwd") + b_cols[l][None, :]
                          for l in range(depth)])
    mod_all = _all_gather_devices(mod_cols.reshape(depth * N_DEV * bl, ada_cols))
    mod_all = mod_all.reshape(N_DEV, depth, N_DEV, bl, ada_cols)
    mod_mine = lax.dynamic_index_in_dim(mod_all, dev, axis=2, keepdims=False)
    mod = jnp.concatenate([mod_mine[2 * k] for k in range(N_CHIPS)], axis=-1)

    big = COL_SHARDED + ROW_SHARDED
    flip = lambda t: jnp.transpose(t, (2, 0, 1))
    unflip = lambda t: jnp.transpose(t, (1, 2, 0))
    shards = {(n, l): weights[n][l].astype(BF16) for n in big for l in range(depth)}
    for l in range(depth):
        shards[('w_in', l)] = jnp.pad(flip(w_in)[:, l, :].astype(BF16), ((0, IN_SHARD_PAD - IN_SHARD), (0, 0)))
    gathered = _Gathered(shards)
    reducer = _Reducer()
    small_w = []
    for l in range(depth):
        w = {n: weights[n][l] for n in SMALL_LAYER if n != 'conv_w'}
        w['conv_w'] = conv_w_full[l]
        small_w.append(w)
    loss_local, gx, gmod, glayers, gfinal = _forward_backward(x, mod, small_w, gathered, reducer, final_norm_g,
                                                              loss_target)

    small_parts = [jnp.stack([glayers[l][n] for l in range(depth)]) for n in SMALL_LAYER]
    small_parts += [gfinal, loss_local.reshape(1), gmod]
    small_shapes = [p.shape for p in small_parts]
    last = _all_gather_devices(_pack(small_parts))
    summed = _unpack(_sum_leading(last, "sum_devices").reshape(-1), small_shapes)
    small = dict(zip(SMALL_LAYER, summed[:len(SMALL_LAYER)]))
    g_final, loss, gmod_sum = summed[len(SMALL_LAYER):]
    small['conv_w'] = lax.dynamic_slice_in_dim(small['conv_w'], chip * conv_cols, conv_cols, axis=2)
    gmod_all = jnp.stack([_unpack(last[i].reshape(-1), small_shapes)[-1] for i in range(N_DEV)], axis=1)
    gmod_all = gmod_all.reshape(depth, N_DEV * bl, gmod.shape[-1])
    gmod_cols = lax.dynamic_slice_in_dim(gmod_all, chip * ada_cols, ada_cols, axis=2)
    g_w_ada = jnp.stack([_matmul(c_act_b, gmod_cols[l].astype(BF16), ta=True, name="ada_dw") for l in range(depth)])
    g_b_ada = gmod_sum[:, 0]
    for i in range(1, bl):
        g_b_ada = g_b_ada + gmod_sum[:, i]

    grad = {'w_ada': g_w_ada, 'b_ada': g_b_ada, 'final_norm_g': g_final}
    for n in SMALL_LAYER:
        grad[n] = small[n]

    delta, new_m, new_v = {}, {}, {}
    left = [key for key in reducer.parts if key not in reducer.result and not reducer.recv[key]]
    grad['w_down'] = jnp.stack([reducer.result[('w_down', l)] for l in range(depth)])
    delta['w_down'], new_m['w_down'], new_v['w_down'] = reducer.carry(
        left, lambda p: _adamw(w_down, grad['w_down'], m_w_down, v_w_down, plan=p))
    reducer.flush()
    for n in big:
        if n in TRANSPOSED:
            grad[n] = jnp.stack([reducer.result[(n, l)][:weights[n].shape[2]] for l in range(depth)], axis=1)
        elif n != 'w_down':
            grad[n] = jnp.stack([reducer.result[(n, l)] for l in range(depth)])
    for n in WEIGHTS:
        if n in TRANSPOSED:
            outs = _adamw(flip(weights[n]), grad[n], flip(args["m_" + n]), flip(args["v_" + n]))
            grad[n], delta[n], new_m[n], new_v[n] = [unflip(t) for t in [grad[n]] + outs]
        elif n != 'w_down':
            delta[n], new_m[n], new_v[n] = _adamw(weights[n], grad[n], args["m_" + n], args["v_" + n])
    return (loss.reshape(()), gx, *[grad[n] for n in WEIGHTS], *[delta[n] for n in WEIGHTS],
            *[new_m[n] for n in WEIGHTS], *[new_v[n] for n in WEIGHTS])
```

```python
import functools

import numpy as np
import jax
import jax.numpy as jnp
from jax import lax
from jax.experimental import pallas as pl
from jax.experimental.pallas import tpu as pltpu

F32 = jnp.float32
BF16 = jnp.bfloat16
HIGHEST = lax.Precision.HIGHEST
MESH = pl.DeviceIdType.MESH

GRID_W = 64
ROPE_THETA = 10000.0
EPS = 1e-6

GQA_HEADS, GQA_KV_HEADS, GQA_HEAD_DIM = 6, 2, 128
GQA_WIDTH = GQA_HEADS * GQA_HEAD_DIM
GQA_KV_WIDTH = GQA_KV_HEADS * GQA_HEAD_DIM
MLA_HEADS, MLA_Q_LORA, MLA_KV_LORA = 4, 512, 256
MLA_NOPE_DIM, MLA_ROPE_DIM, MLA_V_DIM = 128, 64, 128
SSD_HEADS, SSD_HEAD_DIM, SSD_GROUPS, SSD_STATE, SSD_CONV, SSD_CHUNK = 12, 64, 2, 128, 5, 128
SSD_INNER = SSD_HEADS * SSD_HEAD_DIM
SSD_CONV_DIM = SSD_INNER + 2 * SSD_GROUPS * SSD_STATE
SSD_GROUP_HEADS = SSD_HEADS // SSD_GROUPS
SSD_GROUP_WIDTH = SSD_GROUP_HEADS * SSD_HEAD_DIM
IN_SPLITS = (GQA_WIDTH, GQA_KV_WIDTH, GQA_KV_WIDTH, MLA_Q_LORA, MLA_KV_LORA, MLA_ROPE_DIM, SSD_INNER, SSD_CONV_DIM,
             2 * SSD_HEADS)
IN_COLS = sum(IN_SPLITS)
LANES = 128
N_CHIPS = 4
IN_SHARD = IN_COLS // N_CHIPS
IN_SHARD_PAD = -(-IN_SHARD // LANES) * LANES


def _in_cols(proj, lo, hi):
    parts = []
    for chip in range(lo // IN_SHARD, (hi - 1) // IN_SHARD + 1):
        a, z = max(lo, chip * IN_SHARD), min(hi, (chip + 1) * IN_SHARD)
        base = chip * IN_SHARD_PAD - chip * IN_SHARD
        parts.append(proj[:, base + a:base + z])
    return parts[0] if len(parts) == 1 else jnp.concatenate(parts, axis=-1)

ADAM_LR, ADAM_B1, ADAM_B2, ADAM_EPS, ADAM_WD, ADAM_STEP = 0.001, 0.9, 0.999, 1e-08, 0.01, 10

N_DEV = 8
TILE_BYTES = 2 * 1024 * 1024


def _pick(n, cands):
    for t in cands:
        if n % t == 0:
            return t
    return n


ADAM_TILE_BYTES = 2 * 1024 * 1024
ADAM_VMEM_LIMIT = 48 * 1024 * 1024
SWIGLU_TILE_BYTES = 4 * 1024 * 1024


def _row_tile(rows, row_bytes, limit=TILE_BYTES):
    for t in (2048, 1024, 512, 256, 128, 64, 32, 16, 8):
        if rows % t == 0 and t * row_bytes <= limit:
            return t
    return rows


MM_VMEM_BUDGET = 36 * 1024 * 1024
MM_VMEM_LIMIT = 56 * 1024 * 1024
MM_MAX_TILE = 2048
MM_MAX_K_TILE = 4096
MXU_DIM = 256
HBM_BYTES_PER_US = 3.0e6
MXU_FLOPS_PER_US = 9.0e8
STEP_US = 0.35


def _tile_cands(d, cap):
    if d % LANES:
        return [d]
    return [t for t in range(LANES, min(d, cap) + 1, LANES) if d % t == 0] or [d]


def _mm_tiles(m, n, kdim, n_unit=None, k_unit=None):
    up = lambda t: -(-t // MXU_DIM) * MXU_DIM
    best = None
    for tm in _tile_cands(m, MM_MAX_TILE):
        for tn in _tile_cands(n_unit or n, MM_MAX_TILE):
            for tk in _tile_cands(k_unit or kdim, MM_MAX_K_TILE):
                if 2 * (tm * tk * 2 + tk * tn * 2 + tm * tn * 4) > MM_VMEM_BUDGET:
                    continue
                ni, nj, nk = m // tm, n // tn, kdim // tk
                a_reads = 1 if nk == 1 else nj
                b_reads = 1 if (nk == 1 and nj == 1) else ni
                hbm = (m * kdim * 2 * a_reads + kdim * n * 2 * b_reads + m * n * 4) / HBM_BYTES_PER_US
                mxu = ni * nj * nk * 2.0 * max(tm, 8) * up(tn) * up(tk) / MXU_FLOPS_PER_US
                cost = max(hbm, mxu) + 0.25 * min(hbm, mxu) + ni * nj * nk * STEP_US
                if best is None or cost < best[0]:
                    best = (cost, tm, tn, tk)
    return best[1:]


def _ride(plan, refs, first, last, compute):
    if plan is None:
        compute()
        return
    ins, outs, sems = plan.split(refs)

    @pl.when(first)
    def _():
        plan.start(ins, outs, sems)

    compute()

    @pl.when(last)
    def _():
        plan.finish(ins, outs, sems)


def _plan_specs(plan):
    if plan is None:
        return [], [], [], [], []
    in_specs, out_specs, scratch = plan.specs()
    return in_specs, out_specs, plan.out_shapes, scratch, plan.inputs


def _matmul(a, b, ta=False, tb=False, name="mm", plan=None, chips=None):
    assert a.dtype == BF16 and b.dtype in (BF16, F32), (a.dtype, b.dtype)
    if ta:
        kdim, m = a.shape
    else:
        m, kdim = a.shape
    n_unit = k_unit = None
    if chips == 'b':
        nb, rows, unit = b.shape
        if tb:
            n, k2, k_unit = rows, nb * unit, unit
        else:
            k2, n, n_unit = rows, nb * unit, unit
    else:
        if tb:
            n, k2 = b.shape
        else:
            k2, n = b.shape
        if chips == 'out':
            n_unit = n // N_CHIPS
    assert kdim == k2, (a.shape, b.shape, ta, tb)
    tm, tn, tk = _mm_tiles(m, n, kdim, n_unit, k_unit)
    ni, nj, nk = m // tm, n // tn, kdim // tk
    dn = (((0 if ta else 1,), (1 if tb else 0,)), ((), ()))
    p_in, p_out, p_shapes, p_scratch, p_args = _plan_specs(plan)

    def body(a_ref, b_ref, *rest):
        o_ref = rest[len(p_in)]
        i, j, k = pl.program_id(0), pl.program_id(1), pl.program_id(2)

        def compute():
            bv = (b_ref[0] if chips == 'b' else b_ref[...]).astype(BF16)
            part = lax.dot_general(a_ref[...], bv, dn, preferred_element_type=F32)
            if chips == 'out':
                part = part[None]
            if nk == 1:
                o_ref[...] = part
            else:
                @pl.when(k == 0)
                def _():
                    o_ref[...] = part

                @pl.when(k > 0)
                def _():
                    o_ref[...] += part

        _ride(plan, rest[:len(p_in)] + rest[len(p_in) + 1:], (i == 0) & (j == 0) & (k == 0),
              (i == ni - 1) & (j == nj - 1) & (k == nk - 1), compute)

    a_spec = pl.BlockSpec((tk, tm), lambda i, j, k: (k, i)) if ta else pl.BlockSpec((tm, tk), lambda i, j, k: (i, k))
    if chips == 'b' and tb:
        per = k_unit // tk
        b_spec = pl.BlockSpec((1, tn, tk), lambda i, j, k: (k // per, j, k % per))
    elif chips == 'b':
        per = n_unit // tn
        b_spec = pl.BlockSpec((1, tk, tn), lambda i, j, k: (j // per, k, j % per))
    else:
        b_spec = pl.BlockSpec((tn, tk), lambda i, j, k: (j, k)) if tb else pl.BlockSpec((tk, tn), lambda i, j, k: (k, j))
    if chips == 'out':
        per = n_unit // tn
        o_spec = pl.BlockSpec((1, tm, tn), lambda i, j, k: (j // per, i, j % per))
        o_shape = jax.ShapeDtypeStruct((N_CHIPS, m, n_unit), F32)
    else:
        o_spec = pl.BlockSpec((tm, tn), lambda i, j, k: (i, j))
        o_shape = jax.ShapeDtypeStruct((m, n), F32)
    outs = pl.pallas_call(
        body, name=name, grid=(ni, nj, nk),
        in_specs=[a_spec, b_spec] + p_in, out_specs=[o_spec] + p_out,
        out_shape=[o_shape] + p_shapes, scratch_shapes=p_scratch,
        compiler_params=pltpu.CompilerParams(
            dimension_semantics=("arbitrary" if plan is not None else "parallel", "arbitrary", "arbitrary"),
            vmem_limit_bytes=MM_VMEM_LIMIT),
    )(a, b, *p_args)
    return outs[0] if plan is None else (outs[0], list(outs[1:]))


@jax.custom_vjp
def linear(x, w):
    return _matmul(x.astype(BF16), w.astype(BF16), name="linear_fwd")


def _linear_fwd(x, w):
    xb, wb = x.astype(BF16), w.astype(BF16)
    return _matmul(xb, wb, name="linear_fwd"), (xb, wb)


def _linear_bwd(res, dy):
    xb, wb = res
    dyb = dy.astype(BF16)
    return _matmul(dyb, wb, tb=True, name="linear_dx"), _matmul(xb, dyb, ta=True, name="linear_dw")


linear.defvjp(_linear_fwd, _linear_bwd)


def _rms_fwd_call(x, g, groups):
    rows, cols = x.shape
    d = cols // groups
    tr = _row_tile(rows, cols * 4)

    def body(x_ref, g_ref, y_ref):
        for gi in range(groups):
            sl = slice(gi * d, (gi + 1) * d)
            xs = x_ref[:, sl]
            r = lax.rsqrt(jnp.mean(xs * xs, axis=-1, keepdims=True) + EPS)
            y_ref[:, sl] = xs * r * g_ref[:, sl]

    return pl.pallas_call(
        body, name="rms_fwd", grid=(rows // tr,),
        in_specs=[pl.BlockSpec((tr, cols), lambda i: (i, 0)), pl.BlockSpec((1, cols), lambda i: (0, 0))],
        out_specs=pl.BlockSpec((tr, cols), lambda i: (i, 0)),
        out_shape=jax.ShapeDtypeStruct((rows, cols), F32),
        compiler_params=pltpu.CompilerParams(dimension_semantics=("parallel",)),
    )(x, g)


def _rms_bwd_call(x, g, dy, groups):
    rows, cols = x.shape
    d = cols // groups
    tr = _row_tile(rows, cols * 4)

    def body(x_ref, g_ref, dy_ref, dx_ref, dg_ref):
        @pl.when(pl.program_id(0) == 0)
        def _():
            dg_ref[...] = jnp.zeros_like(dg_ref)

        for gi in range(groups):
            sl = slice(gi * d, (gi + 1) * d)
            xs = x_ref[:, sl]
            dys = dy_ref[:, sl]
            r = lax.rsqrt(jnp.mean(xs * xs, axis=-1, keepdims=True) + EPS)
            xhat = xs * r
            dg_ref[:, sl] += jnp.sum(dys * xhat, axis=0, keepdims=True)
            dxhat = dys * g_ref[:, sl]
            dx_ref[:, sl] = r * (dxhat - xhat * jnp.mean(dxhat * xhat, axis=-1, keepdims=True))

    return pl.pallas_call(
        body, name="rms_bwd", grid=(rows // tr,),
        in_specs=[pl.BlockSpec((tr, cols), lambda i: (i, 0)), pl.BlockSpec((1, cols), lambda i: (0, 0)),
                  pl.BlockSpec((tr, cols), lambda i: (i, 0))],
        out_specs=[pl.BlockSpec((tr, cols), lambda i: (i, 0)), pl.BlockSpec((1, cols), lambda i: (0, 0))],
        out_shape=[jax.ShapeDtypeStruct((rows, cols), F32), jax.ShapeDtypeStruct((1, cols), F32)],
        compiler_params=pltpu.CompilerParams(dimension_semantics=("arbitrary",)),
    )(x, g, dy)


@functools.partial(jax.custom_vjp, nondiff_argnums=(2,))
def rms_norm(x, g, groups):
    return _rms_fwd_call(x, g, groups)


def _rms_norm_fwd(x, g, groups):
    return _rms_fwd_call(x, g, groups), (x, g)


def _rms_norm_bwd(groups, res, dy):
    x, g = res
    dx, dg = _rms_bwd_call(x, g, dy, groups)
    return dx, dg


rms_norm.defvjp(_rms_norm_fwd, _rms_norm_bwd)


NT_DIMS = (((1,), (1,)), ((), ()))
TN_DIMS = (((0,), (0,)), ((), ()))


LOG2E = 1.4426950408889634
ATTN_VMEM_LIMIT = 60 * 1024 * 1024

def _exp_rows(q, k, scale):
    s2 = lax.dot_general(q, k, NT_DIMS, preferred_element_type=F32) * (scale * LOG2E)
    e = jnp.exp2(s2 - jnp.max(s2, axis=-1, keepdims=True))
    return e, 1.0 / jnp.sum(e, axis=-1, keepdims=True)


def _attn_fwd_call(q, k, v, scale):
    b, h, s, dk = q.shape
    hkv, dv = k.shape[1], v.shape[3]
    rep = h // hkv
    tq = _pick(s, (1024, 512, 256, 128))

    def body(q_ref, k_ref, v_ref, o_ref):
        e, inv = _exp_rows(q_ref[0, 0], k_ref[0, 0], scale)
        o_ref[0, 0] = jnp.dot(e.astype(BF16), v_ref[0, 0], preferred_element_type=F32) * inv

    return pl.pallas_call(
        body, name="attn_fwd", grid=(b, h, s // tq),
        in_specs=[pl.BlockSpec((1, 1, tq, dk), lambda bi, hi, qi: (bi, hi, qi, 0)),
                  pl.BlockSpec((1, 1, s, dk), lambda bi, hi, qi: (bi, hi // rep, 0, 0)),
                  pl.BlockSpec((1, 1, s, dv), lambda bi, hi, qi: (bi, hi // rep, 0, 0))],
        out_specs=pl.BlockSpec((1, 1, tq, dv), lambda bi, hi, qi: (bi, hi, qi, 0)),
        out_shape=jax.ShapeDtypeStruct((b, h, s, dv), F32),
        compiler_params=pltpu.CompilerParams(dimension_semantics=("parallel", "parallel", "parallel"),
                                             vmem_limit_bytes=ATTN_VMEM_LIMIT),
    )(q, k, v)


def _attn_bwd_call(q, k, v, do, scale):
    b, h, s, dk = q.shape
    hkv, dv = k.shape[1], v.shape[3]
    rep = h // hkv
    tq = _pick(s, (1024, 512, 256, 128))

    def body(q_ref, k_ref, v_ref, do_ref, dq_ref, dk_ref, dv_ref):
        @pl.when((pl.program_id(2) == 0) & (pl.program_id(3) == 0))
        def _():
            dk_ref[...] = jnp.zeros_like(dk_ref)
            dv_ref[...] = jnp.zeros_like(dv_ref)

        qb = q_ref[0, 0]
        kb = k_ref[0, 0]
        vb = v_ref[0, 0]
        dob = do_ref[0, 0]
        e, inv = _exp_rows(qb, kb, scale)
        dp = lax.dot_general(dob, vb, NT_DIMS, preferred_element_type=F32)
        delta = jnp.sum(e * dp, axis=-1, keepdims=True) * inv
        ds = (e * ((dp - delta) * (inv * scale))).astype(BF16)
        dq_ref[0, 0] = jnp.dot(ds, kb, preferred_element_type=F32)
        dk_ref[0, 0] += lax.dot_general(ds, qb, TN_DIMS, preferred_element_type=F32)
        dv_ref[0, 0] += lax.dot_general(e.astype(BF16), (dob.astype(F32) * inv).astype(BF16), TN_DIMS,
                                        preferred_element_type=F32)

    return pl.pallas_call(
        body, name="attn_bwd", grid=(b, hkv, rep, s // tq),
        in_specs=[pl.BlockSpec((1, 1, tq, dk), lambda bi, gi, ri, qi: (bi, gi * rep + ri, qi, 0)),
                  pl.BlockSpec((1, 1, s, dk), lambda bi, gi, ri, qi: (bi, gi, 0, 0)),
                  pl.BlockSpec((1, 1, s, dv), lambda bi, gi, ri, qi: (bi, gi, 0, 0)),
                  pl.BlockSpec((1, 1, tq, dv), lambda bi, gi, ri, qi: (bi, gi * rep + ri, qi, 0))],
        out_specs=[pl.BlockSpec((1, 1, tq, dk), lambda bi, gi, ri, qi: (bi, gi * rep + ri, qi, 0)),
                   pl.BlockSpec((1, 1, s, dk), lambda bi, gi, ri, qi: (bi, gi, 0, 0)),
                   pl.BlockSpec((1, 1, s, dv), lambda bi, gi, ri, qi: (bi, gi, 0, 0))],
        out_shape=[jax.ShapeDtypeStruct(q.shape, F32), jax.ShapeDtypeStruct(k.shape, F32),
                   jax.ShapeDtypeStruct(v.shape, F32)],
        compiler_params=pltpu.CompilerParams(
            dimension_semantics=("parallel", "parallel", "arbitrary", "arbitrary"), vmem_limit_bytes=ATTN_VMEM_LIMIT),
    )(q, k, v, do)


@functools.partial(jax.custom_vjp, nondiff_argnums=(3,))
def attention(q, k, v, scale):
    return _attn_fwd_call(q.astype(BF16), k.astype(BF16), v.astype(BF16), scale)


def _attention_fwd(q, k, v, scale):
    qb, kb, vb = q.astype(BF16), k.astype(BF16), v.astype(BF16)
    return _attn_fwd_call(qb, kb, vb, scale), (qb, kb, vb)


def _attention_bwd(scale, res, do):
    qb, kb, vb = res
    return tuple(_attn_bwd_call(qb, kb, vb, do.astype(BF16), scale))


attention.defvjp(_attention_fwd, _attention_bwd)


CONV_COL_TILE = 256
CONV_PACK_ROWS = 8


def _shifted(x, off, rows):
    if off == 0:
        return x
    s = x.shape[0]
    rolled = pltpu.roll(x, (-off) % s, 0)
    valid = (rows + off >= 0) & (rows + off < s)
    return jnp.where(valid, rolled, 0.0)


def _conv_pre(x, wb_ref, rows):
    z = jnp.zeros_like(x) + wb_ref[SSD_CONV:SSD_CONV + 1, :]
    for j in range(SSD_CONV):
        z = z + wb_ref[j:j + 1, :] * _shifted(x, j - SSD_CONV // 2, rows)
    return z


def _conv_fwd_call(x, wb):
    b, s, c = x.shape
    tc = _pick(c, (CONV_COL_TILE, LANES))

    def body(x_ref, wb_ref, y_ref):
        xv = x_ref[0]
        rows = lax.broadcasted_iota(jnp.int32, xv.shape, 0)
        z = _conv_pre(xv, wb_ref, rows)
        y_ref[0] = z * jax.nn.sigmoid(z)

    return pl.pallas_call(
        body, name="conv_fwd", grid=(b, c // tc),
        in_specs=[pl.BlockSpec((1, s, tc), lambda bi, ci: (bi, 0, ci)),
                  pl.BlockSpec((CONV_PACK_ROWS, tc), lambda bi, ci: (0, ci))],
        out_specs=pl.BlockSpec((1, s, tc), lambda bi, ci: (bi, 0, ci)),
        out_shape=jax.ShapeDtypeStruct(x.shape, F32),
        compiler_params=pltpu.CompilerParams(dimension_semantics=("parallel", "parallel")),
    )(x, wb)


def _conv_bwd_call(x, wb, dy):
    b, s, c = x.shape
    tc = _pick(c, (CONV_COL_TILE, LANES))

    def body(x_ref, wb_ref, dy_ref, dx_ref, dwb_ref):
        xv = x_ref[0]
        rows = lax.broadcasted_iota(jnp.int32, xv.shape, 0)
        z = _conv_pre(xv, wb_ref, rows)
        sg = jax.nn.sigmoid(z)
        dz = dy_ref[0] * (sg * (1.0 + z * (1.0 - sg)))
        dx = jnp.zeros_like(xv)
        for j in range(SSD_CONV):
            off = j - SSD_CONV // 2
            dx = dx + wb_ref[j:j + 1, :] * _shifted(dz, -off, rows)
            dwb_ref[0, j:j + 1, :] = jnp.sum(dz * _shifted(xv, off, rows), axis=0, keepdims=True)
        dx_ref[0] = dx
        dwb_ref[0, SSD_CONV:SSD_CONV + 1, :] = jnp.sum(dz, axis=0, keepdims=True)
        dwb_ref[0, SSD_CONV + 1:, :] = jnp.zeros((CONV_PACK_ROWS - SSD_CONV - 1, dz.shape[1]), F32)

    return pl.pallas_call(
        body, name="conv_bwd", grid=(b, c // tc),
        in_specs=[pl.BlockSpec((1, s, tc), lambda bi, ci: (bi, 0, ci)),
                  pl.BlockSpec((CONV_PACK_ROWS, tc), lambda bi, ci: (0, ci)),
                  pl.BlockSpec((1, s, tc), lambda bi, ci: (bi, 0, ci))],
        out_specs=[pl.BlockSpec((1, s, tc), lambda bi, ci: (bi, 0, ci)),
                   pl.BlockSpec((1, CONV_PACK_ROWS, tc), lambda bi, ci: (bi, 0, ci))],
        out_shape=[jax.ShapeDtypeStruct(x.shape, F32), jax.ShapeDtypeStruct((b, CONV_PACK_ROWS, c), F32)],
        compiler_params=pltpu.CompilerParams(dimension_semantics=("parallel", "parallel")),
    )(x, wb, dy)


@jax.custom_vjp
def conv_silu(x, wb):
    return _conv_fwd_call(x, wb)


def _conv_silu_fwd(x, wb):
    return _conv_fwd_call(x, wb), (x, wb)


def _conv_silu_bwd(res, dy):
    x, wb = res
    dx, dwb = _conv_bwd_call(x, wb, dy)
    return dx, jnp.sum(dwb, axis=0)


conv_silu.defvjp(_conv_silu_fwd, _conv_silu_bwd)


SSD_PAIRS = SSD_GROUP_HEADS // 2
NEG_INF = -1e30


def _ssd_common(x_ref, dtx_ref, dtt_ref, anx_ref, anc_ref, b_ref, c_ref, reverse):
    L = SSD_CHUNK
    xv = x_ref[0]
    dt = dtx_ref[0]
    ri = lax.broadcasted_iota(jnp.int32, (L, L), 0)
    ci = lax.broadcasted_iota(jnp.int32, (L, L), 1)
    causal = (ri <= ci) if reverse else (ri >= ci)
    tri = causal.astype(F32)
    a_cs = jnp.dot(tri, dt * anx_ref[...], precision=HIGHEST, preferred_element_type=F32)
    a_row = dtt_ref[0, 0] * anc_ref[0]
    acs_row = lax.dot_general(a_row, tri, NT_DIMS, precision=HIGHEST, preferred_element_type=F32)
    xd = xv * dt
    bmat = b_ref[0].astype(BF16)
    cmat = c_ref[0].astype(BF16)
    gmat = lax.dot_general(cmat, bmat, NT_DIMS, preferred_element_type=F32)
    return xv, dt, causal, tri, a_cs, acs_row, xd, bmat, cmat, gmat


def _ssd_lambda(a_cs, acs_row, causal, h):
    col = a_cs[:, h * SSD_HEAD_DIM:h * SSD_HEAD_DIM + 1]
    row = acs_row[h:h + 1, :]
    return jnp.exp(jnp.where(causal, col - row, NEG_INF))


def _ssd_fwd_call(x, dtx, dtt, anx, anc, bm, cm, reverse):
    b, s, _ = x.shape
    L, N, GW = SSD_CHUNK, SSD_STATE, SSD_GROUP_WIDTH
    nc = s // L
    end = 0 if reverse else L - 1

    def body(x_ref, dtx_ref, dtt_ref, anx_ref, anc_ref, b_ref, c_ref, y_ref, hs_ref, state):
        @pl.when(pl.program_id(2) == 0)
        def _():
            state[...] = jnp.zeros_like(state)

        xv, dt, causal, tri, a_cs, acs_row, xd, bmat, cmat, gmat = _ssd_common(
            x_ref, dtx_ref, dtt_ref, anx_ref, anc_ref, b_ref, c_ref, reverse)
        hin = state[...]
        hs_ref[0, 0, 0] = hin
        y_off = jnp.dot(cmat, hin.astype(BF16), preferred_element_type=F32) * jnp.exp(a_cs)
        a_end = a_cs[end:end + 1, :]
        s_new = lax.dot_general(bmat, (xd * jnp.exp(a_end - a_cs)).astype(BF16), TN_DIMS, preferred_element_type=F32)
        state[...] = jnp.exp(a_end) * hin + s_new
        lane = lax.broadcasted_iota(jnp.int32, (L, LANES), 1)
        for pr in range(SSD_PAIRS):
            sl = slice(pr * LANES, (pr + 1) * LANES)
            xdp = xd[:, sl].astype(BF16)
            w0 = (gmat * _ssd_lambda(a_cs, acs_row, causal, 2 * pr)).astype(BF16)
            w1 = (gmat * _ssd_lambda(a_cs, acs_row, causal, 2 * pr + 1)).astype(BF16)
            y0 = jnp.dot(w0, xdp, preferred_element_type=F32)
            y1 = jnp.dot(w1, xdp, preferred_element_type=F32)
            y_ref[0, :, sl] = jnp.where(lane < SSD_HEAD_DIM, y0, y1) + y_off[:, sl]

    G = SSD_GROUPS
    chunk = (lambda c: nc - 1 - c) if reverse else (lambda c: c)
    seq = lambda bi, gi, c: (bi, chunk(c), gi)
    return pl.pallas_call(
        body, name="ssd_fwd", grid=(b, G, nc),
        in_specs=[pl.BlockSpec((1, L, GW), seq),
                  pl.BlockSpec((1, L, GW), seq),
                  pl.BlockSpec((1, 1, SSD_GROUP_HEADS, L), lambda bi, gi, c: (bi, gi, 0, chunk(c))),
                  pl.BlockSpec((1, GW), lambda bi, gi, c: (0, gi)),
                  pl.BlockSpec((1, SSD_GROUP_HEADS, 1), lambda bi, gi, c: (gi, 0, 0)),
                  pl.BlockSpec((1, L, N), seq),
                  pl.BlockSpec((1, L, N), seq)],
        out_specs=[pl.BlockSpec((1, L, GW), seq),
                   pl.BlockSpec((1, 1, 1, N, GW), lambda bi, gi, c: (bi, gi, chunk(c), 0, 0))],
        out_shape=[jax.ShapeDtypeStruct(x.shape, F32), jax.ShapeDtypeStruct((b, G, nc, N, GW), F32)],
        scratch_shapes=[pltpu.VMEM((N, GW), F32)],
        compiler_params=pltpu.CompilerParams(dimension_semantics=("parallel", "parallel", "arbitrary")),
    )(x, dtx, dtt, anx, anc, bm, cm)


def _ssd_bwd_call(x, dtx, dtt, anx, anc, bm, cm, hs, dy, reverse):
    b, s, _ = x.shape
    L, N, GW = SSD_CHUNK, SSD_STATE, SSD_GROUP_WIDTH
    nc = s // L
    end = 0 if reverse else L - 1

    def body(x_ref, dtx_ref, dtt_ref, anx_ref, anc_ref, b_ref, c_ref, hs_ref, dy_ref,
             dx_ref, ddt_ref, dan_ref, db_ref, dc_ref, dstate):
        @pl.when(pl.program_id(2) == 0)
        def _():
            dstate[...] = jnp.zeros_like(dstate)

        xv, dt, causal, tri, a_cs, acs_row, xd, bmat, cmat, gmat = _ssd_common(
            x_ref, dtx_ref, dtt_ref, anx_ref, anc_ref, b_ref, c_ref, reverse)
        hin = hs_ref[0, 0, 0]
        hinb = hin.astype(BF16)
        dyv = dy_ref[0]
        ds_out = dstate[...]
        dsb = ds_out.astype(BF16)
        eacs = jnp.exp(a_cs)
        a_end = a_cs[end:end + 1, :]
        e_end = jnp.exp(a_end)
        dec = jnp.exp(a_end - a_cs)
        dye = dyv * eacs
        dyeb = dye.astype(BF16)
        xdec = xd * dec
        ch = jnp.dot(cmat, hinb, preferred_element_type=F32)
        bds = jnp.dot(bmat, dsb, preferred_element_type=F32)
        t_state = xdec * bds
        d_aend = jnp.sum(t_state, axis=0, keepdims=True) + e_end * jnp.sum(ds_out * hin, axis=0, keepdims=True)
        dacs = dye * ch - t_state
        dxd_state = bds * dec
        dstate[...] = e_end * ds_out + lax.dot_general(cmat, dyeb, TN_DIMS, preferred_element_type=F32)

        lane = lax.broadcasted_iota(jnp.int32, (L, LANES), 1)
        dg = jnp.zeros((L, L), F32)
        dxd_parts, dacs_parts = [], []
        for pr in range(SSD_PAIRS):
            sl = slice(pr * LANES, (pr + 1) * LANES)
            xdp = xd[:, sl]
            dyp = dyv[:, sl]
            dxd_p = jnp.zeros((L, LANES), F32)
            dacs_p = jnp.zeros((L, LANES), F32)
            for half in range(2):
                mine = (lane < SSD_HEAD_DIM) if half == 0 else (lane >= SSD_HEAD_DIM)
                lam = _ssd_lambda(a_cs, acs_row, causal, 2 * pr + half)
                w = gmat * lam
                xdh = jnp.where(mine, xdp, 0.0).astype(BF16)
                dyh = jnp.where(mine, dyp, 0.0).astype(BF16)
                dw = lax.dot_general(dyh, xdh, NT_DIMS, preferred_element_type=F32)
                dg = dg + dw * lam
                mm = dw * w
                rs = jnp.sum(mm, axis=1, keepdims=True)
                cs = jnp.sum(mm.T, axis=1, keepdims=True)
                dacs_p = dacs_p + jnp.where(mine, (rs - cs) * (1.0 / SSD_HEAD_DIM), 0.0)
                wtdy = lax.dot_general(w.astype(BF16), dyh, TN_DIMS, preferred_element_type=F32)
                dxd_p = dxd_p + wtdy
            dxd_parts.append(dxd_p)
            dacs_parts.append(dacs_p)
        dxd = jnp.concatenate(dxd_parts, axis=1) + dxd_state
        dacs = dacs + jnp.concatenate(dacs_parts, axis=1)
        last = lax.broadcasted_iota(jnp.int32, dacs.shape, 0) == end
        dacs = dacs + jnp.where(last, d_aend, 0.0)
        da = lax.dot_general(tri, dacs, TN_DIMS, precision=HIGHEST, preferred_element_type=F32)
        dgb = dg.astype(BF16)
        dc_ref[0] = (jnp.dot(dgb, bmat, preferred_element_type=F32)
                     + lax.dot_general(dyeb, hinb, NT_DIMS, preferred_element_type=F32))
        db_ref[0] = (lax.dot_general(dgb, cmat, TN_DIMS, preferred_element_type=F32)
                     + lax.dot_general(xdec.astype(BF16), dsb, NT_DIMS, preferred_element_type=F32))
        dx_ref[0] = dxd * dt
        ddt_ref[0] = da * anx_ref[...] + dxd * xv
        dan_ref[0, 0, 0] = jnp.sum(da * dt, axis=0, keepdims=True)

    G = SSD_GROUPS
    chunk = (lambda c: c) if reverse else (lambda c: nc - 1 - c)
    rev = lambda bi, gi, c: (bi, chunk(c), gi)
    return pl.pallas_call(
        body, name="ssd_bwd", grid=(b, G, nc),
        in_specs=[pl.BlockSpec((1, L, GW), rev),
                  pl.BlockSpec((1, L, GW), rev),
                  pl.BlockSpec((1, 1, SSD_GROUP_HEADS, L), lambda bi, gi, c: (bi, gi, 0, chunk(c))),
                  pl.BlockSpec((1, GW), lambda bi, gi, c: (0, gi)),
                  pl.BlockSpec((1, SSD_GROUP_HEADS, 1), lambda bi, gi, c: (gi, 0, 0)),
                  pl.BlockSpec((1, L, N), rev),
                  pl.BlockSpec((1, L, N), rev),
                  pl.BlockSpec((1, 1, 1, N, GW), lambda bi, gi, c: (bi, gi, chunk(c), 0, 0)),
                  pl.BlockSpec((1, L, GW), rev)],
        out_specs=[pl.BlockSpec((1, L, GW), rev),
                   pl.BlockSpec((1, L, GW), rev),
                   pl.BlockSpec((1, 1, 1, 1, GW), lambda bi, gi, c: (bi, gi, chunk(c), 0, 0)),
                   pl.BlockSpec((1, L, N), rev),
                   pl.BlockSpec((1, L, N), rev)],
        out_shape=[jax.ShapeDtypeStruct(x.shape, F32), jax.ShapeDtypeStruct(x.shape, F32),
                   jax.ShapeDtypeStruct((b, G, nc, 1, GW), F32),
                   jax.ShapeDtypeStruct(bm.shape, F32), jax.ShapeDtypeStruct(cm.shape, F32)],
        scratch_shapes=[pltpu.VMEM((N, GW), F32)],
        compiler_params=pltpu.CompilerParams(dimension_semantics=("parallel", "parallel", "arbitrary")),
    )(x, dtx, dtt, anx, anc, bm, cm, hs, dy)


@functools.partial(jax.custom_vjp, nondiff_argnums=(7,))
def _ssd_scan(x, dtx, dtt, anx, anc, bm, cm, reverse):
    return _ssd_fwd_call(x, dtx, dtt, anx, anc, bm, cm, reverse)[0]


def _ssd_scan_fwd(x, dtx, dtt, anx, anc, bm, cm, reverse):
    y, hs = _ssd_fwd_call(x, dtx, dtt, anx, anc, bm, cm, reverse)
    return y, (x, dtx, dtt, anx, anc, bm, cm, hs)


def _ssd_scan_bwd(reverse, res, dy):
    x, dtx, dtt, anx, anc, bm, cm, hs = res
    dx, ddtx, dan, db, dc = _ssd_bwd_call(x, dtx, dtt, anx, anc, bm, cm, hs, dy, reverse)
    b, g, nc, _, gw = dan.shape
    danx = jnp.sum(dan, axis=(0, 2, 3)).reshape(1, g * gw)
    return dx, ddtx, jnp.zeros_like(dtt), danx, jnp.zeros_like(anc), db, dc


_ssd_scan.defvjp(_ssd_scan_fwd, _ssd_scan_bwd)


def ssd_chunked(xs, dt, a_neg, bm, cm, reverse):
    b, s, _ = xs.shape
    dtx = jnp.repeat(dt, SSD_HEAD_DIM, axis=-1)
    dtt = jnp.transpose(dt, (0, 2, 1)).reshape(b, SSD_GROUPS, SSD_GROUP_HEADS, s)
    anx = jnp.repeat(a_neg, SSD_HEAD_DIM)[None, :]
    anc = a_neg.reshape(SSD_GROUPS, SSD_GROUP_HEADS, 1)
    return _ssd_scan(xs, dtx, dtt, anx, anc, bm, cm, reverse)


def _loss_call(y, t):
    rows, cols = y.shape
    tr = _row_tile(rows, cols * 4)

    def body(y_ref, t_ref, loss_ref, diff_ref):
        @pl.when(pl.program_id(0) == 0)
        def _():
            loss_ref[...] = jnp.zeros_like(loss_ref)

        d = y_ref[...] - t_ref[...]
        diff_ref[...] = d * (1.0 / cols)
        part = jnp.sum(jnp.sum(d * d, axis=1, keepdims=True), axis=0, keepdims=True)
        loss_ref[...] += part * (0.5 / cols)

    return pl.pallas_call(
        body, name="loss_head", grid=(rows // tr,),
        in_specs=[pl.BlockSpec((tr, cols), lambda i: (i, 0)), pl.BlockSpec((tr, cols), lambda i: (i, 0))],
        out_specs=[pl.BlockSpec((1, 1), lambda i: (0, 0)), pl.BlockSpec((tr, cols), lambda i: (i, 0))],
        out_shape=[jax.ShapeDtypeStruct((1, 1), F32), jax.ShapeDtypeStruct((rows, cols), F32)],
        compiler_params=pltpu.CompilerParams(dimension_semantics=("arbitrary",)),
    )(y, t)


@jax.custom_vjp
def loss_head(y, t):
    return _loss_call(y, t)[0][0, 0]


def _loss_head_fwd(y, t):
    loss, diff = _loss_call(y, t)
    return loss[0, 0], diff


def _loss_head_bwd(diff, g):
    return g * diff, jnp.zeros_like(diff)


loss_head.defvjp(_loss_head_fwd, _loss_head_bwd)


def _axial_rope_tables(seq_len, rot_dim):
    rows = seq_len // GRID_W
    row_idx = jnp.repeat(jnp.arange(rows), GRID_W).astype(F32)
    col_idx = jnp.tile(jnp.arange(GRID_W), rows).astype(F32)
    axis_dim = rot_dim // 2
    inv_freq = jnp.power(ROPE_THETA, -jnp.arange(0, axis_dim, 2, dtype=F32) / axis_dim)
    ang_r = row_idx[:, None] * inv_freq[None, :]
    ang_c = col_idx[:, None] * inv_freq[None, :]
    return jnp.cos(ang_r), jnp.sin(ang_r), jnp.cos(ang_c), jnp.sin(ang_c)


def _rotate(x, cos, sin):
    x1, x2 = jnp.split(x, 2, axis=-1)
    cos = cos[:, None, :]
    sin = sin[:, None, :]
    return jnp.concatenate([x1 * cos - x2 * sin, x1 * sin + x2 * cos], axis=-1)


def _apply_axial_rope(x, tables):
    cos_r, sin_r, cos_c, sin_c = tables
    x_row, x_col = jnp.split(x, 2, axis=-1)
    return jnp.concatenate([_rotate(x_row, cos_r, sin_r), _rotate(x_col, cos_c, sin_c)], axis=-1)


def _heads_first(t):
    return jnp.transpose(t, (0, 2, 1, 3))


def _gqa_group(q, k, v, q_norm_g, k_norm_g, rope, b, s):
    q = rms_norm(q, jnp.tile(q_norm_g, GQA_HEADS)[None, :], GQA_HEADS).reshape(b, s, GQA_HEADS, GQA_HEAD_DIM)
    k = rms_norm(k, jnp.tile(k_norm_g, GQA_KV_HEADS)[None, :], GQA_KV_HEADS).reshape(b, s, GQA_KV_HEADS, GQA_HEAD_DIM)
    v = v.reshape(b, s, GQA_KV_HEADS, GQA_HEAD_DIM)
    q = _apply_axial_rope(q, rope)
    k = _apply_axial_rope(k, rope)
    o = attention(_heads_first(q), _heads_first(k), _heads_first(v), GQA_HEAD_DIM ** -0.5)
    return _heads_first(o).reshape(b * s, GQA_WIDTH)


def _mla_group(c_q, c_kv, k_pe, q_norm_g, w_uq, kv_norm_g, w_ukv, rope, b, s):
    q = linear(rms_norm(c_q, q_norm_g[None, :], 1), w_uq).reshape(b, s, MLA_HEADS, MLA_NOPE_DIM + MLA_ROPE_DIM)
    q_nope, q_pe = q[..., :MLA_NOPE_DIM], q[..., MLA_NOPE_DIM:]
    kv = linear(rms_norm(c_kv, kv_norm_g[None, :], 1), w_ukv).reshape(b, s, MLA_HEADS, MLA_NOPE_DIM + MLA_V_DIM)
    k_nope, v = kv[..., :MLA_NOPE_DIM], kv[..., MLA_NOPE_DIM:]
    q_pe = _apply_axial_rope(q_pe, rope)
    k_pe = _apply_axial_rope(k_pe.reshape(b, s, 1, MLA_ROPE_DIM), rope)
    q = jnp.concatenate([q_nope, q_pe], axis=-1)
    k = jnp.concatenate([k_nope, jnp.broadcast_to(k_pe, (b, s, MLA_HEADS, MLA_ROPE_DIM))], axis=-1)
    o = attention(_heads_first(q), _heads_first(k), _heads_first(v), (MLA_NOPE_DIM + MLA_ROPE_DIM) ** -0.5)
    return _heads_first(o).reshape(b * s, MLA_HEADS * MLA_V_DIM)


def _ssd_group(z, xbc, dt_raw, conv_w, conv_b, dt_bias, a_log, d_skip, norm_g, b, s):
    wb = jnp.concatenate([conv_w, conv_b[None, :], jnp.zeros((CONV_PACK_ROWS - SSD_CONV - 1, SSD_CONV_DIM), F32)], axis=0)
    xbc = conv_silu(xbc.reshape(b, s, SSD_CONV_DIM), wb)
    xs = xbc[..., :SSD_INNER]
    bm = xbc[..., SSD_INNER:SSD_INNER + SSD_GROUPS * SSD_STATE]
    cm = xbc[..., SSD_INNER + SSD_GROUPS * SSD_STATE:]
    dt = jax.nn.softplus(dt_raw.reshape(b, s, 2, SSD_HEADS) + dt_bias)
    a_neg = -jnp.exp(a_log)
    y_fwd = ssd_chunked(xs, dt[:, :, 0], a_neg[0], bm, cm, False)
    y_bwd = ssd_chunked(xs, dt[:, :, 1], a_neg[1], bm, cm, True)
    y = y_fwd + y_bwd + xs * jnp.repeat(d_skip, SSD_HEAD_DIM)
    y = y.reshape(b * s, SSD_INNER) * jax.nn.silu(z)
    return rms_norm(y, norm_g[None, :], SSD_GROUPS)


MIXER_WEIGHTS = ('q_norm_g', 'k_norm_g', 'mla_q_norm_g', 'w_uq', 'mla_kv_norm_g', 'w_ukv', 'conv_w', 'conv_b',
                 'dt_bias', 'a_log', 'd_skip', 'ssd_norm_g')


def _mixer(proj, w, rope_a, rope_b, b, s):
    idx = np.cumsum(IN_SPLITS).tolist()
    q_a, k_a, v_a, cq_b, ckv_b, kpe_b, z_c, xbc_c, dt_c = [_in_cols(proj, lo, hi)
                                                           for lo, hi in zip([0] + idx[:-1], idx)]
    o_a = _gqa_group(q_a, k_a, v_a, w["q_norm_g"], w["k_norm_g"], rope_a, b, s)
    o_b = _mla_group(cq_b, ckv_b, kpe_b, w["mla_q_norm_g"], w["w_uq"], w["mla_kv_norm_g"], w["w_ukv"], rope_b, b, s)
    o_c = _ssd_group(z_c, xbc_c, dt_c, w["conv_w"], w["conv_b"], w["dt_bias"], w["a_log"], w["d_skip"],
                     w["ssd_norm_g"], b, s)
    return jnp.concatenate([o_a, o_b, o_c], axis=-1)


def _seq_tile(s, row_bytes):
    return _row_tile(s, row_bytes)


def _normmod_fwd(x, g, scale, shift):
    b, s, d = x.shape
    tr = _seq_tile(s, d * 4)

    def body(x_ref, g_ref, sc_ref, sh_ref, h_ref):
        xv = x_ref[0]
        r = lax.rsqrt(jnp.mean(xv * xv, axis=-1, keepdims=True) + EPS)
        h_ref[0] = (xv * r * g_ref[...] * (1.0 + sc_ref[0]) + sh_ref[0]).astype(BF16)

    act = pl.BlockSpec((1, tr, d), lambda bi, i: (bi, i, 0))
    vec = pl.BlockSpec((1, 1, d), lambda bi, i: (bi, 0, 0))
    return pl.pallas_call(
        body, name="normmod_fwd", grid=(b, s // tr),
        in_specs=[act, pl.BlockSpec((1, d), lambda bi, i: (0, 0)), vec, vec], out_specs=act,
        out_shape=jax.ShapeDtypeStruct((b, s, d), BF16),
        compiler_params=pltpu.CompilerParams(dimension_semantics=("parallel", "parallel")),
    )(x, g, scale, shift)


def _normmod_bwd(x, g, scale, dh, resid):
    b, s, d = x.shape
    tr = _seq_tile(s, d * 4)

    def body(x_ref, g_ref, sc_ref, dh_ref, res_ref, dx_ref, dg_ref, dsc_ref, dsh_ref):
        bi, i = pl.program_id(0), pl.program_id(1)

        @pl.when((bi == 0) & (i == 0))
        def _():
            dg_ref[...] = jnp.zeros_like(dg_ref)

        @pl.when(i == 0)
        def _():
            dsc_ref[...] = jnp.zeros_like(dsc_ref)
            dsh_ref[...] = jnp.zeros_like(dsh_ref)

        xv = x_ref[0]
        dhv = dh_ref[0]
        gv = g_ref[...]
        r = lax.rsqrt(jnp.mean(xv * xv, axis=-1, keepdims=True) + EPS)
        xhat = xv * r
        dsh_ref[0] += jnp.sum(dhv, axis=0, keepdims=True)
        dsc_ref[0] += jnp.sum(dhv * (xhat * gv), axis=0, keepdims=True)
        dn = dhv * (1.0 + sc_ref[0])
        dg_ref[...] += jnp.sum(dn * xhat, axis=0, keepdims=True)
        dxhat = dn * gv
        dx_ref[0] = r * (dxhat - xhat * jnp.mean(dxhat * xhat, axis=-1, keepdims=True)) + res_ref[0]

    act = pl.BlockSpec((1, tr, d), lambda bi, i: (bi, i, 0))
    vec = pl.BlockSpec((1, 1, d), lambda bi, i: (bi, 0, 0))
    gain = pl.BlockSpec((1, d), lambda bi, i: (0, 0))
    return pl.pallas_call(
        body, name="normmod_bwd", grid=(b, s // tr),
        in_specs=[act, gain, vec, act, act], out_specs=[act, gain, vec, vec],
        out_shape=[jax.ShapeDtypeStruct((b, s, d), F32), jax.ShapeDtypeStruct((1, d), F32),
                   jax.ShapeDtypeStruct((b, 1, d), F32), jax.ShapeDtypeStruct((b, 1, d), F32)],
        compiler_params=pltpu.CompilerParams(dimension_semantics=("arbitrary", "arbitrary")),
    )(x, g, scale, dh, resid)


def _gated_add(x, gate, t):
    b, s, d = x.shape
    tr = _seq_tile(s, d * 4)

    def body(x_ref, g_ref, t_ref, o_ref):
        o_ref[0] = x_ref[0] + g_ref[0] * t_ref[0]

    act = pl.BlockSpec((1, tr, d), lambda bi, i: (bi, i, 0))
    vec = pl.BlockSpec((1, 1, d), lambda bi, i: (bi, 0, 0))
    return pl.pallas_call(
        body, name="gated_add", grid=(b, s // tr), in_specs=[act, vec, act], out_specs=act,
        out_shape=jax.ShapeDtypeStruct((b, s, d), F32),
        compiler_params=pltpu.CompilerParams(dimension_semantics=("parallel", "parallel")),
    )(x, gate, t)


def _gated_bwd(dy, gate, t):
    b, s, d = dy.shape
    tr = _seq_tile(s, d * 4)

    def body(dy_ref, g_ref, t_ref, dt_ref, dgate_ref):
        @pl.when(pl.program_id(1) == 0)
        def _():
            dgate_ref[...] = jnp.zeros_like(dgate_ref)

        dyv = dy_ref[0]
        dt_ref[0] = (g_ref[0] * dyv).astype(BF16)
        dgate_ref[0] += jnp.sum(dyv * t_ref[0], axis=0, keepdims=True)

    act = pl.BlockSpec((1, tr, d), lambda bi, i: (bi, i, 0))
    vec = pl.BlockSpec((1, 1, d), lambda bi, i: (bi, 0, 0))
    return pl.pallas_call(
        body, name="gated_bwd", grid=(b, s // tr), in_specs=[act, vec, act], out_specs=[act, vec],
        out_shape=[jax.ShapeDtypeStruct((b, s, d), BF16), jax.ShapeDtypeStruct((b, 1, d), F32)],
        compiler_params=pltpu.CompilerParams(dimension_semantics=("parallel", "arbitrary")),
    )(dy, gate, t)


def _swiglu_fwd(gu, plan=None):
    rows, f2 = gu.shape
    f = f2 // 2
    tr = _row_tile(rows, f2 * 4, SWIGLU_TILE_BYTES)
    steps = rows // tr
    p_in, p_out, p_shapes, p_scratch, p_args = _plan_specs(plan)

    def body(gu_ref, *rest):
        a_ref = rest[len(p_in)]
        i = pl.program_id(0)

        def compute():
            gt = gu_ref[:, :f]
            a_ref[...] = (gt * jax.nn.sigmoid(gt) * gu_ref[:, f:]).astype(BF16)

        _ride(plan, rest[:len(p_in)] + rest[len(p_in) + 1:], i == 0, i == steps - 1, compute)

    outs = pl.pallas_call(
        body, name="swiglu_fwd", grid=(steps,),
        in_specs=[pl.BlockSpec((tr, f2), lambda i: (i, 0))] + p_in,
        out_specs=[pl.BlockSpec((tr, f), lambda i: (i, 0))] + p_out,
        out_shape=[jax.ShapeDtypeStruct((rows, f), BF16)] + p_shapes, scratch_shapes=p_scratch,
        compiler_params=pltpu.CompilerParams(dimension_semantics=("arbitrary" if plan is not None else "parallel",)),
    )(gu, *p_args)
    return outs[0] if plan is None else (outs[0], list(outs[1:]))


def _swiglu_bwd(gu, dact):
    rows, f2 = gu.shape
    f = f2 // 2
    tr = _row_tile(rows, f2 * 4, SWIGLU_TILE_BYTES)

    def body(gu_ref, da_ref, dgu_ref):
        gt = gu_ref[:, :f]
        up = gu_ref[:, f:]
        da = da_ref[...]
        sg = jax.nn.sigmoid(gt)
        dgu_ref[:, :f] = (da * up * (sg * (1.0 + gt * (1.0 - sg)))).astype(BF16)
        dgu_ref[:, f:] = (da * gt * sg).astype(BF16)

    return pl.pallas_call(
        body, name="swiglu_bwd", grid=(rows // tr,),
        in_specs=[pl.BlockSpec((tr, f2), lambda i: (i, 0)), pl.BlockSpec((tr, f), lambda i: (i, 0))],
        out_specs=pl.BlockSpec((tr, f2), lambda i: (i, 0)),
        out_shape=jax.ShapeDtypeStruct((rows, f2), BF16),
        compiler_params=pltpu.CompilerParams(dimension_semantics=("parallel",)),
    )(gu, dact)


class _Gathered:
    def __init__(self, shards):
        self.shards, self.full = shards, {}

    def plan(self, keys):
        return _gather_plan([self.shards[k] for k in keys])

    def store(self, keys, outs):
        for key, out in zip(keys, outs):
            name = key[0]
            g = out.reshape((N_CHIPS,) + self.shards[key].shape)
            if name in CHIP_BLOCKED:
                full = g
            elif name in COL_SHARDED and name not in TRANSPOSED:
                full = _cols_full(g).astype(F32)
            else:
                full = g.reshape(g.shape[0] * g.shape[1], g.shape[2])
            self.full[key] = full

    def carry(self, keys, fn):
        if not keys:
            return fn(None)
        res, outs = fn(self.plan(keys))
        self.store(keys, outs)
        return res


def _gather_schedule(depth):
    every = [(n, l) for l in range(depth) for n in ('w_uq', 'w_ukv')]
    sched = {'first': [('w_in', 0), ('w_gate_up', 0)] + every}
    for l in range(depth):
        sched[('w_in_fwd', l)] = [('w_out', l)] + ([('w_in', l + 1)] if l + 1 < depth else [])
        if l == 0:
            sched[('swiglu_fwd', l)] = [('w_down', l)]
        if l + 1 < depth:
            sched[('w_gate_up_fwd', l)] = [('w_gate_up', l + 1)]
            sched[('w_down_fwd', l)] = [('w_down', l + 1)]
    return sched


GATE_UP_PIECES = (1, 1, 3, 3)


class _Reducer:
    def __init__(self):
        self.parts, self.recv, self.result = {}, {}, {}

    def add(self, items):
        blocks = []
        for (name, _), grad in items:
            if name in CHIP_BLOCKED:
                blocks.append(grad)
            elif name in COL_SHARDED and name not in TRANSPOSED:
                blocks.append(_cols_split(grad))
            else:
                blocks.append(grad.reshape(N_CHIPS, grad.shape[0] // N_CHIPS, grad.shape[1]))
        for (key, _), parts in zip(items, _rs_parts(blocks)):
            self.parts[key] = parts
            self.recv[key] = []

    def pieces(self, key):
        rows = self.parts[key].shape[1]
        shares = GATE_UP_PIECES if key[0] == 'w_gate_up' else (1,)
        unit = rows // sum(shares)
        starts = np.cumsum((0,) + shares[:-1])
        return [(key, int(a) * unit, n * unit) for a, n in zip(starts, shares)]

    def plan(self, jobs):
        return _chip_exchange_plan([(self.parts[key], row0, rows) for key, row0, rows in jobs])

    def store(self, jobs, outs):
        complete = []
        for (key, row0, rows), out in zip(jobs, outs):
            self.recv[key].append((row0, out))
            if len(self.recv[key]) == len(self.pieces(key)):
                complete.append(key)
        if complete:
            items = [(self.parts[key], sorted(self.recv[key], key=lambda t: t[0])) for key in complete]
            self.result.update(zip(complete, _rs_result(items)))

    def carry(self, keys, fn, piece=None, also=()):
        jobs = [j for key in keys for j in self.pieces(key)]
        if piece is not None:
            jobs = [j for key in keys for j in self.pieces(key)[piece:piece + 1]]
        jobs += [j for key in also for j in self.pieces(key)]
        if not jobs:
            return fn(None)
        res, outs = fn(self.plan(jobs))
        self.store(jobs, outs)
        return res

    def flush(self):
        jobs = [j for key in self.parts for j in self.pieces(key)
                if key not in self.result and j[1] not in [r for r, _ in self.recv[key]]]
        if jobs:
            self.store(jobs, _run_plan(self.plan(jobs), "rs_chip_exchange"))


def _layer_fwd(x, mod, w, gathered, l, sched, rope_a, rope_b):
    b, s, d = x.shape
    m = b * s
    shift1, scale1, gate1, shift2, scale2, gate2 = [t[:, None, :] for t in jnp.split(mod, 6, axis=-1)]
    g1, g2 = w["norm1_g"][None, :], w["norm2_g"][None, :]
    full = lambda n: gathered.full[(n, l)]
    h1 = _normmod_fwd(x, g1, scale1, shift1).reshape(m, d)
    proj = gathered.carry(sched.get(('w_in_fwd', l)), lambda p: _matmul(h1, full('w_in'), tb=True, name="w_in_fwd", plan=p))
    mixer_w = {n: (full(n) if n in COL_SHARDED else w[n]) for n in MIXER_WEIGHTS}
    o, mixer_vjp = jax.vjp(lambda p, mw: _mixer(p, mw, rope_a, rope_b, b, s), proj, mixer_w)
    o = o.astype(BF16)
    mix = _matmul(o, full('w_out'), name="w_out_fwd").reshape(b, s, d)
    x_mid = _gated_add(x, gate1, mix)
    h2 = _normmod_fwd(x_mid, g2, scale2, shift2).reshape(m, d)
    gu = gathered.carry(sched.get(('w_gate_up_fwd', l)),
                        lambda p: _matmul(h2, full('w_gate_up'), name="w_gate_up_fwd", plan=p, chips='b'))
    act = gathered.carry(sched.get(('swiglu_fwd', l)), lambda p: _swiglu_fwd(gu, plan=p))
    ffn = gathered.carry(sched.get(('w_down_fwd', l)), lambda p: _matmul(act, full('w_down'), name="w_down_fwd", plan=p))
    ffn = ffn.reshape(b, s, d)
    x_out = _gated_add(x_mid, gate2, ffn)
    res = (x, x_mid, h1, h2, o, mix, gu, act, ffn, mixer_vjp, scale1, gate1, scale2, gate2, g1, g2)
    return x_out, res


def _layer_bwd(res, gathered, reducer, l, depth, dx_out):
    x, x_mid, h1, h2, o, mix, gu, act, ffn, mixer_vjp, scale1, gate1, scale2, gate2, g1, g2 = res
    b, s, d = x.shape
    m = b * s
    full = lambda n: gathered.full[(n, l)]
    above = l + 1 < depth
    dffn, dgate2 = _gated_bwd(dx_out, gate2, ffn)
    dffn = dffn.reshape(m, d)
    dact = reducer.carry([('w_out', l + 1), ('w_uq', l + 1), ('w_ukv', l + 1)] if above else [],
                         lambda p: _matmul(dffn, full('w_down'), tb=True, name="w_down_dx", plan=p))
    dw = reducer.carry([('w_in', l + 1)] if above else [],
                       lambda p: _matmul(act, dffn, ta=True, name="w_down_dw", plan=p))
    reducer.add([(('w_down', l), dw)])
    dgu = _swiglu_bwd(gu, dact)
    dh2 = reducer.carry([('w_down', l)], lambda p: _matmul(dgu, full('w_gate_up'), tb=True, name="w_gate_up_dx", plan=p,
                                                           chips='b'))
    dh2 = dh2.reshape(b, s, d)
    reducer.add([(('w_gate_up', l), _matmul(h2, dgu, ta=True, name="w_gate_up_dw", chips='out'))])
    dx_mid, dg2, dscale2, dshift2 = _normmod_bwd(x_mid, g2, scale2, dh2, dx_out)
    dmix, dgate1 = _gated_bwd(dx_mid, gate1, mix)
    dmix = dmix.reshape(m, d)
    gate_up = [('w_gate_up', l)]
    do = reducer.carry(gate_up, lambda p: _matmul(dmix, full('w_out'), tb=True, name="w_out_dx", plan=p), piece=0)
    dw_out = reducer.carry(gate_up, lambda p: _matmul(o, dmix, ta=True, name="w_out_dw", plan=p), piece=1)
    dproj, grads = mixer_vjp(do)
    grads = dict(grads)
    reducer.add([(('w_out', l), dw_out), (('w_uq', l), grads.pop('w_uq')), (('w_ukv', l), grads.pop('w_ukv'))])
    dproj = dproj.astype(BF16)
    bottom = [('w_out', l), ('w_uq', l), ('w_ukv', l)] if l == 0 else []
    dh1 = reducer.carry(gate_up, lambda p: _matmul(dproj, full('w_in'), name="w_in_dx", plan=p), piece=2, also=bottom)
    dh1 = dh1.reshape(b, s, d)
    dw = reducer.carry(gate_up, lambda p: _matmul(dproj, h1, ta=True, name="w_in_dw", plan=p), piece=3)
    reducer.add([(('w_in', l), dw)])
    dx, dg1, dscale1, dshift1 = _normmod_bwd(x, g1, scale1, dh1, dx_mid)
    grads["norm1_g"], grads["norm2_g"] = dg1[0], dg2[0]
    dmod = jnp.concatenate([dshift1, dscale1, dgate1, dshift2, dscale2, dgate2], axis=-1)[:, 0, :]
    return dx, dmod, grads


def _tail_loss(x2, final_norm_g, target2):
    return loss_head(rms_norm(x2, final_norm_g[None, :], 1), target2)


def _forward_backward(x, mod, small, gathered, reducer, final_norm_g, target):
    b, s, d = x.shape
    depth = len(small)
    rope_a = _axial_rope_tables(s, GQA_HEAD_DIM)
    rope_b = _axial_rope_tables(s, MLA_ROPE_DIM)
    sched = _gather_schedule(depth)
    first = sched['first']
    gathered.store(first, _run_plan(gathered.plan(first), "all_gather_chips"))
    saved = []
    for l in range(depth):
        x, res = _layer_fwd(x, mod[l], small[l], gathered, l, sched, rope_a, rope_b)
        saved.append(res)
    loss, (dx2, dfinal) = jax.value_and_grad(_tail_loss, argnums=(0, 1))(
        x.reshape(b * s, d), final_norm_g, target.reshape(b * s, d))
    dx = dx2.reshape(b, s, d)
    dmods, gsmall = [None] * depth, [None] * depth
    for l in reversed(range(depth)):
        dx, dmods[l], gsmall[l] = _layer_bwd(saved[l], gathered, reducer, l, depth, dx)
    return loss, dx, jnp.stack(dmods), gsmall, dfinal


ANY = pl.BlockSpec(memory_space=pl.ANY)


def _flip_if(v, bit):
    return 1 - v if bit else v


def _all_gather_devices(x):
    def body(x_ref, out_ref, send_sems, recv_sems):
        mx, my, mc = lax.axis_index("x"), lax.axis_index("y"), lax.axis_index("c")
        me = 4 * mx + 2 * my + mc
        sends = []
        for k in range(1, N_DEV):
            peer = (_flip_if(mx, k & 4), _flip_if(my, k & 2), _flip_if(mc, k & 1))
            cp = pltpu.make_async_remote_copy(src_ref=x_ref, dst_ref=out_ref.at[me], send_sem=send_sems.at[k - 1],
                                              recv_sem=recv_sems.at[k - 1], device_id=peer, device_id_type=MESH)
            cp.start()
            sends.append(cp)
        for k in range(1, N_DEV):
            peer = (_flip_if(mx, k & 4), _flip_if(my, k & 2), _flip_if(mc, k & 1))
            src = 4 * peer[0] + 2 * peer[1] + peer[2]
            pltpu.make_async_remote_copy(src_ref=x_ref, dst_ref=out_ref.at[src], send_sem=send_sems.at[k - 1],
                                         recv_sem=recv_sems.at[k - 1], device_id=peer, device_id_type=MESH).wait_recv()
        for cp in sends:
            cp.wait_send()

    out = pl.pallas_call(
        body, name="all_gather_devices", in_specs=[ANY], out_specs=ANY,
        out_shape=jax.ShapeDtypeStruct((N_DEV,) + x.shape, x.dtype),
        scratch_shapes=[pltpu.SemaphoreType.DMA((N_DEV - 1,)), pltpu.SemaphoreType.DMA((N_DEV - 1,))],
    )(x)
    me = 4 * lax.axis_index("x") + 2 * lax.axis_index("y") + lax.axis_index("c")
    return lax.dynamic_update_index_in_dim(out, x, me, 0)


class _Plan:
    def __init__(self, inputs, out_shapes, sem_counts, start, finish):
        self.inputs, self.out_shapes, self.sem_counts = list(inputs), list(out_shapes), list(sem_counts)
        self.start, self.finish = start, finish

    def specs(self):
        return ([ANY] * len(self.inputs), [ANY] * len(self.out_shapes),
                [pltpu.SemaphoreType.DMA((c,)) for c in self.sem_counts])

    def split(self, refs):
        a, b = len(self.inputs), len(self.inputs) + len(self.out_shapes)
        return refs[:a], refs[a:b], refs[b:]


def _run_plan(plan, name):
    def body(*refs):
        ins, outs, sems = plan.split(refs)
        plan.start(ins, outs, sems)
        plan.finish(ins, outs, sems)

    in_specs, out_specs, scratch = plan.specs()
    return pl.pallas_call(body, name=name, in_specs=in_specs, out_specs=out_specs, out_shape=plan.out_shapes,
                          scratch_shapes=scratch)(*plan.inputs)


def _gather_plan(shards):
    n = len(shards)
    halves = [t.reshape(2, t.shape[0] // 2, t.shape[1]) for t in shards]
    count = (N_CHIPS - 1) * n

    def copies(kind, ins, outs, sems):
        ici_send, ici_recv, d2d_send, d2d_recv, own_send, own_recv = sems
        mx, my, mc = lax.axis_index("x"), lax.axis_index("y"), lax.axis_index("c")
        me = 2 * mx + my
        sibling = (mx, my, 1 - mc)
        if kind == 'own':
            return [pltpu.make_async_remote_copy(src_ref=ins[i], dst_ref=outs[i].at[me], send_sem=own_send.at[i],
                                                 recv_sem=own_recv.at[i], device_id=sibling, device_id_type=MESH)
                    for i in range(n)]
        cps = []
        for k in range(1, N_CHIPS):
            peer = (_flip_if(mx, k & 2), _flip_if(my, k & 1), mc)
            src = 2 * peer[0] + peer[1]
            for i in range(n):
                j = (k - 1) * n + i
                if kind in ('ici', 'landed'):
                    dst = outs[i].at[me, mc] if kind == 'ici' else outs[i].at[src, mc]
                    cps.append(pltpu.make_async_remote_copy(
                        src_ref=ins[i].at[mc], dst_ref=dst, send_sem=ici_send.at[j], recv_sem=ici_recv.at[j],
                        device_id=peer, device_id_type=MESH))
                else:
                    half = outs[i].at[src, mc] if kind == 'fwd' else outs[i].at[src, 1 - mc]
                    cps.append(pltpu.make_async_remote_copy(
                        src_ref=half, dst_ref=half, send_sem=d2d_send.at[j], recv_sem=d2d_recv.at[j],
                        device_id=sibling, device_id_type=MESH))
        return cps

    def start(ins, outs, sems):
        for cp in copies('own', ins, outs, sems) + copies('ici', ins, outs, sems):
            cp.start()

    def finish(ins, outs, sems):
        fwd = copies('fwd', ins, outs, sems)
        for arrived, onward in zip(copies('landed', ins, outs, sems), fwd):
            arrived.wait_recv()
            onward.start()
        own = copies('own', ins, outs, sems)
        for cp in copies('fwd_in', ins, outs, sems) + own:
            cp.wait_recv()
        for cp in own + copies('ici', ins, outs, sems) + fwd:
            cp.wait_send()

    out_shapes = [jax.ShapeDtypeStruct((N_CHIPS,) + t.shape, t.dtype) for t in halves]
    return _Plan(halves, out_shapes, [count] * 4 + [n] * 2, start, finish)


def _sibling_exchange(blocks, name):
    n = len(blocks)

    def body(*refs):
        ins, outs = refs[:n], refs[n:2 * n]
        send_sems, recv_sems = refs[2 * n:]
        mx, my, mc = lax.axis_index("x"), lax.axis_index("y"), lax.axis_index("c")
        cps = []
        for i in range(n):
            cp = pltpu.make_async_remote_copy(src_ref=ins[i], dst_ref=outs[i], send_sem=send_sems.at[i],
                                              recv_sem=recv_sems.at[i], device_id=(mx, my, 1 - mc),
                                              device_id_type=MESH)
            cp.start()
            cps.append(cp)
        for cp in cps:
            cp.wait()

    return pl.pallas_call(
        body, name=name, in_specs=[ANY] * n, out_specs=[ANY] * n,
        out_shape=[jax.ShapeDtypeStruct(t.shape, t.dtype) for t in blocks],
        scratch_shapes=[pltpu.SemaphoreType.DMA((n,)), pltpu.SemaphoreType.DMA((n,))],
    )(*blocks)


def _add_halves(own, recv):
    nb, r, c = own.shape
    tr = _row_tile(r, c * 4)

    def body(g_ref, r_ref, o_ref):
        o_ref[...] = (g_ref[...] + r_ref[...].astype(F32)).astype(BF16)

    spec = pl.BlockSpec((1, tr, c), lambda k, i: (k, i, 0))
    return pl.pallas_call(
        body, name="rs_add_halves", grid=(nb, r // tr), in_specs=[spec, spec], out_specs=spec,
        out_shape=jax.ShapeDtypeStruct((nb, r, c), BF16),
        compiler_params=pltpu.CompilerParams(dimension_semantics=("parallel", "parallel")),
    )(own, recv)


def _chip_exchange_plan(jobs):
    n = len(jobs)
    count = (N_CHIPS - 1) * n

    def copies(ins, outs, sems):
        send_sems, recv_sems = sems
        mx, my, mc = lax.axis_index("x"), lax.axis_index("y"), lax.axis_index("c")
        cps = []
        for k in range(1, N_CHIPS):
            peer = (_flip_if(mx, k & 2), _flip_if(my, k & 1), mc)
            dst_chip = 2 * peer[0] + peer[1]
            for i, (_, row0, rows) in enumerate(jobs):
                j = (k - 1) * n + i
                cps.append(pltpu.make_async_remote_copy(
                    src_ref=ins[i].at[dst_chip, pl.ds(row0, rows)], dst_ref=outs[i].at[k - 1],
                    send_sem=send_sems.at[j], recv_sem=recv_sems.at[j], device_id=peer, device_id_type=MESH))
        return cps

    def start(ins, outs, sems):
        for cp in copies(ins, outs, sems):
            cp.start()

    def finish(ins, outs, sems):
        for cp in copies(ins, outs, sems):
            cp.wait()

    out_shapes = [jax.ShapeDtypeStruct((N_CHIPS - 1, rows, p.shape[2]), p.dtype) for p, _, rows in jobs]
    return _Plan([p for p, _, _ in jobs], out_shapes, [count, count], start, finish)


def _sum_chips(parts, recv, chip, row0):
    _, rows, c = recv.shape
    tr = _row_tile(rows, c * 4)
    assert row0 % tr == 0

    def body(chip_ref, p_ref, r_ref, o_ref):
        acc = p_ref[0].astype(F32)
        for k in range(N_CHIPS - 1):
            acc = acc + r_ref[k].astype(F32)
        o_ref[...] = acc

    return pl.pallas_call(
        body, name="rs_sum_chips",
        grid_spec=pltpu.PrefetchScalarGridSpec(
            num_scalar_prefetch=1, grid=(rows // tr,),
            in_specs=[pl.BlockSpec((1, tr, c), lambda i, chip_ref: (chip_ref[0], i + row0 // tr, 0)),
                      pl.BlockSpec((N_CHIPS - 1, tr, c), lambda i, chip_ref: (0, i, 0))],
            out_specs=pl.BlockSpec((tr, c), lambda i, chip_ref: (i, 0))),
        out_shape=jax.ShapeDtypeStruct((rows, c), F32),
        compiler_params=pltpu.CompilerParams(dimension_semantics=("parallel",)),
    )(chip, parts, recv)


def _sum_leading(t, name):
    nb, r, c = t.shape
    tr = _row_tile(r, c * 4 * nb)

    def body(t_ref, o_ref):
        acc = t_ref[0]
        for k in range(1, nb):
            acc = acc + t_ref[k]
        o_ref[...] = acc

    return pl.pallas_call(
        body, name=name, grid=(r // tr,),
        in_specs=[pl.BlockSpec((nb, tr, c), lambda i: (0, i, 0))],
        out_specs=pl.BlockSpec((tr, c), lambda i: (i, 0)),
        out_shape=jax.ShapeDtypeStruct((r, c), F32),
        compiler_params=pltpu.CompilerParams(dimension_semantics=("parallel",)),
    )(t)


def _rs_parts(grads):
    mc = lax.axis_index("c")
    split = [g.reshape(g.shape[0], 2, g.shape[1] // 2, g.shape[2]) for g in grads]
    own = [lax.dynamic_index_in_dim(g, mc, axis=1, keepdims=False) for g in split]
    away = [lax.dynamic_index_in_dim(g, 1 - mc, axis=1, keepdims=False).astype(BF16) for g in split]
    return [_add_halves(o, r) for o, r in zip(own, _sibling_exchange(away, "rs_sibling_exchange"))]


def _rs_result(items):
    mc = lax.axis_index("c")
    chip = (2 * lax.axis_index("x") + lax.axis_index("y")).astype(jnp.int32).reshape(1)
    mine = []
    for parts, pieces in items:
        done = [_sum_chips(parts, recv, chip, row0) for row0, recv in pieces]
        mine.append(done[0] if len(done) == 1 else jnp.concatenate(done, axis=0))
    theirs = _sibling_exchange(mine, "rs_sibling_swap")
    return [jnp.concatenate([jnp.where(mc == 0, a, b), jnp.where(mc == 0, b, a)], axis=0)
            for a, b in zip(mine, theirs)]


def _adamw(w, g, m, v, plan=None):
    shape = w.shape
    cols = shape[-1]
    if len(shape) == 3:
        lead, rows = shape[0], shape[1]
    else:
        lead, rows = 1, (int(np.prod(shape[:-1])) if len(shape) > 1 else 1)
    w2, g2, m2, v2 = [t.reshape(lead, rows, cols) for t in (w, g, m, v)]
    tr = _row_tile(rows, cols * 4, ADAM_TILE_BYTES)
    t0, tc = 1, cols
    if tr == rows and rows * cols * 4 > ADAM_TILE_BYTES and cols % LANES == 0:
        tc = max(t for t in _tile_cands(cols, cols) if t == LANES or rows * t * 4 <= ADAM_TILE_BYTES)
    elif rows < 8 and cols % LANES == 0:
        fits = [(a * t, t, a) for a in range(1, lead + 1) if lead % a == 0 for t in _tile_cands(cols, cols)
                if a * rows * t * 4 <= ADAM_TILE_BYTES]
        _, tc, t0 = max(fits) if fits else (0, LANES, 1)
    per = (rows // tr) * (cols // tc)
    steps = (lead // t0) * per
    p_in, p_out, p_shapes, p_scratch, p_args = _plan_specs(plan)

    def body(w_ref, g_ref, m_ref, v_ref, *rest):
        d_ref, mo_ref, vo_ref = rest[len(p_in):len(p_in) + 3]
        i = pl.program_id(0) * per + pl.program_id(1)

        def compute():
            gv = g_ref[...]
            mn = ADAM_B1 * m_ref[...] + (1.0 - ADAM_B1) * gv
            vn = ADAM_B2 * v_ref[...] + (1.0 - ADAM_B2) * (gv * gv)
            m_hat = mn / (1.0 - ADAM_B1 ** ADAM_STEP)
            v_hat = vn / (1.0 - ADAM_B2 ** ADAM_STEP)
            d_ref[...] = -ADAM_LR * (m_hat / (jnp.sqrt(v_hat) + ADAM_EPS) + ADAM_WD * w_ref[...])
            mo_ref[...] = mn
            vo_ref[...] = vn

        _ride(plan, rest[:len(p_in)] + rest[len(p_in) + 3:], i == 0, i == steps - 1, compute)

    col_blocks = cols // tc
    spec = pl.BlockSpec((t0, tr, tc), lambda a, i: (a, i // col_blocks, i % col_blocks))
    sem = "arbitrary" if plan is not None else "parallel"
    outs = pl.pallas_call(
        body, name="adamw", grid=(lead // t0, per), in_specs=[spec] * 4 + p_in, out_specs=[spec] * 3 + p_out,
        out_shape=[jax.ShapeDtypeStruct((lead, rows, cols), F32)] * 3 + p_shapes, scratch_shapes=p_scratch,
        compiler_params=pltpu.CompilerParams(dimension_semantics=(sem, sem), vmem_limit_bytes=ADAM_VMEM_LIMIT),
    )(w2, g2, m2, v2, *p_args)
    res = [t.reshape(shape) for t in outs[:3]]
    return res if plan is None else (res, list(outs[3:]))


WEIGHTS = ['w_ada', 'b_ada', 'norm1_g', 'norm2_g', 'w_in', 'q_norm_g', 'k_norm_g', 'mla_q_norm_g', 'w_uq',
           'mla_kv_norm_g', 'w_ukv', 'conv_w', 'conv_b', 'dt_bias', 'a_log', 'd_skip', 'ssd_norm_g', 'w_out',
           'w_gate_up', 'w_down', 'final_norm_g']
COL_SHARDED = ('w_in', 'w_uq', 'w_ukv', 'w_gate_up')
ROW_SHARDED = ('w_out', 'w_down')
CHIP_BLOCKED = ('w_gate_up',)
TRANSPOSED = ('w_in',)
SMALL_LAYER = ('norm1_g', 'norm2_g', 'q_norm_g', 'k_norm_g', 'mla_q_norm_g', 'mla_kv_norm_g', 'conv_w', 'conv_b',
               'dt_bias', 'a_log', 'd_skip', 'ssd_norm_g')


def _pack(parts):
    flat = jnp.concatenate([p.reshape(-1) for p in parts])
    n = flat.shape[0]
    rows = -(-n // (8 * LANES)) * 8
    return jnp.pad(flat, (0, rows * LANES - n)).reshape(rows, LANES)


def _unpack(flat, shapes):
    out, pos = [], 0
    for shp in shapes:
        size = int(np.prod(shp))
        out.append(flat[pos:pos + size].reshape(shp))
        pos += size
    return out


def _cols_full(gathered):
    k, r, c = gathered.shape
    return jnp.transpose(gathered, (1, 0, 2)).reshape(r, k * c)


def _cols_split(full):
    r, c4 = full.shape
    return jnp.transpose(full.reshape(r, N_CHIPS, c4 // N_CHIPS), (1, 0, 2))


def kernel(x, c, w_ada, b_ada, norm1_g, norm2_g, w_in, q_norm_g, k_norm_g, mla_q_norm_g, w_uq, mla_kv_norm_g, w_ukv, conv_w, conv_b, dt_bias, a_log, d_skip, ssd_norm_g, w_out, w_gate_up, w_down, final_norm_g, loss_target, m_w_ada, m_b_ada, m_norm1_g, m_norm2_g, m_w_in, m_q_norm_g, m_k_norm_g, m_mla_q_norm_g, m_w_uq, m_mla_kv_norm_g, m_w_ukv, m_conv_w, m_conv_b, m_dt_bias, m_a_log, m_d_skip, m_ssd_norm_g, m_w_out, m_w_gate_up, m_w_down, m_final_norm_g, v_w_ada, v_b_ada, v_norm1_g, v_norm2_g, v_w_in, v_q_norm_g, v_k_norm_g, v_mla_q_norm_g, v_w_uq, v_mla_kv_norm_g, v_w_ukv, v_conv_w, v_conv_b, v_dt_bias, v_a_log, v_d_skip, v_ssd_norm_g, v_w_out, v_w_gate_up, v_w_down, v_final_norm_g):
    args = dict(locals())
    weights = {n: args[n] for n in WEIGHTS}
    depth = w_in.shape[0]
    bl, s, d = x.shape
    mx, my, mc = lax.axis_index("x"), lax.axis_index("y"), lax.axis_index("c")
    chip = 2 * mx + my
    dev = 2 * chip + mc
    ada_cols = w_ada.shape[-1]
    conv_cols = conv_w.shape[-1]

    first_shapes = [c.shape, conv_w.shape]
    first = _all_gather_devices(_pack([c, conv_w]))
    first = [_unpack(first[i].reshape(-1), first_shapes) for i in range(N_DEV)]
    c_act = jax.nn.silu(jnp.concatenate([f[0] for f in first], axis=0))
    conv_w_full = jnp.concatenate([first[2 * k][1] for k in range(N_CHIPS)], axis=-1)

    b_cols = lax.dynamic_slice_in_dim(b_ada, chip * ada_cols, ada_cols, axis=1)
    c_act_b = c_act.astype(BF16)
    mod_cols = jnp.stack([_matmul(c_act_b, w_ada[l], name="ada_fwd") + b_cols[l][None, :]
                          for l in range(depth)])
    mod_all = _all_gather_devices(mod_cols.reshape(depth * N_DEV * bl, ada_cols))
    mod_all = mod_all.reshape(N_DEV, depth, N_DEV, bl, ada_cols)
    mod_mine = lax.dynamic_index_in_dim(mod_all, dev, axis=2, keepdims=False)
    mod = jnp.concatenate([mod_mine[2 * k] for k in range(N_CHIPS)], axis=-1)

    big = COL_SHARDED + ROW_SHARDED
    flip = lambda t: jnp.transpose(t, (2, 0, 1))
    unflip = lambda t: jnp.transpose(t, (1, 2, 0))
    shards = {(n, l): weights[n][l].astype(BF16) for n in big for l in range(depth)}
    for l in range(depth):
        shards[('w_in', l)] = jnp.pad(flip(w_in)[:, l, :].astype(BF16), ((0, IN_SHARD_PAD - IN_SHARD), (0, 0)))
    gathered = _Gathered(shards)
    reducer = _Reducer()
    small_w = []
    for l in range(depth):
        w = {n: weights[n][l] for n in SMALL_LAYER if n != 'conv_w'}
        w['conv_w'] = conv_w_full[l]
        small_w.append(w)
    loss_local, gx, gmod, glayers, gfinal = _forward_backward(x, mod, small_w, gathered, reducer, final_norm_g,
                                                              loss_target)

    small_parts = [jnp.stack([glayers[l][n] for l in range(depth)]) for n in SMALL_LAYER]
    small_parts += [gfinal, loss_local.reshape(1), gmod]
    small_shapes = [p.shape for p in small_parts]
    last = _all_gather_devices(_pack(small_parts))
    summed = _unpack(_sum_leading(last, "sum_devices").reshape(-1), small_shapes)
    small = dict(zip(SMALL_LAYER, summed[:len(SMALL_LAYER)]))
    g_final, loss, gmod_sum = summed[len(SMALL_LAYER):]
    small['conv_w'] = lax.dynamic_slice_in_dim(small['conv_w'], chip * conv_cols, conv_cols, axis=2)
    gmod_all = jnp.stack([_unpack(last[i].reshape(-1), small_shapes)[-1] for i in range(N_DEV)], axis=1)
    gmod_all = gmod_all.reshape(depth, N_DEV * bl, gmod.shape[-1])
    gmod_cols = lax.dynamic_slice_in_dim(gmod_all, chip * ada_cols, ada_cols, axis=2)
    g_w_ada = jnp.stack([_matmul(c_act_b, gmod_cols[l].astype(BF16), ta=True, name="ada_dw") for l in range(depth)])
    g_b_ada = gmod_sum[:, 0]
    for i in range(1, bl):
        g_b_ada = g_b_ada + gmod_sum[:, i]

    grad = {'w_ada': g_w_ada, 'b_ada': g_b_ada, 'final_norm_g': g_final}
    for n in SMALL_LAYER:
        grad[n] = small[n]

    delta, new_m, new_v = {}, {}, {}
    left = [key for key in reducer.parts if key not in reducer.result and not reducer.recv[key]]
    grad['w_down'] = jnp.stack([reducer.result[('w_down', l)] for l in range(depth)])
    delta['w_down'], new_m['w_down'], new_v['w_down'] = reducer.carry(
        left, lambda p: _adamw(w_down, grad['w_down'], m_w_down, v_w_down, plan=p))
    reducer.flush()
    for n in big:
        if n in TRANSPOSED:
            grad[n] = jnp.stack([reducer.result[(n, l)][:weights[n].shape[2]] for l in range(depth)], axis=1)
        elif n != 'w_down':
            grad[n] = jnp.stack([reducer.result[(n, l)] for l in range(depth)])
    for n in WEIGHTS:
        if n in TRANSPOSED:
            outs = _adamw(flip(weights[n]), grad[n], flip(args["m_" + n]), flip(args["v_" + n]))
            grad[n], delta[n], new_m[n], new_v[n] = [unflip(t) for t in [grad[n]] + outs]
        elif n != 'w_down':
            delta[n], new_m[n], new_v[n] = _adamw(weights[n], grad[n], args["m_" + n], args["v_" + n])
    return (loss.reshape(()), gx, *[grad[n] for n in WEIGHTS], *[delta[n] for n in WEIGHTS],
            *[new_m[n] for n in WEIGHTS], *[new_v[n] for n in WEIGHTS])
```

```python
import functools

import numpy as np
import jax
import jax.numpy as jnp
from jax import lax
from jax.experimental import pallas as pl
from jax.experimental.pallas import tpu as pltpu

F32 = jnp.float32
BF16 = jnp.bfloat16
HIGHEST = lax.Precision.HIGHEST
MESH = pl.DeviceIdType.MESH

GRID_W = 64
ROPE_THETA = 10000.0
EPS = 1e-6

GQA_HEADS, GQA_KV_HEADS, GQA_HEAD_DIM = 6, 2, 128
GQA_WIDTH = GQA_HEADS * GQA_HEAD_DIM
GQA_KV_WIDTH = GQA_KV_HEADS * GQA_HEAD_DIM
MLA_HEADS, MLA_Q_LORA, MLA_KV_LORA = 4, 512, 256
MLA_NOPE_DIM, MLA_ROPE_DIM, MLA_V_DIM = 128, 64, 128
SSD_HEADS, SSD_HEAD_DIM, SSD_GROUPS, SSD_STATE, SSD_CONV, SSD_CHUNK = 12, 64, 2, 128, 5, 128
SSD_INNER = SSD_HEADS * SSD_HEAD_DIM
SSD_CONV_DIM = SSD_INNER + 2 * SSD_GROUPS * SSD_STATE
SSD_GROUP_HEADS = SSD_HEADS // SSD_GROUPS
SSD_GROUP_WIDTH = SSD_GROUP_HEADS * SSD_HEAD_DIM
IN_SPLITS = (GQA_WIDTH, GQA_KV_WIDTH, GQA_KV_WIDTH, MLA_Q_LORA, MLA_KV_LORA, MLA_ROPE_DIM, SSD_INNER, SSD_CONV_DIM,
             2 * SSD_HEADS)
IN_COLS = sum(IN_SPLITS)
LANES = 128
N_CHIPS = 4
IN_SHARD = IN_COLS // N_CHIPS
IN_SHARD_PAD = -(-IN_SHARD // LANES) * LANES


def _in_cols(proj, lo, hi):
    parts = []
    for chip in range(lo // IN_SHARD, (hi - 1) // IN_SHARD + 1):
        a, z = max(lo, chip * IN_SHARD), min(hi, (chip + 1) * IN_SHARD)
        base = chip * IN_SHARD_PAD - chip * IN_SHARD
        parts.append(proj[:, base + a:base + z])
    return parts[0] if len(parts) == 1 else jnp.concatenate(parts, axis=-1)

ADAM_LR, ADAM_B1, ADAM_B2, ADAM_EPS, ADAM_WD, ADAM_STEP = 0.001, 0.9, 0.999, 1e-08, 0.01, 10

N_DEV = 8
TILE_BYTES = 2 * 1024 * 1024


def _pick(n, cands):
    for t in cands:
        if n % t == 0:
            return t
    return n


ADAM_TILE_BYTES = 2 * 1024 * 1024
ADAM_VMEM_LIMIT = 48 * 1024 * 1024
SWIGLU_TILE_BYTES = 4 * 1024 * 1024


def _row_tile(rows, row_bytes, limit=TILE_BYTES):
    for t in (2048, 1024, 512, 256, 128, 64, 32, 16, 8):
        if rows % t == 0 and t * row_bytes <= limit:
            return t
    return rows


MM_VMEM_BUDGET = 36 * 1024 * 1024
MM_VMEM_LIMIT = 56 * 1024 * 1024
MM_MAX_TILE = 2048
MM_MAX_K_TILE = 4096
MXU_DIM = 256
HBM_BYTES_PER_US = 3.0e6
MXU_FLOPS_PER_US = 9.0e8
STEP_US = 0.35


def _tile_cands(d, cap):
    if d % LANES:
        return [d]
    return [t for t in range(LANES, min(d, cap) + 1, LANES) if d % t == 0] or [d]


def _mm_tiles(m, n, kdim, n_unit=None, k_unit=None):
    up = lambda t: -(-t // MXU_DIM) * MXU_DIM
    best = None
    for tm in _tile_cands(m, MM_MAX_TILE):
        for tn in _tile_cands(n_unit or n, MM_MAX_TILE):
            for tk in _tile_cands(k_unit or kdim, MM_MAX_K_TILE):
                if 2 * (tm * tk * 2 + tk * tn * 2 + tm * tn * 4) > MM_VMEM_BUDGET:
                    continue
                ni, nj, nk = m // tm, n // tn, kdim // tk
                a_reads = 1 if nk == 1 else nj
                b_reads = 1 if (nk == 1 and nj == 1) else ni
                hbm = (m * kdim * 2 * a_reads + kdim * n * 2 * b_reads + m * n * 4) / HBM_BYTES_PER_US
                mxu = ni * nj * nk * 2.0 * max(tm, 8) * up(tn) * up(tk) / MXU_FLOPS_PER_US
                cost = max(hbm, mxu) + 0.25 * min(hbm, mxu) + ni * nj * nk * STEP_US
                if best is None or cost < best[0]:
                    best = (cost, tm, tn, tk)
    return best[1:]


def _ride(plan, refs, first, last, compute):
    if plan is None:
        compute()
        return
    ins, outs, sems = plan.split(refs)

    @pl.when(first)
    def _():
        plan.start(ins, outs, sems)

    compute()

    @pl.when(last)
    def _():
        plan.finish(ins, outs, sems)


def _plan_specs(plan):
    if plan is None:
        return [], [], [], [], []
    in_specs, out_specs, scratch = plan.specs()
    return in_specs, out_specs, plan.out_shapes, scratch, plan.inputs


def _matmul(a, b, ta=False, tb=False, name="mm", plan=None, chips=None):
    assert a.dtype == BF16 and b.dtype in (BF16, F32), (a.dtype, b.dtype)
    if ta:
        kdim, m = a.shape
    else:
        m, kdim = a.shape
    n_unit = k_unit = None
    if chips == 'b':
        nb, rows, unit = b.shape
        if tb:
            n, k2, k_unit = rows, nb * unit, unit
        else:
            k2, n, n_unit = rows, nb * unit, unit
    else:
        if tb:
            n, k2 = b.shape
        else:
            k2, n = b.shape
        if chips == 'out':
            n_unit = n // N_CHIPS
    assert kdim == k2, (a.shape, b.shape, ta, tb)
    tm, tn, tk = _mm_tiles(m, n, kdim, n_unit, k_unit)
    ni, nj, nk = m // tm, n // tn, kdim // tk
    dn = (((0 if ta else 1,), (1 if tb else 0,)), ((), ()))
    p_in, p_out, p_shapes, p_scratch, p_args = _plan_specs(plan)

    def body(a_ref, b_ref, *rest):
        o_ref = rest[len(p_in)]
        i, j, k = pl.program_id(0), pl.program_id(1), pl.program_id(2)

        def compute():
            bv = (b_ref[0] if chips == 'b' else b_ref[...]).astype(BF16)
            part = lax.dot_general(a_ref[...], bv, dn, preferred_element_type=F32)
            if chips == 'out':
                part = part[None]
            if nk == 1:
                o_ref[...] = part
            else:
                @pl.when(k == 0)
                def _():
                    o_ref[...] = part

                @pl.when(k > 0)
                def _():
                    o_ref[...] += part

        _ride(plan, rest[:len(p_in)] + rest[len(p_in) + 1:], (i == 0) & (j == 0) & (k == 0),
              (i == ni - 1) & (j == nj - 1) & (k == nk - 1), compute)

    a_spec = pl.BlockSpec((tk, tm), lambda i, j, k: (k, i)) if ta else pl.BlockSpec((tm, tk), lambda i, j, k: (i, k))
    if chips == 'b' and tb:
        per = k_unit // tk
        b_spec = pl.BlockSpec((1, tn, tk), lambda i, j, k: (k // per, j, k % per))
    elif chips == 'b':
        per = n_unit // tn
        b_spec = pl.BlockSpec((1, tk, tn), lambda i, j, k: (j // per, k, j % per))
    else:
        b_spec = pl.BlockSpec((tn, tk), lambda i, j, k: (j, k)) if tb else pl.BlockSpec((tk, tn), lambda i, j, k: (k, j))
    if chips == 'out':
        per = n_unit // tn
        o_spec = pl.BlockSpec((1, tm, tn), lambda i, j, k: (j // per, i, j % per))
        o_shape = jax.ShapeDtypeStruct((N_CHIPS, m, n_unit), F32)
    else:
        o_spec = pl.BlockSpec((tm, tn), lambda i, j, k: (i, j))
        o_shape = jax.ShapeDtypeStruct((m, n), F32)
    outs = pl.pallas_call(
        body, name=name, grid=(ni, nj, nk),
        in_specs=[a_spec, b_spec] + p_in, out_specs=[o_spec] + p_out,
        out_shape=[o_shape] + p_shapes, scratch_shapes=p_scratch,
        compiler_params=pltpu.CompilerParams(
            dimension_semantics=("arbitrary" if plan is not None else "parallel", "arbitrary", "arbitrary"),
            vmem_limit_bytes=MM_VMEM_LIMIT),
    )(a, b, *p_args)
    return outs[0] if plan is None else (outs[0], list(outs[1:]))


@jax.custom_vjp
def linear(x, w):
    return _matmul(x.astype(BF16), w.astype(BF16), name="linear_fwd")


def _linear_fwd(x, w):
    xb, wb = x.astype(BF16), w.astype(BF16)
    return _matmul(xb, wb, name="linear_fwd"), (xb, wb)


def _linear_bwd(res, dy):
    xb, wb = res
    dyb = dy.astype(BF16)
    return _matmul(dyb, wb, tb=True, name="linear_dx"), _matmul(xb, dyb, ta=True, name="linear_dw")


linear.defvjp(_linear_fwd, _linear_bwd)


def _rms_fwd_call(x, g, groups):
    rows, cols = x.shape
    d = cols // groups
    tr = _row_tile(rows, cols * 4)

    def body(x_ref, g_ref, y_ref):
        for gi in range(groups):
            sl = slice(gi * d, (gi + 1) * d)
            xs = x_ref[:, sl]
            r = lax.rsqrt(jnp.mean(xs * xs, axis=-1, keepdims=True) + EPS)
            y_ref[:, sl] = xs * r * g_ref[:, sl]

    return pl.pallas_call(
        body, name="rms_fwd", grid=(rows // tr,),
        in_specs=[pl.BlockSpec((tr, cols), lambda i: (i, 0)), pl.BlockSpec((1, cols), lambda i: (0, 0))],
        out_specs=pl.BlockSpec((tr, cols), lambda i: (i, 0)),
        out_shape=jax.ShapeDtypeStruct((rows, cols), F32),
        compiler_params=pltpu.CompilerParams(dimension_semantics=("parallel",)),
    )(x, g)


def _rms_bwd_call(x, g, dy, groups):
    rows, cols = x.shape
    d = cols // groups
    tr = _row_tile(rows, cols * 4)

    def body(x_ref, g_ref, dy_ref, dx_ref, dg_ref):
        @pl.when(pl.program_id(0) == 0)
        def _():
            dg_ref[...] = jnp.zeros_like(dg_ref)

        for gi in range(groups):
            sl = slice(gi * d, (gi + 1) * d)
            xs = x_ref[:, sl]
            dys = dy_ref[:, sl]
            r = lax.rsqrt(jnp.mean(xs * xs, axis=-1, keepdims=True) + EPS)
            xhat = xs * r
            dg_ref[:, sl] += jnp.sum(dys * xhat, axis=0, keepdims=True)
            dxhat = dys * g_ref[:, sl]
            dx_ref[:, sl] = r * (dxhat - xhat * jnp.mean(dxhat * xhat, axis=-1, keepdims=True))

    return pl.pallas_call(
        body, name="rms_bwd", grid=(rows // tr,),
        in_specs=[pl.BlockSpec((tr, cols), lambda i: (i, 0)), pl.BlockSpec((1, cols), lambda i: (0, 0)),
                  pl.BlockSpec((tr, cols), lambda i: (i, 0))],
        out_specs=[pl.BlockSpec((tr, cols), lambda i: (i, 0)), pl.BlockSpec((1, cols), lambda i: (0, 0))],
        out_shape=[jax.ShapeDtypeStruct((rows, cols), F32), jax.ShapeDtypeStruct((1, cols), F32)],
        compiler_params=pltpu.CompilerParams(dimension_semantics=("arbitrary",)),
    )(x, g, dy)


@functools.partial(jax.custom_vjp, nondiff_argnums=(2,))
def rms_norm(x, g, groups):
    return _rms_fwd_call(x, g, groups)


def _rms_norm_fwd(x, g, groups):
    return _rms_fwd_call(x, g, groups), (x, g)


def _rms_norm_bwd(groups, res, dy):
    x, g = res
    dx, dg = _rms_bwd_call(x, g, dy, groups)
    return dx, dg


rms_norm.defvjp(_rms_norm_fwd, _rms_norm_bwd)


NT_DIMS = (((1,), (1,)), ((), ()))
TN_DIMS = (((0,), (0,)), ((), ()))


LOG2E = 1.4426950408889634
ATTN_VMEM_LIMIT = 60 * 1024 * 1024

def _exp_rows(q, k, scale):
    s2 = lax.dot_general(q, k, NT_DIMS, preferred_element_type=F32) * (scale * LOG2E)
    e = jnp.exp2(s2 - jnp.max(s2, axis=-1, keepdims=True))
    return e, 1.0 / jnp.sum(e, axis=-1, keepdims=True)


def _attn_fwd_call(q, k, v, scale):
    b, h, s, dk = q.shape
    hkv, dv = k.shape[1], v.shape[3]
    rep = h // hkv
    tq = _pick(s, (1024, 512, 256, 128))

    def body(q_ref, k_ref, v_ref, o_ref):
        e, inv = _exp_rows(q_ref[0, 0], k_ref[0, 0], scale)
        o_ref[0, 0] = jnp.dot(e.astype(BF16), v_ref[0, 0], preferred_element_type=F32) * inv

    return pl.pallas_call(
        body, name="attn_fwd", grid=(b, h, s // tq),
        in_specs=[pl.BlockSpec((1, 1, tq, dk), lambda bi, hi, qi: (bi, hi, qi, 0)),
                  pl.BlockSpec((1, 1, s, dk), lambda bi, hi, qi: (bi, hi // rep, 0, 0)),
                  pl.BlockSpec((1, 1, s, dv), lambda bi, hi, qi: (bi, hi // rep, 0, 0))],
        out_specs=pl.BlockSpec((1, 1, tq, dv), lambda bi, hi, qi: (bi, hi, qi, 0)),
        out_shape=jax.ShapeDtypeStruct((b, h, s, dv), F32),
        compiler_params=pltpu.CompilerParams(dimension_semantics=("parallel", "parallel", "parallel"),
                                             vmem_limit_bytes=ATTN_VMEM_LIMIT),
    )(q, k, v)


def _attn_bwd_call(q, k, v, do, scale):
    b, h, s, dk = q.shape
    hkv, dv = k.shape[1], v.shape[3]
    rep = h // hkv
    tq = _pick(s, (1024, 512, 256, 128))

    def body(q_ref, k_ref, v_ref, do_ref, dq_ref, dk_ref, dv_ref):
        @pl.when((pl.program_id(2) == 0) & (pl.program_id(3) == 0))
        def _():
            dk_ref[...] = jnp.zeros_like(dk_ref)
            dv_ref[...] = jnp.zeros_like(dv_ref)

        qb = q_ref[0, 0]
        kb = k_ref[0, 0]
        vb = v_ref[0, 0]
        dob = do_ref[0, 0]
        e, inv = _exp_rows(qb, kb, scale)
        dp = lax.dot_general(dob, vb, NT_DIMS, preferred_element_type=F32)
        delta = jnp.sum(e * dp, axis=-1, keepdims=True) * inv
        ds = (e * ((dp - delta) * (inv * scale))).astype(BF16)
        dq_ref[0, 0] = jnp.dot(ds, kb, preferred_element_type=F32)
        dk_ref[0, 0] += lax.dot_general(ds, qb, TN_DIMS, preferred_element_type=F32)
        dv_ref[0, 0] += lax.dot_general(e.astype(BF16), (dob.astype(F32) * inv).astype(BF16), TN_DIMS,
                                        preferred_element_type=F32)

    return pl.pallas_call(
        body, name="attn_bwd", grid=(b, hkv, rep, s // tq),
        in_specs=[pl.BlockSpec((1, 1, tq, dk), lambda bi, gi, ri, qi: (bi, gi * rep + ri, qi, 0)),
                  pl.BlockSpec((1, 1, s, dk), lambda bi, gi, ri, qi: (bi, gi, 0, 0)),
                  pl.BlockSpec((1, 1, s, dv), lambda bi, gi, ri, qi: (bi, gi, 0, 0)),
                  pl.BlockSpec((1, 1, tq, dv), lambda bi, gi, ri, qi: (bi, gi * rep + ri, qi, 0))],
        out_specs=[pl.BlockSpec((1, 1, tq, dk), lambda bi, gi, ri, qi: (bi, gi * rep + ri, qi, 0)),
                   pl.BlockSpec((1, 1, s, dk), lambda bi, gi, ri, qi: (bi, gi, 0, 0)),
                   pl.BlockSpec((1, 1, s, dv), lambda bi, gi, ri, qi: (bi, gi, 0, 0))],
        out_shape=[jax.ShapeDtypeStruct(q.shape, F32), jax.ShapeDtypeStruct(k.shape, F32),
                   jax.ShapeDtypeStruct(v.shape, F32)],
        compiler_params=pltpu.CompilerParams(
            dimension_semantics=("parallel", "parallel", "arbitrary", "arbitrary"), vmem_limit_bytes=ATTN_VMEM_LIMIT),
    )(q, k, v, do)


@functools.partial(jax.custom_vjp, nondiff_argnums=(3,))
def attention(q, k, v, scale):
    return _attn_fwd_call(q.astype(BF16), k.astype(BF16), v.astype(BF16), scale)


def _attention_fwd(q, k, v, scale):
    qb, kb, vb = q.astype(BF16), k.astype(BF16), v.astype(BF16)
    return _attn_fwd_call(qb, kb, vb, scale), (qb, kb, vb)


def _attention_bwd(scale, res, do):
    qb, kb, vb = res
    return tuple(_attn_bwd_call(qb, kb, vb, do.astype(BF16), scale))


attention.defvjp(_attention_fwd, _attention_bwd)


CONV_COL_TILE = 256
CONV_PACK_ROWS = 8


def _shifted(x, off, rows):
    if off == 0:
        return x
    s = x.shape[0]
    rolled = pltpu.roll(x, (-off) % s, 0)
    valid = (rows + off >= 0) & (rows + off < s)
    return jnp.where(valid, rolled, 0.0)


def _conv_pre(x, wb_ref, rows):
    z = jnp.zeros_like(x) + wb_ref[SSD_CONV:SSD_CONV + 1, :]
    for j in range(SSD_CONV):
        z = z + wb_ref[j:j + 1, :] * _shifted(x, j - SSD_CONV // 2, rows)
    return z


def _conv_fwd_call(x, wb):
    b, s, c = x.shape
    tc = _pick(c, (CONV_COL_TILE, LANES))

    def body(x_ref, wb_ref, y_ref):
        xv = x_ref[0]
        rows = lax.broadcasted_iota(jnp.int32, xv.shape, 0)
        z = _conv_pre(xv, wb_ref, rows)
        y_ref[0] = z * jax.nn.sigmoid(z)

    return pl.pallas_call(
        body, name="conv_fwd", grid=(b, c // tc),
        in_specs=[pl.BlockSpec((1, s, tc), lambda bi, ci: (bi, 0, ci)),
                  pl.BlockSpec((CONV_PACK_ROWS, tc), lambda bi, ci: (0, ci))],
        out_specs=pl.BlockSpec((1, s, tc), lambda bi, ci: (bi, 0, ci)),
        out_shape=jax.ShapeDtypeStruct(x.shape, F32),
        compiler_params=pltpu.CompilerParams(dimension_semantics=("parallel", "parallel")),
    )(x, wb)


def _conv_bwd_call(x, wb, dy):
    b, s, c = x.shape
    tc = _pick(c, (CONV_COL_TILE, LANES))

    def body(x_ref, wb_ref, dy_ref, dx_ref, dwb_ref):
        xv = x_ref[0]
        rows = lax.broadcasted_iota(jnp.int32, xv.shape, 0)
        z = _conv_pre(xv, wb_ref, rows)
        sg = jax.nn.sigmoid(z)
        dz = dy_ref[0] * (sg * (1.0 + z * (1.0 - sg)))
        dx = jnp.zeros_like(xv)
        for j in range(SSD_CONV):
            off = j - SSD_CONV // 2
            dx = dx + wb_ref[j:j + 1, :] * _shifted(dz, -off, rows)
            dwb_ref[0, j:j + 1, :] = jnp.sum(dz * _shifted(xv, off, rows), axis=0, keepdims=True)
        dx_ref[0] = dx
        dwb_ref[0, SSD_CONV:SSD_CONV + 1, :] = jnp.sum(dz, axis=0, keepdims=True)
        dwb_ref[0, SSD_CONV + 1:, :] = jnp.zeros((CONV_PACK_ROWS - SSD_CONV - 1, dz.shape[1]), F32)

    return pl.pallas_call(
        body, name="conv_bwd", grid=(b, c // tc),
        in_specs=[pl.BlockSpec((1, s, tc), lambda bi, ci: (bi, 0, ci)),
                  pl.BlockSpec((CONV_PACK_ROWS, tc), lambda bi, ci: (0, ci)),
                  pl.BlockSpec((1, s, tc), lambda bi, ci: (bi, 0, ci))],
        out_specs=[pl.BlockSpec((1, s, tc), lambda bi, ci: (bi, 0, ci)),
                   pl.BlockSpec((1, CONV_PACK_ROWS, tc), lambda bi, ci: (bi, 0, ci))],
        out_shape=[jax.ShapeDtypeStruct(x.shape, F32), jax.ShapeDtypeStruct((b, CONV_PACK_ROWS, c), F32)],
        compiler_params=pltpu.CompilerParams(dimension_semantics=("parallel", "parallel")),
    )(x, wb, dy)


@jax.custom_vjp
def conv_silu(x, wb):
    return _conv_fwd_call(x, wb)


def _conv_silu_fwd(x, wb):
    return _conv_fwd_call(x, wb), (x, wb)


def _conv_silu_bwd(res, dy):
    x, wb = res
    dx, dwb = _conv_bwd_call(x, wb, dy)
    return dx, jnp.sum(dwb, axis=0)


conv_silu.defvjp(_conv_silu_fwd, _conv_silu_bwd)


SSD_PAIRS = SSD_GROUP_HEADS // 2
NEG_INF = -1e30


def _ssd_common(x_ref, dtx_ref, dtt_ref, anx_ref, anc_ref, b_ref, c_ref, reverse):
    L = SSD_CHUNK
    xv = x_ref[0]
    dt = dtx_ref[0]
    ri = lax.broadcasted_iota(jnp.int32, (L, L), 0)
    ci = lax.broadcasted_iota(jnp.int32, (L, L), 1)
    causal = (ri <= ci) if reverse else (ri >= ci)
    tri = causal.astype(F32)
    a_cs = jnp.dot(tri, dt * anx_ref[...], precision=HIGHEST, preferred_element_type=F32)
    a_row = dtt_ref[0, 0] * anc_ref[0]
    acs_row = lax.dot_general(a_row, tri, NT_DIMS, precision=HIGHEST, preferred_element_type=F32)
    xd = xv * dt
    bmat = b_ref[0].astype(BF16)
    cmat = c_ref[0].astype(BF16)
    gmat = lax.dot_general(cmat, bmat, NT_DIMS, preferred_element_type=F32)
    return xv, dt, causal, tri, a_cs, acs_row, xd, bmat, cmat, gmat


def _ssd_lambda(a_cs, acs_row, causal, h):
    col = a_cs[:, h * SSD_HEAD_DIM:h * SSD_HEAD_DIM + 1]
    row = acs_row[h:h + 1, :]
    return jnp.exp(jnp.where(causal, col - row, NEG_INF))


def _ssd_fwd_call(x, dtx, dtt, anx, anc, bm, cm, reverse):
    b, s, _ = x.shape
    L, N, GW = SSD_CHUNK, SSD_STATE, SSD_GROUP_WIDTH
    nc = s // L
    end = 0 if reverse else L - 1

    def body(x_ref, dtx_ref, dtt_ref, anx_ref, anc_ref, b_ref, c_ref, y_ref, hs_ref, state):
        @pl.when(pl.program_id(2) == 0)
        def _():
            state[...] = jnp.zeros_like(state)

        xv, dt, causal, tri, a_cs, acs_row, xd, bmat, cmat, gmat = _ssd_common(
            x_ref, dtx_ref, dtt_ref, anx_ref, anc_ref, b_ref, c_ref, reverse)
        hin = state[...]
        hs_ref[0, 0, 0] = hin
        y_off = jnp.dot(cmat, hin.astype(BF16), preferred_element_type=F32) * jnp.exp(a_cs)
        a_end = a_cs[end:end + 1, :]
        s_new = lax.dot_general(bmat, (xd * jnp.exp(a_end - a_cs)).astype(BF16), TN_DIMS, preferred_element_type=F32)
        state[...] = jnp.exp(a_end) * hin + s_new
        lane = lax.broadcasted_iota(jnp.int32, (L, LANES), 1)
        for pr in range(SSD_PAIRS):
            sl = slice(pr * LANES, (pr + 1) * LANES)
            xdp = xd[:, sl].astype(BF16)
            w0 = (gmat * _ssd_lambda(a_cs, acs_row, causal, 2 * pr)).astype(BF16)
            w1 = (gmat * _ssd_lambda(a_cs, acs_row, causal, 2 * pr + 1)).astype(BF16)
            y0 = jnp.dot(w0, xdp, preferred_element_type=F32)
            y1 = jnp.dot(w1, xdp, preferred_element_type=F32)
            y_ref[0, :, sl] = jnp.where(lane < SSD_HEAD_DIM, y0, y1) + y_off[:, sl]

    G = SSD_GROUPS
    chunk = (lambda c: nc - 1 - c) if reverse else (lambda c: c)
    seq = lambda bi, gi, c: (bi, chunk(c), gi)
    return pl.pallas_call(
        body, name="ssd_fwd", grid=(b, G, nc),
        in_specs=[pl.BlockSpec((1, L, GW), seq),
                  pl.BlockSpec((1, L, GW), seq),
                  pl.BlockSpec((1, 1, SSD_GROUP_HEADS, L), lambda bi, gi, c: (bi, gi, 0, chunk(c))),
                  pl.BlockSpec((1, GW), lambda bi, gi, c: (0, gi)),
                  pl.BlockSpec((1, SSD_GROUP_HEADS, 1), lambda bi, gi, c: (gi, 0, 0)),
                  pl.BlockSpec((1, L, N), seq),
                  pl.BlockSpec((1, L, N), seq)],
        out_specs=[pl.BlockSpec((1, L, GW), seq),
                   pl.BlockSpec((1, 1, 1, N, GW), lambda bi, gi, c: (bi, gi, chunk(c), 0, 0))],
        out_shape=[jax.ShapeDtypeStruct(x.shape, F32), jax.ShapeDtypeStruct((b, G, nc, N, GW), F32)],
        scratch_shapes=[pltpu.VMEM((N, GW), F32)],
        compiler_params=pltpu.CompilerParams(dimension_semantics=("parallel", "parallel", "arbitrary")),
    )(x, dtx, dtt, anx, anc, bm, cm)


def _ssd_bwd_call(x, dtx, dtt, anx, anc, bm, cm, hs, dy, reverse):
    b, s, _ = x.shape
    L, N, GW = SSD_CHUNK, SSD_STATE, SSD_GROUP_WIDTH
    nc = s // L
    end = 0 if reverse else L - 1

    def body(x_ref, dtx_ref, dtt_ref, anx_ref, anc_ref, b_ref, c_ref, hs_ref, dy_ref,
             dx_ref, ddt_ref, dan_ref, db_ref, dc_ref, dstate):
        @pl.when(pl.program_id(2) == 0)
        def _():
            dstate[...] = jnp.zeros_like(dstate)

        xv, dt, causal, tri, a_cs, acs_row, xd, bmat, cmat, gmat = _ssd_common(
            x_ref, dtx_ref, dtt_ref, anx_ref, anc_ref, b_ref, c_ref, reverse)
        hin = hs_ref[0, 0, 0]
        hinb = hin.astype(BF16)
        dyv = dy_ref[0]
        ds_out = dstate[...]
        dsb = ds_out.astype(BF16)
        eacs = jnp.exp(a_cs)
        a_end = a_cs[end:end + 1, :]
        e_end = jnp.exp(a_end)
        dec = jnp.exp(a_end - a_cs)
        dye = dyv * eacs
        dyeb = dye.astype(BF16)
        xdec = xd * dec
        ch = jnp.dot(cmat, hinb, preferred_element_type=F32)
        bds = jnp.dot(bmat, dsb, preferred_element_type=F32)
        t_state = xdec * bds
        d_aend = jnp.sum(t_state, axis=0, keepdims=True) + e_end * jnp.sum(ds_out * hin, axis=0, keepdims=True)
        dacs = dye * ch - t_state
        dxd_state = bds * dec
        dstate[...] = e_end * ds_out + lax.dot_general(cmat, dyeb, TN_DIMS, preferred_element_type=F32)

        lane = lax.broadcasted_iota(jnp.int32, (L, LANES), 1)
        dg = jnp.zeros((L, L), F32)
        dxd_parts, dacs_parts = [], []
        for pr in range(SSD_PAIRS):
            sl = slice(pr * LANES, (pr + 1) * LANES)
            xdp = xd[:, sl]
            dyp = dyv[:, sl]
            dxd_p = jnp.zeros((L, LANES), F32)
            dacs_p = jnp.zeros((L, LANES), F32)
            for half in range(2):
                mine = (lane < SSD_HEAD_DIM) if half == 0 else (lane >= SSD_HEAD_DIM)
                lam = _ssd_lambda(a_cs, acs_row, causal, 2 * pr + half)
                w = gmat * lam
                xdh = jnp.where(mine, xdp, 0.0).astype(BF16)
                dyh = jnp.where(mine, dyp, 0.0).astype(BF16)
                dw = lax.dot_general(dyh, xdh, NT_DIMS, preferred_element_type=F32)
                dg = dg + dw * lam
                mm = dw * w
                rs = jnp.sum(mm, axis=1, keepdims=True)
                cs = jnp.sum(mm.T, axis=1, keepdims=True)
                dacs_p = dacs_p + jnp.where(mine, (rs - cs) * (1.0 / SSD_HEAD_DIM), 0.0)
                wtdy = lax.dot_general(w.astype(BF16), dyh, TN_DIMS, preferred_element_type=F32)
                dxd_p = dxd_p + wtdy
            dxd_parts.append(dxd_p)
            dacs_parts.append(dacs_p)
        dxd = jnp.concatenate(dxd_parts, axis=1) + dxd_state
        dacs = dacs + jnp.concatenate(dacs_parts, axis=1)
        last = lax.broadcasted_iota(jnp.int32, dacs.shape, 0) == end
        dacs = dacs + jnp.where(last, d_aend, 0.0)
        da = lax.dot_general(tri, dacs, TN_DIMS, precision=HIGHEST, preferred_element_type=F32)
        dgb = dg.astype(BF16)
        dc_ref[0] = (jnp.dot(dgb, bmat, preferred_element_type=F32)
                     + lax.dot_general(dyeb, hinb, NT_DIMS, preferred_element_type=F32))
        db_ref[0] = (lax.dot_general(dgb, cmat, TN_DIMS, preferred_element_type=F32)
                     + lax.dot_general(xdec.astype(BF16), dsb, NT_DIMS, preferred_element_type=F32))
        dx_ref[0] = dxd * dt
        ddt_ref[0] = da * anx_ref[...] + dxd * xv
        dan_ref[0, 0, 0] = jnp.sum(da * dt, axis=0, keepdims=True)

    G = SSD_GROUPS
    chunk = (lambda c: c) if reverse else (lambda c: nc - 1 - c)
    rev = lambda bi, gi, c: (bi, chunk(c), gi)
    return pl.pallas_call(
        body, name="ssd_bwd", grid=(b, G, nc),
        in_specs=[pl.BlockSpec((1, L, GW), rev),
                  pl.BlockSpec((1, L, GW), rev),
                  pl.BlockSpec((1, 1, SSD_GROUP_HEADS, L), lambda bi, gi, c: (bi, gi, 0, chunk(c))),
                  pl.BlockSpec((1, GW), lambda bi, gi, c: (0, gi)),
                  pl.BlockSpec((1, SSD_GROUP_HEADS, 1), lambda bi, gi, c: (gi, 0, 0)),
                  pl.BlockSpec((1, L, N), rev),
                  pl.BlockSpec((1, L, N), rev),
                  pl.BlockSpec((1, 1, 1, N, GW), lambda bi, gi, c: (bi, gi, chunk(c), 0, 0)),
                  pl.BlockSpec((1, L, GW), rev)],
        out_specs=[pl.BlockSpec((1, L, GW), rev),
                   pl.BlockSpec((1, L, GW), rev),
                   pl.BlockSpec((1, 1, 1, 1, GW), lambda bi, gi, c: (bi, gi, chunk(c), 0, 0)),
                   pl.BlockSpec((1, L, N), rev),
                   pl.BlockSpec((1, L, N), rev)],
        out_shape=[jax.ShapeDtypeStruct(x.shape, F32), jax.ShapeDtypeStruct(x.shape, F32),
                   jax.ShapeDtypeStruct((b, G, nc, 1, GW), F32),
                   jax.ShapeDtypeStruct(bm.shape, F32), jax.ShapeDtypeStruct(cm.shape, F32)],
        scratch_shapes=[pltpu.VMEM((N, GW), F32)],
        compiler_params=pltpu.CompilerParams(dimension_semantics=("parallel", "parallel", "arbitrary")),
    )(x, dtx, dtt, anx, anc, bm, cm, hs, dy)


@functools.partial(jax.custom_vjp, nondiff_argnums=(7,))
def _ssd_scan(x, dtx, dtt, anx, anc, bm, cm, reverse):
    return _ssd_fwd_call(x, dtx, dtt, anx, anc, bm, cm, reverse)[0]


def _ssd_scan_fwd(x, dtx, dtt, anx, anc, bm, cm, reverse):
    y, hs = _ssd_fwd_call(x, dtx, dtt, anx, anc, bm, cm, reverse)
    return y, (x, dtx, dtt, anx, anc, bm, cm, hs)


def _ssd_scan_bwd(reverse, res, dy):
    x, dtx, dtt, anx, anc, bm, cm, hs = res
    dx, ddtx, dan, db, dc = _ssd_bwd_call(x, dtx, dtt, anx, anc, bm, cm, hs, dy, reverse)
    b, g, nc, _, gw = dan.shape
    danx = jnp.sum(dan, axis=(0, 2, 3)).reshape(1, g * gw)
    return dx, ddtx, jnp.zeros_like(dtt), danx, jnp.zeros_like(anc), db, dc


_ssd_scan.defvjp(_ssd_scan_fwd, _ssd_scan_bwd)


def ssd_chunked(xs, dt, a_neg, bm, cm, reverse):
    b, s, _ = xs.shape
    dtx = jnp.repeat(dt, SSD_HEAD_DIM, axis=-1)
    dtt = jnp.transpose(dt, (0, 2, 1)).reshape(b, SSD_GROUPS, SSD_GROUP_HEADS, s)
    anx = jnp.repeat(a_neg, SSD_HEAD_DIM)[None, :]
    anc = a_neg.reshape(SSD_GROUPS, SSD_GROUP_HEADS, 1)
    return _ssd_scan(xs, dtx, dtt, anx, anc, bm, cm, reverse)


def _loss_call(y, t):
    rows, cols = y.shape
    tr = _row_tile(rows, cols * 4)

    def body(y_ref, t_ref, loss_ref, diff_ref):
        @pl.when(pl.program_id(0) == 0)
        def _():
            loss_ref[...] = jnp.zeros_like(loss_ref)

        d = y_ref[...] - t_ref[...]
        diff_ref[...] = d * (1.0 / cols)
        part = jnp.sum(jnp.sum(d * d, axis=1, keepdims=True), axis=0, keepdims=True)
        loss_ref[...] += part * (0.5 / cols)

    return pl.pallas_call(
        body, name="loss_head", grid=(rows // tr,),
        in_specs=[pl.BlockSpec((tr, cols), lambda i: (i, 0)), pl.BlockSpec((tr, cols), lambda i: (i, 0))],
        out_specs=[pl.BlockSpec((1, 1), lambda i: (0, 0)), pl.BlockSpec((tr, cols), lambda i: (i, 0))],
        out_shape=[jax.ShapeDtypeStruct((1, 1), F32), jax.ShapeDtypeStruct((rows, cols), F32)],
        compiler_params=pltpu.CompilerParams(dimension_semantics=("arbitrary",)),
    )(y, t)


@jax.custom_vjp
def loss_head(y, t):
    return _loss_call(y, t)[0][0, 0]


def _loss_head_fwd(y, t):
    loss, diff = _loss_call(y, t)
    return loss[0, 0], diff


def _loss_head_bwd(diff, g):
    return g * diff, jnp.zeros_like(diff)


loss_head.defvjp(_loss_head_fwd, _loss_head_bwd)


def _axial_rope_tables(seq_len, rot_dim):
    rows = seq_len // GRID_W
    row_idx = jnp.repeat(jnp.arange(rows), GRID_W).astype(F32)
    col_idx = jnp.tile(jnp.arange(GRID_W), rows).astype(F32)
    axis_dim = rot_dim // 2
    inv_freq = jnp.power(ROPE_THETA, -jnp.arange(0, axis_dim, 2, dtype=F32) / axis_dim)
    ang_r = row_idx[:, None] * inv_freq[None, :]
    ang_c = col_idx[:, None] * inv_freq[None, :]
    return jnp.cos(ang_r), jnp.sin(ang_r), jnp.cos(ang_c), jnp.sin(ang_c)


def _rotate(x, cos, sin):
    x1, x2 = jnp.split(x, 2, axis=-1)
    cos = cos[:, None, :]
    sin = sin[:, None, :]
    return jnp.concatenate([x1 * cos - x2 * sin, x1 * sin + x2 * cos], axis=-1)


def _apply_axial_rope(x, tables):
    cos_r, sin_r, cos_c, sin_c = tables
    x_row, x_col = jnp.split(x, 2, axis=-1)
    return jnp.concatenate([_rotate(x_row, cos_r, sin_r), _rotate(x_col, cos_c, sin_c)], axis=-1)


def _heads_first(t):
    return jnp.transpose(t, (0, 2, 1, 3))


def _gqa_group(q, k, v, q_norm_g, k_norm_g, rope, b, s):
    q = rms_norm(q, jnp.tile(q_norm_g, GQA_HEADS)[None, :], GQA_HEADS).reshape(b, s, GQA_HEADS, GQA_HEAD_DIM)
    k = rms_norm(k, jnp.tile(k_norm_g, GQA_KV_HEADS)[None, :], GQA_KV_HEADS).reshape(b, s, GQA_KV_HEADS, GQA_HEAD_DIM)
    v = v.reshape(b, s, GQA_KV_HEADS, GQA_HEAD_DIM)
    q = _apply_axial_rope(q, rope)
    k = _apply_axial_rope(k, rope)
    o = attention(_heads_first(q), _heads_first(k), _heads_first(v), GQA_HEAD_DIM ** -0.5)
    return _heads_first(o).reshape(b * s, GQA_WIDTH)


def _mla_group(c_q, c_kv, k_pe, q_norm_g, w_uq, kv_norm_g, w_ukv, rope, b, s):
    q = linear(rms_norm(c_q, q_norm_g[None, :], 1), w_uq).reshape(b, s, MLA_HEADS, MLA_NOPE_DIM + MLA_ROPE_DIM)
    q_nope, q_pe = q[..., :MLA_NOPE_DIM], q[..., MLA_NOPE_DIM:]
    kv = linear(rms_norm(c_kv, kv_norm_g[None, :], 1), w_ukv).reshape(b, s, MLA_HEADS, MLA_NOPE_DIM + MLA_V_DIM)
    k_nope, v = kv[..., :MLA_NOPE_DIM], kv[..., MLA_NOPE_DIM:]
    q_pe = _apply_axial_rope(q_pe, rope)
    k_pe = _apply_axial_rope(k_pe.reshape(b, s, 1, MLA_ROPE_DIM), rope)
    q = jnp.concatenate([q_nope, q_pe], axis=-1)
    k = jnp.concatenate([k_nope, jnp.broadcast_to(k_pe, (b, s, MLA_HEADS, MLA_ROPE_DIM))], axis=-1)
    o = attention(_heads_first(q), _heads_first(k), _heads_first(v), (MLA_NOPE_DIM + MLA_ROPE_DIM) ** -0.5)
    return _heads_first(o).reshape(b * s, MLA_HEADS * MLA_V_DIM)


def _ssd_group(z, xbc, dt_raw, conv_w, conv_b, dt_bias, a_log, d_skip, norm_g, b, s):
    wb = jnp.concatenate([conv_w, conv_b[None, :], jnp.zeros((CONV_PACK_ROWS - SSD_CONV - 1, SSD_CONV_DIM), F32)], axis=0)
    xbc = conv_silu(xbc.reshape(b, s, SSD_CONV_DIM), wb)
    xs = xbc[..., :SSD_INNER]
    bm = xbc[..., SSD_INNER:SSD_INNER + SSD_GROUPS * SSD_STATE]
    cm = xbc[..., SSD_INNER + SSD_GROUPS * SSD_STATE:]
    dt = jax.nn.softplus(dt_raw.reshape(b, s, 2, SSD_HEADS) + dt_bias)
    a_neg = -jnp.exp(a_log)
    y_fwd = ssd_chunked(xs, dt[:, :, 0], a_neg[0], bm, cm, False)
    y_bwd = ssd_chunked(xs, dt[:, :, 1], a_neg[1], bm, cm, True)
    y = y_fwd + y_bwd + xs * jnp.repeat(d_skip, SSD_HEAD_DIM)
    y = y.reshape(b * s, SSD_INNER) * jax.nn.silu(z)
    return rms_norm(y, norm_g[None, :], SSD_GROUPS)


MIXER_WEIGHTS = ('q_norm_g', 'k_norm_g', 'mla_q_norm_g', 'w_uq', 'mla_kv_norm_g', 'w_ukv', 'conv_w', 'conv_b',
                 'dt_bias', 'a_log', 'd_skip', 'ssd_norm_g')


def _mixer(proj, w, rope_a, rope_b, b, s):
    idx = np.cumsum(IN_SPLITS).tolist()
    q_a, k_a, v_a, cq_b, ckv_b, kpe_b, z_c, xbc_c, dt_c = [_in_cols(proj, lo, hi)
                                                           for lo, hi in zip([0] + idx[:-1], idx)]
    o_a = _gqa_group(q_a, k_a, v_a, w["q_norm_g"], w["k_norm_g"], rope_a, b, s)
    o_b = _mla_group(cq_b, ckv_b, kpe_b, w["mla_q_norm_g"], w["w_uq"], w["mla_kv_norm_g"], w["w_ukv"], rope_b, b, s)
    o_c = _ssd_group(z_c, xbc_c, dt_c, w["conv_w"], w["conv_b"], w["dt_bias"], w["a_log"], w["d_skip"],
                     w["ssd_norm_g"], b, s)
    return jnp.concatenate([o_a, o_b, o_c], axis=-1)


def _seq_tile(s, row_bytes):
    return _row_tile(s, row_bytes)


def _normmod_fwd(x, g, scale, shift):
    b, s, d = x.shape
    tr = _seq_tile(s, d * 4)

    def body(x_ref, g_ref, sc_ref, sh_ref, h_ref):
        xv = x_ref[0]
        r = lax.rsqrt(jnp.mean(xv * xv, axis=-1, keepdims=True) + EPS)
        h_ref[0] = (xv * r * g_ref[...] * (1.0 + sc_ref[0]) + sh_ref[0]).astype(BF16)

    act = pl.BlockSpec((1, tr, d), lambda bi, i: (bi, i, 0))
    vec = pl.BlockSpec((1, 1, d), lambda bi, i: (bi, 0, 0))
    return pl.pallas_call(
        body, name="normmod_fwd", grid=(b, s // tr),
        in_specs=[act, pl.BlockSpec((1, d), lambda bi, i: (0, 0)), vec, vec], out_specs=act,
        out_shape=jax.ShapeDtypeStruct((b, s, d), BF16),
        compiler_params=pltpu.CompilerParams(dimension_semantics=("parallel", "parallel")),
    )(x, g, scale, shift)


def _normmod_bwd(x, g, scale, dh, resid):
    b, s, d = x.shape
    tr = _seq_tile(s, d * 4)

    def body(x_ref, g_ref, sc_ref, dh_ref, res_ref, dx_ref, dg_ref, dsc_ref, dsh_ref):
        bi, i = pl.program_id(0), pl.program_id(1)

        @pl.when((bi == 0) & (i == 0))
        def _():
            dg_ref[...] = jnp.zeros_like(dg_ref)

        @pl.when(i == 0)
        def _():
            dsc_ref[...] = jnp.zeros_like(dsc_ref)
            dsh_ref[...] = jnp.zeros_like(dsh_ref)

        xv = x_ref[0]
        dhv = dh_ref[0]
        gv = g_ref[...]
        r = lax.rsqrt(jnp.mean(xv * xv, axis=-1, keepdims=True) + EPS)
        xhat = xv * r
        dsh_ref[0] += jnp.sum(dhv, axis=0, keepdims=True)
        dsc_ref[0] += jnp.sum(dhv * (xhat * gv), axis=0, keepdims=True)
        dn = dhv * (1.0 + sc_ref[0])
        dg_ref[...] += jnp.sum(dn * xhat, axis=0, keepdims=True)
        dxhat = dn * gv
        dx_ref[0] = r * (dxhat - xhat * jnp.mean(dxhat * xhat, axis=-1, keepdims=True)) + res_ref[0]

    act = pl.BlockSpec((1, tr, d), lambda bi, i: (bi, i, 0))
    vec = pl.BlockSpec((1, 1, d), lambda bi, i: (bi, 0, 0))
    gain = pl.BlockSpec((1, d), lambda bi, i: (0, 0))
    return pl.pallas_call(
        body, name="normmod_bwd", grid=(b, s // tr),
        in_specs=[act, gain, vec, act, act], out_specs=[act, gain, vec, vec],
        out_shape=[jax.ShapeDtypeStruct((b, s, d), F32), jax.ShapeDtypeStruct((1, d), F32),
                   jax.ShapeDtypeStruct((b, 1, d), F32), jax.ShapeDtypeStruct((b, 1, d), F32)],
        compiler_params=pltpu.CompilerParams(dimension_semantics=("arbitrary", "arbitrary")),
    )(x, g, scale, dh, resid)


def _gated_add(x, gate, t):
    b, s, d = x.shape
    tr = _seq_tile(s, d * 4)

    def body(x_ref, g_ref, t_ref, o_ref):
        o_ref[0] = x_ref[0] + g_ref[0] * t_ref[0]

    act = pl.BlockSpec((1, tr, d), lambda bi, i: (bi, i, 0))
    vec = pl.BlockSpec((1, 1, d), lambda bi, i: (bi, 0, 0))
    return pl.pallas_call(
        body, name="gated_add", grid=(b, s // tr), in_specs=[act, vec, act], out_specs=act,
        out_shape=jax.ShapeDtypeStruct((b, s, d), F32),
        compiler_params=pltpu.CompilerParams(dimension_semantics=("parallel", "parallel")),
    )(x, gate, t)


def _gated_bwd(dy, gate, t):
    b, s, d = dy.shape
    tr = _seq_tile(s, d * 4)

    def body(dy_ref, g_ref, t_ref, dt_ref, dgate_ref):
        @pl.when(pl.program_id(1) == 0)
        def _():
            dgate_ref[...] = jnp.zeros_like(dgate_ref)

        dyv = dy_ref[0]
        dt_ref[0] = (g_ref[0] * dyv).astype(BF16)
        dgate_ref[0] += jnp.sum(dyv * t_ref[0], axis=0, keepdims=True)

    act = pl.BlockSpec((1, tr, d), lambda bi, i: (bi, i, 0))
    vec = pl.BlockSpec((1, 1, d), lambda bi, i: (bi, 0, 0))
    return pl.pallas_call(
        body, name="gated_bwd", grid=(b, s // tr), in_specs=[act, vec, act], out_specs=[act, vec],
        out_shape=[jax.ShapeDtypeStruct((b, s, d), BF16), jax.ShapeDtypeStruct((b, 1, d), F32)],
        compiler_params=pltpu.CompilerParams(dimension_semantics=("parallel", "arbitrary")),
    )(dy, gate, t)


def _swiglu_fwd(gu, plan=None):
    rows, f2 = gu.shape
    f = f2 // 2
    tr = _row_tile(rows, f2 * 4, SWIGLU_TILE_BYTES)
    steps = rows // tr
    p_in, p_out, p_shapes, p_scratch, p_args = _plan_specs(plan)

    def body(gu_ref, *rest):
        a_ref = rest[len(p_in)]
        i = pl.program_id(0)

        def compute():
            gt = gu_ref[:, :f]
            a_ref[...] = (gt * jax.nn.sigmoid(gt) * gu_ref[:, f:]).astype(BF16)

        _ride(plan, rest[:len(p_in)] + rest[len(p_in) + 1:], i == 0, i == steps - 1, compute)

    outs = pl.pallas_call(
        body, name="swiglu_fwd", grid=(steps,),
        in_specs=[pl.BlockSpec((tr, f2), lambda i: (i, 0))] + p_in,
        out_specs=[pl.BlockSpec((tr, f), lambda i: (i, 0))] + p_out,
        out_shape=[jax.ShapeDtypeStruct((rows, f), BF16)] + p_shapes, scratch_shapes=p_scratch,
        compiler_params=pltpu.CompilerParams(dimension_semantics=("arbitrary" if plan is not None else "parallel",)),
    )(gu, *p_args)
    return outs[0] if plan is None else (outs[0], list(outs[1:]))


def _swiglu_bwd(gu, dact):
    rows, f2 = gu.shape
    f = f2 // 2
    tr = _row_tile(rows, f2 * 4, SWIGLU_TILE_BYTES)

    def body(gu_ref, da_ref, dgu_ref):
        gt = gu_ref[:, :f]
        up = gu_ref[:, f:]
        da = da_ref[...]
        sg = jax.nn.sigmoid(gt)
        dgu_ref[:, :f] = (da * up * (sg * (1.0 + gt * (1.0 - sg)))).astype(BF16)
        dgu_ref[:, f:] = (da * gt * sg).astype(BF16)

    return pl.pallas_call(
        body, name="swiglu_bwd", grid=(rows // tr,),
        in_specs=[pl.BlockSpec((tr, f2), lambda i: (i, 0)), pl.BlockSpec((tr, f), lambda i: (i, 0))],
        out_specs=pl.BlockSpec((tr, f2), lambda i: (i, 0)),
        out_shape=jax.ShapeDtypeStruct((rows, f2), BF16),
        compiler_params=pltpu.CompilerParams(dimension_semantics=("parallel",)),
    )(gu, dact)


class _Gathered:
    def __init__(self, shards):
        self.shards, self.full = shards, {}

    def plan(self, keys):
        return _gather_plan([self.shards[k] for k in keys])

    def store(self, keys, outs):
        for key, out in zip(keys, outs):
            name = key[0]
            g = out.reshape((N_CHIPS,) + self.shards[key].shape)
            if name in CHIP_BLOCKED:
                full = g
            elif name in COL_SHARDED and name not in TRANSPOSED:
                full = _cols_full(g).astype(F32)
            else:
                full = g.reshape(g.shape[0] * g.shape[1], g.shape[2])
            self.full[key] = full

    def carry(self, keys, fn):
        if not keys:
            return fn(None)
        res, outs = fn(self.plan(keys))
        self.store(keys, outs)
        return res


def _gather_schedule(depth):
    every = [(n, l) for l in range(depth) for n in ('w_uq', 'w_ukv')]
    sched = {'first': [('w_in', 0)] + every}
    for l in range(depth):
        sched[('w_in_fwd', l)] = ([('w_gate_up', l)] if l == 0 else []) + [('w_out', l)]
        if l == 0:
            sched[('swiglu_fwd', l)] = [('w_down', l)]
        if l + 1 < depth:
            sched[('w_out_fwd', l)] = [('w_in', l + 1)]
            sched[('w_gate_up_fwd', l)] = [('w_gate_up', l + 1)]
            sched[('w_down_fwd', l)] = [('w_down', l + 1)]
    return sched


GATE_UP_PIECES = (1, 1, 3, 3)


class _Reducer:
    def __init__(self):
        self.parts, self.recv, self.result = {}, {}, {}

    def add(self, items):
        blocks = []
        for (name, _), grad in items:
            if name in CHIP_BLOCKED:
                blocks.append(grad)
            elif name in COL_SHARDED and name not in TRANSPOSED:
                blocks.append(_cols_split(grad))
            else:
                blocks.append(grad.reshape(N_CHIPS, grad.shape[0] // N_CHIPS, grad.shape[1]))
        for (key, _), parts in zip(items, _rs_parts(blocks)):
            self.parts[key] = parts
            self.recv[key] = []

    def pieces(self, key):
        rows = self.parts[key].shape[1]
        shares = GATE_UP_PIECES if key[0] == 'w_gate_up' else (1,)
        unit = rows // sum(shares)
        starts = np.cumsum((0,) + shares[:-1])
        return [(key, int(a) * unit, n * unit) for a, n in zip(starts, shares)]

    def plan(self, jobs):
        return _chip_exchange_plan([(self.parts[key], row0, rows) for key, row0, rows in jobs])

    def store(self, jobs, outs):
        complete = []
        for (key, row0, rows), out in zip(jobs, outs):
            self.recv[key].append((row0, out))
            if len(self.recv[key]) == len(self.pieces(key)):
                complete.append(key)
        if complete:
            items = [(self.parts[key], sorted(self.recv[key], key=lambda t: t[0])) for key in complete]
            self.result.update(zip(complete, _rs_result(items)))

    def carry(self, keys, fn, piece=None, also=()):
        jobs = [j for key in keys for j in self.pieces(key)]
        if piece is not None:
            jobs = [j for key in keys for j in self.pieces(key)[piece:piece + 1]]
        jobs += [j for key in also for j in self.pieces(key)]
        if not jobs:
            return fn(None)
        res, outs = fn(self.plan(jobs))
        self.store(jobs, outs)
        return res

    def flush(self):
        jobs = [j for key in self.parts for j in self.pieces(key)
                if key not in self.result and j[1] not in [r for r, _ in self.recv[key]]]
        if jobs:
            self.store(jobs, _run_plan(self.plan(jobs), "rs_chip_exchange"))


def _layer_fwd(x, mod, w, gathered, l, sched, rope_a, rope_b):
    b, s, d = x.shape
    m = b * s
    shift1, scale1, gate1, shift2, scale2, gate2 = [t[:, None, :] for t in jnp.split(mod, 6, axis=-1)]
    g1, g2 = w["norm1_g"][None, :], w["norm2_g"][None, :]
    full = lambda n: gathered.full[(n, l)]
    h1 = _normmod_fwd(x, g1, scale1, shift1).reshape(m, d)
    proj = gathered.carry(sched.get(('w_in_fwd', l)), lambda p: _matmul(h1, full('w_in'), tb=True, name="w_in_fwd", plan=p))
    mixer_w = {n: (full(n) if n in COL_SHARDED else w[n]) for n in MIXER_WEIGHTS}
    o, mixer_vjp = jax.vjp(lambda p, mw: _mixer(p, mw, rope_a, rope_b, b, s), proj, mixer_w)
    o = o.astype(BF16)
    mix = gathered.carry(sched.get(('w_out_fwd', l)), lambda p: _matmul(o, full('w_out'), name="w_out_fwd", plan=p))
    mix = mix.reshape(b, s, d)
    x_mid = _gated_add(x, gate1, mix)
    h2 = _normmod_fwd(x_mid, g2, scale2, shift2).reshape(m, d)
    gu = gathered.carry(sched.get(('w_gate_up_fwd', l)),
                        lambda p: _matmul(h2, full('w_gate_up'), name="w_gate_up_fwd", plan=p, chips='b'))
    act = gathered.carry(sched.get(('swiglu_fwd', l)), lambda p: _swiglu_fwd(gu, plan=p))
    ffn = gathered.carry(sched.get(('w_down_fwd', l)), lambda p: _matmul(act, full('w_down'), name="w_down_fwd", plan=p))
    ffn = ffn.reshape(b, s, d)
    x_out = _gated_add(x_mid, gate2, ffn)
    res = (x, x_mid, h1, h2, o, mix, gu, act, ffn, mixer_vjp, scale1, gate1, scale2, gate2, g1, g2)
    return x_out, res


def _layer_bwd(res, gathered, reducer, l, depth, dx_out):
    x, x_mid, h1, h2, o, mix, gu, act, ffn, mixer_vjp, scale1, gate1, scale2, gate2, g1, g2 = res
    b, s, d = x.shape
    m = b * s
    full = lambda n: gathered.full[(n, l)]
    above = l + 1 < depth
    dffn, dgate2 = _gated_bwd(dx_out, gate2, ffn)
    dffn = dffn.reshape(m, d)
    dact = reducer.carry([('w_out', l + 1), ('w_uq', l + 1), ('w_ukv', l + 1)] if above else [],
                         lambda p: _matmul(dffn, full('w_down'), tb=True, name="w_down_dx", plan=p))
    dw = reducer.carry([('w_in', l + 1)] if above else [],
                       lambda p: _matmul(act, dffn, ta=True, name="w_down_dw", plan=p))
    reducer.add([(('w_down', l), dw)])
    dgu = _swiglu_bwd(gu, dact)
    dh2 = reducer.carry([('w_down', l)], lambda p: _matmul(dgu, full('w_gate_up'), tb=True, name="w_gate_up_dx", plan=p,
                                                           chips='b'))
    dh2 = dh2.reshape(b, s, d)
    reducer.add([(('w_gate_up', l), _matmul(h2, dgu, ta=True, name="w_gate_up_dw", chips='out'))])
    dx_mid, dg2, dscale2, dshift2 = _normmod_bwd(x_mid, g2, scale2, dh2, dx_out)
    dmix, dgate1 = _gated_bwd(dx_mid, gate1, mix)
    dmix = dmix.reshape(m, d)
    gate_up = [('w_gate_up', l)]
    do = reducer.carry(gate_up, lambda p: _matmul(dmix, full('w_out'), tb=True, name="w_out_dx", plan=p), piece=0)
    dw_out = reducer.carry(gate_up, lambda p: _matmul(o, dmix, ta=True, name="w_out_dw", plan=p), piece=1)
    dproj, grads = mixer_vjp(do)
    grads = dict(grads)
    reducer.add([(('w_out', l), dw_out), (('w_uq', l), grads.pop('w_uq')), (('w_ukv', l), grads.pop('w_ukv'))])
    dproj = dproj.astype(BF16)
    bottom = [('w_out', l), ('w_uq', l), ('w_ukv', l)] if l == 0 else []
    dh1 = reducer.carry(gate_up, lambda p: _matmul(dproj, full('w_in'), name="w_in_dx", plan=p), piece=2, also=bottom)
    dh1 = dh1.reshape(b, s, d)
    dw = reducer.carry(gate_up, lambda p: _matmul(dproj, h1, ta=True, name="w_in_dw", plan=p), piece=3)
    reducer.add([(('w_in', l), dw)])
    dx, dg1, dscale1, dshift1 = _normmod_bwd(x, g1, scale1, dh1, dx_mid)
    grads["norm1_g"], grads["norm2_g"] = dg1[0], dg2[0]
    dmod = jnp.concatenate([dshift1, dscale1, dgate1, dshift2, dscale2, dgate2], axis=-1)[:, 0, :]
    return dx, dmod, grads


def _tail_loss(x2, final_norm_g, target2):
    return loss_head(rms_norm(x2, final_norm_g[None, :], 1), target2)


def _forward_backward(x, mod, small, gathered, reducer, final_norm_g, target):
    b, s, d = x.shape
    depth = len(small)
    rope_a = _axial_rope_tables(s, GQA_HEAD_DIM)
    rope_b = _axial_rope_tables(s, MLA_ROPE_DIM)
    sched = _gather_schedule(depth)
    first = sched['first']
    gathered.store(first, _run_plan(gathered.plan(first), "all_gather_chips"))
    saved = []
    for l in range(depth):
        x, res = _layer_fwd(x, mod[l], small[l], gathered, l, sched, rope_a, rope_b)
        saved.append(res)
    loss, (dx2, dfinal) = jax.value_and_grad(_tail_loss, argnums=(0, 1))(
        x.reshape(b * s, d), final_norm_g, target.reshape(b * s, d))
    dx = dx2.reshape(b, s, d)
    dmods, gsmall = [None] * depth, [None] * depth
    for l in reversed(range(depth)):
        dx, dmods[l], gsmall[l] = _layer_bwd(saved[l], gathered, reducer, l, depth, dx)
    return loss, dx, jnp.stack(dmods), gsmall, dfinal


ANY = pl.BlockSpec(memory_space=pl.ANY)


def _flip_if(v, bit):
    return 1 - v if bit else v


def _all_gather_devices(x):
    def body(x_ref, out_ref, send_sems, recv_sems):
        mx, my, mc = lax.axis_index("x"), lax.axis_index("y"), lax.axis_index("c")
        me = 4 * mx + 2 * my + mc
        sends = []
        for k in range(1, N_DEV):
            peer = (_flip_if(mx, k & 4), _flip_if(my, k & 2), _flip_if(mc, k & 1))
            cp = pltpu.make_async_remote_copy(src_ref=x_ref, dst_ref=out_ref.at[me], send_sem=send_sems.at[k - 1],
                                              recv_sem=recv_sems.at[k - 1], device_id=peer, device_id_type=MESH)
            cp.start()
            sends.append(cp)
        for k in range(1, N_DEV):
            peer = (_flip_if(mx, k & 4), _flip_if(my, k & 2), _flip_if(mc, k & 1))
            src = 4 * peer[0] + 2 * peer[1] + peer[2]
            pltpu.make_async_remote_copy(src_ref=x_ref, dst_ref=out_ref.at[src], send_sem=send_sems.at[k - 1],
                                         recv_sem=recv_sems.at[k - 1], device_id=peer, device_id_type=MESH).wait_recv()
        for cp in sends:
            cp.wait_send()

    out = pl.pallas_call(
        body, name="all_gather_devices", in_specs=[ANY], out_specs=ANY,
        out_shape=jax.ShapeDtypeStruct((N_DEV,) + x.shape, x.dtype),
        scratch_shapes=[pltpu.SemaphoreType.DMA((N_DEV - 1,)), pltpu.SemaphoreType.DMA((N_DEV - 1,))],
    )(x)
    me = 4 * lax.axis_index("x") + 2 * lax.axis_index("y") + lax.axis_index("c")
    return lax.dynamic_update_index_in_dim(out, x, me, 0)


class _Plan:
    def __init__(self, inputs, out_shapes, sem_counts, start, finish):
        self.inputs, self.out_shapes, self.sem_counts = list(inputs), list(out_shapes), list(sem_counts)
        self.start, self.finish = start, finish

    def specs(self):
        return ([ANY] * len(self.inputs), [ANY] * len(self.out_shapes),
                [pltpu.SemaphoreType.DMA((c,)) for c in self.sem_counts])

    def split(self, refs):
        a, b = len(self.inputs), len(self.inputs) + len(self.out_shapes)
        return refs[:a], refs[a:b], refs[b:]


def _run_plan(plan, name):
    def body(*refs):
        ins, outs, sems = plan.split(refs)
        plan.start(ins, outs, sems)
        plan.finish(ins, outs, sems)

    in_specs, out_specs, scratch = plan.specs()
    return pl.pallas_call(body, name=name, in_specs=in_specs, out_specs=out_specs, out_shape=plan.out_shapes,
                          scratch_shapes=scratch)(*plan.inputs)


def _gather_plan(shards):
    n = len(shards)
    halves = [t.reshape(2, t.shape[0] // 2, t.shape[1]) for t in shards]
    count = (N_CHIPS - 1) * n

    def copies(kind, ins, outs, sems):
        ici_send, ici_recv, d2d_send, d2d_recv, own_send, own_recv = sems
        mx, my, mc = lax.axis_index("x"), lax.axis_index("y"), lax.axis_index("c")
        me = 2 * mx + my
        sibling = (mx, my, 1 - mc)
        if kind == 'own':
            return [pltpu.make_async_remote_copy(src_ref=ins[i], dst_ref=outs[i].at[me], send_sem=own_send.at[i],
                                                 recv_sem=own_recv.at[i], device_id=sibling, device_id_type=MESH)
                    for i in range(n)]
        cps = []
        for k in range(1, N_CHIPS):
            peer = (_flip_if(mx, k & 2), _flip_if(my, k & 1), mc)
            src = 2 * peer[0] + peer[1]
            for i in range(n):
                j = (k - 1) * n + i
                if kind in ('ici', 'landed'):
                    dst = outs[i].at[me, mc] if kind == 'ici' else outs[i].at[src, mc]
                    cps.append(pltpu.make_async_remote_copy(
                        src_ref=ins[i].at[mc], dst_ref=dst, send_sem=ici_send.at[j], recv_sem=ici_recv.at[j],
                        device_id=peer, device_id_type=MESH))
                else:
                    half = outs[i].at[src, mc] if kind == 'fwd' else outs[i].at[src, 1 - mc]
                    cps.append(pltpu.make_async_remote_copy(
                        src_ref=half, dst_ref=half, send_sem=d2d_send.at[j], recv_sem=d2d_recv.at[j],
                        device_id=sibling, device_id_type=MESH))
        return cps

    def start(ins, outs, sems):
        for cp in copies('own', ins, outs, sems) + copies('ici', ins, outs, sems):
            cp.start()

    def finish(ins, outs, sems):
        fwd = copies('fwd', ins, outs, sems)
        for arrived, onward in zip(copies('landed', ins, outs, sems), fwd):
            arrived.wait_recv()
            onward.start()
        own = copies('own', ins, outs, sems)
        for cp in copies('fwd_in', ins, outs, sems) + own:
            cp.wait_recv()
        for cp in own + copies('ici', ins, outs, sems) + fwd:
            cp.wait_send()

    out_shapes = [jax.ShapeDtypeStruct((N_CHIPS,) + t.shape, t.dtype) for t in halves]
    return _Plan(halves, out_shapes, [count] * 4 + [n] * 2, start, finish)


def _sibling_exchange(blocks, name):
    n = len(blocks)

    def body(*refs):
        ins, outs = refs[:n], refs[n:2 * n]
        send_sems, recv_sems = refs[2 * n:]
        mx, my, mc = lax.axis_index("x"), lax.axis_index("y"), lax.axis_index("c")
        cps = []
        for i in range(n):
            cp = pltpu.make_async_remote_copy(src_ref=ins[i], dst_ref=outs[i], send_sem=send_sems.at[i],
                                              recv_sem=recv_sems.at[i], device_id=(mx, my, 1 - mc),
                                              device_id_type=MESH)
            cp.start()
            cps.append(cp)
        for cp in cps:
            cp.wait()

    return pl.pallas_call(
        body, name=name, in_specs=[ANY] * n, out_specs=[ANY] * n,
        out_shape=[jax.ShapeDtypeStruct(t.shape, t.dtype) for t in blocks],
        scratch_shapes=[pltpu.SemaphoreType.DMA((n,)), pltpu.SemaphoreType.DMA((n,))],
    )(*blocks)


def _add_halves(own, recv):
    nb, r, c = own.shape
    tr = _row_tile(r, c * 4)

    def body(g_ref, r_ref, o_ref):
        o_ref[...] = (g_ref[...] + r_ref[...].astype(F32)).astype(BF16)

    spec = pl.BlockSpec((1, tr, c), lambda k, i: (k, i, 0))
    return pl.pallas_call(
        body, name="rs_add_halves", grid=(nb, r // tr), in_specs=[spec, spec], out_specs=spec,
        out_shape=jax.ShapeDtypeStruct((nb, r, c), BF16),
        compiler_params=pltpu.CompilerParams(dimension_semantics=("parallel", "parallel")),
    )(own, recv)


def _chip_exchange_plan(jobs):
    n = len(jobs)
    count = (N_CHIPS - 1) * n

    def copies(ins, outs, sems):
        send_sems, recv_sems = sems
        mx, my, mc = lax.axis_index("x"), lax.axis_index("y"), lax.axis_index("c")
        cps = []
        for k in range(1, N_CHIPS):
            peer = (_flip_if(mx, k & 2), _flip_if(my, k & 1), mc)
            dst_chip = 2 * peer[0] + peer[1]
            for i, (_, row0, rows) in enumerate(jobs):
                j = (k - 1) * n + i
                cps.append(pltpu.make_async_remote_copy(
                    src_ref=ins[i].at[dst_chip, pl.ds(row0, rows)], dst_ref=outs[i].at[k - 1],
                    send_sem=send_sems.at[j], recv_sem=recv_sems.at[j], device_id=peer, device_id_type=MESH))
        return cps

    def start(ins, outs, sems):
        for cp in copies(ins, outs, sems):
            cp.start()

    def finish(ins, outs, sems):
        for cp in copies(ins, outs, sems):
            cp.wait()

    out_shapes = [jax.ShapeDtypeStruct((N_CHIPS - 1, rows, p.shape[2]), p.dtype) for p, _, rows in jobs]
    return _Plan([p for p, _, _ in jobs], out_shapes, [count, count], start, finish)


def _sum_chips(parts, recv, chip, row0):
    _, rows, c = recv.shape
    tr = _row_tile(rows, c * 4)
    assert row0 % tr == 0

    def body(chip_ref, p_ref, r_ref, o_ref):
        acc = p_ref[0].astype(F32)
        for k in range(N_CHIPS - 1):
            acc = acc + r_ref[k].astype(F32)
        o_ref[...] = acc

    return pl.pallas_call(
        body, name="rs_sum_chips",
        grid_spec=pltpu.PrefetchScalarGridSpec(
            num_scalar_prefetch=1, grid=(rows // tr,),
            in_specs=[pl.BlockSpec((1, tr, c), lambda i, chip_ref: (chip_ref[0], i + row0 // tr, 0)),
                      pl.BlockSpec((N_CHIPS - 1, tr, c), lambda i, chip_ref: (0, i, 0))],
            out_specs=pl.BlockSpec((tr, c), lambda i, chip_ref: (i, 0))),
        out_shape=jax.ShapeDtypeStruct((rows, c), F32),
        compiler_params=pltpu.CompilerParams(dimension_semantics=("parallel",)),
    )(chip, parts, recv)


def _sum_leading(t, name):
    nb, r, c = t.shape
    tr = _row_tile(r, c * 4 * nb)

    def body(t_ref, o_ref):
        acc = t_ref[0]
        for k in range(1, nb):
            acc = acc + t_ref[k]
        o_ref[...] = acc

    return pl.pallas_call(
        body, name=name, grid=(r // tr,),
        in_specs=[pl.BlockSpec((nb, tr, c), lambda i: (0, i, 0))],
        out_specs=pl.BlockSpec((tr, c), lambda i: (i, 0)),
        out_shape=jax.ShapeDtypeStruct((r, c), F32),
        compiler_params=pltpu.CompilerParams(dimension_semantics=("parallel",)),
    )(t)


def _rs_parts(grads):
    mc = lax.axis_index("c")
    split = [g.reshape(g.shape[0], 2, g.shape[1] // 2, g.shape[2]) for g in grads]
    own = [lax.dynamic_index_in_dim(g, mc, axis=1, keepdims=False) for g in split]
    away = [lax.dynamic_index_in_dim(g, 1 - mc, axis=1, keepdims=False).astype(BF16) for g in split]
    return [_add_halves(o, r) for o, r in zip(own, _sibling_exchange(away, "rs_sibling_exchange"))]


def _rs_result(items):
    mc = lax.axis_index("c")
    chip = (2 * lax.axis_index("x") + lax.axis_index("y")).astype(jnp.int32).reshape(1)
    mine = []
    for parts, pieces in items:
        done = [_sum_chips(parts, recv, chip, row0) for row0, recv in pieces]
        mine.append(done[0] if len(done) == 1 else jnp.concatenate(done, axis=0))
    theirs = _sibling_exchange(mine, "rs_sibling_swap")
    return [jnp.concatenate([jnp.where(mc == 0, a, b), jnp.where(mc == 0, b, a)], axis=0)
            for a, b in zip(mine, theirs)]


def _adamw(w, g, m, v, plan=None):
    shape = w.shape
    cols = shape[-1]
    if len(shape) == 3:
        lead, rows = shape[0], shape[1]
    else:
        lead, rows = 1, (int(np.prod(shape[:-1])) if len(shape) > 1 else 1)
    w2, g2, m2, v2 = [t.reshape(lead, rows, cols) for t in (w, g, m, v)]
    tr = _row_tile(rows, cols * 4, ADAM_TILE_BYTES)
    t0, tc = 1, cols
    if tr == rows and rows * cols * 4 > ADAM_TILE_BYTES and cols % LANES == 0:
        tc = max(t for t in _tile_cands(cols, cols) if t == LANES or rows * t * 4 <= ADAM_TILE_BYTES)
    elif rows < 8 and cols % LANES == 0:
        fits = [(a * t, t, a) for a in range(1, lead + 1) if lead % a == 0 for t in _tile_cands(cols, cols)
                if a * rows * t * 4 <= ADAM_TILE_BYTES]
        _, tc, t0 = max(fits) if fits else (0, LANES, 1)
    per = (rows // tr) * (cols // tc)
    steps = (lead // t0) * per
    p_in, p_out, p_shapes, p_scratch, p_args = _plan_specs(plan)

    def body(w_ref, g_ref, m_ref, v_ref, *rest):
        d_ref, mo_ref, vo_ref = rest[len(p_in):len(p_in) + 3]
        i = pl.program_id(0) * per + pl.program_id(1)

        def compute():
            gv = g_ref[...]
            mn = ADAM_B1 * m_ref[...] + (1.0 - ADAM_B1) * gv
            vn = ADAM_B2 * v_ref[...] + (1.0 - ADAM_B2) * (gv * gv)
            m_hat = mn / (1.0 - ADAM_B1 ** ADAM_STEP)
            v_hat = vn / (1.0 - ADAM_B2 ** ADAM_STEP)
            d_ref[...] = -ADAM_LR * (m_hat / (jnp.sqrt(v_hat) + ADAM_EPS) + ADAM_WD * w_ref[...])
            mo_ref[...] = mn
            vo_ref[...] = vn

        _ride(plan, rest[:len(p_in)] + rest[len(p_in) + 3:], i == 0, i == steps - 1, compute)

    col_blocks = cols // tc
    spec = pl.BlockSpec((t0, tr, tc), lambda a, i: (a, i // col_blocks, i % col_blocks))
    sem = "arbitrary" if plan is not None else "parallel"
    outs = pl.pallas_call(
        body, name="adamw", grid=(lead // t0, per), in_specs=[spec] * 4 + p_in, out_specs=[spec] * 3 + p_out,
        out_shape=[jax.ShapeDtypeStruct((lead, rows, cols), F32)] * 3 + p_shapes, scratch_shapes=p_scratch,
        compiler_params=pltpu.CompilerParams(dimension_semantics=(sem, sem), vmem_limit_bytes=ADAM_VMEM_LIMIT),
    )(w2, g2, m2, v2, *p_args)
    res = [t.reshape(shape) for t in outs[:3]]
    return res if plan is None else (res, list(outs[3:]))


WEIGHTS = ['w_ada', 'b_ada', 'norm1_g', 'norm2_g', 'w_in', 'q_norm_g', 'k_norm_g', 'mla_q_norm_g', 'w_uq',
           'mla_kv_norm_g', 'w_ukv', 'conv_w', 'conv_b', 'dt_bias', 'a_log', 'd_skip', 'ssd_norm_g', 'w_out',
           'w_gate_up', 'w_down', 'final_norm_g']
COL_SHARDED = ('w_in', 'w_uq', 'w_ukv', 'w_gate_up')
ROW_SHARDED = ('w_out', 'w_down')
CHIP_BLOCKED = ('w_gate_up',)
TRANSPOSED = ('w_in',)
SMALL_LAYER = ('norm1_g', 'norm2_g', 'q_norm_g', 'k_norm_g', 'mla_q_norm_g', 'mla_kv_norm_g', 'conv_w', 'conv_b',
               'dt_bias', 'a_log', 'd_skip', 'ssd_norm_g')


def _pack(parts):
    flat = jnp.concatenate([p.reshape(-1) for p in parts])
    n = flat.shape[0]
    rows = -(-n // (8 * LANES)) * 8
    return jnp.pad(flat, (0, rows * LANES - n)).reshape(rows, LANES)


def _unpack(flat, shapes):
    out, pos = [], 0
    for shp in shapes:
        size = int(np.prod(shp))
        out.append(flat[pos:pos + size].reshape(shp))
        pos += size
    return out


def _cols_full(gathered):
    k, r, c = gathered.shape
    return jnp.transpose(gathered, (1, 0, 2)).reshape(r, k * c)


def _cols_split(full):
    r, c4 = full.shape
    return jnp.transpose(full.reshape(r, N_CHIPS, c4 // N_CHIPS), (1, 0, 2))


def kernel(x, c, w_ada, b_ada, norm1_g, norm2_g, w_in, q_norm_g, k_norm_g, mla_q_norm_g, w_uq, mla_kv_norm_g, w_ukv, conv_w, conv_b, dt_bias, a_log, d_skip, ssd_norm_g, w_out, w_gate_up, w_down, final_norm_g, loss_target, m_w_ada, m_b_ada, m_norm1_g, m_norm2_g, m_w_in, m_q_norm_g, m_k_norm_g, m_mla_q_norm_g, m_w_uq, m_mla_kv_norm_g, m_w_ukv, m_conv_w, m_conv_b, m_dt_bias, m_a_log, m_d_skip, m_ssd_norm_g, m_w_out, m_w_gate_up, m_w_down, m_final_norm_g, v_w_ada, v_b_ada, v_norm1_g, v_norm2_g, v_w_in, v_q_norm_g, v_k_norm_g, v_mla_q_norm_g, v_w_uq, v_mla_kv_norm_g, v_w_ukv, v_conv_w, v_conv_b, v_dt_bias, v_a_log, v_d_skip, v_ssd_norm_g, v_w_out, v_w_gate_up, v_w_down, v_final_norm_g):
    args = dict(locals())
    weights = {n: args[n] for n in WEIGHTS}
    depth = w_in.shape[0]
    bl, s, d = x.shape
    mx, my, mc = lax.axis_index("x"), lax.axis_index("y"), lax.axis_index("c")
    chip = 2 * mx + my
    dev = 2 * chip + mc
    ada_cols = w_ada.shape[-1]
    conv_cols = conv_w.shape[-1]

    first_shapes = [c.shape, conv_w.shape]
    first = _all_gather_devices(_pack([c, conv_w]))
    first = [_unpack(first[i].reshape(-1), first_shapes) for i in range(N_DEV)]
    c_act = jax.nn.silu(jnp.concatenate([f[0] for f in first], axis=0))
    conv_w_full = jnp.concatenate([first[2 * k][1] for k in range(N_CHIPS)], axis=-1)

    b_cols = lax.dynamic_slice_in_dim(b_ada, chip * ada_cols, ada_cols, axis=1)
    c_act_b = c_act.astype(BF16)
    mod_cols = jnp.stack([_matmul(c_act_b, w_ada[l], name="ada_fwd") + b_cols[l][None, :]
                          for l in range(depth)])
    mod_all = _all_gather_devices(mod_cols.reshape(depth * N_DEV * bl, ada_cols))
    mod_all = mod_all.reshape(N_DEV, depth, N_DEV, bl, ada_cols)
    mod_mine = lax.dynamic_index_in_dim(mod_all, dev, axis=2, keepdims=False)
    mod = jnp.concatenate([mod_mine[2 * k] for k in range(N_CHIPS)], axis=-1)

    big = COL_SHARDED + ROW_SHARDED
    flip = lambda t: jnp.transpose(t, (2, 0, 1))
    unflip = lambda t: jnp.transpose(t, (1, 2, 0))
    shards = {(n, l): weights[n][l].astype(BF16) for n in big for l in range(depth)}
    for l in range(depth):
        shards[('w_in', l)] = jnp.pad(flip(w_in)[:, l, :].astype(BF16), ((0, IN_SHARD_PAD - IN_SHARD), (0, 0)))
    gathered = _Gathered(shards)
    reducer = _Reducer()
    small_w = []
    for l in range(depth):
        w = {n: weights[n][l] for n in SMALL_LAYER if n != 'conv_w'}
        w['conv_w'] = conv_w_full[l]
        small_w.append(w)
    loss_local, gx, gmod, glayers, gfinal = _forward_backward(x, mod, small_w, gathered, reducer, final_norm_g,
                                                              loss_target)

    small_parts = [jnp.stack([glayers[l][n] for l in range(depth)]) for n in SMALL_LAYER]
    small_parts += [gfinal, loss_local.reshape(1), gmod]
    small_shapes = [p.shape for p in small_parts]
    last = _all_gather_devices(_pack(small_parts))
    summed = _unpack(_sum_leading(last, "sum_devices").reshape(-1), small_shapes)
    small = dict(zip(SMALL_LAYER, summed[:len(SMALL_LAYER)]))
    g_final, loss, gmod_sum = summed[len(SMALL_LAYER):]
    small['conv_w'] = lax.dynamic_slice_in_dim(small['conv_w'], chip * conv_cols, conv_cols, axis=2)
    gmod_all = jnp.stack([_unpack(last[i].reshape(-1), small_shapes)[-1] for i in range(N_DEV)], axis=1)
    gmod_all = gmod_all.reshape(depth, N_DEV * bl, gmod.shape[-1])
    gmod_cols = lax.dynamic_slice_in_dim(gmod_all, chip * ada_cols, ada_cols, axis=2)
    g_w_ada = jnp.stack([_matmul(c_act_b, gmod_cols[l].astype(BF16), ta=True, name="ada_dw") for l in range(depth)])
    g_b_ada = gmod_sum[:, 0]
    for i in range(1, bl):
        g_b_ada = g_b_ada + gmod_sum[:, i]

    grad = {'w_ada': g_w_ada, 'b_ada': g_b_ada, 'final_norm_g': g_final}
    for n in SMALL_LAYER:
        grad[n] = small[n]

    delta, new_m, new_v = {}, {}, {}
    left = [key for key in reducer.parts if key not in reducer.result and not reducer.recv[key]]
    grad['w_down'] = jnp.stack([reducer.result[('w_down', l)] for l in range(depth)])
    delta['w_down'], new_m['w_down'], new_v['w_down'] = reducer.carry(
        left, lambda p: _adamw(w_down, grad['w_down'], m_w_down, v_w_down, plan=p))
    reducer.flush()
    for n in big:
        if n in TRANSPOSED:
            grad[n] = jnp.stack([reducer.result[(n, l)][:weights[n].shape[2]] for l in range(depth)], axis=1)
        elif n != 'w_down':
            grad[n] = jnp.stack([reducer.result[(n, l)] for l in range(depth)])
    for n in WEIGHTS:
        if n in TRANSPOSED:
            outs = _adamw(flip(weights[n]), grad[n], flip(args["m_" + n]), flip(args["v_" + n]))
            grad[n], delta[n], new_m[n], new_v[n] = [unflip(t) for t in [grad[n]] + outs]
        elif n != 'w_down':
            delta[n], new_m[n], new_v[n] = _adamw(weights[n], grad[n], args["m_" + n], args["v_" + n])
    return (loss.reshape(()), gx, *[grad[n] for n in WEIGHTS], *[delta[n] for n in WEIGHTS],
            *[new_m[n] for n in WEIGHTS], *[new_v[n] for n in WEIGHTS])
```

```python
import functools

import numpy as np
import jax
import jax.numpy as jnp
from jax import lax
from jax.experimental import pallas as pl
from jax.experimental.pallas import tpu as pltpu

F32 = jnp.float32
BF16 = jnp.bfloat16
HIGHEST = lax.Precision.HIGHEST
MESH = pl.DeviceIdType.MESH

GRID_W = 64
ROPE_THETA = 10000.0
EPS = 1e-6

GQA_HEADS, GQA_KV_HEADS, GQA_HEAD_DIM = 6, 2, 128
GQA_WIDTH = GQA_HEADS * GQA_HEAD_DIM
GQA_KV_WIDTH = GQA_KV_HEADS * GQA_HEAD_DIM
MLA_HEADS, MLA_Q_LORA, MLA_KV_LORA = 4, 512, 256
MLA_NOPE_DIM, MLA_ROPE_DIM, MLA_V_DIM = 128, 64, 128
SSD_HEADS, SSD_HEAD_DIM, SSD_GROUPS, SSD_STATE, SSD_CONV, SSD_CHUNK = 12, 64, 2, 128, 5, 128
SSD_INNER = SSD_HEADS * SSD_HEAD_DIM
SSD_CONV_DIM = SSD_INNER + 2 * SSD_GROUPS * SSD_STATE
SSD_GROUP_HEADS = SSD_HEADS // SSD_GROUPS
SSD_GROUP_WIDTH = SSD_GROUP_HEADS * SSD_HEAD_DIM
IN_SPLITS = (GQA_WIDTH, GQA_KV_WIDTH, GQA_KV_WIDTH, MLA_Q_LORA, MLA_KV_LORA, MLA_ROPE_DIM, SSD_INNER, SSD_CONV_DIM,
             2 * SSD_HEADS)
IN_COLS = sum(IN_SPLITS)
LANES = 128
N_CHIPS = 4
IN_SHARD = IN_COLS // N_CHIPS
IN_SHARD_PAD = -(-IN_SHARD // LANES) * LANES


def _in_cols(proj, lo, hi):
    parts = []
    for chip in range(lo // IN_SHARD, (hi - 1) // IN_SHARD + 1):
        a, z = max(lo, chip * IN_SHARD), min(hi, (chip + 1) * IN_SHARD)
        base = chip * IN_SHARD_PAD - chip * IN_SHARD
        parts.append(proj[:, base + a:base + z])
    return parts[0] if len(parts) == 1 else jnp.concatenate(parts, axis=-1)

ADAM_LR, ADAM_B1, ADAM_B2, ADAM_EPS, ADAM_WD, ADAM_STEP = 0.001, 0.9, 0.999, 1e-08, 0.01, 10

N_DEV = 8
TILE_BYTES = 2 * 1024 * 1024


def _pick(n, cands):
    for t in cands:
        if n % t == 0:
            return t
    return n


ADAM_TILE_BYTES = 2 * 1024 * 1024
ADAM_VMEM_LIMIT = 48 * 1024 * 1024
SWIGLU_TILE_BYTES = 4 * 1024 * 1024


def _row_tile(rows, row_bytes, limit=TILE_BYTES):
    for t in (2048, 1024, 512, 256, 128, 64, 32, 16, 8):
        if rows % t == 0 and t * row_bytes <= limit:
            return t
    return rows


MM_VMEM_BUDGET = 36 * 1024 * 1024
MM_VMEM_LIMIT = 56 * 1024 * 1024
MM_MAX_TILE = 2048
MM_MAX_K_TILE = 4096
MXU_DIM = 256
HBM_BYTES_PER_US = 3.0e6
MXU_FLOPS_PER_US = 9.0e8
STEP_US = 0.35


def _tile_cands(d, cap):
    if d % LANES:
        return [d]
    return [t for t in range(LANES, min(d, cap) + 1, LANES) if d % t == 0] or [d]


def _mm_tiles(m, n, kdim, n_unit=None, k_unit=None):
    up = lambda t: -(-t // MXU_DIM) * MXU_DIM
    best = None
    for tm in _tile_cands(m, MM_MAX_TILE):
        for tn in _tile_cands(n_unit or n, MM_MAX_TILE):
            for tk in _tile_cands(k_unit or kdim, MM_MAX_K_TILE):
                if 2 * (tm * tk * 2 + tk * tn * 2 + tm * tn * 4) > MM_VMEM_BUDGET:
                    continue
                ni, nj, nk = m // tm, n // tn, kdim // tk
                a_reads = 1 if nk == 1 else nj
                b_reads = 1 if (nk == 1 and nj == 1) else ni
                hbm = (m * kdim * 2 * a_reads + kdim * n * 2 * b_reads + m * n * 4) / HBM_BYTES_PER_US
                mxu = ni * nj * nk * 2.0 * max(tm, 8) * up(tn) * up(tk) / MXU_FLOPS_PER_US
                cost = max(hbm, mxu) + 0.25 * min(hbm, mxu) + ni * nj * nk * STEP_US
                if best is None or cost < best[0]:
                    best = (cost, tm, tn, tk)
    return best[1:]


def _ride(plan, refs, first, last, compute):
    if plan is None:
        compute()
        return
    ins, outs, sems = plan.split(refs)

    @pl.when(first)
    def _():
        plan.start(ins, outs, sems)

    compute()

    @pl.when(last)
    def _():
        plan.finish(ins, outs, sems)


def _plan_specs(plan):
    if plan is None:
        return [], [], [], [], []
    in_specs, out_specs, scratch = plan.specs()
    return in_specs, out_specs, plan.out_shapes, scratch, plan.inputs


def _matmul(a, b, ta=False, tb=False, name="mm", plan=None, chips=None):
    assert a.dtype == BF16 and b.dtype in (BF16, F32), (a.dtype, b.dtype)
    if ta:
        kdim, m = a.shape
    else:
        m, kdim = a.shape
    n_unit = k_unit = None
    if chips == 'b':
        nb, rows, unit = b.shape
        if tb:
            n, k2, k_unit = rows, nb * unit, unit
        else:
            k2, n, n_unit = rows, nb * unit, unit
    else:
        if tb:
            n, k2 = b.shape
        else:
            k2, n = b.shape
        if chips == 'out':
            n_unit = n // N_CHIPS
    assert kdim == k2, (a.shape, b.shape, ta, tb)
    tm, tn, tk = _mm_tiles(m, n, kdim, n_unit, k_unit)
    ni, nj, nk = m // tm, n // tn, kdim // tk
    dn = (((0 if ta else 1,), (1 if tb else 0,)), ((), ()))
    p_in, p_out, p_shapes, p_scratch, p_args = _plan_specs(plan)

    def body(a_ref, b_ref, *rest):
        o_ref = rest[len(p_in)]
        i, j, k = pl.program_id(0), pl.program_id(1), pl.program_id(2)

        def compute():
            bv = (b_ref[0] if chips == 'b' else b_ref[...]).astype(BF16)
            part = lax.dot_general(a_ref[...], bv, dn, preferred_element_type=F32)
            if chips == 'out':
                part = part[None]
            if nk == 1:
                o_ref[...] = part
            else:
                @pl.when(k == 0)
                def _():
                    o_ref[...] = part

                @pl.when(k > 0)
                def _():
                    o_ref[...] += part

        _ride(plan, rest[:len(p_in)] + rest[len(p_in) + 1:], (i == 0) & (j == 0) & (k == 0),
              (i == ni - 1) & (j == nj - 1) & (k == nk - 1), compute)

    a_spec = pl.BlockSpec((tk, tm), lambda i, j, k: (k, i)) if ta else pl.BlockSpec((tm, tk), lambda i, j, k: (i, k))
    if chips == 'b' and tb:
        per = k_unit // tk
        b_spec = pl.BlockSpec((1, tn, tk), lambda i, j, k: (k // per, j, k % per))
    elif chips == 'b':
        per = n_unit // tn
        b_spec = pl.BlockSpec((1, tk, tn), lambda i, j, k: (j // per, k, j % per))
    else:
        b_spec = pl.BlockSpec((tn, tk), lambda i, j, k: (j, k)) if tb else pl.BlockSpec((tk, tn), lambda i, j, k: (k, j))
    if chips == 'out':
        per = n_unit // tn
        o_spec = pl.BlockSpec((1, tm, tn), lambda i, j, k: (j // per, i, j % per))
        o_shape = jax.ShapeDtypeStruct((N_CHIPS, m, n_unit), F32)
    else:
        o_spec = pl.BlockSpec((tm, tn), lambda i, j, k: (i, j))
        o_shape = jax.ShapeDtypeStruct((m, n), F32)
    outs = pl.pallas_call(
        body, name=name, grid=(ni, nj, nk),
        in_specs=[a_spec, b_spec] + p_in, out_specs=[o_spec] + p_out,
        out_shape=[o_shape] + p_shapes, scratch_shapes=p_scratch,
        compiler_params=pltpu.CompilerParams(
            dimension_semantics=("arbitrary" if plan is not None else "parallel", "arbitrary", "arbitrary"),
            vmem_limit_bytes=MM_VMEM_LIMIT),
    )(a, b, *p_args)
    return outs[0] if plan is None else (outs[0], list(outs[1:]))


@jax.custom_vjp
def linear(x, w):
    return _matmul(x.astype(BF16), w.astype(BF16), name="linear_fwd")


def _linear_fwd(x, w):
    xb, wb = x.astype(BF16), w.astype(BF16)
    return _matmul(xb, wb, name="linear_fwd"), (xb, wb)


def _linear_bwd(res, dy):
    xb, wb = res
    dyb = dy.astype(BF16)
    return _matmul(dyb, wb, tb=True, name="linear_dx"), _matmul(xb, dyb, ta=True, name="linear_dw")


linear.defvjp(_linear_fwd, _linear_bwd)


def _rms_fwd_call(x, g, groups):
    rows, cols = x.shape
    d = cols // groups
    tr = _row_tile(rows, cols * 4)

    def body(x_ref, g_ref, y_ref):
        for gi in range(groups):
            sl = slice(gi * d, (gi + 1) * d)
            xs = x_ref[:, sl]
            r = lax.rsqrt(jnp.mean(xs * xs, axis=-1, keepdims=True) + EPS)
            y_ref[:, sl] = xs * r * g_ref[:, sl]

    return pl.pallas_call(
        body, name="rms_fwd", grid=(rows // tr,),
        in_specs=[pl.BlockSpec((tr, cols), lambda i: (i, 0)), pl.BlockSpec((1, cols), lambda i: (0, 0))],
        out_specs=pl.BlockSpec((tr, cols), lambda i: (i, 0)),
        out_shape=jax.ShapeDtypeStruct((rows, cols), F32),
        compiler_params=pltpu.CompilerParams(dimension_semantics=("parallel",)),
    )(x, g)


def _rms_bwd_call(x, g, dy, groups):
    rows, cols = x.shape
    d = cols // groups
    tr = _row_tile(rows, cols * 4)

    def body(x_ref, g_ref, dy_ref, dx_ref, dg_ref):
        @pl.when(pl.program_id(0) == 0)
        def _():
            dg_ref[...] = jnp.zeros_like(dg_ref)

        for gi in range(groups):
            sl = slice(gi * d, (gi + 1) * d)
            xs = x_ref[:, sl]
            dys = dy_ref[:, sl]
            r = lax.rsqrt(jnp.mean(xs * xs, axis=-1, keepdims=True) + EPS)
            xhat = xs * r
            dg_ref[:, sl] += jnp.sum(dys * xhat, axis=0, keepdims=True)
            dxhat = dys * g_ref[:, sl]
            dx_ref[:, sl] = r * (dxhat - xhat * jnp.mean(dxhat * xhat, axis=-1, keepdims=True))

    return pl.pallas_call(
        body, name="rms_bwd", grid=(rows // tr,),
        in_specs=[pl.BlockSpec((tr, cols), lambda i: (i, 0)), pl.BlockSpec((1, cols), lambda i: (0, 0)),
                  pl.BlockSpec((tr, cols), lambda i: (i, 0))],
        out_specs=[pl.BlockSpec((tr, cols), lambda i: (i, 0)), pl.BlockSpec((1, cols), lambda i: (0, 0))],
        out_shape=[jax.ShapeDtypeStruct((rows, cols), F32), jax.ShapeDtypeStruct((1, cols), F32)],
        compiler_params=pltpu.CompilerParams(dimension_semantics=("arbitrary",)),
    )(x, g, dy)


@functools.partial(jax.custom_vjp, nondiff_argnums=(2,))
def rms_norm(x, g, groups):
    return _rms_fwd_call(x, g, groups)


def _rms_norm_fwd(x, g, groups):
    return _rms_fwd_call(x, g, groups), (x, g)


def _rms_norm_bwd(groups, res, dy):
    x, g = res
    dx, dg = _rms_bwd_call(x, g, dy, groups)
    return dx, dg


rms_norm.defvjp(_rms_norm_fwd, _rms_norm_bwd)


NT_DIMS = (((1,), (1,)), ((), ()))
TN_DIMS = (((0,), (0,)), ((), ()))


LOG2E = 1.4426950408889634
ATTN_VMEM_LIMIT = 60 * 1024 * 1024
ATTN_SUB_ROWS = 256

def _exp_rows(q, k, scale):
    s2 = lax.dot_general(q, k, NT_DIMS, preferred_element_type=F32) * (scale * LOG2E)
    e = jnp.exp2(s2 - jnp.max(s2, axis=-1, keepdims=True))
    return e, 1.0 / jnp.sum(e, axis=-1, keepdims=True)


def _attn_fwd_call(q, k, v, scale):
    b, h, s, dk = q.shape
    hkv, dv = k.shape[1], v.shape[3]
    rep = h // hkv
    tq = _pick(s, (1024, 512, 256, 128))

    sub = min(ATTN_SUB_ROWS, tq)

    def body(q_ref, k_ref, v_ref, o_ref):
        kb, vb = k_ref[0, 0], v_ref[0, 0]
        for r in range(tq // sub):
            rows = pl.ds(r * sub, sub)
            e, inv = _exp_rows(q_ref[0, 0, rows, :], kb, scale)
            o_ref[0, 0, rows, :] = jnp.dot(e.astype(BF16), vb, preferred_element_type=F32) * inv

    return pl.pallas_call(
        body, name="attn_fwd", grid=(b, h, s // tq),
        in_specs=[pl.BlockSpec((1, 1, tq, dk), lambda bi, hi, qi: (bi, hi, qi, 0)),
                  pl.BlockSpec((1, 1, s, dk), lambda bi, hi, qi: (bi, hi // rep, 0, 0)),
                  pl.BlockSpec((1, 1, s, dv), lambda bi, hi, qi: (bi, hi // rep, 0, 0))],
        out_specs=pl.BlockSpec((1, 1, tq, dv), lambda bi, hi, qi: (bi, hi, qi, 0)),
        out_shape=jax.ShapeDtypeStruct((b, h, s, dv), F32),
        compiler_params=pltpu.CompilerParams(dimension_semantics=("parallel", "parallel", "parallel"),
                                             vmem_limit_bytes=ATTN_VMEM_LIMIT),
    )(q, k, v)


def _attn_bwd_call(q, k, v, do, scale):
    b, h, s, dk = q.shape
    hkv, dv = k.shape[1], v.shape[3]
    rep = h // hkv
    tq = _pick(s, (1024, 512, 256, 128))
    sub = min(ATTN_SUB_ROWS, tq)

    def body(q_ref, k_ref, v_ref, do_ref, dq_ref, dk_ref, dv_ref):
        @pl.when((pl.program_id(2) == 0) & (pl.program_id(3) == 0))
        def _():
            dk_ref[...] = jnp.zeros_like(dk_ref)
            dv_ref[...] = jnp.zeros_like(dv_ref)

        kb = k_ref[0, 0]
        vb = v_ref[0, 0]
        ds_parts, e_parts, do_parts = [], [], []
        for r in range(tq // sub):
            rows = pl.ds(r * sub, sub)
            dob = do_ref[0, 0, rows, :]
            e, inv = _exp_rows(q_ref[0, 0, rows, :], kb, scale)
            dp = lax.dot_general(dob, vb, NT_DIMS, preferred_element_type=F32)
            delta = jnp.sum(e * dp, axis=-1, keepdims=True) * inv
            ds = (e * ((dp - delta) * (inv * scale))).astype(BF16)
            dq_ref[0, 0, rows, :] = jnp.dot(ds, kb, preferred_element_type=F32)
            ds_parts.append(ds)
            e_parts.append(e.astype(BF16))
            do_parts.append((dob.astype(F32) * inv).astype(BF16))
        join = lambda parts: parts[0] if len(parts) == 1 else jnp.concatenate(parts, axis=0)
        dk_ref[0, 0] += lax.dot_general(join(ds_parts), q_ref[0, 0], TN_DIMS, preferred_element_type=F32)
        dv_ref[0, 0] += lax.dot_general(join(e_parts), join(do_parts), TN_DIMS, preferred_element_type=F32)

    return pl.pallas_call(
        body, name="attn_bwd", grid=(b, hkv, rep, s // tq),
        in_specs=[pl.BlockSpec((1, 1, tq, dk), lambda bi, gi, ri, qi: (bi, gi * rep + ri, qi, 0)),
                  pl.BlockSpec((1, 1, s, dk), lambda bi, gi, ri, qi: (bi, gi, 0, 0)),
                  pl.BlockSpec((1, 1, s, dv), lambda bi, gi, ri, qi: (bi, gi, 0, 0)),
                  pl.BlockSpec((1, 1, tq, dv), lambda bi, gi, ri, qi: (bi, gi * rep + ri, qi, 0))],
        out_specs=[pl.BlockSpec((1, 1, tq, dk), lambda bi, gi, ri, qi: (bi, gi * rep + ri, qi, 0)),
                   pl.BlockSpec((1, 1, s, dk), lambda bi, gi, ri, qi: (bi, gi, 0, 0)),
                   pl.BlockSpec((1, 1, s, dv), lambda bi, gi, ri, qi: (bi, gi, 0, 0))],
        out_shape=[jax.ShapeDtypeStruct(q.shape, F32), jax.ShapeDtypeStruct(k.shape, F32),
                   jax.ShapeDtypeStruct(v.shape, F32)],
        compiler_params=pltpu.CompilerParams(
            dimension_semantics=("parallel", "parallel", "arbitrary", "arbitrary"), vmem_limit_bytes=ATTN_VMEM_LIMIT),
    )(q, k, v, do)


@functools.partial(jax.custom_vjp, nondiff_argnums=(3,))
def attention(q, k, v, scale):
    return _attn_fwd_call(q.astype(BF16), k.astype(BF16), v.astype(BF16), scale)


def _attention_fwd(q, k, v, scale):
    qb, kb, vb = q.astype(BF16), k.astype(BF16), v.astype(BF16)
    return _attn_fwd_call(qb, kb, vb, scale), (qb, kb, vb)


def _attention_bwd(scale, res, do):
    qb, kb, vb = res
    return tuple(_attn_bwd_call(qb, kb, vb, do.astype(BF16), scale))


attention.defvjp(_attention_fwd, _attention_bwd)


CONV_COL_TILE = 256
CONV_PACK_ROWS = 8


def _shifted(x, off, rows):
    if off == 0:
        return x
    s = x.shape[0]
    rolled = pltpu.roll(x, (-off) % s, 0)
    valid = (rows + off >= 0) & (rows + off < s)
    return jnp.where(valid, rolled, 0.0)


def _conv_pre(x, wb_ref, rows):
    z = jnp.zeros_like(x) + wb_ref[SSD_CONV:SSD_CONV + 1, :]
    for j in range(SSD_CONV):
        z = z + wb_ref[j:j + 1, :] * _shifted(x, j - SSD_CONV // 2, rows)
    return z


def _conv_fwd_call(x, wb):
    b, s, c = x.shape
    tc = _pick(c, (CONV_COL_TILE, LANES))

    def body(x_ref, wb_ref, y_ref):
        xv = x_ref[0]
        rows = lax.broadcasted_iota(jnp.int32, xv.shape, 0)
        z = _conv_pre(xv, wb_ref, rows)
        y_ref[0] = z * jax.nn.sigmoid(z)

    return pl.pallas_call(
        body, name="conv_fwd", grid=(b, c // tc),
        in_specs=[pl.BlockSpec((1, s, tc), lambda bi, ci: (bi, 0, ci)),
                  pl.BlockSpec((CONV_PACK_ROWS, tc), lambda bi, ci: (0, ci))],
        out_specs=pl.BlockSpec((1, s, tc), lambda bi, ci: (bi, 0, ci)),
        out_shape=jax.ShapeDtypeStruct(x.shape, F32),
        compiler_params=pltpu.CompilerParams(dimension_semantics=("parallel", "parallel")),
    )(x, wb)


def _conv_bwd_call(x, wb, dy):
    b, s, c = x.shape
    tc = _pick(c, (CONV_COL_TILE, LANES))

    def body(x_ref, wb_ref, dy_ref, dx_ref, dwb_ref):
        xv = x_ref[0]
        rows = lax.broadcasted_iota(jnp.int32, xv.shape, 0)
        z = _conv_pre(xv, wb_ref, rows)
        sg = jax.nn.sigmoid(z)
        dz = dy_ref[0] * (sg * (1.0 + z * (1.0 - sg)))
        dx = jnp.zeros_like(xv)
        for j in range(SSD_CONV):
            off = j - SSD_CONV // 2
            dx = dx + wb_ref[j:j + 1, :] * _shifted(dz, -off, rows)
            dwb_ref[0, j:j + 1, :] = jnp.sum(dz * _shifted(xv, off, rows), axis=0, keepdims=True)
        dx_ref[0] = dx
        dwb_ref[0, SSD_CONV:SSD_CONV + 1, :] = jnp.sum(dz, axis=0, keepdims=True)
        dwb_ref[0, SSD_CONV + 1:, :] = jnp.zeros((CONV_PACK_ROWS - SSD_CONV - 1, dz.shape[1]), F32)

    return pl.pallas_call(
        body, name="conv_bwd", grid=(b, c // tc),
        in_specs=[pl.BlockSpec((1, s, tc), lambda bi, ci: (bi, 0, ci)),
                  pl.BlockSpec((CONV_PACK_ROWS, tc), lambda bi, ci: (0, ci)),
                  pl.BlockSpec((1, s, tc), lambda bi, ci: (bi, 0, ci))],
        out_specs=[pl.BlockSpec((1, s, tc), lambda bi, ci: (bi, 0, ci)),
                   pl.BlockSpec((1, CONV_PACK_ROWS, tc), lambda bi, ci: (bi, 0, ci))],
        out_shape=[jax.ShapeDtypeStruct(x.shape, F32), jax.ShapeDtypeStruct((b, CONV_PACK_ROWS, c), F32)],
        compiler_params=pltpu.CompilerParams(dimension_semantics=("parallel", "parallel")),
    )(x, wb, dy)


@jax.custom_vjp
def conv_silu(x, wb):
    return _conv_fwd_call(x, wb)


def _conv_silu_fwd(x, wb):
    return _conv_fwd_call(x, wb), (x, wb)


def _conv_silu_bwd(res, dy):
    x, wb = res
    dx, dwb = _conv_bwd_call(x, wb, dy)
    return dx, jnp.sum(dwb, axis=0)


conv_silu.defvjp(_conv_silu_fwd, _conv_silu_bwd)


SSD_PAIRS = SSD_GROUP_HEADS // 2
NEG_INF = -1e30


def _ssd_common(x_ref, dtx_ref, dtt_ref, anx_ref, anc_ref, b_ref, c_ref, reverse):
    L = SSD_CHUNK
    xv = x_ref[0]
    dt = dtx_ref[0]
    ri = lax.broadcasted_iota(jnp.int32, (L, L), 0)
    ci = lax.broadcasted_iota(jnp.int32, (L, L), 1)
    causal = (ri <= ci) if reverse else (ri >= ci)
    tri = causal.astype(F32)
    a_cs = jnp.dot(tri, dt * anx_ref[...], precision=HIGHEST, preferred_element_type=F32)
    a_row = dtt_ref[0, 0] * anc_ref[0]
    acs_row = lax.dot_general(a_row, tri, NT_DIMS, precision=HIGHEST, preferred_element_type=F32)
    xd = xv * dt
    bmat = b_ref[0].astype(BF16)
    cmat = c_ref[0].astype(BF16)
    gmat = lax.dot_general(cmat, bmat, NT_DIMS, preferred_element_type=F32)
    return xv, dt, causal, tri, a_cs, acs_row, xd, bmat, cmat, gmat


def _ssd_lambda(a_cs, acs_row, causal, h):
    col = a_cs[:, h * SSD_HEAD_DIM:h * SSD_HEAD_DIM + 1]
    row = acs_row[h:h + 1, :]
    return jnp.exp(jnp.where(causal, col - row, NEG_INF))


def _ssd_fwd_call(x, dtx, dtt, anx, anc, bm, cm, reverse):
    b, s, _ = x.shape
    L, N, GW = SSD_CHUNK, SSD_STATE, SSD_GROUP_WIDTH
    nc = s // L
    end = 0 if reverse else L - 1

    def body(x_ref, dtx_ref, dtt_ref, anx_ref, anc_ref, b_ref, c_ref, y_ref, hs_ref, state):
        @pl.when(pl.program_id(2) == 0)
        def _():
            state[...] = jnp.zeros_like(state)

        xv, dt, causal, tri, a_cs, acs_row, xd, bmat, cmat, gmat = _ssd_common(
            x_ref, dtx_ref, dtt_ref, anx_ref, anc_ref, b_ref, c_ref, reverse)
        hin = state[...]
        hs_ref[0, 0, 0] = hin
        y_off = jnp.dot(cmat, hin.astype(BF16), preferred_element_type=F32) * jnp.exp(a_cs)
        a_end = a_cs[end:end + 1, :]
        s_new = lax.dot_general(bmat, (xd * jnp.exp(a_end - a_cs)).astype(BF16), TN_DIMS, preferred_element_type=F32)
        state[...] = jnp.exp(a_end) * hin + s_new
        lane = lax.broadcasted_iota(jnp.int32, (L, LANES), 1)
        for pr in range(SSD_PAIRS):
            sl = slice(pr * LANES, (pr + 1) * LANES)
            xdp = xd[:, sl].astype(BF16)
            w0 = (gmat * _ssd_lambda(a_cs, acs_row, causal, 2 * pr)).astype(BF16)
            w1 = (gmat * _ssd_lambda(a_cs, acs_row, causal, 2 * pr + 1)).astype(BF16)
            y0 = jnp.dot(w0, xdp, preferred_element_type=F32)
            y1 = jnp.dot(w1, xdp, preferred_element_type=F32)
            y_ref[0, :, sl] = jnp.where(lane < SSD_HEAD_DIM, y0, y1) + y_off[:, sl]

    G = SSD_GROUPS
    chunk = (lambda c: nc - 1 - c) if reverse else (lambda c: c)
    seq = lambda bi, gi, c: (bi, chunk(c), gi)
    return pl.pallas_call(
        body, name="ssd_fwd", grid=(b, G, nc),
        in_specs=[pl.BlockSpec((1, L, GW), seq),
                  pl.BlockSpec((1, L, GW), seq),
                  pl.BlockSpec((1, 1, SSD_GROUP_HEADS, L), lambda bi, gi, c: (bi, gi, 0, chunk(c))),
                  pl.BlockSpec((1, GW), lambda bi, gi, c: (0, gi)),
                  pl.BlockSpec((1, SSD_GROUP_HEADS, 1), lambda bi, gi, c: (gi, 0, 0)),
                  pl.BlockSpec((1, L, N), seq),
                  pl.BlockSpec((1, L, N), seq)],
        out_specs=[pl.BlockSpec((1, L, GW), seq),
                   pl.BlockSpec((1, 1, 1, N, GW), lambda bi, gi, c: (bi, gi, chunk(c), 0, 0))],
        out_shape=[jax.ShapeDtypeStruct(x.shape, F32), jax.ShapeDtypeStruct((b, G, nc, N, GW), F32)],
        scratch_shapes=[pltpu.VMEM((N, GW), F32)],
        compiler_params=pltpu.CompilerParams(dimension_semantics=("parallel", "parallel", "arbitrary")),
    )(x, dtx, dtt, anx, anc, bm, cm)


def _ssd_bwd_call(x, dtx, dtt, anx, anc, bm, cm, hs, dy, reverse):
    b, s, _ = x.shape
    L, N, GW = SSD_CHUNK, SSD_STATE, SSD_GROUP_WIDTH
    nc = s // L
    end = 0 if reverse else L - 1

    def body(x_ref, dtx_ref, dtt_ref, anx_ref, anc_ref, b_ref, c_ref, hs_ref, dy_ref,
             dx_ref, ddt_ref, dan_ref, db_ref, dc_ref, dstate):
        @pl.when(pl.program_id(2) == 0)
        def _():
            dstate[...] = jnp.zeros_like(dstate)

        xv, dt, causal, tri, a_cs, acs_row, xd, bmat, cmat, gmat = _ssd_common(
            x_ref, dtx_ref, dtt_ref, anx_ref, anc_ref, b_ref, c_ref, reverse)
        hin = hs_ref[0, 0, 0]
        hinb = hin.astype(BF16)
        dyv = dy_ref[0]
        ds_out = dstate[...]
        dsb = ds_out.astype(BF16)
        eacs = jnp.exp(a_cs)
        a_end = a_cs[end:end + 1, :]
        e_end = jnp.exp(a_end)
        dec = jnp.exp(a_end - a_cs)
        dye = dyv * eacs
        dyeb = dye.astype(BF16)
        xdec = xd * dec
        ch = jnp.dot(cmat, hinb, preferred_element_type=F32)
        bds = jnp.dot(bmat, dsb, preferred_element_type=F32)
        t_state = xdec * bds
        d_aend = jnp.sum(t_state, axis=0, keepdims=True) + e_end * jnp.sum(ds_out * hin, axis=0, keepdims=True)
        dacs = dye * ch - t_state
        dxd_state = bds * dec
        dstate[...] = e_end * ds_out + lax.dot_general(cmat, dyeb, TN_DIMS, preferred_element_type=F32)

        lane = lax.broadcasted_iota(jnp.int32, (L, LANES), 1)
        dg = jnp.zeros((L, L), F32)
        dxd_parts, dacs_parts = [], []
        for pr in range(SSD_PAIRS):
            sl = slice(pr * LANES, (pr + 1) * LANES)
            xdp = xd[:, sl]
            dyp = dyv[:, sl]
            dxd_p = jnp.zeros((L, LANES), F32)
            dacs_p = jnp.zeros((L, LANES), F32)
            for half in range(2):
                mine = (lane < SSD_HEAD_DIM) if half == 0 else (lane >= SSD_HEAD_DIM)
                lam = _ssd_lambda(a_cs, acs_row, causal, 2 * pr + half)
                w = gmat * lam
                xdh = jnp.where(mine, xdp, 0.0).astype(BF16)
                dyh = jnp.where(mine, dyp, 0.0).astype(BF16)
                dw = lax.dot_general(dyh, xdh, NT_DIMS, preferred_element_type=F32)
                dg = dg + dw * lam
                mm = dw * w
                rs = jnp.sum(mm, axis=1, keepdims=True)
                cs = jnp.sum(mm.T, axis=1, keepdims=True)
                dacs_p = dacs_p + jnp.where(mine, (rs - cs) * (1.0 / SSD_HEAD_DIM), 0.0)
                wtdy = lax.dot_general(w.astype(BF16), dyh, TN_DIMS, preferred_element_type=F32)
                dxd_p = dxd_p + wtdy
            dxd_parts.append(dxd_p)
            dacs_parts.append(dacs_p)
        dxd = jnp.concatenate(dxd_parts, axis=1) + dxd_state
        dacs = dacs + jnp.concatenate(dacs_parts, axis=1)
        last = lax.broadcasted_iota(jnp.int32, dacs.shape, 0) == end
        dacs = dacs + jnp.where(last, d_aend, 0.0)
        da = lax.dot_general(tri, dacs, TN_DIMS, precision=HIGHEST, preferred_element_type=F32)
        dgb = dg.astype(BF16)
        dc_ref[0] = (jnp.dot(dgb, bmat, preferred_element_type=F32)
                     + lax.dot_general(dyeb, hinb, NT_DIMS, preferred_element_type=F32))
        db_ref[0] = (lax.dot_general(dgb, cmat, TN_DIMS, preferred_element_type=F32)
                     + lax.dot_general(xdec.astype(BF16), dsb, NT_DIMS, preferred_element_type=F32))
        dx_ref[0] = dxd * dt
        ddt_ref[0] = da * anx_ref[...] + dxd * xv
        dan_ref[0, 0, 0] = jnp.sum(da * dt, axis=0, keepdims=True)

    G = SSD_GROUPS
    chunk = (lambda c: c) if reverse else (lambda c: nc - 1 - c)
    rev = lambda bi, gi, c: (bi, chunk(c), gi)
    return pl.pallas_call(
        body, name="ssd_bwd", grid=(b, G, nc),
        in_specs=[pl.BlockSpec((1, L, GW), rev),
                  pl.BlockSpec((1, L, GW), rev),
                  pl.BlockSpec((1, 1, SSD_GROUP_HEADS, L), lambda bi, gi, c: (bi, gi, 0, chunk(c))),
                  pl.BlockSpec((1, GW), lambda bi, gi, c: (0, gi)),
                  pl.BlockSpec((1, SSD_GROUP_HEADS, 1), lambda bi, gi, c: (gi, 0, 0)),
                  pl.BlockSpec((1, L, N), rev),
                  pl.BlockSpec((1, L, N), rev),
                  pl.BlockSpec((1, 1, 1, N, GW), lambda bi, gi, c: (bi, gi, chunk(c), 0, 0)),
                  pl.BlockSpec((1, L, GW), rev)],
        out_specs=[pl.BlockSpec((1, L, GW), rev),
                   pl.BlockSpec((1, L, GW), rev),
                   pl.BlockSpec((1, 1, 1, 1, GW), lambda bi, gi, c: (bi, gi, chunk(c), 0, 0)),
                   pl.BlockSpec((1, L, N), rev),
                   pl.BlockSpec((1, L, N), rev)],
        out_shape=[jax.ShapeDtypeStruct(x.shape, F32), jax.ShapeDtypeStruct(x.shape, F32),
                   jax.ShapeDtypeStruct((b, G, nc, 1, GW), F32),
                   jax.ShapeDtypeStruct(bm.shape, F32), jax.ShapeDtypeStruct(cm.shape, F32)],
        scratch_shapes=[pltpu.VMEM((N, GW), F32)],
        compiler_params=pltpu.CompilerParams(dimension_semantics=("parallel", "parallel", "arbitrary")),
    )(x, dtx, dtt, anx, anc, bm, cm, hs, dy)


@functools.partial(jax.custom_vjp, nondiff_argnums=(7,))
def _ssd_scan(x, dtx, dtt, anx, anc, bm, cm, reverse):
    return _ssd_fwd_call(x, dtx, dtt, anx, anc, bm, cm, reverse)[0]


def _ssd_scan_fwd(x, dtx, dtt, anx, anc, bm, cm, reverse):
    y, hs = _ssd_fwd_call(x, dtx, dtt, anx, anc, bm, cm, reverse)
    return y, (x, dtx, dtt, anx, anc, bm, cm, hs)


def _ssd_scan_bwd(reverse, res, dy):
    x, dtx, dtt, anx, anc, bm, cm, hs = res
    dx, ddtx, dan, db, dc = _ssd_bwd_call(x, dtx, dtt, anx, anc, bm, cm, hs, dy, reverse)
    b, g, nc, _, gw = dan.shape
    danx = jnp.sum(dan, axis=(0, 2, 3)).reshape(1, g * gw)
    return dx, ddtx, jnp.zeros_like(dtt), danx, jnp.zeros_like(anc), db, dc


_ssd_scan.defvjp(_ssd_scan_fwd, _ssd_scan_bwd)


def ssd_chunked(xs, dt, a_neg, bm, cm, reverse):
    b, s, _ = xs.shape
    dtx = jnp.repeat(dt, SSD_HEAD_DIM, axis=-1)
    dtt = jnp.transpose(dt, (0, 2, 1)).reshape(b, SSD_GROUPS, SSD_GROUP_HEADS, s)
    anx = jnp.repeat(a_neg, SSD_HEAD_DIM)[None, :]
    anc = a_neg.reshape(SSD_GROUPS, SSD_GROUP_HEADS, 1)
    return _ssd_scan(xs, dtx, dtt, anx, anc, bm, cm, reverse)


def _loss_call(y, t):
    rows, cols = y.shape
    tr = _row_tile(rows, cols * 4)

    def body(y_ref, t_ref, loss_ref, diff_ref):
        @pl.when(pl.program_id(0) == 0)
        def _():
            loss_ref[...] = jnp.zeros_like(loss_ref)

        d = y_ref[...] - t_ref[...]
        diff_ref[...] = d * (1.0 / cols)
        part = jnp.sum(jnp.sum(d * d, axis=1, keepdims=True), axis=0, keepdims=True)
        loss_ref[...] += part * (0.5 / cols)

    return pl.pallas_call(
        body, name="loss_head", grid=(rows // tr,),
        in_specs=[pl.BlockSpec((tr, cols), lambda i: (i, 0)), pl.BlockSpec((tr, cols), lambda i: (i, 0))],
        out_specs=[pl.BlockSpec((1, 1), lambda i: (0, 0)), pl.BlockSpec((tr, cols), lambda i: (i, 0))],
        out_shape=[jax.ShapeDtypeStruct((1, 1), F32), jax.ShapeDtypeStruct((rows, cols), F32)],
        compiler_params=pltpu.CompilerParams(dimension_semantics=("arbitrary",)),
    )(y, t)


@jax.custom_vjp
def loss_head(y, t):
    return _loss_call(y, t)[0][0, 0]


def _loss_head_fwd(y, t):
    loss, diff = _loss_call(y, t)
    return loss[0, 0], diff


def _loss_head_bwd(diff, g):
    return g * diff, jnp.zeros_like(diff)


loss_head.defvjp(_loss_head_fwd, _loss_head_bwd)


def _axial_rope_tables(seq_len, rot_dim):
    rows = seq_len // GRID_W
    row_idx = jnp.repeat(jnp.arange(rows), GRID_W).astype(F32)
    col_idx = jnp.tile(jnp.arange(GRID_W), rows).astype(F32)
    axis_dim = rot_dim // 2
    inv_freq = jnp.power(ROPE_THETA, -jnp.arange(0, axis_dim, 2, dtype=F32) / axis_dim)
    ang_r = row_idx[:, None] * inv_freq[None, :]
    ang_c = col_idx[:, None] * inv_freq[None, :]
    return jnp.cos(ang_r), jnp.sin(ang_r), jnp.cos(ang_c), jnp.sin(ang_c)


def _rotate(x, cos, sin):
    x1, x2 = jnp.split(x, 2, axis=-1)
    cos = cos[:, None, :]
    sin = sin[:, None, :]
    return jnp.concatenate([x1 * cos - x2 * sin, x1 * sin + x2 * cos], axis=-1)


def _apply_axial_rope(x, tables):
    cos_r, sin_r, cos_c, sin_c = tables
    x_row, x_col = jnp.split(x, 2, axis=-1)
    return jnp.concatenate([_rotate(x_row, cos_r, sin_r), _rotate(x_col, cos_c, sin_c)], axis=-1)


def _heads_first(t):
    return jnp.transpose(t, (0, 2, 1, 3))


def _gqa_group(q, k, v, q_norm_g, k_norm_g, rope, b, s):
    q = rms_norm(q, jnp.tile(q_norm_g, GQA_HEADS)[None, :], GQA_HEADS).reshape(b, s, GQA_HEADS, GQA_HEAD_DIM)
    k = rms_norm(k, jnp.tile(k_norm_g, GQA_KV_HEADS)[None, :], GQA_KV_HEADS).reshape(b, s, GQA_KV_HEADS, GQA_HEAD_DIM)
    v = v.reshape(b, s, GQA_KV_HEADS, GQA_HEAD_DIM)
    q = _apply_axial_rope(q, rope)
    k = _apply_axial_rope(k, rope)
    o = attention(_heads_first(q), _heads_first(k), _heads_first(v), GQA_HEAD_DIM ** -0.5)
    return _heads_first(o).reshape(b * s, GQA_WIDTH)


def _mla_group(c_q, c_kv, k_pe, q_norm_g, w_uq, kv_norm_g, w_ukv, rope, b, s):
    q = linear(rms_norm(c_q, q_norm_g[None, :], 1), w_uq).reshape(b, s, MLA_HEADS, MLA_NOPE_DIM + MLA_ROPE_DIM)
    q_nope, q_pe = q[..., :MLA_NOPE_DIM], q[..., MLA_NOPE_DIM:]
    kv = linear(rms_norm(c_kv, kv_norm_g[None, :], 1), w_ukv).reshape(b, s, MLA_HEADS, MLA_NOPE_DIM + MLA_V_DIM)
    k_nope, v = kv[..., :MLA_NOPE_DIM], kv[..., MLA_NOPE_DIM:]
    q_pe = _apply_axial_rope(q_pe, rope)
    k_pe = _apply_axial_rope(k_pe.reshape(b, s, 1, MLA_ROPE_DIM), rope)
    q = jnp.concatenate([q_nope, q_pe], axis=-1)
    k = jnp.concatenate([k_nope, jnp.broadcast_to(k_pe, (b, s, MLA_HEADS, MLA_ROPE_DIM))], axis=-1)
    o = attention(_heads_first(q), _heads_first(k), _heads_first(v), (MLA_NOPE_DIM + MLA_ROPE_DIM) ** -0.5)
    return _heads_first(o).reshape(b * s, MLA_HEADS * MLA_V_DIM)


def _ssd_group(z, xbc, dt_raw, conv_w, conv_b, dt_bias, a_log, d_skip, norm_g, b, s):
    wb = jnp.concatenate([conv_w, conv_b[None, :], jnp.zeros((CONV_PACK_ROWS - SSD_CONV - 1, SSD_CONV_DIM), F32)], axis=0)
    xbc = conv_silu(xbc.reshape(b, s, SSD_CONV_DIM), wb)
    xs = xbc[..., :SSD_INNER]
    bm = xbc[..., SSD_INNER:SSD_INNER + SSD_GROUPS * SSD_STATE]
    cm = xbc[..., SSD_INNER + SSD_GROUPS * SSD_STATE:]
    dt = jax.nn.softplus(dt_raw.reshape(b, s, 2, SSD_HEADS) + dt_bias)
    a_neg = -jnp.exp(a_log)
    y_fwd = ssd_chunked(xs, dt[:, :, 0], a_neg[0], bm, cm, False)
    y_bwd = ssd_chunked(xs, dt[:, :, 1], a_neg[1], bm, cm, True)
    y = y_fwd + y_bwd + xs * jnp.repeat(d_skip, SSD_HEAD_DIM)
    y = y.reshape(b * s, SSD_INNER) * jax.nn.silu(z)
    return rms_norm(y, norm_g[None, :], SSD_GROUPS)


MIXER_WEIGHTS = ('q_norm_g', 'k_norm_g', 'mla_q_norm_g', 'w_uq', 'mla_kv_norm_g', 'w_ukv', 'conv_w', 'conv_b',
                 'dt_bias', 'a_log', 'd_skip', 'ssd_norm_g')


def _mixer(proj, w, rope_a, rope_b, b, s):
    idx = np.cumsum(IN_SPLITS).tolist()
    q_a, k_a, v_a, cq_b, ckv_b, kpe_b, z_c, xbc_c, dt_c = [_in_cols(proj, lo, hi)
                                                           for lo, hi in zip([0] + idx[:-1], idx)]
    o_a = _gqa_group(q_a, k_a, v_a, w["q_norm_g"], w["k_norm_g"], rope_a, b, s)
    o_b = _mla_group(cq_b, ckv_b, kpe_b, w["mla_q_norm_g"], w["w_uq"], w["mla_kv_norm_g"], w["w_ukv"], rope_b, b, s)
    o_c = _ssd_group(z_c, xbc_c, dt_c, w["conv_w"], w["conv_b"], w["dt_bias"], w["a_log"], w["d_skip"],
                     w["ssd_norm_g"], b, s)
    return jnp.concatenate([o_a, o_b, o_c], axis=-1)


def _seq_tile(s, row_bytes):
    return _row_tile(s, row_bytes)


def _normmod_fwd(x, g, scale, shift):
    b, s, d = x.shape
    tr = _seq_tile(s, d * 4)

    def body(x_ref, g_ref, sc_ref, sh_ref, h_ref):
        xv = x_ref[0]
        r = lax.rsqrt(jnp.mean(xv * xv, axis=-1, keepdims=True) + EPS)
        h_ref[0] = (xv * r * g_ref[...] * (1.0 + sc_ref[0]) + sh_ref[0]).astype(BF16)

    act = pl.BlockSpec((1, tr, d), lambda bi, i: (bi, i, 0))
    vec = pl.BlockSpec((1, 1, d), lambda bi, i: (bi, 0, 0))
    return pl.pallas_call(
        body, name="normmod_fwd", grid=(b, s // tr),
        in_specs=[act, pl.BlockSpec((1, d), lambda bi, i: (0, 0)), vec, vec], out_specs=act,
        out_shape=jax.ShapeDtypeStruct((b, s, d), BF16),
        compiler_params=pltpu.CompilerParams(dimension_semantics=("parallel", "parallel")),
    )(x, g, scale, shift)


def _normmod_bwd(x, g, scale, dh, resid):
    b, s, d = x.shape
    tr = _seq_tile(s, d * 4)

    def body(x_ref, g_ref, sc_ref, dh_ref, res_ref, dx_ref, dg_ref, dsc_ref, dsh_ref):
        bi, i = pl.program_id(0), pl.program_id(1)

        @pl.when((bi == 0) & (i == 0))
        def _():
            dg_ref[...] = jnp.zeros_like(dg_ref)

        @pl.when(i == 0)
        def _():
            dsc_ref[...] = jnp.zeros_like(dsc_ref)
            dsh_ref[...] = jnp.zeros_like(dsh_ref)

        xv = x_ref[0]
        dhv = dh_ref[0]
        gv = g_ref[...]
        r = lax.rsqrt(jnp.mean(xv * xv, axis=-1, keepdims=True) + EPS)
        xhat = xv * r
        dsh_ref[0] += jnp.sum(dhv, axis=0, keepdims=True)
        dsc_ref[0] += jnp.sum(dhv * (xhat * gv), axis=0, keepdims=True)
        dn = dhv * (1.0 + sc_ref[0])
        dg_ref[...] += jnp.sum(dn * xhat, axis=0, keepdims=True)
        dxhat = dn * gv
        dx_ref[0] = r * (dxhat - xhat * jnp.mean(dxhat * xhat, axis=-1, keepdims=True)) + res_ref[0]

    act = pl.BlockSpec((1, tr, d), lambda bi, i: (bi, i, 0))
    vec = pl.BlockSpec((1, 1, d), lambda bi, i: (bi, 0, 0))
    gain = pl.BlockSpec((1, d), lambda bi, i: (0, 0))
    return pl.pallas_call(
        body, name="normmod_bwd", grid=(b, s // tr),
        in_specs=[act, gain, vec, act, act], out_specs=[act, gain, vec, vec],
        out_shape=[jax.ShapeDtypeStruct((b, s, d), F32), jax.ShapeDtypeStruct((1, d), F32),
                   jax.ShapeDtypeStruct((b, 1, d), F32), jax.ShapeDtypeStruct((b, 1, d), F32)],
        compiler_params=pltpu.CompilerParams(dimension_semantics=("arbitrary", "arbitrary")),
    )(x, g, scale, dh, resid)


def _gated_add(x, gate, t):
    b, s, d = x.shape
    tr = _seq_tile(s, d * 4)

    def body(x_ref, g_ref, t_ref, o_ref):
        o_ref[0] = x_ref[0] + g_ref[0] * t_ref[0]

    act = pl.BlockSpec((1, tr, d), lambda bi, i: (bi, i, 0))
    vec = pl.BlockSpec((1, 1, d), lambda bi, i: (bi, 0, 0))
    return pl.pallas_call(
        body, name="gated_add", grid=(b, s // tr), in_specs=[act, vec, act], out_specs=act,
        out_shape=jax.ShapeDtypeStruct((b, s, d), F32),
        compiler_params=pltpu.CompilerParams(dimension_semantics=("parallel", "parallel")),
    )(x, gate, t)


def _gated_bwd(dy, gate, t):
    b, s, d = dy.shape
    tr = _seq_tile(s, d * 4)

    def body(dy_ref, g_ref, t_ref, dt_ref, dgate_ref):
        @pl.when(pl.program_id(1) == 0)
        def _():
            dgate_ref[...] = jnp.zeros_like(dgate_ref)

        dyv = dy_ref[0]
        dt_ref[0] = (g_ref[0] * dyv).astype(BF16)
        dgate_ref[0] += jnp.sum(dyv * t_ref[0], axis=0, keepdims=True)

    act = pl.BlockSpec((1, tr, d), lambda bi, i: (bi, i, 0))
    vec = pl.BlockSpec((1, 1, d), lambda bi, i: (bi, 0, 0))
    return pl.pallas_call(
        body, name="gated_bwd", grid=(b, s // tr), in_specs=[act, vec, act], out_specs=[act, vec],
        out_shape=[jax.ShapeDtypeStruct((b, s, d), BF16), jax.ShapeDtypeStruct((b, 1, d), F32)],
        compiler_params=pltpu.CompilerParams(dimension_semantics=("parallel", "arbitrary")),
    )(dy, gate, t)


def _swiglu_fwd(gu, plan=None):
    rows, f2 = gu.shape
    f = f2 // 2
    tr = _row_tile(rows, f2 * 4, SWIGLU_TILE_BYTES)
    steps = rows // tr
    p_in, p_out, p_shapes, p_scratch, p_args = _plan_specs(plan)

    def body(gu_ref, *rest):
        a_ref = rest[len(p_in)]
        i = pl.program_id(0)

        def compute():
            gt = gu_ref[:, :f]
            a_ref[...] = (gt * jax.nn.sigmoid(gt) * gu_ref[:, f:]).astype(BF16)

        _ride(plan, rest[:len(p_in)] + rest[len(p_in) + 1:], i == 0, i == steps - 1, compute)

    outs = pl.pallas_call(
        body, name="swiglu_fwd", grid=(steps,),
        in_specs=[pl.BlockSpec((tr, f2), lambda i: (i, 0))] + p_in,
        out_specs=[pl.BlockSpec((tr, f), lambda i: (i, 0))] + p_out,
        out_shape=[jax.ShapeDtypeStruct((rows, f), BF16)] + p_shapes, scratch_shapes=p_scratch,
        compiler_params=pltpu.CompilerParams(dimension_semantics=("arbitrary" if plan is not None else "parallel",)),
    )(gu, *p_args)
    return outs[0] if plan is None else (outs[0], list(outs[1:]))


def _swiglu_bwd(gu, dact):
    rows, f2 = gu.shape
    f = f2 // 2
    tr = _row_tile(rows, f2 * 4, SWIGLU_TILE_BYTES)

    def body(gu_ref, da_ref, dgu_ref):
        gt = gu_ref[:, :f]
        up = gu_ref[:, f:]
        da = da_ref[...]
        sg = jax.nn.sigmoid(gt)
        dgu_ref[:, :f] = (da * up * (sg * (1.0 + gt * (1.0 - sg)))).astype(BF16)
        dgu_ref[:, f:] = (da * gt * sg).astype(BF16)

    return pl.pallas_call(
        body, name="swiglu_bwd", grid=(rows // tr,),
        in_specs=[pl.BlockSpec((tr, f2), lambda i: (i, 0)), pl.BlockSpec((tr, f), lambda i: (i, 0))],
        out_specs=pl.BlockSpec((tr, f2), lambda i: (i, 0)),
        out_shape=jax.ShapeDtypeStruct((rows, f2), BF16),
        compiler_params=pltpu.CompilerParams(dimension_semantics=("parallel",)),
    )(gu, dact)


class _Gathered:
    def __init__(self, shards):
        self.shards, self.full = shards, {}

    def plan(self, keys):
        return _gather_plan([self.shards[k] for k in keys])

    def store(self, keys, outs):
        for key, out in zip(keys, outs):
            name = key[0]
            g = out.reshape((N_CHIPS,) + self.shards[key].shape)
            if name in CHIP_BLOCKED:
                full = g
            elif name in COL_SHARDED and name not in TRANSPOSED:
                full = _cols_full(g).astype(F32)
            else:
                full = g.reshape(g.shape[0] * g.shape[1], g.shape[2])
            self.full[key] = full

    def carry(self, keys, fn):
        if not keys:
            return fn(None)
        res, outs = fn(self.plan(keys))
        self.store(keys, outs)
        return res


def _gather_schedule(depth):
    every = [(n, l) for l in range(depth) for n in ('w_uq', 'w_ukv')]
    sched = {'first': [('w_in', 0)] + every}
    for l in range(depth):
        sched[('w_in_fwd', l)] = ([('w_gate_up', l)] if l == 0 else []) + [('w_out', l)]
        if l == 0:
            sched[('swiglu_fwd', l)] = [('w_down', l)]
        if l + 1 < depth:
            sched[('w_out_fwd', l)] = [('w_in', l + 1)]
            sched[('w_gate_up_fwd', l)] = [('w_gate_up', l + 1)]
            sched[('w_down_fwd', l)] = [('w_down', l + 1)]
    return sched


GATE_UP_PIECES = (1, 1, 3, 3)


class _Reducer:
    def __init__(self):
        self.parts, self.recv, self.result = {}, {}, {}

    def add(self, items):
        blocks = []
        for (name, _), grad in items:
            if name in CHIP_BLOCKED:
                blocks.append(grad)
            elif name in COL_SHARDED and name not in TRANSPOSED:
                blocks.append(_cols_split(grad))
            else:
                blocks.append(grad.reshape(N_CHIPS, grad.shape[0] // N_CHIPS, grad.shape[1]))
        for (key, _), parts in zip(items, _rs_parts(blocks)):
            self.parts[key] = parts
            self.recv[key] = []

    def pieces(self, key):
        rows = self.parts[key].shape[1]
        shares = GATE_UP_PIECES if key[0] == 'w_gate_up' else (1,)
        unit = rows // sum(shares)
        starts = np.cumsum((0,) + shares[:-1])
        return [(key, int(a) * unit, n * unit) for a, n in zip(starts, shares)]

    def plan(self, jobs):
        return _chip_exchange_plan([(self.parts[key], row0, rows) for key, row0, rows in jobs])

    def store(self, jobs, outs):
        complete = []
        for (key, row0, rows), out in zip(jobs, outs):
            self.recv[key].append((row0, out))
            if len(self.recv[key]) == len(self.pieces(key)):
                complete.append(key)
        if complete:
            items = [(self.parts[key], sorted(self.recv[key], key=lambda t: t[0])) for key in complete]
            self.result.update(zip(complete, _rs_result(items)))

    def carry(self, keys, fn, piece=None, also=()):
        jobs = [j for key in keys for j in self.pieces(key)]
        if piece is not None:
            jobs = [j for key in keys for j in self.pieces(key)[piece:piece + 1]]
        jobs += [j for key in also for j in self.pieces(key)]
        if not jobs:
            return fn(None)
        res, outs = fn(self.plan(jobs))
        self.store(jobs, outs)
        return res

    def flush(self):
        jobs = [j for key in self.parts for j in self.pieces(key)
                if key not in self.result and j[1] not in [r for r, _ in self.recv[key]]]
        if jobs:
            self.store(jobs, _run_plan(self.plan(jobs), "rs_chip_exchange"))


def _layer_fwd(x, mod, w, gathered, l, sched, rope_a, rope_b):
    b, s, d = x.shape
    m = b * s
    shift1, scale1, gate1, shift2, scale2, gate2 = [t[:, None, :] for t in jnp.split(mod, 6, axis=-1)]
    g1, g2 = w["norm1_g"][None, :], w["norm2_g"][None, :]
    full = lambda n: gathered.full[(n, l)]
    h1 = _normmod_fwd(x, g1, scale1, shift1).reshape(m, d)
    proj = gathered.carry(sched.get(('w_in_fwd', l)), lambda p: _matmul(h1, full('w_in'), tb=True, name="w_in_fwd", plan=p))
    mixer_w = {n: (full(n) if n in COL_SHARDED else w[n]) for n in MIXER_WEIGHTS}
    o, mixer_vjp = jax.vjp(lambda p, mw: _mixer(p, mw, rope_a, rope_b, b, s), proj, mixer_w)
    o = o.astype(BF16)
    mix = gathered.carry(sched.get(('w_out_fwd', l)), lambda p: _matmul(o, full('w_out'), name="w_out_fwd", plan=p))
    mix = mix.reshape(b, s, d)
    x_mid = _gated_add(x, gate1, mix)
    h2 = _normmod_fwd(x_mid, g2, scale2, shift2).reshape(m, d)
    gu = gathered.carry(sched.get(('w_gate_up_fwd', l)),
                        lambda p: _matmul(h2, full('w_gate_up'), name="w_gate_up_fwd", plan=p, chips='b'))
    act = gathered.carry(sched.get(('swiglu_fwd', l)), lambda p: _swiglu_fwd(gu, plan=p))
    ffn = gathered.carry(sched.get(('w_down_fwd', l)), lambda p: _matmul(act, full('w_down'), name="w_down_fwd", plan=p))
    ffn = ffn.reshape(b, s, d)
    x_out = _gated_add(x_mid, gate2, ffn)
    res = (x, x_mid, h1, h2, o, mix, gu, act, ffn, mixer_vjp, scale1, gate1, scale2, gate2, g1, g2)
    return x_out, res


def _layer_bwd(res, gathered, reducer, l, depth, dx_out):
    x, x_mid, h1, h2, o, mix, gu, act, ffn, mixer_vjp, scale1, gate1, scale2, gate2, g1, g2 = res
    b, s, d = x.shape
    m = b * s
    full = lambda n: gathered.full[(n, l)]
    above = l + 1 < depth
    dffn, dgate2 = _gated_bwd(dx_out, gate2, ffn)
    dffn = dffn.reshape(m, d)
    dact = reducer.carry([('w_out', l + 1), ('w_uq', l + 1), ('w_ukv', l + 1)] if above else [],
                         lambda p: _matmul(dffn, full('w_down'), tb=True, name="w_down_dx", plan=p))
    dw = reducer.carry([('w_in', l + 1)] if above else [],
                       lambda p: _matmul(act, dffn, ta=True, name="w_down_dw", plan=p))
    reducer.add([(('w_down', l), dw)])
    dgu = _swiglu_bwd(gu, dact)
    dh2 = reducer.carry([('w_down', l)], lambda p: _matmul(dgu, full('w_gate_up'), tb=True, name="w_gate_up_dx", plan=p,
                                                           chips='b'))
    dh2 = dh2.reshape(b, s, d)
    reducer.add([(('w_gate_up', l), _matmul(h2, dgu, ta=True, name="w_gate_up_dw", chips='out'))])
    dx_mid, dg2, dscale2, dshift2 = _normmod_bwd(x_mid, g2, scale2, dh2, dx_out)
    dmix, dgate1 = _gated_bwd(dx_mid, gate1, mix)
    dmix = dmix.reshape(m, d)
    gate_up = [('w_gate_up', l)]
    do = reducer.carry(gate_up, lambda p: _matmul(dmix, full('w_out'), tb=True, name="w_out_dx", plan=p), piece=0)
    dw_out = reducer.carry(gate_up, lambda p: _matmul(o, dmix, ta=True, name="w_out_dw", plan=p), piece=1)
    dproj, grads = mixer_vjp(do)
    grads = dict(grads)
    reducer.add([(('w_out', l), dw_out), (('w_uq', l), grads.pop('w_uq')), (('w_ukv', l), grads.pop('w_ukv'))])
    dproj = dproj.astype(BF16)
    bottom = [('w_out', l), ('w_uq', l), ('w_ukv', l)] if l == 0 else []
    dh1 = reducer.carry(gate_up, lambda p: _matmul(dproj, full('w_in'), name="w_in_dx", plan=p), piece=2, also=bottom)
    dh1 = dh1.reshape(b, s, d)
    dw = reducer.carry(gate_up, lambda p: _matmul(dproj, h1, ta=True, name="w_in_dw", plan=p), piece=3)
    reducer.add([(('w_in', l), dw)])
    dx, dg1, dscale1, dshift1 = _normmod_bwd(x, g1, scale1, dh1, dx_mid)
    grads["norm1_g"], grads["norm2_g"] = dg1[0], dg2[0]
    dmod = jnp.concatenate([dshift1, dscale1, dgate1, dshift2, dscale2, dgate2], axis=-1)[:, 0, :]
    return dx, dmod, grads


def _tail_loss(x2, final_norm_g, target2):
    return loss_head(rms_norm(x2, final_norm_g[None, :], 1), target2)


def _forward_backward(x, mod, small, gathered, reducer, final_norm_g, target):
    b, s, d = x.shape
    depth = len(small)
    rope_a = _axial_rope_tables(s, GQA_HEAD_DIM)
    rope_b = _axial_rope_tables(s, MLA_ROPE_DIM)
    sched = _gather_schedule(depth)
    first = sched['first']
    gathered.store(first, _run_plan(gathered.plan(first), "all_gather_chips"))
    saved = []
    for l in range(depth):
        x, res = _layer_fwd(x, mod[l], small[l], gathered, l, sched, rope_a, rope_b)
        saved.append(res)
    loss, (dx2, dfinal) = jax.value_and_grad(_tail_loss, argnums=(0, 1))(
        x.reshape(b * s, d), final_norm_g, target.reshape(b * s, d))
    dx = dx2.reshape(b, s, d)
    dmods, gsmall = [None] * depth, [None] * depth
    for l in reversed(range(depth)):
        dx, dmods[l], gsmall[l] = _layer_bwd(saved[l], gathered, reducer, l, depth, dx)
    return loss, dx, jnp.stack(dmods), gsmall, dfinal


ANY = pl.BlockSpec(memory_space=pl.ANY)


def _flip_if(v, bit):
    return 1 - v if bit else v


def _all_gather_devices(x):
    def body(x_ref, out_ref, send_sems, recv_sems):
        mx, my, mc = lax.axis_index("x"), lax.axis_index("y"), lax.axis_index("c")
        me = 4 * mx + 2 * my + mc
        sends = []
        for k in range(1, N_DEV):
            peer = (_flip_if(mx, k & 4), _flip_if(my, k & 2), _flip_if(mc, k & 1))
            cp = pltpu.make_async_remote_copy(src_ref=x_ref, dst_ref=out_ref.at[me], send_sem=send_sems.at[k - 1],
                                              recv_sem=recv_sems.at[k - 1], device_id=peer, device_id_type=MESH)
            cp.start()
            sends.append(cp)
        for k in range(1, N_DEV):
            peer = (_flip_if(mx, k & 4), _flip_if(my, k & 2), _flip_if(mc, k & 1))
            src = 4 * peer[0] + 2 * peer[1] + peer[2]
            pltpu.make_async_remote_copy(src_ref=x_ref, dst_ref=out_ref.at[src], send_sem=send_sems.at[k - 1],
                                         recv_sem=recv_sems.at[k - 1], device_id=peer, device_id_type=MESH).wait_recv()
        for cp in sends:
            cp.wait_send()

    out = pl.pallas_call(
        body, name="all_gather_devices", in_specs=[ANY], out_specs=ANY,
        out_shape=jax.ShapeDtypeStruct((N_DEV,) + x.shape, x.dtype),
        scratch_shapes=[pltpu.SemaphoreType.DMA((N_DEV - 1,)), pltpu.SemaphoreType.DMA((N_DEV - 1,))],
    )(x)
    me = 4 * lax.axis_index("x") + 2 * lax.axis_index("y") + lax.axis_index("c")
    return lax.dynamic_update_index_in_dim(out, x, me, 0)


class _Plan:
    def __init__(self, inputs, out_shapes, sem_counts, start, finish):
        self.inputs, self.out_shapes, self.sem_counts = list(inputs), list(out_shapes), list(sem_counts)
        self.start, self.finish = start, finish

    def specs(self):
        return ([ANY] * len(self.inputs), [ANY] * len(self.out_shapes),
                [pltpu.SemaphoreType.DMA((c,)) for c in self.sem_counts])

    def split(self, refs):
        a, b = len(self.inputs), len(self.inputs) + len(self.out_shapes)
        return refs[:a], refs[a:b], refs[b:]


def _run_plan(plan, name):
    def body(*refs):
        ins, outs, sems = plan.split(refs)
        plan.start(ins, outs, sems)
        plan.finish(ins, outs, sems)

    in_specs, out_specs, scratch = plan.specs()
    return pl.pallas_call(body, name=name, in_specs=in_specs, out_specs=out_specs, out_shape=plan.out_shapes,
                          scratch_shapes=scratch)(*plan.inputs)


def _gather_plan(shards):
    n = len(shards)
    halves = [t.reshape(2, t.shape[0] // 2, t.shape[1]) for t in shards]
    count = (N_CHIPS - 1) * n

    def copies(kind, ins, outs, sems):
        ici_send, ici_recv, d2d_send, d2d_recv, own_send, own_recv = sems
        mx, my, mc = lax.axis_index("x"), lax.axis_index("y"), lax.axis_index("c")
        me = 2 * mx + my
        sibling = (mx, my, 1 - mc)
        if kind == 'own':
            return [pltpu.make_async_remote_copy(src_ref=ins[i], dst_ref=outs[i].at[me], send_sem=own_send.at[i],
                                                 recv_sem=own_recv.at[i], device_id=sibling, device_id_type=MESH)
                    for i in range(n)]
        cps = []
        for k in range(1, N_CHIPS):
            peer = (_flip_if(mx, k & 2), _flip_if(my, k & 1), mc)
            src = 2 * peer[0] + peer[1]
            for i in range(n):
                j = (k - 1) * n + i
                if kind in ('ici', 'landed'):
                    dst = outs[i].at[me, mc] if kind == 'ici' else outs[i].at[src, mc]
                    cps.append(pltpu.make_async_remote_copy(
                        src_ref=ins[i].at[mc], dst_ref=dst, send_sem=ici_send.at[j], recv_sem=ici_recv.at[j],
                        device_id=peer, device_id_type=MESH))
                else:
                    half = outs[i].at[src, mc] if kind == 'fwd' else outs[i].at[src, 1 - mc]
                    cps.append(pltpu.make_async_remote_copy(
                        src_ref=half, dst_ref=half, send_sem=d2d_send.at[j], recv_sem=d2d_recv.at[j],
                        device_id=sibling, device_id_type=MESH))
        return cps

    def start(ins, outs, sems):
        for cp in copies('own', ins, outs, sems) + copies('ici', ins, outs, sems):
            cp.start()

    def finish(ins, outs, sems):
        fwd = copies('fwd', ins, outs, sems)
        for arrived, onward in zip(copies('landed', ins, outs, sems), fwd):
            arrived.wait_recv()
            onward.start()
        own = copies('own', ins, outs, sems)
        for cp in copies('fwd_in', ins, outs, sems) + own:
            cp.wait_recv()
        for cp in own + copies('ici', ins, outs, sems) + fwd:
            cp.wait_send()

    out_shapes = [jax.ShapeDtypeStruct((N_CHIPS,) + t.shape, t.dtype) for t in halves]
    return _Plan(halves, out_shapes, [count] * 4 + [n] * 2, start, finish)


def _sibling_exchange(blocks, name):
    n = len(blocks)

    def body(*refs):
        ins, outs = refs[:n], refs[n:2 * n]
        send_sems, recv_sems = refs[2 * n:]
        mx, my, mc = lax.axis_index("x"), lax.axis_index("y"), lax.axis_index("c")
        cps = []
        for i in range(n):
            cp = pltpu.make_async_remote_copy(src_ref=ins[i], dst_ref=outs[i], send_sem=send_sems.at[i],
                                              recv_sem=recv_sems.at[i], device_id=(mx, my, 1 - mc),
                                              device_id_type=MESH)
            cp.start()
            cps.append(cp)
        for cp in cps:
            cp.wait()

    return pl.pallas_call(
        body, name=name, in_specs=[ANY] * n, out_specs=[ANY] * n,
        out_shape=[jax.ShapeDtypeStruct(t.shape, t.dtype) for t in blocks],
        scratch_shapes=[pltpu.SemaphoreType.DMA((n,)), pltpu.SemaphoreType.DMA((n,))],
    )(*blocks)


def _add_halves(own, recv):
    nb, r, c = own.shape
    tr = _row_tile(r, c * 4)

    def body(g_ref, r_ref, o_ref):
        o_ref[...] = (g_ref[...] + r_ref[...].astype(F32)).astype(BF16)

    spec = pl.BlockSpec((1, tr, c), lambda k, i: (k, i, 0))
    return pl.pallas_call(
        body, name="rs_add_halves", grid=(nb, r // tr), in_specs=[spec, spec], out_specs=spec,
        out_shape=jax.ShapeDtypeStruct((nb, r, c), BF16),
        compiler_params=pltpu.CompilerParams(dimension_semantics=("parallel", "parallel")),
    )(own, recv)


def _chip_exchange_plan(jobs):
    n = len(jobs)
    count = (N_CHIPS - 1) * n

    def copies(ins, outs, sems):
        send_sems, recv_sems = sems
        mx, my, mc = lax.axis_index("x"), lax.axis_index("y"), lax.axis_index("c")
        cps = []
        for k in range(1, N_CHIPS):
            peer = (_flip_if(mx, k & 2), _flip_if(my, k & 1), mc)
            dst_chip = 2 * peer[0] + peer[1]
            for i, (_, row0, rows) in enumerate(jobs):
                j = (k - 1) * n + i
                cps.append(pltpu.make_async_remote_copy(
                    src_ref=ins[i].at[dst_chip, pl.ds(row0, rows)], dst_ref=outs[i].at[k - 1],
                    send_sem=send_sems.at[j], recv_sem=recv_sems.at[j], device_id=peer, device_id_type=MESH))
        return cps

    def start(ins, outs, sems):
        for cp in copies(ins, outs, sems):
            cp.start()

    def finish(ins, outs, sems):
        for cp in copies(ins, outs, sems):
            cp.wait()

    out_shapes = [jax.ShapeDtypeStruct((N_CHIPS - 1, rows, p.shape[2]), p.dtype) for p, _, rows in jobs]
    return _Plan([p for p, _, _ in jobs], out_shapes, [count, count], start, finish)


def _sum_chips(parts, recv, chip, row0):
    _, rows, c = recv.shape
    tr = _row_tile(rows, c * 4)
    assert row0 % tr == 0

    def body(chip_ref, p_ref, r_ref, o_ref):
        acc = p_ref[0].astype(F32)
        for k in range(N_CHIPS - 1):
            acc = acc + r_ref[k].astype(F32)
        o_ref[...] = acc

    return pl.pallas_call(
        body, name="rs_sum_chips",
        grid_spec=pltpu.PrefetchScalarGridSpec(
            num_scalar_prefetch=1, grid=(rows // tr,),
            in_specs=[pl.BlockSpec((1, tr, c), lambda i, chip_ref: (chip_ref[0], i + row0 // tr, 0)),
                      pl.BlockSpec((N_CHIPS - 1, tr, c), lambda i, chip_ref: (0, i, 0))],
            out_specs=pl.BlockSpec((tr, c), lambda i, chip_ref: (i, 0))),
        out_shape=jax.ShapeDtypeStruct((rows, c), F32),
        compiler_params=pltpu.CompilerParams(dimension_semantics=("parallel",)),
    )(chip, parts, recv)


def _sum_leading(t, name):
    nb, r, c = t.shape
    tr = _row_tile(r, c * 4 * nb)

    def body(t_ref, o_ref):
        acc = t_ref[0]
        for k in range(1, nb):
            acc = acc + t_ref[k]
        o_ref[...] = acc

    return pl.pallas_call(
        body, name=name, grid=(r // tr,),
        in_specs=[pl.BlockSpec((nb, tr, c), lambda i: (0, i, 0))],
        out_specs=pl.BlockSpec((tr, c), lambda i: (i, 0)),
        out_shape=jax.ShapeDtypeStruct((r, c), F32),
        compiler_params=pltpu.CompilerParams(dimension_semantics=("parallel",)),
    )(t)


def _rs_parts(grads):
    mc = lax.axis_index("c")
    split = [g.reshape(g.shape[0], 2, g.shape[1] // 2, g.shape[2]) for g in grads]
    own = [lax.dynamic_index_in_dim(g, mc, axis=1, keepdims=False) for g in split]
    away = [lax.dynamic_index_in_dim(g, 1 - mc, axis=1, keepdims=False).astype(BF16) for g in split]
    return [_add_halves(o, r) for o, r in zip(own, _sibling_exchange(away, "rs_sibling_exchange"))]


def _rs_result(items):
    mc = lax.axis_index("c")
    chip = (2 * lax.axis_index("x") + lax.axis_index("y")).astype(jnp.int32).reshape(1)
    mine = []
    for parts, pieces in items:
        done = [_sum_chips(parts, recv, chip, row0) for row0, recv in pieces]
        mine.append(done[0] if len(done) == 1 else jnp.concatenate(done, axis=0))
    theirs = _sibling_exchange(mine, "rs_sibling_swap")
    return [jnp.concatenate([jnp.where(mc == 0, a, b), jnp.where(mc == 0, b, a)], axis=0)
            for a, b in zip(mine, theirs)]


def _adamw(w, g, m, v, plan=None):
    shape = w.shape
    cols = shape[-1]
    if len(shape) == 3:
        lead, rows = shape[0], shape[1]
    else:
        lead, rows = 1, (int(np.prod(shape[:-1])) if len(shape) > 1 else 1)
    w2, g2, m2, v2 = [t.reshape(lead, rows, cols) for t in (w, g, m, v)]
    tr = _row_tile(rows, cols * 4, ADAM_TILE_BYTES)
    t0, tc = 1, cols
    if tr == rows and rows * cols * 4 > ADAM_TILE_BYTES and cols % LANES == 0:
        tc = max(t for t in _tile_cands(cols, cols) if t == LANES or rows * t * 4 <= ADAM_TILE_BYTES)
    elif rows < 8 and cols % LANES == 0:
        fits = [(a * t, t, a) for a in range(1, lead + 1) if lead % a == 0 for t in _tile_cands(cols, cols)
                if a * rows * t * 4 <= ADAM_TILE_BYTES]
        _, tc, t0 = max(fits) if fits else (0, LANES, 1)
    per = (rows // tr) * (cols // tc)
    steps = (lead // t0) * per
    p_in, p_out, p_shapes, p_scratch, p_args = _plan_specs(plan)

    def body(w_ref, g_ref, m_ref, v_ref, *rest):
        d_ref, mo_ref, vo_ref = rest[len(p_in):len(p_in) + 3]
        i = pl.program_id(0) * per + pl.program_id(1)

        def compute():
            gv = g_ref[...]
            mn = ADAM_B1 * m_ref[...] + (1.0 - ADAM_B1) * gv
            vn = ADAM_B2 * v_ref[...] + (1.0 - ADAM_B2) * (gv * gv)
            m_hat = mn / (1.0 - ADAM_B1 ** ADAM_STEP)
            v_hat = vn / (1.0 - ADAM_B2 ** ADAM_STEP)
            d_ref[...] = -ADAM_LR * (m_hat / (jnp.sqrt(v_hat) + ADAM_EPS) + ADAM_WD * w_ref[...])
            mo_ref[...] = mn
            vo_ref[...] = vn

        _ride(plan, rest[:len(p_in)] + rest[len(p_in) + 3:], i == 0, i == steps - 1, compute)

    col_blocks = cols // tc
    spec = pl.BlockSpec((t0, tr, tc), lambda a, i: (a, i // col_blocks, i % col_blocks))
    sem = "arbitrary" if plan is not None else "parallel"
    outs = pl.pallas_call(
        body, name="adamw", grid=(lead // t0, per), in_specs=[spec] * 4 + p_in, out_specs=[spec] * 3 + p_out,
        out_shape=[jax.ShapeDtypeStruct((lead, rows, cols), F32)] * 3 + p_shapes, scratch_shapes=p_scratch,
        compiler_params=pltpu.CompilerParams(dimension_semantics=(sem, sem), vmem_limit_bytes=ADAM_VMEM_LIMIT),
    )(w2, g2, m2, v2, *p_args)
    res = [t.reshape(shape) for t in outs[:3]]
    return res if plan is None else (res, list(outs[3:]))


WEIGHTS = ['w_ada', 'b_ada', 'norm1_g', 'norm2_g', 'w_in', 'q_norm_g', 'k_norm_g', 'mla_q_norm_g', 'w_uq',
           'mla_kv_norm_g', 'w_ukv', 'conv_w', 'conv_b', 'dt_bias', 'a_log', 'd_skip', 'ssd_norm_g', 'w_out',
           'w_gate_up', 'w_down', 'final_norm_g']
COL_SHARDED = ('w_in', 'w_uq', 'w_ukv', 'w_gate_up')
ROW_SHARDED = ('w_out', 'w_down')
CHIP_BLOCKED = ('w_gate_up',)
TRANSPOSED = ('w_in',)
SMALL_LAYER = ('norm1_g', 'norm2_g', 'q_norm_g', 'k_norm_g', 'mla_q_norm_g', 'mla_kv_norm_g', 'conv_w', 'conv_b',
               'dt_bias', 'a_log', 'd_skip', 'ssd_norm_g')


def _pack(parts):
    flat = jnp.concatenate([p.reshape(-1) for p in parts])
    n = flat.shape[0]
    rows = -(-n // (8 * LANES)) * 8
    return jnp.pad(flat, (0, rows * LANES - n)).reshape(rows, LANES)


def _unpack(flat, shapes):
    out, pos = [], 0
    for shp in shapes:
        size = int(np.prod(shp))
        out.append(flat[pos:pos + size].reshape(shp))
        pos += size
    return out


def _cols_full(gathered):
    k, r, c = gathered.shape
    return jnp.transpose(gathered, (1, 0, 2)).reshape(r, k * c)


def _cols_split(full):
    r, c4 = full.shape
    return jnp.transpose(full.reshape(r, N_CHIPS, c4 // N_CHIPS), (1, 0, 2))


def kernel(x, c, w_ada, b_ada, norm1_g, norm2_g, w_in, q_norm_g, k_norm_g, mla_q_norm_g, w_uq, mla_kv_norm_g, w_ukv, conv_w, conv_b, dt_bias, a_log, d_skip, ssd_norm_g, w_out, w_gate_up, w_down, final_norm_g, loss_target, m_w_ada, m_b_ada, m_norm1_g, m_norm2_g, m_w_in, m_q_norm_g, m_k_norm_g, m_mla_q_norm_g, m_w_uq, m_mla_kv_norm_g, m_w_ukv, m_conv_w, m_conv_b, m_dt_bias, m_a_log, m_d_skip, m_ssd_norm_g, m_w_out, m_w_gate_up, m_w_down, m_final_norm_g, v_w_ada, v_b_ada, v_norm1_g, v_norm2_g, v_w_in, v_q_norm_g, v_k_norm_g, v_mla_q_norm_g, v_w_uq, v_mla_kv_norm_g, v_w_ukv, v_conv_w, v_conv_b, v_dt_bias, v_a_log, v_d_skip, v_ssd_norm_g, v_w_out, v_w_gate_up, v_w_down, v_final_norm_g):
    args = dict(locals())
    weights = {n: args[n] for n in WEIGHTS}
    depth = w_in.shape[0]
    bl, s, d = x.shape
    mx, my, mc = lax.axis_index("x"), lax.axis_index("y"), lax.axis_index("c")
    chip = 2 * mx + my
    dev = 2 * chip + mc
    ada_cols = w_ada.shape[-1]
    conv_cols = conv_w.shape[-1]

    first_shapes = [c.shape, conv_w.shape]
    first = _all_gather_devices(_pack([c, conv_w]))
    first = [_unpack(first[i].reshape(-1), first_shapes) for i in range(N_DEV)]
    c_act = jax.nn.silu(jnp.concatenate([f[0] for f in first], axis=0))
    conv_w_full = jnp.concatenate([first[2 * k][1] for k in range(N_CHIPS)], axis=-1)

    b_cols = lax.dynamic_slice_in_dim(b_ada, chip * ada_cols, ada_cols, axis=1)
    c_act_b = c_act.astype(BF16)
    mod_cols = jnp.stack([_matmul(c_act_b, w_ada[l], name="ada_fwd") + b_cols[l][None, :]
                          for l in range(depth)])
    mod_all = _all_gather_devices(mod_cols.reshape(depth * N_DEV * bl, ada_cols))
    mod_all = mod_all.reshape(N_DEV, depth, N_DEV, bl, ada_cols)
    mod_mine = lax.dynamic_index_in_dim(mod_all, dev, axis=2, keepdims=False)
    mod = jnp.concatenate([mod_mine[2 * k] for k in range(N_CHIPS)], axis=-1)

    big = COL_SHARDED + ROW_SHARDED
    flip = lambda t: jnp.transpose(t, (2, 0, 1))
    unflip = lambda t: jnp.transpose(t, (1, 2, 0))
    shards = {(n, l): weights[n][l].astype(BF16) for n in big for l in range(depth)}
    for l in range(depth):
        shards[('w_in', l)] = jnp.pad(flip(w_in)[:, l, :].astype(BF16), ((0, IN_SHARD_PAD - IN_SHARD), (0, 0)))
    gathered = _Gathered(shards)
    reducer = _Reducer()
    small_w = []
    for l in range(depth):
        w = {n: weights[n][l] for n in SMALL_LAYER if n != 'conv_w'}
        w['conv_w'] = conv_w_full[l]
        small_w.append(w)
    loss_local, gx, gmod, glayers, gfinal = _forward_backward(x, mod, small_w, gathered, reducer, final_norm_g,
                                                              loss_target)

    small_parts = [jnp.stack([glayers[l][n] for l in range(depth)]) for n in SMALL_LAYER]
    small_parts += [gfinal, loss_local.reshape(1), gmod]
    small_shapes = [p.shape for p in small_parts]
    last = _all_gather_devices(_pack(small_parts))
    summed = _unpack(_sum_leading(last, "sum_devices").reshape(-1), small_shapes)
    small = dict(zip(SMALL_LAYER, summed[:len(SMALL_LAYER)]))
    g_final, loss, gmod_sum = summed[len(SMALL_LAYER):]
    small['conv_w'] = lax.dynamic_slice_in_dim(small['conv_w'], chip * conv_cols, conv_cols, axis=2)
    gmod_all = jnp.stack([_unpack(last[i].reshape(-1), small_shapes)[-1] for i in range(N_DEV)], axis=1)
    gmod_all = gmod_all.reshape(depth, N_DEV * bl, gmod.shape[-1])
    gmod_cols = lax.dynamic_slice_in_dim(gmod_all, chip * ada_cols, ada_cols, axis=2)
    g_w_ada = jnp.stack([_matmul(c_act_b, gmod_cols[l].astype(BF16), ta=True, name="ada_dw") for l in range(depth)])
    g_b_ada = gmod_sum[:, 0]
    for i in range(1, bl):
        g_b_ada = g_b_ada + gmod_sum[:, i]

    grad = {'w_ada': g_w_ada, 'b_ada': g_b_ada, 'final_norm_g': g_final}
    for n in SMALL_LAYER:
        grad[n] = small[n]

    delta, new_m, new_v = {}, {}, {}
    left = [key for key in reducer.parts if key not in reducer.result and not reducer.recv[key]]
    grad['w_down'] = jnp.stack([reducer.result[('w_down', l)] for l in range(depth)])
    delta['w_down'], new_m['w_down'], new_v['w_down'] = reducer.carry(
        left, lambda p: _adamw(w_down, grad['w_down'], m_w_down, v_w_down, plan=p))
    reducer.flush()
    for n in big:
        if n in TRANSPOSED:
            grad[n] = jnp.stack([reducer.result[(n, l)][:weights[n].shape[2]] for l in range(depth)], axis=1)
        elif n != 'w_down':
            grad[n] = jnp.stack([reducer.result[(n, l)] for l in range(depth)])
    for n in WEIGHTS:
        if n in TRANSPOSED:
            outs = _adamw(flip(weights[n]), grad[n], flip(args["m_" + n]), flip(args["v_" + n]))
            grad[n], delta[n], new_m[n], new_v[n] = [unflip(t) for t in [grad[n]] + outs]
        elif n != 'w_down':
            delta[n], new_m[n], new_v[n] = _adamw(weights[n], grad[n], args["m_" + n], args["v_" + n])
    return (loss.reshape(()), gx, *[grad[n] for n in WEIGHTS], *[delta[n] for n in WEIGHTS],
            *[new_m[n] for n in WEIGHTS], *[new_v[n] for n in WEIGHTS])
```

```python
import functools

import numpy as np
import jax
import jax.numpy as jnp
from jax import lax
from jax.experimental import pallas as pl
from jax.experimental.pallas import tpu as pltpu

F32 = jnp.float32
BF16 = jnp.bfloat16
HIGHEST = lax.Precision.HIGHEST
MESH = pl.DeviceIdType.MESH

GRID_W = 64
ROPE_THETA = 10000.0
EPS = 1e-6

GQA_HEADS, GQA_KV_HEADS, GQA_HEAD_DIM = 6, 2, 128
GQA_WIDTH = GQA_HEADS * GQA_HEAD_DIM
GQA_KV_WIDTH = GQA_KV_HEADS * GQA_HEAD_DIM
MLA_HEADS, MLA_Q_LORA, MLA_KV_LORA = 4, 512, 256
MLA_NOPE_DIM, MLA_ROPE_DIM, MLA_V_DIM = 128, 64, 128
SSD_HEADS, SSD_HEAD_DIM, SSD_GROUPS, SSD_STATE, SSD_CONV, SSD_CHUNK = 12, 64, 2, 128, 5, 128
SSD_INNER = SSD_HEADS * SSD_HEAD_DIM
SSD_CONV_DIM = SSD_INNER + 2 * SSD_GROUPS * SSD_STATE
SSD_GROUP_HEADS = SSD_HEADS // SSD_GROUPS
SSD_GROUP_WIDTH = SSD_GROUP_HEADS * SSD_HEAD_DIM
IN_SPLITS = (GQA_WIDTH, GQA_KV_WIDTH, GQA_KV_WIDTH, MLA_Q_LORA, MLA_KV_LORA, MLA_ROPE_DIM, SSD_INNER, SSD_CONV_DIM,
             2 * SSD_HEADS)
IN_COLS = sum(IN_SPLITS)
LANES = 128
N_CHIPS = 4
IN_SHARD = IN_COLS // N_CHIPS
IN_SHARD_PAD = -(-IN_SHARD // LANES) * LANES


def _in_cols(proj, lo, hi):
    parts = []
    for chip in range(lo // IN_SHARD, (hi - 1) // IN_SHARD + 1):
        a, z = max(lo, chip * IN_SHARD), min(hi, (chip + 1) * IN_SHARD)
        base = chip * IN_SHARD_PAD - chip * IN_SHARD
        parts.append(proj[:, base + a:base + z])
    return parts[0] if len(parts) == 1 else jnp.concatenate(parts, axis=-1)

ADAM_LR, ADAM_B1, ADAM_B2, ADAM_EPS, ADAM_WD, ADAM_STEP = 0.001, 0.9, 0.999, 1e-08, 0.01, 10

N_DEV = 8
TILE_BYTES = 2 * 1024 * 1024


def _pick(n, cands):
    for t in cands:
        if n % t == 0:
            return t
    return n


ADAM_TILE_BYTES = 2 * 1024 * 1024
ADAM_VMEM_LIMIT = 48 * 1024 * 1024
SWIGLU_TILE_BYTES = 4 * 1024 * 1024


def _row_tile(rows, row_bytes, limit=TILE_BYTES):
    for t in (2048, 1024, 512, 256, 128, 64, 32, 16, 8):
        if rows % t == 0 and t * row_bytes <= limit:
            return t
    return rows


MM_VMEM_BUDGET = 36 * 1024 * 1024
MM_VMEM_LIMIT = 56 * 1024 * 1024
MM_MAX_TILE = 2048
MM_MAX_K_TILE = 4096
MXU_DIM = 256
HBM_BYTES_PER_US = 3.0e6
MXU_FLOPS_PER_US = 9.0e8
STEP_US = 0.35


def _tile_cands(d, cap):
    if d % LANES:
        return [d]
    return [t for t in range(LANES, min(d, cap) + 1, LANES) if d % t == 0] or [d]


def _mm_tiles(m, n, kdim, n_unit=None, k_unit=None):
    up = lambda t: -(-t // MXU_DIM) * MXU_DIM
    best = None
    for tm in _tile_cands(m, MM_MAX_TILE):
        for tn in _tile_cands(n_unit or n, MM_MAX_TILE):
            for tk in _tile_cands(k_unit or kdim, MM_MAX_K_TILE):
                if 2 * (tm * tk * 2 + tk * tn * 2 + tm * tn * 4) > MM_VMEM_BUDGET:
                    continue
                ni, nj, nk = m // tm, n // tn, kdim // tk
                a_reads = 1 if nk == 1 else nj
                b_reads = 1 if (nk == 1 and nj == 1) else ni
                hbm = (m * kdim * 2 * a_reads + kdim * n * 2 * b_reads + m * n * 4) / HBM_BYTES_PER_US
                mxu = ni * nj * nk * 2.0 * max(tm, 8) * up(tn) * up(tk) / MXU_FLOPS_PER_US
                cost = max(hbm, mxu) + 0.25 * min(hbm, mxu) + ni * nj * nk * STEP_US
                if best is None or cost < best[0]:
                    best = (cost, tm, tn, tk)
    return best[1:]


def _ride(plan, refs, first, last, compute):
    if plan is None:
        compute()
        return
    ins, outs, sems = plan.split(refs)

    @pl.when(first)
    def _():
        plan.start(ins, outs, sems)

    compute()

    @pl.when(last)
    def _():
        plan.finish(ins, outs, sems)


def _plan_specs(plan):
    if plan is None:
        return [], [], [], [], []
    in_specs, out_specs, scratch = plan.specs()
    return in_specs, out_specs, plan.out_shapes, scratch, plan.inputs


def _matmul(a, b, ta=False, tb=False, name="mm", plan=None, chips=None):
    assert a.dtype == BF16 and b.dtype in (BF16, F32), (a.dtype, b.dtype)
    if ta:
        kdim, m = a.shape
    else:
        m, kdim = a.shape
    n_unit = k_unit = None
    if chips == 'b':
        nb, rows, unit = b.shape
        if tb:
            n, k2, k_unit = rows, nb * unit, unit
        else:
            k2, n, n_unit = rows, nb * unit, unit
    else:
        if tb:
            n, k2 = b.shape
        else:
            k2, n = b.shape
        if chips == 'out':
            n_unit = n // N_CHIPS
    assert kdim == k2, (a.shape, b.shape, ta, tb)
    tm, tn, tk = _mm_tiles(m, n, kdim, n_unit, k_unit)
    ni, nj, nk = m // tm, n // tn, kdim // tk
    dn = (((0 if ta else 1,), (1 if tb else 0,)), ((), ()))
    p_in, p_out, p_shapes, p_scratch, p_args = _plan_specs(plan)

    def body(a_ref, b_ref, *rest):
        o_ref = rest[len(p_in)]
        i, j, k = pl.program_id(0), pl.program_id(1), pl.program_id(2)

        def compute():
            bv = (b_ref[0] if chips == 'b' else b_ref[...]).astype(BF16)
            part = lax.dot_general(a_ref[...], bv, dn, preferred_element_type=F32)
            if chips == 'out':
                part = part[None]
            if nk == 1:
                o_ref[...] = part
            else:
                @pl.when(k == 0)
                def _():
                    o_ref[...] = part

                @pl.when(k > 0)
                def _():
                    o_ref[...] += part

        _ride(plan, rest[:len(p_in)] + rest[len(p_in) + 1:], (i == 0) & (j == 0) & (k == 0),
              (i == ni - 1) & (j == nj - 1) & (k == nk - 1), compute)

    a_spec = pl.BlockSpec((tk, tm), lambda i, j, k: (k, i)) if ta else pl.BlockSpec((tm, tk), lambda i, j, k: (i, k))
    if chips == 'b' and tb:
        per = k_unit // tk
        b_spec = pl.BlockSpec((1, tn, tk), lambda i, j, k: (k // per, j, k % per))
    elif chips == 'b':
        per = n_unit // tn
        b_spec = pl.BlockSpec((1, tk, tn), lambda i, j, k: (j // per, k, j % per))
    else:
        b_spec = pl.BlockSpec((tn, tk), lambda i, j, k: (j, k)) if tb else pl.BlockSpec((tk, tn), lambda i, j, k: (k, j))
    if chips == 'out':
        per = n_unit // tn
        o_spec = pl.BlockSpec((1, tm, tn), lambda i, j, k: (j // per, i, j % per))
        o_shape = jax.ShapeDtypeStruct((N_CHIPS, m, n_unit), F32)
    else:
        o_spec = pl.BlockSpec((tm, tn), lambda i, j, k: (i, j))
        o_shape = jax.ShapeDtypeStruct((m, n), F32)
    outs = pl.pallas_call(
        body, name=name, grid=(ni, nj, nk),
        in_specs=[a_spec, b_spec] + p_in, out_specs=[o_spec] + p_out,
        out_shape=[o_shape] + p_shapes, scratch_shapes=p_scratch,
        compiler_params=pltpu.CompilerParams(
            dimension_semantics=("arbitrary" if plan is not None else "parallel", "arbitrary", "arbitrary"),
            vmem_limit_bytes=MM_VMEM_LIMIT),
    )(a, b, *p_args)
    return outs[0] if plan is None else (outs[0], list(outs[1:]))


@jax.custom_vjp
def linear(x, w):
    return _matmul(x.astype(BF16), w.astype(BF16), name="linear_fwd")


def _linear_fwd(x, w):
    xb, wb = x.astype(BF16), w.astype(BF16)
    return _matmul(xb, wb, name="linear_fwd"), (xb, wb)


def _linear_bwd(res, dy):
    xb, wb = res
    dyb = dy.astype(BF16)
    return _matmul(dyb, wb, tb=True, name="linear_dx"), _matmul(xb, dyb, ta=True, name="linear_dw")


linear.defvjp(_linear_fwd, _linear_bwd)


def _rms_fwd_call(x, g, groups):
    rows, cols = x.shape
    d = cols // groups
    tr = _row_tile(rows, cols * 4)

    def body(x_ref, g_ref, y_ref):
        for gi in range(groups):
            sl = slice(gi * d, (gi + 1) * d)
            xs = x_ref[:, sl]
            r = lax.rsqrt(jnp.mean(xs * xs, axis=-1, keepdims=True) + EPS)
            y_ref[:, sl] = xs * r * g_ref[:, sl]

    return pl.pallas_call(
        body, name="rms_fwd", grid=(rows // tr,),
        in_specs=[pl.BlockSpec((tr, cols), lambda i: (i, 0)), pl.BlockSpec((1, cols), lambda i: (0, 0))],
        out_specs=pl.BlockSpec((tr, cols), lambda i: (i, 0)),
        out_shape=jax.ShapeDtypeStruct((rows, cols), F32),
        compiler_params=pltpu.CompilerParams(dimension_semantics=("parallel",)),
    )(x, g)


def _rms_bwd_call(x, g, dy, groups):
    rows, cols = x.shape
    d = cols // groups
    tr = _row_tile(rows, cols * 4)

    def body(x_ref, g_ref, dy_ref, dx_ref, dg_ref):
        @pl.when(pl.program_id(0) == 0)
        def _():
            dg_ref[...] = jnp.zeros_like(dg_ref)

        for gi in range(groups):
            sl = slice(gi * d, (gi + 1) * d)
            xs = x_ref[:, sl]
            dys = dy_ref[:, sl]
            r = lax.rsqrt(jnp.mean(xs * xs, axis=-1, keepdims=True) + EPS)
            xhat = xs * r
            dg_ref[:, sl] += jnp.sum(dys * xhat, axis=0, keepdims=True)
            dxhat = dys * g_ref[:, sl]
            dx_ref[:, sl] = r * (dxhat - xhat * jnp.mean(dxhat * xhat, axis=-1, keepdims=True))

    return pl.pallas_call(
        body, name="rms_bwd", grid=(rows // tr,),
        in_specs=[pl.BlockSpec((tr, cols), lambda i: (i, 0)), pl.BlockSpec((1, cols), lambda i: (0, 0)),
                  pl.BlockSpec((tr, cols), lambda i: (i, 0))],
        out_specs=[pl.BlockSpec((tr, cols), lambda i: (i, 0)), pl.BlockSpec((1, cols), lambda i: (0, 0))],
        out_shape=[jax.ShapeDtypeStruct((rows, cols), F32), jax.ShapeDtypeStruct((1, cols), F32)],
        compiler_params=pltpu.CompilerParams(dimension_semantics=("arbitrary",)),
    )(x, g, dy)


@functools.partial(jax.custom_vjp, nondiff_argnums=(2,))
def rms_norm(x, g, groups):
    return _rms_fwd_call(x, g, groups)


def _rms_norm_fwd(x, g, groups):
    return _rms_fwd_call(x, g, groups), (x, g)


def _rms_norm_bwd(groups, res, dy):
    x, g = res
    dx, dg = _rms_bwd_call(x, g, dy, groups)
    return dx, dg


rms_norm.defvjp(_rms_norm_fwd, _rms_norm_bwd)


NT_DIMS = (((1,), (1,)), ((), ()))
TN_DIMS = (((0,), (0,)), ((), ()))


LOG2E = 1.4426950408889634
ATTN_VMEM_LIMIT = 60 * 1024 * 1024
ATTN_SUB_ROWS = 512

def _exp_rows(q, k, scale):
    s2 = lax.dot_general(q, k, NT_DIMS, preferred_element_type=F32) * (scale * LOG2E)
    e = jnp.exp2(s2 - jnp.max(s2, axis=-1, keepdims=True))
    return e, 1.0 / jnp.sum(e, axis=-1, keepdims=True)


def _attn_fwd_call(q, k, v, scale):
    b, h, s, dk = q.shape
    hkv, dv = k.shape[1], v.shape[3]
    rep = h // hkv
    tq = _pick(s, (1024, 512, 256, 128))

    sub = min(ATTN_SUB_ROWS, tq)

    def body(q_ref, k_ref, v_ref, o_ref):
        kb, vb = k_ref[0, 0], v_ref[0, 0]
        for r in range(tq // sub):
            rows = pl.ds(r * sub, sub)
            e, inv = _exp_rows(q_ref[0, 0, rows, :], kb, scale)
            o_ref[0, 0, rows, :] = jnp.dot(e.astype(BF16), vb, preferred_element_type=F32) * inv

    return pl.pallas_call(
        body, name="attn_fwd", grid=(b, h, s // tq),
        in_specs=[pl.BlockSpec((1, 1, tq, dk), lambda bi, hi, qi: (bi, hi, qi, 0)),
                  pl.BlockSpec((1, 1, s, dk), lambda bi, hi, qi: (bi, hi // rep, 0, 0)),
                  pl.BlockSpec((1, 1, s, dv), lambda bi, hi, qi: (bi, hi // rep, 0, 0))],
        out_specs=pl.BlockSpec((1, 1, tq, dv), lambda bi, hi, qi: (bi, hi, qi, 0)),
        out_shape=jax.ShapeDtypeStruct((b, h, s, dv), F32),
        compiler_params=pltpu.CompilerParams(dimension_semantics=("parallel", "parallel", "parallel"),
                                             vmem_limit_bytes=ATTN_VMEM_LIMIT),
    )(q, k, v)


def _attn_bwd_call(q, k, v, do, scale):
    b, h, s, dk = q.shape
    hkv, dv = k.shape[1], v.shape[3]
    rep = h // hkv
    tq = _pick(s, (1024, 512, 256, 128))
    sub = min(ATTN_SUB_ROWS, tq)

    def body(q_ref, k_ref, v_ref, do_ref, dq_ref, dk_ref, dv_ref):
        @pl.when((pl.program_id(2) == 0) & (pl.program_id(3) == 0))
        def _():
            dk_ref[...] = jnp.zeros_like(dk_ref)
            dv_ref[...] = jnp.zeros_like(dv_ref)

        kb = k_ref[0, 0]
        vb = v_ref[0, 0]
        ds_parts, e_parts, do_parts = [], [], []
        for r in range(tq // sub):
            rows = pl.ds(r * sub, sub)
            dob = do_ref[0, 0, rows, :]
            e, inv = _exp_rows(q_ref[0, 0, rows, :], kb, scale)
            dp = lax.dot_general(dob, vb, NT_DIMS, preferred_element_type=F32)
            delta = jnp.sum(e * dp, axis=-1, keepdims=True) * inv
            ds = (e * ((dp - delta) * (inv * scale))).astype(BF16)
            dq_ref[0, 0, rows, :] = jnp.dot(ds, kb, preferred_element_type=F32)
            ds_parts.append(ds)
            e_parts.append(e.astype(BF16))
            do_parts.append((dob.astype(F32) * inv).astype(BF16))
        join = lambda parts: parts[0] if len(parts) == 1 else jnp.concatenate(parts, axis=0)
        dk_ref[0, 0] += lax.dot_general(join(ds_parts), q_ref[0, 0], TN_DIMS, preferred_element_type=F32)
        dv_ref[0, 0] += lax.dot_general(join(e_parts), join(do_parts), TN_DIMS, preferred_element_type=F32)

    return pl.pallas_call(
        body, name="attn_bwd", grid=(b, hkv, rep, s // tq),
        in_specs=[pl.BlockSpec((1, 1, tq, dk), lambda bi, gi, ri, qi: (bi, gi * rep + ri, qi, 0)),
                  pl.BlockSpec((1, 1, s, dk), lambda bi, gi, ri, qi: (bi, gi, 0, 0)),
                  pl.BlockSpec((1, 1, s, dv), lambda bi, gi, ri, qi: (bi, gi, 0, 0)),
                  pl.BlockSpec((1, 1, tq, dv), lambda bi, gi, ri, qi: (bi, gi * rep + ri, qi, 0))],
        out_specs=[pl.BlockSpec((1, 1, tq, dk), lambda bi, gi, ri, qi: (bi, gi * rep + ri, qi, 0)),
                   pl.BlockSpec((1, 1, s, dk), lambda bi, gi, ri, qi: (bi, gi, 0, 0)),
                   pl.BlockSpec((1, 1, s, dv), lambda bi, gi, ri, qi: (bi, gi, 0, 0))],
        out_shape=[jax.ShapeDtypeStruct(q.shape, F32), jax.ShapeDtypeStruct(k.shape, F32),
                   jax.ShapeDtypeStruct(v.shape, F32)],
        compiler_params=pltpu.CompilerParams(
            dimension_semantics=("parallel", "parallel", "arbitrary", "arbitrary"), vmem_limit_bytes=ATTN_VMEM_LIMIT),
    )(q, k, v, do)


@functools.partial(jax.custom_vjp, nondiff_argnums=(3,))
def attention(q, k, v, scale):
    return _attn_fwd_call(q.astype(BF16), k.astype(BF16), v.astype(BF16), scale)


def _attention_fwd(q, k, v, scale):
    qb, kb, vb = q.astype(BF16), k.astype(BF16), v.astype(BF16)
    return _attn_fwd_call(qb, kb, vb, scale), (qb, kb, vb)


def _attention_bwd(scale, res, do):
    qb, kb, vb = res
    return tuple(_attn_bwd_call(qb, kb, vb, do.astype(BF16), scale))


attention.defvjp(_attention_fwd, _attention_bwd)


CONV_COL_TILE = 256
CONV_PACK_ROWS = 8


def _shifted(x, off, rows):
    if off == 0:
        return x
    s = x.shape[0]
    rolled = pltpu.roll(x, (-off) % s, 0)
    valid = (rows + off >= 0) & (rows + off < s)
    return jnp.where(valid, rolled, 0.0)


def _conv_pre(x, wb_ref, rows):
    z = jnp.zeros_like(x) + wb_ref[SSD_CONV:SSD_CONV + 1, :]
    for j in range(SSD_CONV):
        z = z + wb_ref[j:j + 1, :] * _shifted(x, j - SSD_CONV // 2, rows)
    return z


def _conv_fwd_call(x, wb):
    b, s, c = x.shape
    tc = _pick(c, (CONV_COL_TILE, LANES))

    def body(x_ref, wb_ref, y_ref):
        xv = x_ref[0]
        rows = lax.broadcasted_iota(jnp.int32, xv.shape, 0)
        z = _conv_pre(xv, wb_ref, rows)
        y_ref[0] = z * jax.nn.sigmoid(z)

    return pl.pallas_call(
        body, name="conv_fwd", grid=(b, c // tc),
        in_specs=[pl.BlockSpec((1, s, tc), lambda bi, ci: (bi, 0, ci)),
                  pl.BlockSpec((CONV_PACK_ROWS, tc), lambda bi, ci: (0, ci))],
        out_specs=pl.BlockSpec((1, s, tc), lambda bi, ci: (bi, 0, ci)),
        out_shape=jax.ShapeDtypeStruct(x.shape, F32),
        compiler_params=pltpu.CompilerParams(dimension_semantics=("parallel", "parallel")),
    )(x, wb)


def _conv_bwd_call(x, wb, dy):
    b, s, c = x.shape
    tc = _pick(c, (CONV_COL_TILE, LANES))

    def body(x_ref, wb_ref, dy_ref, dx_ref, dwb_ref):
        xv = x_ref[0]
        rows = lax.broadcasted_iota(jnp.int32, xv.shape, 0)
        z = _conv_pre(xv, wb_ref, rows)
        sg = jax.nn.sigmoid(z)
        dz = dy_ref[0] * (sg * (1.0 + z * (1.0 - sg)))
        dx = jnp.zeros_like(xv)
        for j in range(SSD_CONV):
            off = j - SSD_CONV // 2
            dx = dx + wb_ref[j:j + 1, :] * _shifted(dz, -off, rows)
            dwb_ref[0, j:j + 1, :] = jnp.sum(dz * _shifted(xv, off, rows), axis=0, keepdims=True)
        dx_ref[0] = dx
        dwb_ref[0, SSD_CONV:SSD_CONV + 1, :] = jnp.sum(dz, axis=0, keepdims=True)
        dwb_ref[0, SSD_CONV + 1:, :] = jnp.zeros((CONV_PACK_ROWS - SSD_CONV - 1, dz.shape[1]), F32)

    return pl.pallas_call(
        body, name="conv_bwd", grid=(b, c // tc),
        in_specs=[pl.BlockSpec((1, s, tc), lambda bi, ci: (bi, 0, ci)),
                  pl.BlockSpec((CONV_PACK_ROWS, tc), lambda bi, ci: (0, ci)),
                  pl.BlockSpec((1, s, tc), lambda bi, ci: (bi, 0, ci))],
        out_specs=[pl.BlockSpec((1, s, tc), lambda bi, ci: (bi, 0, ci)),
                   pl.BlockSpec((1, CONV_PACK_ROWS, tc), lambda bi, ci: (bi, 0, ci))],
        out_shape=[jax.ShapeDtypeStruct(x.shape, F32), jax.ShapeDtypeStruct((b, CONV_PACK_ROWS, c), F32)],
        compiler_params=pltpu.CompilerParams(dimension_semantics=("parallel", "parallel")),
    )(x, wb, dy)


@jax.custom_vjp
def conv_silu(x, wb):
    return _conv_fwd_call(x, wb)


def _conv_silu_fwd(x, wb):
    return _conv_fwd_call(x, wb), (x, wb)


def _conv_silu_bwd(res, dy):
    x, wb = res
    dx, dwb = _conv_bwd_call(x, wb, dy)
    return dx, jnp.sum(dwb, axis=0)


conv_silu.defvjp(_conv_silu_fwd, _conv_silu_bwd)


SSD_PAIRS = SSD_GROUP_HEADS // 2
NEG_INF = -1e30


def _ssd_common(x_ref, dtx_ref, dtt_ref, anx_ref, anc_ref, b_ref, c_ref, reverse):
    L = SSD_CHUNK
    xv = x_ref[0]
    dt = dtx_ref[0]
    ri = lax.broadcasted_iota(jnp.int32, (L, L), 0)
    ci = lax.broadcasted_iota(jnp.int32, (L, L), 1)
    causal = (ri <= ci) if reverse else (ri >= ci)
    tri = causal.astype(F32)
    a_cs = jnp.dot(tri, dt * anx_ref[...], precision=HIGHEST, preferred_element_type=F32)
    a_row = dtt_ref[0, 0] * anc_ref[0]
    acs_row = lax.dot_general(a_row, tri, NT_DIMS, precision=HIGHEST, preferred_element_type=F32)
    xd = xv * dt
    bmat = b_ref[0].astype(BF16)
    cmat = c_ref[0].astype(BF16)
    gmat = lax.dot_general(cmat, bmat, NT_DIMS, preferred_element_type=F32)
    return xv, dt, causal, tri, a_cs, acs_row, xd, bmat, cmat, gmat


def _ssd_lambda(a_cs, acs_row, causal, h):
    col = a_cs[:, h * SSD_HEAD_DIM:h * SSD_HEAD_DIM + 1]
    row = acs_row[h:h + 1, :]
    return jnp.exp(jnp.where(causal, col - row, NEG_INF))


def _ssd_fwd_call(x, dtx, dtt, anx, anc, bm, cm, reverse):
    b, s, _ = x.shape
    L, N, GW = SSD_CHUNK, SSD_STATE, SSD_GROUP_WIDTH
    nc = s // L
    end = 0 if reverse else L - 1

    def body(x_ref, dtx_ref, dtt_ref, anx_ref, anc_ref, b_ref, c_ref, y_ref, hs_ref, state):
        @pl.when(pl.program_id(2) == 0)
        def _():
            state[...] = jnp.zeros_like(state)

        xv, dt, causal, tri, a_cs, acs_row, xd, bmat, cmat, gmat = _ssd_common(
            x_ref, dtx_ref, dtt_ref, anx_ref, anc_ref, b_ref, c_ref, reverse)
        hin = state[...]
        hs_ref[0, 0, 0] = hin
        y_off = jnp.dot(cmat, hin.astype(BF16), preferred_element_type=F32) * jnp.exp(a_cs)
        a_end = a_cs[end:end + 1, :]
        s_new = lax.dot_general(bmat, (xd * jnp.exp(a_end - a_cs)).astype(BF16), TN_DIMS, preferred_element_type=F32)
        state[...] = jnp.exp(a_end) * hin + s_new
        lane = lax.broadcasted_iota(jnp.int32, (L, LANES), 1)
        for pr in range(SSD_PAIRS):
            sl = slice(pr * LANES, (pr + 1) * LANES)
            xdp = xd[:, sl].astype(BF16)
            w0 = (gmat * _ssd_lambda(a_cs, acs_row, causal, 2 * pr)).astype(BF16)
            w1 = (gmat * _ssd_lambda(a_cs, acs_row, causal, 2 * pr + 1)).astype(BF16)
            y0 = jnp.dot(w0, xdp, preferred_element_type=F32)
            y1 = jnp.dot(w1, xdp, preferred_element_type=F32)
            y_ref[0, :, sl] = jnp.where(lane < SSD_HEAD_DIM, y0, y1) + y_off[:, sl]

    G = SSD_GROUPS
    chunk = (lambda c: nc - 1 - c) if reverse else (lambda c: c)
    seq = lambda bi, gi, c: (bi, chunk(c), gi)
    return pl.pallas_call(
        body, name="ssd_fwd", grid=(b, G, nc),
        in_specs=[pl.BlockSpec((1, L, GW), seq),
                  pl.BlockSpec((1, L, GW), seq),
                  pl.BlockSpec((1, 1, SSD_GROUP_HEADS, L), lambda bi, gi, c: (bi, gi, 0, chunk(c))),
                  pl.BlockSpec((1, GW), lambda bi, gi, c: (0, gi)),
                  pl.BlockSpec((1, SSD_GROUP_HEADS, 1), lambda bi, gi, c: (gi, 0, 0)),
                  pl.BlockSpec((1, L, N), seq),
                  pl.BlockSpec((1, L, N), seq)],
        out_specs=[pl.BlockSpec((1, L, GW), seq),
                   pl.BlockSpec((1, 1, 1, N, GW), lambda bi, gi, c: (bi, gi, chunk(c), 0, 0))],
        out_shape=[jax.ShapeDtypeStruct(x.shape, F32), jax.ShapeDtypeStruct((b, G, nc, N, GW), F32)],
        scratch_shapes=[pltpu.VMEM((N, GW), F32)],
        compiler_params=pltpu.CompilerParams(dimension_semantics=("parallel", "parallel", "arbitrary")),
    )(x, dtx, dtt, anx, anc, bm, cm)


def _ssd_bwd_call(x, dtx, dtt, anx, anc, bm, cm, hs, dy, reverse):
    b, s, _ = x.shape
    L, N, GW = SSD_CHUNK, SSD_STATE, SSD_GROUP_WIDTH
    nc = s // L
    end = 0 if reverse else L - 1

    def body(x_ref, dtx_ref, dtt_ref, anx_ref, anc_ref, b_ref, c_ref, hs_ref, dy_ref,
             dx_ref, ddt_ref, dan_ref, db_ref, dc_ref, dstate):
        @pl.when(pl.program_id(2) == 0)
        def _():
            dstate[...] = jnp.zeros_like(dstate)

        xv, dt, causal, tri, a_cs, acs_row, xd, bmat, cmat, gmat = _ssd_common(
            x_ref, dtx_ref, dtt_ref, anx_ref, anc_ref, b_ref, c_ref, reverse)
        hin = hs_ref[0, 0, 0]
        hinb = hin.astype(BF16)
        dyv = dy_ref[0]
        ds_out = dstate[...]
        dsb = ds_out.astype(BF16)
        eacs = jnp.exp(a_cs)
        a_end = a_cs[end:end + 1, :]
        e_end = jnp.exp(a_end)
        dec = jnp.exp(a_end - a_cs)
        dye = dyv * eacs
        dyeb = dye.astype(BF16)
        xdec = xd * dec
        ch = jnp.dot(cmat, hinb, preferred_element_type=F32)
        bds = jnp.dot(bmat, dsb, preferred_element_type=F32)
        t_state = xdec * bds
        d_aend = jnp.sum(t_state, axis=0, keepdims=True) + e_end * jnp.sum(ds_out * hin, axis=0, keepdims=True)
        dacs = dye * ch - t_state
        dxd_state = bds * dec
        dstate[...] = e_end * ds_out + lax.dot_general(cmat, dyeb, TN_DIMS, preferred_element_type=F32)

        lane = lax.broadcasted_iota(jnp.int32, (L, LANES), 1)
        dg = jnp.zeros((L, L), F32)
        dxd_parts, dacs_parts = [], []
        for pr in range(SSD_PAIRS):
            sl = slice(pr * LANES, (pr + 1) * LANES)
            xdp = xd[:, sl]
            dyp = dyv[:, sl]
            dxd_p = jnp.zeros((L, LANES), F32)
            dacs_p = jnp.zeros((L, LANES), F32)
            for half in range(2):
                mine = (lane < SSD_HEAD_DIM) if half == 0 else (lane >= SSD_HEAD_DIM)
                lam = _ssd_lambda(a_cs, acs_row, causal, 2 * pr + half)
                w = gmat * lam
                xdh = jnp.where(mine, xdp, 0.0).astype(BF16)
                dyh = jnp.where(mine, dyp, 0.0).astype(BF16)
                dw = lax.dot_general(dyh, xdh, NT_DIMS, preferred_element_type=F32)
                dg = dg + dw * lam
                mm = dw * w
                rs = jnp.sum(mm, axis=1, keepdims=True)
                cs = jnp.sum(mm.T, axis=1, keepdims=True)
                dacs_p = dacs_p + jnp.where(mine, (rs - cs) * (1.0 / SSD_HEAD_DIM), 0.0)
                wtdy = lax.dot_general(w.astype(BF16), dyh, TN_DIMS, preferred_element_type=F32)
                dxd_p = dxd_p + wtdy
            dxd_parts.append(dxd_p)
            dacs_parts.append(dacs_p)
        dxd = jnp.concatenate(dxd_parts, axis=1) + dxd_state
        dacs = dacs + jnp.concatenate(dacs_parts, axis=1)
        last = lax.broadcasted_iota(jnp.int32, dacs.shape, 0) == end
        dacs = dacs + jnp.where(last, d_aend, 0.0)
        da = lax.dot_general(tri, dacs, TN_DIMS, precision=HIGHEST, preferred_element_type=F32)
        dgb = dg.astype(BF16)
        dc_ref[0] = (jnp.dot(dgb, bmat, preferred_element_type=F32)
                     + lax.dot_general(dyeb, hinb, NT_DIMS, preferred_element_type=F32))
        db_ref[0] = (lax.dot_general(dgb, cmat, TN_DIMS, preferred_element_type=F32)
                     + lax.dot_general(xdec.astype(BF16), dsb, NT_DIMS, preferred_element_type=F32))
        dx_ref[0] = dxd * dt
        ddt_ref[0] = da * anx_ref[...] + dxd * xv
        dan_ref[0, 0, 0] = jnp.sum(da * dt, axis=0, keepdims=True)

    G = SSD_GROUPS
    chunk = (lambda c: c) if reverse else (lambda c: nc - 1 - c)
    rev = lambda bi, gi, c: (bi, chunk(c), gi)
    return pl.pallas_call(
        body, name="ssd_bwd", grid=(b, G, nc),
        in_specs=[pl.BlockSpec((1, L, GW), rev),
                  pl.BlockSpec((1, L, GW), rev),
                  pl.BlockSpec((1, 1, SSD_GROUP_HEADS, L), lambda bi, gi, c: (bi, gi, 0, chunk(c))),
                  pl.BlockSpec((1, GW), lambda bi, gi, c: (0, gi)),
                  pl.BlockSpec((1, SSD_GROUP_HEADS, 1), lambda bi, gi, c: (gi, 0, 0)),
                  pl.BlockSpec((1, L, N), rev),
                  pl.BlockSpec((1, L, N), rev),
                  pl.BlockSpec((1, 1, 1, N, GW), lambda bi, gi, c: (bi, gi, chunk(c), 0, 0)),
                  pl.BlockSpec((1, L, GW), rev)],
        out_specs=[pl.BlockSpec((1, L, GW), rev),
                   pl.BlockSpec((1, L, GW), rev),
                   pl.BlockSpec((1, 1, 1, 1, GW), lambda bi, gi, c: (bi, gi, chunk(c), 0, 0)),
                   pl.BlockSpec((1, L, N), rev),
                   pl.BlockSpec((1, L, N), rev)],
        out_shape=[jax.ShapeDtypeStruct(x.shape, F32), jax.ShapeDtypeStruct(x.shape, F32),
                   jax.ShapeDtypeStruct((b, G, nc, 1, GW), F32),
                   jax.ShapeDtypeStruct(bm.shape, F32), jax.ShapeDtypeStruct(cm.shape, F32)],
        scratch_shapes=[pltpu.VMEM((N, GW), F32)],
        compiler_params=pltpu.CompilerParams(dimension_semantics=("parallel", "parallel", "arbitrary")),
    )(x, dtx, dtt, anx, anc, bm, cm, hs, dy)


@functools.partial(jax.custom_vjp, nondiff_argnums=(7,))
def _ssd_scan(x, dtx, dtt, anx, anc, bm, cm, reverse):
    return _ssd_fwd_call(x, dtx, dtt, anx, anc, bm, cm, reverse)[0]


def _ssd_scan_fwd(x, dtx, dtt, anx, anc, bm, cm, reverse):
    y, hs = _ssd_fwd_call(x, dtx, dtt, anx, anc, bm, cm, reverse)
    return y, (x, dtx, dtt, anx, anc, bm, cm, hs)


def _ssd_scan_bwd(reverse, res, dy):
    x, dtx, dtt, anx, anc, bm, cm, hs = res
    dx, ddtx, dan, db, dc = _ssd_bwd_call(x, dtx, dtt, anx, anc, bm, cm, hs, dy, reverse)
    b, g, nc, _, gw = dan.shape
    danx = jnp.sum(dan, axis=(0, 2, 3)).reshape(1, g * gw)
    return dx, ddtx, jnp.zeros_like(dtt), danx, jnp.zeros_like(anc), db, dc


_ssd_scan.defvjp(_ssd_scan_fwd, _ssd_scan_bwd)


def ssd_chunked(xs, dt, a_neg, bm, cm, reverse):
    b, s, _ = xs.shape
    dtx = jnp.repeat(dt, SSD_HEAD_DIM, axis=-1)
    dtt = jnp.transpose(dt, (0, 2, 1)).reshape(b, SSD_GROUPS, SSD_GROUP_HEADS, s)
    anx = jnp.repeat(a_neg, SSD_HEAD_DIM)[None, :]
    anc = a_neg.reshape(SSD_GROUPS, SSD_GROUP_HEADS, 1)
    return _ssd_scan(xs, dtx, dtt, anx, anc, bm, cm, reverse)


def _loss_call(y, t):
    rows, cols = y.shape
    tr = _row_tile(rows, cols * 4)

    def body(y_ref, t_ref, loss_ref, diff_ref):
        @pl.when(pl.program_id(0) == 0)
        def _():
            loss_ref[...] = jnp.zeros_like(loss_ref)

        d = y_ref[...] - t_ref[...]
        diff_ref[...] = d * (1.0 / cols)
        part = jnp.sum(jnp.sum(d * d, axis=1, keepdims=True), axis=0, keepdims=True)
        loss_ref[...] += part * (0.5 / cols)

    return pl.pallas_call(
        body, name="loss_head", grid=(rows // tr,),
        in_specs=[pl.BlockSpec((tr, cols), lambda i: (i, 0)), pl.BlockSpec((tr, cols), lambda i: (i, 0))],
        out_specs=[pl.BlockSpec((1, 1), lambda i: (0, 0)), pl.BlockSpec((tr, cols), lambda i: (i, 0))],
        out_shape=[jax.ShapeDtypeStruct((1, 1), F32), jax.ShapeDtypeStruct((rows, cols), F32)],
        compiler_params=pltpu.CompilerParams(dimension_semantics=("arbitrary",)),
    )(y, t)


@jax.custom_vjp
def loss_head(y, t):
    return _loss_call(y, t)[0][0, 0]


def _loss_head_fwd(y, t):
    loss, diff = _loss_call(y, t)
    return loss[0, 0], diff


def _loss_head_bwd(diff, g):
    return g * diff, jnp.zeros_like(diff)


loss_head.defvjp(_loss_head_fwd, _loss_head_bwd)


def _axial_rope_tables(seq_len, rot_dim):
    rows = seq_len // GRID_W
    row_idx = jnp.repeat(jnp.arange(rows), GRID_W).astype(F32)
    col_idx = jnp.tile(jnp.arange(GRID_W), rows).astype(F32)
    axis_dim = rot_dim // 2
    inv_freq = jnp.power(ROPE_THETA, -jnp.arange(0, axis_dim, 2, dtype=F32) / axis_dim)
    ang_r = row_idx[:, None] * inv_freq[None, :]
    ang_c = col_idx[:, None] * inv_freq[None, :]
    return jnp.cos(ang_r), jnp.sin(ang_r), jnp.cos(ang_c), jnp.sin(ang_c)


def _rotate(x, cos, sin):
    x1, x2 = jnp.split(x, 2, axis=-1)
    cos = cos[:, None, :]
    sin = sin[:, None, :]
    return jnp.concatenate([x1 * cos - x2 * sin, x1 * sin + x2 * cos], axis=-1)


def _apply_axial_rope(x, tables):
    cos_r, sin_r, cos_c, sin_c = tables
    x_row, x_col = jnp.split(x, 2, axis=-1)
    return jnp.concatenate([_rotate(x_row, cos_r, sin_r), _rotate(x_col, cos_c, sin_c)], axis=-1)


def _heads_first(t):
    return jnp.transpose(t, (0, 2, 1, 3))


def _gqa_group(q, k, v, q_norm_g, k_norm_g, rope, b, s):
    q = rms_norm(q, jnp.tile(q_norm_g, GQA_HEADS)[None, :], GQA_HEADS).reshape(b, s, GQA_HEADS, GQA_HEAD_DIM)
    k = rms_norm(k, jnp.tile(k_norm_g, GQA_KV_HEADS)[None, :], GQA_KV_HEADS).reshape(b, s, GQA_KV_HEADS, GQA_HEAD_DIM)
    v = v.reshape(b, s, GQA_KV_HEADS, GQA_HEAD_DIM)
    q = _apply_axial_rope(q, rope)
    k = _apply_axial_rope(k, rope)
    o = attention(_heads_first(q), _heads_first(k), _heads_first(v), GQA_HEAD_DIM ** -0.5)
    return _heads_first(o).reshape(b * s, GQA_WIDTH)


def _mla_group(c_q, c_kv, k_pe, q_norm_g, w_uq, kv_norm_g, w_ukv, rope, b, s):
    q = linear(rms_norm(c_q, q_norm_g[None, :], 1), w_uq).reshape(b, s, MLA_HEADS, MLA_NOPE_DIM + MLA_ROPE_DIM)
    q_nope, q_pe = q[..., :MLA_NOPE_DIM], q[..., MLA_NOPE_DIM:]
    kv = linear(rms_norm(c_kv, kv_norm_g[None, :], 1), w_ukv).reshape(b, s, MLA_HEADS, MLA_NOPE_DIM + MLA_V_DIM)
    k_nope, v = kv[..., :MLA_NOPE_DIM], kv[..., MLA_NOPE_DIM:]
    q_pe = _apply_axial_rope(q_pe, rope)
    k_pe = _apply_axial_rope(k_pe.reshape(b, s, 1, MLA_ROPE_DIM), rope)
    q = jnp.concatenate([q_nope, q_pe], axis=-1)
    k = jnp.concatenate([k_nope, jnp.broadcast_to(k_pe, (b, s, MLA_HEADS, MLA_ROPE_DIM))], axis=-1)
    o = attention(_heads_first(q), _heads_first(k), _heads_first(v), (MLA_NOPE_DIM + MLA_ROPE_DIM) ** -0.5)
    return _heads_first(o).reshape(b * s, MLA_HEADS * MLA_V_DIM)


def _ssd_group(z, xbc, dt_raw, conv_w, conv_b, dt_bias, a_log, d_skip, norm_g, b, s):
    wb = jnp.concatenate([conv_w, conv_b[None, :], jnp.zeros((CONV_PACK_ROWS - SSD_CONV - 1, SSD_CONV_DIM), F32)], axis=0)
    xbc = conv_silu(xbc.reshape(b, s, SSD_CONV_DIM), wb)
    xs = xbc[..., :SSD_INNER]
    bm = xbc[..., SSD_INNER:SSD_INNER + SSD_GROUPS * SSD_STATE]
    cm = xbc[..., SSD_INNER + SSD_GROUPS * SSD_STATE:]
    dt = jax.nn.softplus(dt_raw.reshape(b, s, 2, SSD_HEADS) + dt_bias)
    a_neg = -jnp.exp(a_log)
    y_fwd = ssd_chunked(xs, dt[:, :, 0], a_neg[0], bm, cm, False)
    y_bwd = ssd_chunked(xs, dt[:, :, 1], a_neg[1], bm, cm, True)
    y = y_fwd + y_bwd + xs * jnp.repeat(d_skip, SSD_HEAD_DIM)
    y = y.reshape(b * s, SSD_INNER) * jax.nn.silu(z)
    return rms_norm(y, norm_g[None, :], SSD_GROUPS)


MIXER_WEIGHTS = ('q_norm_g', 'k_norm_g', 'mla_q_norm_g', 'w_uq', 'mla_kv_norm_g', 'w_ukv', 'conv_w', 'conv_b',
                 'dt_bias', 'a_log', 'd_skip', 'ssd_norm_g')


def _mixer(proj, w, rope_a, rope_b, b, s):
    idx = np.cumsum(IN_SPLITS).tolist()
    q_a, k_a, v_a, cq_b, ckv_b, kpe_b, z_c, xbc_c, dt_c = [_in_cols(proj, lo, hi)
                                                           for lo, hi in zip([0] + idx[:-1], idx)]
    o_a = _gqa_group(q_a, k_a, v_a, w["q_norm_g"], w["k_norm_g"], rope_a, b, s)
    o_b = _mla_group(cq_b, ckv_b, kpe_b, w["mla_q_norm_g"], w["w_uq"], w["mla_kv_norm_g"], w["w_ukv"], rope_b, b, s)
    o_c = _ssd_group(z_c, xbc_c, dt_c, w["conv_w"], w["conv_b"], w["dt_bias"], w["a_log"], w["d_skip"],
                     w["ssd_norm_g"], b, s)
    return jnp.concatenate([o_a, o_b, o_c], axis=-1)


def _seq_tile(s, row_bytes):
    return _row_tile(s, row_bytes)


def _normmod_fwd(x, g, scale, shift):
    b, s, d = x.shape
    tr = _seq_tile(s, d * 4)

    def body(x_ref, g_ref, sc_ref, sh_ref, h_ref):
        xv = x_ref[0]
        r = lax.rsqrt(jnp.mean(xv * xv, axis=-1, keepdims=True) + EPS)
        h_ref[0] = (xv * r * g_ref[...] * (1.0 + sc_ref[0]) + sh_ref[0]).astype(BF16)

    act = pl.BlockSpec((1, tr, d), lambda bi, i: (bi, i, 0))
    vec = pl.BlockSpec((1, 1, d), lambda bi, i: (bi, 0, 0))
    return pl.pallas_call(
        body, name="normmod_fwd", grid=(b, s // tr),
        in_specs=[act, pl.BlockSpec((1, d), lambda bi, i: (0, 0)), vec, vec], out_specs=act,
        out_shape=jax.ShapeDtypeStruct((b, s, d), BF16),
        compiler_params=pltpu.CompilerParams(dimension_semantics=("parallel", "parallel")),
    )(x, g, scale, shift)


def _normmod_bwd(x, g, scale, dh, resid):
    b, s, d = x.shape
    tr = _seq_tile(s, d * 4)

    def body(x_ref, g_ref, sc_ref, dh_ref, res_ref, dx_ref, dg_ref, dsc_ref, dsh_ref):
        bi, i = pl.program_id(0), pl.program_id(1)

        @pl.when((bi == 0) & (i == 0))
        def _():
            dg_ref[...] = jnp.zeros_like(dg_ref)

        @pl.when(i == 0)
        def _():
            dsc_ref[...] = jnp.zeros_like(dsc_ref)
            dsh_ref[...] = jnp.zeros_like(dsh_ref)

        xv = x_ref[0]
        dhv = dh_ref[0]
        gv = g_ref[...]
        r = lax.rsqrt(jnp.mean(xv * xv, axis=-1, keepdims=True) + EPS)
        xhat = xv * r
        dsh_ref[0] += jnp.sum(dhv, axis=0, keepdims=True)
        dsc_ref[0] += jnp.sum(dhv * (xhat * gv), axis=0, keepdims=True)
        dn = dhv * (1.0 + sc_ref[0])
        dg_ref[...] += jnp.sum(dn * xhat, axis=0, keepdims=True)
        dxhat = dn * gv
        dx_ref[0] = r * (dxhat - xhat * jnp.mean(dxhat * xhat, axis=-1, keepdims=True)) + res_ref[0]

    act = pl.BlockSpec((1, tr, d), lambda bi, i: (bi, i, 0))
    vec = pl.BlockSpec((1, 1, d), lambda bi, i: (bi, 0, 0))
    gain = pl.BlockSpec((1, d), lambda bi, i: (0, 0))
    return pl.pallas_call(
        body, name="normmod_bwd", grid=(b, s // tr),
        in_specs=[act, gain, vec, act, act], out_specs=[act, gain, vec, vec],
        out_shape=[jax.ShapeDtypeStruct((b, s, d), F32), jax.ShapeDtypeStruct((1, d), F32),
                   jax.ShapeDtypeStruct((b, 1, d), F32), jax.ShapeDtypeStruct((b, 1, d), F32)],
        compiler_params=pltpu.CompilerParams(dimension_semantics=("arbitrary", "arbitrary")),
    )(x, g, scale, dh, resid)


def _gated_add(x, gate, t):
    b, s, d = x.shape
    tr = _seq_tile(s, d * 4)

    def body(x_ref, g_ref, t_ref, o_ref):
        o_ref[0] = x_ref[0] + g_ref[0] * t_ref[0]

    act = pl.BlockSpec((1, tr, d), lambda bi, i: (bi, i, 0))
    vec = pl.BlockSpec((1, 1, d), lambda bi, i: (bi, 0, 0))
    return pl.pallas_call(
        body, name="gated_add", grid=(b, s // tr), in_specs=[act, vec, act], out_specs=act,
        out_shape=jax.ShapeDtypeStruct((b, s, d), F32),
        compiler_params=pltpu.CompilerParams(dimension_semantics=("parallel", "parallel")),
    )(x, gate, t)


def _gated_bwd(dy, gate, t):
    b, s, d = dy.shape
    tr = _seq_tile(s, d * 4)

    def body(dy_ref, g_ref, t_ref, dt_ref, dgate_ref):
        @pl.when(pl.program_id(1) == 0)
        def _():
            dgate_ref[...] = jnp.zeros_like(dgate_ref)

        dyv = dy_ref[0]
        dt_ref[0] = (g_ref[0] * dyv).astype(BF16)
        dgate_ref[0] += jnp.sum(dyv * t_ref[0], axis=0, keepdims=True)

    act = pl.BlockSpec((1, tr, d), lambda bi, i: (bi, i, 0))
    vec = pl.BlockSpec((1, 1, d), lambda bi, i: (bi, 0, 0))
    return pl.pallas_call(
        body, name="gated_bwd", grid=(b, s // tr), in_specs=[act, vec, act], out_specs=[act, vec],
        out_shape=[jax.ShapeDtypeStruct((b, s, d), BF16), jax.ShapeDtypeStruct((b, 1, d), F32)],
        compiler_params=pltpu.CompilerParams(dimension_semantics=("parallel", "arbitrary")),
    )(dy, gate, t)


def _swiglu_fwd(gu, plan=None):
    rows, f2 = gu.shape
    f = f2 // 2
    tr = _row_tile(rows, f2 * 4, SWIGLU_TILE_BYTES)
    steps = rows // tr
    p_in, p_out, p_shapes, p_scratch, p_args = _plan_specs(plan)

    def body(gu_ref, *rest):
        a_ref = rest[len(p_in)]
        i = pl.program_id(0)

        def compute():
            gt = gu_ref[:, :f]
            a_ref[...] = (gt * jax.nn.sigmoid(gt) * gu_ref[:, f:]).astype(BF16)

        _ride(plan, rest[:len(p_in)] + rest[len(p_in) + 1:], i == 0, i == steps - 1, compute)

    outs = pl.pallas_call(
        body, name="swiglu_fwd", grid=(steps,),
        in_specs=[pl.BlockSpec((tr, f2), lambda i: (i, 0))] + p_in,
        out_specs=[pl.BlockSpec((tr, f), lambda i: (i, 0))] + p_out,
        out_shape=[jax.ShapeDtypeStruct((rows, f), BF16)] + p_shapes, scratch_shapes=p_scratch,
        compiler_params=pltpu.CompilerParams(dimension_semantics=("arbitrary" if plan is not None else "parallel",)),
    )(gu, *p_args)
    return outs[0] if plan is None else (outs[0], list(outs[1:]))


def _swiglu_bwd(gu, dact):
    rows, f2 = gu.shape
    f = f2 // 2
    tr = _row_tile(rows, f2 * 4, SWIGLU_TILE_BYTES)

    def body(gu_ref, da_ref, dgu_ref):
        gt = gu_ref[:, :f]
        up = gu_ref[:, f:]
        da = da_ref[...]
        sg = jax.nn.sigmoid(gt)
        dgu_ref[:, :f] = (da * up * (sg * (1.0 + gt * (1.0 - sg)))).astype(BF16)
        dgu_ref[:, f:] = (da * gt * sg).astype(BF16)

    return pl.pallas_call(
        body, name="swiglu_bwd", grid=(rows // tr,),
        in_specs=[pl.BlockSpec((tr, f2), lambda i: (i, 0)), pl.BlockSpec((tr, f), lambda i: (i, 0))],
        out_specs=pl.BlockSpec((tr, f2), lambda i: (i, 0)),
        out_shape=jax.ShapeDtypeStruct((rows, f2), BF16),
        compiler_params=pltpu.CompilerParams(dimension_semantics=("parallel",)),
    )(gu, dact)


class _Gathered:
    def __init__(self, shards):
        self.shards, self.full = shards, {}

    def plan(self, keys):
        return _gather_plan([self.shards[k] for k in keys])

    def store(self, keys, outs):
        for key, out in zip(keys, outs):
            name = key[0]
            g = out.reshape((N_CHIPS,) + self.shards[key].shape)
            if name in CHIP_BLOCKED:
                full = g
            elif name in COL_SHARDED and name not in TRANSPOSED:
                full = _cols_full(g).astype(F32)
            else:
                full = g.reshape(g.shape[0] * g.shape[1], g.shape[2])
            self.full[key] = full

    def carry(self, keys, fn):
        if not keys:
            return fn(None)
        res, outs = fn(self.plan(keys))
        self.store(keys, outs)
        return res


def _gather_schedule(depth):
    every = [(n, l) for l in range(depth) for n in ('w_uq', 'w_ukv')]
    sched = {'first': [('w_in', 0)] + every}
    for l in range(depth):
        sched[('w_in_fwd', l)] = ([('w_gate_up', l)] if l == 0 else []) + [('w_out', l)]
        if l == 0:
            sched[('swiglu_fwd', l)] = [('w_down', l)]
        if l + 1 < depth:
            sched[('w_out_fwd', l)] = [('w_in', l + 1)]
            sched[('w_gate_up_fwd', l)] = [('w_gate_up', l + 1)]
            sched[('w_down_fwd', l)] = [('w_down', l + 1)]
    return sched


GATE_UP_PIECES = (1, 1, 3, 3)


class _Reducer:
    def __init__(self):
        self.parts, self.recv, self.result = {}, {}, {}

    def add(self, items):
        blocks = []
        for (name, _), grad in items:
            if name in CHIP_BLOCKED:
                blocks.append(grad)
            elif name in COL_SHARDED and name not in TRANSPOSED:
                blocks.append(_cols_split(grad))
            else:
                blocks.append(grad.reshape(N_CHIPS, grad.shape[0] // N_CHIPS, grad.shape[1]))
        for (key, _), parts in zip(items, _rs_parts(blocks)):
            self.parts[key] = parts
            self.recv[key] = []

    def pieces(self, key):
        rows = self.parts[key].shape[1]
        shares = GATE_UP_PIECES if key[0] == 'w_gate_up' else (1,)
        unit = rows // sum(shares)
        starts = np.cumsum((0,) + shares[:-1])
        return [(key, int(a) * unit, n * unit) for a, n in zip(starts, shares)]

    def plan(self, jobs):
        return _chip_exchange_plan([(self.parts[key], row0, rows) for key, row0, rows in jobs])

    def store(self, jobs, outs):
        complete = []
        for (key, row0, rows), out in zip(jobs, outs):
            self.recv[key].append((row0, out))
            if len(self.recv[key]) == len(self.pieces(key)):
                complete.append(key)
        if complete:
            items = [(self.parts[key], sorted(self.recv[key], key=lambda t: t[0])) for key in complete]
            self.result.update(zip(complete, _rs_result(items)))

    def carry(self, keys, fn, piece=None, also=()):
        jobs = [j for key in keys for j in self.pieces(key)]
        if piece is not None:
            jobs = [j for key in keys for j in self.pieces(key)[piece:piece + 1]]
        jobs += [j for key in also for j in self.pieces(key)]
        if not jobs:
            return fn(None)
        res, outs = fn(self.plan(jobs))
        self.store(jobs, outs)
        return res

    def flush(self):
        jobs = [j for key in self.parts for j in self.pieces(key)
                if key not in self.result and j[1] not in [r for r, _ in self.recv[key]]]
        if jobs:
            self.store(jobs, _run_plan(self.plan(jobs), "rs_chip_exchange"))


def _layer_fwd(x, mod, w, gathered, l, sched, rope_a, rope_b):
    b, s, d = x.shape
    m = b * s
    shift1, scale1, gate1, shift2, scale2, gate2 = [t[:, None, :] for t in jnp.split(mod, 6, axis=-1)]
    g1, g2 = w["norm1_g"][None, :], w["norm2_g"][None, :]
    full = lambda n: gathered.full[(n, l)]
    h1 = _normmod_fwd(x, g1, scale1, shift1).reshape(m, d)
    proj = gathered.carry(sched.get(('w_in_fwd', l)), lambda p: _matmul(h1, full('w_in'), tb=True, name="w_in_fwd", plan=p))
    mixer_w = {n: (full(n) if n in COL_SHARDED else w[n]) for n in MIXER_WEIGHTS}
    o, mixer_vjp = jax.vjp(lambda p, mw: _mixer(p, mw, rope_a, rope_b, b, s), proj, mixer_w)
    o = o.astype(BF16)
    mix = gathered.carry(sched.get(('w_out_fwd', l)), lambda p: _matmul(o, full('w_out'), name="w_out_fwd", plan=p))
    mix = mix.reshape(b, s, d)
    x_mid = _gated_add(x, gate1, mix)
    h2 = _normmod_fwd(x_mid, g2, scale2, shift2).reshape(m, d)
    gu = gathered.carry(sched.get(('w_gate_up_fwd', l)),
                        lambda p: _matmul(h2, full('w_gate_up'), name="w_gate_up_fwd", plan=p, chips='b'))
    act = gathered.carry(sched.get(('swiglu_fwd', l)), lambda p: _swiglu_fwd(gu, plan=p))
    ffn = gathered.carry(sched.get(('w_down_fwd', l)), lambda p: _matmul(act, full('w_down'), name="w_down_fwd", plan=p))
    ffn = ffn.reshape(b, s, d)
    x_out = _gated_add(x_mid, gate2, ffn)
    res = (x, x_mid, h1, h2, o, mix, gu, act, ffn, mixer_vjp, scale1, gate1, scale2, gate2, g1, g2)
    return x_out, res


def _layer_bwd(res, gathered, reducer, l, depth, dx_out):
    x, x_mid, h1, h2, o, mix, gu, act, ffn, mixer_vjp, scale1, gate1, scale2, gate2, g1, g2 = res
    b, s, d = x.shape
    m = b * s
    full = lambda n: gathered.full[(n, l)]
    above = l + 1 < depth
    dffn, dgate2 = _gated_bwd(dx_out, gate2, ffn)
    dffn = dffn.reshape(m, d)
    dact = reducer.carry([('w_out', l + 1), ('w_uq', l + 1), ('w_ukv', l + 1)] if above else [],
                         lambda p: _matmul(dffn, full('w_down'), tb=True, name="w_down_dx", plan=p))
    dw = reducer.carry([('w_in', l + 1)] if above else [],
                       lambda p: _matmul(act, dffn, ta=True, name="w_down_dw", plan=p))
    reducer.add([(('w_down', l), dw)])
    dgu = _swiglu_bwd(gu, dact)
    dh2 = reducer.carry([('w_down', l)], lambda p: _matmul(dgu, full('w_gate_up'), tb=True, name="w_gate_up_dx", plan=p,
                                                           chips='b'))
    dh2 = dh2.reshape(b, s, d)
    reducer.add([(('w_gate_up', l), _matmul(h2, dgu, ta=True, name="w_gate_up_dw", chips='out'))])
    dx_mid, dg2, dscale2, dshift2 = _normmod_bwd(x_mid, g2, scale2, dh2, dx_out)
    dmix, dgate1 = _gated_bwd(dx_mid, gate1, mix)
    dmix = dmix.reshape(m, d)
    gate_up = [('w_gate_up', l)]
    do = reducer.carry(gate_up, lambda p: _matmul(dmix, full('w_out'), tb=True, name="w_out_dx", plan=p), piece=0)
    dw_out = reducer.carry(gate_up, lambda p: _matmul(o, dmix, ta=True, name="w_out_dw", plan=p), piece=1)
    dproj, grads = mixer_vjp(do)
    grads = dict(grads)
    reducer.add([(('w_out', l), dw_out), (('w_uq', l), grads.pop('w_uq')), (('w_ukv', l), grads.pop('w_ukv'))])
    dproj = dproj.astype(BF16)
    bottom = [('w_out', l), ('w_uq', l), ('w_ukv', l)] if l == 0 else []
    dh1 = reducer.carry(gate_up, lambda p: _matmul(dproj, full('w_in'), name="w_in_dx", plan=p), piece=2, also=bottom)
    dh1 = dh1.reshape(b, s, d)
    dw = reducer.carry(gate_up, lambda p: _matmul(dproj, h1, ta=True, name="w_in_dw", plan=p), piece=3)
    reducer.add([(('w_in', l), dw)])
    dx, dg1, dscale1, dshift1 = _normmod_bwd(x, g1, scale1, dh1, dx_mid)
    grads["norm1_g"], grads["norm2_g"] = dg1[0], dg2[0]
    dmod = jnp.concatenate([dshift1, dscale1, dgate1, dshift2, dscale2, dgate2], axis=-1)[:, 0, :]
    return dx, dmod, grads


def _tail_loss(x2, final_norm_g, target2):
    return loss_head(rms_norm(x2, final_norm_g[None, :], 1), target2)


def _forward_backward(x, mod, small, gathered, reducer, final_norm_g, target):
    b, s, d = x.shape
    depth = len(small)
    rope_a = _axial_rope_tables(s, GQA_HEAD_DIM)
    rope_b = _axial_rope_tables(s, MLA_ROPE_DIM)
    sched = _gather_schedule(depth)
    first = sched['first']
    gathered.store(first, _run_plan(gathered.plan(first), "all_gather_chips"))
    saved = []
    for l in range(depth):
        x, res = _layer_fwd(x, mod[l], small[l], gathered, l, sched, rope_a, rope_b)
        saved.append(res)
    loss, (dx2, dfinal) = jax.value_and_grad(_tail_loss, argnums=(0, 1))(
        x.reshape(b * s, d), final_norm_g, target.reshape(b * s, d))
    dx = dx2.reshape(b, s, d)
    dmods, gsmall = [None] * depth, [None] * depth
    for l in reversed(range(depth)):
        dx, dmods[l], gsmall[l] = _layer_bwd(saved[l], gathered, reducer, l, depth, dx)
    return loss, dx, jnp.stack(dmods), gsmall, dfinal


ANY = pl.BlockSpec(memory_space=pl.ANY)


def _flip_if(v, bit):
    return 1 - v if bit else v


def _all_gather_devices(x):
    def body(x_ref, out_ref, send_sems, recv_sems):
        mx, my, mc = lax.axis_index("x"), lax.axis_index("y"), lax.axis_index("c")
        me = 4 * mx + 2 * my + mc
        sends = []
        for k in range(1, N_DEV):
            peer = (_flip_if(mx, k & 4), _flip_if(my, k & 2), _flip_if(mc, k & 1))
            cp = pltpu.make_async_remote_copy(src_ref=x_ref, dst_ref=out_ref.at[me], send_sem=send_sems.at[k - 1],
                                              recv_sem=recv_sems.at[k - 1], device_id=peer, device_id_type=MESH)
            cp.start()
            sends.append(cp)
        for k in range(1, N_DEV):
            peer = (_flip_if(mx, k & 4), _flip_if(my, k & 2), _flip_if(mc, k & 1))
            src = 4 * peer[0] + 2 * peer[1] + peer[2]
            pltpu.make_async_remote_copy(src_ref=x_ref, dst_ref=out_ref.at[src], send_sem=send_sems.at[k - 1],
                                         recv_sem=recv_sems.at[k - 1], device_id=peer, device_id_type=MESH).wait_recv()
        for cp in sends:
            cp.wait_send()

    out = pl.pallas_call(
        body, name="all_gather_devices", in_specs=[ANY], out_specs=ANY,
        out_shape=jax.ShapeDtypeStruct((N_DEV,) + x.shape, x.dtype),
        scratch_shapes=[pltpu.SemaphoreType.DMA((N_DEV - 1,)), pltpu.SemaphoreType.DMA((N_DEV - 1,))],
    )(x)
    me = 4 * lax.axis_index("x") + 2 * lax.axis_index("y") + lax.axis_index("c")
    return lax.dynamic_update_index_in_dim(out, x, me, 0)


class _Plan:
    def __init__(self, inputs, out_shapes, sem_counts, start, finish):
        self.inputs, self.out_shapes, self.sem_counts = list(inputs), list(out_shapes), list(sem_counts)
        self.start, self.finish = start, finish

    def specs(self):
        return ([ANY] * len(self.inputs), [ANY] * len(self.out_shapes),
                [pltpu.SemaphoreType.DMA((c,)) for c in self.sem_counts])

    def split(self, refs):
        a, b = len(self.inputs), len(self.inputs) + len(self.out_shapes)
        return refs[:a], refs[a:b], refs[b:]


def _run_plan(plan, name):
    def body(*refs):
        ins, outs, sems = plan.split(refs)
        plan.start(ins, outs, sems)
        plan.finish(ins, outs, sems)

    in_specs, out_specs, scratch = plan.specs()
    return pl.pallas_call(body, name=name, in_specs=in_specs, out_specs=out_specs, out_shape=plan.out_shapes,
                          scratch_shapes=scratch)(*plan.inputs)


def _gather_plan(shards):
    n = len(shards)
    halves = [t.reshape(2, t.shape[0] // 2, t.shape[1]) for t in shards]
    count = (N_CHIPS - 1) * n

    def copies(kind, ins, outs, sems):
        ici_send, ici_recv, d2d_send, d2d_recv, own_send, own_recv = sems
        mx, my, mc = lax.axis_index("x"), lax.axis_index("y"), lax.axis_index("c")
        me = 2 * mx + my
        sibling = (mx, my, 1 - mc)
        if kind == 'own':
            return [pltpu.make_async_remote_copy(src_ref=ins[i], dst_ref=outs[i].at[me], send_sem=own_send.at[i],
                                                 recv_sem=own_recv.at[i], device_id=sibling, device_id_type=MESH)
                    for i in range(n)]
        cps = []
        for k in range(1, N_CHIPS):
            peer = (_flip_if(mx, k & 2), _flip_if(my, k & 1), mc)
            src = 2 * peer[0] + peer[1]
            for i in range(n):
                j = (k - 1) * n + i
                if kind in ('ici', 'landed'):
                    dst = outs[i].at[me, mc] if kind == 'ici' else outs[i].at[src, mc]
                    cps.append(pltpu.make_async_remote_copy(
                        src_ref=ins[i].at[mc], dst_ref=dst, send_sem=ici_send.at[j], recv_sem=ici_recv.at[j],
                        device_id=peer, device_id_type=MESH))
                else:
                    half = outs[i].at[src, mc] if kind == 'fwd' else outs[i].at[src, 1 - mc]
                    cps.append(pltpu.make_async_remote_copy(
                        src_ref=half, dst_ref=half, send_sem=d2d_send.at[j], recv_sem=d2d_recv.at[j],
                        device_id=sibling, device_id_type=MESH))
        return cps

    def start(ins, outs, sems):
        for cp in copies('own', ins, outs, sems) + copies('ici', ins, outs, sems):
            cp.start()

    def finish(ins, outs, sems):
        fwd = copies('fwd', ins, outs, sems)
        for arrived, onward in zip(copies('landed', ins, outs, sems), fwd):
            arrived.wait_recv()
            onward.start()
        own = copies('own', ins, outs, sems)
        for cp in copies('fwd_in', ins, outs, sems) + own:
            cp.wait_recv()
        for cp in own + copies('ici', ins, outs, sems) + fwd:
            cp.wait_send()

    out_shapes = [jax.ShapeDtypeStruct((N_CHIPS,) + t.shape, t.dtype) for t in halves]
    return _Plan(halves, out_shapes, [count] * 4 + [n] * 2, start, finish)


def _sibling_exchange(blocks, name):
    n = len(blocks)

    def body(*refs):
        ins, outs = refs[:n], refs[n:2 * n]
        send_sems, recv_sems = refs[2 * n:]
        mx, my, mc = lax.axis_index("x"), lax.axis_index("y"), lax.axis_index("c")
        cps = []
        for i in range(n):
            cp = pltpu.make_async_remote_copy(src_ref=ins[i], dst_ref=outs[i], send_sem=send_sems.at[i],
                                              recv_sem=recv_sems.at[i], device_id=(mx, my, 1 - mc),
                                              device_id_type=MESH)
            cp.start()
            cps.append(cp)
        for cp in cps:
            cp.wait()

    return pl.pallas_call(
        body, name=name, in_specs=[ANY] * n, out_specs=[ANY] * n,
        out_shape=[jax.ShapeDtypeStruct(t.shape, t.dtype) for t in blocks],
        scratch_shapes=[pltpu.SemaphoreType.DMA((n,)), pltpu.SemaphoreType.DMA((n,))],
    )(*blocks)


def _add_halves(own, recv):
    nb, r, c = own.shape
    tr = _row_tile(r, c * 4)

    def body(g_ref, r_ref, o_ref):
        o_ref[...] = (g_ref[...] + r_ref[...].astype(F32)).astype(BF16)

    spec = pl.BlockSpec((1, tr, c), lambda k, i: (k, i, 0))
    return pl.pallas_call(
        body, name="rs_add_halves", grid=(nb, r // tr), in_specs=[spec, spec], out_specs=spec,
        out_shape=jax.ShapeDtypeStruct((nb, r, c), BF16),
        compiler_params=pltpu.CompilerParams(dimension_semantics=("parallel", "parallel")),
    )(own, recv)


def _chip_exchange_plan(jobs):
    n = len(jobs)
    count = (N_CHIPS - 1) * n

    def copies(ins, outs, sems):
        send_sems, recv_sems = sems
        mx, my, mc = lax.axis_index("x"), lax.axis_index("y"), lax.axis_index("c")
        cps = []
        for k in range(1, N_CHIPS):
            peer = (_flip_if(mx, k & 2), _flip_if(my, k & 1), mc)
            dst_chip = 2 * peer[0] + peer[1]
            for i, (_, row0, rows) in enumerate(jobs):
                j = (k - 1) * n + i
                cps.append(pltpu.make_async_remote_copy(
                    src_ref=ins[i].at[dst_chip, pl.ds(row0, rows)], dst_ref=outs[i].at[k - 1],
                    send_sem=send_sems.at[j], recv_sem=recv_sems.at[j], device_id=peer, device_id_type=MESH))
        return cps

    def start(ins, outs, sems):
        for cp in copies(ins, outs, sems):
            cp.start()

    def finish(ins, outs, sems):
        for cp in copies(ins, outs, sems):
            cp.wait()

    out_shapes = [jax.ShapeDtypeStruct((N_CHIPS - 1, rows, p.shape[2]), p.dtype) for p, _, rows in jobs]
    return _Plan([p for p, _, _ in jobs], out_shapes, [count, count], start, finish)


def _sum_chips(parts, recv, chip, row0):
    _, rows, c = recv.shape
    tr = _row_tile(rows, c * 4)
    assert row0 % tr == 0

    def body(chip_ref, p_ref, r_ref, o_ref):
        acc = p_ref[0].astype(F32)
        for k in range(N_CHIPS - 1):
            acc = acc + r_ref[k].astype(F32)
        o_ref[...] = acc

    return pl.pallas_call(
        body, name="rs_sum_chips",
        grid_spec=pltpu.PrefetchScalarGridSpec(
            num_scalar_prefetch=1, grid=(rows // tr,),
            in_specs=[pl.BlockSpec((1, tr, c), lambda i, chip_ref: (chip_ref[0], i + row0 // tr, 0)),
                      pl.BlockSpec((N_CHIPS - 1, tr, c), lambda i, chip_ref: (0, i, 0))],
            out_specs=pl.BlockSpec((tr, c), lambda i, chip_ref: (i, 0))),
        out_shape=jax.ShapeDtypeStruct((rows, c), F32),
        compiler_params=pltpu.CompilerParams(dimension_semantics=("parallel",)),
    )(chip, parts, recv)


def _sum_leading(t, name):
    nb, r, c = t.shape
    tr = _row_tile(r, c * 4 * nb)

    def body(t_ref, o_ref):
        acc = t_ref[0]
        for k in range(1, nb):
            acc = acc + t_ref[k]
        o_ref[...] = acc

    return pl.pallas_call(
        body, name=name, grid=(r // tr,),
        in_specs=[pl.BlockSpec((nb, tr, c), lambda i: (0, i, 0))],
        out_specs=pl.BlockSpec((tr, c), lambda i: (i, 0)),
        out_shape=jax.ShapeDtypeStruct((r, c), F32),
        compiler_params=pltpu.CompilerParams(dimension_semantics=("parallel",)),
    )(t)


def _rs_parts(grads):
    mc = lax.axis_index("c")
    split = [g.reshape(g.shape[0], 2, g.shape[1] // 2, g.shape[2]) for g in grads]
    own = [lax.dynamic_index_in_dim(g, mc, axis=1, keepdims=False) for g in split]
    away = [lax.dynamic_index_in_dim(g, 1 - mc, axis=1, keepdims=False).astype(BF16) for g in split]
    return [_add_halves(o, r) for o, r in zip(own, _sibling_exchange(away, "rs_sibling_exchange"))]


def _rs_result(items):
    mc = lax.axis_index("c")
    chip = (2 * lax.axis_index("x") + lax.axis_index("y")).astype(jnp.int32).reshape(1)
    mine = []
    for parts, pieces in items:
        done = [_sum_chips(parts, recv, chip, row0) for row0, recv in pieces]
        mine.append(done[0] if len(done) == 1 else jnp.concatenate(done, axis=0))
    theirs = _sibling_exchange(mine, "rs_sibling_swap")
    return [jnp.concatenate([jnp.where(mc == 0, a, b), jnp.where(mc == 0, b, a)], axis=0)
            for a, b in zip(mine, theirs)]


def _adamw(w, g, m, v, plan=None):
    shape = w.shape
    cols = shape[-1]
    if len(shape) == 3:
        lead, rows = shape[0], shape[1]
    else:
        lead, rows = 1, (int(np.prod(shape[:-1])) if len(shape) > 1 else 1)
    w2, g2, m2, v2 = [t.reshape(lead, rows, cols) for t in (w, g, m, v)]
    tr = _row_tile(rows, cols * 4, ADAM_TILE_BYTES)
    t0, tc = 1, cols
    if tr == rows and rows * cols * 4 > ADAM_TILE_BYTES and cols % LANES == 0:
        tc = max(t for t in _tile_cands(cols, cols) if t == LANES or rows * t * 4 <= ADAM_TILE_BYTES)
    elif rows < 8 and cols % LANES == 0:
        fits = [(a * t, t, a) for a in range(1, lead + 1) if lead % a == 0 for t in _tile_cands(cols, cols)
                if a * rows * t * 4 <= ADAM_TILE_BYTES]
        _, tc, t0 = max(fits) if fits else (0, LANES, 1)
    per = (rows // tr) * (cols // tc)
    steps = (lead // t0) * per
    p_in, p_out, p_shapes, p_scratch, p_args = _plan_specs(plan)

    def body(w_ref, g_ref, m_ref, v_ref, *rest):
        d_ref, mo_ref, vo_ref = rest[len(p_in):len(p_in) + 3]
        i = pl.program_id(0) * per + pl.program_id(1)

        def compute():
            gv = g_ref[...]
            mn = ADAM_B1 * m_ref[...] + (1.0 - ADAM_B1) * gv
            vn = ADAM_B2 * v_ref[...] + (1.0 - ADAM_B2) * (gv * gv)
            m_hat = mn / (1.0 - ADAM_B1 ** ADAM_STEP)
            v_hat = vn / (1.0 - ADAM_B2 ** ADAM_STEP)
            d_ref[...] = -ADAM_LR * (m_hat / (jnp.sqrt(v_hat) + ADAM_EPS) + ADAM_WD * w_ref[...])
            mo_ref[...] = mn
            vo_ref[...] = vn

        _ride(plan, rest[:len(p_in)] + rest[len(p_in) + 3:], i == 0, i == steps - 1, compute)

    col_blocks = cols // tc
    spec = pl.BlockSpec((t0, tr, tc), lambda a, i: (a, i // col_blocks, i % col_blocks))
    sem = "arbitrary" if plan is not None else "parallel"
    outs = pl.pallas_call(
        body, name="adamw", grid=(lead // t0, per), in_specs=[spec] * 4 + p_in, out_specs=[spec] * 3 + p_out,
        out_shape=[jax.ShapeDtypeStruct((lead, rows, cols), F32)] * 3 + p_shapes, scratch_shapes=p_scratch,
        compiler_params=pltpu.CompilerParams(dimension_semantics=(sem, sem), vmem_limit_bytes=ADAM_VMEM_LIMIT),
    )(w2, g2, m2, v2, *p_args)
    res = [t.reshape(shape) for t in outs[:3]]
    return res if plan is None else (res, list(outs[3:]))


WEIGHTS = ['w_ada', 'b_ada', 'norm1_g', 'norm2_g', 'w_in', 'q_norm_g', 'k_norm_g', 'mla_q_norm_g', 'w_uq',
           'mla_kv_norm_g', 'w_ukv', 'conv_w', 'conv_b', 'dt_bias', 'a_log', 'd_skip', 'ssd_norm_g', 'w_out',
           'w_gate_up', 'w_down', 'final_norm_g']
COL_SHARDED = ('w_in', 'w_uq', 'w_ukv', 'w_gate_up')
ROW_SHARDED = ('w_out', 'w_down')
CHIP_BLOCKED = ('w_gate_up',)
TRANSPOSED = ('w_in',)
SMALL_LAYER = ('norm1_g', 'norm2_g', 'q_norm_g', 'k_norm_g', 'mla_q_norm_g', 'mla_kv_norm_g', 'conv_w', 'conv_b',
               'dt_bias', 'a_log', 'd_skip', 'ssd_norm_g')


def _pack(parts):
    flat = jnp.concatenate([p.reshape(-1) for p in parts])
    n = flat.shape[0]
    rows = -(-n // (8 * LANES)) * 8
    return jnp.pad(flat, (0, rows * LANES - n)).reshape(rows, LANES)


def _unpack(flat, shapes):
    out, pos = [], 0
    for shp in shapes:
        size = int(np.prod(shp))
        out.append(flat[pos:pos + size].reshape(shp))
        pos += size
    return out


def _cols_full(gathered):
    k, r, c = gathered.shape
    return jnp.transpose(gathered, (1, 0, 2)).reshape(r, k * c)


def _cols_split(full):
    r, c4 = full.shape
    return jnp.transpose(full.reshape(r, N_CHIPS, c4 // N_CHIPS), (1, 0, 2))


def kernel(x, c, w_ada, b_ada, norm1_g, norm2_g, w_in, q_norm_g, k_norm_g, mla_q_norm_g, w_uq, mla_kv_norm_g, w_ukv, conv_w, conv_b, dt_bias, a_log, d_skip, ssd_norm_g, w_out, w_gate_up, w_down, final_norm_g, loss_target, m_w_ada, m_b_ada, m_norm1_g, m_norm2_g, m_w_in, m_q_norm_g, m_k_norm_g, m_mla_q_norm_g, m_w_uq, m_mla_kv_norm_g, m_w_ukv, m_conv_w, m_conv_b, m_dt_bias, m_a_log, m_d_skip, m_ssd_norm_g, m_w_out, m_w_gate_up, m_w_down, m_final_norm_g, v_w_ada, v_b_ada, v_norm1_g, v_norm2_g, v_w_in, v_q_norm_g, v_k_norm_g, v_mla_q_norm_g, v_w_uq, v_mla_kv_norm_g, v_w_ukv, v_conv_w, v_conv_b, v_dt_bias, v_a_log, v_d_skip, v_ssd_norm_g, v_w_out, v_w_gate_up, v_w_down, v_final_norm_g):
    args = dict(locals())
    weights = {n: args[n] for n in WEIGHTS}
    depth = w_in.shape[0]
    bl, s, d = x.shape
    mx, my, mc = lax.axis_index("x"), lax.axis_index("y"), lax.axis_index("c")
    chip = 2 * mx + my
    dev = 2 * chip + mc
    ada_cols = w_ada.shape[-1]
    conv_cols = conv_w.shape[-1]

    first_shapes = [c.shape, conv_w.shape]
    first = _all_gather_devices(_pack([c, conv_w]))
    first = [_unpack(first[i].reshape(-1), first_shapes) for i in range(N_DEV)]
    c_act = jax.nn.silu(jnp.concatenate([f[0] for f in first], axis=0))
    conv_w_full = jnp.concatenate([first[2 * k][1] for k in range(N_CHIPS)], axis=-1)

    b_cols = lax.dynamic_slice_in_dim(b_ada, chip * ada_cols, ada_cols, axis=1)
    c_act_b = c_act.astype(BF16)
    mod_cols = jnp.stack([_matmul(c_act_b, w_ada[l], name="ada_fwd") + b_cols[l][None, :]
                          for l in range(depth)])
    mod_all = _all_gather_devices(mod_cols.reshape(depth * N_DEV * bl, ada_cols))
    mod_all = mod_all.reshape(N_DEV, depth, N_DEV, bl, ada_cols)
    mod_mine = lax.dynamic_index_in_dim(mod_all, dev, axis=2, keepdims=False)
    mod = jnp.concatenate([mod_mine[2 * k] for k in range(N_CHIPS)], axis=-1)

    big = COL_SHARDED + ROW_SHARDED
    flip = lambda t: jnp.transpose(t, (2, 0, 1))
    unflip = lambda t: jnp.transpose(t, (1, 2, 0))
    shards = {(n, l): weights[n][l].astype(BF16) for n in big for l in range(depth)}
    for l in range(depth):
        shards[('w_in', l)] = jnp.pad(flip(w_in)[:, l, :].astype(BF16), ((0, IN_SHARD_PAD - IN_SHARD), (0, 0)))
    gathered = _Gathered(shards)
    reducer = _Reducer()
    small_w = []
    for l in range(depth):
        w = {n: weights[n][l] for n in SMALL_LAYER if n != 'conv_w'}
        w['conv_w'] = conv_w_full[l]
        small_w.append(w)
    loss_local, gx, gmod, glayers, gfinal = _forward_backward(x, mod, small_w, gathered, reducer, final_norm_g,
                                                              loss_target)

    small_parts = [jnp.stack([glayers[l][n] for l in range(depth)]) for n in SMALL_LAYER]
    small_parts += [gfinal, loss_local.reshape(1), gmod]
    small_shapes = [p.shape for p in small_parts]
    last = _all_gather_devices(_pack(small_parts))
    summed = _unpack(_sum_leading(last, "sum_devices").reshape(-1), small_shapes)
    small = dict(zip(SMALL_LAYER, summed[:len(SMALL_LAYER)]))
    g_final, loss, gmod_sum = summed[len(SMALL_LAYER):]
    small['conv_w'] = lax.dynamic_slice_in_dim(small['conv_w'], chip * conv_cols, conv_cols, axis=2)
    gmod_all = jnp.stack([_unpack(last[i].reshape(-1), small_shapes)[-1] for i in range(N_DEV)], axis=1)
    gmod_all = gmod_all.reshape(depth, N_DEV * bl, gmod.shape[-1])
    gmod_cols = lax.dynamic_slice_in_dim(gmod_all, chip * ada_cols, ada_cols, axis=2)
    g_w_ada = jnp.stack([_matmul(c_act_b, gmod_cols[l].astype(BF16), ta=True, name="ada_dw") for l in range(depth)])
    g_b_ada = gmod_sum[:, 0]
    for i in range(1, bl):
        g_b_ada = g_b_ada + gmod_sum[:, i]

    grad = {'w_ada': g_w_ada, 'b_ada': g_b_ada, 'final_norm_g': g_final}
    for n in SMALL_LAYER:
        grad[n] = small[n]

    delta, new_m, new_v = {}, {}, {}
    left = [key for key in reducer.parts if key not in reducer.result and not reducer.recv[key]]
    grad['w_down'] = jnp.stack([reducer.result[('w_down', l)] for l in range(depth)])
    delta['w_down'], new_m['w_down'], new_v['w_down'] = reducer.carry(
        left, lambda p: _adamw(w_down, grad['w_down'], m_w_down, v_w_down, plan=p))
    reducer.flush()
    for n in big:
        if n in TRANSPOSED:
            grad[n] = jnp.stack([reducer.result[(n, l)][:weights[n].shape[2]] for l in range(depth)], axis=1)
        elif n != 'w_down':
            grad[n] = jnp.stack([reducer.result[(n, l)] for l in range(depth)])
    for n in WEIGHTS:
        if n in TRANSPOSED:
            outs = _adamw(flip(weights[n]), grad[n], flip(args["m_" + n]), flip(args["v_" + n]))
            grad[n], delta[n], new_m[n], new_v[n] = [unflip(t) for t in [grad[n]] + outs]
        elif n != 'w_down':
            delta[n], new_m[n], new_v[n] = _adamw(weights[n], grad[n], args["m_" + n], args["v_" + n])
    return (loss.reshape(()), gx, *[grad[n] for n in WEIGHTS], *[delta[n] for n in WEIGHTS],
            *[new_m[n] for n in WEIGHTS], *[new_v[n] for n in WEIGHTS])
```
